```python
import math
import jax, jax.numpy as jnp
from jax import lax
import numpy as np

D_MODEL = 1024
BATCH = 16
SEQ = 256
DEPTH = 2
DEC_BATCH = 4
DEC_SEQ = 4096
PAST_LEN = 256

GRID_W = 64
D_RNN = 512
LRU_BLOCKS = 8
LRU_BLOCK = D_RNN // LRU_BLOCKS
CONV_W = 4
CONV_LEFT = 2
LRU_C = 8.0
DA_HEADS = 4
DA_QK = 64
DA_V = 2 * DA_QK
ROPE_PAIRS = DA_QK // 4
ROPE_BASE = 10000.0
Q_BLOCK = 128
RET_HEADS = 4
RET_QK = 64
RET_V = 128
RET_CHUNK = 128
BRANCH_W = 512
N_BRANCH = 3
D_IN = 2 * D_RNN + DA_HEADS * (4 * DA_QK + DA_V) + RET_HEADS * (2 * RET_QK + 2 * RET_V) + N_BRANCH * D_MODEL
N_EXPERTS = 64
TOP_K = 8
N_GROUPS = 8
TOPK_GROUPS = 4
D_EXPERT = 256
D_SHARED = 256
ROUTED_SCALE = 2.5
EXPERT_BLOCK = 4
EPS = 1e-6
F32 = jnp.float32

kernel_name = 'hybrid_diffusion_lru_diffattn_retention_moe_step'


def _in_splits():
    sizes = [D_RNN, D_RNN, DA_HEADS * 2 * DA_QK, DA_HEADS * 2 * DA_QK, DA_HEADS * DA_V,
             RET_HEADS * RET_QK, RET_HEADS * RET_QK, RET_HEADS * RET_V, RET_HEADS * RET_V]
    return [int(s) for s in np.cumsum(sizes)]


def rms_norm(x, w=None):
    xf = x.astype(F32)
    y = xf * lax.rsqrt(jnp.mean(xf * xf, axis=-1, keepdims=True) + EPS)
    if w is not None:
        y = y * w.astype(F32)
    return y.astype(x.dtype)


def centred_depthwise_conv(x, w, b):
    rhs = w[:, None, :].astype(x.dtype)
    y = lax.conv_general_dilated(x, rhs, window_strides=(1,),
                                 padding=[(CONV_LEFT, CONV_W - 1 - CONV_LEFT)],
                                 dimension_numbers=('NWC', 'WIO', 'NWC'),
                                 feature_group_count=x.shape[-1])
    return y + b.astype(x.dtype)


def linear_scan(a, u, h0):
    def combine(left, right):
        a_l, u_l = left
        a_r, u_r = right
        return a_l * a_r, a_r * u_l + u_r
    a_cum, u_cum = lax.associative_scan(combine, (a, u), axis=1)
    return u_cum + a_cum * h0[:, None, :]


def rglru_direction(xc, wg, bg, lam, h0, reverse):
    B, T, W = xc.shape
    xb = xc.reshape(B, T, LRU_BLOCKS, LRU_BLOCK)
    g = jnp.einsum('btni,gnij->gbtnj', xb, wg.astype(xc.dtype)).reshape(2, B, T, W).astype(F32)
    g = g + bg.astype(F32)[:, None, None, :]
    r = jax.nn.sigmoid(g[0])
    i = jax.nn.sigmoid(g[1])
    log_a = -LRU_C * r * jax.nn.softplus(-lam.astype(F32))
    a = jnp.exp(log_a)
    u = jnp.sqrt(-jnp.expm1(2.0 * log_a)) * i * xc.astype(F32)
    if reverse:
        a = jnp.flip(a, 1)
        u = jnp.flip(u, 1)
    h = linear_scan(a, u, h0.astype(F32))
    h_last = h[:, -1]
    if reverse:
        h = jnp.flip(h, 1)
    return h, h_last


def _rotate(x, ang):
    cos = jnp.cos(ang)[None, :, None, None, :]
    sin = jnp.sin(ang)[None, :, None, None, :]
    x1, x2 = x[..., :ROPE_PAIRS], x[..., ROPE_PAIRS:]
    return jnp.concatenate([x1 * cos - x2 * sin, x2 * cos + x1 * sin], axis=-1)


def rope_2d(x):
    T = x.shape[1]
    rows = T // GRID_W
    row = jnp.repeat(jnp.arange(rows, dtype=F32), GRID_W)
    col = jnp.tile(jnp.arange(GRID_W, dtype=F32), rows)
    inv = ROPE_BASE ** (-jnp.arange(ROPE_PAIRS, dtype=F32) / ROPE_PAIRS)
    xf = x.astype(F32)
    half = DA_QK // 2
    out = jnp.concatenate([_rotate(xf[..., :half], row[:, None] * inv[None, :]),
                           _rotate(xf[..., half:], col[:, None] * inv[None, :])], axis=-1)
    return out.astype(x.dtype)


def diff_attention(q, k, v, lam):
    B, Tq, H, _, dq = q.shape
    dv = v.shape[-1]
    nb = Tq // Q_BLOCK
    qb = jnp.moveaxis(q.reshape(B, nb, Q_BLOCK, H, 2, dq), 1, 0)
    scale = dq ** -0.5

    def block(qi):
        s = jnp.einsum('bqhcd,bkhcd->bhcqk', qi, k).astype(F32) * scale
        p = jax.nn.softmax(s, axis=-1)
        wgt = p[:, :, 0] - lam * p[:, :, 1]
        return jnp.einsum('bhqk,bkhe->bqhe', wgt.astype(v.dtype), v)

    o = lax.map(block, qb)
    return jnp.moveaxis(o, 0, 1).reshape(B, Tq, H, dv)


def retention_direction(q, k, v, log_g, s0):
    B, T, H, dk = q.shape
    dv = v.shape[-1]
    C = RET_CHUNK
    N = T // C
    qc = q.astype(F32).reshape(B, N, C, H, dk)
    kc = k.astype(F32).reshape(B, N, C, H, dk)
    vc = v.astype(F32).reshape(B, N, C, H, dv)
    pos = jnp.arange(C, dtype=F32)
    diff = pos[:, None] - pos[None, :]
    decay = jnp.where(diff >= 0, jnp.exp(jnp.maximum(diff, 0.0)[None] * log_g[:, None, None]), 0.0)
    scores = jnp.einsum('bnihd,bnjhd->bnhij', qc, kc) * decay[None, None]
    inner = jnp.einsum('bnhij,bnjhe->bnihe', scores, vc)
    k_decay = jnp.exp((C - 1.0 - pos)[:, None] * log_g[None, :])
    U = jnp.einsum('bnjhd,jh,bnjhe->bnhde', kc, k_decay, vc)
    chunk_decay = jnp.exp(C * log_g)[None, :, None, None]

    def step(S, U_n):
        return chunk_decay * S + U_n, S

    s_fin, s_start = lax.scan(step, s0.astype(F32), jnp.moveaxis(U, 1, 0))
    q_decay = jnp.exp((pos + 1.0)[:, None] * log_g[None, :])
    cross = jnp.einsum('bnihd,nbhde->bnihe', qc * q_decay[None, None, :, :, None], s_start)
    return (inner + cross).reshape(B, T, H, dv), s_fin


def moe(h, lp):
    B, T, D = h.shape
    hf = h.reshape(B * T, D)
    N = hf.shape[0]
    scores = jax.nn.sigmoid((hf @ lp['router_w']).astype(F32))
    biased = scores + lp['router_bias'].astype(F32)
    per_group = N_EXPERTS // N_GROUPS
    grp_score = lax.top_k(biased.reshape(N, N_GROUPS, per_group), 2)[0].sum(-1)
    _, gidx = lax.top_k(grp_score, TOPK_GROUPS)
    gmask = jax.nn.one_hot(gidx, N_GROUPS, dtype=F32).sum(1)
    emask = jnp.repeat(gmask, per_group, axis=1) > 0
    _, eidx = lax.top_k(jnp.where(emask, biased, -jnp.inf), TOP_K)
    w = jnp.take_along_axis(scores, eidx, axis=1)
    w = w / jnp.sum(w, axis=-1, keepdims=True) * ROUTED_SCALE
    gates = jnp.sum(jax.nn.one_hot(eidx, N_EXPERTS, dtype=F32) * w[..., None], axis=1)
    nblk = N_EXPERTS // EXPERT_BLOCK
    gu_b = lp['w_exp_gu'].reshape(nblk, EXPERT_BLOCK, D, 2 * D_EXPERT)
    dn_b = lp['w_exp_down'].reshape(nblk, EXPERT_BLOCK, D_EXPERT, D)
    g_b = jnp.moveaxis(gates.reshape(N, nblk, EXPERT_BLOCK), 1, 0)

    def body(acc, xs):
        gu, dn, g = xs
        a = jnp.einsum('nd,edf->nef', hf, gu)
        hg, hu = jnp.split(a, 2, axis=-1)
        act = (jax.nn.silu(hg) * hu).astype(F32) * g[..., None]
        return acc + jnp.einsum('nef,efd->nd', act.astype(hf.dtype), dn).astype(F32), None

    routed, _ = lax.scan(body, jnp.zeros((N, D), F32), (gu_b, dn_b, g_b))
    sg, su = jnp.split(hf @ lp['w_sh_gu'], 2, axis=-1)
    shared = (jax.nn.silu(sg) * su) @ lp['w_sh_down']
    return (routed.astype(h.dtype) + shared).reshape(B, T, D)


def mixer(h, lp, lam_init, latent, k_prev, v_prev, lru_h0, ret_s0):
    B, T, _ = h.shape
    proj = h @ lp['w_in']
    xa, ga, dq, dk, dv, rq, rk, rv, rg, gl = jnp.split(proj, _in_splits(), axis=-1)
    xc = centred_depthwise_conv(xa, lp['conv_w'], lp['conv_b'])
    hf, lf = rglru_direction(xc, lp['lru_gate_w'][0], lp['lru_gate_b'][0], lp['lru_lambda'][0], lru_h0[:, 0], False)
    hb, lb = rglru_direction(xc, lp['lru_gate_w'][1], lp['lru_gate_b'][1], lp['lru_lambda'][1], lru_h0[:, 1], True)
    branch_a = jax.nn.gelu(ga) * (hf + hb).astype(h.dtype)
    q = rms_norm(dq.reshape(B, T, DA_HEADS, 2, DA_QK), lp['q_norm_w'])
    k = rms_norm(dk.reshape(B, T, DA_HEADS, 2, DA_QK), lp['k_norm_w'])
    v = dv.reshape(B, T, DA_HEADS, DA_V)
    if latent:
        q = rope_2d(q)
        k = rope_2d(k)
        k_all = jnp.concatenate([k_prev.astype(k.dtype), k], axis=1)
        v_all = jnp.concatenate([v_prev.astype(v.dtype), v], axis=1)
    else:
        k_all, v_all = k, v
    lam_p = lp['diff_lambda'].astype(F32)
    lam = jnp.exp(jnp.sum(lam_p[0] * lam_p[1])) - jnp.exp(jnp.sum(lam_p[2] * lam_p[3])) + lam_init
    att = diff_attention(q, k_all, v_all, lam)
    branch_b = (rms_norm(att, lp['subln_w']) * (1.0 - lam_init)).reshape(B, T, DA_HEADS * DA_V)
    q_r = rq.reshape(B, T, RET_HEADS, RET_QK)
    k_r = rk.reshape(B, T, RET_HEADS, RET_QK) * (RET_QK ** -0.5)
    v_r = rv.reshape(B, T, RET_HEADS, RET_V)
    log_g = jax.nn.log_sigmoid(lp['ret_decay'].astype(F32))
    of, sf = retention_direction(q_r, k_r, v_r, log_g[0], ret_s0[:, 0])
    ob, sb = retention_direction(jnp.flip(q_r, 1), jnp.flip(k_r, 1), jnp.flip(v_r, 1), log_g[1], ret_s0[:, 1])
    y_r = rms_norm(of + jnp.flip(ob, 1)).astype(h.dtype)
    branch_c = y_r.reshape(B, T, RET_HEADS * RET_V) * jax.nn.silu(rg)
    branches = jnp.stack([branch_a, branch_b, branch_c], axis=2)
    proj_b = jnp.einsum('btge,ged->btgd', branches, lp['w_branch'])
    gates = jax.nn.sigmoid(gl.reshape(B, T, N_BRANCH, D_MODEL))
    out = jnp.sum(gates * proj_b, axis=2) @ lp['w_out']
    ctx = None if latent else (k, v, jnp.stack([lf, lb], axis=1), jnp.stack([sf, sb], axis=1))
    return out, ctx


def trunk_layer(x, mod, lp, lam_init, latent, k_prev, v_prev, lru_h0, ret_s0):
    sh1, sc1, g1, sh2, sc2, g2 = jnp.split(mod, 6, axis=-1)
    h = rms_norm(x, lp['norm1_w']) * (1.0 + sc1) + sh1
    m, ctx = mixer(h, lp, lam_init, latent, k_prev, v_prev, lru_h0, ret_s0)
    x = x + g1 * m
    h = rms_norm(x, lp['norm2_w']) * (1.0 + sc2) + sh2
    x = x + g2 * moe(h, lp)
    return x, ctx


def setup_inputs(seed: int = 0) -> dict:
    key = jax.random.key(seed)
    ks = jax.random.split(key, 32)

    def nrm(k, shape, s):
        return jax.random.normal(k, shape, F32) * s

    gamma = 1.0 - 2.0 ** (-5.0 - jnp.arange(RET_HEADS, dtype=F32))
    ret_logit = jnp.log(gamma) - jnp.log1p(-gamma)
    u = jax.random.uniform(ks[13], (DEPTH, 2, D_RNN), F32, 0.9, 0.999)
    s = u ** (1.0 / LRU_C)
    return {
        'x_prompt': nrm(ks[0], (BATCH, SEQ, D_MODEL), 1.0),
        'x_sample': nrm(ks[1], (DEC_BATCH, DEC_SEQ, D_MODEL), 1.0),
        'cache_k': nrm(ks[2], (DEC_BATCH, DEPTH, PAST_LEN, DA_HEADS, 2, DA_QK), 1.0),
        'cache_v': nrm(ks[3], (DEC_BATCH, DEPTH, PAST_LEN, DA_HEADS, DA_V), 1.0),
        'state_lru': nrm(ks[4], (DEC_BATCH, DEPTH, 2, D_RNN), 1.0),
        'state_ret': nrm(ks[5], (DEC_BATCH, DEPTH, 2, RET_HEADS, RET_QK, RET_V), 1.0),
        'c': nrm(ks[6], (DEC_BATCH, D_MODEL), 1.0),
        'c_ctx': nrm(ks[7], (D_MODEL,), 1.0),
        'ada_w': nrm(ks[8], (DEPTH, D_MODEL, 6 * D_MODEL), 0.5 * D_MODEL ** -0.5),
        'ada_b': nrm(ks[9], (DEPTH, 6 * D_MODEL), 0.02),
        'norm1_w': 1.0 + nrm(ks[10], (DEPTH, D_MODEL), 0.02),
        'norm2_w': 1.0 + nrm(ks[11], (DEPTH, D_MODEL), 0.02),
        'w_in': nrm(ks[12], (DEPTH, D_MODEL, D_IN), D_MODEL ** -0.5),
        'conv_w': nrm(ks[14], (DEPTH, CONV_W, D_RNN), CONV_W ** -0.5),
        'conv_b': nrm(ks[15], (DEPTH, D_RNN), 0.01),
        'lru_gate_w': nrm(ks[16], (DEPTH, 2, 2, LRU_BLOCKS, LRU_BLOCK, LRU_BLOCK), LRU_BLOCK ** -0.5),
        'lru_gate_b': nrm(ks[17], (DEPTH, 2, 2, D_RNN), 0.01),
        'lru_lambda': jnp.log(s) - jnp.log1p(-s),
        'q_norm_w': 1.0 + nrm(ks[18], (DEPTH, DA_QK), 0.02),
        'k_norm_w': 1.0 + nrm(ks[19], (DEPTH, DA_QK), 0.02),
        'diff_lambda': nrm(ks[20], (DEPTH, 4, DA_QK), 0.1),
        'subln_w': 1.0 + nrm(ks[21], (DEPTH, DA_V), 0.02),
        'ret_decay': ret_logit[None, None, :] + nrm(ks[22], (DEPTH, 2, RET_HEADS), 0.05),
        'w_branch': nrm(ks[23], (DEPTH, N_BRANCH, BRANCH_W, D_MODEL), BRANCH_W ** -0.5),
        'w_out': nrm(ks[24], (DEPTH, D_MODEL, D_MODEL), D_MODEL ** -0.5),
        'router_w': nrm(ks[25], (DEPTH, D_MODEL, N_EXPERTS), D_MODEL ** -0.5),
        'router_bias': nrm(ks[26], (DEPTH, N_EXPERTS), 0.01),
        'w_exp_gu': nrm(ks[27], (DEPTH, N_EXPERTS, D_MODEL, 2 * D_EXPERT), D_MODEL ** -0.5),
        'w_exp_down': nrm(ks[28], (DEPTH, N_EXPERTS, D_EXPERT, D_MODEL), D_EXPERT ** -0.5),
        'w_sh_gu': nrm(ks[29], (DEPTH, D_MODEL, 2 * D_SHARED), D_MODEL ** -0.5),
        'w_sh_down': nrm(ks[30], (DEPTH, D_SHARED, D_MODEL), D_SHARED ** -0.5),
    }


def reference(x_prompt, x_sample, cache_k, cache_v, state_lru, state_ret, c, c_ctx,
              ada_w, ada_b, norm1_w, norm2_w, w_in, conv_w, conv_b, lru_gate_w, lru_gate_b,
              lru_lambda, q_norm_w, k_norm_w, diff_lambda, subln_w, ret_decay, w_branch, w_out,
              router_w, router_bias, w_exp_gu, w_exp_down, w_sh_gu, w_sh_down):
    weights = {
        'ada_w': ada_w, 'ada_b': ada_b, 'norm1_w': norm1_w, 'norm2_w': norm2_w, 'w_in': w_in,
        'conv_w': conv_w, 'conv_b': conv_b, 'lru_gate_w': lru_gate_w, 'lru_gate_b': lru_gate_b,
        'lru_lambda': lru_lambda, 'q_norm_w': q_norm_w, 'k_norm_w': k_norm_w,
        'diff_lambda': diff_lambda, 'subln_w': subln_w, 'ret_decay': ret_decay,
        'w_branch': w_branch, 'w_out': w_out, 'router_w': router_w, 'router_bias': router_bias,
        'w_exp_gu': w_exp_gu, 'w_exp_down': w_exp_down, 'w_sh_gu': w_sh_gu, 'w_sh_down': w_sh_down,
    }
    x = x_prompt
    bp = x.shape[0]
    lru0 = jnp.zeros((bp, 2, D_RNN), F32)
    ret0 = jnp.zeros((bp, 2, RET_HEADS, RET_QK, RET_V), F32)
    ks, vs, lrus, rets = [], [], [], []
    for l in range(DEPTH):
        lp = {name: arr[l] for name, arr in weights.items()}
        lam_init = 0.8 - 0.6 * math.exp(-0.3 * l)
        mod = (jax.nn.silu(c_ctx) @ lp['ada_w'] + lp['ada_b'])[None, None, :]
        x, ctx = trunk_layer(x, mod, lp, lam_init, False, None, None, lru0, ret0)
        ks.append(ctx[0])
        vs.append(ctx[1])
        lrus.append(ctx[2])
        rets.append(ctx[3])
    y_prompt = x
    new_cache_k = jnp.stack(ks, axis=1)
    new_cache_v = jnp.stack(vs, axis=1)
    new_state_lru = jnp.stack(lrus, axis=1)
    new_state_ret = jnp.stack(rets, axis=1)
    x = x_sample
    for l in range(DEPTH):
        lp = {name: arr[l] for name, arr in weights.items()}
        lam_init = 0.8 - 0.6 * math.exp(-0.3 * l)
        mod = (jax.nn.silu(c) @ lp['ada_w'] + lp['ada_b'])[:, None, :]
        x, _ = trunk_layer(x, mod, lp, lam_init, True, cache_k[:, l], cache_v[:, l],
                           state_lru[:, l], state_ret[:, l])
    y_sample = x
    return (y_prompt, y_sample, new_cache_k, new_cache_v, new_state_lru, new_state_ret)
```

```python
import functools
import math

import numpy as np
import jax
import jax.numpy as jnp
from jax import lax
from jax.experimental import pallas as pl
from jax.experimental.pallas import tpu as pltpu

F32 = jnp.float32
BF16 = jnp.bfloat16

D_MODEL = 1024
DEPTH = 2
GRID_W = 64
D_RNN = 512
LRU_BLOCKS = 8
LRU_BLOCK = D_RNN // LRU_BLOCKS
CONV_W = 4
LRU_C = 8.0
DA_HEADS = 4
DA_QK = 64
DA_V = 128
ROPE_PAIRS = DA_QK // 4
ROPE_BASE = 10000.0
RET_HEADS = 4
RET_QK = 64
RET_V = 128
BRANCH_W = 512
N_BRANCH = 3
D_IN = 7168
N_EXPERTS = 64
TOP_K = 8
N_GROUPS = 8
TOPK_GROUPS = 4
D_EXPERT = 256
ROUTED_SCALE = 2.5
EPS = 1e-6

C_XA, C_GA, C_DQ, C_DK, C_DV = 0, 512, 1024, 1536, 2048
C_RQ, C_RK, C_RV, C_RG, C_GL = 2560, 2816, 3072, 3584, 4096

BLK = 256
MOD_ROWS = 8
VMEM_LIMIT = 56 * 1024 * 1024


def _cparams(sem):
    return pltpu.CompilerParams(dimension_semantics=sem, vmem_limit_bytes=VMEM_LIMIT)


class _Geom:
    def __init__(self, batch, seq, dec_batch, dec_seq):
        assert seq == BLK and dec_seq % BLK == 0
        self.batch, self.seq, self.dec_batch, self.dec_seq = batch, seq, dec_batch, dec_seq
        self.n_ctx = batch * seq
        self.n_lat = dec_batch * dec_seq
        self.n_tok = self.n_ctx + self.n_lat
        self.ctx_blocks = self.n_ctx // BLK
        self.lat_blocks = dec_seq // BLK
        self.n_blocks = self.n_tok // BLK
        self.n_seq = batch + dec_batch

    def mod_row(self, i, tile):
        nct = self.n_ctx // tile
        per = self.dec_seq // tile
        return jnp.where(i < nct, 0, 1 + (i - nct) // per)

    def seq_id(self, i):
        return jnp.where(i < self.ctx_blocks, i, self.ctx_blocks + (i - self.ctx_blocks) // self.lat_blocks)

    def seq_start(self, i):
        return jnp.logical_or(i < self.ctx_blocks, (i - self.ctx_blocks) % self.lat_blocks == 0)

    def seq_end(self, i):
        return jnp.logical_or(i < self.ctx_blocks, (i - self.ctx_blocks) % self.lat_blocks == self.lat_blocks - 1)


def _ada_kernel(c_ref, w_ref, b_ref, o_ref):
    cv = c_ref[...]
    s = cv * jax.nn.sigmoid(cv)
    o_ref[...] = jnp.dot(s, w_ref[...], preferred_element_type=F32,
                         precision=lax.Precision.HIGHEST) + b_ref[...]


def _ada_call(cvec, ada_w, ada_b):
    depth = ada_w.shape[0]
    nt = 6
    return pl.pallas_call(
        _ada_kernel,
        grid=(depth, nt),
        in_specs=[pl.BlockSpec((MOD_ROWS, D_MODEL), lambda l, j: (0, 0)),
                  pl.BlockSpec((None, D_MODEL, D_MODEL), lambda l, j: (l, 0, j)),
                  pl.BlockSpec((None, 1, D_MODEL), lambda l, j: (l, 0, j))],
        out_specs=pl.BlockSpec((None, MOD_ROWS, D_MODEL), lambda l, j: (l, 0, j)),
        out_shape=jax.ShapeDtypeStruct((depth, MOD_ROWS, 6 * D_MODEL), F32),
        compiler_params=_cparams(("arbitrary", "arbitrary")),
        name="ada_mod",
    )(cvec, ada_w, ada_b.reshape(depth, 1, 6 * D_MODEL))


def _mod_spec(geom, l, which, tile, ngrid):
    if ngrid == 1:
        return pl.BlockSpec((None, None, None, 1, D_MODEL),
                            lambda i: (l, geom.mod_row(i, tile), which, 0, 0))
    return pl.BlockSpec((None, None, None, 1, D_MODEL),
                        lambda i, j: (l, geom.mod_row(i, tile), which, 0, 0))


def _inproj_kernel(x_ref, sc_ref, sh_ref, nw_ref, w_ref, wkt_ref, o_ref, kt_ref, h_scr):
    @pl.when(pl.program_id(1) == 0)
    def _():
        x = x_ref[...]
        ms = jnp.mean(x * x, axis=-1, keepdims=True)
        y = x * lax.rsqrt(ms + EPS) * nw_ref[...]
        hb = (y * (1.0 + sc_ref[...]) + sh_ref[...]).astype(BF16)
        h_scr[...] = hb
        kt_ref[...] = lax.dot_general(wkt_ref[...], hb, (((1,), (1,)), ((), ())),
                                      preferred_element_type=F32).astype(BF16)

    o_ref[...] = jnp.dot(h_scr[...], w_ref[...], preferred_element_type=F32).astype(BF16)


def _inproj_call(geom, l, x, mod6, norm_w, w_in_bf, w_rkt_bf):
    tm, tn = 1024, 1024
    grid = (geom.n_tok // tm, D_IN // tn)
    return pl.pallas_call(
        _inproj_kernel,
        grid=grid,
        in_specs=[pl.BlockSpec((tm, D_MODEL), lambda i, j: (i, 0)),
                  _mod_spec(geom, l, 1, tm, 2),
                  _mod_spec(geom, l, 0, tm, 2),
                  pl.BlockSpec((1, D_MODEL), lambda i, j: (0, 0)),
                  pl.BlockSpec((D_MODEL, tn), lambda i, j: (0, j)),
                  pl.BlockSpec((RET_HEADS * RET_QK, D_MODEL), lambda i, j: (0, 0))],
        out_specs=[pl.BlockSpec((tm, tn), lambda i, j: (i, j)),
                   pl.BlockSpec((RET_HEADS * RET_QK, tm), lambda i, j: (0, i))],
        out_shape=[jax.ShapeDtypeStruct((geom.n_tok, D_IN), BF16),
                   jax.ShapeDtypeStruct((RET_HEADS * RET_QK, geom.n_tok), BF16)],
        scratch_shapes=[pltpu.VMEM((tm, D_MODEL), BF16)],
        compiler_params=_cparams(("arbitrary", "arbitrary")),
        name="inproj",
    )(x, mod6, mod6, norm_w.reshape(1, D_MODEL), w_in_bf, w_rkt_bf)


def _gelu_tanh(x):
    return 0.5 * x * (1.0 + jnp.tanh(math.sqrt(2.0 / math.pi) * (x + 0.044715 * (x * x * x))))


def _lru_kernel(geom, reverse, *refs):
    if reverse:
        (xa_ref, xp_ref, xn_ref, cw_ref, cb_ref, wg_ref, bg_ref, sp_ref, h0_ref, ga_ref, hf_ref,
         out_ref, hl_ref, a_scr, u_scr, h_scr, c_scr) = refs
    else:
        (xa_ref, xp_ref, xn_ref, cw_ref, cb_ref, wg_ref, bg_ref, sp_ref, h0_ref,
         out_ref, hl_ref, a_scr, u_scr, c_scr) = refs
        h_scr = out_ref
    g = pl.program_id(0)
    i = geom.n_blocks - 1 - g if reverse else g
    start = geom.seq_start(i)
    end = geom.seq_end(i)

    @pl.when(end if reverse else start)
    def _():
        c_scr[...] = h0_ref[...]

    x = xa_ref[...].astype(F32)
    pm = jnp.where(start, 0.0, 1.0)
    nm = jnp.where(end, 0.0, 1.0)
    hp = xp_ref.shape[0]
    p1 = xp_ref[hp - 1:hp, :].astype(F32) * pm
    p2 = xp_ref[hp - 2:hp - 1, :].astype(F32) * pm
    n0 = xn_ref[0:1, :].astype(F32) * nm
    row = lax.broadcasted_iota(jnp.int32, x.shape, 0)
    xm1 = jnp.where(row == 0, p1, pltpu.roll(x, 1, 0))
    xm2 = jnp.where(row == 0, p2, jnp.where(row == 1, p1, pltpu.roll(x, 2, 0)))
    xp1 = jnp.where(row == BLK - 1, n0, pltpu.roll(x, BLK - 1, 0))
    xc = (cw_ref[0:1, :] * xm2 + cw_ref[1:2, :] * xm1 + cw_ref[2:3, :] * x
          + cw_ref[3:4, :] * xp1 + cb_ref[...])

    gt = jnp.dot(xc.astype(BF16), wg_ref[...], preferred_element_type=F32) + bg_ref[...]
    r = jax.nn.sigmoid(gt[:, :D_RNN])
    ig = jax.nn.sigmoid(gt[:, D_RNN:])
    a = jnp.exp(-LRU_C * r * sp_ref[...])
    u = jnp.sqrt(1.0 - a * a) * ig * xc
    a_scr[...] = a
    u_scr[...] = u

    def body(t, h):
        tt = BLK - 1 - t if reverse else t
        h = a_scr[pl.ds(tt, 1), :] * h + u_scr[pl.ds(tt, 1), :]
        h_scr[pl.ds(tt, 1), :] = h
        return h

    h = lax.fori_loop(0, BLK, body, c_scr[...], unroll=8)
    c_scr[...] = h
    hl_ref[...] = h
    if reverse:
        gv = ga_ref[...].astype(F32)
        out_ref[...] = (_gelu_tanh(gv) * (hf_ref[...] + h_scr[...])).astype(BF16)


def _lru_call(geom, reverse, proj, conv_w, conv_b, wg, bg, sp, h0, hf=None):
    nb = geom.n_blocks
    halo = 16
    hpb = BLK // halo

    def blk(g):
        return nb - 1 - g if reverse else g

    d = 1 if reverse else 0
    in_specs = [
        pl.BlockSpec((BLK, D_RNN), lambda g: (blk(g), C_XA // D_RNN)),
        pl.BlockSpec((halo, D_RNN), lambda g: (jnp.maximum(blk(g) * hpb - 1, 0), C_XA // D_RNN)),
        pl.BlockSpec((halo, D_RNN), lambda g: (jnp.minimum((blk(g) + 1) * hpb, nb * hpb - 1), C_XA // D_RNN)),
        pl.BlockSpec((CONV_W, D_RNN), lambda g: (0, 0)),
        pl.BlockSpec((1, D_RNN), lambda g: (0, 0)),
        pl.BlockSpec((D_RNN, 2 * D_RNN), lambda g: (0, 0)),
        pl.BlockSpec((1, 2 * D_RNN), lambda g: (0, 0)),
        pl.BlockSpec((1, D_RNN), lambda g: (0, 0)),
        pl.BlockSpec((None, None, 1, D_RNN), lambda g: (geom.seq_id(blk(g)), d, 0, 0)),
    ]
    args = [proj, proj, proj, conv_w, conv_b, wg, bg, sp, h0]
    scratch = [pltpu.VMEM((BLK, D_RNN), F32), pltpu.VMEM((BLK, D_RNN), F32)]
    if reverse:
        in_specs += [pl.BlockSpec((BLK, D_RNN), lambda g: (blk(g), C_GA // D_RNN)),
                     pl.BlockSpec((BLK, D_RNN), lambda g: (blk(g), 0))]
        args += [proj, hf]
        scratch += [pltpu.VMEM((BLK, D_RNN), F32)]
        out_dtype = BF16
    else:
        out_dtype = F32
    scratch += [pltpu.VMEM((1, D_RNN), F32)]
    return pl.pallas_call(
        functools.partial(_lru_kernel, geom, reverse),
        grid=(nb,),
        in_specs=in_specs,
        out_specs=[pl.BlockSpec((BLK, D_RNN), lambda g: (blk(g), 0)),
                   pl.BlockSpec((None, 1, D_RNN), lambda g: (blk(g), 0, 0))],
        out_shape=[jax.ShapeDtypeStruct((geom.n_tok, D_RNN), out_dtype),
                   jax.ShapeDtypeStruct((nb, 1, D_RNN), F32)],
        scratch_shapes=scratch,
        compiler_params=_cparams(("arbitrary",)),
        name="lru_bwd" if reverse else "lru_fwd",
    )(*args)


def _group_rms(x, w, ones):
    xx = x * x
    hi = xx.astype(BF16)
    lo = (xx - hi.astype(F32)).astype(BF16)
    ss = (jnp.dot(hi, ones, preferred_element_type=F32)
          + jnp.dot(lo, ones, preferred_element_type=F32))
    return x * lax.rsqrt(ss * (1.0 / DA_QK) + EPS) * w


def _rope(x, cos, sin):
    lane = lax.broadcasted_iota(jnp.int32, x.shape, 1)
    first = (lane % (2 * ROPE_PAIRS)) < ROPE_PAIRS
    w = x.shape[1]
    partner = jnp.where(first, pltpu.roll(x, w - ROPE_PAIRS, 1), pltpu.roll(x, ROPE_PAIRS, 1))
    return x * cos + partner * sin


def _prep_kernel(rope, *refs):
    if rope:
        dq_ref, dk_ref, qw_ref, kw_ref, ones_ref, cos_ref, sin_ref, q_out, k_out = refs
    else:
        dq_ref, dk_ref, qw_ref, kw_ref, ones_ref, q_out, k_out, kf_out = refs
    ones = ones_ref[...]
    q = _group_rms(dq_ref[...].astype(F32), qw_ref[...], ones)
    k = _group_rms(dk_ref[...].astype(F32), kw_ref[...], ones)
    if rope:
        cos = jnp.concatenate([cos_ref[...]] * 4, axis=1)
        sin = jnp.concatenate([sin_ref[...]] * 4, axis=1)
        q = _rope(q, cos, sin)
        k = _rope(k, cos, sin)
    else:
        kf_out[...] = k
    q_out[...] = (q * (DA_QK ** -0.5)).astype(BF16)
    k_out[...] = k.astype(BF16)


def _prep_call(geom, latent, proj, qw, kw, ones, cos=None, sin=None):
    tm = 512
    w = DA_HEADS * 2 * DA_QK
    if latent:
        n, off = geom.n_lat, geom.n_ctx // tm
        per = geom.dec_seq // tm
    else:
        n, off = geom.n_ctx, 0
    in_specs = [pl.BlockSpec((tm, w), lambda i: (i + off, C_DQ // w)),
                pl.BlockSpec((tm, w), lambda i: (i + off, C_DK // w)),
                pl.BlockSpec((1, w), lambda i: (0, 0)),
                pl.BlockSpec((1, w), lambda i: (0, 0)),
                pl.BlockSpec((w, w), lambda i: (0, 0))]
    args = [proj, proj, qw, kw, ones]
    out_specs = [pl.BlockSpec((tm, w), lambda i: (i, 0)), pl.BlockSpec((tm, w), lambda i: (i, 0))]
    out_shape = [jax.ShapeDtypeStruct((n, w), BF16), jax.ShapeDtypeStruct((n, w), BF16)]
    if latent:
        in_specs += [pl.BlockSpec((tm, 2 * DA_QK), lambda i: (i % per, 0)),
                     pl.BlockSpec((tm, 2 * DA_QK), lambda i: (i % per, 0))]
        args += [cos, sin]
    else:
        out_specs.append(pl.BlockSpec((tm, w), lambda i: (i, 0)))
        out_shape.append(jax.ShapeDtypeStruct((n, w), F32))
    return pl.pallas_call(
        functools.partial(_prep_kernel, latent),
        grid=(n // tm,),
        in_specs=in_specs, out_specs=out_specs, out_shape=out_shape,
        compiler_params=_cparams(("arbitrary",)),
        name="qk_prep_lat" if latent else "qk_prep_ctx",
    )(*args)


def _softmax_rows(s):
    m = jnp.max(s, axis=-1, keepdims=True)
    e = jnp.exp(s - m)
    return e / jnp.sum(e, axis=-1, keepdims=True)


def _attn_kernel(out_scale, lam_ref, q_ref, k_ref, v_ref, sw_ref, o_ref):
    lam = lam_ref[0]
    q = q_ref[...]
    lane = lax.broadcasted_iota(jnp.int32, q.shape, 1)
    zero = jnp.zeros_like(q)
    q1 = jnp.where(lane < DA_QK, q, zero)
    q2 = jnp.where(lane >= DA_QK, q, zero)
    k = k_ref[...]
    nt = (((1,), (1,)), ((), ()))
    p1 = _softmax_rows(lax.dot_general(q1, k, nt, preferred_element_type=F32))
    p2 = _softmax_rows(lax.dot_general(q2, k, nt, preferred_element_type=F32))
    wgt = (p1 - lam * p2).astype(BF16)
    o = jnp.dot(wgt, v_ref[...], preferred_element_type=F32)
    y = o * lax.rsqrt(jnp.mean(o * o, axis=-1, keepdims=True) + EPS) * sw_ref[...]
    o_ref[...] = (y * out_scale).astype(BF16)


def _attn_call(lam, lam_init, q, k_all, v_all, subln_w, tq):
    b, t_q, _ = q.shape
    t_k = k_all.shape[1]
    hw = 2 * DA_QK
    return pl.pallas_call(
        functools.partial(_attn_kernel, 1.0 - lam_init),
        grid=(b, DA_HEADS, t_q // tq),
        in_specs=[pl.BlockSpec(memory_space=pltpu.SMEM),
                  pl.BlockSpec((None, tq, hw), lambda bi, h, qi: (bi, qi, h)),
                  pl.BlockSpec((None, t_k, hw), lambda bi, h, qi: (bi, 0, h)),
                  pl.BlockSpec((None, t_k, DA_V), lambda bi, h, qi: (bi, 0, h)),
                  pl.BlockSpec((1, DA_V), lambda bi, h, qi: (0, 0))],
        out_specs=pl.BlockSpec((None, tq, DA_V), lambda bi, h, qi: (bi, qi, h)),
        out_shape=jax.ShapeDtypeStruct((b, t_q, DA_HEADS * DA_V), BF16),
        compiler_params=_cparams(("arbitrary", "arbitrary", "arbitrary")),
        name="diff_attn",
    )(lam, q, k_all, v_all, subln_w.reshape(1, DA_V))


def _ret_state_update(kt, v, kd, cd, s_old):
    parts = []
    for h in range(RET_HEADS):
        rows = slice(h * RET_QK, (h + 1) * RET_QK)
        kh = (kt[rows, :].astype(F32) * kd[rows, :]).astype(BF16)
        parts.append(jnp.dot(kh, v[:, h * RET_V:(h + 1) * RET_V], preferred_element_type=F32))
    return cd * s_old + jnp.concatenate(parts, axis=0)


def _ret_bwd_kernel(geom, kt_ref, v_ref, kd_ref, cd_ref, s0_ref, sstart_ref, send_ref, s_scr):
    i = geom.n_blocks - 1 - pl.program_id(0)

    @pl.when(geom.seq_end(i))
    def _():
        s_scr[...] = s0_ref[...]

    s_old = s_scr[...]
    sstart_ref[...] = s_old
    kt = kt_ref[...] * jnp.asarray(RET_QK ** -0.5, BF16)
    s_new = _ret_state_update(kt, v_ref[...], kd_ref[...], cd_ref[...], s_old)
    s_scr[...] = s_new
    send_ref[...] = s_new


def _ret_bwd_call(geom, proj, rkt, kd_b, cd_b, s0):
    nb = geom.n_blocks
    hs = RET_HEADS * RET_QK

    def blk(g):
        return nb - 1 - g

    return pl.pallas_call(
        functools.partial(_ret_bwd_kernel, geom),
        grid=(nb,),
        in_specs=[pl.BlockSpec((hs, BLK), lambda g: (0, blk(g))),
                  pl.BlockSpec((BLK, RET_HEADS * RET_V), lambda g: (blk(g), C_RV // (RET_HEADS * RET_V))),
                  pl.BlockSpec((hs, BLK), lambda g: (0, 0)),
                  pl.BlockSpec((hs, RET_V), lambda g: (0, 0)),
                  pl.BlockSpec((None, None, hs, RET_V), lambda g: (geom.seq_id(blk(g)), 1, 0, 0))],
        out_specs=[pl.BlockSpec((None, hs, RET_V), lambda g: (blk(g), 0, 0)),
                   pl.BlockSpec((None, hs, RET_V), lambda g: (blk(g), 0, 0))],
        out_shape=[jax.ShapeDtypeStruct((nb, hs, RET_V), F32),
                   jax.ShapeDtypeStruct((nb, hs, RET_V), F32)],
        scratch_shapes=[pltpu.VMEM((hs, RET_V), F32)],
        compiler_params=_cparams(("arbitrary",)),
        name="ret_bwd_state",
    )(rkt, proj, kd_b, cd_b, s0)


def _ret_main_kernel(geom, q_ref, kt_ref, v_ref, g_ref, dsum_ref, qdf_ref, qdb_ref, kd_ref, cd_ref,
                     s0_ref, sb_ref, o_ref, send_ref, s_scr):
    i = pl.program_id(0)

    @pl.when(geom.seq_start(i))
    def _():
        s_scr[...] = s0_ref[...]

    s_f = s_scr[...]
    s_fb = s_f.astype(BF16)
    s_bb = sb_ref[...].astype(BF16)
    q = q_ref[...].astype(F32)
    kt = kt_ref[...] * jnp.asarray(RET_QK ** -0.5, BF16)
    v = v_ref[...]
    lane = lax.broadcasted_iota(jnp.int32, q.shape, 1)
    for h in range(RET_HEADS):
        in_head = (lane >= h * RET_QK) & (lane < (h + 1) * RET_QK)
        qh = jnp.where(in_head, q, 0.0)
        vh = v[:, h * RET_V:(h + 1) * RET_V]
        sc = jnp.dot(qh.astype(BF16), kt, preferred_element_type=F32) * dsum_ref[h]
        o = jnp.dot(sc.astype(BF16), vh, preferred_element_type=F32)
        o += jnp.dot((qh * qdf_ref[...]).astype(BF16), s_fb, preferred_element_type=F32)
        o += jnp.dot((qh * qdb_ref[...]).astype(BF16), s_bb, preferred_element_type=F32)
        y = o * lax.rsqrt(jnp.mean(o * o, axis=-1, keepdims=True) + EPS)
        gv = g_ref[:, h * RET_V:(h + 1) * RET_V].astype(F32)
        o_ref[:, h * RET_V:(h + 1) * RET_V] = (y * (gv * jax.nn.sigmoid(gv))).astype(BF16)
    s_new = _ret_state_update(kt, v, kd_ref[...], cd_ref[...], s_f)
    s_scr[...] = s_new
    send_ref[...] = s_new


def _ret_main_call(geom, proj, rkt, dsum, qdf, qdb, kd_f, cd_f, s0, sb_start):
    nb = geom.n_blocks
    hs = RET_HEADS * RET_QK
    hv = RET_HEADS * RET_V
    return pl.pallas_call(
        functools.partial(_ret_main_kernel, geom),
        grid=(nb,),
        in_specs=[pl.BlockSpec((BLK, hs), lambda g: (g, C_RQ // hs)),
                  pl.BlockSpec((hs, BLK), lambda g: (0, g)),
                  pl.BlockSpec((BLK, hv), lambda g: (g, C_RV // hv)),
                  pl.BlockSpec((BLK, hv), lambda g: (g, C_RG // hv)),
                  pl.BlockSpec((RET_HEADS, BLK, BLK), lambda g: (0, 0, 0)),
                  pl.BlockSpec((BLK, hs), lambda g: (0, 0)),
                  pl.BlockSpec((BLK, hs), lambda g: (0, 0)),
                  pl.BlockSpec((hs, BLK), lambda g: (0, 0)),
                  pl.BlockSpec((hs, RET_V), lambda g: (0, 0)),
                  pl.BlockSpec((None, None, hs, RET_V), lambda g: (geom.seq_id(g), 0, 0, 0)),
                  pl.BlockSpec((None, hs, RET_V), lambda g: (g, 0, 0))],
        out_specs=[pl.BlockSpec((BLK, hv), lambda g: (g, 0)),
                   pl.BlockSpec((None, hs, RET_V), lambda g: (g, 0, 0))],
        out_shape=[jax.ShapeDtypeStruct((geom.n_tok, hv), BF16),
                   jax.ShapeDtypeStruct((nb, hs, RET_V), F32)],
        scratch_shapes=[pltpu.VMEM((hs, RET_V), F32)],
        compiler_params=_cparams(("arbitrary",)),
        name="ret_main",
    )(proj, rkt, proj, proj, dsum, qdf, qdb, kd_f, cd_f, s0, sb_start)


def _ret_tables(ret_decay_l):
    log_g = jax.nn.log_sigmoid(ret_decay_l.astype(F32))
    pos = jnp.arange(BLK, dtype=F32)
    diff = pos[:, None] - pos[None, :]
    lf = log_g[0][:, None, None]
    lb = log_g[1][:, None, None]
    dsum = (jnp.where(diff >= 0, jnp.exp(jnp.maximum(diff, 0.0)[None] * lf), 0.0)
            + jnp.where(diff <= 0, jnp.exp(jnp.maximum(-diff, 0.0)[None] * lb), 0.0))

    def per_lane(e, lg):
        return jnp.repeat(jnp.exp(e[:, None] * lg[None, :]), RET_QK, axis=1)

    qdf = per_lane(pos + 1.0, log_g[0])
    qdb = per_lane(BLK - pos, log_g[1])
    kd_f = per_lane(BLK - 1.0 - pos, log_g[0]).T
    kd_b = per_lane(pos, log_g[1]).T
    cd_f = jnp.broadcast_to(jnp.repeat(jnp.exp(BLK * log_g[0]), RET_QK)[:, None], (RET_HEADS * RET_QK, RET_V))
    cd_b = jnp.broadcast_to(jnp.repeat(jnp.exp(BLK * log_g[1]), RET_QK)[:, None], (RET_HEADS * RET_QK, RET_V))
    return dsum, qdf, qdb, kd_f, kd_b, cd_f, cd_b


def _merge_kernel(ba_ref, bb_ref, bc_ref, g0_ref, g1_ref, g2_ref, x_ref, gate_ref, sc_ref, sh_ref,
                  nw_ref, wb_ref, wo_ref, rhi_ref, rlo_ref, x1_ref, h2_ref, lt_ref):
    acc = None
    for br, (b_ref, g_ref) in enumerate(((ba_ref, g0_ref), (bb_ref, g1_ref), (bc_ref, g2_ref))):
        p = jnp.dot(b_ref[...], wb_ref[br], preferred_element_type=F32)
        t = jax.nn.sigmoid(g_ref[...].astype(F32)) * p
        acc = t if acc is None else acc + t
    m = jnp.dot(acc.astype(BF16), wo_ref[...], preferred_element_type=F32)
    x1 = x_ref[...] + gate_ref[...] * m
    x1_ref[...] = x1
    ms = jnp.mean(x1 * x1, axis=-1, keepdims=True)
    h2 = x1 * lax.rsqrt(ms + EPS) * nw_ref[...] * (1.0 + sc_ref[...]) + sh_ref[...]
    h2b = h2.astype(BF16)
    h2_ref[...] = h2b
    h2lo = (h2 - h2b.astype(F32)).astype(BF16)
    nt = (((1,), (1,)), ((), ()))
    lt_ref[...] = (lax.dot_general(rhi_ref[...], h2b, nt, preferred_element_type=F32)
                   + lax.dot_general(rhi_ref[...], h2lo, nt, preferred_element_type=F32)
                   + lax.dot_general(rlo_ref[...], h2b, nt, preferred_element_type=F32))


def _merge_call(geom, l, ba, bb, bc, proj, x, mod6, norm2_w, wb_bf, wo_bf, r_hi, r_lo):
    tm = 512
    gcol = C_GL // D_MODEL
    full = lambda shape: pl.BlockSpec(shape, lambda i: tuple(0 for _ in shape))
    tok = lambda w: pl.BlockSpec((tm, w), lambda i: (i, 0))
    return pl.pallas_call(
        _merge_kernel,
        grid=(geom.n_tok // tm,),
        in_specs=[tok(BRANCH_W), tok(BRANCH_W), tok(BRANCH_W),
                  pl.BlockSpec((tm, D_MODEL), lambda i: (i, gcol)),
                  pl.BlockSpec((tm, D_MODEL), lambda i: (i, gcol + 1)),
                  pl.BlockSpec((tm, D_MODEL), lambda i: (i, gcol + 2)),
                  tok(D_MODEL),
                  _mod_spec(geom, l, 2, tm, 1), _mod_spec(geom, l, 4, tm, 1), _mod_spec(geom, l, 3, tm, 1),
                  full((1, D_MODEL)),
                  full((N_BRANCH, BRANCH_W, D_MODEL)), full((D_MODEL, D_MODEL)),
                  full((N_EXPERTS, D_MODEL)), full((N_EXPERTS, D_MODEL))],
        out_specs=[tok(D_MODEL), tok(D_MODEL), pl.BlockSpec((N_EXPERTS, tm), lambda i: (0, i))],
        out_shape=[jax.ShapeDtypeStruct((geom.n_tok, D_MODEL), F32),
                   jax.ShapeDtypeStruct((geom.n_tok, D_MODEL), BF16),
                   jax.ShapeDtypeStruct((N_EXPERTS, geom.n_tok), F32)],
        compiler_params=_cparams(("arbitrary",)),
        name="merge_out",
    )(ba, bb, bc, proj, proj, proj, x, mod6, mod6, mod6, norm2_w.reshape(1, D_MODEL),
      wb_bf, wo_bf, r_hi, r_lo)


def _router_kernel(lt_ref, bias_ref, g_ref):
    per = N_EXPERTS // N_GROUPS
    tm = lt_ref.shape[1]
    scores = jax.nn.sigmoid(lt_ref[...])
    biased = scores + bias_ref[...]
    b3 = biased.reshape(N_GROUPS, per, tm)
    neg = jnp.float32(-jnp.inf)
    m1 = jnp.max(b3, axis=1, keepdims=True)
    is_m1 = b3 == m1
    cnt = jnp.sum(is_m1.astype(F32), axis=1, keepdims=True)
    m2 = jnp.max(jnp.where(is_m1, neg, b3), axis=1, keepdims=True)
    grp = (m1 + jnp.where(cnt >= 2.0, m1, m2)).reshape(N_GROUPS, tm)
    gidx = lax.broadcasted_iota(jnp.int32, (N_GROUPS, tm), 0)
    grank = jnp.zeros((N_GROUPS, tm), F32)
    for g2 in range(N_GROUPS):
        other = grp[g2:g2 + 1, :]
        ahead = (other > grp) | ((other == grp) & (gidx > g2))
        grank += ahead.astype(F32)
    gsel = (grank < float(TOPK_GROUPS)).astype(F32)
    emask = jnp.broadcast_to(gsel.reshape(N_GROUPS, 1, tm), (N_GROUPS, per, tm)).reshape(N_EXPERTS, tm)
    masked = jnp.where(emask > 0.0, biased, neg)
    eidx = lax.broadcasted_iota(jnp.int32, (N_EXPERTS, tm), 0)
    erank = jnp.zeros((N_EXPERTS, tm), F32)
    for e2 in range(N_EXPERTS):
        other = masked[e2:e2 + 1, :]
        ahead = (other > masked) | ((other == masked) & (eidx > e2))
        erank += ahead.astype(F32)
    w = jnp.where(erank < float(TOP_K), scores, 0.0)
    g_ref[...] = w / jnp.sum(w, axis=0, keepdims=True) * ROUTED_SCALE


def _router_call(geom, logits_t, bias):
    tm = 512
    return pl.pallas_call(
        _router_kernel,
        grid=(geom.n_tok // tm,),
        in_specs=[pl.BlockSpec((N_EXPERTS, tm), lambda i: (0, i)),
                  pl.BlockSpec((N_EXPERTS, 1), lambda i: (0, 0))],
        out_specs=pl.BlockSpec((N_EXPERTS, tm), lambda i: (0, i)),
        out_shape=jax.ShapeDtypeStruct((N_EXPERTS, geom.n_tok), F32),
        compiler_params=_cparams(("arbitrary",)),
        name="router",
    )(logits_t, bias.reshape(N_EXPERTS, 1))


MOE_EB = 5


def _moe_kernel(h_ref, g_ref, gu_ref, dn_ref, x1_ref, gate_ref, o_ref, acc_scr):
    j = pl.program_id(1)

    @pl.when(j == 0)
    def _():
        acc_scr[...] = jnp.zeros_like(acc_scr)

    h = h_ref[...]
    gts = g_ref[...]
    lane = lax.broadcasted_iota(jnp.int32, gts.shape, 1)
    acc = acc_scr[...]
    for e in range(MOE_EB):
        a = jnp.dot(h, gu_ref[e], preferred_element_type=F32)
        hg = a[:, :D_EXPERT]
        hu = a[:, D_EXPERT:]
        ge = jnp.sum(jnp.where(lane == j * MOE_EB + e, gts, 0.0), axis=1, keepdims=True)
        act = (hg * jax.nn.sigmoid(hg)) * hu * ge
        acc += jnp.dot(act.astype(BF16), dn_ref[e], preferred_element_type=F32)
    acc_scr[...] = acc

    @pl.when(j == pl.num_programs(1) - 1)
    def _():
        o_ref[...] = x1_ref[...] + gate_ref[...] * acc


def _moe_call(geom, l, h2, gates, gu_bf, dn_bf, x1, mod6):
    tm = 512
    ne = gu_bf.shape[0]
    return pl.pallas_call(
        _moe_kernel,
        grid=(geom.n_tok // tm, ne // MOE_EB),
        in_specs=[pl.BlockSpec((tm, D_MODEL), lambda i, j: (i, 0)),
                  pl.BlockSpec((tm, 128), lambda i, j: (i, 0)),
                  pl.BlockSpec((MOE_EB, D_MODEL, 2 * D_EXPERT), lambda i, j: (j, 0, 0)),
                  pl.BlockSpec((MOE_EB, D_EXPERT, D_MODEL), lambda i, j: (j, 0, 0)),
                  pl.BlockSpec((tm, D_MODEL), lambda i, j: (i, 0)),
                  _mod_spec(geom, l, 5, tm, 2)],
        out_specs=pl.BlockSpec((tm, D_MODEL), lambda i, j: (i, 0)),
        out_shape=jax.ShapeDtypeStruct((geom.n_tok, D_MODEL), F32),
        scratch_shapes=[pltpu.VMEM((tm, D_MODEL), F32)],
        compiler_params=_cparams(("arbitrary", "arbitrary")),
        name="moe_experts",
    )(h2, gates, gu_bf, dn_bf, x1, mod6)


def _rope_tables(dec_seq):
    rows = dec_seq // GRID_W
    row = jnp.repeat(jnp.arange(rows, dtype=F32), GRID_W)
    col = jnp.tile(jnp.arange(GRID_W, dtype=F32), rows)
    inv = ROPE_BASE ** (-jnp.arange(ROPE_PAIRS, dtype=F32) / ROPE_PAIRS)
    ar = row[:, None] * inv[None, :]
    ac = col[:, None] * inv[None, :]
    cos64 = jnp.concatenate([jnp.cos(ar), jnp.cos(ar), jnp.cos(ac), jnp.cos(ac)], axis=1)
    sin64 = jnp.concatenate([-jnp.sin(ar), jnp.sin(ar), -jnp.sin(ac), jnp.sin(ac)], axis=1)
    return jnp.tile(cos64, (1, 2)), jnp.tile(sin64, (1, 2))


def _block_diag_gate(wg_dir):
    eye = jnp.eye(LRU_BLOCKS, dtype=F32)
    dense = jnp.einsum('gnij,nm->gnimj', wg_dir.astype(F32), eye).reshape(2, D_RNN, D_RNN)
    return jnp.concatenate([dense[0], dense[1]], axis=1)


def kernel(x_prompt, x_sample, cache_k, cache_v, state_lru, state_ret, c, c_ctx, ada_w, ada_b, norm1_w, norm2_w, w_in, conv_w, conv_b, lru_gate_w, lru_gate_b, lru_lambda, q_norm_w, k_norm_w, diff_lambda, subln_w, ret_decay, w_branch, w_out, router_w, router_bias, w_exp_gu, w_exp_down, w_sh_gu, w_sh_down):
    batch, seq, _ = x_prompt.shape
    dec_batch, dec_seq, _ = x_sample.shape
    assert 1 + dec_batch <= MOD_ROWS
    geom = _Geom(batch, seq, dec_batch, dec_seq)
    hs = RET_HEADS * RET_QK
    aw = DA_HEADS * 2 * DA_QK

    x = jnp.concatenate([x_prompt.reshape(geom.n_ctx, D_MODEL), x_sample.reshape(geom.n_lat, D_MODEL)], axis=0)
    cvec = jnp.zeros((MOD_ROWS, D_MODEL), F32).at[0].set(c_ctx).at[1:1 + dec_batch].set(c)
    mod6 = _ada_call(cvec, ada_w, ada_b).reshape(DEPTH, MOD_ROWS, 6, 1, D_MODEL)

    ones_bd = jnp.kron(jnp.eye(aw // DA_QK, dtype=F32), jnp.ones((DA_QK, DA_QK), F32)).astype(BF16)
    cos_t, sin_t = _rope_tables(dec_seq)

    ks, vs, lrus, rets = [], [], [], []
    for l in range(DEPTH):
        lam_init = 0.8 - 0.6 * math.exp(-0.3 * l)
        w_in_bf = w_in[l].astype(BF16)
        w_rkt_bf = w_in[l][:, C_RK:C_RK + hs].T.astype(BF16)
        proj, rkt = _inproj_call(geom, l, x, mod6, norm1_w[l], w_in_bf, w_rkt_bf)

        sp = jax.nn.softplus(-lru_lambda[l].astype(F32))
        h0 = jnp.concatenate([jnp.zeros((batch, 2, D_RNN), F32), state_lru[:, l].astype(F32)], axis=0)
        h0 = h0.reshape(geom.n_seq, 2, 1, D_RNN)
        cb = conv_b[l].reshape(1, D_RNN)
        lru_args = []
        for d in range(2):
            lru_args.append((_block_diag_gate(lru_gate_w[l, d]).astype(BF16),
                             lru_gate_b[l, d].reshape(1, 2 * D_RNN), sp[d].reshape(1, D_RNN)))
        hf, hf_last = _lru_call(geom, False, proj, conv_w[l], cb, *lru_args[0], h0)
        branch_a, hb_last = _lru_call(geom, True, proj, conv_w[l], cb, *lru_args[1], h0, hf)

        qw = jnp.tile(q_norm_w[l], aw // DA_QK).reshape(1, aw)
        kw = jnp.tile(k_norm_w[l], aw // DA_QK).reshape(1, aw)
        q_c, k_c, k_c32 = _prep_call(geom, False, proj, qw, kw, ones_bd)
        q_l, k_l = _prep_call(geom, True, proj, qw, kw, ones_bd, cos_t, sin_t)
        lam_p = diff_lambda[l].astype(F32)
        lam = (jnp.exp(jnp.sum(lam_p[0] * lam_p[1])) - jnp.exp(jnp.sum(lam_p[2] * lam_p[3])) + lam_init).reshape(1)
        v_tok = proj[:, C_DV:C_DV + DA_HEADS * DA_V]
        v_c = v_tok[:geom.n_ctx].reshape(batch, seq, -1)
        k_all = jnp.concatenate([cache_k[:, l].reshape(dec_batch, -1, aw).astype(BF16),
                                 k_l.reshape(dec_batch, dec_seq, aw)], axis=1)
        v_all = jnp.concatenate([cache_v[:, l].reshape(dec_batch, -1, DA_HEADS * DA_V).astype(BF16),
                                 v_tok[geom.n_ctx:].reshape(dec_batch, dec_seq, -1)], axis=1)
        att_c = _attn_call(lam, lam_init, q_c.reshape(batch, seq, aw), k_c.reshape(batch, seq, aw), v_c,
                           subln_w[l], 256)
        att_l = _attn_call(lam, lam_init, q_l.reshape(dec_batch, dec_seq, aw), k_all, v_all, subln_w[l], 128)
        branch_b = jnp.concatenate([att_c.reshape(geom.n_ctx, -1), att_l.reshape(geom.n_lat, -1)], axis=0)

        dsum, qdf, qdb, kd_f, kd_b, cd_f, cd_b = _ret_tables(ret_decay[l])
        s0 = jnp.concatenate([jnp.zeros((batch, 2, hs, RET_V), F32),
                              state_ret[:, l].astype(F32).reshape(dec_batch, 2, hs, RET_V)], axis=0)
        sb_start, sb_end = _ret_bwd_call(geom, proj, rkt, kd_b, cd_b, s0)
        branch_c, sf_end = _ret_main_call(geom, proj, rkt, dsum, qdf, qdb, kd_f, cd_f, s0, sb_start)

        r_t = router_w[l].T.astype(F32)
        r_hi = r_t.astype(BF16)
        r_lo = (r_t - r_hi.astype(F32)).astype(BF16)
        x1, h2, logits_t = _merge_call(geom, l, branch_a, branch_b, branch_c, proj, x, mod6, norm2_w[l],
                                       w_branch[l].astype(BF16), w_out[l].astype(BF16), r_hi, r_lo)
        gates_t = _router_call(geom, logits_t, router_bias[l].astype(F32))
        gates = jnp.concatenate([gates_t.T, jnp.ones((geom.n_tok, 1), F32),
                                 jnp.zeros((geom.n_tok, 128 - N_EXPERTS - 1), F32)], axis=1)
        gu_bf = jnp.concatenate([w_exp_gu[l], w_sh_gu[l][None]], axis=0).astype(BF16)
        dn_bf = jnp.concatenate([w_exp_down[l], w_sh_down[l][None]], axis=0).astype(BF16)
        x = _moe_call(geom, l, h2, gates, gu_bf, dn_bf, x1, mod6)

        ks.append(k_c32.reshape(batch, seq, DA_HEADS, 2, DA_QK))
        vs.append(v_c.astype(F32).reshape(batch, seq, DA_HEADS, DA_V))
        lrus.append(jnp.stack([hf_last[:batch, 0], hb_last[:batch, 0]], axis=1))
        rets.append(jnp.stack([sf_end[:batch].reshape(batch, RET_HEADS, RET_QK, RET_V),
                               sb_end[:batch].reshape(batch, RET_HEADS, RET_QK, RET_V)], axis=1))

    y_prompt = x[:geom.n_ctx].reshape(batch, seq, D_MODEL)
    y_sample = x[geom.n_ctx:].reshape(dec_batch, dec_seq, D_MODEL)
    return (y_prompt, y_sample, jnp.stack(ks, axis=1), jnp.stack(vs, axis=1),
            jnp.stack(lrus, axis=1), jnp.stack(rets, axis=1))
```

```python
import functools
import math

import numpy as np
import jax
import jax.numpy as jnp
from jax import lax
from jax.experimental import pallas as pl
from jax.experimental.pallas import tpu as pltpu

F32 = jnp.float32
BF16 = jnp.bfloat16

D_MODEL = 1024
DEPTH = 2
GRID_W = 64
D_RNN = 512
LRU_BLOCKS = 8
LRU_BLOCK = D_RNN // LRU_BLOCKS
CONV_W = 4
LRU_C = 8.0
DA_HEADS = 4
DA_QK = 64
DA_V = 128
ROPE_PAIRS = DA_QK // 4
ROPE_BASE = 10000.0
RET_HEADS = 4
RET_QK = 64
RET_V = 128
BRANCH_W = 512
N_BRANCH = 3
D_IN = 7168
N_EXPERTS = 64
TOP_K = 8
N_GROUPS = 8
TOPK_GROUPS = 4
D_EXPERT = 256
ROUTED_SCALE = 2.5
EPS = 1e-6

C_XA, C_GA, C_DQ, C_DK, C_DV = 0, 512, 1024, 1536, 2048
C_RQ, C_RK, C_RV, C_RG, C_GL = 2560, 2816, 3072, 3584, 4096

BLK = 256
LRU_SUB = 8
LRU_LANES = D_RNN // 128
MOD_ROWS = 8
VMEM_LIMIT = 56 * 1024 * 1024


def _cparams(sem):
    return pltpu.CompilerParams(dimension_semantics=sem, vmem_limit_bytes=VMEM_LIMIT)


class _Geom:
    def __init__(self, batch, seq, dec_batch, dec_seq):
        assert seq == BLK and dec_seq % BLK == 0
        self.batch, self.seq, self.dec_batch, self.dec_seq = batch, seq, dec_batch, dec_seq
        self.n_ctx = batch * seq
        self.n_lat = dec_batch * dec_seq
        self.n_tok = self.n_ctx + self.n_lat
        self.ctx_blocks = self.n_ctx // BLK
        self.lat_blocks = dec_seq // BLK
        self.n_blocks = self.n_tok // BLK
        self.n_seq = batch + dec_batch

    def mod_row(self, i, tile):
        nct = self.n_ctx // tile
        per = self.dec_seq // tile
        return jnp.where(i < nct, 0, 1 + (i - nct) // per)

    def seq_id(self, i):
        return jnp.where(i < self.ctx_blocks, i, self.ctx_blocks + (i - self.ctx_blocks) // self.lat_blocks)

    def seq_start(self, i):
        return jnp.logical_or(i < self.ctx_blocks, (i - self.ctx_blocks) % self.lat_blocks == 0)

    def seq_end(self, i):
        return jnp.logical_or(i < self.ctx_blocks, (i - self.ctx_blocks) % self.lat_blocks == self.lat_blocks - 1)


def _ada_kernel(c_ref, w_ref, b_ref, o_ref):
    cv = c_ref[...]
    s = cv * jax.nn.sigmoid(cv)
    o_ref[...] = jnp.dot(s, w_ref[...], preferred_element_type=F32,
                         precision=lax.Precision.HIGHEST) + b_ref[...]


def _ada_call(cvec, ada_w, ada_b):
    depth = ada_w.shape[0]
    nt = 6
    return pl.pallas_call(
        _ada_kernel,
        grid=(depth, nt),
        in_specs=[pl.BlockSpec((MOD_ROWS, D_MODEL), lambda l, j: (0, 0)),
                  pl.BlockSpec((None, D_MODEL, D_MODEL), lambda l, j: (l, 0, j)),
                  pl.BlockSpec((None, 1, D_MODEL), lambda l, j: (l, 0, j))],
        out_specs=pl.BlockSpec((None, MOD_ROWS, D_MODEL), lambda l, j: (l, 0, j)),
        out_shape=jax.ShapeDtypeStruct((depth, MOD_ROWS, 6 * D_MODEL), F32),
        compiler_params=_cparams(("arbitrary", "arbitrary")),
        name="ada_mod",
    )(cvec, ada_w, ada_b.reshape(depth, 1, 6 * D_MODEL))


def _mod_spec(geom, l, which, tile, ngrid):
    if ngrid == 1:
        return pl.BlockSpec((None, None, None, 1, D_MODEL),
                            lambda i: (l, geom.mod_row(i, tile), which, 0, 0))
    return pl.BlockSpec((None, None, None, 1, D_MODEL),
                        lambda i, j: (l, geom.mod_row(i, tile), which, 0, 0))


def _inproj_kernel(x_ref, sc_ref, sh_ref, nw_ref, w_ref, wkt_ref, o_ref, kt_ref, h_scr):
    @pl.when(pl.program_id(1) == 0)
    def _():
        x = x_ref[...]
        ms = jnp.mean(x * x, axis=-1, keepdims=True)
        y = x * lax.rsqrt(ms + EPS) * nw_ref[...]
        hb = (y * (1.0 + sc_ref[...]) + sh_ref[...]).astype(BF16)
        h_scr[...] = hb
        kt_ref[...] = lax.dot_general(wkt_ref[...], hb, (((1,), (1,)), ((), ())),
                                      preferred_element_type=F32).astype(BF16)

    o_ref[...] = jnp.dot(h_scr[...], w_ref[...], preferred_element_type=F32).astype(BF16)


def _inproj_call(geom, l, x, mod6, norm_w, w_in_bf, w_rkt_bf):
    tm, tn = 1024, 1024
    grid = (geom.n_tok // tm, D_IN // tn)
    return pl.pallas_call(
        _inproj_kernel,
        grid=grid,
        in_specs=[pl.BlockSpec((tm, D_MODEL), lambda i, j: (i, 0)),
                  _mod_spec(geom, l, 1, tm, 2),
                  _mod_spec(geom, l, 0, tm, 2),
                  pl.BlockSpec((1, D_MODEL), lambda i, j: (0, 0)),
                  pl.BlockSpec((D_MODEL, tn), lambda i, j: (0, j)),
                  pl.BlockSpec((RET_HEADS * RET_QK, D_MODEL), lambda i, j: (0, 0))],
        out_specs=[pl.BlockSpec((tm, tn), lambda i, j: (i, j)),
                   pl.BlockSpec((RET_HEADS * RET_QK, tm), lambda i, j: (0, i))],
        out_shape=[jax.ShapeDtypeStruct((geom.n_tok, D_IN), BF16),
                   jax.ShapeDtypeStruct((RET_HEADS * RET_QK, geom.n_tok), BF16)],
        scratch_shapes=[pltpu.VMEM((tm, D_MODEL), BF16)],
        compiler_params=_cparams(("arbitrary", "arbitrary")),
        name="inproj",
    )(x, mod6, mod6, norm_w.reshape(1, D_MODEL), w_in_bf, w_rkt_bf)


def _gelu_tanh(x):
    return 0.5 * x * (1.0 + jnp.tanh(math.sqrt(2.0 / math.pi) * (x + 0.044715 * (x * x * x))))


def _lru_kernel(geom, reverse, *refs):
    if reverse:
        (xa_ref, xp_ref, xn_ref, cw_ref, cb_ref, wg_ref, bg_ref, sp_ref, h0_ref, perm_ref, permt_ref,
         ga_ref, hf_ref, out_ref, hl_ref, c_scr) = refs
    else:
        (xa_ref, xp_ref, xn_ref, cw_ref, cb_ref, wg_ref, bg_ref, sp_ref, h0_ref, perm_ref,
         out_ref, hl_ref, c_scr) = refs
    g = pl.program_id(0)
    i = geom.n_blocks - 1 - g if reverse else g
    start = geom.seq_start(i)
    end = geom.seq_end(i)

    @pl.when(end if reverse else start)
    def _():
        c_scr[...] = h0_ref[...]

    sub_len = BLK // LRU_SUB
    perm = perm_ref[...]
    x = jnp.dot(perm, xa_ref[...], preferred_element_type=F32)
    pm = jnp.where(start, 0.0, 1.0)
    nm = jnp.where(end, 0.0, 1.0)
    hp = xp_ref.shape[0]
    p1 = xp_ref[hp - 1:hp, :].astype(F32) * pm
    p2 = xp_ref[hp - 2:hp - 1, :].astype(F32) * pm
    n0 = xn_ref[0:1, :].astype(F32) * nm
    row = lax.broadcasted_iota(jnp.int32, x.shape, 0)
    xm1 = jnp.where(row < LRU_SUB, pltpu.roll(x, LRU_SUB + 1, 0), pltpu.roll(x, LRU_SUB, 0))
    xm1 = jnp.where(row == 0, p1, xm1)
    xm2 = jnp.where(row < 2 * LRU_SUB, pltpu.roll(x, 2 * LRU_SUB + 1, 0), pltpu.roll(x, 2 * LRU_SUB, 0))
    xm2 = jnp.where(row == 0, p2, jnp.where(row == LRU_SUB, p1, xm2))
    xp1 = jnp.where(row >= BLK - LRU_SUB, pltpu.roll(x, BLK - LRU_SUB - 1, 0),
                    pltpu.roll(x, BLK - LRU_SUB, 0))
    xp1 = jnp.where(row == BLK - 1, n0, xp1)
    xc = (cw_ref[0:1, :] * xm2 + cw_ref[1:2, :] * xm1 + cw_ref[2:3, :] * x
          + cw_ref[3:4, :] * xp1 + cb_ref[...])

    gt = jnp.dot(xc.astype(BF16), wg_ref[...], preferred_element_type=F32) + bg_ref[...]
    r = jax.nn.sigmoid(gt[:, :D_RNN])
    ig = jax.nn.sigmoid(gt[:, D_RNN:])
    a = jnp.exp(-LRU_C * r * sp_ref[...])
    u = jnp.sqrt(1.0 - a * a) * ig * xc

    h = jnp.zeros((LRU_SUB, D_RNN), F32)
    p = jnp.ones((LRU_SUB, D_RNN), F32)
    h_loc = [None] * sub_len
    p_loc = [None] * sub_len
    for t in (range(sub_len - 1, -1, -1) if reverse else range(sub_len)):
        a_t = a[t * LRU_SUB:(t + 1) * LRU_SUB, :]
        h = a_t * h + u[t * LRU_SUB:(t + 1) * LRU_SUB, :]
        p = a_t * p
        h_loc[t] = h
        p_loc[t] = p
    h_in = [None] * LRU_SUB
    state = c_scr[...]
    for k in (range(LRU_SUB - 1, -1, -1) if reverse else range(LRU_SUB)):
        h_in[k] = state
        state = h[k:k + 1, :] + p[k:k + 1, :] * state
    c_scr[...] = state
    hl_ref[...] = state
    h_in = jnp.concatenate(h_in, axis=0)
    h_full = jnp.concatenate([h_loc[t] + p_loc[t] * h_in for t in range(sub_len)], axis=0)
    if reverse:
        gv = jnp.dot(perm, ga_ref[...], preferred_element_type=F32)
        y = (_gelu_tanh(gv) * (hf_ref[...] + h_full)).astype(BF16)
        out_ref[...] = jnp.dot(permt_ref[...], y, preferred_element_type=F32).astype(BF16)
    else:
        out_ref[...] = h_full


def _lru_call(geom, reverse, proj, conv_w, conv_b, wg, bg, sp, h0, hf=None):
    nb = geom.n_blocks
    halo = 16
    hpb = BLK // halo

    def blk(g):
        return nb - 1 - g if reverse else g

    d = 1 if reverse else 0
    in_specs = [
        pl.BlockSpec((BLK, D_RNN), lambda g: (blk(g), C_XA // D_RNN)),
        pl.BlockSpec((halo, D_RNN), lambda g: (jnp.maximum(blk(g) * hpb - 1, 0), C_XA // D_RNN)),
        pl.BlockSpec((halo, D_RNN), lambda g: (jnp.minimum((blk(g) + 1) * hpb, nb * hpb - 1), C_XA // D_RNN)),
        pl.BlockSpec((CONV_W, D_RNN), lambda g: (0, 0)),
        pl.BlockSpec((1, D_RNN), lambda g: (0, 0)),
        pl.BlockSpec((D_RNN, 2 * D_RNN), lambda g: (0, 0)),
        pl.BlockSpec((1, 2 * D_RNN), lambda g: (0, 0)),
        pl.BlockSpec((1, D_RNN), lambda g: (0, 0)),
        pl.BlockSpec((None, None, 1, D_RNN), lambda g: (geom.seq_id(blk(g)), d, 0, 0)),
    ]
    pos = np.arange(BLK)
    perm_np = np.zeros((BLK, BLK), np.float32)
    perm_np[pos, (pos % LRU_SUB) * (BLK // LRU_SUB) + pos // LRU_SUB] = 1.0
    in_specs.append(pl.BlockSpec((BLK, BLK), lambda g: (0, 0)))
    args = [proj, proj, proj, conv_w, conv_b, wg, bg, sp, h0, jnp.asarray(perm_np, BF16)]
    if reverse:
        in_specs += [pl.BlockSpec((BLK, BLK), lambda g: (0, 0)),
                     pl.BlockSpec((BLK, D_RNN), lambda g: (blk(g), C_GA // D_RNN)),
                     pl.BlockSpec((BLK, D_RNN), lambda g: (blk(g), 0))]
        args += [jnp.asarray(perm_np.T, BF16), proj, hf]
        out_dtype = BF16
    else:
        out_dtype = F32
    scratch = [pltpu.VMEM((1, D_RNN), F32)]
    return pl.pallas_call(
        functools.partial(_lru_kernel, geom, reverse),
        grid=(nb,),
        in_specs=in_specs,
        out_specs=[pl.BlockSpec((BLK, D_RNN), lambda g: (blk(g), 0)),
                   pl.BlockSpec((None, 1, D_RNN), lambda g: (blk(g), 0, 0))],
        out_shape=[jax.ShapeDtypeStruct((geom.n_tok, D_RNN), out_dtype),
                   jax.ShapeDtypeStruct((nb, 1, D_RNN), F32)],
        scratch_shapes=scratch,
        compiler_params=_cparams(("arbitrary",)),
        name="lru_bwd" if reverse else "lru_fwd",
    )(*args)


def _group_rms(x, w, ones):
    xx = x * x
    hi = xx.astype(BF16)
    lo = (xx - hi.astype(F32)).astype(BF16)
    ss = (jnp.dot(hi, ones, preferred_element_type=F32)
          + jnp.dot(lo, ones, preferred_element_type=F32))
    return x * lax.rsqrt(ss * (1.0 / DA_QK) + EPS) * w


def _rope(x, cos, sin):
    lane = lax.broadcasted_iota(jnp.int32, x.shape, 1)
    first = (lane % (2 * ROPE_PAIRS)) < ROPE_PAIRS
    w = x.shape[1]
    partner = jnp.where(first, pltpu.roll(x, w - ROPE_PAIRS, 1), pltpu.roll(x, ROPE_PAIRS, 1))
    return x * cos + partner * sin


def _prep_kernel(rope, *refs):
    if rope:
        dq_ref, dk_ref, qw_ref, kw_ref, ones_ref, cos_ref, sin_ref, q_out, k_out = refs
    else:
        dq_ref, dk_ref, qw_ref, kw_ref, ones_ref, q_out, k_out, kf_out = refs
    ones = ones_ref[...]
    q = _group_rms(dq_ref[...].astype(F32), qw_ref[...], ones)
    k = _group_rms(dk_ref[...].astype(F32), kw_ref[...], ones)
    if rope:
        cos = jnp.concatenate([cos_ref[...]] * 4, axis=1)
        sin = jnp.concatenate([sin_ref[...]] * 4, axis=1)
        q = _rope(q, cos, sin)
        k = _rope(k, cos, sin)
    else:
        kf_out[...] = k
    q_out[...] = (q * (DA_QK ** -0.5 * math.log2(math.e))).astype(BF16)
    k_out[...] = k.astype(BF16)


def _prep_call(geom, latent, proj, qw, kw, ones, cos=None, sin=None):
    tm = 512
    w = DA_HEADS * 2 * DA_QK
    if latent:
        n, off = geom.n_lat, geom.n_ctx // tm
        per = geom.dec_seq // tm
    else:
        n, off = geom.n_ctx, 0
    in_specs = [pl.BlockSpec((tm, w), lambda i: (i + off, C_DQ // w)),
                pl.BlockSpec((tm, w), lambda i: (i + off, C_DK // w)),
                pl.BlockSpec((1, w), lambda i: (0, 0)),
                pl.BlockSpec((1, w), lambda i: (0, 0)),
                pl.BlockSpec((w, w), lambda i: (0, 0))]
    args = [proj, proj, qw, kw, ones]
    out_specs = [pl.BlockSpec((tm, w), lambda i: (i, 0)), pl.BlockSpec((tm, w), lambda i: (i, 0))]
    out_shape = [jax.ShapeDtypeStruct((n, w), BF16), jax.ShapeDtypeStruct((n, w), BF16)]
    if latent:
        in_specs += [pl.BlockSpec((tm, 2 * DA_QK), lambda i: (i % per, 0)),
                     pl.BlockSpec((tm, 2 * DA_QK), lambda i: (i % per, 0))]
        args += [cos, sin]
    else:
        out_specs.append(pl.BlockSpec((tm, w), lambda i: (i, 0)))
        out_shape.append(jax.ShapeDtypeStruct((n, w), F32))
    return pl.pallas_call(
        functools.partial(_prep_kernel, latent),
        grid=(n // tm,),
        in_specs=in_specs, out_specs=out_specs, out_shape=out_shape,
        compiler_params=_cparams(("arbitrary",)),
        name="qk_prep_lat" if latent else "qk_prep_ctx",
    )(*args)


ATT_KC = 256
LOG2E = math.log2(math.e)
ATT_SAFE_LOGIT = 60.0


def _attn_kernel(out_scale, has_cache, *refs):
    if has_cache:
        par_ref, q_ref, kc_ref, vc_ref, kl_ref, vl_ref, sw_ref, o_ref, e_scr, o_scr = refs
        srcs = [(kc_ref, vc_ref), (kl_ref, vl_ref)]
    else:
        par_ref, q_ref, kl_ref, vl_ref, sw_ref, o_ref, e_scr, o_scr = refs
        srcs = [(kl_ref, vl_ref)]
    chunks = [(kr, vr, st) for kr, vr in srcs for st in range(0, kr.shape[0], ATT_KC)]
    lam = par_ref[0]
    no_shift = par_ref[1] > 0.5
    tq = q_ref.shape[0]
    q = q_ref[...]
    lane = lax.broadcasted_iota(jnp.int32, q.shape, 1)
    zero = jnp.zeros_like(q)
    qq = jnp.concatenate([jnp.where(lane < DA_QK, q, zero), jnp.where(lane >= DA_QK, q, zero)], axis=0)
    nt = (((1,), (1,)), ((), ()))
    half = ATT_KC // 2

    def logits(kr, st):
        return lax.dot_general(qq, kr[st:st + ATT_KC, :], nt, preferred_element_type=F32)

    def run(shift):
        m = None
        if shift:
            for c, (kr, vr, st) in enumerate(chunks):
                s = logits(kr, st)
                e_scr[c] = s
                mc = jnp.max(s, axis=-1, keepdims=True)
                m = mc if m is None else jnp.maximum(m, mc)
        lsum = None
        for c, (kr, vr, st) in enumerate(chunks):
            e = jnp.exp2(e_scr[c] - m) if shift else jnp.exp2(logits(kr, st))
            e_scr[c] = e
            part = e[:, :half] + e[:, half:]
            lsum = part if lsum is None else lsum + part
        l = jnp.sum(lsum, axis=-1, keepdims=True)
        l1 = l[0:tq]
        rho = lam * l1 / l[tq:2 * tq]
        acc = None
        for c, (kr, vr, st) in enumerate(chunks):
            w = (e_scr[c, 0:tq, :] - rho * e_scr[c, tq:2 * tq, :]).astype(BF16)
            t = jnp.dot(w, vr[st:st + ATT_KC, :], preferred_element_type=F32)
            acc = t if acc is None else acc + t
        o_scr[...] = acc / l1

    @pl.when(no_shift)
    def _():
        run(False)

    @pl.when(jnp.logical_not(no_shift))
    def _():
        run(True)

    o = o_scr[...]
    y = o * lax.rsqrt(jnp.mean(o * o, axis=-1, keepdims=True) + EPS) * sw_ref[...]
    o_ref[...] = (y * out_scale).astype(BF16)


def _attn_call(par, lam_init, q2d, k2d, proj, v_row_off, n_b, t_q, t_kl, tq, subln_w, cache=None):
    hw = 2 * DA_QK
    nq = t_q // tq
    vcol = C_DV // DA_V
    in_specs = [pl.BlockSpec(memory_space=pltpu.SMEM),
                pl.BlockSpec((tq, hw), lambda b, h, qi: (b * nq + qi, h))]
    args = [par, q2d]
    n_chunks = t_kl // ATT_KC
    if cache is not None:
        kc, vc = cache
        p = kc.shape[1]
        n_chunks += p // ATT_KC
        in_specs += [pl.BlockSpec((None, p, hw), lambda b, h, qi: (b, 0, h)),
                     pl.BlockSpec((None, p, DA_V), lambda b, h, qi: (b, 0, h))]
        args += [kc, vc]
    in_specs += [pl.BlockSpec((t_kl, hw), lambda b, h, qi: (b, h)),
                 pl.BlockSpec((t_kl, DA_V), lambda b, h, qi: (v_row_off + b, vcol + h)),
                 pl.BlockSpec((1, DA_V), lambda b, h, qi: (0, 0))]
    args += [k2d, proj, subln_w.reshape(1, DA_V)]
    return pl.pallas_call(
        functools.partial(_attn_kernel, 1.0 - lam_init, cache is not None),
        grid=(n_b, DA_HEADS, nq),
        in_specs=in_specs,
        out_specs=pl.BlockSpec((tq, DA_V), lambda b, h, qi: (b * nq + qi, h)),
        out_shape=jax.ShapeDtypeStruct((n_b * t_q, DA_HEADS * DA_V), BF16),
        scratch_shapes=[pltpu.VMEM((n_chunks, 2 * tq, ATT_KC), F32), pltpu.VMEM((tq, DA_V), F32)],
        compiler_params=_cparams(("arbitrary", "arbitrary", "arbitrary")),
        name="diff_attn_lat" if cache is not None else "diff_attn_ctx",
    )(*args)


def _ret_state_update(kt, v, kd, cd, s_old):
    parts = []
    for h in range(RET_HEADS):
        rows = slice(h * RET_QK, (h + 1) * RET_QK)
        kh = (kt[rows, :].astype(F32) * kd[rows, :]).astype(BF16)
        parts.append(jnp.dot(kh, v[:, h * RET_V:(h + 1) * RET_V], preferred_element_type=F32))
    return cd * s_old + jnp.concatenate(parts, axis=0)


def _ret_bwd_kernel(geom, kt_ref, v_ref, kd_ref, cd_ref, s0_ref, sstart_ref, send_ref, s_scr):
    i = geom.n_blocks - 1 - pl.program_id(0)

    @pl.when(geom.seq_end(i))
    def _():
        s_scr[...] = s0_ref[...]

    s_old = s_scr[...]
    sstart_ref[...] = s_old
    kt = kt_ref[...] * jnp.asarray(RET_QK ** -0.5, BF16)
    s_new = _ret_state_update(kt, v_ref[...], kd_ref[...], cd_ref[...], s_old)
    s_scr[...] = s_new
    send_ref[...] = s_new


def _ret_bwd_call(geom, proj, rkt, kd_b, cd_b, s0):
    nb = geom.n_blocks
    hs = RET_HEADS * RET_QK

    def blk(g):
        return nb - 1 - g

    return pl.pallas_call(
        functools.partial(_ret_bwd_kernel, geom),
        grid=(nb,),
        in_specs=[pl.BlockSpec((hs, BLK), lambda g: (0, blk(g))),
                  pl.BlockSpec((BLK, RET_HEADS * RET_V), lambda g: (blk(g), C_RV // (RET_HEADS * RET_V))),
                  pl.BlockSpec((hs, BLK), lambda g: (0, 0)),
                  pl.BlockSpec((hs, RET_V), lambda g: (0, 0)),
                  pl.BlockSpec((None, None, hs, RET_V), lambda g: (geom.seq_id(blk(g)), 1, 0, 0))],
        out_specs=[pl.BlockSpec((None, hs, RET_V), lambda g: (blk(g), 0, 0)),
                   pl.BlockSpec((None, hs, RET_V), lambda g: (blk(g), 0, 0))],
        out_shape=[jax.ShapeDtypeStruct((nb, hs, RET_V), F32),
                   jax.ShapeDtypeStruct((nb, hs, RET_V), F32)],
        scratch_shapes=[pltpu.VMEM((hs, RET_V), F32)],
        compiler_params=_cparams(("arbitrary",)),
        name="ret_bwd_state",
    )(rkt, proj, kd_b, cd_b, s0)


def _ret_main_kernel(geom, q_ref, kt_ref, v_ref, g_ref, dsum_ref, qdf_ref, qdb_ref, kd_ref, cd_ref,
                     s0_ref, sb_ref, o_ref, send_ref, s_scr):
    i = pl.program_id(0)

    @pl.when(geom.seq_start(i))
    def _():
        s_scr[...] = s0_ref[...]

    s_f = s_scr[...]
    s_fb = s_f.astype(BF16)
    s_bb = sb_ref[...].astype(BF16)
    q = q_ref[...].astype(F32)
    kt = kt_ref[...] * jnp.asarray(RET_QK ** -0.5, BF16)
    v = v_ref[...]
    lane = lax.broadcasted_iota(jnp.int32, q.shape, 1)
    for h in range(RET_HEADS):
        in_head = (lane >= h * RET_QK) & (lane < (h + 1) * RET_QK)
        qh = jnp.where(in_head, q, 0.0)
        vh = v[:, h * RET_V:(h + 1) * RET_V]
        sc = jnp.dot(qh.astype(BF16), kt, preferred_element_type=F32) * dsum_ref[h]
        o = jnp.dot(sc.astype(BF16), vh, preferred_element_type=F32)
        o += jnp.dot((qh * qdf_ref[...]).astype(BF16), s_fb, preferred_element_type=F32)
        o += jnp.dot((qh * qdb_ref[...]).astype(BF16), s_bb, preferred_element_type=F32)
        y = o * lax.rsqrt(jnp.mean(o * o, axis=-1, keepdims=True) + EPS)
        gv = g_ref[:, h * RET_V:(h + 1) * RET_V].astype(F32)
        o_ref[:, h * RET_V:(h + 1) * RET_V] = (y * (gv * jax.nn.sigmoid(gv))).astype(BF16)
    s_new = _ret_state_update(kt, v, kd_ref[...], cd_ref[...], s_f)
    s_scr[...] = s_new
    send_ref[...] = s_new


def _ret_main_call(geom, proj, rkt, dsum, qdf, qdb, kd_f, cd_f, s0, sb_start):
    nb = geom.n_blocks
    hs = RET_HEADS * RET_QK
    hv = RET_HEADS * RET_V
    return pl.pallas_call(
        functools.partial(_ret_main_kernel, geom),
        grid=(nb,),
        in_specs=[pl.BlockSpec((BLK, hs), lambda g: (g, C_RQ // hs)),
                  pl.BlockSpec((hs, BLK), lambda g: (0, g)),
                  pl.BlockSpec((BLK, hv), lambda g: (g, C_RV // hv)),
                  pl.BlockSpec((BLK, hv), lambda g: (g, C_RG // hv)),
                  pl.BlockSpec((RET_HEADS, BLK, BLK), lambda g: (0, 0, 0)),
                  pl.BlockSpec((BLK, hs), lambda g: (0, 0)),
                  pl.BlockSpec((BLK, hs), lambda g: (0, 0)),
                  pl.BlockSpec((hs, BLK), lambda g: (0, 0)),
                  pl.BlockSpec((hs, RET_V), lambda g: (0, 0)),
                  pl.BlockSpec((None, None, hs, RET_V), lambda g: (geom.seq_id(g), 0, 0, 0)),
                  pl.BlockSpec((None, hs, RET_V), lambda g: (g, 0, 0))],
        out_specs=[pl.BlockSpec((BLK, hv), lambda g: (g, 0)),
                   pl.BlockSpec((None, hs, RET_V), lambda g: (g, 0, 0))],
        out_shape=[jax.ShapeDtypeStruct((geom.n_tok, hv), BF16),
                   jax.ShapeDtypeStruct((nb, hs, RET_V), F32)],
        scratch_shapes=[pltpu.VMEM((hs, RET_V), F32)],
        compiler_params=_cparams(("arbitrary",)),
        name="ret_main",
    )(proj, rkt, proj, proj, dsum, qdf, qdb, kd_f, cd_f, s0, sb_start)


def _ret_tables(ret_decay_l):
    log_g = jax.nn.log_sigmoid(ret_decay_l.astype(F32))
    pos = jnp.arange(BLK, dtype=F32)
    diff = pos[:, None] - pos[None, :]
    lf = log_g[0][:, None, None]
    lb = log_g[1][:, None, None]
    dsum = (jnp.where(diff >= 0, jnp.exp(jnp.maximum(diff, 0.0)[None] * lf), 0.0)
            + jnp.where(diff <= 0, jnp.exp(jnp.maximum(-diff, 0.0)[None] * lb), 0.0))

    def per_lane(e, lg):
        return jnp.repeat(jnp.exp(e[:, None] * lg[None, :]), RET_QK, axis=1)

    qdf = per_lane(pos + 1.0, log_g[0])
    qdb = per_lane(BLK - pos, log_g[1])
    kd_f = per_lane(BLK - 1.0 - pos, log_g[0]).T
    kd_b = per_lane(pos, log_g[1]).T
    cd_f = jnp.broadcast_to(jnp.repeat(jnp.exp(BLK * log_g[0]), RET_QK)[:, None], (RET_HEADS * RET_QK, RET_V))
    cd_b = jnp.broadcast_to(jnp.repeat(jnp.exp(BLK * log_g[1]), RET_QK)[:, None], (RET_HEADS * RET_QK, RET_V))
    return dsum, qdf, qdb, kd_f, kd_b, cd_f, cd_b


def _merge_kernel(ba_ref, bb_ref, bc_ref, g0_ref, g1_ref, g2_ref, x_ref, gate_ref, sc_ref, sh_ref,
                  nw_ref, wb_ref, wo_ref, rhi_ref, rlo_ref, x1_ref, h2_ref, lt_ref):
    acc = None
    for br, (b_ref, g_ref) in enumerate(((ba_ref, g0_ref), (bb_ref, g1_ref), (bc_ref, g2_ref))):
        p = jnp.dot(b_ref[...], wb_ref[br], preferred_element_type=F32)
        t = jax.nn.sigmoid(g_ref[...].astype(F32)) * p
        acc = t if acc is None else acc + t
    m = jnp.dot(acc.astype(BF16), wo_ref[...], preferred_element_type=F32)
    x1 = x_ref[...] + gate_ref[...] * m
    x1_ref[...] = x1
    ms = jnp.mean(x1 * x1, axis=-1, keepdims=True)
    h2 = x1 * lax.rsqrt(ms + EPS) * nw_ref[...] * (1.0 + sc_ref[...]) + sh_ref[...]
    h2b = h2.astype(BF16)
    h2_ref[...] = h2b
    h2lo = (h2 - h2b.astype(F32)).astype(BF16)
    nt = (((1,), (1,)), ((), ()))
    lt_ref[...] = (lax.dot_general(rhi_ref[...], h2b, nt, preferred_element_type=F32)
                   + lax.dot_general(rhi_ref[...], h2lo, nt, preferred_element_type=F32)
                   + lax.dot_general(rlo_ref[...], h2b, nt, preferred_element_type=F32))


def _merge_call(geom, l, ba, bb, bc, proj, x, mod6, norm2_w, wb_bf, wo_bf, r_hi, r_lo):
    tm = 512
    gcol = C_GL // D_MODEL
    full = lambda shape: pl.BlockSpec(shape, lambda i: tuple(0 for _ in shape))
    tok = lambda w: pl.BlockSpec((tm, w), lambda i: (i, 0))
    return pl.pallas_call(
        _merge_kernel,
        grid=(geom.n_tok // tm,),
        in_specs=[tok(BRANCH_W), tok(BRANCH_W), tok(BRANCH_W),
                  pl.BlockSpec((tm, D_MODEL), lambda i: (i, gcol)),
                  pl.BlockSpec((tm, D_MODEL), lambda i: (i, gcol + 1)),
                  pl.BlockSpec((tm, D_MODEL), lambda i: (i, gcol + 2)),
                  tok(D_MODEL),
                  _mod_spec(geom, l, 2, tm, 1), _mod_spec(geom, l, 4, tm, 1), _mod_spec(geom, l, 3, tm, 1),
                  full((1, D_MODEL)),
                  full((N_BRANCH, BRANCH_W, D_MODEL)), full((D_MODEL, D_MODEL)),
                  full((N_EXPERTS, D_MODEL)), full((N_EXPERTS, D_MODEL))],
        out_specs=[tok(D_MODEL), tok(D_MODEL), pl.BlockSpec((N_EXPERTS, tm), lambda i: (0, i))],
        out_shape=[jax.ShapeDtypeStruct((geom.n_tok, D_MODEL), F32),
                   jax.ShapeDtypeStruct((geom.n_tok, D_MODEL), BF16),
                   jax.ShapeDtypeStruct((N_EXPERTS, geom.n_tok), F32)],
        compiler_params=_cparams(("arbitrary",)),
        name="merge_out",
    )(ba, bb, bc, proj, proj, proj, x, mod6, mod6, mod6, norm2_w.reshape(1, D_MODEL),
      wb_bf, wo_bf, r_hi, r_lo)


def _router_kernel(lt_ref, bias_ref, g_ref):
    per = N_EXPERTS // N_GROUPS
    tm = lt_ref.shape[1]
    scores = jax.nn.sigmoid(lt_ref[...])
    biased = scores + bias_ref[...]
    b3 = biased.reshape(N_GROUPS, per, tm)
    neg = jnp.float32(-jnp.inf)
    m1 = jnp.max(b3, axis=1, keepdims=True)
    is_m1 = b3 == m1
    cnt = jnp.sum(is_m1.astype(F32), axis=1, keepdims=True)
    m2 = jnp.max(jnp.where(is_m1, neg, b3), axis=1, keepdims=True)
    grp = (m1 + jnp.where(cnt >= 2.0, m1, m2)).reshape(N_GROUPS, tm)
    gidx = lax.broadcasted_iota(jnp.int32, (N_GROUPS, tm), 0)
    grank = jnp.zeros((N_GROUPS, tm), F32)
    for g2 in range(N_GROUPS):
        other = grp[g2:g2 + 1, :]
        ahead = (other > grp) | ((other == grp) & (gidx > g2))
        grank += ahead.astype(F32)
    gsel = (grank < float(TOPK_GROUPS)).astype(F32)
    emask = jnp.broadcast_to(gsel.reshape(N_GROUPS, 1, tm), (N_GROUPS, per, tm)).reshape(N_EXPERTS, tm)
    masked = jnp.where(emask > 0.0, biased, neg)
    eidx = lax.broadcasted_iota(jnp.int32, (N_EXPERTS, tm), 0)
    erank = jnp.zeros((N_EXPERTS, tm), F32)
    for e2 in range(N_EXPERTS):
        other = masked[e2:e2 + 1, :]
        ahead = (other > masked) | ((other == masked) & (eidx > e2))
        erank += ahead.astype(F32)
    w = jnp.where(erank < float(TOP_K), scores, 0.0)
    g_ref[...] = w / jnp.sum(w, axis=0, keepdims=True) * ROUTED_SCALE


def _router_call(geom, logits_t, bias):
    tm = 512
    return pl.pallas_call(
        _router_kernel,
        grid=(geom.n_tok // tm,),
        in_specs=[pl.BlockSpec((N_EXPERTS, tm), lambda i: (0, i)),
                  pl.BlockSpec((N_EXPERTS, 1), lambda i: (0, 0))],
        out_specs=pl.BlockSpec((N_EXPERTS, tm), lambda i: (0, i)),
        out_shape=jax.ShapeDtypeStruct((N_EXPERTS, geom.n_tok), F32),
        compiler_params=_cparams(("arbitrary",)),
        name="router",
    )(logits_t, bias.reshape(N_EXPERTS, 1))


MOE_EB = 5


def _moe_kernel(h_ref, g_ref, gu_ref, dn_ref, x1_ref, gate_ref, o_ref, acc_scr):
    j = pl.program_id(1)

    @pl.when(j == 0)
    def _():
        acc_scr[...] = jnp.zeros_like(acc_scr)

    h = h_ref[...]
    gts = g_ref[...]
    lane = lax.broadcasted_iota(jnp.int32, gts.shape, 1)
    acc = acc_scr[...]
    for e in range(MOE_EB):
        a = jnp.dot(h, gu_ref[e], preferred_element_type=F32)
        hg = a[:, :D_EXPERT]
        hu = a[:, D_EXPERT:]
        ge = jnp.sum(jnp.where(lane == j * MOE_EB + e, gts, 0.0), axis=1, keepdims=True)
        act = (hg * jax.nn.sigmoid(hg)) * hu * ge
        acc += jnp.dot(act.astype(BF16), dn_ref[e], preferred_element_type=F32)
    acc_scr[...] = acc

    @pl.when(j == pl.num_programs(1) - 1)
    def _():
        o_ref[...] = x1_ref[...] + gate_ref[...] * acc


def _moe_call(geom, l, h2, gates, gu_bf, dn_bf, x1, mod6):
    tm = 512
    ne = gu_bf.shape[0]
    return pl.pallas_call(
        _moe_kernel,
        grid=(geom.n_tok // tm, ne // MOE_EB),
        in_specs=[pl.BlockSpec((tm, D_MODEL), lambda i, j: (i, 0)),
                  pl.BlockSpec((tm, 128), lambda i, j: (i, 0)),
                  pl.BlockSpec((MOE_EB, D_MODEL, 2 * D_EXPERT), lambda i, j: (j, 0, 0)),
                  pl.BlockSpec((MOE_EB, D_EXPERT, D_MODEL), lambda i, j: (j, 0, 0)),
                  pl.BlockSpec((tm, D_MODEL), lambda i, j: (i, 0)),
                  _mod_spec(geom, l, 5, tm, 2)],
        out_specs=pl.BlockSpec((tm, D_MODEL), lambda i, j: (i, 0)),
        out_shape=jax.ShapeDtypeStruct((geom.n_tok, D_MODEL), F32),
        scratch_shapes=[pltpu.VMEM((tm, D_MODEL), F32)],
        compiler_params=_cparams(("arbitrary", "arbitrary")),
        name="moe_experts",
    )(h2, gates, gu_bf, dn_bf, x1, mod6)


def _rope_tables(dec_seq):
    rows = dec_seq // GRID_W
    row = jnp.repeat(jnp.arange(rows, dtype=F32), GRID_W)
    col = jnp.tile(jnp.arange(GRID_W, dtype=F32), rows)
    inv = ROPE_BASE ** (-jnp.arange(ROPE_PAIRS, dtype=F32) / ROPE_PAIRS)
    ar = row[:, None] * inv[None, :]
    ac = col[:, None] * inv[None, :]
    cos64 = jnp.concatenate([jnp.cos(ar), jnp.cos(ar), jnp.cos(ac), jnp.cos(ac)], axis=1)
    sin64 = jnp.concatenate([-jnp.sin(ar), jnp.sin(ar), -jnp.sin(ac), jnp.sin(ac)], axis=1)
    return jnp.tile(cos64, (1, 2)), jnp.tile(sin64, (1, 2))


def _block_diag_gate(wg_dir):
    eye = jnp.eye(LRU_BLOCKS, dtype=F32)
    dense = jnp.einsum('gnij,nm->gnimj', wg_dir.astype(F32), eye).reshape(2, D_RNN, D_RNN)
    return jnp.concatenate([dense[0], dense[1]], axis=1)


def kernel(x_prompt, x_sample, cache_k, cache_v, state_lru, state_ret, c, c_ctx, ada_w, ada_b, norm1_w, norm2_w, w_in, conv_w, conv_b, lru_gate_w, lru_gate_b, lru_lambda, q_norm_w, k_norm_w, diff_lambda, subln_w, ret_decay, w_branch, w_out, router_w, router_bias, w_exp_gu, w_exp_down, w_sh_gu, w_sh_down):
    batch, seq, _ = x_prompt.shape
    dec_batch, dec_seq, _ = x_sample.shape
    assert 1 + dec_batch <= MOD_ROWS
    geom = _Geom(batch, seq, dec_batch, dec_seq)
    hs = RET_HEADS * RET_QK
    aw = DA_HEADS * 2 * DA_QK

    x = jnp.concatenate([x_prompt.reshape(geom.n_ctx, D_MODEL), x_sample.reshape(geom.n_lat, D_MODEL)], axis=0)
    cvec = jnp.zeros((MOD_ROWS, D_MODEL), F32).at[0].set(c_ctx).at[1:1 + dec_batch].set(c)
    mod6 = _ada_call(cvec, ada_w, ada_b).reshape(DEPTH, MOD_ROWS, 6, 1, D_MODEL)

    ones_bd = jnp.kron(jnp.eye(aw // DA_QK, dtype=F32), jnp.ones((DA_QK, DA_QK), F32)).astype(BF16)
    cos_t, sin_t = _rope_tables(dec_seq)

    ks, vs, lrus, rets = [], [], [], []
    for l in range(DEPTH):
        lam_init = 0.8 - 0.6 * math.exp(-0.3 * l)
        w_in_bf = w_in[l].astype(BF16)
        w_rkt_bf = w_in[l][:, C_RK:C_RK + hs].T.astype(BF16)
        proj, rkt = _inproj_call(geom, l, x, mod6, norm1_w[l], w_in_bf, w_rkt_bf)

        sp = jax.nn.softplus(-lru_lambda[l].astype(F32))
        h0 = jnp.concatenate([jnp.zeros((batch, 2, D_RNN), F32), state_lru[:, l].astype(F32)], axis=0)
        h0 = h0.reshape(geom.n_seq, 2, 1, D_RNN)
        cb = conv_b[l].reshape(1, D_RNN)
        lru_args = []
        for d in range(2):
            lru_args.append((_block_diag_gate(lru_gate_w[l, d]).astype(BF16),
                             lru_gate_b[l, d].reshape(1, 2 * D_RNN), sp[d].reshape(1, D_RNN)))
        hf, hf_last = _lru_call(geom, False, proj, conv_w[l], cb, *lru_args[0], h0)
        branch_a, hb_last = _lru_call(geom, True, proj, conv_w[l], cb, *lru_args[1], h0, hf)

        qw = jnp.tile(q_norm_w[l], aw // DA_QK).reshape(1, aw)
        kw = jnp.tile(k_norm_w[l], aw // DA_QK).reshape(1, aw)
        q_c, k_c, k_c32 = _prep_call(geom, False, proj, qw, kw, ones_bd)
        q_l, k_l = _prep_call(geom, True, proj, qw, kw, ones_bd, cos_t, sin_t)
        lam_p = diff_lambda[l].astype(F32)
        lam = jnp.exp(jnp.sum(lam_p[0] * lam_p[1])) - jnp.exp(jnp.sum(lam_p[2] * lam_p[3])) + lam_init
        q_bound = DA_QK * jnp.max(jnp.square(q_norm_w[l].astype(F32))) * (DA_QK ** -0.5 * LOG2E) ** 2
        k_bound = DA_QK * jnp.max(jnp.square(k_norm_w[l].astype(F32)))
        kc32 = cache_k[:, l].astype(F32)
        kc_bound = jnp.maximum(k_bound, jnp.max(jnp.sum(jnp.square(kc32), axis=-1)))

        def attn_par(kb):
            ok = (q_bound * kb * 1.05 < ATT_SAFE_LOGIT ** 2).astype(F32)
            return jnp.stack([lam, ok])

        assert geom.n_ctx % dec_seq == 0
        v_c = proj[:geom.n_ctx, C_DV:C_DV + DA_HEADS * DA_V].reshape(batch, seq, -1)
        cache = (kc32.reshape(dec_batch, -1, aw).astype(BF16),
                 cache_v[:, l].reshape(dec_batch, -1, DA_HEADS * DA_V).astype(BF16))
        att_c = _attn_call(attn_par(k_bound), lam_init, q_c, k_c, proj, 0, batch, seq, seq, 256, subln_w[l])
        att_l = _attn_call(attn_par(kc_bound), lam_init, q_l, k_l, proj, geom.n_ctx // dec_seq, dec_batch,
                           dec_seq, dec_seq, 256, subln_w[l], cache)
        branch_b = jnp.concatenate([att_c, att_l], axis=0)

        dsum, qdf, qdb, kd_f, kd_b, cd_f, cd_b = _ret_tables(ret_decay[l])
        s0 = jnp.concatenate([jnp.zeros((batch, 2, hs, RET_V), F32),
                              state_ret[:, l].astype(F32).reshape(dec_batch, 2, hs, RET_V)], axis=0)
        sb_start, sb_end = _ret_bwd_call(geom, proj, rkt, kd_b, cd_b, s0)
        branch_c, sf_end = _ret_main_call(geom, proj, rkt, dsum, qdf, qdb, kd_f, cd_f, s0, sb_start)

        r_t = router_w[l].T.astype(F32)
        r_hi = r_t.astype(BF16)
        r_lo = (r_t - r_hi.astype(F32)).astype(BF16)
        x1, h2, logits_t = _merge_call(geom, l, branch_a, branch_b, branch_c, proj, x, mod6, norm2_w[l],
                                       w_branch[l].astype(BF16), w_out[l].astype(BF16), r_hi, r_lo)
        gates_t = _router_call(geom, logits_t, router_bias[l].astype(F32))
        gates = jnp.concatenate([gates_t.T, jnp.ones((geom.n_tok, 1), F32),
                                 jnp.zeros((geom.n_tok, 128 - N_EXPERTS - 1), F32)], axis=1)
        gu_bf = jnp.concatenate([w_exp_gu[l], w_sh_gu[l][None]], axis=0).astype(BF16)
        dn_bf = jnp.concatenate([w_exp_down[l], w_sh_down[l][None]], axis=0).astype(BF16)
        x = _moe_call(geom, l, h2, gates, gu_bf, dn_bf, x1, mod6)

        ks.append(k_c32.reshape(batch, seq, DA_HEADS, 2, DA_QK))
        vs.append(v_c.astype(F32).reshape(batch, seq, DA_HEADS, DA_V))
        lrus.append(jnp.stack([hf_last[:batch, 0], hb_last[:batch, 0]], axis=1))
        rets.append(jnp.stack([sf_end[:batch].reshape(batch, RET_HEADS, RET_QK, RET_V),
                               sb_end[:batch].reshape(batch, RET_HEADS, RET_QK, RET_V)], axis=1))

    y_prompt = x[:geom.n_ctx].reshape(batch, seq, D_MODEL)
    y_sample = x[geom.n_ctx:].reshape(dec_batch, dec_seq, D_MODEL)
    return (y_prompt, y_sample, jnp.stack(ks, axis=1), jnp.stack(vs, axis=1),
            jnp.stack(lrus, axis=1), jnp.stack(rets, axis=1))
```

```python
import functools
import math

import numpy as np
import jax
import jax.numpy as jnp
from jax import lax
from jax.experimental import pallas as pl
from jax.experimental.pallas import tpu as pltpu

F32 = jnp.float32
BF16 = jnp.bfloat16

D_MODEL = 1024
DEPTH = 2
GRID_W = 64
D_RNN = 512
LRU_BLOCKS = 8
LRU_BLOCK = D_RNN // LRU_BLOCKS
CONV_W = 4
LRU_C = 8.0
DA_HEADS = 4
DA_QK = 64
DA_V = 128
ROPE_PAIRS = DA_QK // 4
ROPE_BASE = 10000.0
RET_HEADS = 4
RET_QK = 64
RET_V = 128
BRANCH_W = 512
N_BRANCH = 3
D_IN = 7168
N_EXPERTS = 64
TOP_K = 8
N_GROUPS = 8
TOPK_GROUPS = 4
D_EXPERT = 256
ROUTED_SCALE = 2.5
EPS = 1e-6

C_XA, C_GA, C_DQ, C_DK, C_DV = 0, 512, 1024, 1536, 2048
C_RQ, C_RK, C_RV, C_RG, C_GL = 2560, 2816, 3072, 3584, 4096

BLK = 256
LRU_SUB = 8
LRU_LANES = D_RNN // 128
GATE_W = 128
MOD_ROWS = 8
VMEM_LIMIT = 56 * 1024 * 1024


def _cparams(sem):
    return pltpu.CompilerParams(dimension_semantics=sem, vmem_limit_bytes=VMEM_LIMIT)


class _Geom:
    def __init__(self, batch, seq, dec_batch, dec_seq):
        assert seq == BLK and dec_seq % BLK == 0
        self.batch, self.seq, self.dec_batch, self.dec_seq = batch, seq, dec_batch, dec_seq
        self.n_ctx = batch * seq
        self.n_lat = dec_batch * dec_seq
        self.n_tok = self.n_ctx + self.n_lat
        self.ctx_blocks = self.n_ctx // BLK
        self.lat_blocks = dec_seq // BLK
        self.n_blocks = self.n_tok // BLK
        self.n_seq = batch + dec_batch

    def mod_row(self, i, tile):
        nct = self.n_ctx // tile
        per = self.dec_seq // tile
        return jnp.where(i < nct, 0, 1 + (i - nct) // per)

    def seq_id(self, i):
        return jnp.where(i < self.ctx_blocks, i, self.ctx_blocks + (i - self.ctx_blocks) // self.lat_blocks)

    def seq_start(self, i):
        return jnp.logical_or(i < self.ctx_blocks, (i - self.ctx_blocks) % self.lat_blocks == 0)

    def seq_end(self, i):
        return jnp.logical_or(i < self.ctx_blocks, (i - self.ctx_blocks) % self.lat_blocks == self.lat_blocks - 1)


def _ada_kernel(c_ref, w_ref, b_ref, o_ref):
    cv = c_ref[...]
    s = cv * jax.nn.sigmoid(cv)
    o_ref[...] = jnp.dot(s, w_ref[...], preferred_element_type=F32,
                         precision=lax.Precision.HIGHEST) + b_ref[...]


def _ada_call(cvec, ada_w, ada_b):
    depth = ada_w.shape[0]
    nt = 6
    return pl.pallas_call(
        _ada_kernel,
        grid=(depth, nt),
        in_specs=[pl.BlockSpec((MOD_ROWS, D_MODEL), lambda l, j: (0, 0)),
                  pl.BlockSpec((None, D_MODEL, D_MODEL), lambda l, j: (l, 0, j)),
                  pl.BlockSpec((None, 1, D_MODEL), lambda l, j: (l, 0, j))],
        out_specs=pl.BlockSpec((None, MOD_ROWS, D_MODEL), lambda l, j: (l, 0, j)),
        out_shape=jax.ShapeDtypeStruct((depth, MOD_ROWS, 6 * D_MODEL), F32),
        compiler_params=_cparams(("arbitrary", "arbitrary")),
        name="ada_mod",
    )(cvec, ada_w, ada_b.reshape(depth, 1, 6 * D_MODEL))


def _mod_spec(geom, l, which, tile, ngrid):
    if ngrid == 1:
        return pl.BlockSpec((None, None, None, 1, D_MODEL),
                            lambda i: (l, geom.mod_row(i, tile), which, 0, 0))
    return pl.BlockSpec((None, None, None, 1, D_MODEL),
                        lambda i, j: (l, geom.mod_row(i, tile), which, 0, 0))


def _split_in_specs(geom, tile, width, ngrid):
    nct = geom.n_ctx // tile
    if ngrid == 1:
        return [pl.BlockSpec((tile, width), lambda i: (jnp.minimum(i, nct - 1), 0)),
                pl.BlockSpec((tile, width), lambda i: (jnp.maximum(i - nct, 0), 0))]
    return [pl.BlockSpec((tile, width), lambda i, j: (jnp.minimum(i, nct - 1), 0)),
            pl.BlockSpec((tile, width), lambda i, j: (jnp.maximum(i - nct, 0), 0))]


def _pick_part(n_ctx_tiles, c_ref, l_ref):
    return jnp.where(pl.program_id(0) < n_ctx_tiles, c_ref[...], l_ref[...])


def _inproj_kernel(n_ctx_tiles, xc_ref, xl_ref, sc_ref, sh_ref, nw_ref, w_ref, wkt_ref, o_ref, kt_ref, h_scr):
    @pl.when(pl.program_id(1) == 0)
    def _():
        x = _pick_part(n_ctx_tiles, xc_ref, xl_ref)
        ms = jnp.mean(x * x, axis=-1, keepdims=True)
        y = x * lax.rsqrt(ms + EPS) * nw_ref[...]
        hb = (y * (1.0 + sc_ref[...]) + sh_ref[...]).astype(BF16)
        h_scr[...] = hb
        kt_ref[...] = lax.dot_general(wkt_ref[...], hb, (((1,), (1,)), ((), ())),
                                      preferred_element_type=F32).astype(BF16)

    o_ref[...] = jnp.dot(h_scr[...], w_ref[...], preferred_element_type=F32).astype(BF16)


def _inproj_call(geom, l, x_ctx, x_lat, mod6, norm_w, w_in_bf, w_rkt_bf):
    tm, tn = 1024, 1024
    grid = (geom.n_tok // tm, D_IN // tn)
    return pl.pallas_call(
        functools.partial(_inproj_kernel, geom.n_ctx // tm),
        grid=grid,
        in_specs=_split_in_specs(geom, tm, D_MODEL, 2) + [
                  _mod_spec(geom, l, 1, tm, 2),
                  _mod_spec(geom, l, 0, tm, 2),
                  pl.BlockSpec((1, D_MODEL), lambda i, j: (0, 0)),
                  pl.BlockSpec((D_MODEL, tn), lambda i, j: (0, j)),
                  pl.BlockSpec((RET_HEADS * RET_QK, D_MODEL), lambda i, j: (0, 0))],
        out_specs=[pl.BlockSpec((tm, tn), lambda i, j: (i, j)),
                   pl.BlockSpec((RET_HEADS * RET_QK, tm), lambda i, j: (0, i))],
        out_shape=[jax.ShapeDtypeStruct((geom.n_tok, D_IN), BF16),
                   jax.ShapeDtypeStruct((RET_HEADS * RET_QK, geom.n_tok), BF16)],
        scratch_shapes=[pltpu.VMEM((tm, D_MODEL), BF16)],
        compiler_params=_cparams(("arbitrary", "arbitrary")),
        name="inproj",
    )(x_ctx, x_lat, mod6, mod6, norm_w.reshape(1, D_MODEL), w_in_bf, w_rkt_bf)


def _gelu_tanh(x):
    return 0.5 * x * (1.0 + jnp.tanh(math.sqrt(2.0 / math.pi) * (x + 0.044715 * (x * x * x))))


def _lru_kernel(geom, reverse, *refs):
    if reverse:
        (xa_ref, xp_ref, xn_ref, cw_ref, cb_ref, wg_ref, bg_ref, sp_ref, h0_ref, perm_ref, permt_ref,
         ga_ref, hf_ref, out_ref, hl_ref, c_scr) = refs
    else:
        (xa_ref, xp_ref, xn_ref, cw_ref, cb_ref, wg_ref, bg_ref, sp_ref, h0_ref, perm_ref,
         out_ref, hl_ref, c_scr) = refs
    g = pl.program_id(0)
    i = geom.n_blocks - 1 - g if reverse else g
    start = geom.seq_start(i)
    end = geom.seq_end(i)

    @pl.when(end if reverse else start)
    def _():
        c_scr[...] = h0_ref[...]

    sub_len = BLK // LRU_SUB
    perm = perm_ref[...]
    x = jnp.dot(perm, xa_ref[...], preferred_element_type=F32)
    pm = jnp.where(start, 0.0, 1.0)
    nm = jnp.where(end, 0.0, 1.0)
    hp = xp_ref.shape[0]
    p1 = xp_ref[hp - 1:hp, :].astype(F32) * pm
    p2 = xp_ref[hp - 2:hp - 1, :].astype(F32) * pm
    n0 = xn_ref[0:1, :].astype(F32) * nm
    row = lax.broadcasted_iota(jnp.int32, x.shape, 0)
    xm1 = jnp.where(row < LRU_SUB, pltpu.roll(x, LRU_SUB + 1, 0), pltpu.roll(x, LRU_SUB, 0))
    xm1 = jnp.where(row == 0, p1, xm1)
    xm2 = jnp.where(row < 2 * LRU_SUB, pltpu.roll(x, 2 * LRU_SUB + 1, 0), pltpu.roll(x, 2 * LRU_SUB, 0))
    xm2 = jnp.where(row == 0, p2, jnp.where(row == LRU_SUB, p1, xm2))
    xp1 = jnp.where(row >= BLK - LRU_SUB, pltpu.roll(x, BLK - LRU_SUB - 1, 0),
                    pltpu.roll(x, BLK - LRU_SUB, 0))
    xp1 = jnp.where(row == BLK - 1, n0, xp1)
    xc = (cw_ref[0:1, :] * xm2 + cw_ref[1:2, :] * xm1 + cw_ref[2:3, :] * x
          + cw_ref[3:4, :] * xp1 + cb_ref[...])

    gt = jnp.dot(xc.astype(BF16), wg_ref[...], preferred_element_type=F32) + bg_ref[...]
    r = jax.nn.sigmoid(gt[:, :D_RNN])
    ig = jax.nn.sigmoid(gt[:, D_RNN:])
    a = jnp.exp(-LRU_C * r * sp_ref[...])
    u = jnp.sqrt(1.0 - a * a) * ig * xc

    h = jnp.zeros((LRU_SUB, D_RNN), F32)
    p = jnp.ones((LRU_SUB, D_RNN), F32)
    h_loc = [None] * sub_len
    p_loc = [None] * sub_len
    for t in (range(sub_len - 1, -1, -1) if reverse else range(sub_len)):
        a_t = a[t * LRU_SUB:(t + 1) * LRU_SUB, :]
        h = a_t * h + u[t * LRU_SUB:(t + 1) * LRU_SUB, :]
        p = a_t * p
        h_loc[t] = h
        p_loc[t] = p
    h_in = [None] * LRU_SUB
    state = c_scr[...]
    for k in (range(LRU_SUB - 1, -1, -1) if reverse else range(LRU_SUB)):
        h_in[k] = state
        state = h[k:k + 1, :] + p[k:k + 1, :] * state
    c_scr[...] = state
    hl_ref[...] = state
    h_in = jnp.concatenate(h_in, axis=0)
    h_full = jnp.concatenate([h_loc[t] + p_loc[t] * h_in for t in range(sub_len)], axis=0)
    if reverse:
        gv = jnp.dot(perm, ga_ref[...], preferred_element_type=F32)
        y = (_gelu_tanh(gv) * (hf_ref[...] + h_full)).astype(BF16)
        out_ref[...] = jnp.dot(permt_ref[...], y, preferred_element_type=F32).astype(BF16)
    else:
        out_ref[...] = h_full


def _lru_call(geom, reverse, proj, conv_w, conv_b, wg, bg, sp, h0, hf=None):
    nb = geom.n_blocks
    halo = 16
    hpb = BLK // halo

    def blk(g):
        return nb - 1 - g if reverse else g

    d = 1 if reverse else 0
    in_specs = [
        pl.BlockSpec((BLK, D_RNN), lambda g: (blk(g), C_XA // D_RNN)),
        pl.BlockSpec((halo, D_RNN), lambda g: (jnp.maximum(blk(g) * hpb - 1, 0), C_XA // D_RNN)),
        pl.BlockSpec((halo, D_RNN), lambda g: (jnp.minimum((blk(g) + 1) * hpb, nb * hpb - 1), C_XA // D_RNN)),
        pl.BlockSpec((CONV_W, D_RNN), lambda g: (0, 0)),
        pl.BlockSpec((1, D_RNN), lambda g: (0, 0)),
        pl.BlockSpec((D_RNN, 2 * D_RNN), lambda g: (0, 0)),
        pl.BlockSpec((1, 2 * D_RNN), lambda g: (0, 0)),
        pl.BlockSpec((1, D_RNN), lambda g: (0, 0)),
        pl.BlockSpec((None, None, 1, D_RNN), lambda g: (geom.seq_id(blk(g)), d, 0, 0)),
    ]
    pos = np.arange(BLK)
    perm_np = np.zeros((BLK, BLK), np.float32)
    perm_np[pos, (pos % LRU_SUB) * (BLK // LRU_SUB) + pos // LRU_SUB] = 1.0
    in_specs.append(pl.BlockSpec((BLK, BLK), lambda g: (0, 0)))
    args = [proj, proj, proj, conv_w, conv_b, wg, bg, sp, h0, jnp.asarray(perm_np, BF16)]
    if reverse:
        in_specs += [pl.BlockSpec((BLK, BLK), lambda g: (0, 0)),
                     pl.BlockSpec((BLK, D_RNN), lambda g: (blk(g), C_GA // D_RNN)),
                     pl.BlockSpec((BLK, D_RNN), lambda g: (blk(g), 0))]
        args += [jnp.asarray(perm_np.T, BF16), proj, hf]
        out_dtype = BF16
    else:
        out_dtype = F32
    scratch = [pltpu.VMEM((1, D_RNN), F32)]
    return pl.pallas_call(
        functools.partial(_lru_kernel, geom, reverse),
        grid=(nb,),
        in_specs=in_specs,
        out_specs=[pl.BlockSpec((BLK, D_RNN), lambda g: (blk(g), 0)),
                   pl.BlockSpec((None, 1, D_RNN), lambda g: (blk(g), 0, 0))],
        out_shape=[jax.ShapeDtypeStruct((geom.n_tok, D_RNN), out_dtype),
                   jax.ShapeDtypeStruct((nb, 1, D_RNN), F32)],
        scratch_shapes=scratch,
        compiler_params=_cparams(("arbitrary",)),
        name="lru_bwd" if reverse else "lru_fwd",
    )(*args)


def _group_rms(x, w, ones):
    xx = x * x
    hi = xx.astype(BF16)
    lo = (xx - hi.astype(F32)).astype(BF16)
    ss = (jnp.dot(hi, ones, preferred_element_type=F32)
          + jnp.dot(lo, ones, preferred_element_type=F32))
    return x * lax.rsqrt(ss * (1.0 / DA_QK) + EPS) * w


def _rope(x, cos, sin):
    lane = lax.broadcasted_iota(jnp.int32, x.shape, 1)
    first = (lane % (2 * ROPE_PAIRS)) < ROPE_PAIRS
    w = x.shape[1]
    partner = jnp.where(first, pltpu.roll(x, w - ROPE_PAIRS, 1), pltpu.roll(x, ROPE_PAIRS, 1))
    return x * cos + partner * sin


def _prep_kernel(rope, *refs):
    if rope:
        dq_ref, dk_ref, qw_ref, kw_ref, ones_ref, cos_ref, sin_ref, q_out, k_out = refs
    else:
        dq_ref, dk_ref, qw_ref, kw_ref, ones_ref, dv_ref, q_out, k_out, kf_out, vf_out = refs
        vf_out[...] = dv_ref[...].astype(F32)
    ones = ones_ref[...]
    q = _group_rms(dq_ref[...].astype(F32), qw_ref[...], ones)
    k = _group_rms(dk_ref[...].astype(F32), kw_ref[...], ones)
    if rope:
        cos = jnp.concatenate([cos_ref[...]] * 4, axis=1)
        sin = jnp.concatenate([sin_ref[...]] * 4, axis=1)
        q = _rope(q, cos, sin)
        k = _rope(k, cos, sin)
    else:
        kf_out[...] = k
    q_out[...] = (q * (DA_QK ** -0.5 * math.log2(math.e))).astype(BF16)
    k_out[...] = k.astype(BF16)


def _prep_call(geom, latent, proj, qw, kw, ones, cos=None, sin=None):
    tm = 512
    w = DA_HEADS * 2 * DA_QK
    if latent:
        n, off = geom.n_lat, geom.n_ctx // tm
        per = geom.dec_seq // tm
    else:
        n, off = geom.n_ctx, 0
    in_specs = [pl.BlockSpec((tm, w), lambda i: (i + off, C_DQ // w)),
                pl.BlockSpec((tm, w), lambda i: (i + off, C_DK // w)),
                pl.BlockSpec((1, w), lambda i: (0, 0)),
                pl.BlockSpec((1, w), lambda i: (0, 0)),
                pl.BlockSpec((w, w), lambda i: (0, 0))]
    args = [proj, proj, qw, kw, ones]
    out_specs = [pl.BlockSpec((tm, w), lambda i: (i, 0)), pl.BlockSpec((tm, w), lambda i: (i, 0))]
    out_shape = [jax.ShapeDtypeStruct((n, w), BF16), jax.ShapeDtypeStruct((n, w), BF16)]
    if latent:
        in_specs += [pl.BlockSpec((tm, 2 * DA_QK), lambda i: (i % per, 0)),
                     pl.BlockSpec((tm, 2 * DA_QK), lambda i: (i % per, 0))]
        args += [cos, sin]
    else:
        in_specs.append(pl.BlockSpec((tm, w), lambda i: (i, C_DV // w)))
        args.append(proj)
        out_specs += [pl.BlockSpec((tm, w), lambda i: (i, 0)), pl.BlockSpec((tm, w), lambda i: (i, 0))]
        out_shape += [jax.ShapeDtypeStruct((n, w), F32), jax.ShapeDtypeStruct((n, w), F32)]
    return pl.pallas_call(
        functools.partial(_prep_kernel, latent),
        grid=(n // tm,),
        in_specs=in_specs, out_specs=out_specs, out_shape=out_shape,
        compiler_params=_cparams(("arbitrary",)),
        name="qk_prep_lat" if latent else "qk_prep_ctx",
    )(*args)


ATT_KC = 256
LOG2E = math.log2(math.e)
ATT_SAFE_LOGIT = 60.0


def _attn_kernel(out_scale, has_cache, *refs):
    if has_cache:
        par_ref, q_ref, kc_ref, vc_ref, kl_ref, vl_ref, sw_ref, o_ref, e_scr, o_scr = refs
        srcs = [(kc_ref, vc_ref), (kl_ref, vl_ref)]
    else:
        par_ref, q_ref, kl_ref, vl_ref, sw_ref, o_ref, e_scr, o_scr = refs
        srcs = [(kl_ref, vl_ref)]
    chunks = [(kr, vr, st) for kr, vr in srcs for st in range(0, kr.shape[0], ATT_KC)]
    lam = par_ref[0]
    no_shift = par_ref[1] > 0.5
    tq = q_ref.shape[0]
    q = q_ref[...]
    lane = lax.broadcasted_iota(jnp.int32, q.shape, 1)
    zero = jnp.zeros_like(q)
    qq = jnp.concatenate([jnp.where(lane < DA_QK, q, zero), jnp.where(lane >= DA_QK, q, zero)], axis=0)
    nt = (((1,), (1,)), ((), ()))
    half = ATT_KC // 2

    def logits(kr, st):
        return lax.dot_general(qq, kr[st:st + ATT_KC, :], nt, preferred_element_type=F32)

    def run(shift):
        m = None
        if shift:
            for c, (kr, vr, st) in enumerate(chunks):
                s = logits(kr, st)
                e_scr[c] = s
                mc = jnp.max(s, axis=-1, keepdims=True)
                m = mc if m is None else jnp.maximum(m, mc)
        lsum = None
        for c, (kr, vr, st) in enumerate(chunks):
            e = jnp.exp2(e_scr[c] - m) if shift else jnp.exp2(logits(kr, st))
            e_scr[c] = e
            part = e[:, :half] + e[:, half:]
            lsum = part if lsum is None else lsum + part
        l = jnp.sum(lsum, axis=-1, keepdims=True)
        l1 = l[0:tq]
        rho = lam * l1 / l[tq:2 * tq]
        acc = None
        for c, (kr, vr, st) in enumerate(chunks):
            w = (e_scr[c, 0:tq, :] - rho * e_scr[c, tq:2 * tq, :]).astype(BF16)
            t = jnp.dot(w, vr[st:st + ATT_KC, :], preferred_element_type=F32)
            acc = t if acc is None else acc + t
        o_scr[...] = acc / l1

    @pl.when(no_shift)
    def _():
        run(False)

    @pl.when(jnp.logical_not(no_shift))
    def _():
        run(True)

    o = o_scr[...]
    y = o * lax.rsqrt(jnp.mean(o * o, axis=-1, keepdims=True) + EPS) * sw_ref[...]
    o_ref[...] = (y * out_scale).astype(BF16)


def _attn_call(par, lam_init, q2d, k2d, proj, v_row_off, n_b, t_q, t_kl, tq, subln_w, cache=None):
    hw = 2 * DA_QK
    nq = t_q // tq
    vcol = C_DV // DA_V
    in_specs = [pl.BlockSpec(memory_space=pltpu.SMEM),
                pl.BlockSpec((tq, hw), lambda b, h, qi: (b * nq + qi, h))]
    args = [par, q2d]
    n_chunks = t_kl // ATT_KC
    if cache is not None:
        kc, vc = cache
        p = kc.shape[1]
        n_chunks += p // ATT_KC
        in_specs += [pl.BlockSpec((None, p, hw), lambda b, h, qi: (b, 0, h)),
                     pl.BlockSpec((None, p, DA_V), lambda b, h, qi: (b, 0, h))]
        args += [kc, vc]
    in_specs += [pl.BlockSpec((t_kl, hw), lambda b, h, qi: (b, h)),
                 pl.BlockSpec((t_kl, DA_V), lambda b, h, qi: (v_row_off + b, vcol + h)),
                 pl.BlockSpec((1, DA_V), lambda b, h, qi: (0, 0))]
    args += [k2d, proj, subln_w.reshape(1, DA_V)]
    return pl.pallas_call(
        functools.partial(_attn_kernel, 1.0 - lam_init, cache is not None),
        grid=(n_b, DA_HEADS, nq),
        in_specs=in_specs,
        out_specs=pl.BlockSpec((tq, DA_V), lambda b, h, qi: (b * nq + qi, h)),
        out_shape=jax.ShapeDtypeStruct((n_b * t_q, DA_HEADS * DA_V), BF16),
        scratch_shapes=[pltpu.VMEM((n_chunks, 2 * tq, ATT_KC), F32), pltpu.VMEM((tq, DA_V), F32)],
        compiler_params=_cparams(("arbitrary", "arbitrary", "arbitrary")),
        name="diff_attn_lat" if cache is not None else "diff_attn_ctx",
    )(*args)


def _ret_state_update(kt, v, kd, cd, s_old):
    parts = []
    for h in range(RET_HEADS):
        rows = slice(h * RET_QK, (h + 1) * RET_QK)
        kh = (kt[rows, :].astype(F32) * kd[rows, :]).astype(BF16)
        parts.append(jnp.dot(kh, v[:, h * RET_V:(h + 1) * RET_V], preferred_element_type=F32))
    return cd * s_old + jnp.concatenate(parts, axis=0)


def _ret_bwd_kernel(geom, kt_ref, v_ref, kd_ref, cd_ref, s0_ref, sstart_ref, send_ref, s_scr):
    i = geom.n_blocks - 1 - pl.program_id(0)

    @pl.when(geom.seq_end(i))
    def _():
        s_scr[...] = s0_ref[...]

    s_old = s_scr[...]
    sstart_ref[...] = s_old
    kt = kt_ref[...] * jnp.asarray(RET_QK ** -0.5, BF16)
    s_new = _ret_state_update(kt, v_ref[...], kd_ref[...], cd_ref[...], s_old)
    s_scr[...] = s_new
    send_ref[...] = s_new


def _ret_bwd_call(geom, proj, rkt, kd_b, cd_b, s0):
    nb = geom.n_blocks
    hs = RET_HEADS * RET_QK

    def blk(g):
        return nb - 1 - g

    return pl.pallas_call(
        functools.partial(_ret_bwd_kernel, geom),
        grid=(nb,),
        in_specs=[pl.BlockSpec((hs, BLK), lambda g: (0, blk(g))),
                  pl.BlockSpec((BLK, RET_HEADS * RET_V), lambda g: (blk(g), C_RV // (RET_HEADS * RET_V))),
                  pl.BlockSpec((hs, BLK), lambda g: (0, 0)),
                  pl.BlockSpec((hs, RET_V), lambda g: (0, 0)),
                  pl.BlockSpec((None, None, hs, RET_V), lambda g: (geom.seq_id(blk(g)), 1, 0, 0))],
        out_specs=[pl.BlockSpec((None, hs, RET_V), lambda g: (blk(g), 0, 0)),
                   pl.BlockSpec((None, hs, RET_V), lambda g: (blk(g), 0, 0))],
        out_shape=[jax.ShapeDtypeStruct((nb, hs, RET_V), F32),
                   jax.ShapeDtypeStruct((nb, hs, RET_V), F32)],
        scratch_shapes=[pltpu.VMEM((hs, RET_V), F32)],
        compiler_params=_cparams(("arbitrary",)),
        name="ret_bwd_state",
    )(rkt, proj, kd_b, cd_b, s0)


def _ret_main_kernel(geom, q_ref, kt_ref, v_ref, g_ref, dsum_ref, qdf_ref, qdb_ref, kd_ref, cd_ref,
                     s0_ref, sb_ref, o_ref, send_ref, s_scr):
    i = pl.program_id(0)

    @pl.when(geom.seq_start(i))
    def _():
        s_scr[...] = s0_ref[...]

    s_f = s_scr[...]
    s_fb = s_f.astype(BF16)
    s_bb = sb_ref[...].astype(BF16)
    q = q_ref[...].astype(F32)
    kt = kt_ref[...] * jnp.asarray(RET_QK ** -0.5, BF16)
    v = v_ref[...]
    lane = lax.broadcasted_iota(jnp.int32, q.shape, 1)
    for h in range(RET_HEADS):
        in_head = (lane >= h * RET_QK) & (lane < (h + 1) * RET_QK)
        qh = jnp.where(in_head, q, 0.0)
        vh = v[:, h * RET_V:(h + 1) * RET_V]
        sc = jnp.dot(qh.astype(BF16), kt, preferred_element_type=F32) * dsum_ref[h]
        o = jnp.dot(sc.astype(BF16), vh, preferred_element_type=F32)
        o += jnp.dot((qh * qdf_ref[...]).astype(BF16), s_fb, preferred_element_type=F32)
        o += jnp.dot((qh * qdb_ref[...]).astype(BF16), s_bb, preferred_element_type=F32)
        y = o * lax.rsqrt(jnp.mean(o * o, axis=-1, keepdims=True) + EPS)
        gv = g_ref[:, h * RET_V:(h + 1) * RET_V].astype(F32)
        o_ref[:, h * RET_V:(h + 1) * RET_V] = (y * (gv * jax.nn.sigmoid(gv))).astype(BF16)
    s_new = _ret_state_update(kt, v, kd_ref[...], cd_ref[...], s_f)
    s_scr[...] = s_new
    send_ref[...] = s_new


def _ret_main_call(geom, proj, rkt, dsum, qdf, qdb, kd_f, cd_f, s0, sb_start):
    nb = geom.n_blocks
    hs = RET_HEADS * RET_QK
    hv = RET_HEADS * RET_V
    return pl.pallas_call(
        functools.partial(_ret_main_kernel, geom),
        grid=(nb,),
        in_specs=[pl.BlockSpec((BLK, hs), lambda g: (g, C_RQ // hs)),
                  pl.BlockSpec((hs, BLK), lambda g: (0, g)),
                  pl.BlockSpec((BLK, hv), lambda g: (g, C_RV // hv)),
                  pl.BlockSpec((BLK, hv), lambda g: (g, C_RG // hv)),
                  pl.BlockSpec((RET_HEADS, BLK, BLK), lambda g: (0, 0, 0)),
                  pl.BlockSpec((BLK, hs), lambda g: (0, 0)),
                  pl.BlockSpec((BLK, hs), lambda g: (0, 0)),
                  pl.BlockSpec((hs, BLK), lambda g: (0, 0)),
                  pl.BlockSpec((hs, RET_V), lambda g: (0, 0)),
                  pl.BlockSpec((None, None, hs, RET_V), lambda g: (geom.seq_id(g), 0, 0, 0)),
                  pl.BlockSpec((None, hs, RET_V), lambda g: (g, 0, 0))],
        out_specs=[pl.BlockSpec((BLK, hv), lambda g: (g, 0)),
                   pl.BlockSpec((None, hs, RET_V), lambda g: (g, 0, 0))],
        out_shape=[jax.ShapeDtypeStruct((geom.n_tok, hv), BF16),
                   jax.ShapeDtypeStruct((nb, hs, RET_V), F32)],
        scratch_shapes=[pltpu.VMEM((hs, RET_V), F32)],
        compiler_params=_cparams(("arbitrary",)),
        name="ret_main",
    )(proj, rkt, proj, proj, dsum, qdf, qdb, kd_f, cd_f, s0, sb_start)


def _ret_tables(ret_decay_l):
    log_g = jax.nn.log_sigmoid(ret_decay_l.astype(F32))
    pos = jnp.arange(BLK, dtype=F32)
    diff = pos[:, None] - pos[None, :]
    lf = log_g[0][:, None, None]
    lb = log_g[1][:, None, None]
    dsum = (jnp.where(diff >= 0, jnp.exp(jnp.maximum(diff, 0.0)[None] * lf), 0.0)
            + jnp.where(diff <= 0, jnp.exp(jnp.maximum(-diff, 0.0)[None] * lb), 0.0))

    def per_lane(e, lg):
        return jnp.repeat(jnp.exp(e[:, None] * lg[None, :]), RET_QK, axis=1)

    qdf = per_lane(pos + 1.0, log_g[0])
    qdb = per_lane(BLK - pos, log_g[1])
    kd_f = per_lane(BLK - 1.0 - pos, log_g[0]).T
    kd_b = per_lane(pos, log_g[1]).T
    cd_f = jnp.broadcast_to(jnp.repeat(jnp.exp(BLK * log_g[0]), RET_QK)[:, None], (RET_HEADS * RET_QK, RET_V))
    cd_b = jnp.broadcast_to(jnp.repeat(jnp.exp(BLK * log_g[1]), RET_QK)[:, None], (RET_HEADS * RET_QK, RET_V))
    return dsum, qdf, qdb, kd_f, kd_b, cd_f, cd_b


def _merge_kernel(n_ctx_tiles, ba_ref, bbc_ref, bbl_ref, bc_ref, g0_ref, g1_ref, g2_ref, xc_ref, xl_ref,
                  gate_ref, sc_ref, sh_ref, nw_ref, wb_ref, wo_ref, rhi_ref, rlo_ref, x1_ref, h2_ref, lt_ref):
    branches = (ba_ref[...], _pick_part(n_ctx_tiles, bbc_ref, bbl_ref), bc_ref[...])
    acc = None
    for br, (b, g_ref) in enumerate(zip(branches, (g0_ref, g1_ref, g2_ref))):
        p = jnp.dot(b, wb_ref[br], preferred_element_type=F32)
        t = jax.nn.sigmoid(g_ref[...].astype(F32)) * p
        acc = t if acc is None else acc + t
    m = jnp.dot(acc.astype(BF16), wo_ref[...], preferred_element_type=F32)
    x1 = _pick_part(n_ctx_tiles, xc_ref, xl_ref) + gate_ref[...] * m
    x1_ref[...] = x1
    ms = jnp.mean(x1 * x1, axis=-1, keepdims=True)
    h2 = x1 * lax.rsqrt(ms + EPS) * nw_ref[...] * (1.0 + sc_ref[...]) + sh_ref[...]
    h2b = h2.astype(BF16)
    h2_ref[...] = h2b
    h2lo = (h2 - h2b.astype(F32)).astype(BF16)
    nt = (((1,), (1,)), ((), ()))
    lt_ref[...] = (lax.dot_general(rhi_ref[...], h2b, nt, preferred_element_type=F32)
                   + lax.dot_general(rhi_ref[...], h2lo, nt, preferred_element_type=F32)
                   + lax.dot_general(rlo_ref[...], h2b, nt, preferred_element_type=F32))


def _merge_call(geom, l, ba, bb_ctx, bb_lat, bc, proj, x_ctx, x_lat, mod6, norm2_w, wb_bf, wo_bf, r_hi, r_lo):
    tm = 512
    gcol = C_GL // D_MODEL
    full = lambda shape: pl.BlockSpec(shape, lambda i: tuple(0 for _ in shape))
    tok = lambda w: pl.BlockSpec((tm, w), lambda i: (i, 0))
    return pl.pallas_call(
        functools.partial(_merge_kernel, geom.n_ctx // tm),
        grid=(geom.n_tok // tm,),
        in_specs=[tok(BRANCH_W)] + _split_in_specs(geom, tm, BRANCH_W, 1) + [tok(BRANCH_W),
                  pl.BlockSpec((tm, D_MODEL), lambda i: (i, gcol)),
                  pl.BlockSpec((tm, D_MODEL), lambda i: (i, gcol + 1)),
                  pl.BlockSpec((tm, D_MODEL), lambda i: (i, gcol + 2))]
                 + _split_in_specs(geom, tm, D_MODEL, 1) + [
                  _mod_spec(geom, l, 2, tm, 1), _mod_spec(geom, l, 4, tm, 1), _mod_spec(geom, l, 3, tm, 1),
                  full((1, D_MODEL)),
                  full((N_BRANCH, BRANCH_W, D_MODEL)), full((D_MODEL, D_MODEL)),
                  full((N_EXPERTS, D_MODEL)), full((N_EXPERTS, D_MODEL))],
        out_specs=[tok(D_MODEL), tok(D_MODEL), pl.BlockSpec((N_EXPERTS, tm), lambda i: (0, i))],
        out_shape=[jax.ShapeDtypeStruct((geom.n_tok, D_MODEL), F32),
                   jax.ShapeDtypeStruct((geom.n_tok, D_MODEL), BF16),
                   jax.ShapeDtypeStruct((N_EXPERTS, geom.n_tok), F32)],
        compiler_params=_cparams(("arbitrary",)),
        name="merge_out",
    )(ba, bb_ctx, bb_lat, bc, proj, proj, proj, x_ctx, x_lat, mod6, mod6, mod6,
      norm2_w.reshape(1, D_MODEL), wb_bf, wo_bf, r_hi, r_lo)


def _router_kernel(lt_ref, bias_ref, g_ref):
    per = N_EXPERTS // N_GROUPS
    tm = lt_ref.shape[1]
    scores = jax.nn.sigmoid(lt_ref[...])
    biased = scores + bias_ref[...]
    b3 = biased.reshape(N_GROUPS, per, tm)
    neg = jnp.float32(-jnp.inf)
    m1 = jnp.max(b3, axis=1, keepdims=True)
    is_m1 = b3 == m1
    cnt = jnp.sum(is_m1.astype(F32), axis=1, keepdims=True)
    m2 = jnp.max(jnp.where(is_m1, neg, b3), axis=1, keepdims=True)
    grp = (m1 + jnp.where(cnt >= 2.0, m1, m2)).reshape(N_GROUPS, tm)
    gidx = lax.broadcasted_iota(jnp.int32, (N_GROUPS, tm), 0)
    grank = jnp.zeros((N_GROUPS, tm), F32)
    for g2 in range(N_GROUPS):
        other = grp[g2:g2 + 1, :]
        ahead = (other > grp) | ((other == grp) & (gidx > g2))
        grank += ahead.astype(F32)
    gsel = (grank < float(TOPK_GROUPS)).astype(F32)
    emask = jnp.broadcast_to(gsel.reshape(N_GROUPS, 1, tm), (N_GROUPS, per, tm)).reshape(N_EXPERTS, tm)
    masked = jnp.where(emask > 0.0, biased, neg)
    eidx = lax.broadcasted_iota(jnp.int32, (N_EXPERTS, tm), 0)
    erank = jnp.zeros((N_EXPERTS, tm), F32)
    for e2 in range(N_EXPERTS):
        other = masked[e2:e2 + 1, :]
        ahead = (other > masked) | ((other == masked) & (eidx > e2))
        erank += ahead.astype(F32)
    w = jnp.where(erank < float(TOP_K), scores, 0.0)
    gates_t = w / jnp.sum(w, axis=0, keepdims=True) * ROUTED_SCALE
    pad = jnp.zeros((GATE_W - N_EXPERTS, tm), F32)
    g_ref[...] = jnp.concatenate([gates_t, pad], axis=0).T


def _router_call(geom, logits_t, bias):
    tm = 512
    return pl.pallas_call(
        _router_kernel,
        grid=(geom.n_tok // tm,),
        in_specs=[pl.BlockSpec((N_EXPERTS, tm), lambda i: (0, i)),
                  pl.BlockSpec((N_EXPERTS, 1), lambda i: (0, 0))],
        out_specs=pl.BlockSpec((tm, GATE_W), lambda i: (i, 0)),
        out_shape=jax.ShapeDtypeStruct((geom.n_tok, GATE_W), F32),
        compiler_params=_cparams(("arbitrary",)),
        name="router",
    )(logits_t, bias.reshape(N_EXPERTS, 1))


MOE_EB = 8
MOE_TM = 512


def _moe_kernel(n_ctx_tiles, h_ref, g_ref, gu_ref, dn_ref, sgu_ref, sdn_ref, x1_ref, gate_ref,
                oc_ref, ol_ref, acc_scr):
    i = pl.program_id(0)
    j = pl.program_id(1)
    h = h_ref[...]

    def expert(gu, dn, ge):
        a = jnp.dot(h, gu, preferred_element_type=F32)
        hg = a[:, :D_EXPERT]
        act = (hg * jax.nn.sigmoid(hg)) * a[:, D_EXPERT:]
        if ge is not None:
            act = act * ge
        return jnp.dot(act.astype(BF16), dn, preferred_element_type=F32)

    @pl.when(j == 0)
    def _():
        acc_scr[...] = expert(sgu_ref[...], sdn_ref[...], None)

    gts = g_ref[...]
    lane = lax.broadcasted_iota(jnp.int32, gts.shape, 1)
    acc = acc_scr[...]
    for e in range(MOE_EB):
        ge = jnp.sum(jnp.where(lane == j * MOE_EB + e, gts, 0.0), axis=1, keepdims=True)
        acc += expert(gu_ref[e], dn_ref[e], ge)
    acc_scr[...] = acc

    @pl.when(j == pl.num_programs(1) - 1)
    def _():
        y = x1_ref[...] + gate_ref[...] * acc

        @pl.when(i < n_ctx_tiles)
        def _():
            oc_ref[...] = y

        @pl.when(i >= n_ctx_tiles)
        def _():
            ol_ref[...] = y


def _moe_call(geom, l, h2, gates, gu_bf, dn_bf, sgu_bf, sdn_bf, x1, mod6):
    tm = MOE_TM
    nct = geom.n_ctx // tm
    return pl.pallas_call(
        functools.partial(_moe_kernel, nct),
        grid=(geom.n_tok // tm, N_EXPERTS // MOE_EB),
        in_specs=[pl.BlockSpec((tm, D_MODEL), lambda i, j: (i, 0)),
                  pl.BlockSpec((tm, GATE_W), lambda i, j: (i, 0)),
                  pl.BlockSpec((MOE_EB, D_MODEL, 2 * D_EXPERT), lambda i, j: (j, 0, 0)),
                  pl.BlockSpec((MOE_EB, D_EXPERT, D_MODEL), lambda i, j: (j, 0, 0)),
                  pl.BlockSpec((D_MODEL, 2 * D_EXPERT), lambda i, j: (0, 0)),
                  pl.BlockSpec((D_EXPERT, D_MODEL), lambda i, j: (0, 0)),
                  pl.BlockSpec((tm, D_MODEL), lambda i, j: (i, 0)),
                  _mod_spec(geom, l, 5, tm, 2)],
        out_specs=[pl.BlockSpec((tm, D_MODEL), lambda i, j: (jnp.minimum(i, nct - 1), 0)),
                   pl.BlockSpec((tm, D_MODEL), lambda i, j: (jnp.maximum(i - nct, 0), 0))],
        out_shape=[jax.ShapeDtypeStruct((geom.n_ctx, D_MODEL), F32),
                   jax.ShapeDtypeStruct((geom.n_lat, D_MODEL), F32)],
        scratch_shapes=[pltpu.VMEM((tm, D_MODEL), F32)],
        compiler_params=_cparams(("arbitrary", "arbitrary")),
        name="moe_experts",
    )(h2, gates, gu_bf, dn_bf, sgu_bf, sdn_bf, x1, mod6)


def _rope_tables(dec_seq):
    rows = dec_seq // GRID_W
    row = jnp.repeat(jnp.arange(rows, dtype=F32), GRID_W)
    col = jnp.tile(jnp.arange(GRID_W, dtype=F32), rows)
    inv = ROPE_BASE ** (-jnp.arange(ROPE_PAIRS, dtype=F32) / ROPE_PAIRS)
    ar = row[:, None] * inv[None, :]
    ac = col[:, None] * inv[None, :]
    cos64 = jnp.concatenate([jnp.cos(ar), jnp.cos(ar), jnp.cos(ac), jnp.cos(ac)], axis=1)
    sin64 = jnp.concatenate([-jnp.sin(ar), jnp.sin(ar), -jnp.sin(ac), jnp.sin(ac)], axis=1)
    return jnp.tile(cos64, (1, 2)), jnp.tile(sin64, (1, 2))


def _block_diag_gate(wg_dir):
    eye = jnp.eye(LRU_BLOCKS, dtype=F32)
    dense = jnp.einsum('gnij,nm->gnimj', wg_dir.astype(F32), eye).reshape(2, D_RNN, D_RNN)
    return jnp.concatenate([dense[0], dense[1]], axis=1)


def kernel(x_prompt, x_sample, cache_k, cache_v, state_lru, state_ret, c, c_ctx, ada_w, ada_b, norm1_w, norm2_w, w_in, conv_w, conv_b, lru_gate_w, lru_gate_b, lru_lambda, q_norm_w, k_norm_w, diff_lambda, subln_w, ret_decay, w_branch, w_out, router_w, router_bias, w_exp_gu, w_exp_down, w_sh_gu, w_sh_down):
    batch, seq, _ = x_prompt.shape
    dec_batch, dec_seq, _ = x_sample.shape
    assert 1 + dec_batch <= MOD_ROWS
    geom = _Geom(batch, seq, dec_batch, dec_seq)
    hs = RET_HEADS * RET_QK
    aw = DA_HEADS * 2 * DA_QK

    x_ctx = x_prompt.reshape(geom.n_ctx, D_MODEL)
    x_lat = x_sample.reshape(geom.n_lat, D_MODEL)
    cvec = jnp.zeros((MOD_ROWS, D_MODEL), F32).at[0].set(c_ctx).at[1:1 + dec_batch].set(c)
    mod6 = _ada_call(cvec, ada_w, ada_b).reshape(DEPTH, MOD_ROWS, 6, 1, D_MODEL)

    ones_bd = jnp.kron(jnp.eye(aw // DA_QK, dtype=F32), jnp.ones((DA_QK, DA_QK), F32)).astype(BF16)
    cos_t, sin_t = _rope_tables(dec_seq)

    ks, vs, lrus, rets = [], [], [], []
    for l in range(DEPTH):
        lam_init = 0.8 - 0.6 * math.exp(-0.3 * l)
        w_in_bf = w_in[l].astype(BF16)
        w_rkt_bf = w_in[l][:, C_RK:C_RK + hs].T.astype(BF16)
        proj, rkt = _inproj_call(geom, l, x_ctx, x_lat, mod6, norm1_w[l], w_in_bf, w_rkt_bf)

        sp = jax.nn.softplus(-lru_lambda[l].astype(F32))
        h0 = jnp.concatenate([jnp.zeros((batch, 2, D_RNN), F32), state_lru[:, l].astype(F32)], axis=0)
        h0 = h0.reshape(geom.n_seq, 2, 1, D_RNN)
        cb = conv_b[l].reshape(1, D_RNN)
        lru_args = []
        for d in range(2):
            lru_args.append((_block_diag_gate(lru_gate_w[l, d]).astype(BF16),
                             lru_gate_b[l, d].reshape(1, 2 * D_RNN), sp[d].reshape(1, D_RNN)))
        hf, hf_last = _lru_call(geom, False, proj, conv_w[l], cb, *lru_args[0], h0)
        branch_a, hb_last = _lru_call(geom, True, proj, conv_w[l], cb, *lru_args[1], h0, hf)

        qw = jnp.tile(q_norm_w[l], aw // DA_QK).reshape(1, aw)
        kw = jnp.tile(k_norm_w[l], aw // DA_QK).reshape(1, aw)
        q_c, k_c, k_c32, v_c32 = _prep_call(geom, False, proj, qw, kw, ones_bd)
        q_l, k_l = _prep_call(geom, True, proj, qw, kw, ones_bd, cos_t, sin_t)
        lam_p = diff_lambda[l].astype(F32)
        lam = jnp.exp(jnp.sum(lam_p[0] * lam_p[1])) - jnp.exp(jnp.sum(lam_p[2] * lam_p[3])) + lam_init
        q_bound = DA_QK * jnp.max(jnp.square(q_norm_w[l].astype(F32))) * (DA_QK ** -0.5 * LOG2E) ** 2
        k_bound = DA_QK * jnp.max(jnp.square(k_norm_w[l].astype(F32)))
        kc32 = cache_k[:, l].astype(F32)
        kc_bound = jnp.maximum(k_bound, jnp.max(jnp.sum(jnp.square(kc32), axis=-1)))

        def attn_par(kb):
            ok = (q_bound * kb * 1.05 < ATT_SAFE_LOGIT ** 2).astype(F32)
            return jnp.stack([lam, ok])

        assert geom.n_ctx % dec_seq == 0
        cache = (kc32.reshape(dec_batch, -1, aw).astype(BF16),
                 cache_v[:, l].reshape(dec_batch, -1, DA_HEADS * DA_V).astype(BF16))
        att_c = _attn_call(attn_par(k_bound), lam_init, q_c, k_c, proj, 0, batch, seq, seq, 256, subln_w[l])
        att_l = _attn_call(attn_par(kc_bound), lam_init, q_l, k_l, proj, geom.n_ctx // dec_seq, dec_batch,
                           dec_seq, dec_seq, 256, subln_w[l], cache)

        dsum, qdf, qdb, kd_f, kd_b, cd_f, cd_b = _ret_tables(ret_decay[l])
        s0 = jnp.concatenate([jnp.zeros((batch, 2, hs, RET_V), F32),
                              state_ret[:, l].astype(F32).reshape(dec_batch, 2, hs, RET_V)], axis=0)
        sb_start, sb_end = _ret_bwd_call(geom, proj, rkt, kd_b, cd_b, s0)
        branch_c, sf_end = _ret_main_call(geom, proj, rkt, dsum, qdf, qdb, kd_f, cd_f, s0, sb_start)

        r_t = router_w[l].T.astype(F32)
        r_hi = r_t.astype(BF16)
        r_lo = (r_t - r_hi.astype(F32)).astype(BF16)
        x1, h2, logits_t = _merge_call(geom, l, branch_a, att_c, att_l, branch_c, proj, x_ctx, x_lat, mod6,
                                       norm2_w[l], w_branch[l].astype(BF16), w_out[l].astype(BF16), r_hi, r_lo)
        gates = _router_call(geom, logits_t, router_bias[l].astype(F32))
        x_ctx, x_lat = _moe_call(geom, l, h2, gates, w_exp_gu[l].astype(BF16), w_exp_down[l].astype(BF16),
                                 w_sh_gu[l].astype(BF16), w_sh_down[l].astype(BF16), x1, mod6)

        ks.append(k_c32.reshape(batch, seq, DA_HEADS, 2, DA_QK))
        vs.append(v_c32.reshape(batch, seq, DA_HEADS, DA_V))
        lrus.append(jnp.stack([hf_last[:batch, 0], hb_last[:batch, 0]], axis=1))
        rets.append(jnp.stack([sf_end[:batch].reshape(batch, RET_HEADS, RET_QK, RET_V),
                               sb_end[:batch].reshape(batch, RET_HEADS, RET_QK, RET_V)], axis=1))

    y_prompt = x_ctx.reshape(batch, seq, D_MODEL)
    y_sample = x_lat.reshape(dec_batch, dec_seq, D_MODEL)
    return (y_prompt, y_sample, jnp.stack(ks, axis=1), jnp.stack(vs, axis=1),
            jnp.stack(lrus, axis=1), jnp.stack(rets, axis=1))
```

```python
import functools
import math

import numpy as np
import jax
import jax.numpy as jnp
from jax import lax
from jax.experimental import pallas as pl
from jax.experimental.pallas import tpu as pltpu

F32 = jnp.float32
BF16 = jnp.bfloat16

D_MODEL = 1024
DEPTH = 2
GRID_W = 64
D_RNN = 512
LRU_BLOCKS = 8
LRU_BLOCK = D_RNN // LRU_BLOCKS
CONV_W = 4
LRU_C = 8.0
DA_HEADS = 4
DA_QK = 64
DA_V = 128
ROPE_PAIRS = DA_QK // 4
ROPE_BASE = 10000.0
RET_HEADS = 4
RET_QK = 64
RET_V = 128
BRANCH_W = 512
N_BRANCH = 3
D_IN = 7168
N_EXPERTS = 64
TOP_K = 8
N_GROUPS = 8
TOPK_GROUPS = 4
D_EXPERT = 256
ROUTED_SCALE = 2.5
EPS = 1e-6

C_XA, C_GA, C_DQ, C_DK, C_DV = 0, 512, 1024, 1536, 2048
C_RQ, C_RK, C_RV, C_RG, C_GL = 2560, 2816, 3072, 3584, 4096

BLK = 256
LRU_SUB = 8
LRU_LANES = D_RNN // 128
GATE_W = 128
MOD_ROWS = 8
VMEM_LIMIT = 56 * 1024 * 1024


def _cparams(sem):
    return pltpu.CompilerParams(dimension_semantics=sem, vmem_limit_bytes=VMEM_LIMIT)


class _Geom:
    def __init__(self, batch, seq, dec_batch, dec_seq):
        assert seq == BLK and dec_seq % BLK == 0
        self.batch, self.seq, self.dec_batch, self.dec_seq = batch, seq, dec_batch, dec_seq
        self.n_ctx = batch * seq
        self.n_lat = dec_batch * dec_seq
        self.n_tok = self.n_ctx + self.n_lat
        self.ctx_blocks = self.n_ctx // BLK
        self.lat_blocks = dec_seq // BLK
        self.n_blocks = self.n_tok // BLK
        self.n_seq = batch + dec_batch

    def mod_row(self, i, tile):
        nct = self.n_ctx // tile
        per = self.dec_seq // tile
        return jnp.where(i < nct, 0, 1 + (i - nct) // per)

    def seq_id(self, i):
        return jnp.where(i < self.ctx_blocks, i, self.ctx_blocks + (i - self.ctx_blocks) // self.lat_blocks)

    def seq_start(self, i):
        return jnp.logical_or(i < self.ctx_blocks, (i - self.ctx_blocks) % self.lat_blocks == 0)

    def seq_end(self, i):
        return jnp.logical_or(i < self.ctx_blocks, (i - self.ctx_blocks) % self.lat_blocks == self.lat_blocks - 1)


def _ada_kernel(c_ref, w_ref, b_ref, o_ref):
    cv = c_ref[...]
    s = cv * jax.nn.sigmoid(cv)
    o_ref[...] = jnp.dot(s, w_ref[...], preferred_element_type=F32,
                         precision=lax.Precision.HIGHEST) + b_ref[...]


def _ada_call(cvec, ada_w, ada_b):
    depth = ada_w.shape[0]
    nt = 6
    return pl.pallas_call(
        _ada_kernel,
        grid=(depth, nt),
        in_specs=[pl.BlockSpec((MOD_ROWS, D_MODEL), lambda l, j: (0, 0)),
                  pl.BlockSpec((None, D_MODEL, D_MODEL), lambda l, j: (l, 0, j)),
                  pl.BlockSpec((None, 1, D_MODEL), lambda l, j: (l, 0, j))],
        out_specs=pl.BlockSpec((None, MOD_ROWS, D_MODEL), lambda l, j: (l, 0, j)),
        out_shape=jax.ShapeDtypeStruct((depth, MOD_ROWS, 6 * D_MODEL), F32),
        compiler_params=_cparams(("arbitrary", "arbitrary")),
        name="ada_mod",
    )(cvec, ada_w, ada_b.reshape(depth, 1, 6 * D_MODEL))


def _mod_spec(geom, l, which, tile, ngrid):
    if ngrid == 1:
        return pl.BlockSpec((None, None, None, 1, D_MODEL),
                            lambda i: (l, geom.mod_row(i, tile), which, 0, 0))
    return pl.BlockSpec((None, None, None, 1, D_MODEL),
                        lambda i, j: (l, geom.mod_row(i, tile), which, 0, 0))


def _split_in_specs(geom, tile, width, ngrid):
    nct = geom.n_ctx // tile
    if ngrid == 1:
        return [pl.BlockSpec((tile, width), lambda i: (jnp.minimum(i, nct - 1), 0)),
                pl.BlockSpec((tile, width), lambda i: (jnp.maximum(i - nct, 0), 0))]
    return [pl.BlockSpec((tile, width), lambda i, j: (jnp.minimum(i, nct - 1), 0)),
            pl.BlockSpec((tile, width), lambda i, j: (jnp.maximum(i - nct, 0), 0))]


def _pick_part(n_ctx_tiles, c_ref, l_ref):
    return jnp.where(pl.program_id(0) < n_ctx_tiles, c_ref[...], l_ref[...])


def _inproj_kernel(n_ctx_tiles, xc_ref, xl_ref, sc_ref, sh_ref, nw_ref, w_ref, wkt_ref, o_ref, kt_ref, h_scr):
    @pl.when(pl.program_id(1) == 0)
    def _():
        x = _pick_part(n_ctx_tiles, xc_ref, xl_ref)
        ms = jnp.mean(x * x, axis=-1, keepdims=True)
        y = x * lax.rsqrt(ms + EPS) * nw_ref[...]
        hb = (y * (1.0 + sc_ref[...]) + sh_ref[...]).astype(BF16)
        h_scr[...] = hb
        kt_ref[...] = lax.dot_general(wkt_ref[...], hb, (((1,), (1,)), ((), ())),
                                      preferred_element_type=F32).astype(BF16)

    o_ref[...] = jnp.dot(h_scr[...], w_ref[...], preferred_element_type=F32).astype(BF16)


def _inproj_call(geom, l, x_ctx, x_lat, mod6, norm_w, w_in_bf, w_rkt_bf):
    tm, tn = 1024, 1024
    grid = (geom.n_tok // tm, D_IN // tn)
    return pl.pallas_call(
        functools.partial(_inproj_kernel, geom.n_ctx // tm),
        grid=grid,
        in_specs=_split_in_specs(geom, tm, D_MODEL, 2) + [
                  _mod_spec(geom, l, 1, tm, 2),
                  _mod_spec(geom, l, 0, tm, 2),
                  pl.BlockSpec((1, D_MODEL), lambda i, j: (0, 0)),
                  pl.BlockSpec((None, D_MODEL, tn), lambda i, j: (l, 0, j)),
                  pl.BlockSpec((RET_HEADS * RET_QK, D_MODEL), lambda i, j: (0, 0))],
        out_specs=[pl.BlockSpec((tm, tn), lambda i, j: (i, j)),
                   pl.BlockSpec((RET_HEADS * RET_QK, tm), lambda i, j: (0, i))],
        out_shape=[jax.ShapeDtypeStruct((geom.n_tok, D_IN), BF16),
                   jax.ShapeDtypeStruct((RET_HEADS * RET_QK, geom.n_tok), BF16)],
        scratch_shapes=[pltpu.VMEM((tm, D_MODEL), BF16)],
        compiler_params=_cparams(("arbitrary", "arbitrary")),
        name="inproj",
    )(x_ctx, x_lat, mod6, mod6, norm_w.reshape(1, D_MODEL), w_in_bf, w_rkt_bf)


def _gelu_tanh(x):
    return 0.5 * x * (1.0 + jnp.tanh(math.sqrt(2.0 / math.pi) * (x + 0.044715 * (x * x * x))))


def _lru_kernel(geom, reverse, *refs):
    if reverse:
        (xa_ref, xp_ref, xn_ref, cw_ref, cb_ref, wg_ref, bg_ref, sp_ref, h0_ref, perm_ref, permt_ref,
         ga_ref, hf_ref, out_ref, hl_ref, c_scr) = refs
    else:
        (xa_ref, xp_ref, xn_ref, cw_ref, cb_ref, wg_ref, bg_ref, sp_ref, h0_ref, perm_ref,
         out_ref, hl_ref, c_scr) = refs
    g = pl.program_id(0)
    i = geom.n_blocks - 1 - g if reverse else g
    start = geom.seq_start(i)
    end = geom.seq_end(i)

    @pl.when(end if reverse else start)
    def _():
        c_scr[...] = h0_ref[...]

    sub_len = BLK // LRU_SUB
    perm = perm_ref[...]
    x = jnp.dot(perm, xa_ref[...], preferred_element_type=F32)
    pm = jnp.where(start, 0.0, 1.0)
    nm = jnp.where(end, 0.0, 1.0)
    hp = xp_ref.shape[0]
    p1 = xp_ref[hp - 1:hp, :].astype(F32) * pm
    p2 = xp_ref[hp - 2:hp - 1, :].astype(F32) * pm
    n0 = xn_ref[0:1, :].astype(F32) * nm
    row = lax.broadcasted_iota(jnp.int32, x.shape, 0)
    xm1 = jnp.where(row < LRU_SUB, pltpu.roll(x, LRU_SUB + 1, 0), pltpu.roll(x, LRU_SUB, 0))
    xm1 = jnp.where(row == 0, p1, xm1)
    xm2 = jnp.where(row < 2 * LRU_SUB, pltpu.roll(x, 2 * LRU_SUB + 1, 0), pltpu.roll(x, 2 * LRU_SUB, 0))
    xm2 = jnp.where(row == 0, p2, jnp.where(row == LRU_SUB, p1, xm2))
    xp1 = jnp.where(row >= BLK - LRU_SUB, pltpu.roll(x, BLK - LRU_SUB - 1, 0),
                    pltpu.roll(x, BLK - LRU_SUB, 0))
    xp1 = jnp.where(row == BLK - 1, n0, xp1)
    xc = (cw_ref[0:1, :] * xm2 + cw_ref[1:2, :] * xm1 + cw_ref[2:3, :] * x
          + cw_ref[3:4, :] * xp1 + cb_ref[...])

    gt = jnp.dot(xc.astype(BF16), wg_ref[...], preferred_element_type=F32) + bg_ref[...]
    r = jax.nn.sigmoid(gt[:, :D_RNN])
    ig = jax.nn.sigmoid(gt[:, D_RNN:])
    a = jnp.exp(-LRU_C * r * sp_ref[...])
    u = jnp.sqrt(1.0 - a * a) * ig * xc

    h = jnp.zeros((LRU_SUB, D_RNN), F32)
    p = jnp.ones((LRU_SUB, D_RNN), F32)
    h_loc = [None] * sub_len
    p_loc = [None] * sub_len
    for t in (range(sub_len - 1, -1, -1) if reverse else range(sub_len)):
        a_t = a[t * LRU_SUB:(t + 1) * LRU_SUB, :]
        h = a_t * h + u[t * LRU_SUB:(t + 1) * LRU_SUB, :]
        p = a_t * p
        h_loc[t] = h
        p_loc[t] = p
    h_in = [None] * LRU_SUB
    state = c_scr[...]
    for k in (range(LRU_SUB - 1, -1, -1) if reverse else range(LRU_SUB)):
        h_in[k] = state
        state = h[k:k + 1, :] + p[k:k + 1, :] * state
    c_scr[...] = state
    hl_ref[...] = state
    h_in = jnp.concatenate(h_in, axis=0)
    h_full = jnp.concatenate([h_loc[t] + p_loc[t] * h_in for t in range(sub_len)], axis=0)
    if reverse:
        gv = jnp.dot(perm, ga_ref[...], preferred_element_type=F32)
        y = (_gelu_tanh(gv) * (hf_ref[...] + h_full)).astype(BF16)
        out_ref[...] = jnp.dot(permt_ref[...], y, preferred_element_type=F32).astype(BF16)
    else:
        out_ref[...] = h_full


def _lru_call(geom, reverse, proj, conv_w, conv_b, wg, bg, sp, h0, hf=None):
    nb = geom.n_blocks
    halo = 16
    hpb = BLK // halo

    def blk(g):
        return nb - 1 - g if reverse else g

    d = 1 if reverse else 0
    in_specs = [
        pl.BlockSpec((BLK, D_RNN), lambda g: (blk(g), C_XA // D_RNN)),
        pl.BlockSpec((halo, D_RNN), lambda g: (jnp.maximum(blk(g) * hpb - 1, 0), C_XA // D_RNN)),
        pl.BlockSpec((halo, D_RNN), lambda g: (jnp.minimum((blk(g) + 1) * hpb, nb * hpb - 1), C_XA // D_RNN)),
        pl.BlockSpec((CONV_W, D_RNN), lambda g: (0, 0)),
        pl.BlockSpec((1, D_RNN), lambda g: (0, 0)),
        pl.BlockSpec((D_RNN, 2 * D_RNN), lambda g: (0, 0)),
        pl.BlockSpec((1, 2 * D_RNN), lambda g: (0, 0)),
        pl.BlockSpec((1, D_RNN), lambda g: (0, 0)),
        pl.BlockSpec((None, None, 1, D_RNN), lambda g: (geom.seq_id(blk(g)), d, 0, 0)),
    ]
    pos = np.arange(BLK)
    perm_np = np.zeros((BLK, BLK), np.float32)
    perm_np[pos, (pos % LRU_SUB) * (BLK // LRU_SUB) + pos // LRU_SUB] = 1.0
    in_specs.append(pl.BlockSpec((BLK, BLK), lambda g: (0, 0)))
    args = [proj, proj, proj, conv_w, conv_b, wg, bg, sp, h0, jnp.asarray(perm_np, BF16)]
    if reverse:
        in_specs += [pl.BlockSpec((BLK, BLK), lambda g: (0, 0)),
                     pl.BlockSpec((BLK, D_RNN), lambda g: (blk(g), C_GA // D_RNN)),
                     pl.BlockSpec((BLK, D_RNN), lambda g: (blk(g), 0))]
        args += [jnp.asarray(perm_np.T, BF16), proj, hf]
        out_dtype = BF16
    else:
        out_dtype = F32
    scratch = [pltpu.VMEM((1, D_RNN), F32)]
    return pl.pallas_call(
        functools.partial(_lru_kernel, geom, reverse),
        grid=(nb,),
        in_specs=in_specs,
        out_specs=[pl.BlockSpec((BLK, D_RNN), lambda g: (blk(g), 0)),
                   pl.BlockSpec((None, 1, D_RNN), lambda g: (blk(g), 0, 0))],
        out_shape=[jax.ShapeDtypeStruct((geom.n_tok, D_RNN), out_dtype),
                   jax.ShapeDtypeStruct((nb, 1, D_RNN), F32)],
        scratch_shapes=scratch,
        compiler_params=_cparams(("arbitrary",)),
        name="lru_bwd" if reverse else "lru_fwd",
    )(*args)


def _group_rms(x, w, ones):
    xx = x * x
    hi = xx.astype(BF16)
    lo = (xx - hi.astype(F32)).astype(BF16)
    ss = (jnp.dot(hi, ones, preferred_element_type=F32)
          + jnp.dot(lo, ones, preferred_element_type=F32))
    return x * lax.rsqrt(ss * (1.0 / DA_QK) + EPS) * w


def _rope(x, cos, sin):
    lane = lax.broadcasted_iota(jnp.int32, x.shape, 1)
    first = (lane % (2 * ROPE_PAIRS)) < ROPE_PAIRS
    w = x.shape[1]
    partner = jnp.where(first, pltpu.roll(x, w - ROPE_PAIRS, 1), pltpu.roll(x, ROPE_PAIRS, 1))
    return x * cos + partner * sin


def _prep_kernel(rope, *refs):
    if rope:
        dq_ref, dk_ref, qw_ref, kw_ref, ones_ref, cos_ref, sin_ref, q_out, k_out = refs
    else:
        dq_ref, dk_ref, qw_ref, kw_ref, ones_ref, dv_ref, q_out, k_out, kf_out, vf_out = refs
        vf_out[...] = dv_ref[...].astype(F32)
    ones = ones_ref[...]
    q = _group_rms(dq_ref[...].astype(F32), qw_ref[...], ones)
    k = _group_rms(dk_ref[...].astype(F32), kw_ref[...], ones)
    if rope:
        cos = jnp.concatenate([cos_ref[...]] * 4, axis=1)
        sin = jnp.concatenate([sin_ref[...]] * 4, axis=1)
        q = _rope(q, cos, sin)
        k = _rope(k, cos, sin)
    else:
        kf_out[...] = k
    q_out[...] = (q * (DA_QK ** -0.5 * math.log2(math.e))).astype(BF16)
    k_out[...] = k.astype(BF16)


def _prep_call(geom, latent, proj, qw, kw, ones, cos=None, sin=None):
    tm = 512
    w = DA_HEADS * 2 * DA_QK
    if latent:
        n, off = geom.n_lat, geom.n_ctx // tm
        per = geom.dec_seq // tm
    else:
        n, off = geom.n_ctx, 0
    in_specs = [pl.BlockSpec((tm, w), lambda i: (i + off, C_DQ // w)),
                pl.BlockSpec((tm, w), lambda i: (i + off, C_DK // w)),
                pl.BlockSpec((1, w), lambda i: (0, 0)),
                pl.BlockSpec((1, w), lambda i: (0, 0)),
                pl.BlockSpec((w, w), lambda i: (0, 0))]
    args = [proj, proj, qw, kw, ones]
    out_specs = [pl.BlockSpec((tm, w), lambda i: (i, 0)), pl.BlockSpec((tm, w), lambda i: (i, 0))]
    out_shape = [jax.ShapeDtypeStruct((n, w), BF16), jax.ShapeDtypeStruct((n, w), BF16)]
    if latent:
        in_specs += [pl.BlockSpec((tm, 2 * DA_QK), lambda i: (i % per, 0)),
                     pl.BlockSpec((tm, 2 * DA_QK), lambda i: (i % per, 0))]
        args += [cos, sin]
    else:
        in_specs.append(pl.BlockSpec((tm, w), lambda i: (i, C_DV // w)))
        args.append(proj)
        out_specs += [pl.BlockSpec((tm, w), lambda i: (i, 0)), pl.BlockSpec((tm, w), lambda i: (i, 0))]
        out_shape += [jax.ShapeDtypeStruct((n, w), F32), jax.ShapeDtypeStruct((n, w), F32)]
    return pl.pallas_call(
        functools.partial(_prep_kernel, latent),
        grid=(n // tm,),
        in_specs=in_specs, out_specs=out_specs, out_shape=out_shape,
        compiler_params=_cparams(("arbitrary",)),
        name="qk_prep_lat" if latent else "qk_prep_ctx",
    )(*args)


ATT_KC = 256
ATT_TQ = 256
LOG2E = math.log2(math.e)
ATT_SAFE_LOGIT = 60.0


def _attn_kernel(out_scale, has_cache, *refs):
    if has_cache:
        par_ref, q_ref, kc_ref, vc_ref, kl_ref, vl_ref, sw_ref, o_ref, e_scr, o_scr = refs
        srcs = [(kc_ref, vc_ref), (kl_ref, vl_ref)]
    else:
        par_ref, q_ref, kl_ref, vl_ref, sw_ref, o_ref, e_scr, o_scr = refs
        srcs = [(kl_ref, vl_ref)]
    chunks = [(kr, vr, st) for kr, vr in srcs for st in range(0, kr.shape[0], ATT_KC)]
    lam = par_ref[0]
    no_shift = par_ref[1] > 0.5
    tqs = ATT_TQ
    nsub = q_ref.shape[0] // tqs
    nt = (((1,), (1,)), ((), ()))
    half = ATT_KC // 2

    def stacked_q(sb):
        q = q_ref[sb * tqs:(sb + 1) * tqs, :]
        lane = lax.broadcasted_iota(jnp.int32, q.shape, 1)
        zero = jnp.zeros_like(q)
        return jnp.concatenate([jnp.where(lane < DA_QK, q, zero), jnp.where(lane >= DA_QK, q, zero)], axis=0)

    def logits(qq, c):
        kr, vr, st = chunks[c]
        return lax.dot_general(qq, kr[st:st + ATT_KC, :], nt, preferred_element_type=F32)

    def fold(total, e):
        part = e[:, :half] + e[:, half:]
        return part if total is None else total + part

    def row_stats(lsum):
        l = jnp.sum(lsum, axis=-1, keepdims=True)
        l1 = l[0:tqs]
        return l1, lam * l1 / l[tqs:2 * tqs]

    def pv(acc, buf, c, rho):
        kr, vr, st = chunks[c]
        w = (e_scr[buf, c, 0:tqs, :] - rho * e_scr[buf, c, tqs:2 * tqs, :]).astype(BF16)
        t = jnp.dot(w, vr[st:st + ATT_KC, :], preferred_element_type=F32)
        return t if acc is None else acc + t

    nck = len(chunks)

    @pl.when(no_shift)
    def _():
        stats = None
        for sb in range(nsub + 1):
            qq = stacked_q(sb) if sb < nsub else None
            lsum, acc = None, None
            for c in range(nck):
                if sb < nsub:
                    e = jnp.exp2(logits(qq, c))
                    e_scr[sb % 2, c] = e
                    lsum = fold(lsum, e)
                if sb > 0:
                    acc = pv(acc, (sb - 1) % 2, c, stats[1])
            if sb > 0:
                o_scr[(sb - 1) * tqs:sb * tqs, :] = acc / stats[0]
            if sb < nsub:
                stats = row_stats(lsum)

    @pl.when(jnp.logical_not(no_shift))
    def _():
        for sb in range(nsub):
            qq = stacked_q(sb)
            m = None
            for c in range(nck):
                s = logits(qq, c)
                e_scr[0, c] = s
                mc = jnp.max(s, axis=-1, keepdims=True)
                m = mc if m is None else jnp.maximum(m, mc)
            lsum = None
            for c in range(nck):
                e = jnp.exp2(e_scr[0, c] - m)
                e_scr[0, c] = e
                lsum = fold(lsum, e)
            l1, rho = row_stats(lsum)
            acc = None
            for c in range(nck):
                acc = pv(acc, 0, c, rho)
            o_scr[sb * tqs:(sb + 1) * tqs, :] = acc / l1

    o = o_scr[...]
    y = o * lax.rsqrt(jnp.mean(o * o, axis=-1, keepdims=True) + EPS) * sw_ref[...]
    o_ref[...] = (y * out_scale).astype(BF16)


def _attn_call(par, lam_init, q2d, k2d, proj, v_row_off, n_b, t_q, t_kl, tq, subln_w, cache=None):
    hw = 2 * DA_QK
    nq = t_q // tq
    vcol = C_DV // DA_V
    in_specs = [pl.BlockSpec(memory_space=pltpu.SMEM),
                pl.BlockSpec((tq, hw), lambda b, h, qi: (b * nq + qi, h))]
    args = [par, q2d]
    n_chunks = t_kl // ATT_KC
    if cache is not None:
        kc, vc = cache
        p = kc.shape[1]
        n_chunks += p // ATT_KC
        in_specs += [pl.BlockSpec((None, p, hw), lambda b, h, qi: (b, 0, h)),
                     pl.BlockSpec((None, p, DA_V), lambda b, h, qi: (b, 0, h))]
        args += [kc, vc]
    in_specs += [pl.BlockSpec((t_kl, hw), lambda b, h, qi: (b, h)),
                 pl.BlockSpec((t_kl, DA_V), lambda b, h, qi: (v_row_off + b, vcol + h)),
                 pl.BlockSpec((1, DA_V), lambda b, h, qi: (0, 0))]
    args += [k2d, proj, subln_w.reshape(1, DA_V)]
    return pl.pallas_call(
        functools.partial(_attn_kernel, 1.0 - lam_init, cache is not None),
        grid=(n_b, DA_HEADS, nq),
        in_specs=in_specs,
        out_specs=pl.BlockSpec((tq, DA_V), lambda b, h, qi: (b * nq + qi, h)),
        out_shape=jax.ShapeDtypeStruct((n_b * t_q, DA_HEADS * DA_V), BF16),
        scratch_shapes=[pltpu.VMEM((2 if tq > ATT_TQ else 1, n_chunks, 2 * ATT_TQ, ATT_KC), F32),
                        pltpu.VMEM((tq, DA_V), F32)],
        compiler_params=_cparams(("arbitrary", "arbitrary", "arbitrary")),
        name="diff_attn_lat" if cache is not None else "diff_attn_ctx",
    )(*args)


def _ret_state_update(kt, v, kd, cd, s_old):
    parts = []
    for h in range(RET_HEADS):
        rows = slice(h * RET_QK, (h + 1) * RET_QK)
        kh = (kt[rows, :].astype(F32) * kd[rows, :]).astype(BF16)
        parts.append(jnp.dot(kh, v[:, h * RET_V:(h + 1) * RET_V], preferred_element_type=F32))
    return cd * s_old + jnp.concatenate(parts, axis=0)


def _ret_bwd_kernel(geom, kt_ref, v_ref, kd_ref, cd_ref, s0_ref, sstart_ref, send_ref, s_scr):
    i = geom.n_blocks - 1 - pl.program_id(0)

    @pl.when(geom.seq_end(i))
    def _():
        s_scr[...] = s0_ref[...]

    s_old = s_scr[...]
    sstart_ref[...] = s_old
    kt = kt_ref[...] * jnp.asarray(RET_QK ** -0.5, BF16)
    s_new = _ret_state_update(kt, v_ref[...], kd_ref[...], cd_ref[...], s_old)
    s_scr[...] = s_new
    send_ref[...] = s_new


def _ret_bwd_call(geom, proj, rkt, kd_b, cd_b, s0):
    nb = geom.n_blocks
    hs = RET_HEADS * RET_QK

    def blk(g):
        return nb - 1 - g

    return pl.pallas_call(
        functools.partial(_ret_bwd_kernel, geom),
        grid=(nb,),
        in_specs=[pl.BlockSpec((hs, BLK), lambda g: (0, blk(g))),
                  pl.BlockSpec((BLK, RET_HEADS * RET_V), lambda g: (blk(g), C_RV // (RET_HEADS * RET_V))),
                  pl.BlockSpec((hs, BLK), lambda g: (0, 0)),
                  pl.BlockSpec((hs, RET_V), lambda g: (0, 0)),
                  pl.BlockSpec((None, None, hs, RET_V), lambda g: (geom.seq_id(blk(g)), 1, 0, 0))],
        out_specs=[pl.BlockSpec((None, hs, RET_V), lambda g: (blk(g), 0, 0)),
                   pl.BlockSpec((None, hs, RET_V), lambda g: (blk(g), 0, 0))],
        out_shape=[jax.ShapeDtypeStruct((nb, hs, RET_V), F32),
                   jax.ShapeDtypeStruct((nb, hs, RET_V), F32)],
        scratch_shapes=[pltpu.VMEM((hs, RET_V), F32)],
        compiler_params=_cparams(("arbitrary",)),
        name="ret_bwd_state",
    )(rkt, proj, kd_b, cd_b, s0)


def _ret_main_kernel(geom, q_ref, kt_ref, v_ref, g_ref, dsum_ref, qdf_ref, qdb_ref, kd_ref, cd_ref,
                     s0_ref, sb_ref, o_ref, send_ref, s_scr):
    i = pl.program_id(0)

    @pl.when(geom.seq_start(i))
    def _():
        s_scr[...] = s0_ref[...]

    s_f = s_scr[...]
    s_fb = s_f.astype(BF16)
    s_bb = sb_ref[...].astype(BF16)
    q = q_ref[...].astype(F32)
    kt = kt_ref[...] * jnp.asarray(RET_QK ** -0.5, BF16)
    v = v_ref[...]
    lane = lax.broadcasted_iota(jnp.int32, q.shape, 1)
    for h in range(RET_HEADS):
        in_head = (lane >= h * RET_QK) & (lane < (h + 1) * RET_QK)
        qh = jnp.where(in_head, q, 0.0)
        vh = v[:, h * RET_V:(h + 1) * RET_V]
        sc = jnp.dot(qh.astype(BF16), kt, preferred_element_type=F32) * dsum_ref[h]
        o = jnp.dot(sc.astype(BF16), vh, preferred_element_type=F32)
        o += jnp.dot((qh * qdf_ref[...]).astype(BF16), s_fb, preferred_element_type=F32)
        o += jnp.dot((qh * qdb_ref[...]).astype(BF16), s_bb, preferred_element_type=F32)
        y = o * lax.rsqrt(jnp.mean(o * o, axis=-1, keepdims=True) + EPS)
        gv = g_ref[:, h * RET_V:(h + 1) * RET_V].astype(F32)
        o_ref[:, h * RET_V:(h + 1) * RET_V] = (y * (gv * jax.nn.sigmoid(gv))).astype(BF16)
    s_new = _ret_state_update(kt, v, kd_ref[...], cd_ref[...], s_f)
    s_scr[...] = s_new
    send_ref[...] = s_new


def _ret_main_call(geom, proj, rkt, dsum, qdf, qdb, kd_f, cd_f, s0, sb_start):
    nb = geom.n_blocks
    hs = RET_HEADS * RET_QK
    hv = RET_HEADS * RET_V
    return pl.pallas_call(
        functools.partial(_ret_main_kernel, geom),
        grid=(nb,),
        in_specs=[pl.BlockSpec((BLK, hs), lambda g: (g, C_RQ // hs)),
                  pl.BlockSpec((hs, BLK), lambda g: (0, g)),
                  pl.BlockSpec((BLK, hv), lambda g: (g, C_RV // hv)),
                  pl.BlockSpec((BLK, hv), lambda g: (g, C_RG // hv)),
                  pl.BlockSpec((RET_HEADS, BLK, BLK), lambda g: (0, 0, 0)),
                  pl.BlockSpec((BLK, hs), lambda g: (0, 0)),
                  pl.BlockSpec((BLK, hs), lambda g: (0, 0)),
                  pl.BlockSpec((hs, BLK), lambda g: (0, 0)),
                  pl.BlockSpec((hs, RET_V), lambda g: (0, 0)),
                  pl.BlockSpec((None, None, hs, RET_V), lambda g: (geom.seq_id(g), 0, 0, 0)),
                  pl.BlockSpec((None, hs, RET_V), lambda g: (g, 0, 0))],
        out_specs=[pl.BlockSpec((BLK, hv), lambda g: (g, 0)),
                   pl.BlockSpec((None, hs, RET_V), lambda g: (g, 0, 0))],
        out_shape=[jax.ShapeDtypeStruct((geom.n_tok, hv), BF16),
                   jax.ShapeDtypeStruct((nb, hs, RET_V), F32)],
        scratch_shapes=[pltpu.VMEM((hs, RET_V), F32)],
        compiler_params=_cparams(("arbitrary",)),
        name="ret_main",
    )(proj, rkt, proj, proj, dsum, qdf, qdb, kd_f, cd_f, s0, sb_start)


def _ret_tables(ret_decay_l):
    log_g = jax.nn.log_sigmoid(ret_decay_l.astype(F32))
    pos = jnp.arange(BLK, dtype=F32)
    diff = pos[:, None] - pos[None, :]
    lf = log_g[0][:, None, None]
    lb = log_g[1][:, None, None]
    dsum = (jnp.where(diff >= 0, jnp.exp(jnp.maximum(diff, 0.0)[None] * lf), 0.0)
            + jnp.where(diff <= 0, jnp.exp(jnp.maximum(-diff, 0.0)[None] * lb), 0.0))

    def per_lane(e, lg):
        return jnp.repeat(jnp.exp(e[:, None] * lg[None, :]), RET_QK, axis=1)

    qdf = per_lane(pos + 1.0, log_g[0])
    qdb = per_lane(BLK - pos, log_g[1])
    kd_f = per_lane(BLK - 1.0 - pos, log_g[0]).T
    kd_b = per_lane(pos, log_g[1]).T
    cd_f = jnp.broadcast_to(jnp.repeat(jnp.exp(BLK * log_g[0]), RET_QK)[:, None], (RET_HEADS * RET_QK, RET_V))
    cd_b = jnp.broadcast_to(jnp.repeat(jnp.exp(BLK * log_g[1]), RET_QK)[:, None], (RET_HEADS * RET_QK, RET_V))
    return dsum, qdf, qdb, kd_f, kd_b, cd_f, cd_b


def _merge_kernel(n_ctx_tiles, ba_ref, bbc_ref, bbl_ref, bc_ref, g0_ref, g1_ref, g2_ref, xc_ref, xl_ref,
                  gate_ref, sc_ref, sh_ref, nw_ref, wb_ref, wo_ref, rhi_ref, rlo_ref, x1_ref, h2_ref, lt_ref):
    branches = (ba_ref[...], _pick_part(n_ctx_tiles, bbc_ref, bbl_ref), bc_ref[...])
    acc = None
    for br, (b, g_ref) in enumerate(zip(branches, (g0_ref, g1_ref, g2_ref))):
        p = jnp.dot(b, wb_ref[br], preferred_element_type=F32)
        t = jax.nn.sigmoid(g_ref[...].astype(F32)) * p
        acc = t if acc is None else acc + t
    m = jnp.dot(acc.astype(BF16), wo_ref[...], preferred_element_type=F32)
    x1 = _pick_part(n_ctx_tiles, xc_ref, xl_ref) + gate_ref[...] * m
    x1_ref[...] = x1
    ms = jnp.mean(x1 * x1, axis=-1, keepdims=True)
    h2 = x1 * lax.rsqrt(ms + EPS) * nw_ref[...] * (1.0 + sc_ref[...]) + sh_ref[...]
    h2b = h2.astype(BF16)
    h2_ref[...] = h2b
    h2lo = (h2 - h2b.astype(F32)).astype(BF16)
    nt = (((1,), (1,)), ((), ()))
    lt_ref[...] = (lax.dot_general(rhi_ref[...], h2b, nt, preferred_element_type=F32)
                   + lax.dot_general(rhi_ref[...], h2lo, nt, preferred_element_type=F32)
                   + lax.dot_general(rlo_ref[...], h2b, nt, preferred_element_type=F32))


def _merge_call(geom, l, ba, bb_ctx, bb_lat, bc, proj, x_ctx, x_lat, mod6, norm2_w, wb_bf, wo_bf, r_hi, r_lo):
    tm = 512
    gcol = C_GL // D_MODEL
    full = lambda shape: pl.BlockSpec(shape, lambda i: tuple(0 for _ in shape))
    tok = lambda w: pl.BlockSpec((tm, w), lambda i: (i, 0))
    return pl.pallas_call(
        functools.partial(_merge_kernel, geom.n_ctx // tm),
        grid=(geom.n_tok // tm,),
        in_specs=[tok(BRANCH_W)] + _split_in_specs(geom, tm, BRANCH_W, 1) + [tok(BRANCH_W),
                  pl.BlockSpec((tm, D_MODEL), lambda i: (i, gcol)),
                  pl.BlockSpec((tm, D_MODEL), lambda i: (i, gcol + 1)),
                  pl.BlockSpec((tm, D_MODEL), lambda i: (i, gcol + 2))]
                 + _split_in_specs(geom, tm, D_MODEL, 1) + [
                  _mod_spec(geom, l, 2, tm, 1), _mod_spec(geom, l, 4, tm, 1), _mod_spec(geom, l, 3, tm, 1),
                  full((1, D_MODEL)),
                  full((N_BRANCH, BRANCH_W, D_MODEL)), full((D_MODEL, D_MODEL)),
                  full((N_EXPERTS, D_MODEL)), full((N_EXPERTS, D_MODEL))],
        out_specs=[tok(D_MODEL), tok(D_MODEL), pl.BlockSpec((N_EXPERTS, tm), lambda i: (0, i))],
        out_shape=[jax.ShapeDtypeStruct((geom.n_tok, D_MODEL), F32),
                   jax.ShapeDtypeStruct((geom.n_tok, D_MODEL), BF16),
                   jax.ShapeDtypeStruct((N_EXPERTS, geom.n_tok), F32)],
        compiler_params=_cparams(("arbitrary",)),
        name="merge_out",
    )(ba, bb_ctx, bb_lat, bc, proj, proj, proj, x_ctx, x_lat, mod6, mod6, mod6,
      norm2_w.reshape(1, D_MODEL), wb_bf, wo_bf, r_hi, r_lo)


def _router_kernel(lt_ref, bias_ref, g_ref):
    per = N_EXPERTS // N_GROUPS
    tm = lt_ref.shape[1]
    scores = jax.nn.sigmoid(lt_ref[...])
    biased = scores + bias_ref[...]
    b3 = biased.reshape(N_GROUPS, per, tm)
    neg = jnp.float32(-jnp.inf)
    m1 = jnp.max(b3, axis=1, keepdims=True)
    is_m1 = b3 == m1
    cnt = jnp.sum(is_m1.astype(F32), axis=1, keepdims=True)
    m2 = jnp.max(jnp.where(is_m1, neg, b3), axis=1, keepdims=True)
    grp = (m1 + jnp.where(cnt >= 2.0, m1, m2)).reshape(N_GROUPS, tm)
    gidx = lax.broadcasted_iota(jnp.int32, (N_GROUPS, tm), 0)
    grank = jnp.zeros((N_GROUPS, tm), F32)
    for g2 in range(N_GROUPS):
        other = grp[g2:g2 + 1, :]
        ahead = (other > grp) | ((other == grp) & (gidx > g2))
        grank += ahead.astype(F32)
    gsel = (grank < float(TOPK_GROUPS)).astype(F32)
    emask = jnp.broadcast_to(gsel.reshape(N_GROUPS, 1, tm), (N_GROUPS, per, tm)).reshape(N_EXPERTS, tm)
    masked = jnp.where(emask > 0.0, biased, neg)
    eidx = lax.broadcasted_iota(jnp.int32, (N_EXPERTS, tm), 0)
    erank = jnp.zeros((N_EXPERTS, tm), F32)
    for e2 in range(N_EXPERTS):
        other = masked[e2:e2 + 1, :]
        ahead = (other > masked) | ((other == masked) & (eidx > e2))
        erank += ahead.astype(F32)
    w = jnp.where(erank < float(TOP_K), scores, 0.0)
    gates_t = w / jnp.sum(w, axis=0, keepdims=True) * ROUTED_SCALE
    pad = jnp.zeros((GATE_W - N_EXPERTS, tm), F32)
    g_ref[...] = jnp.concatenate([gates_t, pad], axis=0).T


def _router_call(geom, logits_t, bias):
    tm = 512
    return pl.pallas_call(
        _router_kernel,
        grid=(geom.n_tok // tm,),
        in_specs=[pl.BlockSpec((N_EXPERTS, tm), lambda i: (0, i)),
                  pl.BlockSpec((N_EXPERTS, 1), lambda i: (0, 0))],
        out_specs=pl.BlockSpec((tm, GATE_W), lambda i: (i, 0)),
        out_shape=jax.ShapeDtypeStruct((geom.n_tok, GATE_W), F32),
        compiler_params=_cparams(("arbitrary",)),
        name="router",
    )(logits_t, bias.reshape(N_EXPERTS, 1))


MOE_EB = 8
MOE_TM = 512


def _moe_kernel(n_ctx_tiles, h_ref, g_ref, gu_ref, dn_ref, sgu_ref, sdn_ref, x1_ref, gate_ref,
                oc_ref, ol_ref, acc_scr):
    i = pl.program_id(0)
    j = pl.program_id(1)
    h = h_ref[...]

    def expert(gu, dn, ge):
        a = jnp.dot(h, gu, preferred_element_type=F32)
        hg = a[:, :D_EXPERT]
        act = (hg * jax.nn.sigmoid(hg)) * a[:, D_EXPERT:]
        if ge is not None:
            act = act * ge
        return jnp.dot(act.astype(BF16), dn, preferred_element_type=F32)

    @pl.when(j == 0)
    def _():
        acc_scr[...] = expert(sgu_ref[...], sdn_ref[...], None)

    gts = g_ref[...]
    lane = lax.broadcasted_iota(jnp.int32, gts.shape, 1)
    acc = acc_scr[...]
    for e in range(MOE_EB):
        ge = jnp.sum(jnp.where(lane == j * MOE_EB + e, gts, 0.0), axis=1, keepdims=True)
        acc += expert(gu_ref[e], dn_ref[e], ge)
    acc_scr[...] = acc

    @pl.when(j == pl.num_programs(1) - 1)
    def _():
        y = x1_ref[...] + gate_ref[...] * acc

        @pl.when(i < n_ctx_tiles)
        def _():
            oc_ref[...] = y

        @pl.when(i >= n_ctx_tiles)
        def _():
            ol_ref[...] = y


def _moe_call(geom, l, h2, gates, gu_bf, dn_bf, sgu_bf, sdn_bf, x1, mod6):
    tm = MOE_TM
    nct = geom.n_ctx // tm
    return pl.pallas_call(
        functools.partial(_moe_kernel, nct),
        grid=(geom.n_tok // tm, N_EXPERTS // MOE_EB),
        in_specs=[pl.BlockSpec((tm, D_MODEL), lambda i, j: (i, 0)),
                  pl.BlockSpec((tm, GATE_W), lambda i, j: (i, 0)),
                  pl.BlockSpec((None, MOE_EB, D_MODEL, 2 * D_EXPERT), lambda i, j: (l, j, 0, 0)),
                  pl.BlockSpec((None, MOE_EB, D_EXPERT, D_MODEL), lambda i, j: (l, j, 0, 0)),
                  pl.BlockSpec((None, D_MODEL, 2 * D_EXPERT), lambda i, j: (l, 0, 0)),
                  pl.BlockSpec((None, D_EXPERT, D_MODEL), lambda i, j: (l, 0, 0)),
                  pl.BlockSpec((tm, D_MODEL), lambda i, j: (i, 0)),
                  _mod_spec(geom, l, 5, tm, 2)],
        out_specs=[pl.BlockSpec((tm, D_MODEL), lambda i, j: (jnp.minimum(i, nct - 1), 0)),
                   pl.BlockSpec((tm, D_MODEL), lambda i, j: (jnp.maximum(i - nct, 0), 0))],
        out_shape=[jax.ShapeDtypeStruct((geom.n_ctx, D_MODEL), F32),
                   jax.ShapeDtypeStruct((geom.n_lat, D_MODEL), F32)],
        scratch_shapes=[pltpu.VMEM((tm, D_MODEL), F32)],
        compiler_params=_cparams(("arbitrary", "arbitrary")),
        name="moe_experts",
    )(h2, gates, gu_bf, dn_bf, sgu_bf, sdn_bf, x1, mod6)


def _rope_tables(dec_seq):
    rows = dec_seq // GRID_W
    row = jnp.repeat(jnp.arange(rows, dtype=F32), GRID_W)
    col = jnp.tile(jnp.arange(GRID_W, dtype=F32), rows)
    inv = ROPE_BASE ** (-jnp.arange(ROPE_PAIRS, dtype=F32) / ROPE_PAIRS)
    ar = row[:, None] * inv[None, :]
    ac = col[:, None] * inv[None, :]
    cos64 = jnp.concatenate([jnp.cos(ar), jnp.cos(ar), jnp.cos(ac), jnp.cos(ac)], axis=1)
    sin64 = jnp.concatenate([-jnp.sin(ar), jnp.sin(ar), -jnp.sin(ac), jnp.sin(ac)], axis=1)
    return jnp.tile(cos64, (1, 2)), jnp.tile(sin64, (1, 2))


def _block_diag_gate(wg_dir):
    eye = jnp.eye(LRU_BLOCKS, dtype=F32)
    dense = jnp.einsum('gnij,nm->gnimj', wg_dir.astype(F32), eye).reshape(2, D_RNN, D_RNN)
    return jnp.concatenate([dense[0], dense[1]], axis=1)


def kernel(x_prompt, x_sample, cache_k, cache_v, state_lru, state_ret, c, c_ctx, ada_w, ada_b, norm1_w, norm2_w, w_in, conv_w, conv_b, lru_gate_w, lru_gate_b, lru_lambda, q_norm_w, k_norm_w, diff_lambda, subln_w, ret_decay, w_branch, w_out, router_w, router_bias, w_exp_gu, w_exp_down, w_sh_gu, w_sh_down):
    batch, seq, _ = x_prompt.shape
    dec_batch, dec_seq, _ = x_sample.shape
    assert 1 + dec_batch <= MOD_ROWS
    geom = _Geom(batch, seq, dec_batch, dec_seq)
    hs = RET_HEADS * RET_QK
    aw = DA_HEADS * 2 * DA_QK

    x_ctx = x_prompt.reshape(geom.n_ctx, D_MODEL)
    x_lat = x_sample.reshape(geom.n_lat, D_MODEL)
    cvec = jnp.zeros((MOD_ROWS, D_MODEL), F32).at[0].set(c_ctx).at[1:1 + dec_batch].set(c)
    mod6 = _ada_call(cvec, ada_w, ada_b).reshape(DEPTH, MOD_ROWS, 6, 1, D_MODEL)

    ones_bd = jnp.kron(jnp.eye(aw // DA_QK, dtype=F32), jnp.ones((DA_QK, DA_QK), F32)).astype(BF16)
    cos_t, sin_t = _rope_tables(dec_seq)

    w_in_bf = w_in.astype(BF16)
    gu_bf, dn_bf = w_exp_gu.astype(BF16), w_exp_down.astype(BF16)
    sgu_bf, sdn_bf = w_sh_gu.astype(BF16), w_sh_down.astype(BF16)

    ks, vs, lrus, rets = [], [], [], []
    for l in range(DEPTH):
        lam_init = 0.8 - 0.6 * math.exp(-0.3 * l)
        w_rkt_bf = w_in[l][:, C_RK:C_RK + hs].T.astype(BF16)
        proj, rkt = _inproj_call(geom, l, x_ctx, x_lat, mod6, norm1_w[l], w_in_bf, w_rkt_bf)

        sp = jax.nn.softplus(-lru_lambda[l].astype(F32))
        h0 = jnp.concatenate([jnp.zeros((batch, 2, D_RNN), F32), state_lru[:, l].astype(F32)], axis=0)
        h0 = h0.reshape(geom.n_seq, 2, 1, D_RNN)
        cb = conv_b[l].reshape(1, D_RNN)
        lru_args = []
        for d in range(2):
            lru_args.append((_block_diag_gate(lru_gate_w[l, d]).astype(BF16),
                             lru_gate_b[l, d].reshape(1, 2 * D_RNN), sp[d].reshape(1, D_RNN)))
        hf, hf_last = _lru_call(geom, False, proj, conv_w[l], cb, *lru_args[0], h0)
        branch_a, hb_last = _lru_call(geom, True, proj, conv_w[l], cb, *lru_args[1], h0, hf)

        qw = jnp.tile(q_norm_w[l], aw // DA_QK).reshape(1, aw)
        kw = jnp.tile(k_norm_w[l], aw // DA_QK).reshape(1, aw)
        q_c, k_c, k_c32, v_c32 = _prep_call(geom, False, proj, qw, kw, ones_bd)
        q_l, k_l = _prep_call(geom, True, proj, qw, kw, ones_bd, cos_t, sin_t)
        lam_p = diff_lambda[l].astype(F32)
        lam = jnp.exp(jnp.sum(lam_p[0] * lam_p[1])) - jnp.exp(jnp.sum(lam_p[2] * lam_p[3])) + lam_init
        q_bound = DA_QK * jnp.max(jnp.square(q_norm_w[l].astype(F32))) * (DA_QK ** -0.5 * LOG2E) ** 2
        k_bound = DA_QK * jnp.max(jnp.square(k_norm_w[l].astype(F32)))
        kc32 = cache_k[:, l].astype(F32)
        kc_bound = jnp.maximum(k_bound, jnp.max(jnp.sum(jnp.square(kc32), axis=-1)))

        def attn_par(kb):
            ok = (q_bound * kb * 1.05 < ATT_SAFE_LOGIT ** 2).astype(F32)
            return jnp.stack([lam, ok])

        assert geom.n_ctx % dec_seq == 0
        cache = (kc32.reshape(dec_batch, -1, aw).astype(BF16),
                 cache_v[:, l].reshape(dec_batch, -1, DA_HEADS * DA_V).astype(BF16))
        att_c = _attn_call(attn_par(k_bound), lam_init, q_c, k_c, proj, 0, batch, seq, seq, 256, subln_w[l])
        att_l = _attn_call(attn_par(kc_bound), lam_init, q_l, k_l, proj, geom.n_ctx // dec_seq, dec_batch,
                           dec_seq, dec_seq, 4 * ATT_TQ, subln_w[l], cache)

        dsum, qdf, qdb, kd_f, kd_b, cd_f, cd_b = _ret_tables(ret_decay[l])
        s0 = jnp.concatenate([jnp.zeros((batch, 2, hs, RET_V), F32),
                              state_ret[:, l].astype(F32).reshape(dec_batch, 2, hs, RET_V)], axis=0)
        sb_start, sb_end = _ret_bwd_call(geom, proj, rkt, kd_b, cd_b, s0)
        branch_c, sf_end = _ret_main_call(geom, proj, rkt, dsum, qdf, qdb, kd_f, cd_f, s0, sb_start)

        r_t = router_w[l].T.astype(F32)
        r_hi = r_t.astype(BF16)
        r_lo = (r_t - r_hi.astype(F32)).astype(BF16)
        x1, h2, logits_t = _merge_call(geom, l, branch_a, att_c, att_l, branch_c, proj, x_ctx, x_lat, mod6,
                                       norm2_w[l], w_branch[l].astype(BF16), w_out[l].astype(BF16), r_hi, r_lo)
        gates = _router_call(geom, logits_t, router_bias[l].astype(F32))
        x_ctx, x_lat = _moe_call(geom, l, h2, gates, gu_bf, dn_bf, sgu_bf, sdn_bf, x1, mod6)

        ks.append(k_c32.reshape(batch, seq, DA_HEADS, 2, DA_QK))
        vs.append(v_c32.reshape(batch, seq, DA_HEADS, DA_V))
        lrus.append(jnp.stack([hf_last[:batch, 0], hb_last[:batch, 0]], axis=1))
        rets.append(jnp.stack([sf_end[:batch].reshape(batch, RET_HEADS, RET_QK, RET_V),
                               sb_end[:batch].reshape(batch, RET_HEADS, RET_QK, RET_V)], axis=1))

    y_prompt = x_ctx.reshape(batch, seq, D_MODEL)
    y_sample = x_lat.reshape(dec_batch, dec_seq, D_MODEL)
    return (y_prompt, y_sample, jnp.stack(ks, axis=1), jnp.stack(vs, axis=1),
            jnp.stack(lrus, axis=1), jnp.stack(rets, axis=1))
```

```python
import functools
import math

import numpy as np
import jax
import jax.numpy as jnp
from jax import lax
from jax.experimental import pallas as pl
from jax.experimental.pallas import tpu as pltpu

F32 = jnp.float32
BF16 = jnp.bfloat16

D_MODEL = 1024
DEPTH = 2
GRID_W = 64
D_RNN = 512
LRU_BLOCKS = 8
LRU_BLOCK = D_RNN // LRU_BLOCKS
CONV_W = 4
LRU_C = 8.0
DA_HEADS = 4
DA_QK = 64
DA_V = 128
ROPE_PAIRS = DA_QK // 4
ROPE_BASE = 10000.0
RET_HEADS = 4
RET_QK = 64
RET_V = 128
BRANCH_W = 512
N_BRANCH = 3
D_IN = 7168
N_EXPERTS = 64
TOP_K = 8
N_GROUPS = 8
TOPK_GROUPS = 4
D_EXPERT = 256
ROUTED_SCALE = 2.5
EPS = 1e-6

C_XA, C_GA, C_DQ, C_DK, C_DV = 0, 512, 1024, 1536, 2048
C_RQ, C_RK, C_RV, C_RG, C_GL = 2560, 2816, 3072, 3584, 4096

BLK = 256
LRU_SUB = 8
LRU_LANES = D_RNN // 128
GATE_W = 128
MOD_ROWS = 8
VMEM_LIMIT = 56 * 1024 * 1024


def _cparams(sem, vmem_limit=VMEM_LIMIT):
    return pltpu.CompilerParams(dimension_semantics=sem, vmem_limit_bytes=vmem_limit)


class _Geom:
    def __init__(self, batch, seq, dec_batch, dec_seq):
        assert seq == BLK and dec_seq % BLK == 0
        self.batch, self.seq, self.dec_batch, self.dec_seq = batch, seq, dec_batch, dec_seq
        self.n_ctx = batch * seq
        self.n_lat = dec_batch * dec_seq
        self.n_tok = self.n_ctx + self.n_lat
        self.ctx_blocks = self.n_ctx // BLK
        self.lat_blocks = dec_seq // BLK
        self.n_blocks = self.n_tok // BLK
        self.n_seq = batch + dec_batch

    def mod_row(self, i, tile):
        nct = self.n_ctx // tile
        per = self.dec_seq // tile
        return jnp.where(i < nct, 0, 1 + (i - nct) // per)

    def seq_id(self, i):
        return jnp.where(i < self.ctx_blocks, i, self.ctx_blocks + (i - self.ctx_blocks) // self.lat_blocks)

    def seq_start(self, i):
        return jnp.logical_or(i < self.ctx_blocks, (i - self.ctx_blocks) % self.lat_blocks == 0)

    def seq_end(self, i):
        return jnp.logical_or(i < self.ctx_blocks, (i - self.ctx_blocks) % self.lat_blocks == self.lat_blocks - 1)


def _ada_kernel(c_ref, w_ref, b_ref, o_ref):
    cv = c_ref[...]
    s = cv * jax.nn.sigmoid(cv)
    o_ref[...] = jnp.dot(s, w_ref[...], preferred_element_type=F32,
                         precision=lax.Precision.HIGHEST) + b_ref[...]


def _ada_call(cvec, ada_w, ada_b):
    depth = ada_w.shape[0]
    nt = 6
    return pl.pallas_call(
        _ada_kernel,
        grid=(depth, nt),
        in_specs=[pl.BlockSpec((MOD_ROWS, D_MODEL), lambda l, j: (0, 0)),
                  pl.BlockSpec((None, D_MODEL, D_MODEL), lambda l, j: (l, 0, j)),
                  pl.BlockSpec((None, 1, D_MODEL), lambda l, j: (l, 0, j))],
        out_specs=pl.BlockSpec((None, MOD_ROWS, D_MODEL), lambda l, j: (l, 0, j)),
        out_shape=jax.ShapeDtypeStruct((depth, MOD_ROWS, 6 * D_MODEL), F32),
        compiler_params=_cparams(("arbitrary", "arbitrary")),
        name="ada_mod",
    )(cvec, ada_w, ada_b.reshape(depth, 1, 6 * D_MODEL))


def _mod_spec(geom, l, which, tile, ngrid):
    if ngrid == 1:
        return pl.BlockSpec((None, None, None, 1, D_MODEL),
                            lambda i: (l, geom.mod_row(i, tile), which, 0, 0))
    return pl.BlockSpec((None, None, None, 1, D_MODEL),
                        lambda i, j: (l, geom.mod_row(i, tile), which, 0, 0))


def _split_in_specs(geom, tile, width, ngrid):
    nct = geom.n_ctx // tile
    if ngrid == 1:
        return [pl.BlockSpec((tile, width), lambda i: (jnp.minimum(i, nct - 1), 0)),
                pl.BlockSpec((tile, width), lambda i: (jnp.maximum(i - nct, 0), 0))]
    return [pl.BlockSpec((tile, width), lambda i, j: (jnp.minimum(i, nct - 1), 0)),
            pl.BlockSpec((tile, width), lambda i, j: (jnp.maximum(i - nct, 0), 0))]


def _pick_part(n_ctx_tiles, c_ref, l_ref):
    return jnp.where(pl.program_id(0) < n_ctx_tiles, c_ref[...], l_ref[...])


INPROJ_TM = 512
INPROJ_TN = 1024


def _inproj_kernel(n_ctx_tiles, xc_ref, xl_ref, sc_ref, sh_ref, nw_ref, w_ref, wkt_ref, o_ref, kt_ref):
    x = _pick_part(n_ctx_tiles, xc_ref, xl_ref)
    ms = jnp.mean(x * x, axis=-1, keepdims=True)
    y = x * lax.rsqrt(ms + EPS) * nw_ref[...]
    hb = (y * (1.0 + sc_ref[...]) + sh_ref[...]).astype(BF16)
    kt_ref[...] = lax.dot_general(wkt_ref[...], hb, (((1,), (1,)), ((), ())),
                                  preferred_element_type=F32).astype(BF16)
    for j in range(D_IN // INPROJ_TN):
        cols = slice(j * INPROJ_TN, (j + 1) * INPROJ_TN)
        o_ref[:, cols] = jnp.dot(hb, w_ref[:, cols], preferred_element_type=F32).astype(BF16)


def _inproj_call(geom, l, x_ctx, x_lat, mod6, norm_w, w_in_bf, w_rkt_bf):
    tm = INPROJ_TM
    return pl.pallas_call(
        functools.partial(_inproj_kernel, geom.n_ctx // tm),
        grid=(geom.n_tok // tm,),
        in_specs=_split_in_specs(geom, tm, D_MODEL, 1) + [
                  _mod_spec(geom, l, 1, tm, 1),
                  _mod_spec(geom, l, 0, tm, 1),
                  pl.BlockSpec((1, D_MODEL), lambda i: (0, 0)),
                  pl.BlockSpec((None, D_MODEL, D_IN), lambda i: (l, 0, 0), pipeline_mode=pl.Buffered(1)),
                  pl.BlockSpec((RET_HEADS * RET_QK, D_MODEL), lambda i: (0, 0))],
        out_specs=[pl.BlockSpec((tm, D_IN), lambda i: (i, 0)),
                   pl.BlockSpec((RET_HEADS * RET_QK, tm), lambda i: (0, i))],
        out_shape=[jax.ShapeDtypeStruct((geom.n_tok, D_IN), BF16),
                   jax.ShapeDtypeStruct((RET_HEADS * RET_QK, geom.n_tok), BF16)],
        compiler_params=_cparams(("arbitrary",)),
        name="inproj",
    )(x_ctx, x_lat, mod6, mod6, norm_w.reshape(1, D_MODEL), w_in_bf, w_rkt_bf)


def _gelu_tanh(x):
    return 0.5 * x * (1.0 + jnp.tanh(math.sqrt(2.0 / math.pi) * (x + 0.044715 * (x * x * x))))


def _lru_kernel(geom, reverse, *refs):
    if reverse:
        (xa_ref, xp_ref, xn_ref, cw_ref, cb_ref, wg_ref, bg_ref, sp_ref, h0_ref, perm_ref, permt_ref,
         ga_ref, hf_ref, out_ref, hl_ref, c_scr) = refs
    else:
        (xa_ref, xp_ref, xn_ref, cw_ref, cb_ref, wg_ref, bg_ref, sp_ref, h0_ref, perm_ref,
         out_ref, hl_ref, c_scr) = refs
    g = pl.program_id(0)
    i = geom.n_blocks - 1 - g if reverse else g
    start = geom.seq_start(i)
    end = geom.seq_end(i)

    @pl.when(end if reverse else start)
    def _():
        c_scr[...] = h0_ref[...]

    sub_len = BLK // LRU_SUB
    perm = perm_ref[...]
    x = jnp.dot(perm, xa_ref[...], preferred_element_type=F32)
    pm = jnp.where(start, 0.0, 1.0)
    nm = jnp.where(end, 0.0, 1.0)
    hp = xp_ref.shape[0]
    p1 = xp_ref[hp - 1:hp, :].astype(F32) * pm
    p2 = xp_ref[hp - 2:hp - 1, :].astype(F32) * pm
    n0 = xn_ref[0:1, :].astype(F32) * nm
    row = lax.broadcasted_iota(jnp.int32, x.shape, 0)
    xm1 = jnp.where(row < LRU_SUB, pltpu.roll(x, LRU_SUB + 1, 0), pltpu.roll(x, LRU_SUB, 0))
    xm1 = jnp.where(row == 0, p1, xm1)
    xm2 = jnp.where(row < 2 * LRU_SUB, pltpu.roll(x, 2 * LRU_SUB + 1, 0), pltpu.roll(x, 2 * LRU_SUB, 0))
    xm2 = jnp.where(row == 0, p2, jnp.where(row == LRU_SUB, p1, xm2))
    xp1 = jnp.where(row >= BLK - LRU_SUB, pltpu.roll(x, BLK - LRU_SUB - 1, 0),
                    pltpu.roll(x, BLK - LRU_SUB, 0))
    xp1 = jnp.where(row == BLK - 1, n0, xp1)
    xc = (cw_ref[0:1, :] * xm2 + cw_ref[1:2, :] * xm1 + cw_ref[2:3, :] * x
          + cw_ref[3:4, :] * xp1 + cb_ref[...])

    gt = jnp.dot(xc.astype(BF16), wg_ref[...], preferred_element_type=F32) + bg_ref[...]
    r = jax.nn.sigmoid(gt[:, :D_RNN])
    ig = jax.nn.sigmoid(gt[:, D_RNN:])
    a = jnp.exp(-LRU_C * r * sp_ref[...])
    u = jnp.sqrt(1.0 - a * a) * ig * xc

    h = jnp.zeros((LRU_SUB, D_RNN), F32)
    p = jnp.ones((LRU_SUB, D_RNN), F32)
    h_loc = [None] * sub_len
    p_loc = [None] * sub_len
    for t in (range(sub_len - 1, -1, -1) if reverse else range(sub_len)):
        a_t = a[t * LRU_SUB:(t + 1) * LRU_SUB, :]
        h = a_t * h + u[t * LRU_SUB:(t + 1) * LRU_SUB, :]
        p = a_t * p
        h_loc[t] = h
        p_loc[t] = p
    h_in = [None] * LRU_SUB
    state = c_scr[...]
    for k in (range(LRU_SUB - 1, -1, -1) if reverse else range(LRU_SUB)):
        h_in[k] = state
        state = h[k:k + 1, :] + p[k:k + 1, :] * state
    c_scr[...] = state
    hl_ref[...] = state
    h_in = jnp.concatenate(h_in, axis=0)
    h_full = jnp.concatenate([h_loc[t] + p_loc[t] * h_in for t in range(sub_len)], axis=0)
    if reverse:
        gv = jnp.dot(perm, ga_ref[...], preferred_element_type=F32)
        y = (_gelu_tanh(gv) * (hf_ref[...] + h_full)).astype(BF16)
        out_ref[...] = jnp.dot(permt_ref[...], y, preferred_element_type=F32).astype(BF16)
    else:
        out_ref[...] = h_full


def _lru_call(geom, reverse, proj, conv_w, conv_b, wg, bg, sp, h0, hf=None):
    nb = geom.n_blocks
    halo = 16
    hpb = BLK // halo

    def blk(g):
        return nb - 1 - g if reverse else g

    d = 1 if reverse else 0
    in_specs = [
        pl.BlockSpec((BLK, D_RNN), lambda g: (blk(g), C_XA // D_RNN)),
        pl.BlockSpec((halo, D_RNN), lambda g: (jnp.maximum(blk(g) * hpb - 1, 0), C_XA // D_RNN)),
        pl.BlockSpec((halo, D_RNN), lambda g: (jnp.minimum((blk(g) + 1) * hpb, nb * hpb - 1), C_XA // D_RNN)),
        pl.BlockSpec((CONV_W, D_RNN), lambda g: (0, 0)),
        pl.BlockSpec((1, D_RNN), lambda g: (0, 0)),
        pl.BlockSpec((D_RNN, 2 * D_RNN), lambda g: (0, 0)),
        pl.BlockSpec((1, 2 * D_RNN), lambda g: (0, 0)),
        pl.BlockSpec((1, D_RNN), lambda g: (0, 0)),
        pl.BlockSpec((None, None, 1, D_RNN), lambda g: (geom.seq_id(blk(g)), d, 0, 0)),
    ]
    pos = np.arange(BLK)
    perm_np = np.zeros((BLK, BLK), np.float32)
    perm_np[pos, (pos % LRU_SUB) * (BLK // LRU_SUB) + pos // LRU_SUB] = 1.0
    in_specs.append(pl.BlockSpec((BLK, BLK), lambda g: (0, 0)))
    args = [proj, proj, proj, conv_w, conv_b, wg, bg, sp, h0, jnp.asarray(perm_np, BF16)]
    if reverse:
        in_specs += [pl.BlockSpec((BLK, BLK), lambda g: (0, 0)),
                     pl.BlockSpec((BLK, D_RNN), lambda g: (blk(g), C_GA // D_RNN)),
                     pl.BlockSpec((BLK, D_RNN), lambda g: (blk(g), 0))]
        args += [jnp.asarray(perm_np.T, BF16), proj, hf]
        out_dtype = BF16
    else:
        out_dtype = F32
    scratch = [pltpu.VMEM((1, D_RNN), F32)]
    return pl.pallas_call(
        functools.partial(_lru_kernel, geom, reverse),
        grid=(nb,),
        in_specs=in_specs,
        out_specs=[pl.BlockSpec((BLK, D_RNN), lambda g: (blk(g), 0)),
                   pl.BlockSpec((None, 1, D_RNN), lambda g: (blk(g), 0, 0))],
        out_shape=[jax.ShapeDtypeStruct((geom.n_tok, D_RNN), out_dtype),
                   jax.ShapeDtypeStruct((nb, 1, D_RNN), F32)],
        scratch_shapes=scratch,
        compiler_params=_cparams(("arbitrary",)),
        name="lru_bwd" if reverse else "lru_fwd",
    )(*args)


def _group_rms(x, w, ones):
    xx = x * x
    hi = xx.astype(BF16)
    lo = (xx - hi.astype(F32)).astype(BF16)
    ss = (jnp.dot(hi, ones, preferred_element_type=F32)
          + jnp.dot(lo, ones, preferred_element_type=F32))
    return x * lax.rsqrt(ss * (1.0 / DA_QK) + EPS) * w


def _rope(x, cos, sin):
    lane = lax.broadcasted_iota(jnp.int32, x.shape, 1)
    first = (lane % (2 * ROPE_PAIRS)) < ROPE_PAIRS
    w = x.shape[1]
    partner = jnp.where(first, pltpu.roll(x, w - ROPE_PAIRS, 1), pltpu.roll(x, ROPE_PAIRS, 1))
    return x * cos + partner * sin


def _prep_kernel(rope, *refs):
    if rope:
        dq_ref, dk_ref, qw_ref, kw_ref, ones_ref, cos_ref, sin_ref, q_out, k_out = refs
    else:
        dq_ref, dk_ref, qw_ref, kw_ref, ones_ref, dv_ref, q_out, k_out, kf_out, vf_out = refs
        vf_out[...] = dv_ref[...].astype(F32)
    ones = ones_ref[...]
    q = _group_rms(dq_ref[...].astype(F32), qw_ref[...], ones)
    k = _group_rms(dk_ref[...].astype(F32), kw_ref[...], ones)
    if rope:
        cos = jnp.concatenate([cos_ref[...]] * 4, axis=1)
        sin = jnp.concatenate([sin_ref[...]] * 4, axis=1)
        q = _rope(q, cos, sin)
        k = _rope(k, cos, sin)
    else:
        kf_out[...] = k
    q_out[...] = (q * (DA_QK ** -0.5 * math.log2(math.e))).astype(BF16)
    k_out[...] = k.astype(BF16)


def _prep_call(geom, latent, proj, qw, kw, ones, cos=None, sin=None):
    tm = 512
    w = DA_HEADS * 2 * DA_QK
    if latent:
        n, off = geom.n_lat, geom.n_ctx // tm
        per = geom.dec_seq // tm
    else:
        n, off = geom.n_ctx, 0
    in_specs = [pl.BlockSpec((tm, w), lambda i: (i + off, C_DQ // w)),
                pl.BlockSpec((tm, w), lambda i: (i + off, C_DK // w)),
                pl.BlockSpec((1, w), lambda i: (0, 0)),
                pl.BlockSpec((1, w), lambda i: (0, 0)),
                pl.BlockSpec((w, w), lambda i: (0, 0))]
    args = [proj, proj, qw, kw, ones]
    out_specs = [pl.BlockSpec((tm, w), lambda i: (i, 0)), pl.BlockSpec((tm, w), lambda i: (i, 0))]
    out_shape = [jax.ShapeDtypeStruct((n, w), BF16), jax.ShapeDtypeStruct((n, w), BF16)]
    if latent:
        in_specs += [pl.BlockSpec((tm, 2 * DA_QK), lambda i: (i % per, 0)),
                     pl.BlockSpec((tm, 2 * DA_QK), lambda i: (i % per, 0))]
        args += [cos, sin]
    else:
        in_specs.append(pl.BlockSpec((tm, w), lambda i: (i, C_DV // w)))
        args.append(proj)
        out_specs += [pl.BlockSpec((tm, w), lambda i: (i, 0)), pl.BlockSpec((tm, w), lambda i: (i, 0))]
        out_shape += [jax.ShapeDtypeStruct((n, w), F32), jax.ShapeDtypeStruct((n, w), F32)]
    return pl.pallas_call(
        functools.partial(_prep_kernel, latent),
        grid=(n // tm,),
        in_specs=in_specs, out_specs=out_specs, out_shape=out_shape,
        compiler_params=_cparams(("arbitrary",)),
        name="qk_prep_lat" if latent else "qk_prep_ctx",
    )(*args)


ATT_KC = 256
ATT_TQ = 256
LOG2E = math.log2(math.e)
ATT_SAFE_LOGIT = 60.0


def _attn_kernel(out_scale, has_cache, *refs):
    if has_cache:
        par_ref, q_ref, kc_ref, vc_ref, kl_ref, vl_ref, sw_ref, o_ref, e_scr, o_scr = refs
        srcs = [(kc_ref, vc_ref), (kl_ref, vl_ref)]
    else:
        par_ref, q_ref, kl_ref, vl_ref, sw_ref, o_ref, e_scr, o_scr = refs
        srcs = [(kl_ref, vl_ref)]
    chunks = [(kr, vr, st) for kr, vr in srcs for st in range(0, kr.shape[0], ATT_KC)]
    lam = par_ref[0]
    no_shift = par_ref[1] > 0.5
    tqs = ATT_TQ
    nsub = q_ref.shape[0] // tqs
    nt = (((1,), (1,)), ((), ()))
    half = ATT_KC // 2

    def stacked_q(sb):
        q = q_ref[sb * tqs:(sb + 1) * tqs, :]
        lane = lax.broadcasted_iota(jnp.int32, q.shape, 1)
        zero = jnp.zeros_like(q)
        return jnp.concatenate([jnp.where(lane < DA_QK, q, zero), jnp.where(lane >= DA_QK, q, zero)], axis=0)

    def logits(qq, c):
        kr, vr, st = chunks[c]
        return lax.dot_general(qq, kr[st:st + ATT_KC, :], nt, preferred_element_type=F32)

    def fold(total, e):
        part = e[:, :half] + e[:, half:]
        return part if total is None else total + part

    def row_stats(lsum):
        l = jnp.sum(lsum, axis=-1, keepdims=True)
        l1 = l[0:tqs]
        return l1, lam * l1 / l[tqs:2 * tqs]

    def pv(acc, buf, c, rho):
        kr, vr, st = chunks[c]
        w = (e_scr[buf, c, 0:tqs, :] - rho * e_scr[buf, c, tqs:2 * tqs, :]).astype(BF16)
        t = jnp.dot(w, vr[st:st + ATT_KC, :], preferred_element_type=F32)
        return t if acc is None else acc + t

    nck = len(chunks)

    @pl.when(no_shift)
    def _():
        stats = None
        for sb in range(nsub + 1):
            qq = stacked_q(sb) if sb < nsub else None
            lsum, acc = None, None
            for c in range(nck):
                if sb < nsub:
                    e = jnp.exp2(logits(qq, c))
                    e_scr[sb % 2, c] = e
                    lsum = fold(lsum, e)
                if sb > 0:
                    acc = pv(acc, (sb - 1) % 2, c, stats[1])
            if sb > 0:
                o_scr[(sb - 1) * tqs:sb * tqs, :] = acc / stats[0]
            if sb < nsub:
                stats = row_stats(lsum)

    @pl.when(jnp.logical_not(no_shift))
    def _():
        for sb in range(nsub):
            qq = stacked_q(sb)
            m = None
            for c in range(nck):
                s = logits(qq, c)
                e_scr[0, c] = s
                mc = jnp.max(s, axis=-1, keepdims=True)
                m = mc if m is None else jnp.maximum(m, mc)
            lsum = None
            for c in range(nck):
                e = jnp.exp2(e_scr[0, c] - m)
                e_scr[0, c] = e
                lsum = fold(lsum, e)
            l1, rho = row_stats(lsum)
            acc = None
            for c in range(nck):
                acc = pv(acc, 0, c, rho)
            o_scr[sb * tqs:(sb + 1) * tqs, :] = acc / l1

    o = o_scr[...]
    y = o * lax.rsqrt(jnp.mean(o * o, axis=-1, keepdims=True) + EPS) * sw_ref[...]
    o_ref[...] = (y * out_scale).astype(BF16)


def _attn_call(par, lam_init, q2d, k2d, proj, v_row_off, n_b, t_q, t_kl, tq, subln_w, cache=None):
    hw = 2 * DA_QK
    nq = t_q // tq
    vcol = C_DV // DA_V
    in_specs = [pl.BlockSpec(memory_space=pltpu.SMEM),
                pl.BlockSpec((tq, hw), lambda b, h, qi: (b * nq + qi, h))]
    args = [par, q2d]
    n_chunks = t_kl // ATT_KC
    if cache is not None:
        kc, vc = cache
        p = kc.shape[1]
        n_chunks += p // ATT_KC
        in_specs += [pl.BlockSpec((None, p, hw), lambda b, h, qi: (b, 0, h)),
                     pl.BlockSpec((None, p, DA_V), lambda b, h, qi: (b, 0, h))]
        args += [kc, vc]
    in_specs += [pl.BlockSpec((t_kl, hw), lambda b, h, qi: (b, h)),
                 pl.BlockSpec((t_kl, DA_V), lambda b, h, qi: (v_row_off + b, vcol + h)),
                 pl.BlockSpec((1, DA_V), lambda b, h, qi: (0, 0))]
    args += [k2d, proj, subln_w.reshape(1, DA_V)]
    return pl.pallas_call(
        functools.partial(_attn_kernel, 1.0 - lam_init, cache is not None),
        grid=(n_b, DA_HEADS, nq),
        in_specs=in_specs,
        out_specs=pl.BlockSpec((tq, DA_V), lambda b, h, qi: (b * nq + qi, h)),
        out_shape=jax.ShapeDtypeStruct((n_b * t_q, DA_HEADS * DA_V), BF16),
        scratch_shapes=[pltpu.VMEM((2 if tq > ATT_TQ else 1, n_chunks, 2 * ATT_TQ, ATT_KC), F32),
                        pltpu.VMEM((tq, DA_V), F32)],
        compiler_params=_cparams(("arbitrary", "arbitrary", "arbitrary")),
        name="diff_attn_lat" if cache is not None else "diff_attn_ctx",
    )(*args)


def _ret_state_update(kt, v, kd, cd, s_old):
    parts = []
    for h in range(RET_HEADS):
        rows = slice(h * RET_QK, (h + 1) * RET_QK)
        kh = (kt[rows, :].astype(F32) * kd[rows, :]).astype(BF16)
        parts.append(jnp.dot(kh, v[:, h * RET_V:(h + 1) * RET_V], preferred_element_type=F32))
    return cd * s_old + jnp.concatenate(parts, axis=0)


def _ret_bwd_kernel(geom, kt_ref, v_ref, kd_ref, cd_ref, s0_ref, sstart_ref, send_ref, s_scr):
    i = geom.n_blocks - 1 - pl.program_id(0)

    @pl.when(geom.seq_end(i))
    def _():
        s_scr[...] = s0_ref[...]

    s_old = s_scr[...]
    sstart_ref[...] = s_old
    kt = kt_ref[...] * jnp.asarray(RET_QK ** -0.5, BF16)
    s_new = _ret_state_update(kt, v_ref[...], kd_ref[...], cd_ref[...], s_old)
    s_scr[...] = s_new
    send_ref[...] = s_new


def _ret_bwd_call(geom, proj, rkt, kd_b, cd_b, s0):
    nb = geom.n_blocks
    hs = RET_HEADS * RET_QK

    def blk(g):
        return nb - 1 - g

    return pl.pallas_call(
        functools.partial(_ret_bwd_kernel, geom),
        grid=(nb,),
        in_specs=[pl.BlockSpec((hs, BLK), lambda g: (0, blk(g))),
                  pl.BlockSpec((BLK, RET_HEADS * RET_V), lambda g: (blk(g), C_RV // (RET_HEADS * RET_V))),
                  pl.BlockSpec((hs, BLK), lambda g: (0, 0)),
                  pl.BlockSpec((hs, RET_V), lambda g: (0, 0)),
                  pl.BlockSpec((None, None, hs, RET_V), lambda g: (geom.seq_id(blk(g)), 1, 0, 0))],
        out_specs=[pl.BlockSpec((None, hs, RET_V), lambda g: (blk(g), 0, 0)),
                   pl.BlockSpec((None, hs, RET_V), lambda g: (blk(g), 0, 0))],
        out_shape=[jax.ShapeDtypeStruct((nb, hs, RET_V), F32),
                   jax.ShapeDtypeStruct((nb, hs, RET_V), F32)],
        scratch_shapes=[pltpu.VMEM((hs, RET_V), F32)],
        compiler_params=_cparams(("arbitrary",)),
        name="ret_bwd_state",
    )(rkt, proj, kd_b, cd_b, s0)


def _ret_main_kernel(geom, q_ref, kt_ref, v_ref, g_ref, dsum_ref, qdf_ref, qdb_ref, kd_ref, cd_ref,
                     s0_ref, sb_ref, o_ref, send_ref, s_scr):
    i = pl.program_id(0)

    @pl.when(geom.seq_start(i))
    def _():
        s_scr[...] = s0_ref[...]

    s_f = s_scr[...]
    s_fb = s_f.astype(BF16)
    s_bb = sb_ref[...].astype(BF16)
    q = q_ref[...].astype(F32)
    kt = kt_ref[...] * jnp.asarray(RET_QK ** -0.5, BF16)
    v = v_ref[...]
    lane = lax.broadcasted_iota(jnp.int32, q.shape, 1)
    for h in range(RET_HEADS):
        in_head = (lane >= h * RET_QK) & (lane < (h + 1) * RET_QK)
        qh = jnp.where(in_head, q, 0.0)
        vh = v[:, h * RET_V:(h + 1) * RET_V]
        sc = jnp.dot(qh.astype(BF16), kt, preferred_element_type=F32) * dsum_ref[h]
        o = jnp.dot(sc.astype(BF16), vh, preferred_element_type=F32)
        o += jnp.dot((qh * qdf_ref[...]).astype(BF16), s_fb, preferred_element_type=F32)
        o += jnp.dot((qh * qdb_ref[...]).astype(BF16), s_bb, preferred_element_type=F32)
        y = o * lax.rsqrt(jnp.mean(o * o, axis=-1, keepdims=True) + EPS)
        gv = g_ref[:, h * RET_V:(h + 1) * RET_V].astype(F32)
        o_ref[:, h * RET_V:(h + 1) * RET_V] = (y * (gv * jax.nn.sigmoid(gv))).astype(BF16)
    s_new = _ret_state_update(kt, v, kd_ref[...], cd_ref[...], s_f)
    s_scr[...] = s_new
    send_ref[...] = s_new


def _ret_main_call(geom, proj, rkt, dsum, qdf, qdb, kd_f, cd_f, s0, sb_start):
    nb = geom.n_blocks
    hs = RET_HEADS * RET_QK
    hv = RET_HEADS * RET_V
    return pl.pallas_call(
        functools.partial(_ret_main_kernel, geom),
        grid=(nb,),
        in_specs=[pl.BlockSpec((BLK, hs), lambda g: (g, C_RQ // hs)),
                  pl.BlockSpec((hs, BLK), lambda g: (0, g)),
                  pl.BlockSpec((BLK, hv), lambda g: (g, C_RV // hv)),
                  pl.BlockSpec((BLK, hv), lambda g: (g, C_RG // hv)),
                  pl.BlockSpec((RET_HEADS, BLK, BLK), lambda g: (0, 0, 0)),
                  pl.BlockSpec((BLK, hs), lambda g: (0, 0)),
                  pl.BlockSpec((BLK, hs), lambda g: (0, 0)),
                  pl.BlockSpec((hs, BLK), lambda g: (0, 0)),
                  pl.BlockSpec((hs, RET_V), lambda g: (0, 0)),
                  pl.BlockSpec((None, None, hs, RET_V), lambda g: (geom.seq_id(g), 0, 0, 0)),
                  pl.BlockSpec((None, hs, RET_V), lambda g: (g, 0, 0))],
        out_specs=[pl.BlockSpec((BLK, hv), lambda g: (g, 0)),
                   pl.BlockSpec((None, hs, RET_V), lambda g: (g, 0, 0))],
        out_shape=[jax.ShapeDtypeStruct((geom.n_tok, hv), BF16),
                   jax.ShapeDtypeStruct((nb, hs, RET_V), F32)],
        scratch_shapes=[pltpu.VMEM((hs, RET_V), F32)],
        compiler_params=_cparams(("arbitrary",)),
        name="ret_main",
    )(proj, rkt, proj, proj, dsum, qdf, qdb, kd_f, cd_f, s0, sb_start)


def _ret_tables(ret_decay_l):
    log_g = jax.nn.log_sigmoid(ret_decay_l.astype(F32))
    pos = jnp.arange(BLK, dtype=F32)
    diff = pos[:, None] - pos[None, :]
    lf = log_g[0][:, None, None]
    lb = log_g[1][:, None, None]
    dsum = (jnp.where(diff >= 0, jnp.exp(jnp.maximum(diff, 0.0)[None] * lf), 0.0)
            + jnp.where(diff <= 0, jnp.exp(jnp.maximum(-diff, 0.0)[None] * lb), 0.0))

    def per_lane(e, lg):
        return jnp.repeat(jnp.exp(e[:, None] * lg[None, :]), RET_QK, axis=1)

    qdf = per_lane(pos + 1.0, log_g[0])
    qdb = per_lane(BLK - pos, log_g[1])
    kd_f = per_lane(BLK - 1.0 - pos, log_g[0]).T
    kd_b = per_lane(pos, log_g[1]).T
    cd_f = jnp.broadcast_to(jnp.repeat(jnp.exp(BLK * log_g[0]), RET_QK)[:, None], (RET_HEADS * RET_QK, RET_V))
    cd_b = jnp.broadcast_to(jnp.repeat(jnp.exp(BLK * log_g[1]), RET_QK)[:, None], (RET_HEADS * RET_QK, RET_V))
    return dsum, qdf, qdb, kd_f, kd_b, cd_f, cd_b


def _merge_kernel(n_ctx_tiles, ba_ref, bbc_ref, bbl_ref, bc_ref, g0_ref, g1_ref, g2_ref, xc_ref, xl_ref,
                  gate_ref, sc_ref, sh_ref, nw_ref, wb_ref, wo_ref, rhi_ref, rlo_ref, x1_ref, h2_ref, lt_ref):
    branches = (ba_ref[...], _pick_part(n_ctx_tiles, bbc_ref, bbl_ref), bc_ref[...])
    acc = None
    for br, (b, g_ref) in enumerate(zip(branches, (g0_ref, g1_ref, g2_ref))):
        p = jnp.dot(b, wb_ref[br], preferred_element_type=F32)
        t = (0.5 * jnp.tanh(0.5 * g_ref[...].astype(F32)) + 0.5) * p
        acc = t if acc is None else acc + t
    m = jnp.dot(acc.astype(BF16), wo_ref[...], preferred_element_type=F32)
    x1 = _pick_part(n_ctx_tiles, xc_ref, xl_ref) + gate_ref[...] * m
    x1_ref[...] = x1
    ms = jnp.mean(x1 * x1, axis=-1, keepdims=True)
    h2 = x1 * lax.rsqrt(ms + EPS) * nw_ref[...] * (1.0 + sc_ref[...]) + sh_ref[...]
    h2b = h2.astype(BF16)
    h2_ref[...] = h2b
    h2lo = (h2 - h2b.astype(F32)).astype(BF16)
    nt = (((1,), (1,)), ((), ()))
    lt_ref[...] = (lax.dot_general(rhi_ref[...], h2b, nt, preferred_element_type=F32)
                   + lax.dot_general(rhi_ref[...], h2lo, nt, preferred_element_type=F32)
                   + lax.dot_general(rlo_ref[...], h2b, nt, preferred_element_type=F32))


def _merge_call(geom, l, ba, bb_ctx, bb_lat, bc, proj, x_ctx, x_lat, mod6, norm2_w, wb_bf, wo_bf, r_hi, r_lo):
    tm = 512
    gcol = C_GL // D_MODEL
    full = lambda shape: pl.BlockSpec(shape, lambda i: tuple(0 for _ in shape))
    tok = lambda w: pl.BlockSpec((tm, w), lambda i: (i, 0))
    return pl.pallas_call(
        functools.partial(_merge_kernel, geom.n_ctx // tm),
        grid=(geom.n_tok // tm,),
        in_specs=[tok(BRANCH_W)] + _split_in_specs(geom, tm, BRANCH_W, 1) + [tok(BRANCH_W),
                  pl.BlockSpec((tm, D_MODEL), lambda i: (i, gcol)),
                  pl.BlockSpec((tm, D_MODEL), lambda i: (i, gcol + 1)),
                  pl.BlockSpec((tm, D_MODEL), lambda i: (i, gcol + 2))]
                 + _split_in_specs(geom, tm, D_MODEL, 1) + [
                  _mod_spec(geom, l, 2, tm, 1), _mod_spec(geom, l, 4, tm, 1), _mod_spec(geom, l, 3, tm, 1),
                  full((1, D_MODEL)),
                  full((N_BRANCH, BRANCH_W, D_MODEL)), full((D_MODEL, D_MODEL)),
                  full((N_EXPERTS, D_MODEL)), full((N_EXPERTS, D_MODEL))],
        out_specs=[tok(D_MODEL), tok(D_MODEL), pl.BlockSpec((N_EXPERTS, tm), lambda i: (0, i))],
        out_shape=[jax.ShapeDtypeStruct((geom.n_tok, D_MODEL), F32),
                   jax.ShapeDtypeStruct((geom.n_tok, D_MODEL), BF16),
                   jax.ShapeDtypeStruct((N_EXPERTS, geom.n_tok), F32)],
        compiler_params=_cparams(("arbitrary",)),
        name="merge_out",
    )(ba, bb_ctx, bb_lat, bc, proj, proj, proj, x_ctx, x_lat, mod6, mod6, mod6,
      norm2_w.reshape(1, D_MODEL), wb_bf, wo_bf, r_hi, r_lo)


def _router_kernel(lt_ref, bias_ref, g_ref):
    per = N_EXPERTS // N_GROUPS
    tm = lt_ref.shape[1]
    scores = jax.nn.sigmoid(lt_ref[...])
    biased = scores + bias_ref[...]
    b3 = biased.reshape(N_GROUPS, per, tm)
    neg = jnp.float32(-jnp.inf)
    m1 = jnp.max(b3, axis=1, keepdims=True)
    is_m1 = b3 == m1
    cnt = jnp.sum(is_m1.astype(F32), axis=1, keepdims=True)
    m2 = jnp.max(jnp.where(is_m1, neg, b3), axis=1, keepdims=True)
    grp = (m1 + jnp.where(cnt >= 2.0, m1, m2)).reshape(N_GROUPS, tm)
    gidx = lax.broadcasted_iota(jnp.int32, (N_GROUPS, tm), 0)
    grank = jnp.zeros((N_GROUPS, tm), F32)
    for g2 in range(N_GROUPS):
        other = grp[g2:g2 + 1, :]
        ahead = (other > grp) | ((other == grp) & (gidx > g2))
        grank += ahead.astype(F32)
    gsel = (grank < float(TOPK_GROUPS)).astype(F32)
    emask = jnp.broadcast_to(gsel.reshape(N_GROUPS, 1, tm), (N_GROUPS, per, tm)).reshape(N_EXPERTS, tm)
    masked = jnp.where(emask > 0.0, biased, neg)
    eidx = lax.broadcasted_iota(jnp.int32, (N_EXPERTS, tm), 0)
    erank = jnp.zeros((N_EXPERTS, tm), F32)
    for e2 in range(N_EXPERTS):
        other = masked[e2:e2 + 1, :]
        ahead = (other > masked) | ((other == masked) & (eidx > e2))
        erank += ahead.astype(F32)
    w = jnp.where(erank < float(TOP_K), scores, 0.0)
    gates_t = w / jnp.sum(w, axis=0, keepdims=True) * ROUTED_SCALE
    pad = jnp.zeros((GATE_W - N_EXPERTS, tm), F32)
    g_ref[...] = jnp.concatenate([gates_t, pad], axis=0).T


def _router_call(geom, logits_t, bias):
    tm = 512
    return pl.pallas_call(
        _router_kernel,
        grid=(geom.n_tok // tm,),
        in_specs=[pl.BlockSpec((N_EXPERTS, tm), lambda i: (0, i)),
                  pl.BlockSpec((N_EXPERTS, 1), lambda i: (0, 0))],
        out_specs=pl.BlockSpec((tm, GATE_W), lambda i: (i, 0)),
        out_shape=jax.ShapeDtypeStruct((geom.n_tok, GATE_W), F32),
        compiler_params=_cparams(("arbitrary",)),
        name="router",
    )(logits_t, bias.reshape(N_EXPERTS, 1))


MOE_EB = 8
MOE_TM = 1024
MOE_VMEM_LIMIT = 60 * 1024 * 1024


def _moe_kernel(n_ctx_tiles, h_ref, g_ref, gu_ref, dn_ref, sgu_ref, sdn_ref, x1_ref, gate_ref,
                oc_ref, ol_ref, acc_scr):
    i = pl.program_id(0)
    j = pl.program_id(1)
    h = h_ref[...]

    def expert(gu, dn, ge):
        a = jnp.dot(h, gu, preferred_element_type=F32)
        hg = a[:, :D_EXPERT]
        act = (hg * jax.nn.sigmoid(hg)) * a[:, D_EXPERT:]
        if ge is not None:
            act = act * ge
        return jnp.dot(act.astype(BF16), dn, preferred_element_type=F32)

    @pl.when(j == 0)
    def _():
        acc_scr[...] = expert(sgu_ref[...], sdn_ref[...], None)

    gts = g_ref[...]
    lane = lax.broadcasted_iota(jnp.int32, gts.shape, 1)
    acc = acc_scr[...]
    for e in range(MOE_EB):
        ge = jnp.sum(jnp.where(lane == j * MOE_EB + e, gts, 0.0), axis=1, keepdims=True)
        acc += expert(gu_ref[e], dn_ref[e], ge)
    acc_scr[...] = acc

    @pl.when(j == pl.num_programs(1) - 1)
    def _():
        y = x1_ref[...] + gate_ref[...] * acc

        @pl.when(i < n_ctx_tiles)
        def _():
            oc_ref[...] = y

        @pl.when(i >= n_ctx_tiles)
        def _():
            ol_ref[...] = y


def _moe_call(geom, l, h2, gates, gu_bf, dn_bf, sgu_bf, sdn_bf, x1, mod6):
    tm = MOE_TM
    nct = geom.n_ctx // tm
    once = pl.Buffered(1)
    return pl.pallas_call(
        functools.partial(_moe_kernel, nct),
        grid=(geom.n_tok // tm, N_EXPERTS // MOE_EB),
        in_specs=[pl.BlockSpec((tm, D_MODEL), lambda i, j: (i, 0)),
                  pl.BlockSpec((tm, GATE_W), lambda i, j: (i, 0)),
                  pl.BlockSpec((None, MOE_EB, D_MODEL, 2 * D_EXPERT), lambda i, j: (l, j, 0, 0)),
                  pl.BlockSpec((None, MOE_EB, D_EXPERT, D_MODEL), lambda i, j: (l, j, 0, 0)),
                  pl.BlockSpec((None, D_MODEL, 2 * D_EXPERT), lambda i, j: (l, 0, 0), pipeline_mode=once),
                  pl.BlockSpec((None, D_EXPERT, D_MODEL), lambda i, j: (l, 0, 0), pipeline_mode=once),
                  pl.BlockSpec((tm, D_MODEL), lambda i, j: (i, 0), pipeline_mode=once),
                  _mod_spec(geom, l, 5, tm, 2)],
        out_specs=[pl.BlockSpec((tm, D_MODEL), lambda i, j: (jnp.minimum(i, nct - 1), 0), pipeline_mode=once),
                   pl.BlockSpec((tm, D_MODEL), lambda i, j: (jnp.maximum(i - nct, 0), 0), pipeline_mode=once)],
        out_shape=[jax.ShapeDtypeStruct((geom.n_ctx, D_MODEL), F32),
                   jax.ShapeDtypeStruct((geom.n_lat, D_MODEL), F32)],
        scratch_shapes=[pltpu.VMEM((tm, D_MODEL), F32)],
        compiler_params=_cparams(("arbitrary", "arbitrary"), MOE_VMEM_LIMIT),
        name="moe_experts",
    )(h2, gates, gu_bf, dn_bf, sgu_bf, sdn_bf, x1, mod6)


def _rope_tables(dec_seq):
    rows = dec_seq // GRID_W
    row = jnp.repeat(jnp.arange(rows, dtype=F32), GRID_W)
    col = jnp.tile(jnp.arange(GRID_W, dtype=F32), rows)
    inv = ROPE_BASE ** (-jnp.arange(ROPE_PAIRS, dtype=F32) / ROPE_PAIRS)
    ar = row[:, None] * inv[None, :]
    ac = col[:, None] * inv[None, :]
    cos64 = jnp.concatenate([jnp.cos(ar), jnp.cos(ar), jnp.cos(ac), jnp.cos(ac)], axis=1)
    sin64 = jnp.concatenate([-jnp.sin(ar), jnp.sin(ar), -jnp.sin(ac), jnp.sin(ac)], axis=1)
    return jnp.tile(cos64, (1, 2)), jnp.tile(sin64, (1, 2))


def _block_diag_gate(wg_dir):
    eye = jnp.eye(LRU_BLOCKS, dtype=F32)
    dense = jnp.einsum('gnij,nm->gnimj', wg_dir.astype(F32), eye).reshape(2, D_RNN, D_RNN)
    return jnp.concatenate([dense[0], dense[1]], axis=1)


def kernel(x_prompt, x_sample, cache_k, cache_v, state_lru, state_ret, c, c_ctx, ada_w, ada_b, norm1_w, norm2_w, w_in, conv_w, conv_b, lru_gate_w, lru_gate_b, lru_lambda, q_norm_w, k_norm_w, diff_lambda, subln_w, ret_decay, w_branch, w_out, router_w, router_bias, w_exp_gu, w_exp_down, w_sh_gu, w_sh_down):
    batch, seq, _ = x_prompt.shape
    dec_batch, dec_seq, _ = x_sample.shape
    assert 1 + dec_batch <= MOD_ROWS
    geom = _Geom(batch, seq, dec_batch, dec_seq)
    hs = RET_HEADS * RET_QK
    aw = DA_HEADS * 2 * DA_QK

    x_ctx = x_prompt.reshape(geom.n_ctx, D_MODEL)
    x_lat = x_sample.reshape(geom.n_lat, D_MODEL)
    cvec = jnp.zeros((MOD_ROWS, D_MODEL), F32).at[0].set(c_ctx).at[1:1 + dec_batch].set(c)
    mod6 = _ada_call(cvec, ada_w, ada_b).reshape(DEPTH, MOD_ROWS, 6, 1, D_MODEL)

    ones_bd = jnp.kron(jnp.eye(aw // DA_QK, dtype=F32), jnp.ones((DA_QK, DA_QK), F32)).astype(BF16)
    cos_t, sin_t = _rope_tables(dec_seq)

    w_in_bf = w_in.astype(BF16)
    gu_bf, dn_bf = w_exp_gu.astype(BF16), w_exp_down.astype(BF16)
    sgu_bf, sdn_bf = w_sh_gu.astype(BF16), w_sh_down.astype(BF16)

    ks, vs, lrus, rets = [], [], [], []
    for l in range(DEPTH):
        lam_init = 0.8 - 0.6 * math.exp(-0.3 * l)
        w_rkt_bf = w_in[l][:, C_RK:C_RK + hs].T.astype(BF16)
        proj, rkt = _inproj_call(geom, l, x_ctx, x_lat, mod6, norm1_w[l], w_in_bf, w_rkt_bf)

        sp = jax.nn.softplus(-lru_lambda[l].astype(F32))
        h0 = jnp.concatenate([jnp.zeros((batch, 2, D_RNN), F32), state_lru[:, l].astype(F32)], axis=0)
        h0 = h0.reshape(geom.n_seq, 2, 1, D_RNN)
        cb = conv_b[l].reshape(1, D_RNN)
        lru_args = []
        for d in range(2):
            lru_args.append((_block_diag_gate(lru_gate_w[l, d]).astype(BF16),
                             lru_gate_b[l, d].reshape(1, 2 * D_RNN), sp[d].reshape(1, D_RNN)))
        hf, hf_last = _lru_call(geom, False, proj, conv_w[l], cb, *lru_args[0], h0)
        branch_a, hb_last = _lru_call(geom, True, proj, conv_w[l], cb, *lru_args[1], h0, hf)

        qw = jnp.tile(q_norm_w[l], aw // DA_QK).reshape(1, aw)
        kw = jnp.tile(k_norm_w[l], aw // DA_QK).reshape(1, aw)
        q_c, k_c, k_c32, v_c32 = _prep_call(geom, False, proj, qw, kw, ones_bd)
        q_l, k_l = _prep_call(geom, True, proj, qw, kw, ones_bd, cos_t, sin_t)
        lam_p = diff_lambda[l].astype(F32)
        lam = jnp.exp(jnp.sum(lam_p[0] * lam_p[1])) - jnp.exp(jnp.sum(lam_p[2] * lam_p[3])) + lam_init
        q_bound = DA_QK * jnp.max(jnp.square(q_norm_w[l].astype(F32))) * (DA_QK ** -0.5 * LOG2E) ** 2
        k_bound = DA_QK * jnp.max(jnp.square(k_norm_w[l].astype(F32)))
        kc32 = cache_k[:, l].astype(F32)
        kc_bound = jnp.maximum(k_bound, jnp.max(jnp.sum(jnp.square(kc32), axis=-1)))

        def attn_par(kb):
            ok = (q_bound * kb * 1.05 < ATT_SAFE_LOGIT ** 2).astype(F32)
            return jnp.stack([lam, ok])

        assert geom.n_ctx % dec_seq == 0
        cache = (kc32.reshape(dec_batch, -1, aw).astype(BF16),
                 cache_v[:, l].reshape(dec_batch, -1, DA_HEADS * DA_V).astype(BF16))
        att_c = _attn_call(attn_par(k_bound), lam_init, q_c, k_c, proj, 0, batch, seq, seq, 256, subln_w[l])
        att_l = _attn_call(attn_par(kc_bound), lam_init, q_l, k_l, proj, geom.n_ctx // dec_seq, dec_batch,
                           dec_seq, dec_seq, 4 * ATT_TQ, subln_w[l], cache)

        dsum, qdf, qdb, kd_f, kd_b, cd_f, cd_b = _ret_tables(ret_decay[l])
        s0 = jnp.concatenate([jnp.zeros((batch, 2, hs, RET_V), F32),
                              state_ret[:, l].astype(F32).reshape(dec_batch, 2, hs, RET_V)], axis=0)
        sb_start, sb_end = _ret_bwd_call(geom, proj, rkt, kd_b, cd_b, s0)
        branch_c, sf_end = _ret_main_call(geom, proj, rkt, dsum, qdf, qdb, kd_f, cd_f, s0, sb_start)

        r_t = router_w[l].T.astype(F32)
        r_hi = r_t.astype(BF16)
        r_lo = (r_t - r_hi.astype(F32)).astype(BF16)
        x1, h2, logits_t = _merge_call(geom, l, branch_a, att_c, att_l, branch_c, proj, x_ctx, x_lat, mod6,
                                       norm2_w[l], w_branch[l].astype(BF16), w_out[l].astype(BF16), r_hi, r_lo)
        gates = _router_call(geom, logits_t, router_bias[l].astype(F32))
        x_ctx, x_lat = _moe_call(geom, l, h2, gates, gu_bf, dn_bf, sgu_bf, sdn_bf, x1, mod6)

        ks.append(k_c32.reshape(batch, seq, DA_HEADS, 2, DA_QK))
        vs.append(v_c32.reshape(batch, seq, DA_HEADS, DA_V))
        lrus.append(jnp.stack([hf_last[:batch, 0], hb_last[:batch, 0]], axis=1))
        rets.append(jnp.stack([sf_end[:batch].reshape(batch, RET_HEADS, RET_QK, RET_V),
                               sb_end[:batch].reshape(batch, RET_HEADS, RET_QK, RET_V)], axis=1))

    y_prompt = x_ctx.reshape(batch, seq, D_MODEL)
    y_sample = x_lat.reshape(dec_batch, dec_seq, D_MODEL)
    return (y_prompt, y_sample, jnp.stack(ks, axis=1), jnp.stack(vs, axis=1),
            jnp.stack(lrus, axis=1), jnp.stack(rets, axis=1))
```

```python
import functools
import math

import numpy as np
import jax
import jax.numpy as jnp
from jax import lax
from jax.experimental import pallas as pl
from jax.experimental.pallas import tpu as pltpu
from jax.experimental.pallas import tpu_sc as plsc

F32 = jnp.float32
BF16 = jnp.bfloat16

D_MODEL = 1024
DEPTH = 2
GRID_W = 64
D_RNN = 512
LRU_BLOCKS = 8
LRU_BLOCK = D_RNN // LRU_BLOCKS
CONV_W = 4
LRU_C = 8.0
DA_HEADS = 4
DA_QK = 64
DA_V = 128
ROPE_PAIRS = DA_QK // 4
ROPE_BASE = 10000.0
RET_HEADS = 4
RET_QK = 64
RET_V = 128
BRANCH_W = 512
N_BRANCH = 3
D_IN = 7168
N_EXPERTS = 64
TOP_K = 8
N_GROUPS = 8
TOPK_GROUPS = 4
D_EXPERT = 256
ROUTED_SCALE = 2.5
EPS = 1e-6

C_XA, C_GA, C_DQ, C_DK, C_DV = 0, 512, 1024, 1536, 2048
C_RQ, C_RK, C_RV, C_RG, C_GL = 2560, 2816, 3072, 3584, 4096

BLK = 256
LRU_SUB = 8
LRU_LANES = D_RNN // 128
GATE_W = 128
MOD_ROWS = 8
VMEM_LIMIT = 56 * 1024 * 1024


def _cparams(sem, vmem_limit=VMEM_LIMIT):
    return pltpu.CompilerParams(dimension_semantics=sem, vmem_limit_bytes=vmem_limit)


class _Geom:
    def __init__(self, batch, seq, dec_batch, dec_seq):
        assert seq == BLK and dec_seq % BLK == 0
        self.batch, self.seq, self.dec_batch, self.dec_seq = batch, seq, dec_batch, dec_seq
        self.n_ctx = batch * seq
        self.n_lat = dec_batch * dec_seq
        self.n_tok = self.n_ctx + self.n_lat
        self.ctx_blocks = self.n_ctx // BLK
        self.lat_blocks = dec_seq // BLK
        self.n_blocks = self.n_tok // BLK
        self.n_seq = batch + dec_batch

    def mod_row(self, i, tile):
        nct = self.n_ctx // tile
        per = self.dec_seq // tile
        return jnp.where(i < nct, 0, 1 + (i - nct) // per)

    def seq_id(self, i):
        return jnp.where(i < self.ctx_blocks, i, self.ctx_blocks + (i - self.ctx_blocks) // self.lat_blocks)

    def seq_start(self, i):
        return jnp.logical_or(i < self.ctx_blocks, (i - self.ctx_blocks) % self.lat_blocks == 0)

    def seq_end(self, i):
        return jnp.logical_or(i < self.ctx_blocks, (i - self.ctx_blocks) % self.lat_blocks == self.lat_blocks - 1)


def _ada_kernel(c_ref, w_ref, b_ref, o_ref):
    cv = c_ref[...]
    s = cv * jax.nn.sigmoid(cv)
    o_ref[...] = jnp.dot(s, w_ref[...], preferred_element_type=F32,
                         precision=lax.Precision.HIGHEST) + b_ref[...]


def _ada_call(cvec, ada_w, ada_b):
    depth = ada_w.shape[0]
    nt = 6
    return pl.pallas_call(
        _ada_kernel,
        grid=(depth, nt),
        in_specs=[pl.BlockSpec((MOD_ROWS, D_MODEL), lambda l, j: (0, 0)),
                  pl.BlockSpec((None, D_MODEL, D_MODEL), lambda l, j: (l, 0, j)),
                  pl.BlockSpec((None, 1, D_MODEL), lambda l, j: (l, 0, j))],
        out_specs=pl.BlockSpec((None, MOD_ROWS, D_MODEL), lambda l, j: (l, 0, j)),
        out_shape=jax.ShapeDtypeStruct((depth, MOD_ROWS, 6 * D_MODEL), F32),
        compiler_params=_cparams(("arbitrary", "arbitrary")),
        name="ada_mod",
    )(cvec, ada_w, ada_b.reshape(depth, 1, 6 * D_MODEL))


def _mod_spec(geom, l, which, tile, ngrid):
    if ngrid == 1:
        return pl.BlockSpec((None, None, None, 1, D_MODEL),
                            lambda i: (l, geom.mod_row(i, tile), which, 0, 0))
    return pl.BlockSpec((None, None, None, 1, D_MODEL),
                        lambda i, j: (l, geom.mod_row(i, tile), which, 0, 0))


def _split_in_specs(geom, tile, width, ngrid):
    nct = geom.n_ctx // tile
    if ngrid == 1:
        return [pl.BlockSpec((tile, width), lambda i: (jnp.minimum(i, nct - 1), 0)),
                pl.BlockSpec((tile, width), lambda i: (jnp.maximum(i - nct, 0), 0))]
    return [pl.BlockSpec((tile, width), lambda i, j: (jnp.minimum(i, nct - 1), 0)),
            pl.BlockSpec((tile, width), lambda i, j: (jnp.maximum(i - nct, 0), 0))]


def _pick_part(n_ctx_tiles, c_ref, l_ref):
    return jnp.where(pl.program_id(0) < n_ctx_tiles, c_ref[...], l_ref[...])


def _pack_halves(y):
    w = y.shape[1] // 2
    bits = pltpu.bitcast(y, jnp.uint32)
    return (bits[:, :w] >> 16) | (bits[:, w:] & jnp.uint32(0xFFFF0000))


def _unpack_halves(p):
    return pltpu.bitcast(p << 16, F32), pltpu.bitcast(p & jnp.uint32(0xFFFF0000), F32)


INPROJ_TM = 512
INPROJ_TN = 1024


def _inproj_kernel(n_ctx_tiles, xc_ref, xl_ref, sc_ref, sh_ref, nw_ref, w_ref, wkt_ref, o_ref, kt_ref):
    x = _pick_part(n_ctx_tiles, xc_ref, xl_ref)
    ms = jnp.mean(x * x, axis=-1, keepdims=True)
    y = x * lax.rsqrt(ms + EPS) * nw_ref[...]
    hb = (y * (1.0 + sc_ref[...]) + sh_ref[...]).astype(BF16)
    kt_ref[...] = lax.dot_general(wkt_ref[...], hb, (((1,), (1,)), ((), ())),
                                  preferred_element_type=F32).astype(BF16)
    for j in range(D_IN // INPROJ_TN):
        cols = slice(j * INPROJ_TN, (j + 1) * INPROJ_TN)
        o_ref[:, cols] = jnp.dot(hb, w_ref[:, cols], preferred_element_type=F32).astype(BF16)


def _inproj_call(geom, l, x_ctx, x_lat, mod6, norm_w, w_in_bf, w_rkt_bf):
    tm = INPROJ_TM
    return pl.pallas_call(
        functools.partial(_inproj_kernel, geom.n_ctx // tm),
        grid=(geom.n_tok // tm,),
        in_specs=_split_in_specs(geom, tm, D_MODEL, 1) + [
                  _mod_spec(geom, l, 1, tm, 1),
                  _mod_spec(geom, l, 0, tm, 1),
                  pl.BlockSpec((1, D_MODEL), lambda i: (0, 0)),
                  pl.BlockSpec((None, D_MODEL, D_IN), lambda i: (l, 0, 0), pipeline_mode=pl.Buffered(1)),
                  pl.BlockSpec((RET_HEADS * RET_QK, D_MODEL), lambda i: (0, 0))],
        out_specs=[pl.BlockSpec((tm, D_IN), lambda i: (i, 0)),
                   pl.BlockSpec((RET_HEADS * RET_QK, tm), lambda i: (0, i))],
        out_shape=[jax.ShapeDtypeStruct((geom.n_tok, D_IN), BF16),
                   jax.ShapeDtypeStruct((RET_HEADS * RET_QK, geom.n_tok), BF16)],
        compiler_params=_cparams(("arbitrary",)),
        name="inproj",
    )(x_ctx, x_lat, mod6, mod6, norm_w.reshape(1, D_MODEL), w_in_bf, w_rkt_bf)


def _gelu_tanh(x):
    return 0.5 * x * (1.0 + jnp.tanh(math.sqrt(2.0 / math.pi) * (x + 0.044715 * (x * x * x))))


def _lru_kernel(geom, reverse, *refs):
    if reverse:
        (xa_ref, xp_ref, xn_ref, cw_ref, cb_ref, wg_ref, bg_ref, sp_ref, h0_ref, perm_ref, permt_ref,
         ga_ref, hf_ref, out_ref, hl_ref, c_scr) = refs
    else:
        (xa_ref, xp_ref, xn_ref, cw_ref, cb_ref, wg_ref, bg_ref, sp_ref, h0_ref, perm_ref,
         out_ref, hl_ref, c_scr) = refs
    g = pl.program_id(0)
    i = geom.n_blocks - 1 - g if reverse else g
    start = geom.seq_start(i)
    end = geom.seq_end(i)

    @pl.when(end if reverse else start)
    def _():
        c_scr[...] = h0_ref[...]

    sub_len = BLK // LRU_SUB
    perm = perm_ref[...]
    x = jnp.dot(perm, xa_ref[...], preferred_element_type=F32)
    pm = jnp.where(start, 0.0, 1.0)
    nm = jnp.where(end, 0.0, 1.0)
    hp = xp_ref.shape[0]
    p1 = xp_ref[hp - 1:hp, :].astype(F32) * pm
    p2 = xp_ref[hp - 2:hp - 1, :].astype(F32) * pm
    n0 = xn_ref[0:1, :].astype(F32) * nm
    row = lax.broadcasted_iota(jnp.int32, x.shape, 0)
    xm1 = jnp.where(row < LRU_SUB, pltpu.roll(x, LRU_SUB + 1, 0), pltpu.roll(x, LRU_SUB, 0))
    xm1 = jnp.where(row == 0, p1, xm1)
    xm2 = jnp.where(row < 2 * LRU_SUB, pltpu.roll(x, 2 * LRU_SUB + 1, 0), pltpu.roll(x, 2 * LRU_SUB, 0))
    xm2 = jnp.where(row == 0, p2, jnp.where(row == LRU_SUB, p1, xm2))
    xp1 = jnp.where(row >= BLK - LRU_SUB, pltpu.roll(x, BLK - LRU_SUB - 1, 0),
                    pltpu.roll(x, BLK - LRU_SUB, 0))
    xp1 = jnp.where(row == BLK - 1, n0, xp1)
    xc = (cw_ref[0:1, :] * xm2 + cw_ref[1:2, :] * xm1 + cw_ref[2:3, :] * x
          + cw_ref[3:4, :] * xp1 + cb_ref[...])

    gt = jnp.dot(xc.astype(BF16), wg_ref[...], preferred_element_type=F32) + bg_ref[...]
    r = jax.nn.sigmoid(gt[:, :D_RNN])
    ig = jax.nn.sigmoid(gt[:, D_RNN:])
    a = jnp.exp(-LRU_C * r * sp_ref[...])
    u = jnp.sqrt(1.0 - a * a) * ig * xc

    h = jnp.zeros((LRU_SUB, D_RNN), F32)
    p = jnp.ones((LRU_SUB, D_RNN), F32)
    h_loc = [None] * sub_len
    p_loc = [None] * sub_len
    for t in (range(sub_len - 1, -1, -1) if reverse else range(sub_len)):
        a_t = a[t * LRU_SUB:(t + 1) * LRU_SUB, :]
        h = a_t * h + u[t * LRU_SUB:(t + 1) * LRU_SUB, :]
        p = a_t * p
        h_loc[t] = h
        p_loc[t] = p
    h_in = [None] * LRU_SUB
    state = c_scr[...]
    for k in (range(LRU_SUB - 1, -1, -1) if reverse else range(LRU_SUB)):
        h_in[k] = state
        state = h[k:k + 1, :] + p[k:k + 1, :] * state
    c_scr[...] = state
    hl_ref[...] = state
    h_in = jnp.concatenate(h_in, axis=0)
    h_full = jnp.concatenate([h_loc[t] + p_loc[t] * h_in for t in range(sub_len)], axis=0)
    if reverse:
        gv = jnp.dot(perm, ga_ref[...], preferred_element_type=F32)
        y = (_gelu_tanh(gv) * (hf_ref[...] + h_full)).astype(BF16)
        out_ref[...] = jnp.dot(permt_ref[...], y, preferred_element_type=F32).astype(BF16)
    else:
        out_ref[...] = h_full


def _lru_call(geom, reverse, proj, conv_w, conv_b, wg, bg, sp, h0, hf=None):
    nb = geom.n_blocks
    halo = 16
    hpb = BLK // halo

    def blk(g):
        return nb - 1 - g if reverse else g

    d = 1 if reverse else 0
    in_specs = [
        pl.BlockSpec((BLK, D_RNN), lambda g: (blk(g), C_XA // D_RNN)),
        pl.BlockSpec((halo, D_RNN), lambda g: (jnp.maximum(blk(g) * hpb - 1, 0), C_XA // D_RNN)),
        pl.BlockSpec((halo, D_RNN), lambda g: (jnp.minimum((blk(g) + 1) * hpb, nb * hpb - 1), C_XA // D_RNN)),
        pl.BlockSpec((CONV_W, D_RNN), lambda g: (0, 0)),
        pl.BlockSpec((1, D_RNN), lambda g: (0, 0)),
        pl.BlockSpec((D_RNN, 2 * D_RNN), lambda g: (0, 0)),
        pl.BlockSpec((1, 2 * D_RNN), lambda g: (0, 0)),
        pl.BlockSpec((1, D_RNN), lambda g: (0, 0)),
        pl.BlockSpec((None, None, 1, D_RNN), lambda g: (geom.seq_id(blk(g)), d, 0, 0)),
    ]
    pos = np.arange(BLK)
    perm_np = np.zeros((BLK, BLK), np.float32)
    perm_np[pos, (pos % LRU_SUB) * (BLK // LRU_SUB) + pos // LRU_SUB] = 1.0
    in_specs.append(pl.BlockSpec((BLK, BLK), lambda g: (0, 0)))
    args = [proj, proj, proj, conv_w, conv_b, wg, bg, sp, h0, jnp.asarray(perm_np, BF16)]
    if reverse:
        in_specs += [pl.BlockSpec((BLK, BLK), lambda g: (0, 0)),
                     pl.BlockSpec((BLK, D_RNN), lambda g: (blk(g), C_GA // D_RNN)),
                     pl.BlockSpec((BLK, D_RNN), lambda g: (blk(g), 0))]
        args += [jnp.asarray(perm_np.T, BF16), proj, hf]
        out_dtype = BF16
    else:
        out_dtype = F32
    scratch = [pltpu.VMEM((1, D_RNN), F32)]
    return pl.pallas_call(
        functools.partial(_lru_kernel, geom, reverse),
        grid=(nb,),
        in_specs=in_specs,
        out_specs=[pl.BlockSpec((BLK, D_RNN), lambda g: (blk(g), 0)),
                   pl.BlockSpec((None, 1, D_RNN), lambda g: (blk(g), 0, 0))],
        out_shape=[jax.ShapeDtypeStruct((geom.n_tok, D_RNN), out_dtype),
                   jax.ShapeDtypeStruct((nb, 1, D_RNN), F32)],
        scratch_shapes=scratch,
        compiler_params=_cparams(("arbitrary",)),
        name="lru_bwd" if reverse else "lru_fwd",
    )(*args)


def _group_rms(x, w, ones):
    xx = x * x
    hi = xx.astype(BF16)
    lo = (xx - hi.astype(F32)).astype(BF16)
    ss = (jnp.dot(hi, ones, preferred_element_type=F32)
          + jnp.dot(lo, ones, preferred_element_type=F32))
    return x * lax.rsqrt(ss * (1.0 / DA_QK) + EPS) * w


def _rope(x, cos, sin):
    lane = lax.broadcasted_iota(jnp.int32, x.shape, 1)
    first = (lane % (2 * ROPE_PAIRS)) < ROPE_PAIRS
    w = x.shape[1]
    partner = jnp.where(first, pltpu.roll(x, w - ROPE_PAIRS, 1), pltpu.roll(x, ROPE_PAIRS, 1))
    return x * cos + partner * sin


def _prep_kernel(rope, *refs):
    if rope:
        dq_ref, dk_ref, qw_ref, kw_ref, ones_ref, cos_ref, sin_ref, q_out, k_out = refs
    else:
        dq_ref, dk_ref, qw_ref, kw_ref, ones_ref, dv_ref, q_out, k_out, kf_out, vf_out = refs
        vf_out[...] = dv_ref[...].astype(F32)
    ones = ones_ref[...]
    q = _group_rms(dq_ref[...].astype(F32), qw_ref[...], ones)
    k = _group_rms(dk_ref[...].astype(F32), kw_ref[...], ones)
    if rope:
        cos = jnp.concatenate([cos_ref[...]] * 4, axis=1)
        sin = jnp.concatenate([sin_ref[...]] * 4, axis=1)
        q = _rope(q, cos, sin)
        k = _rope(k, cos, sin)
    else:
        kf_out[...] = k
    q_out[...] = (q * (DA_QK ** -0.5 * math.log2(math.e))).astype(BF16)
    k_out[...] = k.astype(BF16)


def _prep_call(geom, latent, proj, qw, kw, ones, cos=None, sin=None):
    tm = 512
    w = DA_HEADS * 2 * DA_QK
    if latent:
        n, off = geom.n_lat, geom.n_ctx // tm
        per = geom.dec_seq // tm
    else:
        n, off = geom.n_ctx, 0
    in_specs = [pl.BlockSpec((tm, w), lambda i: (i + off, C_DQ // w)),
                pl.BlockSpec((tm, w), lambda i: (i + off, C_DK // w)),
                pl.BlockSpec((1, w), lambda i: (0, 0)),
                pl.BlockSpec((1, w), lambda i: (0, 0)),
                pl.BlockSpec((w, w), lambda i: (0, 0))]
    args = [proj, proj, qw, kw, ones]
    out_specs = [pl.BlockSpec((tm, w), lambda i: (i, 0)), pl.BlockSpec((tm, w), lambda i: (i, 0))]
    out_shape = [jax.ShapeDtypeStruct((n, w), BF16), jax.ShapeDtypeStruct((n, w), BF16)]
    if latent:
        in_specs += [pl.BlockSpec((tm, 2 * DA_QK), lambda i: (i % per, 0)),
                     pl.BlockSpec((tm, 2 * DA_QK), lambda i: (i % per, 0))]
        args += [cos, sin]
    else:
        in_specs.append(pl.BlockSpec((tm, w), lambda i: (i, C_DV // w)))
        args.append(proj)
        out_specs += [pl.BlockSpec((tm, w), lambda i: (i, 0)), pl.BlockSpec((tm, w), lambda i: (i, 0))]
        out_shape += [jax.ShapeDtypeStruct((n, w), F32), jax.ShapeDtypeStruct((n, w), F32)]
    return pl.pallas_call(
        functools.partial(_prep_kernel, latent),
        grid=(n // tm,),
        in_specs=in_specs, out_specs=out_specs, out_shape=out_shape,
        compiler_params=_cparams(("arbitrary",)),
        name="qk_prep_lat" if latent else "qk_prep_ctx",
    )(*args)


ATT_KC = 256
ATT_TQ = 256
LOG2E = math.log2(math.e)
ATT_SAFE_LOGIT = 60.0


def _attn_kernel(out_scale, has_cache, *refs):
    if has_cache:
        par_ref, q_ref, kc_ref, vc_ref, kl_ref, vl_ref, sw_ref, o_ref, e_scr, o_scr = refs
        srcs = [(kc_ref, vc_ref), (kl_ref, vl_ref)]
    else:
        par_ref, q_ref, kl_ref, vl_ref, sw_ref, o_ref, e_scr, o_scr = refs
        srcs = [(kl_ref, vl_ref)]
    chunks = [(kr, vr, st) for kr, vr in srcs for st in range(0, kr.shape[0], ATT_KC)]
    lam = par_ref[0]
    no_shift = par_ref[1] > 0.5
    tqs = ATT_TQ
    nsub = q_ref.shape[0] // tqs
    nt = (((1,), (1,)), ((), ()))
    half = ATT_KC // 2

    def stacked_q(sb):
        q = q_ref[sb * tqs:(sb + 1) * tqs, :]
        lane = lax.broadcasted_iota(jnp.int32, q.shape, 1)
        zero = jnp.zeros_like(q)
        return jnp.concatenate([jnp.where(lane < DA_QK, q, zero), jnp.where(lane >= DA_QK, q, zero)], axis=0)

    def logits(qq, c):
        kr, vr, st = chunks[c]
        return lax.dot_general(qq, kr[st:st + ATT_KC, :], nt, preferred_element_type=F32)

    def fold(total, e):
        part = e[:, :half] + e[:, half:]
        return part if total is None else total + part

    def row_stats(lsum):
        l = jnp.sum(lsum, axis=-1, keepdims=True)
        l1 = l[0:tqs]
        return l1, lam * l1 / l[tqs:2 * tqs]

    def pv(acc, buf, c, rho):
        kr, vr, st = chunks[c]
        w = (e_scr[buf, c, 0:tqs, :] - rho * e_scr[buf, c, tqs:2 * tqs, :]).astype(BF16)
        t = jnp.dot(w, vr[st:st + ATT_KC, :], preferred_element_type=F32)
        return t if acc is None else acc + t

    nck = len(chunks)

    @pl.when(no_shift)
    def _():
        stats = None
        for sb in range(nsub + 1):
            qq = stacked_q(sb) if sb < nsub else None
            lsum, acc = None, None
            for c in range(nck):
                if sb < nsub:
                    e = jnp.exp2(logits(qq, c))
                    e_scr[sb % 2, c] = e
                    lsum = fold(lsum, e)
                if sb > 0:
                    acc = pv(acc, (sb - 1) % 2, c, stats[1])
            if sb > 0:
                o_scr[(sb - 1) * tqs:sb * tqs, :] = acc / stats[0]
            if sb < nsub:
                stats = row_stats(lsum)

    @pl.when(jnp.logical_not(no_shift))
    def _():
        for sb in range(nsub):
            qq = stacked_q(sb)
            m = None
            for c in range(nck):
                s = logits(qq, c)
                e_scr[0, c] = s
                mc = jnp.max(s, axis=-1, keepdims=True)
                m = mc if m is None else jnp.maximum(m, mc)
            lsum = None
            for c in range(nck):
                e = jnp.exp2(e_scr[0, c] - m)
                e_scr[0, c] = e
                lsum = fold(lsum, e)
            l1, rho = row_stats(lsum)
            acc = None
            for c in range(nck):
                acc = pv(acc, 0, c, rho)
            o_scr[sb * tqs:(sb + 1) * tqs, :] = acc / l1

    o = o_scr[...]
    y = o * lax.rsqrt(jnp.mean(o * o, axis=-1, keepdims=True) + EPS) * sw_ref[...]
    o_ref[...] = (y * out_scale).astype(BF16)


def _attn_call(par, lam_init, q2d, k2d, proj, v_row_off, n_b, t_q, t_kl, tq, subln_w, cache=None):
    hw = 2 * DA_QK
    nq = t_q // tq
    vcol = C_DV // DA_V
    in_specs = [pl.BlockSpec(memory_space=pltpu.SMEM),
                pl.BlockSpec((tq, hw), lambda b, h, qi: (b * nq + qi, h))]
    args = [par, q2d]
    n_chunks = t_kl // ATT_KC
    if cache is not None:
        kc, vc = cache
        p = kc.shape[1]
        n_chunks += p // ATT_KC
        in_specs += [pl.BlockSpec((None, p, hw), lambda b, h, qi: (b, 0, h)),
                     pl.BlockSpec((None, p, DA_V), lambda b, h, qi: (b, 0, h))]
        args += [kc, vc]
    in_specs += [pl.BlockSpec((t_kl, hw), lambda b, h, qi: (b, h)),
                 pl.BlockSpec((t_kl, DA_V), lambda b, h, qi: (v_row_off + b, vcol + h)),
                 pl.BlockSpec((1, DA_V), lambda b, h, qi: (0, 0))]
    args += [k2d, proj, subln_w.reshape(1, DA_V)]
    return pl.pallas_call(
        functools.partial(_attn_kernel, 1.0 - lam_init, cache is not None),
        grid=(n_b, DA_HEADS, nq),
        in_specs=in_specs,
        out_specs=pl.BlockSpec((tq, DA_V), lambda b, h, qi: (b * nq + qi, h)),
        out_shape=jax.ShapeDtypeStruct((n_b * t_q, DA_HEADS * DA_V), BF16),
        scratch_shapes=[pltpu.VMEM((2 if tq > ATT_TQ else 1, n_chunks, 2 * ATT_TQ, ATT_KC), F32),
                        pltpu.VMEM((tq, DA_V), F32)],
        compiler_params=_cparams(("arbitrary", "arbitrary", "arbitrary")),
        name="diff_attn_lat" if cache is not None else "diff_attn_ctx",
    )(*args)


def _ret_state_update(kt, v, kd, cd, s_old):
    parts = []
    for h in range(RET_HEADS):
        rows = slice(h * RET_QK, (h + 1) * RET_QK)
        kh = (kt[rows, :].astype(F32) * kd[rows, :]).astype(BF16)
        parts.append(jnp.dot(kh, v[:, h * RET_V:(h + 1) * RET_V], preferred_element_type=F32))
    return cd * s_old + jnp.concatenate(parts, axis=0)


def _ret_bwd_kernel(geom, kt_ref, v_ref, kd_ref, cd_ref, s0_ref, sstart_ref, send_ref, s_scr):
    i = geom.n_blocks - 1 - pl.program_id(0)

    @pl.when(geom.seq_end(i))
    def _():
        s_scr[...] = s0_ref[...]

    s_old = s_scr[...]
    sstart_ref[...] = s_old
    kt = kt_ref[...] * jnp.asarray(RET_QK ** -0.5, BF16)
    s_new = _ret_state_update(kt, v_ref[...], kd_ref[...], cd_ref[...], s_old)
    s_scr[...] = s_new
    send_ref[...] = s_new


def _ret_bwd_call(geom, proj, rkt, kd_b, cd_b, s0):
    nb = geom.n_blocks
    hs = RET_HEADS * RET_QK

    def blk(g):
        return nb - 1 - g

    return pl.pallas_call(
        functools.partial(_ret_bwd_kernel, geom),
        grid=(nb,),
        in_specs=[pl.BlockSpec((hs, BLK), lambda g: (0, blk(g))),
                  pl.BlockSpec((BLK, RET_HEADS * RET_V), lambda g: (blk(g), C_RV // (RET_HEADS * RET_V))),
                  pl.BlockSpec((hs, BLK), lambda g: (0, 0)),
                  pl.BlockSpec((hs, RET_V), lambda g: (0, 0)),
                  pl.BlockSpec((None, None, hs, RET_V), lambda g: (geom.seq_id(blk(g)), 1, 0, 0))],
        out_specs=[pl.BlockSpec((None, hs, RET_V), lambda g: (blk(g), 0, 0)),
                   pl.BlockSpec((None, hs, RET_V), lambda g: (blk(g), 0, 0))],
        out_shape=[jax.ShapeDtypeStruct((nb, hs, RET_V), F32),
                   jax.ShapeDtypeStruct((nb, hs, RET_V), F32)],
        scratch_shapes=[pltpu.VMEM((hs, RET_V), F32)],
        compiler_params=_cparams(("arbitrary",)),
        name="ret_bwd_state",
    )(rkt, proj, kd_b, cd_b, s0)


def _ret_main_kernel(geom, q_ref, kt_ref, v_ref, g_ref, dsum_ref, qdf_ref, qdb_ref, kd_ref, cd_ref,
                     s0_ref, sb_ref, o_ref, send_ref, s_scr):
    i = pl.program_id(0)

    @pl.when(geom.seq_start(i))
    def _():
        s_scr[...] = s0_ref[...]

    s_f = s_scr[...]
    s_fb = s_f.astype(BF16)
    s_bb = sb_ref[...].astype(BF16)
    q = q_ref[...].astype(F32)
    kt = kt_ref[...] * jnp.asarray(RET_QK ** -0.5, BF16)
    v = v_ref[...]
    lane = lax.broadcasted_iota(jnp.int32, q.shape, 1)
    for h in range(RET_HEADS):
        in_head = (lane >= h * RET_QK) & (lane < (h + 1) * RET_QK)
        qh = jnp.where(in_head, q, 0.0)
        vh = v[:, h * RET_V:(h + 1) * RET_V]
        sc = jnp.dot(qh.astype(BF16), kt, preferred_element_type=F32) * dsum_ref[h]
        o = jnp.dot(sc.astype(BF16), vh, preferred_element_type=F32)
        o += jnp.dot((qh * qdf_ref[...]).astype(BF16), s_fb, preferred_element_type=F32)
        o += jnp.dot((qh * qdb_ref[...]).astype(BF16), s_bb, preferred_element_type=F32)
        y = o * lax.rsqrt(jnp.mean(o * o, axis=-1, keepdims=True) + EPS)
        gv = g_ref[:, h * RET_V:(h + 1) * RET_V].astype(F32)
        o_ref[:, h * RET_V:(h + 1) * RET_V] = (y * (gv * jax.nn.sigmoid(gv))).astype(BF16)
    s_new = _ret_state_update(kt, v, kd_ref[...], cd_ref[...], s_f)
    s_scr[...] = s_new
    send_ref[...] = s_new


def _ret_main_call(geom, proj, rkt, dsum, qdf, qdb, kd_f, cd_f, s0, sb_start):
    nb = geom.n_blocks
    hs = RET_HEADS * RET_QK
    hv = RET_HEADS * RET_V
    return pl.pallas_call(
        functools.partial(_ret_main_kernel, geom),
        grid=(nb,),
        in_specs=[pl.BlockSpec((BLK, hs), lambda g: (g, C_RQ // hs)),
                  pl.BlockSpec((hs, BLK), lambda g: (0, g)),
                  pl.BlockSpec((BLK, hv), lambda g: (g, C_RV // hv)),
                  pl.BlockSpec((BLK, hv), lambda g: (g, C_RG // hv)),
                  pl.BlockSpec((RET_HEADS, BLK, BLK), lambda g: (0, 0, 0)),
                  pl.BlockSpec((BLK, hs), lambda g: (0, 0)),
                  pl.BlockSpec((BLK, hs), lambda g: (0, 0)),
                  pl.BlockSpec((hs, BLK), lambda g: (0, 0)),
                  pl.BlockSpec((hs, RET_V), lambda g: (0, 0)),
                  pl.BlockSpec((None, None, hs, RET_V), lambda g: (geom.seq_id(g), 0, 0, 0)),
                  pl.BlockSpec((None, hs, RET_V), lambda g: (g, 0, 0))],
        out_specs=[pl.BlockSpec((BLK, hv), lambda g: (g, 0)),
                   pl.BlockSpec((None, hs, RET_V), lambda g: (g, 0, 0))],
        out_shape=[jax.ShapeDtypeStruct((geom.n_tok, hv), BF16),
                   jax.ShapeDtypeStruct((nb, hs, RET_V), F32)],
        scratch_shapes=[pltpu.VMEM((hs, RET_V), F32)],
        compiler_params=_cparams(("arbitrary",)),
        name="ret_main",
    )(proj, rkt, proj, proj, dsum, qdf, qdb, kd_f, cd_f, s0, sb_start)


def _ret_tables(ret_decay_l):
    log_g = jax.nn.log_sigmoid(ret_decay_l.astype(F32))
    pos = jnp.arange(BLK, dtype=F32)
    diff = pos[:, None] - pos[None, :]
    lf = log_g[0][:, None, None]
    lb = log_g[1][:, None, None]
    dsum = (jnp.where(diff >= 0, jnp.exp(jnp.maximum(diff, 0.0)[None] * lf), 0.0)
            + jnp.where(diff <= 0, jnp.exp(jnp.maximum(-diff, 0.0)[None] * lb), 0.0))

    def per_lane(e, lg):
        return jnp.repeat(jnp.exp(e[:, None] * lg[None, :]), RET_QK, axis=1)

    qdf = per_lane(pos + 1.0, log_g[0])
    qdb = per_lane(BLK - pos, log_g[1])
    kd_f = per_lane(BLK - 1.0 - pos, log_g[0]).T
    kd_b = per_lane(pos, log_g[1]).T
    cd_f = jnp.broadcast_to(jnp.repeat(jnp.exp(BLK * log_g[0]), RET_QK)[:, None], (RET_HEADS * RET_QK, RET_V))
    cd_b = jnp.broadcast_to(jnp.repeat(jnp.exp(BLK * log_g[1]), RET_QK)[:, None], (RET_HEADS * RET_QK, RET_V))
    return dsum, qdf, qdb, kd_f, kd_b, cd_f, cd_b


def _merge_kernel(n_ctx_tiles, ba_ref, bbc_ref, bbl_ref, bc_ref, g0_ref, g1_ref, g2_ref, xc_ref, xl_ref,
                  gate_ref, sc_ref, sh_ref, nw_ref, wb_ref, wo_ref, rhi_ref, rlo_ref, x1_ref, h2_ref, h2p_ref,
                  lt_ref):
    branches = (ba_ref[...], _pick_part(n_ctx_tiles, bbc_ref, bbl_ref), bc_ref[...])
    acc = None
    for br, (b, g_ref) in enumerate(zip(branches, (g0_ref, g1_ref, g2_ref))):
        p = jnp.dot(b, wb_ref[br], preferred_element_type=F32)
        t = (0.5 * jnp.tanh(0.5 * g_ref[...].astype(F32)) + 0.5) * p
        acc = t if acc is None else acc + t
    m = jnp.dot(acc.astype(BF16), wo_ref[...], preferred_element_type=F32)
    x1 = _pick_part(n_ctx_tiles, xc_ref, xl_ref) + gate_ref[...] * m
    x1_ref[...] = x1
    ms = jnp.mean(x1 * x1, axis=-1, keepdims=True)
    h2 = x1 * lax.rsqrt(ms + EPS) * nw_ref[...] * (1.0 + sc_ref[...]) + sh_ref[...]
    h2b = h2.astype(BF16)
    h2_ref[...] = h2b
    h2p_ref[...] = _pack_halves(h2b.astype(F32))
    h2lo = (h2 - h2b.astype(F32)).astype(BF16)
    nt = (((1,), (1,)), ((), ()))
    lt_ref[...] = (lax.dot_general(rhi_ref[...], h2b, nt, preferred_element_type=F32)
                   + lax.dot_general(rhi_ref[...], h2lo, nt, preferred_element_type=F32)
                   + lax.dot_general(rlo_ref[...], h2b, nt, preferred_element_type=F32))


def _merge_call(geom, l, ba, bb_ctx, bb_lat, bc, proj, x_ctx, x_lat, mod6, norm2_w, wb_bf, wo_bf, r_hi, r_lo):
    tm = 512
    gcol = C_GL // D_MODEL
    full = lambda shape: pl.BlockSpec(shape, lambda i: tuple(0 for _ in shape))
    tok = lambda w: pl.BlockSpec((tm, w), lambda i: (i, 0))
    return pl.pallas_call(
        functools.partial(_merge_kernel, geom.n_ctx // tm),
        grid=(geom.n_tok // tm,),
        in_specs=[tok(BRANCH_W)] + _split_in_specs(geom, tm, BRANCH_W, 1) + [tok(BRANCH_W),
                  pl.BlockSpec((tm, D_MODEL), lambda i: (i, gcol)),
                  pl.BlockSpec((tm, D_MODEL), lambda i: (i, gcol + 1)),
                  pl.BlockSpec((tm, D_MODEL), lambda i: (i, gcol + 2))]
                 + _split_in_specs(geom, tm, D_MODEL, 1) + [
                  _mod_spec(geom, l, 2, tm, 1), _mod_spec(geom, l, 4, tm, 1), _mod_spec(geom, l, 3, tm, 1),
                  full((1, D_MODEL)),
                  full((N_BRANCH, BRANCH_W, D_MODEL)), full((D_MODEL, D_MODEL)),
                  full((N_EXPERTS, D_MODEL)), full((N_EXPERTS, D_MODEL))],
        out_specs=[tok(D_MODEL), tok(D_MODEL), tok(D_MODEL // 2), pl.BlockSpec((N_EXPERTS, tm), lambda i: (0, i))],
        out_shape=[jax.ShapeDtypeStruct((geom.n_tok, D_MODEL), F32),
                   jax.ShapeDtypeStruct((geom.n_tok, D_MODEL), BF16),
                   jax.ShapeDtypeStruct((geom.n_tok, D_MODEL // 2), jnp.uint32),
                   jax.ShapeDtypeStruct((N_EXPERTS, geom.n_tok), F32)],
        compiler_params=_cparams(("arbitrary",)),
        name="merge_out",
    )(ba, bb_ctx, bb_lat, bc, proj, proj, proj, x_ctx, x_lat, mod6, mod6, mod6,
      norm2_w.reshape(1, D_MODEL), wb_bf, wo_bf, r_hi, r_lo)


def _router_kernel(lt_ref, bias_ref, ltri_ref, utri_ref, g_ref, slot_ref, cnt_ref, cnt_scr):
    per = N_EXPERTS // N_GROUPS
    tm = lt_ref.shape[1]
    scores = jax.nn.sigmoid(lt_ref[...])
    biased = scores + bias_ref[...]
    b3 = biased.reshape(N_GROUPS, per, tm)
    neg = jnp.float32(-jnp.inf)
    m1 = jnp.max(b3, axis=1, keepdims=True)
    is_m1 = b3 == m1
    cnt = jnp.sum(is_m1.astype(F32), axis=1, keepdims=True)
    m2 = jnp.max(jnp.where(is_m1, neg, b3), axis=1, keepdims=True)
    grp = (m1 + jnp.where(cnt >= 2.0, m1, m2)).reshape(N_GROUPS, tm)
    gidx = lax.broadcasted_iota(jnp.int32, (N_GROUPS, tm), 0)
    grank = jnp.zeros((N_GROUPS, tm), F32)
    for g2 in range(N_GROUPS):
        other = grp[g2:g2 + 1, :]
        ahead = (other > grp) | ((other == grp) & (gidx > g2))
        grank += ahead.astype(F32)
    gsel = (grank < float(TOPK_GROUPS)).astype(F32)
    emask = jnp.broadcast_to(gsel.reshape(N_GROUPS, 1, tm), (N_GROUPS, per, tm)).reshape(N_EXPERTS, tm)
    masked = jnp.where(emask > 0.0, biased, neg)
    eidx = lax.broadcasted_iota(jnp.int32, (N_EXPERTS, tm), 0)
    erank = jnp.zeros((N_EXPERTS, tm), F32)
    for e2 in range(N_EXPERTS):
        other = masked[e2:e2 + 1, :]
        ahead = (other > masked) | ((other == masked) & (eidx > e2))
        erank += ahead.astype(F32)
    sel = erank < float(TOP_K)
    w = jnp.where(sel, scores, 0.0)
    gates_t = w / jnp.sum(w, axis=0, keepdims=True) * ROUTED_SCALE

    @pl.when(pl.program_id(0) == 0)
    def _():
        cnt_scr[...] = jnp.zeros_like(cnt_scr)

    selb = sel.astype(BF16)
    slot = jnp.dot(ltri_ref[...], selb, preferred_element_type=F32)
    carry = cnt_scr[:, 0:1]
    rank = jnp.dot(selb, utri_ref[...], preferred_element_type=F32) + carry
    cnt_new = cnt_scr[...] + jnp.sum(sel.astype(F32), axis=1, keepdims=True)
    cnt_scr[...] = cnt_new
    cnt_ref[...] = cnt_new
    eid_f = eidx.astype(F32)
    g_rows, e_rows, r_rows = [], [], []
    for k in range(TOP_K):
        mk = jnp.where(sel & (slot == float(k)), 1.0, 0.0)
        g_rows.append(jnp.sum(mk * gates_t, axis=0, keepdims=True))
        e_rows.append(jnp.sum(mk * eid_f, axis=0, keepdims=True))
        r_rows.append(jnp.sum(mk * rank, axis=0, keepdims=True))
    slot_ref[...] = jnp.concatenate(e_rows + r_rows, axis=0).astype(jnp.int32)
    pad = jnp.zeros((GATE_W - TOP_K, tm), F32)
    g_ref[...] = jnp.concatenate(g_rows + [pad], axis=0).T


ROUTER_TM = 512


def _router_call(geom, logits_t, bias):
    tm = ROUTER_TM
    ltri = jnp.asarray(np.tril(np.ones((N_EXPERTS, N_EXPERTS), np.float32), -1), BF16)
    utri = jnp.asarray(np.triu(np.ones((tm, tm), np.float32), 1), BF16)
    return pl.pallas_call(
        _router_kernel,
        grid=(geom.n_tok // tm,),
        in_specs=[pl.BlockSpec((N_EXPERTS, tm), lambda i: (0, i)),
                  pl.BlockSpec((N_EXPERTS, 1), lambda i: (0, 0)),
                  pl.BlockSpec((N_EXPERTS, N_EXPERTS), lambda i: (0, 0)),
                  pl.BlockSpec((tm, tm), lambda i: (0, 0))],
        out_specs=[pl.BlockSpec((tm, GATE_W), lambda i: (i, 0)),
                   pl.BlockSpec((2 * TOP_K, tm), lambda i: (0, i)),
                   pl.BlockSpec((N_EXPERTS, GATE_W), lambda i: (0, 0))],
        out_shape=[jax.ShapeDtypeStruct((geom.n_tok, GATE_W), F32),
                   jax.ShapeDtypeStruct((2 * TOP_K, geom.n_tok), jnp.int32),
                   jax.ShapeDtypeStruct((N_EXPERTS, GATE_W), F32)],
        scratch_shapes=[pltpu.VMEM((N_EXPERTS, GATE_W), F32)],
        compiler_params=_cparams(("arbitrary",)),
        name="router",
    )(logits_t, bias.reshape(N_EXPERTS, 1), ltri, utri)


MOE_TR = 256
SC_CORES = 2
SC_SUBCORES = 16
SC_CHUNK = 64


def _sc_worker_base(rows_per_worker):
    wid = lax.axis_index("s") * SC_CORES + lax.axis_index("c")
    return wid * rows_per_worker


def _sc_scatter_rows(table, pos_flat, n_slots, n_rows_out):
    n, d = table.shape
    nw = SC_CORES * SC_SUBCORES
    assert n % (nw * SC_CHUNK) == 0
    per_w = n // nw
    mesh = plsc.VectorSubcoreMesh(core_axis_name="c", subcore_axis_name="s")

    @functools.partial(
        pl.kernel, mesh=mesh,
        out_type=jax.ShapeDtypeStruct((n_rows_out, d), table.dtype),
        scratch_types=[[pltpu.VMEM((SC_CHUNK,), jnp.int32) for _ in range(n_slots)],
                       pltpu.VMEM((SC_CHUNK, d), table.dtype),
                       pltpu.SemaphoreType.DMA],
    )
    def scatter(table_hbm, pos_hbm, out_hbm, idx_v, rows_v, sem):
        base = _sc_worker_base(per_w)

        @pl.loop(0, per_w // SC_CHUNK)
        def _(ci):
            off = pl.multiple_of(base + ci * SC_CHUNK, 8)
            for k in range(n_slots):
                pltpu.sync_copy(pos_hbm.at[pl.ds(pl.multiple_of(k * n + off, 8), SC_CHUNK)], idx_v[k])
            pltpu.sync_copy(table_hbm.at[pl.ds(off, SC_CHUNK)], rows_v)
            copies = [pltpu.make_async_copy(rows_v, out_hbm.at[idx_v[k]], sem) for k in range(n_slots)]
            for cp in copies:
                cp.start()
            for cp in copies:
                cp.wait()

    return scatter(table, pos_flat)


def _sc_gather_rows(table, idx):
    b = idx.shape[0]
    d = table.shape[1]
    nw = SC_CORES * SC_SUBCORES
    nbuf = 2
    assert b % (nw * SC_CHUNK * nbuf) == 0
    per_w = b // nw
    n_chunks = per_w // SC_CHUNK
    mesh = plsc.VectorSubcoreMesh(core_axis_name="c", subcore_axis_name="s")

    @functools.partial(
        pl.kernel, mesh=mesh,
        out_type=jax.ShapeDtypeStruct((b, d), table.dtype),
        scratch_types=[pltpu.VMEM((per_w,), jnp.int32),
                       [pltpu.VMEM((SC_CHUNK, d), table.dtype) for _ in range(nbuf)],
                       [pltpu.SemaphoreType.DMA for _ in range(nbuf)],
                       [pltpu.SemaphoreType.DMA for _ in range(nbuf)]],
    )
    def gather(table_hbm, idx_hbm, out_hbm, idx_v, rows, gsem, wsem):
        base = _sc_worker_base(per_w)
        pltpu.sync_copy(idx_hbm.at[pl.ds(pl.multiple_of(base, 8), per_w)], idx_v)

        def fetch(ci, slot):
            src = table_hbm.at[idx_v.at[pl.ds(pl.multiple_of(ci * SC_CHUNK, 8), SC_CHUNK)]]
            return pltpu.make_async_copy(src, rows[slot], gsem[slot])

        def put(ci, slot):
            dst = out_hbm.at[pl.ds(pl.multiple_of(base + ci * SC_CHUNK, 8), SC_CHUNK)]
            return pltpu.make_async_copy(rows[slot], dst, wsem[slot])

        for slot in range(nbuf):
            fetch(slot, slot).start()

        @pl.loop(0, n_chunks, step=nbuf)
        def _(c0):
            for slot in range(nbuf):
                ci = c0 + slot
                fetch(ci, slot).wait()
                put(ci, slot).start()
                put(ci, slot).wait()

                @pl.when(ci + nbuf < n_chunks)
                def _():
                    fetch(ci + nbuf, slot).start()

    return gather(table, idx)


def _route_positions(n_tok, slots, counts):
    cnt = counts[:, 0].astype(jnp.int32)
    cnt_pad = ((cnt + MOE_TR - 1) // MOE_TR) * MOE_TR
    off_end = jnp.cumsum(cnt_pad)
    off = off_end - cnt_pad
    eid, rank = slots[:TOP_K], slots[TOP_K:]
    eids = jnp.arange(N_EXPERTS, dtype=jnp.int32)
    pos = jnp.sum(jnp.where(eid[..., None] == eids, off, 0), axis=-1) + rank
    n_tiles = (TOP_K * n_tok) // MOE_TR + N_EXPERTS
    tile_start = jnp.arange(n_tiles, dtype=jnp.int32) * MOE_TR
    tile_expert = jnp.sum((tile_start[:, None] >= off_end[None, :]).astype(jnp.int32), axis=1)
    tile_expert = jnp.minimum(tile_expert, N_EXPERTS - 1)
    n_used = (off_end[-1] // MOE_TR).reshape(1)
    return pos.reshape(-1), tile_expert, n_used, n_tiles


def _expert_ffn(x_lo, x_hi, gu, dn):
    half = D_MODEL // 2
    a = (jnp.dot(x_lo, gu[0:half, :], preferred_element_type=F32)
         + jnp.dot(x_hi, gu[half:, :], preferred_element_type=F32))
    hg = a[:, :D_EXPERT]
    act = (hg * jax.nn.sigmoid(hg)) * a[:, D_EXPERT:]
    return jnp.dot(act.astype(BF16), dn, preferred_element_type=F32)


def _experts_kernel(te_ref, nu_ref, x_ref, gu_ref, dn_ref, y_ref):
    i = pl.program_id(0)

    @pl.when(i < nu_ref[0])
    def _():
        lo, hi = _unpack_halves(x_ref[...])
        y = _expert_ffn(lo.astype(BF16), hi.astype(BF16), gu_ref[...], dn_ref[...])
        y_ref[...] = _pack_halves(y.astype(BF16).astype(F32))

    @pl.when(i >= nu_ref[0])
    def _():
        y_ref[...] = jnp.zeros_like(y_ref)


def _experts_call(l, xs, tile_expert, n_used, n_tiles, gu_bf, dn_bf):
    half = D_MODEL // 2
    grid_spec = pltpu.PrefetchScalarGridSpec(
        num_scalar_prefetch=2,
        grid=(n_tiles,),
        in_specs=[pl.BlockSpec((MOE_TR, half), lambda i, te, nu: (i, 0)),
                  pl.BlockSpec((None, None, D_MODEL, 2 * D_EXPERT), lambda i, te, nu: (l, te[i], 0, 0)),
                  pl.BlockSpec((None, None, D_EXPERT, D_MODEL), lambda i, te, nu: (l, te[i], 0, 0))],
        out_specs=pl.BlockSpec((MOE_TR, half), lambda i, te, nu: (i, 0)),
    )
    return pl.pallas_call(
        _experts_kernel,
        grid_spec=grid_spec,
        out_shape=jax.ShapeDtypeStruct((n_tiles * MOE_TR, half), jnp.uint32),
        compiler_params=_cparams(("arbitrary",)),
        name="moe_experts",
    )(tile_expert, n_used, xs, gu_bf, dn_bf)


def _moe_out_kernel(n_ctx_tiles, yt_ref, g_ref, h_ref, sgu_ref, sdn_ref, x1_ref, gate_ref, oc_ref, ol_ref):
    i = pl.program_id(0)
    gts = g_ref[...]
    lane = lax.broadcasted_iota(jnp.int32, gts.shape, 1)
    acc_lo, acc_hi = None, None
    for k in range(TOP_K):
        ge = jnp.sum(jnp.where(lane == k, gts, 0.0), axis=1, keepdims=True)
        lo, hi = _unpack_halves(yt_ref[k])
        acc_lo = ge * lo if acc_lo is None else acc_lo + ge * lo
        acc_hi = ge * hi if acc_hi is None else acc_hi + ge * hi
    routed = jnp.concatenate([acc_lo, acc_hi], axis=1)
    h = h_ref[...]
    half = D_MODEL // 2
    shared = _expert_ffn(h[:, :half], h[:, half:], sgu_ref[...], sdn_ref[...])
    y = x1_ref[...] + gate_ref[...] * (routed + shared)

    @pl.when(i < n_ctx_tiles)
    def _():
        oc_ref[...] = y

    @pl.when(i >= n_ctx_tiles)
    def _():
        ol_ref[...] = y


def _moe_out_call(geom, l, yt, gates, h2, sgu_bf, sdn_bf, x1, mod6):
    tm = 512
    nct = geom.n_ctx // tm
    half = D_MODEL // 2
    return pl.pallas_call(
        functools.partial(_moe_out_kernel, nct),
        grid=(geom.n_tok // tm,),
        in_specs=[pl.BlockSpec((TOP_K, tm, half), lambda i: (0, i, 0)),
                  pl.BlockSpec((tm, GATE_W), lambda i: (i, 0)),
                  pl.BlockSpec((tm, D_MODEL), lambda i: (i, 0)),
                  pl.BlockSpec((None, D_MODEL, 2 * D_EXPERT), lambda i: (l, 0, 0)),
                  pl.BlockSpec((None, D_EXPERT, D_MODEL), lambda i: (l, 0, 0)),
                  pl.BlockSpec((tm, D_MODEL), lambda i: (i, 0)),
                  _mod_spec(geom, l, 5, tm, 1)],
        out_specs=[pl.BlockSpec((tm, D_MODEL), lambda i: (jnp.minimum(i, nct - 1), 0)),
                   pl.BlockSpec((tm, D_MODEL), lambda i: (jnp.maximum(i - nct, 0), 0))],
        out_shape=[jax.ShapeDtypeStruct((geom.n_ctx, D_MODEL), F32),
                   jax.ShapeDtypeStruct((geom.n_lat, D_MODEL), F32)],
        compiler_params=_cparams(("arbitrary",)),
        name="moe_out",
    )(yt, gates, h2, sgu_bf, sdn_bf, x1, mod6)


def _moe(geom, l, h2, h2p, gates, slots, counts, gu_bf, dn_bf, sgu_bf, sdn_bf, x1, mod6):
    pos, tile_expert, n_used, n_tiles = _route_positions(geom.n_tok, slots, counts)
    xs = _sc_scatter_rows(h2p, pos, TOP_K, n_tiles * MOE_TR)
    ys = _experts_call(l, xs, tile_expert, n_used, n_tiles, gu_bf, dn_bf)
    yt = _sc_gather_rows(ys, pos).reshape(TOP_K, geom.n_tok, D_MODEL // 2)
    return _moe_out_call(geom, l, yt, gates, h2, sgu_bf, sdn_bf, x1, mod6)


def _rope_tables(dec_seq):
    rows = dec_seq // GRID_W
    row = jnp.repeat(jnp.arange(rows, dtype=F32), GRID_W)
    col = jnp.tile(jnp.arange(GRID_W, dtype=F32), rows)
    inv = ROPE_BASE ** (-jnp.arange(ROPE_PAIRS, dtype=F32) / ROPE_PAIRS)
    ar = row[:, None] * inv[None, :]
    ac = col[:, None] * inv[None, :]
    cos64 = jnp.concatenate([jnp.cos(ar), jnp.cos(ar), jnp.cos(ac), jnp.cos(ac)], axis=1)
    sin64 = jnp.concatenate([-jnp.sin(ar), jnp.sin(ar), -jnp.sin(ac), jnp.sin(ac)], axis=1)
    return jnp.tile(cos64, (1, 2)), jnp.tile(sin64, (1, 2))


def _block_diag_gate(wg_dir):
    eye = jnp.eye(LRU_BLOCKS, dtype=F32)
    dense = jnp.einsum('gnij,nm->gnimj', wg_dir.astype(F32), eye).reshape(2, D_RNN, D_RNN)
    return jnp.concatenate([dense[0], dense[1]], axis=1)


def kernel(x_prompt, x_sample, cache_k, cache_v, state_lru, state_ret, c, c_ctx, ada_w, ada_b, norm1_w, norm2_w, w_in, conv_w, conv_b, lru_gate_w, lru_gate_b, lru_lambda, q_norm_w, k_norm_w, diff_lambda, subln_w, ret_decay, w_branch, w_out, router_w, router_bias, w_exp_gu, w_exp_down, w_sh_gu, w_sh_down):
    batch, seq, _ = x_prompt.shape
    dec_batch, dec_seq, _ = x_sample.shape
    assert 1 + dec_batch <= MOD_ROWS
    geom = _Geom(batch, seq, dec_batch, dec_seq)
    hs = RET_HEADS * RET_QK
    aw = DA_HEADS * 2 * DA_QK

    x_ctx = x_prompt.reshape(geom.n_ctx, D_MODEL)
    x_lat = x_sample.reshape(geom.n_lat, D_MODEL)
    cvec = jnp.zeros((MOD_ROWS, D_MODEL), F32).at[0].set(c_ctx).at[1:1 + dec_batch].set(c)
    mod6 = _ada_call(cvec, ada_w, ada_b).reshape(DEPTH, MOD_ROWS, 6, 1, D_MODEL)

    ones_bd = jnp.kron(jnp.eye(aw // DA_QK, dtype=F32), jnp.ones((DA_QK, DA_QK), F32)).astype(BF16)
    cos_t, sin_t = _rope_tables(dec_seq)

    w_in_bf = w_in.astype(BF16)
    gu_bf, dn_bf = w_exp_gu.astype(BF16), w_exp_down.astype(BF16)
    sgu_bf, sdn_bf = w_sh_gu.astype(BF16), w_sh_down.astype(BF16)

    ks, vs, lrus, rets = [], [], [], []
    for l in range(DEPTH):
        lam_init = 0.8 - 0.6 * math.exp(-0.3 * l)
        w_rkt_bf = w_in[l][:, C_RK:C_RK + hs].T.astype(BF16)
        proj, rkt = _inproj_call(geom, l, x_ctx, x_lat, mod6, norm1_w[l], w_in_bf, w_rkt_bf)

        sp = jax.nn.softplus(-lru_lambda[l].astype(F32))
        h0 = jnp.concatenate([jnp.zeros((batch, 2, D_RNN), F32), state_lru[:, l].astype(F32)], axis=0)
        h0 = h0.reshape(geom.n_seq, 2, 1, D_RNN)
        cb = conv_b[l].reshape(1, D_RNN)
        lru_args = []
        for d in range(2):
            lru_args.append((_block_diag_gate(lru_gate_w[l, d]).astype(BF16),
                             lru_gate_b[l, d].reshape(1, 2 * D_RNN), sp[d].reshape(1, D_RNN)))
        hf, hf_last = _lru_call(geom, False, proj, conv_w[l], cb, *lru_args[0], h0)
        branch_a, hb_last = _lru_call(geom, True, proj, conv_w[l], cb, *lru_args[1], h0, hf)

        qw = jnp.tile(q_norm_w[l], aw // DA_QK).reshape(1, aw)
        kw = jnp.tile(k_norm_w[l], aw // DA_QK).reshape(1, aw)
        q_c, k_c, k_c32, v_c32 = _prep_call(geom, False, proj, qw, kw, ones_bd)
        q_l, k_l = _prep_call(geom, True, proj, qw, kw, ones_bd, cos_t, sin_t)
        lam_p = diff_lambda[l].astype(F32)
        lam = jnp.exp(jnp.sum(lam_p[0] * lam_p[1])) - jnp.exp(jnp.sum(lam_p[2] * lam_p[3])) + lam_init
        q_bound = DA_QK * jnp.max(jnp.square(q_norm_w[l].astype(F32))) * (DA_QK ** -0.5 * LOG2E) ** 2
        k_bound = DA_QK * jnp.max(jnp.square(k_norm_w[l].astype(F32)))
        kc32 = cache_k[:, l].astype(F32)
        kc_bound = jnp.maximum(k_bound, jnp.max(jnp.sum(jnp.square(kc32), axis=-1)))

        def attn_par(kb):
            ok = (q_bound * kb * 1.05 < ATT_SAFE_LOGIT ** 2).astype(F32)
            return jnp.stack([lam, ok])

        assert geom.n_ctx % dec_seq == 0
        cache = (kc32.reshape(dec_batch, -1, aw).astype(BF16),
                 cache_v[:, l].reshape(dec_batch, -1, DA_HEADS * DA_V).astype(BF16))
        att_c = _attn_call(attn_par(k_bound), lam_init, q_c, k_c, proj, 0, batch, seq, seq, 256, subln_w[l])
        att_l = _attn_call(attn_par(kc_bound), lam_init, q_l, k_l, proj, geom.n_ctx // dec_seq, dec_batch,
                           dec_seq, dec_seq, 4 * ATT_TQ, subln_w[l], cache)

        dsum, qdf, qdb, kd_f, kd_b, cd_f, cd_b = _ret_tables(ret_decay[l])
        s0 = jnp.concatenate([jnp.zeros((batch, 2, hs, RET_V), F32),
                              state_ret[:, l].astype(F32).reshape(dec_batch, 2, hs, RET_V)], axis=0)
        sb_start, sb_end = _ret_bwd_call(geom, proj, rkt, kd_b, cd_b, s0)
        branch_c, sf_end = _ret_main_call(geom, proj, rkt, dsum, qdf, qdb, kd_f, cd_f, s0, sb_start)

        r_t = router_w[l].T.astype(F32)
        r_hi = r_t.astype(BF16)
        r_lo = (r_t - r_hi.astype(F32)).astype(BF16)
        x1, h2, h2p, logits_t = _merge_call(geom, l, branch_a, att_c, att_l, branch_c, proj, x_ctx, x_lat, mod6,
                                            norm2_w[l], w_branch[l].astype(BF16), w_out[l].astype(BF16), r_hi, r_lo)
        gates, slots, counts = _router_call(geom, logits_t, router_bias[l].astype(F32))
        x_ctx, x_lat = _moe(geom, l, h2, h2p, gates, slots, counts, gu_bf, dn_bf, sgu_bf, sdn_bf, x1, mod6)

        ks.append(k_c32.reshape(batch, seq, DA_HEADS, 2, DA_QK))
        vs.append(v_c32.reshape(batch, seq, DA_HEADS, DA_V))
        lrus.append(jnp.stack([hf_last[:batch, 0], hb_last[:batch, 0]], axis=1))
        rets.append(jnp.stack([sf_end[:batch].reshape(batch, RET_HEADS, RET_QK, RET_V),
                               sb_end[:batch].reshape(batch, RET_HEADS, RET_QK, RET_V)], axis=1))

    y_prompt = x_ctx.reshape(batch, seq, D_MODEL)
    y_sample = x_lat.reshape(dec_batch, dec_seq, D_MODEL)
    return (y_prompt, y_sample, jnp.stack(ks, axis=1), jnp.stack(vs, axis=1),
            jnp.stack(lrus, axis=1), jnp.stack(rets, axis=1))
```

```python
import functools
import math

import numpy as np
import jax
import jax.numpy as jnp
from jax import lax
from jax.experimental import pallas as pl
from jax.experimental.pallas import tpu as pltpu
from jax.experimental.pallas import tpu_sc as plsc

F32 = jnp.float32
BF16 = jnp.bfloat16

D_MODEL = 1024
DEPTH = 2
GRID_W = 64
D_RNN = 512
LRU_BLOCKS = 8
LRU_BLOCK = D_RNN // LRU_BLOCKS
CONV_W = 4
LRU_C = 8.0
DA_HEADS = 4
DA_QK = 64
DA_V = 128
ROPE_PAIRS = DA_QK // 4
ROPE_BASE = 10000.0
RET_HEADS = 4
RET_QK = 64
RET_V = 128
BRANCH_W = 512
N_BRANCH = 3
D_IN = 7168
N_EXPERTS = 64
TOP_K = 8
N_GROUPS = 8
TOPK_GROUPS = 4
D_EXPERT = 256
ROUTED_SCALE = 2.5
EPS = 1e-6

C_XA, C_GA, C_DQ, C_DK, C_DV = 0, 512, 1024, 1536, 2048
C_RQ, C_RK, C_RV, C_RG, C_GL = 2560, 2816, 3072, 3584, 4096

BLK = 256
LRU_SUB = 8
LRU_LANES = D_RNN // 128
GATE_W = 128
MOD_ROWS = 8
VMEM_LIMIT = 56 * 1024 * 1024


def _cparams(sem, vmem_limit=VMEM_LIMIT):
    return pltpu.CompilerParams(dimension_semantics=sem, vmem_limit_bytes=vmem_limit)


class _Geom:
    def __init__(self, batch, seq, dec_batch, dec_seq):
        assert seq == BLK and dec_seq % BLK == 0
        self.batch, self.seq, self.dec_batch, self.dec_seq = batch, seq, dec_batch, dec_seq
        self.n_ctx = batch * seq
        self.n_lat = dec_batch * dec_seq
        self.n_tok = self.n_ctx + self.n_lat
        self.ctx_blocks = self.n_ctx // BLK
        self.lat_blocks = dec_seq // BLK
        self.n_blocks = self.n_tok // BLK
        self.n_seq = batch + dec_batch

    def mod_row(self, i, tile):
        nct = self.n_ctx // tile
        per = self.dec_seq // tile
        return jnp.where(i < nct, 0, 1 + (i - nct) // per)

    def seq_id(self, i):
        return jnp.where(i < self.ctx_blocks, i, self.ctx_blocks + (i - self.ctx_blocks) // self.lat_blocks)

    def seq_start(self, i):
        return jnp.logical_or(i < self.ctx_blocks, (i - self.ctx_blocks) % self.lat_blocks == 0)

    def seq_end(self, i):
        return jnp.logical_or(i < self.ctx_blocks, (i - self.ctx_blocks) % self.lat_blocks == self.lat_blocks - 1)


def _ada_kernel(c_ref, w_ref, b_ref, o_ref):
    cv = c_ref[...]
    s = cv * jax.nn.sigmoid(cv)
    o_ref[...] = jnp.dot(s, w_ref[...], preferred_element_type=F32,
                         precision=lax.Precision.HIGHEST) + b_ref[...]


def _ada_call(cvec, ada_w, ada_b):
    depth = ada_w.shape[0]
    nt = 6
    return pl.pallas_call(
        _ada_kernel,
        grid=(depth, nt),
        in_specs=[pl.BlockSpec((MOD_ROWS, D_MODEL), lambda l, j: (0, 0)),
                  pl.BlockSpec((None, D_MODEL, D_MODEL), lambda l, j: (l, 0, j)),
                  pl.BlockSpec((None, 1, D_MODEL), lambda l, j: (l, 0, j))],
        out_specs=pl.BlockSpec((None, MOD_ROWS, D_MODEL), lambda l, j: (l, 0, j)),
        out_shape=jax.ShapeDtypeStruct((depth, MOD_ROWS, 6 * D_MODEL), F32),
        compiler_params=_cparams(("arbitrary", "arbitrary")),
        name="ada_mod",
    )(cvec, ada_w, ada_b.reshape(depth, 1, 6 * D_MODEL))


def _mod_spec(geom, l, which, tile, ngrid):
    if ngrid == 1:
        return pl.BlockSpec((None, None, None, 1, D_MODEL),
                            lambda i: (l, geom.mod_row(i, tile), which, 0, 0))
    return pl.BlockSpec((None, None, None, 1, D_MODEL),
                        lambda i, j: (l, geom.mod_row(i, tile), which, 0, 0))


def _split_in_specs(geom, tile, width, ngrid):
    nct = geom.n_ctx // tile
    if ngrid == 1:
        return [pl.BlockSpec((tile, width), lambda i: (jnp.minimum(i, nct - 1), 0)),
                pl.BlockSpec((tile, width), lambda i: (jnp.maximum(i - nct, 0), 0))]
    return [pl.BlockSpec((tile, width), lambda i, j: (jnp.minimum(i, nct - 1), 0)),
            pl.BlockSpec((tile, width), lambda i, j: (jnp.maximum(i - nct, 0), 0))]


def _pick_part(n_ctx_tiles, c_ref, l_ref):
    return jnp.where(pl.program_id(0) < n_ctx_tiles, c_ref[...], l_ref[...])


def _pack_halves(y):
    w = y.shape[1] // 2
    bits = pltpu.bitcast(y, jnp.uint32)
    return (bits[:, :w] >> 16) | (bits[:, w:] & jnp.uint32(0xFFFF0000))


def _unpack_halves(p):
    return pltpu.bitcast(p << 16, F32), pltpu.bitcast(p & jnp.uint32(0xFFFF0000), F32)


INPROJ_TM = 512
INPROJ_TN = 1024


def _inproj_kernel(n_ctx_tiles, xc_ref, xl_ref, sc_ref, sh_ref, nw_ref, w_ref, wkt_ref, o_ref, kt_ref):
    x = _pick_part(n_ctx_tiles, xc_ref, xl_ref)
    ms = jnp.mean(x * x, axis=-1, keepdims=True)
    y = x * lax.rsqrt(ms + EPS) * nw_ref[...]
    hb = (y * (1.0 + sc_ref[...]) + sh_ref[...]).astype(BF16)
    kt_ref[...] = lax.dot_general(wkt_ref[...], hb, (((1,), (1,)), ((), ())),
                                  preferred_element_type=F32).astype(BF16)
    for j in range(D_IN // INPROJ_TN):
        cols = slice(j * INPROJ_TN, (j + 1) * INPROJ_TN)
        o_ref[:, cols] = jnp.dot(hb, w_ref[:, cols], preferred_element_type=F32).astype(BF16)


def _inproj_call(geom, l, x_ctx, x_lat, mod6, norm_w, w_in_bf, w_rkt_bf):
    tm = INPROJ_TM
    return pl.pallas_call(
        functools.partial(_inproj_kernel, geom.n_ctx // tm),
        grid=(geom.n_tok // tm,),
        in_specs=_split_in_specs(geom, tm, D_MODEL, 1) + [
                  _mod_spec(geom, l, 1, tm, 1),
                  _mod_spec(geom, l, 0, tm, 1),
                  pl.BlockSpec((1, D_MODEL), lambda i: (0, 0)),
                  pl.BlockSpec((None, D_MODEL, D_IN), lambda i: (l, 0, 0), pipeline_mode=pl.Buffered(1)),
                  pl.BlockSpec((RET_HEADS * RET_QK, D_MODEL), lambda i: (0, 0))],
        out_specs=[pl.BlockSpec((tm, D_IN), lambda i: (i, 0)),
                   pl.BlockSpec((RET_HEADS * RET_QK, tm), lambda i: (0, i))],
        out_shape=[jax.ShapeDtypeStruct((geom.n_tok, D_IN), BF16),
                   jax.ShapeDtypeStruct((RET_HEADS * RET_QK, geom.n_tok), BF16)],
        compiler_params=_cparams(("arbitrary",)),
        name="inproj",
    )(x_ctx, x_lat, mod6, mod6, norm_w.reshape(1, D_MODEL), w_in_bf, w_rkt_bf)


def _gelu_tanh(x):
    return 0.5 * x * (1.0 + jnp.tanh(math.sqrt(2.0 / math.pi) * (x + 0.044715 * (x * x * x))))


def _lru_kernel(geom, reverse, *refs):
    if reverse:
        (xa_ref, xp_ref, xn_ref, cw_ref, cb_ref, wg_ref, bg_ref, sp_ref, h0_ref, perm_ref, permt_ref,
         ga_ref, hf_ref, out_ref, hl_ref, c_scr) = refs
    else:
        (xa_ref, xp_ref, xn_ref, cw_ref, cb_ref, wg_ref, bg_ref, sp_ref, h0_ref, perm_ref,
         out_ref, hl_ref, c_scr) = refs
    g = pl.program_id(0)
    i = geom.n_blocks - 1 - g if reverse else g
    start = geom.seq_start(i)
    end = geom.seq_end(i)

    @pl.when(end if reverse else start)
    def _():
        c_scr[...] = h0_ref[...]

    sub_len = BLK // LRU_SUB
    perm = perm_ref[...]
    x = jnp.dot(perm, xa_ref[...], preferred_element_type=F32)
    pm = jnp.where(start, 0.0, 1.0)
    nm = jnp.where(end, 0.0, 1.0)
    hp = xp_ref.shape[0]
    p1 = xp_ref[hp - 1:hp, :].astype(F32) * pm
    p2 = xp_ref[hp - 2:hp - 1, :].astype(F32) * pm
    n0 = xn_ref[0:1, :].astype(F32) * nm
    row = lax.broadcasted_iota(jnp.int32, x.shape, 0)
    xm1 = jnp.where(row < LRU_SUB, pltpu.roll(x, LRU_SUB + 1, 0), pltpu.roll(x, LRU_SUB, 0))
    xm1 = jnp.where(row == 0, p1, xm1)
    xm2 = jnp.where(row < 2 * LRU_SUB, pltpu.roll(x, 2 * LRU_SUB + 1, 0), pltpu.roll(x, 2 * LRU_SUB, 0))
    xm2 = jnp.where(row == 0, p2, jnp.where(row == LRU_SUB, p1, xm2))
    xp1 = jnp.where(row >= BLK - LRU_SUB, pltpu.roll(x, BLK - LRU_SUB - 1, 0),
                    pltpu.roll(x, BLK - LRU_SUB, 0))
    xp1 = jnp.where(row == BLK - 1, n0, xp1)
    xc = (cw_ref[0:1, :] * xm2 + cw_ref[1:2, :] * xm1 + cw_ref[2:3, :] * x
          + cw_ref[3:4, :] * xp1 + cb_ref[...])

    gt = jnp.dot(xc.astype(BF16), wg_ref[...], preferred_element_type=F32) + bg_ref[...]
    r = jax.nn.sigmoid(gt[:, :D_RNN])
    ig = jax.nn.sigmoid(gt[:, D_RNN:])
    a = jnp.exp(-LRU_C * r * sp_ref[...])
    u = jnp.sqrt(1.0 - a * a) * ig * xc

    h = jnp.zeros((LRU_SUB, D_RNN), F32)
    p = jnp.ones((LRU_SUB, D_RNN), F32)
    h_loc = [None] * sub_len
    p_loc = [None] * sub_len
    for t in (range(sub_len - 1, -1, -1) if reverse else range(sub_len)):
        a_t = a[t * LRU_SUB:(t + 1) * LRU_SUB, :]
        h = a_t * h + u[t * LRU_SUB:(t + 1) * LRU_SUB, :]
        p = a_t * p
        h_loc[t] = h
        p_loc[t] = p
    h_in = [None] * LRU_SUB
    state = c_scr[...]
    for k in (range(LRU_SUB - 1, -1, -1) if reverse else range(LRU_SUB)):
        h_in[k] = state
        state = h[k:k + 1, :] + p[k:k + 1, :] * state
    c_scr[...] = state
    hl_ref[...] = state
    h_in = jnp.concatenate(h_in, axis=0)
    h_full = jnp.concatenate([h_loc[t] + p_loc[t] * h_in for t in range(sub_len)], axis=0)
    if reverse:
        gv = jnp.dot(perm, ga_ref[...], preferred_element_type=F32)
        y = (_gelu_tanh(gv) * (hf_ref[...] + h_full)).astype(BF16)
        out_ref[...] = jnp.dot(permt_ref[...], y, preferred_element_type=F32).astype(BF16)
    else:
        out_ref[...] = h_full


def _lru_call(geom, reverse, proj, conv_w, conv_b, wg, bg, sp, h0, hf=None):
    nb = geom.n_blocks
    halo = 16
    hpb = BLK // halo

    def blk(g):
        return nb - 1 - g if reverse else g

    d = 1 if reverse else 0
    in_specs = [
        pl.BlockSpec((BLK, D_RNN), lambda g: (blk(g), C_XA // D_RNN)),
        pl.BlockSpec((halo, D_RNN), lambda g: (jnp.maximum(blk(g) * hpb - 1, 0), C_XA // D_RNN)),
        pl.BlockSpec((halo, D_RNN), lambda g: (jnp.minimum((blk(g) + 1) * hpb, nb * hpb - 1), C_XA // D_RNN)),
        pl.BlockSpec((CONV_W, D_RNN), lambda g: (0, 0)),
        pl.BlockSpec((1, D_RNN), lambda g: (0, 0)),
        pl.BlockSpec((D_RNN, 2 * D_RNN), lambda g: (0, 0)),
        pl.BlockSpec((1, 2 * D_RNN), lambda g: (0, 0)),
        pl.BlockSpec((1, D_RNN), lambda g: (0, 0)),
        pl.BlockSpec((None, None, 1, D_RNN), lambda g: (geom.seq_id(blk(g)), d, 0, 0)),
    ]
    pos = np.arange(BLK)
    perm_np = np.zeros((BLK, BLK), np.float32)
    perm_np[pos, (pos % LRU_SUB) * (BLK // LRU_SUB) + pos // LRU_SUB] = 1.0
    in_specs.append(pl.BlockSpec((BLK, BLK), lambda g: (0, 0)))
    args = [proj, proj, proj, conv_w, conv_b, wg, bg, sp, h0, jnp.asarray(perm_np, BF16)]
    if reverse:
        in_specs += [pl.BlockSpec((BLK, BLK), lambda g: (0, 0)),
                     pl.BlockSpec((BLK, D_RNN), lambda g: (blk(g), C_GA // D_RNN)),
                     pl.BlockSpec((BLK, D_RNN), lambda g: (blk(g), 0))]
        args += [jnp.asarray(perm_np.T, BF16), proj, hf]
        out_dtype = BF16
    else:
        out_dtype = F32
    scratch = [pltpu.VMEM((1, D_RNN), F32)]
    return pl.pallas_call(
        functools.partial(_lru_kernel, geom, reverse),
        grid=(nb,),
        in_specs=in_specs,
        out_specs=[pl.BlockSpec((BLK, D_RNN), lambda g: (blk(g), 0)),
                   pl.BlockSpec((None, 1, D_RNN), lambda g: (blk(g), 0, 0))],
        out_shape=[jax.ShapeDtypeStruct((geom.n_tok, D_RNN), out_dtype),
                   jax.ShapeDtypeStruct((nb, 1, D_RNN), F32)],
        scratch_shapes=scratch,
        compiler_params=_cparams(("arbitrary",)),
        name="lru_bwd" if reverse else "lru_fwd",
    )(*args)


def _group_rms(x, w, ones):
    xx = x * x
    hi = xx.astype(BF16)
    lo = (xx - hi.astype(F32)).astype(BF16)
    ss = (jnp.dot(hi, ones, preferred_element_type=F32)
          + jnp.dot(lo, ones, preferred_element_type=F32))
    return x * lax.rsqrt(ss * (1.0 / DA_QK) + EPS) * w


def _rope(x, cos, sin):
    lane = lax.broadcasted_iota(jnp.int32, x.shape, 1)
    first = (lane % (2 * ROPE_PAIRS)) < ROPE_PAIRS
    w = x.shape[1]
    partner = jnp.where(first, pltpu.roll(x, w - ROPE_PAIRS, 1), pltpu.roll(x, ROPE_PAIRS, 1))
    return x * cos + partner * sin


def _prep_kernel(rope, *refs):
    if rope:
        dq_ref, dk_ref, qw_ref, kw_ref, ones_ref, cos_ref, sin_ref, q_out, k_out = refs
    else:
        dq_ref, dk_ref, qw_ref, kw_ref, ones_ref, dv_ref, q_out, k_out, kf_out, vf_out = refs
        vf_out[...] = dv_ref[...].astype(F32)
    ones = ones_ref[...]
    q = _group_rms(dq_ref[...].astype(F32), qw_ref[...], ones)
    k = _group_rms(dk_ref[...].astype(F32), kw_ref[...], ones)
    if rope:
        cos = jnp.concatenate([cos_ref[...]] * 4, axis=1)
        sin = jnp.concatenate([sin_ref[...]] * 4, axis=1)
        q = _rope(q, cos, sin)
        k = _rope(k, cos, sin)
    else:
        kf_out[...] = k
    q_out[...] = (q * (DA_QK ** -0.5 * math.log2(math.e))).astype(BF16)
    k_out[...] = k.astype(BF16)


def _prep_call(geom, latent, proj, qw, kw, ones, cos=None, sin=None):
    tm = 512
    w = DA_HEADS * 2 * DA_QK
    if latent:
        n, off = geom.n_lat, geom.n_ctx // tm
        per = geom.dec_seq // tm
    else:
        n, off = geom.n_ctx, 0
    in_specs = [pl.BlockSpec((tm, w), lambda i: (i + off, C_DQ // w)),
                pl.BlockSpec((tm, w), lambda i: (i + off, C_DK // w)),
                pl.BlockSpec((1, w), lambda i: (0, 0)),
                pl.BlockSpec((1, w), lambda i: (0, 0)),
                pl.BlockSpec((w, w), lambda i: (0, 0))]
    args = [proj, proj, qw, kw, ones]
    out_specs = [pl.BlockSpec((tm, w), lambda i: (i, 0)), pl.BlockSpec((tm, w), lambda i: (i, 0))]
    out_shape = [jax.ShapeDtypeStruct((n, w), BF16), jax.ShapeDtypeStruct((n, w), BF16)]
    if latent:
        in_specs += [pl.BlockSpec((tm, 2 * DA_QK), lambda i: (i % per, 0)),
                     pl.BlockSpec((tm, 2 * DA_QK), lambda i: (i % per, 0))]
        args += [cos, sin]
    else:
        in_specs.append(pl.BlockSpec((tm, w), lambda i: (i, C_DV // w)))
        args.append(proj)
        out_specs += [pl.BlockSpec((tm, w), lambda i: (i, 0)), pl.BlockSpec((tm, w), lambda i: (i, 0))]
        out_shape += [jax.ShapeDtypeStruct((n, w), F32), jax.ShapeDtypeStruct((n, w), F32)]
    return pl.pallas_call(
        functools.partial(_prep_kernel, latent),
        grid=(n // tm,),
        in_specs=in_specs, out_specs=out_specs, out_shape=out_shape,
        compiler_params=_cparams(("arbitrary",)),
        name="qk_prep_lat" if latent else "qk_prep_ctx",
    )(*args)


ATT_KC = 256
ATT_TQ = 256
LOG2E = math.log2(math.e)
ATT_SAFE_LOGIT = 60.0


def _attn_kernel(out_scale, has_cache, *refs):
    if has_cache:
        par_ref, q_ref, kc_ref, vc_ref, kl_ref, vl_ref, sw_ref, o_ref, e_scr, o_scr = refs
        srcs = [(kc_ref, vc_ref), (kl_ref, vl_ref)]
    else:
        par_ref, q_ref, kl_ref, vl_ref, sw_ref, o_ref, e_scr, o_scr = refs
        srcs = [(kl_ref, vl_ref)]
    chunks = [(kr, vr, st) for kr, vr in srcs for st in range(0, kr.shape[0], ATT_KC)]
    lam = par_ref[0]
    no_shift = par_ref[1] > 0.5
    tqs = ATT_TQ
    nsub = q_ref.shape[0] // tqs
    nt = (((1,), (1,)), ((), ()))
    half = ATT_KC // 2

    def stacked_q(sb):
        q = q_ref[sb * tqs:(sb + 1) * tqs, :]
        lane = lax.broadcasted_iota(jnp.int32, q.shape, 1)
        zero = jnp.zeros_like(q)
        return jnp.concatenate([jnp.where(lane < DA_QK, q, zero), jnp.where(lane >= DA_QK, q, zero)], axis=0)

    def logits(qq, c):
        kr, vr, st = chunks[c]
        return lax.dot_general(qq, kr[st:st + ATT_KC, :], nt, preferred_element_type=F32)

    def fold(total, e):
        part = e[:, :half] + e[:, half:]
        return part if total is None else total + part

    def row_stats(lsum):
        l = jnp.sum(lsum, axis=-1, keepdims=True)
        l1 = l[0:tqs]
        return l1, lam * l1 / l[tqs:2 * tqs]

    def pv(acc, buf, c, rho):
        kr, vr, st = chunks[c]
        w = (e_scr[buf, c, 0:tqs, :] - rho * e_scr[buf, c, tqs:2 * tqs, :]).astype(BF16)
        t = jnp.dot(w, vr[st:st + ATT_KC, :], preferred_element_type=F32)
        return t if acc is None else acc + t

    nck = len(chunks)

    @pl.when(no_shift)
    def _():
        stats = None
        for sb in range(nsub + 1):
            qq = stacked_q(sb) if sb < nsub else None
            lsum, acc = None, None
            for c in range(nck):
                if sb < nsub:
                    e = jnp.exp2(logits(qq, c))
                    e_scr[sb % 2, c] = e
                    lsum = fold(lsum, e)
                if sb > 0:
                    acc = pv(acc, (sb - 1) % 2, c, stats[1])
            if sb > 0:
                o_scr[(sb - 1) * tqs:sb * tqs, :] = acc / stats[0]
            if sb < nsub:
                stats = row_stats(lsum)

    @pl.when(jnp.logical_not(no_shift))
    def _():
        for sb in range(nsub):
            qq = stacked_q(sb)
            m = None
            for c in range(nck):
                s = logits(qq, c)
                e_scr[0, c] = s
                mc = jnp.max(s, axis=-1, keepdims=True)
                m = mc if m is None else jnp.maximum(m, mc)
            lsum = None
            for c in range(nck):
                e = jnp.exp2(e_scr[0, c] - m)
                e_scr[0, c] = e
                lsum = fold(lsum, e)
            l1, rho = row_stats(lsum)
            acc = None
            for c in range(nck):
                acc = pv(acc, 0, c, rho)
            o_scr[sb * tqs:(sb + 1) * tqs, :] = acc / l1

    o = o_scr[...]
    y = o * lax.rsqrt(jnp.mean(o * o, axis=-1, keepdims=True) + EPS) * sw_ref[...]
    o_ref[...] = (y * out_scale).astype(BF16)


def _attn_call(par, lam_init, q2d, k2d, proj, v_row_off, n_b, t_q, t_kl, tq, subln_w, cache=None):
    hw = 2 * DA_QK
    nq = t_q // tq
    vcol = C_DV // DA_V
    in_specs = [pl.BlockSpec(memory_space=pltpu.SMEM),
                pl.BlockSpec((tq, hw), lambda b, h, qi: (b * nq + qi, h))]
    args = [par, q2d]
    n_chunks = t_kl // ATT_KC
    if cache is not None:
        kc, vc = cache
        p = kc.shape[1]
        n_chunks += p // ATT_KC
        in_specs += [pl.BlockSpec((None, p, hw), lambda b, h, qi: (b, 0, h)),
                     pl.BlockSpec((None, p, DA_V), lambda b, h, qi: (b, 0, h))]
        args += [kc, vc]
    in_specs += [pl.BlockSpec((t_kl, hw), lambda b, h, qi: (b, h)),
                 pl.BlockSpec((t_kl, DA_V), lambda b, h, qi: (v_row_off + b, vcol + h)),
                 pl.BlockSpec((1, DA_V), lambda b, h, qi: (0, 0))]
    args += [k2d, proj, subln_w.reshape(1, DA_V)]
    return pl.pallas_call(
        functools.partial(_attn_kernel, 1.0 - lam_init, cache is not None),
        grid=(n_b, DA_HEADS, nq),
        in_specs=in_specs,
        out_specs=pl.BlockSpec((tq, DA_V), lambda b, h, qi: (b * nq + qi, h)),
        out_shape=jax.ShapeDtypeStruct((n_b * t_q, DA_HEADS * DA_V), BF16),
        scratch_shapes=[pltpu.VMEM((2 if tq > ATT_TQ else 1, n_chunks, 2 * ATT_TQ, ATT_KC), F32),
                        pltpu.VMEM((tq, DA_V), F32)],
        compiler_params=_cparams(("arbitrary", "arbitrary", "arbitrary")),
        name="diff_attn_lat" if cache is not None else "diff_attn_ctx",
    )(*args)


def _ret_state_update(kt, v, kd, cd, s_old):
    parts = []
    for h in range(RET_HEADS):
        rows = slice(h * RET_QK, (h + 1) * RET_QK)
        kh = (kt[rows, :].astype(F32) * kd[rows, :]).astype(BF16)
        parts.append(jnp.dot(kh, v[:, h * RET_V:(h + 1) * RET_V], preferred_element_type=F32))
    return cd * s_old + jnp.concatenate(parts, axis=0)


def _ret_bwd_kernel(geom, kt_ref, v_ref, kd_ref, cd_ref, s0_ref, sstart_ref, send_ref, s_scr):
    i = geom.n_blocks - 1 - pl.program_id(0)

    @pl.when(geom.seq_end(i))
    def _():
        s_scr[...] = s0_ref[...]

    s_old = s_scr[...]
    sstart_ref[...] = s_old
    kt = kt_ref[...] * jnp.asarray(RET_QK ** -0.5, BF16)
    s_new = _ret_state_update(kt, v_ref[...], kd_ref[...], cd_ref[...], s_old)
    s_scr[...] = s_new
    send_ref[...] = s_new


def _ret_bwd_call(geom, proj, rkt, kd_b, cd_b, s0):
    nb = geom.n_blocks
    hs = RET_HEADS * RET_QK

    def blk(g):
        return nb - 1 - g

    return pl.pallas_call(
        functools.partial(_ret_bwd_kernel, geom),
        grid=(nb,),
        in_specs=[pl.BlockSpec((hs, BLK), lambda g: (0, blk(g))),
                  pl.BlockSpec((BLK, RET_HEADS * RET_V), lambda g: (blk(g), C_RV // (RET_HEADS * RET_V))),
                  pl.BlockSpec((hs, BLK), lambda g: (0, 0)),
                  pl.BlockSpec((hs, RET_V), lambda g: (0, 0)),
                  pl.BlockSpec((None, None, hs, RET_V), lambda g: (geom.seq_id(blk(g)), 1, 0, 0))],
        out_specs=[pl.BlockSpec((None, hs, RET_V), lambda g: (blk(g), 0, 0)),
                   pl.BlockSpec((None, hs, RET_V), lambda g: (blk(g), 0, 0))],
        out_shape=[jax.ShapeDtypeStruct((nb, hs, RET_V), F32),
                   jax.ShapeDtypeStruct((nb, hs, RET_V), F32)],
        scratch_shapes=[pltpu.VMEM((hs, RET_V), F32)],
        compiler_params=_cparams(("arbitrary",)),
        name="ret_bwd_state",
    )(rkt, proj, kd_b, cd_b, s0)


def _ret_main_kernel(geom, q_ref, kt_ref, v_ref, g_ref, dsum_ref, qdf_ref, qdb_ref, kd_ref, cd_ref,
                     s0_ref, sb_ref, o_ref, send_ref, s_scr):
    i = pl.program_id(0)

    @pl.when(geom.seq_start(i))
    def _():
        s_scr[...] = s0_ref[...]

    s_f = s_scr[...]
    s_fb = s_f.astype(BF16)
    s_bb = sb_ref[...].astype(BF16)
    q = q_ref[...].astype(F32)
    kt = kt_ref[...] * jnp.asarray(RET_QK ** -0.5, BF16)
    v = v_ref[...]
    lane = lax.broadcasted_iota(jnp.int32, q.shape, 1)
    for h in range(RET_HEADS):
        in_head = (lane >= h * RET_QK) & (lane < (h + 1) * RET_QK)
        qh = jnp.where(in_head, q, 0.0)
        vh = v[:, h * RET_V:(h + 1) * RET_V]
        sc = jnp.dot(qh.astype(BF16), kt, preferred_element_type=F32) * dsum_ref[h]
        o = jnp.dot(sc.astype(BF16), vh, preferred_element_type=F32)
        o += jnp.dot((qh * qdf_ref[...]).astype(BF16), s_fb, preferred_element_type=F32)
        o += jnp.dot((qh * qdb_ref[...]).astype(BF16), s_bb, preferred_element_type=F32)
        y = o * lax.rsqrt(jnp.mean(o * o, axis=-1, keepdims=True) + EPS)
        gv = g_ref[:, h * RET_V:(h + 1) * RET_V].astype(F32)
        o_ref[:, h * RET_V:(h + 1) * RET_V] = (y * (gv * jax.nn.sigmoid(gv))).astype(BF16)
    s_new = _ret_state_update(kt, v, kd_ref[...], cd_ref[...], s_f)
    s_scr[...] = s_new
    send_ref[...] = s_new


def _ret_main_call(geom, proj, rkt, dsum, qdf, qdb, kd_f, cd_f, s0, sb_start):
    nb = geom.n_blocks
    hs = RET_HEADS * RET_QK
    hv = RET_HEADS * RET_V
    return pl.pallas_call(
        functools.partial(_ret_main_kernel, geom),
        grid=(nb,),
        in_specs=[pl.BlockSpec((BLK, hs), lambda g: (g, C_RQ // hs)),
                  pl.BlockSpec((hs, BLK), lambda g: (0, g)),
                  pl.BlockSpec((BLK, hv), lambda g: (g, C_RV // hv)),
                  pl.BlockSpec((BLK, hv), lambda g: (g, C_RG // hv)),
                  pl.BlockSpec((RET_HEADS, BLK, BLK), lambda g: (0, 0, 0)),
                  pl.BlockSpec((BLK, hs), lambda g: (0, 0)),
                  pl.BlockSpec((BLK, hs), lambda g: (0, 0)),
                  pl.BlockSpec((hs, BLK), lambda g: (0, 0)),
                  pl.BlockSpec((hs, RET_V), lambda g: (0, 0)),
                  pl.BlockSpec((None, None, hs, RET_V), lambda g: (geom.seq_id(g), 0, 0, 0)),
                  pl.BlockSpec((None, hs, RET_V), lambda g: (g, 0, 0))],
        out_specs=[pl.BlockSpec((BLK, hv), lambda g: (g, 0)),
                   pl.BlockSpec((None, hs, RET_V), lambda g: (g, 0, 0))],
        out_shape=[jax.ShapeDtypeStruct((geom.n_tok, hv), BF16),
                   jax.ShapeDtypeStruct((nb, hs, RET_V), F32)],
        scratch_shapes=[pltpu.VMEM((hs, RET_V), F32)],
        compiler_params=_cparams(("arbitrary",)),
        name="ret_main",
    )(proj, rkt, proj, proj, dsum, qdf, qdb, kd_f, cd_f, s0, sb_start)


def _ret_tables(ret_decay_l):
    log_g = jax.nn.log_sigmoid(ret_decay_l.astype(F32))
    pos = jnp.arange(BLK, dtype=F32)
    diff = pos[:, None] - pos[None, :]
    lf = log_g[0][:, None, None]
    lb = log_g[1][:, None, None]
    dsum = (jnp.where(diff >= 0, jnp.exp(jnp.maximum(diff, 0.0)[None] * lf), 0.0)
            + jnp.where(diff <= 0, jnp.exp(jnp.maximum(-diff, 0.0)[None] * lb), 0.0))

    def per_lane(e, lg):
        return jnp.repeat(jnp.exp(e[:, None] * lg[None, :]), RET_QK, axis=1)

    qdf = per_lane(pos + 1.0, log_g[0])
    qdb = per_lane(BLK - pos, log_g[1])
    kd_f = per_lane(BLK - 1.0 - pos, log_g[0]).T
    kd_b = per_lane(pos, log_g[1]).T
    cd_f = jnp.broadcast_to(jnp.repeat(jnp.exp(BLK * log_g[0]), RET_QK)[:, None], (RET_HEADS * RET_QK, RET_V))
    cd_b = jnp.broadcast_to(jnp.repeat(jnp.exp(BLK * log_g[1]), RET_QK)[:, None], (RET_HEADS * RET_QK, RET_V))
    return dsum, qdf, qdb, kd_f, kd_b, cd_f, cd_b


def _merge_kernel(n_ctx_tiles, ba_ref, bbc_ref, bbl_ref, bc_ref, g0_ref, g1_ref, g2_ref, xc_ref, xl_ref,
                  gate_ref, sc_ref, sh_ref, nw_ref, wb_ref, wo_ref, rhi_ref, rlo_ref, x1_ref, h2_ref, h2p_ref,
                  lt_ref):
    branches = (ba_ref[...], _pick_part(n_ctx_tiles, bbc_ref, bbl_ref), bc_ref[...])
    acc = None
    for br, (b, g_ref) in enumerate(zip(branches, (g0_ref, g1_ref, g2_ref))):
        p = jnp.dot(b, wb_ref[br], preferred_element_type=F32)
        t = (0.5 * jnp.tanh(0.5 * g_ref[...].astype(F32)) + 0.5) * p
        acc = t if acc is None else acc + t
    m = jnp.dot(acc.astype(BF16), wo_ref[...], preferred_element_type=F32)
    x1 = _pick_part(n_ctx_tiles, xc_ref, xl_ref) + gate_ref[...] * m
    x1_ref[...] = x1
    ms = jnp.mean(x1 * x1, axis=-1, keepdims=True)
    h2 = x1 * lax.rsqrt(ms + EPS) * nw_ref[...] * (1.0 + sc_ref[...]) + sh_ref[...]
    h2b = h2.astype(BF16)
    h2_ref[...] = h2b
    h2p_ref[...] = _pack_halves(h2b.astype(F32))
    h2lo = (h2 - h2b.astype(F32)).astype(BF16)
    nt = (((1,), (1,)), ((), ()))
    lt_ref[...] = (lax.dot_general(rhi_ref[...], h2b, nt, preferred_element_type=F32)
                   + lax.dot_general(rhi_ref[...], h2lo, nt, preferred_element_type=F32)
                   + lax.dot_general(rlo_ref[...], h2b, nt, preferred_element_type=F32))


def _merge_call(geom, l, ba, bb_ctx, bb_lat, bc, proj, x_ctx, x_lat, mod6, norm2_w, wb_bf, wo_bf, r_hi, r_lo):
    tm = 512
    gcol = C_GL // D_MODEL
    full = lambda shape: pl.BlockSpec(shape, lambda i: tuple(0 for _ in shape))
    tok = lambda w: pl.BlockSpec((tm, w), lambda i: (i, 0))
    return pl.pallas_call(
        functools.partial(_merge_kernel, geom.n_ctx // tm),
        grid=(geom.n_tok // tm,),
        in_specs=[tok(BRANCH_W)] + _split_in_specs(geom, tm, BRANCH_W, 1) + [tok(BRANCH_W),
                  pl.BlockSpec((tm, D_MODEL), lambda i: (i, gcol)),
                  pl.BlockSpec((tm, D_MODEL), lambda i: (i, gcol + 1)),
                  pl.BlockSpec((tm, D_MODEL), lambda i: (i, gcol + 2))]
                 + _split_in_specs(geom, tm, D_MODEL, 1) + [
                  _mod_spec(geom, l, 2, tm, 1), _mod_spec(geom, l, 4, tm, 1), _mod_spec(geom, l, 3, tm, 1),
                  full((1, D_MODEL)),
                  full((N_BRANCH, BRANCH_W, D_MODEL)), full((D_MODEL, D_MODEL)),
                  full((N_EXPERTS, D_MODEL)), full((N_EXPERTS, D_MODEL))],
        out_specs=[tok(D_MODEL), tok(D_MODEL), tok(D_MODEL // 2), pl.BlockSpec((N_EXPERTS, tm), lambda i: (0, i))],
        out_shape=[jax.ShapeDtypeStruct((geom.n_tok, D_MODEL), F32),
                   jax.ShapeDtypeStruct((geom.n_tok, D_MODEL), BF16),
                   jax.ShapeDtypeStruct((geom.n_tok, D_MODEL // 2), jnp.uint32),
                   jax.ShapeDtypeStruct((N_EXPERTS, geom.n_tok), F32)],
        compiler_params=_cparams(("arbitrary",)),
        name="merge_out",
    )(ba, bb_ctx, bb_lat, bc, proj, proj, proj, x_ctx, x_lat, mod6, mod6, mod6,
      norm2_w.reshape(1, D_MODEL), wb_bf, wo_bf, r_hi, r_lo)


def _router_kernel(lt_ref, bias_ref, ltri_ref, utri_ref, g_ref, slot_ref, cnt_ref, cnt_scr):
    per = N_EXPERTS // N_GROUPS
    tm = lt_ref.shape[1]
    scores = jax.nn.sigmoid(lt_ref[...])
    biased = scores + bias_ref[...]
    b3 = biased.reshape(N_GROUPS, per, tm)
    neg = jnp.float32(-jnp.inf)
    m1 = jnp.max(b3, axis=1, keepdims=True)
    is_m1 = b3 == m1
    cnt = jnp.sum(is_m1.astype(F32), axis=1, keepdims=True)
    m2 = jnp.max(jnp.where(is_m1, neg, b3), axis=1, keepdims=True)
    grp = (m1 + jnp.where(cnt >= 2.0, m1, m2)).reshape(N_GROUPS, tm)
    gidx = lax.broadcasted_iota(jnp.int32, (N_GROUPS, tm), 0)
    grank = jnp.zeros((N_GROUPS, tm), F32)
    for g2 in range(N_GROUPS):
        other = grp[g2:g2 + 1, :]
        ahead = (other > grp) | ((other == grp) & (gidx > g2))
        grank += ahead.astype(F32)
    gsel = (grank < float(TOPK_GROUPS)).astype(F32)
    emask = jnp.broadcast_to(gsel.reshape(N_GROUPS, 1, tm), (N_GROUPS, per, tm)).reshape(N_EXPERTS, tm)
    masked = jnp.where(emask > 0.0, biased, neg)
    eidx = lax.broadcasted_iota(jnp.int32, (N_EXPERTS, tm), 0)
    erank = jnp.zeros((N_EXPERTS, tm), F32)
    for e2 in range(N_EXPERTS):
        other = masked[e2:e2 + 1, :]
        ahead = (other > masked) | ((other == masked) & (eidx > e2))
        erank += ahead.astype(F32)
    sel = erank < float(TOP_K)
    w = jnp.where(sel, scores, 0.0)
    gates_t = w / jnp.sum(w, axis=0, keepdims=True) * ROUTED_SCALE

    @pl.when(pl.program_id(0) == 0)
    def _():
        cnt_scr[...] = jnp.zeros_like(cnt_scr)

    selb = sel.astype(BF16)
    slot = jnp.dot(ltri_ref[...], selb, preferred_element_type=F32)
    carry = cnt_scr[:, 0:1]
    rank = jnp.dot(selb, utri_ref[...], preferred_element_type=F32) + carry
    cnt_new = cnt_scr[...] + jnp.sum(sel.astype(F32), axis=1, keepdims=True)
    cnt_scr[...] = cnt_new
    cnt_ref[...] = cnt_new
    eid_f = eidx.astype(F32)
    g_rows, e_rows, r_rows = [], [], []
    for k in range(TOP_K):
        mk = jnp.where(sel & (slot == float(k)), 1.0, 0.0)
        g_rows.append(jnp.sum(mk * gates_t, axis=0, keepdims=True))
        e_rows.append(jnp.sum(mk * eid_f, axis=0, keepdims=True))
        r_rows.append(jnp.sum(mk * rank, axis=0, keepdims=True))
    slot_ref[...] = jnp.concatenate(e_rows + r_rows, axis=0).astype(jnp.int32)
    pad = jnp.zeros((GATE_W - TOP_K, tm), F32)
    g_ref[...] = jnp.concatenate(g_rows + [pad], axis=0).T


ROUTER_TM = 512


def _router_call(geom, logits_t, bias):
    tm = ROUTER_TM
    ltri = jnp.asarray(np.tril(np.ones((N_EXPERTS, N_EXPERTS), np.float32), -1), BF16)
    utri = jnp.asarray(np.triu(np.ones((tm, tm), np.float32), 1), BF16)
    return pl.pallas_call(
        _router_kernel,
        grid=(geom.n_tok // tm,),
        in_specs=[pl.BlockSpec((N_EXPERTS, tm), lambda i: (0, i)),
                  pl.BlockSpec((N_EXPERTS, 1), lambda i: (0, 0)),
                  pl.BlockSpec((N_EXPERTS, N_EXPERTS), lambda i: (0, 0)),
                  pl.BlockSpec((tm, tm), lambda i: (0, 0))],
        out_specs=[pl.BlockSpec((tm, GATE_W), lambda i: (i, 0)),
                   pl.BlockSpec((2 * TOP_K, tm), lambda i: (0, i)),
                   pl.BlockSpec((N_EXPERTS, GATE_W), lambda i: (0, 0))],
        out_shape=[jax.ShapeDtypeStruct((geom.n_tok, GATE_W), F32),
                   jax.ShapeDtypeStruct((2 * TOP_K, geom.n_tok), jnp.int32),
                   jax.ShapeDtypeStruct((N_EXPERTS, GATE_W), F32)],
        scratch_shapes=[pltpu.VMEM((N_EXPERTS, GATE_W), F32)],
        compiler_params=_cparams(("arbitrary",)),
        name="router",
    )(logits_t, bias.reshape(N_EXPERTS, 1), ltri, utri)


MOE_TR = 512
SC_CORES = 2
SC_SUBCORES = 16
SC_CHUNK = 64


def _sc_worker_base(rows_per_worker):
    wid = lax.axis_index("s") * SC_CORES + lax.axis_index("c")
    return wid * rows_per_worker


def _sc_scatter_rows(table, pos_flat, n_slots, n_rows_out):
    n, d = table.shape
    nw = SC_CORES * SC_SUBCORES
    assert n % (nw * SC_CHUNK) == 0
    per_w = n // nw
    mesh = plsc.VectorSubcoreMesh(core_axis_name="c", subcore_axis_name="s")

    @functools.partial(
        pl.kernel, mesh=mesh,
        out_type=jax.ShapeDtypeStruct((n_rows_out, d), table.dtype),
        scratch_types=[[pltpu.VMEM((SC_CHUNK,), jnp.int32) for _ in range(n_slots)],
                       pltpu.VMEM((SC_CHUNK, d), table.dtype),
                       pltpu.SemaphoreType.DMA],
    )
    def scatter(table_hbm, pos_hbm, out_hbm, idx_v, rows_v, sem):
        base = _sc_worker_base(per_w)

        @pl.loop(0, per_w // SC_CHUNK)
        def _(ci):
            off = pl.multiple_of(base + ci * SC_CHUNK, 8)
            for k in range(n_slots):
                pltpu.sync_copy(pos_hbm.at[pl.ds(pl.multiple_of(k * n + off, 8), SC_CHUNK)], idx_v[k])
            pltpu.sync_copy(table_hbm.at[pl.ds(off, SC_CHUNK)], rows_v)
            copies = [pltpu.make_async_copy(rows_v, out_hbm.at[idx_v[k]], sem) for k in range(n_slots)]
            for cp in copies:
                cp.start()
            for cp in copies:
                cp.wait()

    return scatter(table, pos_flat)


def _sc_gather_rows(table, idx):
    b = idx.shape[0]
    d = table.shape[1]
    nw = SC_CORES * SC_SUBCORES
    nbuf = 2
    assert b % (nw * SC_CHUNK * nbuf) == 0
    per_w = b // nw
    n_chunks = per_w // SC_CHUNK
    mesh = plsc.VectorSubcoreMesh(core_axis_name="c", subcore_axis_name="s")

    @functools.partial(
        pl.kernel, mesh=mesh,
        out_type=jax.ShapeDtypeStruct((b, d), table.dtype),
        scratch_types=[pltpu.VMEM((per_w,), jnp.int32),
                       [pltpu.VMEM((SC_CHUNK, d), table.dtype) for _ in range(nbuf)],
                       [pltpu.SemaphoreType.DMA for _ in range(nbuf)],
                       [pltpu.SemaphoreType.DMA for _ in range(nbuf)]],
    )
    def gather(table_hbm, idx_hbm, out_hbm, idx_v, rows, gsem, wsem):
        base = _sc_worker_base(per_w)
        pltpu.sync_copy(idx_hbm.at[pl.ds(pl.multiple_of(base, 8), per_w)], idx_v)

        def fetch(ci, slot):
            src = table_hbm.at[idx_v.at[pl.ds(pl.multiple_of(ci * SC_CHUNK, 8), SC_CHUNK)]]
            return pltpu.make_async_copy(src, rows[slot], gsem[slot])

        def put(ci, slot):
            dst = out_hbm.at[pl.ds(pl.multiple_of(base + ci * SC_CHUNK, 8), SC_CHUNK)]
            return pltpu.make_async_copy(rows[slot], dst, wsem[slot])

        for slot in range(nbuf):
            fetch(slot, slot).start()

        @pl.loop(0, n_chunks, step=nbuf)
        def _(c0):
            for slot in range(nbuf):
                ci = c0 + slot
                fetch(ci, slot).wait()
                put(ci, slot).start()
                put(ci, slot).wait()

                @pl.when(ci + nbuf < n_chunks)
                def _():
                    fetch(ci + nbuf, slot).start()

    return gather(table, idx)


def _route_positions(n_tok, slots, counts):
    cnt = counts[:, 0].astype(jnp.int32)
    cnt_pad = ((cnt + MOE_TR - 1) // MOE_TR) * MOE_TR
    off_end = jnp.cumsum(cnt_pad)
    off = off_end - cnt_pad
    eid, rank = slots[:TOP_K], slots[TOP_K:]
    eids = jnp.arange(N_EXPERTS, dtype=jnp.int32)
    pos = jnp.sum(jnp.where(eid[..., None] == eids, off, 0), axis=-1) + rank
    n_tiles = (TOP_K * n_tok) // MOE_TR + N_EXPERTS
    tile_start = jnp.arange(n_tiles, dtype=jnp.int32) * MOE_TR
    tile_expert = jnp.sum((tile_start[:, None] >= off_end[None, :]).astype(jnp.int32), axis=1)
    tile_expert = jnp.minimum(tile_expert, N_EXPERTS - 1)
    n_used = (off_end[-1] // MOE_TR).reshape(1)
    return pos.reshape(-1), tile_expert, n_used, n_tiles


def _expert_ffn(x_lo, x_hi, gu, dn):
    half = D_MODEL // 2
    a = (jnp.dot(x_lo, gu[0:half, :], preferred_element_type=F32)
         + jnp.dot(x_hi, gu[half:, :], preferred_element_type=F32))
    hg = a[:, :D_EXPERT]
    act = (hg * jax.nn.sigmoid(hg)) * a[:, D_EXPERT:]
    return jnp.dot(act.astype(BF16), dn, preferred_element_type=F32)


def _experts_kernel(te_ref, nu_ref, x_ref, gu_ref, dn_ref, y_ref):
    i = pl.program_id(0)

    @pl.when(i < nu_ref[0])
    def _():
        lo, hi = _unpack_halves(x_ref[...])
        y = _expert_ffn(lo.astype(BF16), hi.astype(BF16), gu_ref[...].astype(BF16), dn_ref[...].astype(BF16))
        y_ref[...] = _pack_halves(y.astype(BF16).astype(F32))

    @pl.when(i >= nu_ref[0])
    def _():
        y_ref[...] = jnp.zeros_like(y_ref)


def _experts_call(l, xs, tile_expert, n_used, n_tiles, w_gu, w_dn):
    half = D_MODEL // 2
    grid_spec = pltpu.PrefetchScalarGridSpec(
        num_scalar_prefetch=2,
        grid=(n_tiles,),
        in_specs=[pl.BlockSpec((MOE_TR, half), lambda i, te, nu: (i, 0)),
                  pl.BlockSpec((None, None, D_MODEL, 2 * D_EXPERT), lambda i, te, nu: (l, te[i], 0, 0)),
                  pl.BlockSpec((None, None, D_EXPERT, D_MODEL), lambda i, te, nu: (l, te[i], 0, 0))],
        out_specs=pl.BlockSpec((MOE_TR, half), lambda i, te, nu: (i, 0)),
    )
    return pl.pallas_call(
        _experts_kernel,
        grid_spec=grid_spec,
        out_shape=jax.ShapeDtypeStruct((n_tiles * MOE_TR, half), jnp.uint32),
        compiler_params=_cparams(("arbitrary",)),
        name="moe_experts",
    )(tile_expert, n_used, xs, w_gu, w_dn)


def _moe_out_kernel(n_ctx_tiles, yt_ref, g_ref, h_ref, sgu_ref, sdn_ref, x1_ref, gate_ref, oc_ref, ol_ref):
    i = pl.program_id(0)
    gts = g_ref[...]
    lane = lax.broadcasted_iota(jnp.int32, gts.shape, 1)
    acc_lo, acc_hi = None, None
    for k in range(TOP_K):
        ge = jnp.sum(jnp.where(lane == k, gts, 0.0), axis=1, keepdims=True)
        lo, hi = _unpack_halves(yt_ref[k])
        acc_lo = ge * lo if acc_lo is None else acc_lo + ge * lo
        acc_hi = ge * hi if acc_hi is None else acc_hi + ge * hi
    routed = jnp.concatenate([acc_lo, acc_hi], axis=1)
    h = h_ref[...]
    half = D_MODEL // 2
    shared = _expert_ffn(h[:, :half], h[:, half:], sgu_ref[...], sdn_ref[...])
    y = x1_ref[...] + gate_ref[...] * (routed + shared)

    @pl.when(i < n_ctx_tiles)
    def _():
        oc_ref[...] = y

    @pl.when(i >= n_ctx_tiles)
    def _():
        ol_ref[...] = y


def _moe_out_call(geom, l, yt, gates, h2, sgu_bf, sdn_bf, x1, mod6):
    tm = 512
    nct = geom.n_ctx // tm
    half = D_MODEL // 2
    return pl.pallas_call(
        functools.partial(_moe_out_kernel, nct),
        grid=(geom.n_tok // tm,),
        in_specs=[pl.BlockSpec((TOP_K, tm, half), lambda i: (0, i, 0)),
                  pl.BlockSpec((tm, GATE_W), lambda i: (i, 0)),
                  pl.BlockSpec((tm, D_MODEL), lambda i: (i, 0)),
                  pl.BlockSpec((None, D_MODEL, 2 * D_EXPERT), lambda i: (l, 0, 0)),
                  pl.BlockSpec((None, D_EXPERT, D_MODEL), lambda i: (l, 0, 0)),
                  pl.BlockSpec((tm, D_MODEL), lambda i: (i, 0)),
                  _mod_spec(geom, l, 5, tm, 1)],
        out_specs=[pl.BlockSpec((tm, D_MODEL), lambda i: (jnp.minimum(i, nct - 1), 0)),
                   pl.BlockSpec((tm, D_MODEL), lambda i: (jnp.maximum(i - nct, 0), 0))],
        out_shape=[jax.ShapeDtypeStruct((geom.n_ctx, D_MODEL), F32),
                   jax.ShapeDtypeStruct((geom.n_lat, D_MODEL), F32)],
        compiler_params=_cparams(("arbitrary",)),
        name="moe_out",
    )(yt, gates, h2, sgu_bf, sdn_bf, x1, mod6)


def _moe(geom, l, h2, h2p, gates, slots, counts, w_gu, w_dn, sgu_bf, sdn_bf, x1, mod6):
    pos, tile_expert, n_used, n_tiles = _route_positions(geom.n_tok, slots, counts)
    xs = _sc_scatter_rows(h2p, pos, TOP_K, n_tiles * MOE_TR)
    ys = _experts_call(l, xs, tile_expert, n_used, n_tiles, w_gu, w_dn)
    yt = _sc_gather_rows(ys, pos).reshape(TOP_K, geom.n_tok, D_MODEL // 2)
    return _moe_out_call(geom, l, yt, gates, h2, sgu_bf, sdn_bf, x1, mod6)


def _rope_tables(dec_seq):
    rows = dec_seq // GRID_W
    row = jnp.repeat(jnp.arange(rows, dtype=F32), GRID_W)
    col = jnp.tile(jnp.arange(GRID_W, dtype=F32), rows)
    inv = ROPE_BASE ** (-jnp.arange(ROPE_PAIRS, dtype=F32) / ROPE_PAIRS)
    ar = row[:, None] * inv[None, :]
    ac = col[:, None] * inv[None, :]
    cos64 = jnp.concatenate([jnp.cos(ar), jnp.cos(ar), jnp.cos(ac), jnp.cos(ac)], axis=1)
    sin64 = jnp.concatenate([-jnp.sin(ar), jnp.sin(ar), -jnp.sin(ac), jnp.sin(ac)], axis=1)
    return jnp.tile(cos64, (1, 2)), jnp.tile(sin64, (1, 2))


def _block_diag_gate(wg_dir):
    eye = jnp.eye(LRU_BLOCKS, dtype=F32)
    dense = jnp.einsum('gnij,nm->gnimj', wg_dir.astype(F32), eye).reshape(2, D_RNN, D_RNN)
    return jnp.concatenate([dense[0], dense[1]], axis=1)


def kernel(x_prompt, x_sample, cache_k, cache_v, state_lru, state_ret, c, c_ctx, ada_w, ada_b, norm1_w, norm2_w, w_in, conv_w, conv_b, lru_gate_w, lru_gate_b, lru_lambda, q_norm_w, k_norm_w, diff_lambda, subln_w, ret_decay, w_branch, w_out, router_w, router_bias, w_exp_gu, w_exp_down, w_sh_gu, w_sh_down):
    batch, seq, _ = x_prompt.shape
    dec_batch, dec_seq, _ = x_sample.shape
    assert 1 + dec_batch <= MOD_ROWS
    geom = _Geom(batch, seq, dec_batch, dec_seq)
    hs = RET_HEADS * RET_QK
    aw = DA_HEADS * 2 * DA_QK

    x_ctx = x_prompt.reshape(geom.n_ctx, D_MODEL)
    x_lat = x_sample.reshape(geom.n_lat, D_MODEL)
    cvec = jnp.zeros((MOD_ROWS, D_MODEL), F32).at[0].set(c_ctx).at[1:1 + dec_batch].set(c)
    mod6 = _ada_call(cvec, ada_w, ada_b).reshape(DEPTH, MOD_ROWS, 6, 1, D_MODEL)

    ones_bd = jnp.kron(jnp.eye(aw // DA_QK, dtype=F32), jnp.ones((DA_QK, DA_QK), F32)).astype(BF16)
    cos_t, sin_t = _rope_tables(dec_seq)

    w_in_bf = w_in.astype(BF16)
    sgu_bf, sdn_bf = w_sh_gu.astype(BF16), w_sh_down.astype(BF16)

    ks, vs, lrus, rets = [], [], [], []
    for l in range(DEPTH):
        lam_init = 0.8 - 0.6 * math.exp(-0.3 * l)
        w_rkt_bf = w_in[l][:, C_RK:C_RK + hs].T.astype(BF16)
        proj, rkt = _inproj_call(geom, l, x_ctx, x_lat, mod6, norm1_w[l], w_in_bf, w_rkt_bf)

        sp = jax.nn.softplus(-lru_lambda[l].astype(F32))
        h0 = jnp.concatenate([jnp.zeros((batch, 2, D_RNN), F32), state_lru[:, l].astype(F32)], axis=0)
        h0 = h0.reshape(geom.n_seq, 2, 1, D_RNN)
        cb = conv_b[l].reshape(1, D_RNN)
        lru_args = []
        for d in range(2):
            lru_args.append((_block_diag_gate(lru_gate_w[l, d]).astype(BF16),
                             lru_gate_b[l, d].reshape(1, 2 * D_RNN), sp[d].reshape(1, D_RNN)))
        hf, hf_last = _lru_call(geom, False, proj, conv_w[l], cb, *lru_args[0], h0)
        branch_a, hb_last = _lru_call(geom, True, proj, conv_w[l], cb, *lru_args[1], h0, hf)

        qw = jnp.tile(q_norm_w[l], aw // DA_QK).reshape(1, aw)
        kw = jnp.tile(k_norm_w[l], aw // DA_QK).reshape(1, aw)
        q_c, k_c, k_c32, v_c32 = _prep_call(geom, False, proj, qw, kw, ones_bd)
        q_l, k_l = _prep_call(geom, True, proj, qw, kw, ones_bd, cos_t, sin_t)
        lam_p = diff_lambda[l].astype(F32)
        lam = jnp.exp(jnp.sum(lam_p[0] * lam_p[1])) - jnp.exp(jnp.sum(lam_p[2] * lam_p[3])) + lam_init
        q_bound = DA_QK * jnp.max(jnp.square(q_norm_w[l].astype(F32))) * (DA_QK ** -0.5 * LOG2E) ** 2
        k_bound = DA_QK * jnp.max(jnp.square(k_norm_w[l].astype(F32)))
        kc32 = cache_k[:, l].astype(F32)
        kc_bound = jnp.maximum(k_bound, jnp.max(jnp.sum(jnp.square(kc32), axis=-1)))

        def attn_par(kb):
            ok = (q_bound * kb * 1.05 < ATT_SAFE_LOGIT ** 2).astype(F32)
            return jnp.stack([lam, ok])

        assert geom.n_ctx % dec_seq == 0
        cache = (kc32.reshape(dec_batch, -1, aw).astype(BF16),
                 cache_v[:, l].reshape(dec_batch, -1, DA_HEADS * DA_V).astype(BF16))
        att_c = _attn_call(attn_par(k_bound), lam_init, q_c, k_c, proj, 0, batch, seq, seq, 256, subln_w[l])
        att_l = _attn_call(attn_par(kc_bound), lam_init, q_l, k_l, proj, geom.n_ctx // dec_seq, dec_batch,
                           dec_seq, dec_seq, 4 * ATT_TQ, subln_w[l], cache)

        dsum, qdf, qdb, kd_f, kd_b, cd_f, cd_b = _ret_tables(ret_decay[l])
        s0 = jnp.concatenate([jnp.zeros((batch, 2, hs, RET_V), F32),
                              state_ret[:, l].astype(F32).reshape(dec_batch, 2, hs, RET_V)], axis=0)
        sb_start, sb_end = _ret_bwd_call(geom, proj, rkt, kd_b, cd_b, s0)
        branch_c, sf_end = _ret_main_call(geom, proj, rkt, dsum, qdf, qdb, kd_f, cd_f, s0, sb_start)

        r_t = router_w[l].T.astype(F32)
        r_hi = r_t.astype(BF16)
        r_lo = (r_t - r_hi.astype(F32)).astype(BF16)
        x1, h2, h2p, logits_t = _merge_call(geom, l, branch_a, att_c, att_l, branch_c, proj, x_ctx, x_lat, mod6,
                                            norm2_w[l], w_branch[l].astype(BF16), w_out[l].astype(BF16), r_hi, r_lo)
        gates, slots, counts = _router_call(geom, logits_t, router_bias[l].astype(F32))
        x_ctx, x_lat = _moe(geom, l, h2, h2p, gates, slots, counts, w_exp_gu, w_exp_down, sgu_bf, sdn_bf, x1, mod6)

        ks.append(k_c32.reshape(batch, seq, DA_HEADS, 2, DA_QK))
        vs.append(v_c32.reshape(batch, seq, DA_HEADS, DA_V))
        lrus.append(jnp.stack([hf_last[:batch, 0], hb_last[:batch, 0]], axis=1))
        rets.append(jnp.stack([sf_end[:batch].reshape(batch, RET_HEADS, RET_QK, RET_V),
                               sb_end[:batch].reshape(batch, RET_HEADS, RET_QK, RET_V)], axis=1))

    y_prompt = x_ctx.reshape(batch, seq, D_MODEL)
    y_sample = x_lat.reshape(dec_batch, dec_seq, D_MODEL)
    return (y_prompt, y_sample, jnp.stack(ks, axis=1), jnp.stack(vs, axis=1),
            jnp.stack(lrus, axis=1), jnp.stack(rets, axis=1))
```

```python
import functools
import math

import numpy as np
import jax
import jax.numpy as jnp
from jax import lax
from jax.experimental import pallas as pl
from jax.experimental.pallas import tpu as pltpu
from jax.experimental.pallas import tpu_sc as plsc

F32 = jnp.float32
BF16 = jnp.bfloat16

D_MODEL = 1024
DEPTH = 2
GRID_W = 64
D_RNN = 512
LRU_BLOCKS = 8
LRU_BLOCK = D_RNN // LRU_BLOCKS
CONV_W = 4
LRU_C = 8.0
DA_HEADS = 4
DA_QK = 64
DA_V = 128
ROPE_PAIRS = DA_QK // 4
ROPE_BASE = 10000.0
RET_HEADS = 4
RET_QK = 64
RET_V = 128
BRANCH_W = 512
N_BRANCH = 3
D_IN = 7168
N_EXPERTS = 64
TOP_K = 8
N_GROUPS = 8
TOPK_GROUPS = 4
D_EXPERT = 256
ROUTED_SCALE = 2.5
EPS = 1e-6

C_XA, C_GA, C_DQ, C_DK, C_DV = 0, 512, 1024, 1536, 2048
C_RQ, C_RK, C_RV, C_RG, C_GL = 2560, 2816, 3072, 3584, 4096

BLK = 256
LRU_SUB = 8
LRU_LANES = D_RNN // 128
GATE_W = 128
MOD_ROWS = 8
VMEM_LIMIT = 56 * 1024 * 1024


def _cparams(sem, vmem_limit=VMEM_LIMIT):
    return pltpu.CompilerParams(dimension_semantics=sem, vmem_limit_bytes=vmem_limit)


class _Geom:
    def __init__(self, batch, seq, dec_batch, dec_seq):
        assert seq == BLK and dec_seq % BLK == 0
        self.batch, self.seq, self.dec_batch, self.dec_seq = batch, seq, dec_batch, dec_seq
        self.n_ctx = batch * seq
        self.n_lat = dec_batch * dec_seq
        self.n_tok = self.n_ctx + self.n_lat
        self.ctx_blocks = self.n_ctx // BLK
        self.lat_blocks = dec_seq // BLK
        self.n_blocks = self.n_tok // BLK
        self.n_seq = batch + dec_batch

    def mod_row(self, i, tile):
        nct = self.n_ctx // tile
        per = self.dec_seq // tile
        return jnp.where(i < nct, 0, 1 + (i - nct) // per)

    def seq_id(self, i):
        return jnp.where(i < self.ctx_blocks, i, self.ctx_blocks + (i - self.ctx_blocks) // self.lat_blocks)

    def seq_start(self, i):
        return jnp.logical_or(i < self.ctx_blocks, (i - self.ctx_blocks) % self.lat_blocks == 0)

    def seq_end(self, i):
        return jnp.logical_or(i < self.ctx_blocks, (i - self.ctx_blocks) % self.lat_blocks == self.lat_blocks - 1)


def _ada_kernel(c_ref, w_ref, b_ref, o_ref):
    cv = c_ref[...]
    s = cv * jax.nn.sigmoid(cv)
    o_ref[...] = jnp.dot(s, w_ref[...], preferred_element_type=F32,
                         precision=lax.Precision.HIGHEST) + b_ref[...]


def _ada_call(cvec, ada_w, ada_b):
    depth = ada_w.shape[0]
    nt = 6
    return pl.pallas_call(
        _ada_kernel,
        grid=(depth, nt),
        in_specs=[pl.BlockSpec((MOD_ROWS, D_MODEL), lambda l, j: (0, 0)),
                  pl.BlockSpec((None, D_MODEL, D_MODEL), lambda l, j: (l, 0, j)),
                  pl.BlockSpec((None, 1, D_MODEL), lambda l, j: (l, 0, j))],
        out_specs=pl.BlockSpec((None, MOD_ROWS, D_MODEL), lambda l, j: (l, 0, j)),
        out_shape=jax.ShapeDtypeStruct((depth, MOD_ROWS, 6 * D_MODEL), F32),
        compiler_params=_cparams(("arbitrary", "arbitrary")),
        name="ada_mod",
    )(cvec, ada_w, ada_b.reshape(depth, 1, 6 * D_MODEL))


def _mod_spec(geom, l, which, tile, ngrid):
    if ngrid == 1:
        return pl.BlockSpec((None, None, None, 1, D_MODEL),
                            lambda i: (l, geom.mod_row(i, tile), which, 0, 0))
    return pl.BlockSpec((None, None, None, 1, D_MODEL),
                        lambda i, j: (l, geom.mod_row(i, tile), which, 0, 0))


def _split_in_specs(geom, tile, width, ngrid):
    nct = geom.n_ctx // tile
    if ngrid == 1:
        return [pl.BlockSpec((tile, width), lambda i: (jnp.minimum(i, nct - 1), 0)),
                pl.BlockSpec((tile, width), lambda i: (jnp.maximum(i - nct, 0), 0))]
    return [pl.BlockSpec((tile, width), lambda i, j: (jnp.minimum(i, nct - 1), 0)),
            pl.BlockSpec((tile, width), lambda i, j: (jnp.maximum(i - nct, 0), 0))]


def _pick_part(n_ctx_tiles, c_ref, l_ref):
    return jnp.where(pl.program_id(0) < n_ctx_tiles, c_ref[...], l_ref[...])


def _pack_halves(y):
    w = y.shape[1] // 2
    bits = pltpu.bitcast(y, jnp.uint32)
    return (bits[:, :w] >> 16) | (bits[:, w:] & jnp.uint32(0xFFFF0000))


def _unpack_halves(p):
    return pltpu.bitcast(p << 16, F32), pltpu.bitcast(p & jnp.uint32(0xFFFF0000), F32)


INPROJ_TM = 512
INPROJ_TN = 1024


def _inproj_kernel(n_ctx_tiles, xc_ref, xl_ref, sc_ref, sh_ref, nw_ref, w_ref, wkt_ref, o_ref, kt_ref):
    x = _pick_part(n_ctx_tiles, xc_ref, xl_ref)
    ms = jnp.mean(x * x, axis=-1, keepdims=True)
    y = x * lax.rsqrt(ms + EPS) * nw_ref[...]
    hb = (y * (1.0 + sc_ref[...]) + sh_ref[...]).astype(BF16)
    kt_ref[...] = lax.dot_general(wkt_ref[...], hb, (((1,), (1,)), ((), ())),
                                  preferred_element_type=F32).astype(BF16)
    for j in range(D_IN // INPROJ_TN):
        cols = slice(j * INPROJ_TN, (j + 1) * INPROJ_TN)
        o_ref[:, cols] = jnp.dot(hb, w_ref[:, cols], preferred_element_type=F32).astype(BF16)


def _inproj_call(geom, l, x_ctx, x_lat, mod6, norm_w, w_in_bf, w_rkt_bf):
    tm = INPROJ_TM
    return pl.pallas_call(
        functools.partial(_inproj_kernel, geom.n_ctx // tm),
        grid=(geom.n_tok // tm,),
        in_specs=_split_in_specs(geom, tm, D_MODEL, 1) + [
                  _mod_spec(geom, l, 1, tm, 1),
                  _mod_spec(geom, l, 0, tm, 1),
                  pl.BlockSpec((1, D_MODEL), lambda i: (0, 0)),
                  pl.BlockSpec((None, D_MODEL, D_IN), lambda i: (l, 0, 0), pipeline_mode=pl.Buffered(1)),
                  pl.BlockSpec((RET_HEADS * RET_QK, D_MODEL), lambda i: (0, 0))],
        out_specs=[pl.BlockSpec((tm, D_IN), lambda i: (i, 0)),
                   pl.BlockSpec((RET_HEADS * RET_QK, tm), lambda i: (0, i))],
        out_shape=[jax.ShapeDtypeStruct((geom.n_tok, D_IN), BF16),
                   jax.ShapeDtypeStruct((RET_HEADS * RET_QK, geom.n_tok), BF16)],
        compiler_params=_cparams(("arbitrary",)),
        name="inproj",
    )(x_ctx, x_lat, mod6, mod6, norm_w.reshape(1, D_MODEL), w_in_bf, w_rkt_bf)


def _gelu_tanh(x):
    return 0.5 * x * (1.0 + jnp.tanh(math.sqrt(2.0 / math.pi) * (x + 0.044715 * (x * x * x))))


def _lru_kernel(geom, reverse, *refs):
    if reverse:
        (xa_ref, xp_ref, xn_ref, cw_ref, cb_ref, wg_ref, bg_ref, sp_ref, h0_ref, perm_ref, permt_ref,
         ga_ref, hf_ref, out_ref, hl_ref, c_scr) = refs
    else:
        (xa_ref, xp_ref, xn_ref, cw_ref, cb_ref, wg_ref, bg_ref, sp_ref, h0_ref, perm_ref,
         out_ref, hl_ref, c_scr) = refs
    g = pl.program_id(0)
    i = geom.n_blocks - 1 - g if reverse else g
    start = geom.seq_start(i)
    end = geom.seq_end(i)

    @pl.when(end if reverse else start)
    def _():
        c_scr[...] = h0_ref[...]

    sub_len = BLK // LRU_SUB
    perm = perm_ref[...]
    x = jnp.dot(perm, xa_ref[...], preferred_element_type=F32)
    pm = jnp.where(start, 0.0, 1.0)
    nm = jnp.where(end, 0.0, 1.0)
    hp = xp_ref.shape[0]
    p1 = xp_ref[hp - 1:hp, :].astype(F32) * pm
    p2 = xp_ref[hp - 2:hp - 1, :].astype(F32) * pm
    n0 = xn_ref[0:1, :].astype(F32) * nm
    row = lax.broadcasted_iota(jnp.int32, x.shape, 0)
    xm1 = jnp.where(row < LRU_SUB, pltpu.roll(x, LRU_SUB + 1, 0), pltpu.roll(x, LRU_SUB, 0))
    xm1 = jnp.where(row == 0, p1, xm1)
    xm2 = jnp.where(row < 2 * LRU_SUB, pltpu.roll(x, 2 * LRU_SUB + 1, 0), pltpu.roll(x, 2 * LRU_SUB, 0))
    xm2 = jnp.where(row == 0, p2, jnp.where(row == LRU_SUB, p1, xm2))
    xp1 = jnp.where(row >= BLK - LRU_SUB, pltpu.roll(x, BLK - LRU_SUB - 1, 0),
                    pltpu.roll(x, BLK - LRU_SUB, 0))
    xp1 = jnp.where(row == BLK - 1, n0, xp1)
    xc = (cw_ref[0:1, :] * xm2 + cw_ref[1:2, :] * xm1 + cw_ref[2:3, :] * x
          + cw_ref[3:4, :] * xp1 + cb_ref[...])

    gt = jnp.dot(xc.astype(BF16), wg_ref[...], preferred_element_type=F32) + bg_ref[...]
    r = jax.nn.sigmoid(gt[:, :D_RNN])
    ig = jax.nn.sigmoid(gt[:, D_RNN:])
    a = jnp.exp(-LRU_C * r * sp_ref[...])
    u = jnp.sqrt(1.0 - a * a) * ig * xc

    h = jnp.zeros((LRU_SUB, D_RNN), F32)
    p = jnp.ones((LRU_SUB, D_RNN), F32)
    h_loc = [None] * sub_len
    p_loc = [None] * sub_len
    for t in (range(sub_len - 1, -1, -1) if reverse else range(sub_len)):
        a_t = a[t * LRU_SUB:(t + 1) * LRU_SUB, :]
        h = a_t * h + u[t * LRU_SUB:(t + 1) * LRU_SUB, :]
        p = a_t * p
        h_loc[t] = h
        p_loc[t] = p
    h_in = [None] * LRU_SUB
    state = c_scr[...]
    for k in (range(LRU_SUB - 1, -1, -1) if reverse else range(LRU_SUB)):
        h_in[k] = state
        state = h[k:k + 1, :] + p[k:k + 1, :] * state
    c_scr[...] = state
    hl_ref[...] = state
    h_in = jnp.concatenate(h_in, axis=0)
    h_full = jnp.concatenate([h_loc[t] + p_loc[t] * h_in for t in range(sub_len)], axis=0)
    if reverse:
        gv = jnp.dot(perm, ga_ref[...], preferred_element_type=F32)
        y = (_gelu_tanh(gv) * (hf_ref[...] + h_full)).astype(BF16)
        out_ref[...] = jnp.dot(permt_ref[...], y, preferred_element_type=F32).astype(BF16)
    else:
        out_ref[...] = h_full


def _lru_call(geom, reverse, proj, conv_w, conv_b, wg, bg, sp, h0, hf=None):
    nb = geom.n_blocks
    halo = 16
    hpb = BLK // halo

    def blk(g):
        return nb - 1 - g if reverse else g

    d = 1 if reverse else 0
    in_specs = [
        pl.BlockSpec((BLK, D_RNN), lambda g: (blk(g), C_XA // D_RNN)),
        pl.BlockSpec((halo, D_RNN), lambda g: (jnp.maximum(blk(g) * hpb - 1, 0), C_XA // D_RNN)),
        pl.BlockSpec((halo, D_RNN), lambda g: (jnp.minimum((blk(g) + 1) * hpb, nb * hpb - 1), C_XA // D_RNN)),
        pl.BlockSpec((CONV_W, D_RNN), lambda g: (0, 0)),
        pl.BlockSpec((1, D_RNN), lambda g: (0, 0)),
        pl.BlockSpec((D_RNN, 2 * D_RNN), lambda g: (0, 0)),
        pl.BlockSpec((1, 2 * D_RNN), lambda g: (0, 0)),
        pl.BlockSpec((1, D_RNN), lambda g: (0, 0)),
        pl.BlockSpec((None, None, 1, D_RNN), lambda g: (geom.seq_id(blk(g)), d, 0, 0)),
    ]
    pos = np.arange(BLK)
    perm_np = np.zeros((BLK, BLK), np.float32)
    perm_np[pos, (pos % LRU_SUB) * (BLK // LRU_SUB) + pos // LRU_SUB] = 1.0
    in_specs.append(pl.BlockSpec((BLK, BLK), lambda g: (0, 0)))
    args = [proj, proj, proj, conv_w, conv_b, wg, bg, sp, h0, jnp.asarray(perm_np, BF16)]
    if reverse:
        in_specs += [pl.BlockSpec((BLK, BLK), lambda g: (0, 0)),
                     pl.BlockSpec((BLK, D_RNN), lambda g: (blk(g), C_GA // D_RNN)),
                     pl.BlockSpec((BLK, D_RNN), lambda g: (blk(g), 0))]
        args += [jnp.asarray(perm_np.T, BF16), proj, hf]
        out_dtype = BF16
    else:
        out_dtype = F32
    scratch = [pltpu.VMEM((1, D_RNN), F32)]
    return pl.pallas_call(
        functools.partial(_lru_kernel, geom, reverse),
        grid=(nb,),
        in_specs=in_specs,
        out_specs=[pl.BlockSpec((BLK, D_RNN), lambda g: (blk(g), 0)),
                   pl.BlockSpec((None, 1, D_RNN), lambda g: (blk(g), 0, 0))],
        out_shape=[jax.ShapeDtypeStruct((geom.n_tok, D_RNN), out_dtype),
                   jax.ShapeDtypeStruct((nb, 1, D_RNN), F32)],
        scratch_shapes=scratch,
        compiler_params=_cparams(("arbitrary",)),
        name="lru_bwd" if reverse else "lru_fwd",
    )(*args)


def _group_rms(x, w, ones):
    xx = x * x
    hi = xx.astype(BF16)
    lo = (xx - hi.astype(F32)).astype(BF16)
    ss = (jnp.dot(hi, ones, preferred_element_type=F32)
          + jnp.dot(lo, ones, preferred_element_type=F32))
    return x * lax.rsqrt(ss * (1.0 / DA_QK) + EPS) * w


def _rope(x, cos, sin):
    lane = lax.broadcasted_iota(jnp.int32, x.shape, 1)
    first = (lane % (2 * ROPE_PAIRS)) < ROPE_PAIRS
    w = x.shape[1]
    partner = jnp.where(first, pltpu.roll(x, w - ROPE_PAIRS, 1), pltpu.roll(x, ROPE_PAIRS, 1))
    return x * cos + partner * sin


def _prep_kernel(rope, *refs):
    if rope:
        dq_ref, dk_ref, qw_ref, kw_ref, ones_ref, cos_ref, sin_ref, q_out, k_out = refs
    else:
        dq_ref, dk_ref, qw_ref, kw_ref, ones_ref, dv_ref, q_out, k_out, kf_out, vf_out = refs
        vf_out[...] = dv_ref[...].astype(F32)
    ones = ones_ref[...]
    q = _group_rms(dq_ref[...].astype(F32), qw_ref[...], ones)
    k = _group_rms(dk_ref[...].astype(F32), kw_ref[...], ones)
    if rope:
        cos = jnp.concatenate([cos_ref[...]] * 4, axis=1)
        sin = jnp.concatenate([sin_ref[...]] * 4, axis=1)
        q = _rope(q, cos, sin)
        k = _rope(k, cos, sin)
    else:
        kf_out[...] = k
    q_out[...] = (q * (DA_QK ** -0.5 * math.log2(math.e))).astype(BF16)
    k_out[...] = k.astype(BF16)


def _prep_call(geom, latent, proj, qw, kw, ones, cos=None, sin=None):
    tm = 512
    w = DA_HEADS * 2 * DA_QK
    if latent:
        n, off = geom.n_lat, geom.n_ctx // tm
        per = geom.dec_seq // tm
    else:
        n, off = geom.n_ctx, 0
    in_specs = [pl.BlockSpec((tm, w), lambda i: (i + off, C_DQ // w)),
                pl.BlockSpec((tm, w), lambda i: (i + off, C_DK // w)),
                pl.BlockSpec((1, w), lambda i: (0, 0)),
                pl.BlockSpec((1, w), lambda i: (0, 0)),
                pl.BlockSpec((w, w), lambda i: (0, 0))]
    args = [proj, proj, qw, kw, ones]
    out_specs = [pl.BlockSpec((tm, w), lambda i: (i, 0)), pl.BlockSpec((tm, w), lambda i: (i, 0))]
    out_shape = [jax.ShapeDtypeStruct((n, w), BF16), jax.ShapeDtypeStruct((n, w), BF16)]
    if latent:
        in_specs += [pl.BlockSpec((tm, 2 * DA_QK), lambda i: (i % per, 0)),
                     pl.BlockSpec((tm, 2 * DA_QK), lambda i: (i % per, 0))]
        args += [cos, sin]
    else:
        in_specs.append(pl.BlockSpec((tm, w), lambda i: (i, C_DV // w)))
        args.append(proj)
        out_specs += [pl.BlockSpec((tm, w), lambda i: (i, 0)), pl.BlockSpec((tm, w), lambda i: (i, 0))]
        out_shape += [jax.ShapeDtypeStruct((n, w), F32), jax.ShapeDtypeStruct((n, w), F32)]
    return pl.pallas_call(
        functools.partial(_prep_kernel, latent),
        grid=(n // tm,),
        in_specs=in_specs, out_specs=out_specs, out_shape=out_shape,
        compiler_params=_cparams(("arbitrary",)),
        name="qk_prep_lat" if latent else "qk_prep_ctx",
    )(*args)


ATT_KC = 256
ATT_TQ = 256
LOG2E = math.log2(math.e)
ATT_SAFE_LOGIT = 60.0


def _attn_kernel(out_scale, has_cache, *refs):
    if has_cache:
        par_ref, q_ref, kc_ref, vc_ref, kl_ref, vl_ref, sw_ref, o_ref, e_scr, o_scr = refs
        srcs = [(kc_ref, vc_ref), (kl_ref, vl_ref)]
    else:
        par_ref, q_ref, kl_ref, vl_ref, sw_ref, o_ref, e_scr, o_scr = refs
        srcs = [(kl_ref, vl_ref)]
    chunks = [(kr, vr, st) for kr, vr in srcs for st in range(0, kr.shape[0], ATT_KC)]
    lam = par_ref[0]
    no_shift = par_ref[1] > 0.5
    tqs = ATT_TQ
    nsub = q_ref.shape[0] // tqs
    nt = (((1,), (1,)), ((), ()))
    half = ATT_KC // 2

    def stacked_q(sb):
        q = q_ref[sb * tqs:(sb + 1) * tqs, :]
        lane = lax.broadcasted_iota(jnp.int32, q.shape, 1)
        zero = jnp.zeros_like(q)
        return jnp.concatenate([jnp.where(lane < DA_QK, q, zero), jnp.where(lane >= DA_QK, q, zero)], axis=0)

    def logits(qq, c):
        kr, vr, st = chunks[c]
        return lax.dot_general(qq, kr[st:st + ATT_KC, :], nt, preferred_element_type=F32)

    def fold(total, e):
        part = e[:, :half] + e[:, half:]
        return part if total is None else total + part

    def row_stats(lsum):
        l = jnp.sum(lsum, axis=-1, keepdims=True)
        l1 = l[0:tqs]
        return l1, lam * l1 / l[tqs:2 * tqs]

    def pv(acc, buf, c, rho):
        kr, vr, st = chunks[c]
        w = (e_scr[buf, c, 0:tqs, :] - rho * e_scr[buf, c, tqs:2 * tqs, :]).astype(BF16)
        t = jnp.dot(w, vr[st:st + ATT_KC, :], preferred_element_type=F32)
        return t if acc is None else acc + t

    nck = len(chunks)

    @pl.when(no_shift)
    def _():
        stats = None
        for sb in range(nsub + 1):
            qq = stacked_q(sb) if sb < nsub else None
            lsum, acc = None, None
            for c in range(nck):
                if sb < nsub:
                    e = jnp.exp2(logits(qq, c))
                    e_scr[sb % 2, c] = e
                    lsum = fold(lsum, e)
                if sb > 0:
                    acc = pv(acc, (sb - 1) % 2, c, stats[1])
            if sb > 0:
                o_scr[(sb - 1) * tqs:sb * tqs, :] = acc / stats[0]
            if sb < nsub:
                stats = row_stats(lsum)

    @pl.when(jnp.logical_not(no_shift))
    def _():
        for sb in range(nsub):
            qq = stacked_q(sb)
            m = None
            for c in range(nck):
                s = logits(qq, c)
                e_scr[0, c] = s
                mc = jnp.max(s, axis=-1, keepdims=True)
                m = mc if m is None else jnp.maximum(m, mc)
            lsum = None
            for c in range(nck):
                e = jnp.exp2(e_scr[0, c] - m)
                e_scr[0, c] = e
                lsum = fold(lsum, e)
            l1, rho = row_stats(lsum)
            acc = None
            for c in range(nck):
                acc = pv(acc, 0, c, rho)
            o_scr[sb * tqs:(sb + 1) * tqs, :] = acc / l1

    o = o_scr[...]
    y = o * lax.rsqrt(jnp.mean(o * o, axis=-1, keepdims=True) + EPS) * sw_ref[...]
    o_ref[...] = (y * out_scale).astype(BF16)


def _attn_call(par, lam_init, q2d, k2d, proj, v_row_off, n_b, t_q, t_kl, tq, subln_w, cache=None):
    hw = 2 * DA_QK
    nq = t_q // tq
    vcol = C_DV // DA_V
    in_specs = [pl.BlockSpec(memory_space=pltpu.SMEM),
                pl.BlockSpec((tq, hw), lambda b, h, qi: (b * nq + qi, h))]
    args = [par, q2d]
    n_chunks = t_kl // ATT_KC
    if cache is not None:
        kc, vc = cache
        p = kc.shape[1]
        n_chunks += p // ATT_KC
        in_specs += [pl.BlockSpec((None, p, hw), lambda b, h, qi: (b, 0, h)),
                     pl.BlockSpec((None, p, DA_V), lambda b, h, qi: (b, 0, h))]
        args += [kc, vc]
    in_specs += [pl.BlockSpec((t_kl, hw), lambda b, h, qi: (b, h)),
                 pl.BlockSpec((t_kl, DA_V), lambda b, h, qi: (v_row_off + b, vcol + h)),
                 pl.BlockSpec((1, DA_V), lambda b, h, qi: (0, 0))]
    args += [k2d, proj, subln_w.reshape(1, DA_V)]
    return pl.pallas_call(
        functools.partial(_attn_kernel, 1.0 - lam_init, cache is not None),
        grid=(n_b, DA_HEADS, nq),
        in_specs=in_specs,
        out_specs=pl.BlockSpec((tq, DA_V), lambda b, h, qi: (b * nq + qi, h)),
        out_shape=jax.ShapeDtypeStruct((n_b * t_q, DA_HEADS * DA_V), BF16),
        scratch_shapes=[pltpu.VMEM((2 if tq > ATT_TQ else 1, n_chunks, 2 * ATT_TQ, ATT_KC), F32),
                        pltpu.VMEM((tq, DA_V), F32)],
        compiler_params=_cparams(("arbitrary", "arbitrary", "arbitrary")),
        name="diff_attn_lat" if cache is not None else "diff_attn_ctx",
    )(*args)


def _ret_state_update(kt, v, kd, cd, s_old):
    parts = []
    for h in range(RET_HEADS):
        rows = slice(h * RET_QK, (h + 1) * RET_QK)
        kh = (kt[rows, :].astype(F32) * kd[rows, :]).astype(BF16)
        parts.append(jnp.dot(kh, v[:, h * RET_V:(h + 1) * RET_V], preferred_element_type=F32))
    return cd * s_old + jnp.concatenate(parts, axis=0)


def _ret_bwd_kernel(geom, kt_ref, v_ref, kd_ref, cd_ref, s0_ref, sstart_ref, send_ref, s_scr):
    i = geom.n_blocks - 1 - pl.program_id(0)

    @pl.when(geom.seq_end(i))
    def _():
        s_scr[...] = s0_ref[...]

    s_old = s_scr[...]
    sstart_ref[...] = s_old
    kt = kt_ref[...] * jnp.asarray(RET_QK ** -0.5, BF16)
    s_new = _ret_state_update(kt, v_ref[...], kd_ref[...], cd_ref[...], s_old)
    s_scr[...] = s_new
    send_ref[...] = s_new


def _ret_bwd_call(geom, proj, rkt, kd_b, cd_b, s0):
    nb = geom.n_blocks
    hs = RET_HEADS * RET_QK

    def blk(g):
        return nb - 1 - g

    return pl.pallas_call(
        functools.partial(_ret_bwd_kernel, geom),
        grid=(nb,),
        in_specs=[pl.BlockSpec((hs, BLK), lambda g: (0, blk(g))),
                  pl.BlockSpec((BLK, RET_HEADS * RET_V), lambda g: (blk(g), C_RV // (RET_HEADS * RET_V))),
                  pl.BlockSpec((hs, BLK), lambda g: (0, 0)),
                  pl.BlockSpec((hs, RET_V), lambda g: (0, 0)),
                  pl.BlockSpec((None, None, hs, RET_V), lambda g: (geom.seq_id(blk(g)), 1, 0, 0))],
        out_specs=[pl.BlockSpec((None, hs, RET_V), lambda g: (blk(g), 0, 0)),
                   pl.BlockSpec((None, hs, RET_V), lambda g: (blk(g), 0, 0))],
        out_shape=[jax.ShapeDtypeStruct((nb, hs, RET_V), F32),
                   jax.ShapeDtypeStruct((nb, hs, RET_V), F32)],
        scratch_shapes=[pltpu.VMEM((hs, RET_V), F32)],
        compiler_params=_cparams(("arbitrary",)),
        name="ret_bwd_state",
    )(rkt, proj, kd_b, cd_b, s0)


def _ret_main_kernel(geom, q_ref, kt_ref, v_ref, g_ref, dsum_ref, qdf_ref, qdb_ref, kd_ref, cd_ref,
                     s0_ref, sb_ref, o_ref, send_ref, s_scr):
    i = pl.program_id(0)

    @pl.when(geom.seq_start(i))
    def _():
        s_scr[...] = s0_ref[...]

    s_f = s_scr[...]
    s_fb = s_f.astype(BF16)
    s_bb = sb_ref[...].astype(BF16)
    q = q_ref[...].astype(F32)
    kt = kt_ref[...] * jnp.asarray(RET_QK ** -0.5, BF16)
    v = v_ref[...]
    lane = lax.broadcasted_iota(jnp.int32, q.shape, 1)
    for h in range(RET_HEADS):
        in_head = (lane >= h * RET_QK) & (lane < (h + 1) * RET_QK)
        qh = jnp.where(in_head, q, 0.0)
        vh = v[:, h * RET_V:(h + 1) * RET_V]
        sc = jnp.dot(qh.astype(BF16), kt, preferred_element_type=F32) * dsum_ref[h]
        o = jnp.dot(sc.astype(BF16), vh, preferred_element_type=F32)
        o += jnp.dot((qh * qdf_ref[...]).astype(BF16), s_fb, preferred_element_type=F32)
        o += jnp.dot((qh * qdb_ref[...]).astype(BF16), s_bb, preferred_element_type=F32)
        y = o * lax.rsqrt(jnp.mean(o * o, axis=-1, keepdims=True) + EPS)
        gv = g_ref[:, h * RET_V:(h + 1) * RET_V].astype(F32)
        o_ref[:, h * RET_V:(h + 1) * RET_V] = (y * (gv * jax.nn.sigmoid(gv))).astype(BF16)
    s_new = _ret_state_update(kt, v, kd_ref[...], cd_ref[...], s_f)
    s_scr[...] = s_new
    send_ref[...] = s_new


def _ret_main_call(geom, proj, rkt, dsum, qdf, qdb, kd_f, cd_f, s0, sb_start):
    nb = geom.n_blocks
    hs = RET_HEADS * RET_QK
    hv = RET_HEADS * RET_V
    return pl.pallas_call(
        functools.partial(_ret_main_kernel, geom),
        grid=(nb,),
        in_specs=[pl.BlockSpec((BLK, hs), lambda g: (g, C_RQ // hs)),
                  pl.BlockSpec((hs, BLK), lambda g: (0, g)),
                  pl.BlockSpec((BLK, hv), lambda g: (g, C_RV // hv)),
                  pl.BlockSpec((BLK, hv), lambda g: (g, C_RG // hv)),
                  pl.BlockSpec((RET_HEADS, BLK, BLK), lambda g: (0, 0, 0)),
                  pl.BlockSpec((BLK, hs), lambda g: (0, 0)),
                  pl.BlockSpec((BLK, hs), lambda g: (0, 0)),
                  pl.BlockSpec((hs, BLK), lambda g: (0, 0)),
                  pl.BlockSpec((hs, RET_V), lambda g: (0, 0)),
                  pl.BlockSpec((None, None, hs, RET_V), lambda g: (geom.seq_id(g), 0, 0, 0)),
                  pl.BlockSpec((None, hs, RET_V), lambda g: (g, 0, 0))],
        out_specs=[pl.BlockSpec((BLK, hv), lambda g: (g, 0)),
                   pl.BlockSpec((None, hs, RET_V), lambda g: (g, 0, 0))],
        out_shape=[jax.ShapeDtypeStruct((geom.n_tok, hv), BF16),
                   jax.ShapeDtypeStruct((nb, hs, RET_V), F32)],
        scratch_shapes=[pltpu.VMEM((hs, RET_V), F32)],
        compiler_params=_cparams(("arbitrary",)),
        name="ret_main",
    )(proj, rkt, proj, proj, dsum, qdf, qdb, kd_f, cd_f, s0, sb_start)


def _ret_tables(ret_decay_l):
    log_g = jax.nn.log_sigmoid(ret_decay_l.astype(F32))
    pos = jnp.arange(BLK, dtype=F32)
    diff = pos[:, None] - pos[None, :]
    lf = log_g[0][:, None, None]
    lb = log_g[1][:, None, None]
    dsum = (jnp.where(diff >= 0, jnp.exp(jnp.maximum(diff, 0.0)[None] * lf), 0.0)
            + jnp.where(diff <= 0, jnp.exp(jnp.maximum(-diff, 0.0)[None] * lb), 0.0))

    def per_lane(e, lg):
        return jnp.repeat(jnp.exp(e[:, None] * lg[None, :]), RET_QK, axis=1)

    qdf = per_lane(pos + 1.0, log_g[0])
    qdb = per_lane(BLK - pos, log_g[1])
    kd_f = per_lane(BLK - 1.0 - pos, log_g[0]).T
    kd_b = per_lane(pos, log_g[1]).T
    cd_f = jnp.broadcast_to(jnp.repeat(jnp.exp(BLK * log_g[0]), RET_QK)[:, None], (RET_HEADS * RET_QK, RET_V))
    cd_b = jnp.broadcast_to(jnp.repeat(jnp.exp(BLK * log_g[1]), RET_QK)[:, None], (RET_HEADS * RET_QK, RET_V))
    return dsum, qdf, qdb, kd_f, kd_b, cd_f, cd_b


def _merge_kernel(n_ctx_tiles, ba_ref, bbc_ref, bbl_ref, bc_ref, g0_ref, g1_ref, g2_ref, xc_ref, xl_ref,
                  gate_ref, sc_ref, sh_ref, nw_ref, wb_ref, wo_ref, rhi_ref, rlo_ref, x1_ref, h2_ref, h2p_ref,
                  lt_ref):
    branches = (ba_ref[...], _pick_part(n_ctx_tiles, bbc_ref, bbl_ref), bc_ref[...])
    acc = None
    for br, (b, g_ref) in enumerate(zip(branches, (g0_ref, g1_ref, g2_ref))):
        p = jnp.dot(b, wb_ref[br], preferred_element_type=F32)
        t = (0.5 * jnp.tanh(0.5 * g_ref[...].astype(F32)) + 0.5) * p
        acc = t if acc is None else acc + t
    m = jnp.dot(acc.astype(BF16), wo_ref[...], preferred_element_type=F32)
    x1 = _pick_part(n_ctx_tiles, xc_ref, xl_ref) + gate_ref[...] * m
    x1_ref[...] = x1
    ms = jnp.mean(x1 * x1, axis=-1, keepdims=True)
    h2 = x1 * lax.rsqrt(ms + EPS) * nw_ref[...] * (1.0 + sc_ref[...]) + sh_ref[...]
    h2b = h2.astype(BF16)
    h2_ref[...] = h2b
    h2p_ref[...] = _pack_halves(h2b.astype(F32))
    h2lo = (h2 - h2b.astype(F32)).astype(BF16)
    nt = (((1,), (1,)), ((), ()))
    lt_ref[...] = (lax.dot_general(rhi_ref[...], h2b, nt, preferred_element_type=F32)
                   + lax.dot_general(rhi_ref[...], h2lo, nt, preferred_element_type=F32)
                   + lax.dot_general(rlo_ref[...], h2b, nt, preferred_element_type=F32))


def _merge_call(geom, l, ba, bb_ctx, bb_lat, bc, proj, x_ctx, x_lat, mod6, norm2_w, wb_bf, wo_bf, r_hi, r_lo):
    tm = 512
    gcol = C_GL // D_MODEL
    full = lambda shape: pl.BlockSpec(shape, lambda i: tuple(0 for _ in shape))
    tok = lambda w: pl.BlockSpec((tm, w), lambda i: (i, 0))
    return pl.pallas_call(
        functools.partial(_merge_kernel, geom.n_ctx // tm),
        grid=(geom.n_tok // tm,),
        in_specs=[tok(BRANCH_W)] + _split_in_specs(geom, tm, BRANCH_W, 1) + [tok(BRANCH_W),
                  pl.BlockSpec((tm, D_MODEL), lambda i: (i, gcol)),
                  pl.BlockSpec((tm, D_MODEL), lambda i: (i, gcol + 1)),
                  pl.BlockSpec((tm, D_MODEL), lambda i: (i, gcol + 2))]
                 + _split_in_specs(geom, tm, D_MODEL, 1) + [
                  _mod_spec(geom, l, 2, tm, 1), _mod_spec(geom, l, 4, tm, 1), _mod_spec(geom, l, 3, tm, 1),
                  full((1, D_MODEL)),
                  full((N_BRANCH, BRANCH_W, D_MODEL)), full((D_MODEL, D_MODEL)),
                  full((N_EXPERTS, D_MODEL)), full((N_EXPERTS, D_MODEL))],
        out_specs=[tok(D_MODEL), tok(D_MODEL), tok(D_MODEL // 2), pl.BlockSpec((N_EXPERTS, tm), lambda i: (0, i))],
        out_shape=[jax.ShapeDtypeStruct((geom.n_tok, D_MODEL), F32),
                   jax.ShapeDtypeStruct((geom.n_tok, D_MODEL), BF16),
                   jax.ShapeDtypeStruct((geom.n_tok, D_MODEL // 2), jnp.uint32),
                   jax.ShapeDtypeStruct((N_EXPERTS, geom.n_tok), F32)],
        compiler_params=_cparams(("arbitrary",)),
        name="merge_out",
    )(ba, bb_ctx, bb_lat, bc, proj, proj, proj, x_ctx, x_lat, mod6, mod6, mod6,
      norm2_w.reshape(1, D_MODEL), wb_bf, wo_bf, r_hi, r_lo)


def _router_kernel(lt_ref, bias_ref, ltri_ref, utri_ref, g_ref, slot_ref, cnt_ref, cnt_scr):
    per = N_EXPERTS // N_GROUPS
    tm = lt_ref.shape[1]
    scores = jax.nn.sigmoid(lt_ref[...])
    biased = scores + bias_ref[...]
    b3 = biased.reshape(N_GROUPS, per, tm)
    neg = jnp.float32(-jnp.inf)
    m1 = jnp.max(b3, axis=1, keepdims=True)
    is_m1 = b3 == m1
    cnt = jnp.sum(is_m1.astype(F32), axis=1, keepdims=True)
    m2 = jnp.max(jnp.where(is_m1, neg, b3), axis=1, keepdims=True)
    grp = (m1 + jnp.where(cnt >= 2.0, m1, m2)).reshape(N_GROUPS, tm)
    gidx = lax.broadcasted_iota(jnp.int32, (N_GROUPS, tm), 0)
    grank = jnp.zeros((N_GROUPS, tm), F32)
    for g2 in range(N_GROUPS):
        other = grp[g2:g2 + 1, :]
        ahead = (other > grp) | ((other == grp) & (gidx > g2))
        grank += ahead.astype(F32)
    gsel = (grank < float(TOPK_GROUPS)).astype(F32)
    emask = jnp.broadcast_to(gsel.reshape(N_GROUPS, 1, tm), (N_GROUPS, per, tm)).reshape(N_EXPERTS, tm)
    masked = jnp.where(emask > 0.0, biased, neg)
    eidx = lax.broadcasted_iota(jnp.int32, (N_EXPERTS, tm), 0)
    erank = jnp.zeros((N_EXPERTS, tm), F32)
    for e2 in range(N_EXPERTS):
        other = masked[e2:e2 + 1, :]
        ahead = (other > masked) | ((other == masked) & (eidx > e2))
        erank += ahead.astype(F32)
    sel = erank < float(TOP_K)
    w = jnp.where(sel, scores, 0.0)
    gates_t = w / jnp.sum(w, axis=0, keepdims=True) * ROUTED_SCALE

    @pl.when(pl.program_id(0) == 0)
    def _():
        cnt_scr[...] = jnp.zeros_like(cnt_scr)

    selb = sel.astype(BF16)
    slot = jnp.dot(ltri_ref[...], selb, preferred_element_type=F32)
    carry = cnt_scr[:, 0:1]
    rank = jnp.dot(selb, utri_ref[...], preferred_element_type=F32) + carry
    cnt_new = cnt_scr[...] + jnp.sum(sel.astype(F32), axis=1, keepdims=True)
    cnt_scr[...] = cnt_new
    cnt_ref[...] = cnt_new
    eid_f = eidx.astype(F32)
    g_rows, e_rows, r_rows = [], [], []
    for k in range(TOP_K):
        mk = jnp.where(sel & (slot == float(k)), 1.0, 0.0)
        g_rows.append(jnp.sum(mk * gates_t, axis=0, keepdims=True))
        e_rows.append(jnp.sum(mk * eid_f, axis=0, keepdims=True))
        r_rows.append(jnp.sum(mk * rank, axis=0, keepdims=True))
    slot_ref[...] = jnp.concatenate(e_rows + r_rows, axis=0).astype(jnp.int32)
    pad = jnp.zeros((GATE_W - TOP_K, tm), F32)
    g_ref[...] = jnp.concatenate(g_rows + [pad], axis=0).T


ROUTER_TM = 512


def _router_call(geom, logits_t, bias):
    tm = ROUTER_TM
    ltri = jnp.asarray(np.tril(np.ones((N_EXPERTS, N_EXPERTS), np.float32), -1), BF16)
    utri = jnp.asarray(np.triu(np.ones((tm, tm), np.float32), 1), BF16)
    return pl.pallas_call(
        _router_kernel,
        grid=(geom.n_tok // tm,),
        in_specs=[pl.BlockSpec((N_EXPERTS, tm), lambda i: (0, i)),
                  pl.BlockSpec((N_EXPERTS, 1), lambda i: (0, 0)),
                  pl.BlockSpec((N_EXPERTS, N_EXPERTS), lambda i: (0, 0)),
                  pl.BlockSpec((tm, tm), lambda i: (0, 0))],
        out_specs=[pl.BlockSpec((tm, GATE_W), lambda i: (i, 0)),
                   pl.BlockSpec((2 * TOP_K, tm), lambda i: (0, i)),
                   pl.BlockSpec((N_EXPERTS, GATE_W), lambda i: (0, 0))],
        out_shape=[jax.ShapeDtypeStruct((geom.n_tok, GATE_W), F32),
                   jax.ShapeDtypeStruct((2 * TOP_K, geom.n_tok), jnp.int32),
                   jax.ShapeDtypeStruct((N_EXPERTS, GATE_W), F32)],
        scratch_shapes=[pltpu.VMEM((N_EXPERTS, GATE_W), F32)],
        compiler_params=_cparams(("arbitrary",)),
        name="router",
    )(logits_t, bias.reshape(N_EXPERTS, 1), ltri, utri)


MOE_TR = 512
SC_CORES = 2
SC_SUBCORES = 16
SC_CHUNK = 64


def _sc_worker_base(rows_per_worker):
    wid = lax.axis_index("s") * SC_CORES + lax.axis_index("c")
    return wid * rows_per_worker


def _sc_scatter_rows(table, pos_flat, n_slots, n_rows_out):
    n, d = table.shape
    nw = SC_CORES * SC_SUBCORES
    assert n % (nw * SC_CHUNK) == 0
    per_w = n // nw
    mesh = plsc.VectorSubcoreMesh(core_axis_name="c", subcore_axis_name="s")

    @functools.partial(
        pl.kernel, mesh=mesh,
        out_type=jax.ShapeDtypeStruct((n_rows_out, d), table.dtype),
        scratch_types=[[pltpu.VMEM((SC_CHUNK,), jnp.int32) for _ in range(n_slots)],
                       pltpu.VMEM((SC_CHUNK, d), table.dtype),
                       pltpu.SemaphoreType.DMA],
    )
    def scatter(table_hbm, pos_hbm, out_hbm, idx_v, rows_v, sem):
        base = _sc_worker_base(per_w)

        @pl.loop(0, per_w // SC_CHUNK)
        def _(ci):
            off = pl.multiple_of(base + ci * SC_CHUNK, 8)
            for k in range(n_slots):
                pltpu.sync_copy(pos_hbm.at[pl.ds(pl.multiple_of(k * n + off, 8), SC_CHUNK)], idx_v[k])
            pltpu.sync_copy(table_hbm.at[pl.ds(off, SC_CHUNK)], rows_v)
            copies = [pltpu.make_async_copy(rows_v, out_hbm.at[idx_v[k]], sem) for k in range(n_slots)]
            for cp in copies:
                cp.start()
            for cp in copies:
                cp.wait()

    return scatter(table, pos_flat)


def _sc_gather_rows(table, idx):
    b = idx.shape[0]
    d = table.shape[1]
    nw = SC_CORES * SC_SUBCORES
    nbuf = 2
    assert b % (nw * SC_CHUNK * nbuf) == 0
    per_w = b // nw
    n_chunks = per_w // SC_CHUNK
    mesh = plsc.VectorSubcoreMesh(core_axis_name="c", subcore_axis_name="s")

    @functools.partial(
        pl.kernel, mesh=mesh,
        out_type=jax.ShapeDtypeStruct((b, d), table.dtype),
        scratch_types=[pltpu.VMEM((per_w,), jnp.int32),
                       [pltpu.VMEM((SC_CHUNK, d), table.dtype) for _ in range(nbuf)],
                       [pltpu.SemaphoreType.DMA for _ in range(nbuf)],
                       [pltpu.SemaphoreType.DMA for _ in range(nbuf)]],
    )
    def gather(table_hbm, idx_hbm, out_hbm, idx_v, rows, gsem, wsem):
        base = _sc_worker_base(per_w)
        pltpu.sync_copy(idx_hbm.at[pl.ds(pl.multiple_of(base, 8), per_w)], idx_v)

        def fetch(ci, slot):
            src = table_hbm.at[idx_v.at[pl.ds(pl.multiple_of(ci * SC_CHUNK, 8), SC_CHUNK)]]
            return pltpu.make_async_copy(src, rows[slot], gsem[slot])

        def put(ci, slot):
            dst = out_hbm.at[pl.ds(pl.multiple_of(base + ci * SC_CHUNK, 8), SC_CHUNK)]
            return pltpu.make_async_copy(rows[slot], dst, wsem[slot])

        for slot in range(nbuf):
            fetch(slot, slot).start()

        @pl.loop(0, n_chunks, step=nbuf)
        def _(c0):
            for slot in range(nbuf):
                ci = c0 + slot
                fetch(ci, slot).wait()
                put(ci, slot).start()
                put(ci, slot).wait()

                @pl.when(ci + nbuf < n_chunks)
                def _():
                    fetch(ci + nbuf, slot).start()

    return gather(table, idx)


def _route_positions(n_tok, slots, counts):
    cnt = counts[:, 0].astype(jnp.int32)
    cnt_pad = ((cnt + MOE_TR - 1) // MOE_TR) * MOE_TR
    off_end = jnp.cumsum(cnt_pad)
    off = off_end - cnt_pad
    eid, rank = slots[:TOP_K], slots[TOP_K:]
    eids = jnp.arange(N_EXPERTS, dtype=jnp.int32)
    pos = jnp.sum(jnp.where(eid[..., None] == eids, off, 0), axis=-1) + rank
    n_tiles = (TOP_K * n_tok) // MOE_TR + N_EXPERTS
    tile_start = jnp.arange(n_tiles, dtype=jnp.int32) * MOE_TR
    tile_expert = jnp.sum((tile_start[:, None] >= off_end[None, :]).astype(jnp.int32), axis=1)
    tile_expert = jnp.minimum(tile_expert, N_EXPERTS - 1)
    n_used = (off_end[-1] // MOE_TR).reshape(1)
    return pos, tile_expert, n_used, n_tiles


def _expert_ffn(x_lo, x_hi, gu, dn):
    half = D_MODEL // 2
    a = (jnp.dot(x_lo, gu[0:half, :], preferred_element_type=F32)
         + jnp.dot(x_hi, gu[half:, :], preferred_element_type=F32))
    hg = a[:, :D_EXPERT]
    act = (hg * jax.nn.sigmoid(hg)) * a[:, D_EXPERT:]
    return jnp.dot(act.astype(BF16), dn, preferred_element_type=F32)


def _experts_kernel(te_ref, nu_ref, x_ref, gu_ref, dn_ref, y_ref):
    i = pl.program_id(0)

    @pl.when(i < nu_ref[0])
    def _():
        lo, hi = _unpack_halves(x_ref[...])
        y = _expert_ffn(lo.astype(BF16), hi.astype(BF16), gu_ref[...].astype(BF16), dn_ref[...].astype(BF16))
        y_ref[...] = _pack_halves(y.astype(BF16).astype(F32))

    @pl.when(i >= nu_ref[0])
    def _():
        y_ref[...] = jnp.zeros_like(y_ref)


def _experts_call(l, xs, tile_expert, n_used, n_tiles, w_gu, w_dn):
    half = D_MODEL // 2
    grid_spec = pltpu.PrefetchScalarGridSpec(
        num_scalar_prefetch=2,
        grid=(n_tiles,),
        in_specs=[pl.BlockSpec((MOE_TR, half), lambda i, te, nu: (jnp.minimum(i, jnp.maximum(nu[0], 1) - 1), 0)),
                  pl.BlockSpec((None, None, D_MODEL, 2 * D_EXPERT), lambda i, te, nu: (l, te[i], 0, 0)),
                  pl.BlockSpec((None, None, D_EXPERT, D_MODEL), lambda i, te, nu: (l, te[i], 0, 0))],
        out_specs=pl.BlockSpec((MOE_TR, half), lambda i, te, nu: (i, 0)),
    )
    return pl.pallas_call(
        _experts_kernel,
        grid_spec=grid_spec,
        out_shape=jax.ShapeDtypeStruct((n_tiles * MOE_TR, half), jnp.uint32),
        compiler_params=_cparams(("arbitrary",)),
        name="moe_experts",
    )(tile_expert, n_used, xs, w_gu, w_dn)


MOE_OUT_PARTS = 2


def _moe_out_kernel(n_ctx_tiles, tile0, first, *refs):
    if first:
        yt_ref, g_ref, h_ref, sgu_ref, sdn_ref, x1_ref, gate_ref, oc_ref, ol_ref = refs
    else:
        yt_ref, g_ref, h_ref, sgu_ref, sdn_ref, x1_ref, gate_ref, _, ol_ref = refs
    i = pl.program_id(0) + tile0
    gts = g_ref[...]
    lane = lax.broadcasted_iota(jnp.int32, gts.shape, 1)
    acc_lo, acc_hi = None, None
    for k in range(TOP_K):
        ge = jnp.sum(jnp.where(lane == k, gts, 0.0), axis=1, keepdims=True)
        lo, hi = _unpack_halves(yt_ref[k])
        acc_lo = ge * lo if acc_lo is None else acc_lo + ge * lo
        acc_hi = ge * hi if acc_hi is None else acc_hi + ge * hi
    routed = jnp.concatenate([acc_lo, acc_hi], axis=1)
    h = h_ref[...]
    half = D_MODEL // 2
    shared = _expert_ffn(h[:, :half], h[:, half:], sgu_ref[...], sdn_ref[...])
    y = x1_ref[...] + gate_ref[...] * (routed + shared)
    if first:
        @pl.when(i < n_ctx_tiles)
        def _():
            oc_ref[...] = y

        @pl.when(i >= n_ctx_tiles)
        def _():
            ol_ref[...] = y
    else:
        ol_ref[...] = y


def _moe_out_call(geom, l, part, yt, gates, h2, sgu_bf, sdn_bf, x1, mod6, prev_lat=None):
    tm = 512
    nct = geom.n_ctx // tm
    n_part = geom.n_tok // tm // MOE_OUT_PARTS
    t0 = part * n_part
    first = part == 0
    assert nct <= n_part
    half = D_MODEL // 2
    tok = lambda w: pl.BlockSpec((tm, w), lambda i: (i + t0, 0))
    in_specs = [pl.BlockSpec((TOP_K, tm, half), lambda i: (0, i, 0)),
                tok(GATE_W), tok(D_MODEL),
                pl.BlockSpec((None, D_MODEL, 2 * D_EXPERT), lambda i: (l, 0, 0)),
                pl.BlockSpec((None, D_EXPERT, D_MODEL), lambda i: (l, 0, 0)),
                tok(D_MODEL),
                pl.BlockSpec((None, None, None, 1, D_MODEL), lambda i: (l, geom.mod_row(i + t0, tm), 5, 0, 0))]
    args = [yt, gates, h2, sgu_bf, sdn_bf, x1, mod6]
    lat_shape = jax.ShapeDtypeStruct((geom.n_lat, D_MODEL), F32)
    if first:
        out_specs = [pl.BlockSpec((tm, D_MODEL), lambda i: (jnp.minimum(i, nct - 1), 0)),
                     pl.BlockSpec((tm, D_MODEL), lambda i: (jnp.maximum(i - nct, 0), 0))]
        out_shape = [jax.ShapeDtypeStruct((geom.n_ctx, D_MODEL), F32), lat_shape]
        aliases = {}
    else:
        in_specs.append(pl.BlockSpec(memory_space=pl.ANY))
        args.append(prev_lat)
        out_specs = [pl.BlockSpec((tm, D_MODEL), lambda i: (i + t0 - nct, 0))]
        out_shape = [lat_shape]
        aliases = {len(args) - 1: 0}
    return pl.pallas_call(
        functools.partial(_moe_out_kernel, nct, t0, first),
        grid=(n_part,),
        in_specs=in_specs, out_specs=out_specs, out_shape=out_shape,
        input_output_aliases=aliases,
        compiler_params=_cparams(("arbitrary",)),
        name="moe_out",
    )(*args)


def _moe(geom, l, h2, h2p, gates, slots, counts, w_gu, w_dn, sgu_bf, sdn_bf, x1, mod6):
    pos, tile_expert, n_used, n_tiles = _route_positions(geom.n_tok, slots, counts)
    xs = _sc_scatter_rows(h2p, pos.reshape(-1), TOP_K, n_tiles * MOE_TR)
    ys = _experts_call(l, xs, tile_expert, n_used, n_tiles, w_gu, w_dn)
    n_part = geom.n_tok // MOE_OUT_PARTS
    y_ctx, y_lat = None, None
    for part in range(MOE_OUT_PARTS):
        pos_p = pos[:, part * n_part:(part + 1) * n_part].reshape(-1)
        yt = _sc_gather_rows(ys, pos_p).reshape(TOP_K, n_part, D_MODEL // 2)
        outs = _moe_out_call(geom, l, part, yt, gates, h2, sgu_bf, sdn_bf, x1, mod6, y_lat)
        if part == 0:
            y_ctx, y_lat = outs
        else:
            (y_lat,) = outs
    return y_ctx, y_lat


def _rope_tables(dec_seq):
    rows = dec_seq // GRID_W
    row = jnp.repeat(jnp.arange(rows, dtype=F32), GRID_W)
    col = jnp.tile(jnp.arange(GRID_W, dtype=F32), rows)
    inv = ROPE_BASE ** (-jnp.arange(ROPE_PAIRS, dtype=F32) / ROPE_PAIRS)
    ar = row[:, None] * inv[None, :]
    ac = col[:, None] * inv[None, :]
    cos64 = jnp.concatenate([jnp.cos(ar), jnp.cos(ar), jnp.cos(ac), jnp.cos(ac)], axis=1)
    sin64 = jnp.concatenate([-jnp.sin(ar), jnp.sin(ar), -jnp.sin(ac), jnp.sin(ac)], axis=1)
    return jnp.tile(cos64, (1, 2)), jnp.tile(sin64, (1, 2))


def _block_diag_gate(wg_dir):
    eye = jnp.eye(LRU_BLOCKS, dtype=F32)
    dense = jnp.einsum('gnij,nm->gnimj', wg_dir.astype(F32), eye).reshape(2, D_RNN, D_RNN)
    return jnp.concatenate([dense[0], dense[1]], axis=1)


def kernel(x_prompt, x_sample, cache_k, cache_v, state_lru, state_ret, c, c_ctx, ada_w, ada_b, norm1_w, norm2_w, w_in, conv_w, conv_b, lru_gate_w, lru_gate_b, lru_lambda, q_norm_w, k_norm_w, diff_lambda, subln_w, ret_decay, w_branch, w_out, router_w, router_bias, w_exp_gu, w_exp_down, w_sh_gu, w_sh_down):
    batch, seq, _ = x_prompt.shape
    dec_batch, dec_seq, _ = x_sample.shape
    assert 1 + dec_batch <= MOD_ROWS
    geom = _Geom(batch, seq, dec_batch, dec_seq)
    hs = RET_HEADS * RET_QK
    aw = DA_HEADS * 2 * DA_QK

    x_ctx = x_prompt.reshape(geom.n_ctx, D_MODEL)
    x_lat = x_sample.reshape(geom.n_lat, D_MODEL)
    cvec = jnp.zeros((MOD_ROWS, D_MODEL), F32).at[0].set(c_ctx).at[1:1 + dec_batch].set(c)
    mod6 = _ada_call(cvec, ada_w, ada_b).reshape(DEPTH, MOD_ROWS, 6, 1, D_MODEL)

    ones_bd = jnp.kron(jnp.eye(aw // DA_QK, dtype=F32), jnp.ones((DA_QK, DA_QK), F32)).astype(BF16)
    cos_t, sin_t = _rope_tables(dec_seq)

    w_in_bf = w_in.astype(BF16)
    sgu_bf, sdn_bf = w_sh_gu.astype(BF16), w_sh_down.astype(BF16)

    ks, vs, lrus, rets = [], [], [], []
    for l in range(DEPTH):
        lam_init = 0.8 - 0.6 * math.exp(-0.3 * l)
        w_rkt_bf = w_in[l][:, C_RK:C_RK + hs].T.astype(BF16)
        proj, rkt = _inproj_call(geom, l, x_ctx, x_lat, mod6, norm1_w[l], w_in_bf, w_rkt_bf)

        sp = jax.nn.softplus(-lru_lambda[l].astype(F32))
        h0 = jnp.concatenate([jnp.zeros((batch, 2, D_RNN), F32), state_lru[:, l].astype(F32)], axis=0)
        h0 = h0.reshape(geom.n_seq, 2, 1, D_RNN)
        cb = conv_b[l].reshape(1, D_RNN)
        lru_args = []
        for d in range(2):
            lru_args.append((_block_diag_gate(lru_gate_w[l, d]).astype(BF16),
                             lru_gate_b[l, d].reshape(1, 2 * D_RNN), sp[d].reshape(1, D_RNN)))
        hf, hf_last = _lru_call(geom, False, proj, conv_w[l], cb, *lru_args[0], h0)
        branch_a, hb_last = _lru_call(geom, True, proj, conv_w[l], cb, *lru_args[1], h0, hf)

        qw = jnp.tile(q_norm_w[l], aw // DA_QK).reshape(1, aw)
        kw = jnp.tile(k_norm_w[l], aw // DA_QK).reshape(1, aw)
        q_c, k_c, k_c32, v_c32 = _prep_call(geom, False, proj, qw, kw, ones_bd)
        q_l, k_l = _prep_call(geom, True, proj, qw, kw, ones_bd, cos_t, sin_t)
        lam_p = diff_lambda[l].astype(F32)
        lam = jnp.exp(jnp.sum(lam_p[0] * lam_p[1])) - jnp.exp(jnp.sum(lam_p[2] * lam_p[3])) + lam_init
        q_bound = DA_QK * jnp.max(jnp.square(q_norm_w[l].astype(F32))) * (DA_QK ** -0.5 * LOG2E) ** 2
        k_bound = DA_QK * jnp.max(jnp.square(k_norm_w[l].astype(F32)))
        kc32 = cache_k[:, l].astype(F32)
        kc_bound = jnp.maximum(k_bound, jnp.max(jnp.sum(jnp.square(kc32), axis=-1)))

        def attn_par(kb):
            ok = (q_bound * kb * 1.05 < ATT_SAFE_LOGIT ** 2).astype(F32)
            return jnp.stack([lam, ok])

        assert geom.n_ctx % dec_seq == 0
        cache = (kc32.reshape(dec_batch, -1, aw).astype(BF16),
                 cache_v[:, l].reshape(dec_batch, -1, DA_HEADS * DA_V).astype(BF16))
        att_c = _attn_call(attn_par(k_bound), lam_init, q_c, k_c, proj, 0, batch, seq, seq, 256, subln_w[l])
        att_l = _attn_call(attn_par(kc_bound), lam_init, q_l, k_l, proj, geom.n_ctx // dec_seq, dec_batch,
                           dec_seq, dec_seq, 4 * ATT_TQ, subln_w[l], cache)

        dsum, qdf, qdb, kd_f, kd_b, cd_f, cd_b = _ret_tables(ret_decay[l])
        s0 = jnp.concatenate([jnp.zeros((batch, 2, hs, RET_V), F32),
                              state_ret[:, l].astype(F32).reshape(dec_batch, 2, hs, RET_V)], axis=0)
        sb_start, sb_end = _ret_bwd_call(geom, proj, rkt, kd_b, cd_b, s0)
        branch_c, sf_end = _ret_main_call(geom, proj, rkt, dsum, qdf, qdb, kd_f, cd_f, s0, sb_start)

        r_t = router_w[l].T.astype(F32)
        r_hi = r_t.astype(BF16)
        r_lo = (r_t - r_hi.astype(F32)).astype(BF16)
        x1, h2, h2p, logits_t = _merge_call(geom, l, branch_a, att_c, att_l, branch_c, proj, x_ctx, x_lat, mod6,
                                            norm2_w[l], w_branch[l].astype(BF16), w_out[l].astype(BF16), r_hi, r_lo)
        gates, slots, counts = _router_call(geom, logits_t, router_bias[l].astype(F32))
        x_ctx, x_lat = _moe(geom, l, h2, h2p, gates, slots, counts, w_exp_gu, w_exp_down, sgu_bf, sdn_bf, x1, mod6)

        ks.append(k_c32.reshape(batch, seq, DA_HEADS, 2, DA_QK))
        vs.append(v_c32.reshape(batch, seq, DA_HEADS, DA_V))
        lrus.append(jnp.stack([hf_last[:batch, 0], hb_last[:batch, 0]], axis=1))
        rets.append(jnp.stack([sf_end[:batch].reshape(batch, RET_HEADS, RET_QK, RET_V),
                               sb_end[:batch].reshape(batch, RET_HEADS, RET_QK, RET_V)], axis=1))

    y_prompt = x_ctx.reshape(batch, seq, D_MODEL)
    y_sample = x_lat.reshape(dec_batch, dec_seq, D_MODEL)
    return (y_prompt, y_sample, jnp.stack(ks, axis=1), jnp.stack(vs, axis=1),
            jnp.stack(lrus, axis=1), jnp.stack(rets, axis=1))
```

```python
import functools
import math

import numpy as np
import jax
import jax.numpy as jnp
from jax import lax
from jax.experimental import pallas as pl
from jax.experimental.pallas import tpu as pltpu
from jax.experimental.pallas import tpu_sc as plsc

F32 = jnp.float32
BF16 = jnp.bfloat16

D_MODEL = 1024
DEPTH = 2
GRID_W = 64
D_RNN = 512
LRU_BLOCKS = 8
LRU_BLOCK = D_RNN // LRU_BLOCKS
CONV_W = 4
LRU_C = 8.0
DA_HEADS = 4
DA_QK = 64
DA_V = 128
ROPE_PAIRS = DA_QK // 4
ROPE_BASE = 10000.0
RET_HEADS = 4
RET_QK = 64
RET_V = 128
BRANCH_W = 512
N_BRANCH = 3
D_IN = 7168
N_EXPERTS = 64
TOP_K = 8
N_GROUPS = 8
TOPK_GROUPS = 4
D_EXPERT = 256
ROUTED_SCALE = 2.5
EPS = 1e-6

C_XA, C_GA, C_DQ, C_DK, C_DV = 0, 512, 1024, 1536, 2048
C_RQ, C_RK, C_RV, C_RG, C_GL = 2560, 2816, 3072, 3584, 4096

BLK = 256
LRU_SUB = 8
GATE_W = 128
MOD_ROWS = 8
VMEM_LIMIT = 56 * 1024 * 1024


def _cparams(sem, vmem_limit=VMEM_LIMIT):
    return pltpu.CompilerParams(dimension_semantics=sem, vmem_limit_bytes=vmem_limit)


class _Geom:
    def __init__(self, batch, seq, dec_batch, dec_seq):
        assert seq == BLK and dec_seq % BLK == 0
        self.batch, self.seq, self.dec_batch, self.dec_seq = batch, seq, dec_batch, dec_seq
        self.n_ctx = batch * seq
        self.n_lat = dec_batch * dec_seq
        self.n_tok = self.n_ctx + self.n_lat
        self.ctx_blocks = self.n_ctx // BLK
        self.lat_blocks = dec_seq // BLK
        self.n_blocks = self.n_tok // BLK
        self.n_seq = batch + dec_batch

    def mod_row(self, i, tile):
        nct = self.n_ctx // tile
        per = self.dec_seq // tile
        return jnp.where(i < nct, 0, 1 + (i - nct) // per)

    def seq_id(self, i):
        return jnp.where(i < self.ctx_blocks, i, self.ctx_blocks + (i - self.ctx_blocks) // self.lat_blocks)

    def seq_start(self, i):
        return jnp.logical_or(i < self.ctx_blocks, (i - self.ctx_blocks) % self.lat_blocks == 0)

    def seq_end(self, i):
        return jnp.logical_or(i < self.ctx_blocks, (i - self.ctx_blocks) % self.lat_blocks == self.lat_blocks - 1)


def _ada_kernel(c_ref, w_ref, b_ref, o_ref):
    cv = c_ref[...]
    s = cv * jax.nn.sigmoid(cv)
    o_ref[...] = jnp.dot(s, w_ref[...], preferred_element_type=F32,
                         precision=lax.Precision.HIGHEST) + b_ref[...]


def _ada_call(cvec, ada_w, ada_b):
    depth = ada_w.shape[0]
    nt = 6
    return pl.pallas_call(
        _ada_kernel,
        grid=(depth, nt),
        in_specs=[pl.BlockSpec((MOD_ROWS, D_MODEL), lambda l, j: (0, 0)),
                  pl.BlockSpec((None, D_MODEL, D_MODEL), lambda l, j: (l, 0, j)),
                  pl.BlockSpec((None, 1, D_MODEL), lambda l, j: (l, 0, j))],
        out_specs=pl.BlockSpec((None, MOD_ROWS, D_MODEL), lambda l, j: (l, 0, j)),
        out_shape=jax.ShapeDtypeStruct((depth, MOD_ROWS, 6 * D_MODEL), F32),
        compiler_params=_cparams(("arbitrary", "arbitrary")),
        name="ada_mod",
    )(cvec, ada_w, ada_b.reshape(depth, 1, 6 * D_MODEL))


def _mod_spec(geom, l, which, tile, ngrid):
    if ngrid == 1:
        return pl.BlockSpec((None, None, None, 1, D_MODEL),
                            lambda i: (l, geom.mod_row(i, tile), which, 0, 0))
    return pl.BlockSpec((None, None, None, 1, D_MODEL),
                        lambda i, j: (l, geom.mod_row(i, tile), which, 0, 0))


def _split_in_specs(geom, tile, width, ngrid):
    nct = geom.n_ctx // tile
    if ngrid == 1:
        return [pl.BlockSpec((tile, width), lambda i: (jnp.minimum(i, nct - 1), 0)),
                pl.BlockSpec((tile, width), lambda i: (jnp.maximum(i - nct, 0), 0))]
    return [pl.BlockSpec((tile, width), lambda i, j: (jnp.minimum(i, nct - 1), 0)),
            pl.BlockSpec((tile, width), lambda i, j: (jnp.maximum(i - nct, 0), 0))]


def _pick_part(n_ctx_tiles, c_ref, l_ref):
    return jnp.where(pl.program_id(0) < n_ctx_tiles, c_ref[...], l_ref[...])


def _pack_halves(y):
    w = y.shape[1] // 2
    bits = pltpu.bitcast(y, jnp.uint32)
    return (bits[:, :w] >> 16) | (bits[:, w:] & jnp.uint32(0xFFFF0000))


def _unpack_halves(p):
    return pltpu.bitcast(p << 16, F32), pltpu.bitcast(p & jnp.uint32(0xFFFF0000), F32)


INPROJ_TM = 512
INPROJ_TN = 1024


def _inproj_kernel(n_ctx_tiles, xc_ref, xl_ref, sc_ref, sh_ref, nw_ref, w_ref, wkt_ref, o_ref, kt_ref):
    x = _pick_part(n_ctx_tiles, xc_ref, xl_ref)
    ms = jnp.mean(x * x, axis=-1, keepdims=True)
    y = x * lax.rsqrt(ms + EPS) * nw_ref[...]
    hb = (y * (1.0 + sc_ref[...]) + sh_ref[...]).astype(BF16)
    kt_ref[...] = lax.dot_general(wkt_ref[...], hb, (((1,), (1,)), ((), ())),
                                  preferred_element_type=F32).astype(BF16)
    for j in range(D_IN // INPROJ_TN):
        cols = slice(j * INPROJ_TN, (j + 1) * INPROJ_TN)
        o_ref[:, cols] = jnp.dot(hb, w_ref[:, cols], preferred_element_type=F32).astype(BF16)


def _inproj_call(geom, l, x_ctx, x_lat, mod6, norm_w, w_in_bf, w_rkt_bf):
    tm = INPROJ_TM
    return pl.pallas_call(
        functools.partial(_inproj_kernel, geom.n_ctx // tm),
        grid=(geom.n_tok // tm,),
        in_specs=_split_in_specs(geom, tm, D_MODEL, 1) + [
                  _mod_spec(geom, l, 1, tm, 1),
                  _mod_spec(geom, l, 0, tm, 1),
                  pl.BlockSpec((1, D_MODEL), lambda i: (0, 0)),
                  pl.BlockSpec((None, D_MODEL, D_IN), lambda i: (l, 0, 0), pipeline_mode=pl.Buffered(1)),
                  pl.BlockSpec((RET_HEADS * RET_QK, D_MODEL), lambda i: (0, 0))],
        out_specs=[pl.BlockSpec((tm, D_IN), lambda i: (i, 0)),
                   pl.BlockSpec((RET_HEADS * RET_QK, tm), lambda i: (0, i))],
        out_shape=[jax.ShapeDtypeStruct((geom.n_tok, D_IN), BF16),
                   jax.ShapeDtypeStruct((RET_HEADS * RET_QK, geom.n_tok), BF16)],
        compiler_params=_cparams(("arbitrary",)),
        name="inproj",
    )(x_ctx, x_lat, mod6, mod6, norm_w.reshape(1, D_MODEL), w_in_bf, w_rkt_bf)


def _gelu_tanh(x):
    return 0.5 * x * (1.0 + jnp.tanh(math.sqrt(2.0 / math.pi) * (x + 0.044715 * (x * x * x))))


def _lru_kernel(geom, reverse, *refs):
    if reverse:
        (xa_ref, xp_ref, xn_ref, cw_ref, cb_ref, wg_ref, bg_ref, sp_ref, h0_ref, perm_ref, permt_ref,
         ga_ref, hf_ref, out_ref, hl_ref, c_scr) = refs
    else:
        (xa_ref, xp_ref, xn_ref, cw_ref, cb_ref, wg_ref, bg_ref, sp_ref, h0_ref, perm_ref,
         out_ref, hl_ref, c_scr) = refs
    g = pl.program_id(0)
    i = geom.n_blocks - 1 - g if reverse else g
    start = geom.seq_start(i)
    end = geom.seq_end(i)

    @pl.when(end if reverse else start)
    def _():
        c_scr[...] = h0_ref[...]

    sub_len = BLK // LRU_SUB
    perm = perm_ref[...]
    x = jnp.dot(perm, xa_ref[...], preferred_element_type=F32)
    pm = jnp.where(start, 0.0, 1.0)
    nm = jnp.where(end, 0.0, 1.0)
    hp = xp_ref.shape[0]
    p1 = xp_ref[hp - 1:hp, :].astype(F32) * pm
    p2 = xp_ref[hp - 2:hp - 1, :].astype(F32) * pm
    n0 = xn_ref[0:1, :].astype(F32) * nm
    row = lax.broadcasted_iota(jnp.int32, x.shape, 0)
    xm1 = jnp.where(row < LRU_SUB, pltpu.roll(x, LRU_SUB + 1, 0), pltpu.roll(x, LRU_SUB, 0))
    xm1 = jnp.where(row == 0, p1, xm1)
    xm2 = jnp.where(row < 2 * LRU_SUB, pltpu.roll(x, 2 * LRU_SUB + 1, 0), pltpu.roll(x, 2 * LRU_SUB, 0))
    xm2 = jnp.where(row == 0, p2, jnp.where(row == LRU_SUB, p1, xm2))
    xp1 = jnp.where(row >= BLK - LRU_SUB, pltpu.roll(x, BLK - LRU_SUB - 1, 0),
                    pltpu.roll(x, BLK - LRU_SUB, 0))
    xp1 = jnp.where(row == BLK - 1, n0, xp1)
    xc = (cw_ref[0:1, :] * xm2 + cw_ref[1:2, :] * xm1 + cw_ref[2:3, :] * x
          + cw_ref[3:4, :] * xp1 + cb_ref[...])

    gt = jnp.dot(xc.astype(BF16), wg_ref[...], preferred_element_type=F32) + bg_ref[...]
    r = jax.nn.sigmoid(gt[:, :D_RNN])
    ig = jax.nn.sigmoid(gt[:, D_RNN:])
    a = jnp.exp(-LRU_C * r * sp_ref[...])
    u = jnp.sqrt(1.0 - a * a) * ig * xc

    h = jnp.zeros((LRU_SUB, D_RNN), F32)
    p = jnp.ones((LRU_SUB, D_RNN), F32)
    h_loc = [None] * sub_len
    p_loc = [None] * sub_len
    for t in (range(sub_len - 1, -1, -1) if reverse else range(sub_len)):
        a_t = a[t * LRU_SUB:(t + 1) * LRU_SUB, :]
        h = a_t * h + u[t * LRU_SUB:(t + 1) * LRU_SUB, :]
        p = a_t * p
        h_loc[t] = h
        p_loc[t] = p
    h_in = [None] * LRU_SUB
    state = c_scr[...]
    for k in (range(LRU_SUB - 1, -1, -1) if reverse else range(LRU_SUB)):
        h_in[k] = state
        state = h[k:k + 1, :] + p[k:k + 1, :] * state
    c_scr[...] = state
    hl_ref[...] = state
    h_in = jnp.concatenate(h_in, axis=0)
    h_full = jnp.concatenate([h_loc[t] + p_loc[t] * h_in for t in range(sub_len)], axis=0)
    if reverse:
        gv = jnp.dot(perm, ga_ref[...], preferred_element_type=F32)
        y = (_gelu_tanh(gv) * (hf_ref[...] + h_full)).astype(BF16)
        out_ref[...] = jnp.dot(permt_ref[...], y, preferred_element_type=F32).astype(BF16)
    else:
        out_ref[...] = h_full


def _lru_call(geom, reverse, proj, conv_w, conv_b, wg, bg, sp, h0, hf=None):
    nb = geom.n_blocks
    halo = 16
    hpb = BLK // halo

    def blk(g):
        return nb - 1 - g if reverse else g

    d = 1 if reverse else 0
    in_specs = [
        pl.BlockSpec((BLK, D_RNN), lambda g: (blk(g), C_XA // D_RNN)),
        pl.BlockSpec((halo, D_RNN), lambda g: (jnp.maximum(blk(g) * hpb - 1, 0), C_XA // D_RNN)),
        pl.BlockSpec((halo, D_RNN), lambda g: (jnp.minimum((blk(g) + 1) * hpb, nb * hpb - 1), C_XA // D_RNN)),
        pl.BlockSpec((CONV_W, D_RNN), lambda g: (0, 0)),
        pl.BlockSpec((1, D_RNN), lambda g: (0, 0)),
        pl.BlockSpec((D_RNN, 2 * D_RNN), lambda g: (0, 0)),
        pl.BlockSpec((1, 2 * D_RNN), lambda g: (0, 0)),
        pl.BlockSpec((1, D_RNN), lambda g: (0, 0)),
        pl.BlockSpec((None, None, 1, D_RNN), lambda g: (geom.seq_id(blk(g)), d, 0, 0)),
    ]
    pos = np.arange(BLK)
    perm_np = np.zeros((BLK, BLK), np.float32)
    perm_np[pos, (pos % LRU_SUB) * (BLK // LRU_SUB) + pos // LRU_SUB] = 1.0
    in_specs.append(pl.BlockSpec((BLK, BLK), lambda g: (0, 0)))
    args = [proj, proj, proj, conv_w, conv_b, wg, bg, sp, h0, jnp.asarray(perm_np, BF16)]
    if reverse:
        in_specs += [pl.BlockSpec((BLK, BLK), lambda g: (0, 0)),
                     pl.BlockSpec((BLK, D_RNN), lambda g: (blk(g), C_GA // D_RNN)),
                     pl.BlockSpec((BLK, D_RNN), lambda g: (blk(g), 0))]
        args += [jnp.asarray(perm_np.T, BF16), proj, hf]
        out_dtype = BF16
    else:
        out_dtype = F32
    scratch = [pltpu.VMEM((1, D_RNN), F32)]
    return pl.pallas_call(
        functools.partial(_lru_kernel, geom, reverse),
        grid=(nb,),
        in_specs=in_specs,
        out_specs=[pl.BlockSpec((BLK, D_RNN), lambda g: (blk(g), 0)),
                   pl.BlockSpec((None, 1, D_RNN), lambda g: (blk(g), 0, 0))],
        out_shape=[jax.ShapeDtypeStruct((geom.n_tok, D_RNN), out_dtype),
                   jax.ShapeDtypeStruct((nb, 1, D_RNN), F32)],
        scratch_shapes=scratch,
        compiler_params=_cparams(("arbitrary",)),
        name="lru_bwd" if reverse else "lru_fwd",
    )(*args)


def _group_rms(x, w, ones):
    xx = x * x
    hi = xx.astype(BF16)
    lo = (xx - hi.astype(F32)).astype(BF16)
    ss = (jnp.dot(hi, ones, preferred_element_type=F32)
          + jnp.dot(lo, ones, preferred_element_type=F32))
    return x * lax.rsqrt(ss * (1.0 / DA_QK) + EPS) * w


def _rope(x, cos, sin):
    lane = lax.broadcasted_iota(jnp.int32, x.shape, 1)
    first = (lane % (2 * ROPE_PAIRS)) < ROPE_PAIRS
    w = x.shape[1]
    partner = jnp.where(first, pltpu.roll(x, w - ROPE_PAIRS, 1), pltpu.roll(x, ROPE_PAIRS, 1))
    return x * cos + partner * sin


def _prep_kernel(rope, *refs):
    if rope:
        dq_ref, dk_ref, qw_ref, kw_ref, ones_ref, cos_ref, sin_ref, q_out, k_out = refs
    else:
        dq_ref, dk_ref, qw_ref, kw_ref, ones_ref, dv_ref, q_out, k_out, kf_out, vf_out = refs
        vf_out[...] = dv_ref[...].astype(F32)
    ones = ones_ref[...]
    q = _group_rms(dq_ref[...].astype(F32), qw_ref[...], ones)
    k = _group_rms(dk_ref[...].astype(F32), kw_ref[...], ones)
    if rope:
        cos = jnp.concatenate([cos_ref[...]] * 4, axis=1)
        sin = jnp.concatenate([sin_ref[...]] * 4, axis=1)
        q = _rope(q, cos, sin)
        k = _rope(k, cos, sin)
    else:
        kf_out[...] = k
    q_out[...] = (q * (DA_QK ** -0.5 * math.log2(math.e))).astype(BF16)
    k_out[...] = k.astype(BF16)


def _prep_call(geom, latent, proj, qw, kw, ones, cos=None, sin=None):
    tm = 512
    w = DA_HEADS * 2 * DA_QK
    if latent:
        n, off = geom.n_lat, geom.n_ctx // tm
        per = geom.dec_seq // tm
    else:
        n, off = geom.n_ctx, 0
    in_specs = [pl.BlockSpec((tm, w), lambda i: (i + off, C_DQ // w)),
                pl.BlockSpec((tm, w), lambda i: (i + off, C_DK // w)),
                pl.BlockSpec((1, w), lambda i: (0, 0)),
                pl.BlockSpec((1, w), lambda i: (0, 0)),
                pl.BlockSpec((w, w), lambda i: (0, 0))]
    args = [proj, proj, qw, kw, ones]
    out_specs = [pl.BlockSpec((tm, w), lambda i: (i, 0)), pl.BlockSpec((tm, w), lambda i: (i, 0))]
    out_shape = [jax.ShapeDtypeStruct((n, w), BF16), jax.ShapeDtypeStruct((n, w), BF16)]
    if latent:
        in_specs += [pl.BlockSpec((tm, 2 * DA_QK), lambda i: (i % per, 0)),
                     pl.BlockSpec((tm, 2 * DA_QK), lambda i: (i % per, 0))]
        args += [cos, sin]
    else:
        in_specs.append(pl.BlockSpec((tm, w), lambda i: (i, C_DV // w)))
        args.append(proj)
        out_specs += [pl.BlockSpec((tm, w), lambda i: (i, 0)), pl.BlockSpec((tm, w), lambda i: (i, 0))]
        out_shape += [jax.ShapeDtypeStruct((n, w), F32), jax.ShapeDtypeStruct((n, w), F32)]
    return pl.pallas_call(
        functools.partial(_prep_kernel, latent),
        grid=(n // tm,),
        in_specs=in_specs, out_specs=out_specs, out_shape=out_shape,
        compiler_params=_cparams(("arbitrary",)),
        name="qk_prep_lat" if latent else "qk_prep_ctx",
    )(*args)


ATT_KC = 256
ATT_TQ = 256
LOG2E = math.log2(math.e)
ATT_SAFE_LOGIT = 60.0


def _attn_kernel(out_scale, has_cache, *refs):
    if has_cache:
        par_ref, q_ref, kc_ref, vc_ref, kl_ref, vl_ref, sw_ref, o_ref, e_scr, o_scr = refs
        srcs = [(kc_ref, vc_ref), (kl_ref, vl_ref)]
    else:
        par_ref, q_ref, kl_ref, vl_ref, sw_ref, o_ref, e_scr, o_scr = refs
        srcs = [(kl_ref, vl_ref)]
    chunks = [(kr, vr, st) for kr, vr in srcs for st in range(0, kr.shape[0], ATT_KC)]
    lam = par_ref[0]
    no_shift = par_ref[1] > 0.5
    tqs = ATT_TQ
    nsub = q_ref.shape[0] // tqs
    nt = (((1,), (1,)), ((), ()))
    half = ATT_KC // 2

    def stacked_q(sb):
        q = q_ref[sb * tqs:(sb + 1) * tqs, :]
        lane = lax.broadcasted_iota(jnp.int32, q.shape, 1)
        zero = jnp.zeros_like(q)
        return jnp.concatenate([jnp.where(lane < DA_QK, q, zero), jnp.where(lane >= DA_QK, q, zero)], axis=0)

    def logits(qq, c):
        kr, vr, st = chunks[c]
        return lax.dot_general(qq, kr[st:st + ATT_KC, :], nt, preferred_element_type=F32)

    def fold(total, e):
        part = e[:, :half] + e[:, half:]
        return part if total is None else total + part

    def row_stats(lsum):
        l = jnp.sum(lsum, axis=-1, keepdims=True)
        l1 = l[0:tqs]
        return l1, lam * l1 / l[tqs:2 * tqs]

    def pv(acc, buf, c, rho):
        kr, vr, st = chunks[c]
        w = (e_scr[buf, c, 0:tqs, :] - rho * e_scr[buf, c, tqs:2 * tqs, :]).astype(BF16)
        t = jnp.dot(w, vr[st:st + ATT_KC, :], preferred_element_type=F32)
        return t if acc is None else acc + t

    nck = len(chunks)

    @pl.when(no_shift)
    def _():
        stats = None
        for sb in range(nsub + 1):
            qq = stacked_q(sb) if sb < nsub else None
            lsum, acc = None, None
            for c in range(nck):
                if sb < nsub:
                    e = jnp.exp2(logits(qq, c))
                    e_scr[sb % 2, c] = e
                    lsum = fold(lsum, e)
                if sb > 0:
                    acc = pv(acc, (sb - 1) % 2, c, stats[1])
            if sb > 0:
                o_scr[(sb - 1) * tqs:sb * tqs, :] = acc / stats[0]
            if sb < nsub:
                stats = row_stats(lsum)

    @pl.when(jnp.logical_not(no_shift))
    def _():
        for sb in range(nsub):
            qq = stacked_q(sb)
            m = None
            for c in range(nck):
                s = logits(qq, c)
                e_scr[0, c] = s
                mc = jnp.max(s, axis=-1, keepdims=True)
                m = mc if m is None else jnp.maximum(m, mc)
            lsum = None
            for c in range(nck):
                e = jnp.exp2(e_scr[0, c] - m)
                e_scr[0, c] = e
                lsum = fold(lsum, e)
            l1, rho = row_stats(lsum)
            acc = None
            for c in range(nck):
                acc = pv(acc, 0, c, rho)
            o_scr[sb * tqs:(sb + 1) * tqs, :] = acc / l1

    o = o_scr[...]
    y = o * lax.rsqrt(jnp.mean(o * o, axis=-1, keepdims=True) + EPS) * sw_ref[...]
    o_ref[...] = (y * out_scale).astype(BF16)


def _attn_call(par, lam_init, q2d, k2d, proj, v_row_off, n_b, t_q, t_kl, tq, subln_w, cache=None):
    hw = 2 * DA_QK
    nq = t_q // tq
    vcol = C_DV // DA_V
    in_specs = [pl.BlockSpec(memory_space=pltpu.SMEM),
                pl.BlockSpec((tq, hw), lambda b, h, qi: (b * nq + qi, h))]
    args = [par, q2d]
    n_chunks = t_kl // ATT_KC
    if cache is not None:
        kc, vc = cache
        p = kc.shape[1]
        n_chunks += p // ATT_KC
        in_specs += [pl.BlockSpec((None, p, hw), lambda b, h, qi: (b, 0, h)),
                     pl.BlockSpec((None, p, DA_V), lambda b, h, qi: (b, 0, h))]
        args += [kc, vc]
    in_specs += [pl.BlockSpec((t_kl, hw), lambda b, h, qi: (b, h)),
                 pl.BlockSpec((t_kl, DA_V), lambda b, h, qi: (v_row_off + b, vcol + h)),
                 pl.BlockSpec((1, DA_V), lambda b, h, qi: (0, 0))]
    args += [k2d, proj, subln_w.reshape(1, DA_V)]
    return pl.pallas_call(
        functools.partial(_attn_kernel, 1.0 - lam_init, cache is not None),
        grid=(n_b, DA_HEADS, nq),
        in_specs=in_specs,
        out_specs=pl.BlockSpec((tq, DA_V), lambda b, h, qi: (b * nq + qi, h)),
        out_shape=jax.ShapeDtypeStruct((n_b * t_q, DA_HEADS * DA_V), BF16),
        scratch_shapes=[pltpu.VMEM((2 if tq > ATT_TQ else 1, n_chunks, 2 * ATT_TQ, ATT_KC), F32),
                        pltpu.VMEM((tq, DA_V), F32)],
        compiler_params=_cparams(("arbitrary", "arbitrary", "arbitrary")),
        name="diff_attn_lat" if cache is not None else "diff_attn_ctx",
    )(*args)


def _ret_state_update(kt, v, kd, cd, s_old):
    parts = []
    for h in range(RET_HEADS):
        rows = slice(h * RET_QK, (h + 1) * RET_QK)
        kh = (kt[rows, :].astype(F32) * kd[rows, :]).astype(BF16)
        parts.append(jnp.dot(kh, v[:, h * RET_V:(h + 1) * RET_V], preferred_element_type=F32))
    return cd * s_old + jnp.concatenate(parts, axis=0)


def _ret_bwd_kernel(geom, kt_ref, v_ref, kd_ref, cd_ref, s0_ref, sstart_ref, send_ref, s_scr):
    i = geom.n_blocks - 1 - pl.program_id(0)

    @pl.when(geom.seq_end(i))
    def _():
        s_scr[...] = s0_ref[...]

    s_old = s_scr[...]
    sstart_ref[...] = s_old
    kt = kt_ref[...] * jnp.asarray(RET_QK ** -0.5, BF16)
    s_new = _ret_state_update(kt, v_ref[...], kd_ref[...], cd_ref[...], s_old)
    s_scr[...] = s_new
    send_ref[...] = s_new


def _ret_bwd_call(geom, proj, rkt, kd_b, cd_b, s0):
    nb = geom.n_blocks
    hs = RET_HEADS * RET_QK

    def blk(g):
        return nb - 1 - g

    return pl.pallas_call(
        functools.partial(_ret_bwd_kernel, geom),
        grid=(nb,),
        in_specs=[pl.BlockSpec((hs, BLK), lambda g: (0, blk(g))),
                  pl.BlockSpec((BLK, RET_HEADS * RET_V), lambda g: (blk(g), C_RV // (RET_HEADS * RET_V))),
                  pl.BlockSpec((hs, BLK), lambda g: (0, 0)),
                  pl.BlockSpec((hs, RET_V), lambda g: (0, 0)),
                  pl.BlockSpec((None, None, hs, RET_V), lambda g: (geom.seq_id(blk(g)), 1, 0, 0))],
        out_specs=[pl.BlockSpec((None, hs, RET_V), lambda g: (blk(g), 0, 0)),
                   pl.BlockSpec((None, hs, RET_V), lambda g: (blk(g), 0, 0))],
        out_shape=[jax.ShapeDtypeStruct((nb, hs, RET_V), F32),
                   jax.ShapeDtypeStruct((nb, hs, RET_V), F32)],
        scratch_shapes=[pltpu.VMEM((hs, RET_V), F32)],
        compiler_params=_cparams(("arbitrary",)),
        name="ret_bwd_state",
    )(rkt, proj, kd_b, cd_b, s0)


def _ret_main_kernel(geom, q_ref, kt_ref, v_ref, g_ref, dsum_ref, qdf_ref, qdb_ref, kd_ref, cd_ref,
                     s0_ref, sb_ref, o_ref, send_ref, s_scr):
    i = pl.program_id(0)

    @pl.when(geom.seq_start(i))
    def _():
        s_scr[...] = s0_ref[...]

    s_f = s_scr[...]
    s_fb = s_f.astype(BF16)
    s_bb = sb_ref[...].astype(BF16)
    q = q_ref[...].astype(F32)
    kt = kt_ref[...] * jnp.asarray(RET_QK ** -0.5, BF16)
    v = v_ref[...]
    lane = lax.broadcasted_iota(jnp.int32, q.shape, 1)
    for h in range(RET_HEADS):
        in_head = (lane >= h * RET_QK) & (lane < (h + 1) * RET_QK)
        qh = jnp.where(in_head, q, 0.0)
        vh = v[:, h * RET_V:(h + 1) * RET_V]
        sc = jnp.dot(qh.astype(BF16), kt, preferred_element_type=F32) * dsum_ref[h]
        o = jnp.dot(sc.astype(BF16), vh, preferred_element_type=F32)
        o += jnp.dot((qh * qdf_ref[...]).astype(BF16), s_fb, preferred_element_type=F32)
        o += jnp.dot((qh * qdb_ref[...]).astype(BF16), s_bb, preferred_element_type=F32)
        y = o * lax.rsqrt(jnp.mean(o * o, axis=-1, keepdims=True) + EPS)
        gv = g_ref[:, h * RET_V:(h + 1) * RET_V].astype(F32)
        o_ref[:, h * RET_V:(h + 1) * RET_V] = (y * (gv * jax.nn.sigmoid(gv))).astype(BF16)
    s_new = _ret_state_update(kt, v, kd_ref[...], cd_ref[...], s_f)
    s_scr[...] = s_new
    send_ref[...] = s_new


def _ret_main_call(geom, proj, rkt, dsum, qdf, qdb, kd_f, cd_f, s0, sb_start):
    nb = geom.n_blocks
    hs = RET_HEADS * RET_QK
    hv = RET_HEADS * RET_V
    return pl.pallas_call(
        functools.partial(_ret_main_kernel, geom),
        grid=(nb,),
        in_specs=[pl.BlockSpec((BLK, hs), lambda g: (g, C_RQ // hs)),
                  pl.BlockSpec((hs, BLK), lambda g: (0, g)),
                  pl.BlockSpec((BLK, hv), lambda g: (g, C_RV // hv)),
                  pl.BlockSpec((BLK, hv), lambda g: (g, C_RG // hv)),
                  pl.BlockSpec((RET_HEADS, BLK, BLK), lambda g: (0, 0, 0)),
                  pl.BlockSpec((BLK, hs), lambda g: (0, 0)),
                  pl.BlockSpec((BLK, hs), lambda g: (0, 0)),
                  pl.BlockSpec((hs, BLK), lambda g: (0, 0)),
                  pl.BlockSpec((hs, RET_V), lambda g: (0, 0)),
                  pl.BlockSpec((None, None, hs, RET_V), lambda g: (geom.seq_id(g), 0, 0, 0)),
                  pl.BlockSpec((None, hs, RET_V), lambda g: (g, 0, 0))],
        out_specs=[pl.BlockSpec((BLK, hv), lambda g: (g, 0)),
                   pl.BlockSpec((None, hs, RET_V), lambda g: (g, 0, 0))],
        out_shape=[jax.ShapeDtypeStruct((geom.n_tok, hv), BF16),
                   jax.ShapeDtypeStruct((nb, hs, RET_V), F32)],
        scratch_shapes=[pltpu.VMEM((hs, RET_V), F32)],
        compiler_params=_cparams(("arbitrary",)),
        name="ret_main",
    )(proj, rkt, proj, proj, dsum, qdf, qdb, kd_f, cd_f, s0, sb_start)


def _ret_tables(ret_decay_l):
    log_g = jax.nn.log_sigmoid(ret_decay_l.astype(F32))
    pos = jnp.arange(BLK, dtype=F32)
    diff = pos[:, None] - pos[None, :]
    lf = log_g[0][:, None, None]
    lb = log_g[1][:, None, None]
    dsum = (jnp.where(diff >= 0, jnp.exp(jnp.maximum(diff, 0.0)[None] * lf), 0.0)
            + jnp.where(diff <= 0, jnp.exp(jnp.maximum(-diff, 0.0)[None] * lb), 0.0))

    def per_lane(e, lg):
        return jnp.repeat(jnp.exp(e[:, None] * lg[None, :]), RET_QK, axis=1)

    qdf = per_lane(pos + 1.0, log_g[0])
    qdb = per_lane(BLK - pos, log_g[1])
    kd_f = per_lane(BLK - 1.0 - pos, log_g[0]).T
    kd_b = per_lane(pos, log_g[1]).T
    cd_f = jnp.broadcast_to(jnp.repeat(jnp.exp(BLK * log_g[0]), RET_QK)[:, None], (RET_HEADS * RET_QK, RET_V))
    cd_b = jnp.broadcast_to(jnp.repeat(jnp.exp(BLK * log_g[1]), RET_QK)[:, None], (RET_HEADS * RET_QK, RET_V))
    return dsum, qdf, qdb, kd_f, kd_b, cd_f, cd_b


def _merge_kernel(n_ctx_tiles, ba_ref, bbc_ref, bbl_ref, bc_ref, g0_ref, g1_ref, g2_ref, xc_ref, xl_ref,
                  gate_ref, sc_ref, sh_ref, nw_ref, wb_ref, wo_ref, rhi_ref, rlo_ref, x1_ref, h2_ref, h2p_ref,
                  lt_ref):
    branches = (ba_ref[...], _pick_part(n_ctx_tiles, bbc_ref, bbl_ref), bc_ref[...])
    acc = None
    for br, (b, g_ref) in enumerate(zip(branches, (g0_ref, g1_ref, g2_ref))):
        p = jnp.dot(b, wb_ref[br], preferred_element_type=F32)
        t = (0.5 * jnp.tanh(0.5 * g_ref[...].astype(F32)) + 0.5) * p
        acc = t if acc is None else acc + t
    m = jnp.dot(acc.astype(BF16), wo_ref[...], preferred_element_type=F32)
    x1 = _pick_part(n_ctx_tiles, xc_ref, xl_ref) + gate_ref[...] * m
    x1_ref[...] = x1
    ms = jnp.mean(x1 * x1, axis=-1, keepdims=True)
    h2 = x1 * lax.rsqrt(ms + EPS) * nw_ref[...] * (1.0 + sc_ref[...]) + sh_ref[...]
    h2b = h2.astype(BF16)
    h2_ref[...] = h2b
    h2p_ref[...] = _pack_halves(h2b.astype(F32))
    h2lo = (h2 - h2b.astype(F32)).astype(BF16)
    nt = (((1,), (1,)), ((), ()))
    lt_ref[...] = (lax.dot_general(rhi_ref[...], h2b, nt, preferred_element_type=F32)
                   + lax.dot_general(rhi_ref[...], h2lo, nt, preferred_element_type=F32)
                   + lax.dot_general(rlo_ref[...], h2b, nt, preferred_element_type=F32))


def _merge_call(geom, l, ba, bb_ctx, bb_lat, bc, proj, x_ctx, x_lat, mod6, norm2_w, wb_bf, wo_bf, r_hi, r_lo):
    tm = 512
    gcol = C_GL // D_MODEL
    full = lambda shape: pl.BlockSpec(shape, lambda i: tuple(0 for _ in shape))
    tok = lambda w: pl.BlockSpec((tm, w), lambda i: (i, 0))
    return pl.pallas_call(
        functools.partial(_merge_kernel, geom.n_ctx // tm),
        grid=(geom.n_tok // tm,),
        in_specs=[tok(BRANCH_W)] + _split_in_specs(geom, tm, BRANCH_W, 1) + [tok(BRANCH_W),
                  pl.BlockSpec((tm, D_MODEL), lambda i: (i, gcol)),
                  pl.BlockSpec((tm, D_MODEL), lambda i: (i, gcol + 1)),
                  pl.BlockSpec((tm, D_MODEL), lambda i: (i, gcol + 2))]
                 + _split_in_specs(geom, tm, D_MODEL, 1) + [
                  _mod_spec(geom, l, 2, tm, 1), _mod_spec(geom, l, 4, tm, 1), _mod_spec(geom, l, 3, tm, 1),
                  full((1, D_MODEL)),
                  full((N_BRANCH, BRANCH_W, D_MODEL)), full((D_MODEL, D_MODEL)),
                  full((N_EXPERTS, D_MODEL)), full((N_EXPERTS, D_MODEL))],
        out_specs=[tok(D_MODEL), tok(D_MODEL), tok(D_MODEL // 2), pl.BlockSpec((N_EXPERTS, tm), lambda i: (0, i))],
        out_shape=[jax.ShapeDtypeStruct((geom.n_tok, D_MODEL), F32),
                   jax.ShapeDtypeStruct((geom.n_tok, D_MODEL), BF16),
                   jax.ShapeDtypeStruct((geom.n_tok, D_MODEL // 2), jnp.uint32),
                   jax.ShapeDtypeStruct((N_EXPERTS, geom.n_tok), F32)],
        compiler_params=_cparams(("arbitrary",)),
        name="merge_out",
    )(ba, bb_ctx, bb_lat, bc, proj, proj, proj, x_ctx, x_lat, mod6, mod6, mod6,
      norm2_w.reshape(1, D_MODEL), wb_bf, wo_bf, r_hi, r_lo)


def _router_kernel(lt_ref, bias_ref, ltri_ref, utri_ref, g_ref, slot_ref, cnt_ref, cnt_scr):
    per = N_EXPERTS // N_GROUPS
    tm = lt_ref.shape[1]
    scores = jax.nn.sigmoid(lt_ref[...])
    biased = scores + bias_ref[...]
    b3 = biased.reshape(N_GROUPS, per, tm)
    neg = jnp.float32(-jnp.inf)
    m1 = jnp.max(b3, axis=1, keepdims=True)
    is_m1 = b3 == m1
    cnt = jnp.sum(is_m1.astype(F32), axis=1, keepdims=True)
    m2 = jnp.max(jnp.where(is_m1, neg, b3), axis=1, keepdims=True)
    grp = (m1 + jnp.where(cnt >= 2.0, m1, m2)).reshape(N_GROUPS, tm)
    gidx = lax.broadcasted_iota(jnp.int32, (N_GROUPS, tm), 0)
    grank = jnp.zeros((N_GROUPS, tm), F32)
    for g2 in range(N_GROUPS):
        other = grp[g2:g2 + 1, :]
        ahead = (other > grp) | ((other == grp) & (gidx > g2))
        grank += ahead.astype(F32)
    gsel = (grank < float(TOPK_GROUPS)).astype(F32)
    emask = jnp.broadcast_to(gsel.reshape(N_GROUPS, 1, tm), (N_GROUPS, per, tm)).reshape(N_EXPERTS, tm)
    masked = jnp.where(emask > 0.0, biased, neg)
    eidx = lax.broadcasted_iota(jnp.int32, (N_EXPERTS, tm), 0)
    erank = jnp.zeros((N_EXPERTS, tm), F32)
    for e2 in range(N_EXPERTS):
        other = masked[e2:e2 + 1, :]
        ahead = (other > masked) | ((other == masked) & (eidx > e2))
        erank += ahead.astype(F32)
    sel = erank < float(TOP_K)
    w = jnp.where(sel, scores, 0.0)
    gates_t = w / jnp.sum(w, axis=0, keepdims=True) * ROUTED_SCALE

    @pl.when(pl.program_id(0) == 0)
    def _():
        cnt_scr[...] = jnp.zeros_like(cnt_scr)

    selb = sel.astype(BF16)
    slot = jnp.dot(ltri_ref[...], selb, preferred_element_type=F32)
    carry = cnt_scr[:, 0:1]
    rank = jnp.dot(selb, utri_ref[...], preferred_element_type=F32) + carry
    cnt_new = cnt_scr[...] + jnp.sum(sel.astype(F32), axis=1, keepdims=True)
    cnt_scr[...] = cnt_new
    cnt_ref[...] = cnt_new
    eid_f = eidx.astype(F32)
    g_rows, e_rows, r_rows = [], [], []
    for k in range(TOP_K):
        mk = jnp.where(sel & (slot == float(k)), 1.0, 0.0)
        g_rows.append(jnp.sum(mk * gates_t, axis=0, keepdims=True))
        e_rows.append(jnp.sum(mk * eid_f, axis=0, keepdims=True))
        r_rows.append(jnp.sum(mk * rank, axis=0, keepdims=True))
    slot_ref[...] = jnp.concatenate(e_rows + r_rows, axis=0).astype(jnp.int32)
    pad = jnp.zeros((GATE_W - TOP_K, tm), F32)
    g_ref[...] = jnp.concatenate(g_rows + [pad], axis=0).T


ROUTER_TM = 512


def _router_call(geom, logits_t, bias):
    tm = ROUTER_TM
    ltri = jnp.asarray(np.tril(np.ones((N_EXPERTS, N_EXPERTS), np.float32), -1), BF16)
    utri = jnp.asarray(np.triu(np.ones((tm, tm), np.float32), 1), BF16)
    return pl.pallas_call(
        _router_kernel,
        grid=(geom.n_tok // tm,),
        in_specs=[pl.BlockSpec((N_EXPERTS, tm), lambda i: (0, i)),
                  pl.BlockSpec((N_EXPERTS, 1), lambda i: (0, 0)),
                  pl.BlockSpec((N_EXPERTS, N_EXPERTS), lambda i: (0, 0)),
                  pl.BlockSpec((tm, tm), lambda i: (0, 0))],
        out_specs=[pl.BlockSpec((tm, GATE_W), lambda i: (i, 0)),
                   pl.BlockSpec((2 * TOP_K, tm), lambda i: (0, i)),
                   pl.BlockSpec((N_EXPERTS, GATE_W), lambda i: (0, 0))],
        out_shape=[jax.ShapeDtypeStruct((geom.n_tok, GATE_W), F32),
                   jax.ShapeDtypeStruct((2 * TOP_K, geom.n_tok), jnp.int32),
                   jax.ShapeDtypeStruct((N_EXPERTS, GATE_W), F32)],
        scratch_shapes=[pltpu.VMEM((N_EXPERTS, GATE_W), F32)],
        compiler_params=_cparams(("arbitrary",)),
        name="router",
    )(logits_t, bias.reshape(N_EXPERTS, 1), ltri, utri)


MOE_TR = 512
SC_CORES = 2
SC_SUBCORES = 16
SC_CHUNK = 64


def _sc_worker_base(rows_per_worker):
    wid = lax.axis_index("s") * SC_CORES + lax.axis_index("c")
    return wid * rows_per_worker


def _sc_scatter_rows(table, pos_flat, n_slots, n_rows_out):
    n, d = table.shape
    nw = SC_CORES * SC_SUBCORES
    assert n % (nw * SC_CHUNK) == 0
    per_w = n // nw
    mesh = plsc.VectorSubcoreMesh(core_axis_name="c", subcore_axis_name="s")

    @functools.partial(
        pl.kernel, mesh=mesh,
        out_type=jax.ShapeDtypeStruct((n_rows_out, d), table.dtype),
        scratch_types=[[pltpu.VMEM((SC_CHUNK,), jnp.int32) for _ in range(n_slots)],
                       pltpu.VMEM((SC_CHUNK, d), table.dtype),
                       pltpu.SemaphoreType.DMA],
    )
    def scatter(table_hbm, pos_hbm, out_hbm, idx_v, rows_v, sem):
        base = _sc_worker_base(per_w)

        @pl.loop(0, per_w // SC_CHUNK)
        def _(ci):
            off = pl.multiple_of(base + ci * SC_CHUNK, 8)
            for k in range(n_slots):
                pltpu.sync_copy(pos_hbm.at[pl.ds(pl.multiple_of(k * n + off, 8), SC_CHUNK)], idx_v[k])
            pltpu.sync_copy(table_hbm.at[pl.ds(off, SC_CHUNK)], rows_v)
            copies = [pltpu.make_async_copy(rows_v, out_hbm.at[idx_v[k]], sem) for k in range(n_slots)]
            for cp in copies:
                cp.start()
            for cp in copies:
                cp.wait()

    return scatter(table, pos_flat)


def _sc_gather_rows(table, idx):
    b = idx.shape[0]
    d = table.shape[1]
    nw = SC_CORES * SC_SUBCORES
    nbuf = 2
    assert b % (nw * SC_CHUNK * nbuf) == 0
    per_w = b // nw
    n_chunks = per_w // SC_CHUNK
    mesh = plsc.VectorSubcoreMesh(core_axis_name="c", subcore_axis_name="s")

    @functools.partial(
        pl.kernel, mesh=mesh,
        out_type=jax.ShapeDtypeStruct((b, d), table.dtype),
        scratch_types=[pltpu.VMEM((per_w,), jnp.int32),
                       [pltpu.VMEM((SC_CHUNK, d), table.dtype) for _ in range(nbuf)],
                       [pltpu.SemaphoreType.DMA for _ in range(nbuf)],
                       [pltpu.SemaphoreType.DMA for _ in range(nbuf)]],
    )
    def gather(table_hbm, idx_hbm, out_hbm, idx_v, rows, gsem, wsem):
        base = _sc_worker_base(per_w)
        pltpu.sync_copy(idx_hbm.at[pl.ds(pl.multiple_of(base, 8), per_w)], idx_v)

        def fetch(ci, slot):
            src = table_hbm.at[idx_v.at[pl.ds(pl.multiple_of(ci * SC_CHUNK, 8), SC_CHUNK)]]
            return pltpu.make_async_copy(src, rows[slot], gsem[slot])

        def put(ci, slot):
            dst = out_hbm.at[pl.ds(pl.multiple_of(base + ci * SC_CHUNK, 8), SC_CHUNK)]
            return pltpu.make_async_copy(rows[slot], dst, wsem[slot])

        for slot in range(nbuf):
            fetch(slot, slot).start()

        @pl.loop(0, n_chunks, step=nbuf)
        def _(c0):
            for slot in range(nbuf):
                ci = c0 + slot
                fetch(ci, slot).wait()
                put(ci, slot).start()
                put(ci, slot).wait()

                @pl.when(ci + nbuf < n_chunks)
                def _():
                    fetch(ci + nbuf, slot).start()

    return gather(table, idx)


def _route_positions(n_tok, slots, counts):
    cnt = counts[:, 0].astype(jnp.int32)
    cnt_pad = ((cnt + MOE_TR - 1) // MOE_TR) * MOE_TR
    off_end = jnp.cumsum(cnt_pad)
    off = off_end - cnt_pad
    eid, rank = slots[:TOP_K], slots[TOP_K:]
    eids = jnp.arange(N_EXPERTS, dtype=jnp.int32)
    pos = jnp.sum(jnp.where(eid[..., None] == eids, off, 0), axis=-1) + rank
    n_tiles = (TOP_K * n_tok) // MOE_TR + N_EXPERTS
    tile_start = jnp.arange(n_tiles, dtype=jnp.int32) * MOE_TR
    tile_expert = jnp.sum((tile_start[:, None] >= off_end[None, :]).astype(jnp.int32), axis=1)
    tile_expert = jnp.minimum(tile_expert, N_EXPERTS - 1)
    n_used = (off_end[-1] // MOE_TR).reshape(1)
    return pos, tile_expert, n_used, n_tiles


def _expert_ffn(x_lo, x_hi, gu, dn):
    half = D_MODEL // 2
    a = (jnp.dot(x_lo, gu[0:half, :], preferred_element_type=F32)
         + jnp.dot(x_hi, gu[half:, :], preferred_element_type=F32))
    hg = a[:, :D_EXPERT]
    act = (hg * jax.nn.sigmoid(hg)) * a[:, D_EXPERT:]
    return jnp.dot(act.astype(BF16), dn, preferred_element_type=F32)


def _experts_kernel(te_ref, nu_ref, x_ref, gu_ref, dn_ref, y_ref):
    i = pl.program_id(0)

    @pl.when(i < nu_ref[0])
    def _():
        lo, hi = _unpack_halves(x_ref[...])
        y = _expert_ffn(lo.astype(BF16), hi.astype(BF16), gu_ref[...].astype(BF16), dn_ref[...].astype(BF16))
        y_ref[...] = _pack_halves(y.astype(BF16).astype(F32))

    @pl.when(i >= nu_ref[0])
    def _():
        y_ref[...] = jnp.zeros_like(y_ref)


def _experts_call(l, xs, tile_expert, n_used, n_tiles, w_gu, w_dn):
    half = D_MODEL // 2
    grid_spec = pltpu.PrefetchScalarGridSpec(
        num_scalar_prefetch=2,
        grid=(n_tiles,),
        in_specs=[pl.BlockSpec((MOE_TR, half), lambda i, te, nu: (jnp.minimum(i, jnp.maximum(nu[0], 1) - 1), 0)),
                  pl.BlockSpec((None, None, D_MODEL, 2 * D_EXPERT), lambda i, te, nu: (l, te[i], 0, 0)),
                  pl.BlockSpec((None, None, D_EXPERT, D_MODEL), lambda i, te, nu: (l, te[i], 0, 0))],
        out_specs=pl.BlockSpec((MOE_TR, half), lambda i, te, nu: (i, 0)),
    )
    return pl.pallas_call(
        _experts_kernel,
        grid_spec=grid_spec,
        out_shape=jax.ShapeDtypeStruct((n_tiles * MOE_TR, half), jnp.uint32),
        compiler_params=_cparams(("arbitrary",)),
        name="moe_experts",
    )(tile_expert, n_used, xs, w_gu, w_dn)


MOE_OUT_PARTS = 2


def _moe_out_kernel(n_ctx_tiles, tile0, first, *refs):
    if first:
        yt_ref, g_ref, h_ref, sgu_ref, sdn_ref, x1_ref, gate_ref, oc_ref, ol_ref = refs
    else:
        yt_ref, g_ref, h_ref, sgu_ref, sdn_ref, x1_ref, gate_ref, _, ol_ref = refs
    i = pl.program_id(0) + tile0
    gts = g_ref[...]
    lane = lax.broadcasted_iota(jnp.int32, gts.shape, 1)
    acc_lo, acc_hi = None, None
    for k in range(TOP_K):
        ge = jnp.sum(jnp.where(lane == k, gts, 0.0), axis=1, keepdims=True)
        lo, hi = _unpack_halves(yt_ref[k])
        acc_lo = ge * lo if acc_lo is None else acc_lo + ge * lo
        acc_hi = ge * hi if acc_hi is None else acc_hi + ge * hi
    routed = jnp.concatenate([acc_lo, acc_hi], axis=1)
    h = h_ref[...]
    half = D_MODEL // 2
    shared = _expert_ffn(h[:, :half], h[:, half:], sgu_ref[...], sdn_ref[...])
    y = x1_ref[...] + gate_ref[...] * (routed + shared)
    if first:
        @pl.when(i < n_ctx_tiles)
        def _():
            oc_ref[...] = y

        @pl.when(i >= n_ctx_tiles)
        def _():
            ol_ref[...] = y
    else:
        ol_ref[...] = y


def _moe_out_call(geom, l, part, yt, gates, h2, sgu_bf, sdn_bf, x1, mod6, prev_lat=None):
    tm = 512
    nct = geom.n_ctx // tm
    n_part = geom.n_tok // tm // MOE_OUT_PARTS
    t0 = part * n_part
    first = part == 0
    assert nct <= n_part
    half = D_MODEL // 2
    tok = lambda w: pl.BlockSpec((tm, w), lambda i: (i + t0, 0))
    in_specs = [pl.BlockSpec((TOP_K, tm, half), lambda i: (0, i, 0)),
                tok(GATE_W), tok(D_MODEL),
                pl.BlockSpec((None, D_MODEL, 2 * D_EXPERT), lambda i: (l, 0, 0)),
                pl.BlockSpec((None, D_EXPERT, D_MODEL), lambda i: (l, 0, 0)),
                tok(D_MODEL),
                pl.BlockSpec((None, None, None, 1, D_MODEL), lambda i: (l, geom.mod_row(i + t0, tm), 5, 0, 0))]
    args = [yt, gates, h2, sgu_bf, sdn_bf, x1, mod6]
    lat_shape = jax.ShapeDtypeStruct((geom.n_lat, D_MODEL), F32)
    if first:
        out_specs = [pl.BlockSpec((tm, D_MODEL), lambda i: (jnp.minimum(i, nct - 1), 0)),
                     pl.BlockSpec((tm, D_MODEL), lambda i: (jnp.maximum(i - nct, 0), 0))]
        out_shape = [jax.ShapeDtypeStruct((geom.n_ctx, D_MODEL), F32), lat_shape]
        aliases = {}
    else:
        in_specs.append(pl.BlockSpec(memory_space=pl.ANY))
        args.append(prev_lat)
        out_specs = [pl.BlockSpec((tm, D_MODEL), lambda i: (i + t0 - nct, 0))]
        out_shape = [lat_shape]
        aliases = {len(args) - 1: 0}
    return pl.pallas_call(
        functools.partial(_moe_out_kernel, nct, t0, first),
        grid=(n_part,),
        in_specs=in_specs, out_specs=out_specs, out_shape=out_shape,
        input_output_aliases=aliases,
        compiler_params=_cparams(("arbitrary",)),
        name="moe_out",
    )(*args)


def _moe(geom, l, h2, h2p, gates, slots, counts, w_gu, w_dn, sgu_bf, sdn_bf, x1, mod6):
    pos, tile_expert, n_used, n_tiles = _route_positions(geom.n_tok, slots, counts)
    xs = _sc_scatter_rows(h2p, pos.reshape(-1), TOP_K, n_tiles * MOE_TR)
    ys = _experts_call(l, xs, tile_expert, n_used, n_tiles, w_gu, w_dn)
    n_part = geom.n_tok // MOE_OUT_PARTS
    y_ctx, y_lat = None, None
    for part in range(MOE_OUT_PARTS):
        pos_p = pos[:, part * n_part:(part + 1) * n_part].reshape(-1)
        yt = _sc_gather_rows(ys, pos_p).reshape(TOP_K, n_part, D_MODEL // 2)
        outs = _moe_out_call(geom, l, part, yt, gates, h2, sgu_bf, sdn_bf, x1, mod6, y_lat)
        if part == 0:
            y_ctx, y_lat = outs
        else:
            (y_lat,) = outs
    return y_ctx, y_lat


def _rope_tables(dec_seq):
    rows = dec_seq // GRID_W
    row = jnp.repeat(jnp.arange(rows, dtype=F32), GRID_W)
    col = jnp.tile(jnp.arange(GRID_W, dtype=F32), rows)
    inv = ROPE_BASE ** (-jnp.arange(ROPE_PAIRS, dtype=F32) / ROPE_PAIRS)
    ar = row[:, None] * inv[None, :]
    ac = col[:, None] * inv[None, :]
    cos64 = jnp.concatenate([jnp.cos(ar), jnp.cos(ar), jnp.cos(ac), jnp.cos(ac)], axis=1)
    sin64 = jnp.concatenate([-jnp.sin(ar), jnp.sin(ar), -jnp.sin(ac), jnp.sin(ac)], axis=1)
    return jnp.tile(cos64, (1, 2)), jnp.tile(sin64, (1, 2))


def _block_diag_gate(wg_dir):
    eye = jnp.eye(LRU_BLOCKS, dtype=F32)
    dense = jnp.einsum('gnij,nm->gnimj', wg_dir.astype(F32), eye).reshape(2, D_RNN, D_RNN)
    return jnp.concatenate([dense[0], dense[1]], axis=1)


def kernel(x_prompt, x_sample, cache_k, cache_v, state_lru, state_ret, c, c_ctx, ada_w, ada_b, norm1_w, norm2_w, w_in, conv_w, conv_b, lru_gate_w, lru_gate_b, lru_lambda, q_norm_w, k_norm_w, diff_lambda, subln_w, ret_decay, w_branch, w_out, router_w, router_bias, w_exp_gu, w_exp_down, w_sh_gu, w_sh_down):
    batch, seq, _ = x_prompt.shape
    dec_batch, dec_seq, _ = x_sample.shape
    assert 1 + dec_batch <= MOD_ROWS
    geom = _Geom(batch, seq, dec_batch, dec_seq)
    hs = RET_HEADS * RET_QK
    aw = DA_HEADS * 2 * DA_QK

    x_ctx = x_prompt.reshape(geom.n_ctx, D_MODEL)
    x_lat = x_sample.reshape(geom.n_lat, D_MODEL)
    cvec = jnp.zeros((MOD_ROWS, D_MODEL), F32).at[0].set(c_ctx).at[1:1 + dec_batch].set(c)
    mod6 = _ada_call(cvec, ada_w, ada_b).reshape(DEPTH, MOD_ROWS, 6, 1, D_MODEL)

    ones_bd = jnp.kron(jnp.eye(aw // DA_QK, dtype=F32), jnp.ones((DA_QK, DA_QK), F32)).astype(BF16)
    cos_t, sin_t = _rope_tables(dec_seq)

    w_in_bf = w_in.astype(BF16)
    sgu_bf, sdn_bf = w_sh_gu.astype(BF16), w_sh_down.astype(BF16)

    ks, vs, lrus, rets = [], [], [], []
    for l in range(DEPTH):
        lam_init = 0.8 - 0.6 * math.exp(-0.3 * l)
        w_rkt_bf = w_in[l][:, C_RK:C_RK + hs].T.astype(BF16)
        proj, rkt = _inproj_call(geom, l, x_ctx, x_lat, mod6, norm1_w[l], w_in_bf, w_rkt_bf)

        sp = jax.nn.softplus(-lru_lambda[l].astype(F32))
        h0 = jnp.concatenate([jnp.zeros((batch, 2, D_RNN), F32), state_lru[:, l].astype(F32)], axis=0)
        h0 = h0.reshape(geom.n_seq, 2, 1, D_RNN)
        cb = conv_b[l].reshape(1, D_RNN)
        lru_args = []
        for d in range(2):
            lru_args.append((_block_diag_gate(lru_gate_w[l, d]).astype(BF16),
                             lru_gate_b[l, d].reshape(1, 2 * D_RNN), sp[d].reshape(1, D_RNN)))
        hf, hf_last = _lru_call(geom, False, proj, conv_w[l], cb, *lru_args[0], h0)
        branch_a, hb_last = _lru_call(geom, True, proj, conv_w[l], cb, *lru_args[1], h0, hf)

        qw = jnp.tile(q_norm_w[l], aw // DA_QK).reshape(1, aw)
        kw = jnp.tile(k_norm_w[l], aw // DA_QK).reshape(1, aw)
        q_c, k_c, k_c32, v_c32 = _prep_call(geom, False, proj, qw, kw, ones_bd)
        q_l, k_l = _prep_call(geom, True, proj, qw, kw, ones_bd, cos_t, sin_t)
        lam_p = diff_lambda[l].astype(F32)
        lam = jnp.exp(jnp.sum(lam_p[0] * lam_p[1])) - jnp.exp(jnp.sum(lam_p[2] * lam_p[3])) + lam_init
        q_bound = DA_QK * jnp.max(jnp.square(q_norm_w[l].astype(F32))) * (DA_QK ** -0.5 * LOG2E) ** 2
        k_bound = DA_QK * jnp.max(jnp.square(k_norm_w[l].astype(F32)))
        kc32 = cache_k[:, l].astype(F32)
        kc_bound = jnp.maximum(k_bound, jnp.max(jnp.sum(jnp.square(kc32), axis=-1)))

        def attn_par(kb):
            ok = (q_bound * kb * 1.05 < ATT_SAFE_LOGIT ** 2).astype(F32)
            return jnp.stack([lam, ok])

        assert geom.n_ctx % dec_seq == 0
        cache = (kc32.reshape(dec_batch, -1, aw).astype(BF16),
                 cache_v[:, l].reshape(dec_batch, -1, DA_HEADS * DA_V).astype(BF16))
        att_c = _attn_call(attn_par(k_bound), lam_init, q_c, k_c, proj, 0, batch, seq, seq, 256, subln_w[l])
        att_l = _attn_call(attn_par(kc_bound), lam_init, q_l, k_l, proj, geom.n_ctx // dec_seq, dec_batch,
                           dec_seq, dec_seq, 4 * ATT_TQ, subln_w[l], cache)

        dsum, qdf, qdb, kd_f, kd_b, cd_f, cd_b = _ret_tables(ret_decay[l])
        s0 = jnp.concatenate([jnp.zeros((batch, 2, hs, RET_V), F32),
                              state_ret[:, l].astype(F32).reshape(dec_batch, 2, hs, RET_V)], axis=0)
        sb_start, sb_end = _ret_bwd_call(geom, proj, rkt, kd_b, cd_b, s0)
        branch_c, sf_end = _ret_main_call(geom, proj, rkt, dsum, qdf, qdb, kd_f, cd_f, s0, sb_start)

        r_t = router_w[l].T.astype(F32)
        r_hi = r_t.astype(BF16)
        r_lo = (r_t - r_hi.astype(F32)).astype(BF16)
        x1, h2, h2p, logits_t = _merge_call(geom, l, branch_a, att_c, att_l, branch_c, proj, x_ctx, x_lat, mod6,
                                            norm2_w[l], w_branch[l].astype(BF16), w_out[l].astype(BF16), r_hi, r_lo)
        gates, slots, counts = _router_call(geom, logits_t, router_bias[l].astype(F32))
        x_ctx, x_lat = _moe(geom, l, h2, h2p, gates, slots, counts, w_exp_gu, w_exp_down, sgu_bf, sdn_bf, x1, mod6)

        ks.append(k_c32.reshape(batch, seq, DA_HEADS, 2, DA_QK))
        vs.append(v_c32.reshape(batch, seq, DA_HEADS, DA_V))
        lrus.append(jnp.stack([hf_last[:batch, 0], hb_last[:batch, 0]], axis=1))
        rets.append(jnp.stack([sf_end[:batch].reshape(batch, RET_HEADS, RET_QK, RET_V),
                               sb_end[:batch].reshape(batch, RET_HEADS, RET_QK, RET_V)], axis=1))

    y_prompt = x_ctx.reshape(batch, seq, D_MODEL)
    y_sample = x_lat.reshape(dec_batch, dec_seq, D_MODEL)
    return (y_prompt, y_sample, jnp.stack(ks, axis=1), jnp.stack(vs, axis=1),
            jnp.stack(lrus, axis=1), jnp.stack(rets, axis=1))
```

```python
import functools
import math

import numpy as np
import jax
import jax.numpy as jnp
from jax import lax
from jax.experimental import pallas as pl
from jax.experimental.pallas import tpu as pltpu
from jax.experimental.pallas import tpu_sc as plsc

F32 = jnp.float32
BF16 = jnp.bfloat16

D_MODEL = 1024
DEPTH = 2
GRID_W = 64
D_RNN = 512
LRU_BLOCKS = 8
LRU_BLOCK = D_RNN // LRU_BLOCKS
CONV_W = 4
LRU_C = 8.0
DA_HEADS = 4
DA_QK = 64
DA_V = 128
ROPE_PAIRS = DA_QK // 4
ROPE_BASE = 10000.0
RET_HEADS = 4
RET_QK = 64
RET_V = 128
BRANCH_W = 512
N_BRANCH = 3
D_IN = 7168
N_EXPERTS = 64
TOP_K = 8
N_GROUPS = 8
TOPK_GROUPS = 4
D_EXPERT = 256
ROUTED_SCALE = 2.5
EPS = 1e-6

C_XA, C_GA, C_DQ, C_DK, C_DV = 0, 512, 1024, 1536, 2048
C_RQ, C_RK, C_RV, C_RG, C_GL = 2560, 2816, 3072, 3584, 4096

BLK = 256
LRU_SUB = 8
GATE_W = 128
MOD_ROWS = 8
VMEM_LIMIT = 56 * 1024 * 1024


def _cparams(sem, vmem_limit=VMEM_LIMIT):
    return pltpu.CompilerParams(dimension_semantics=sem, vmem_limit_bytes=vmem_limit)


class _Geom:
    def __init__(self, batch, seq, dec_batch, dec_seq):
        assert seq == BLK and dec_seq % BLK == 0
        self.batch, self.seq, self.dec_batch, self.dec_seq = batch, seq, dec_batch, dec_seq
        self.n_ctx = batch * seq
        self.n_lat = dec_batch * dec_seq
        self.n_tok = self.n_ctx + self.n_lat
        self.ctx_blocks = self.n_ctx // BLK
        self.lat_blocks = dec_seq // BLK
        self.n_blocks = self.n_tok // BLK
        self.n_seq = batch + dec_batch

    def mod_row(self, i, tile):
        nct = self.n_ctx // tile
        per = self.dec_seq // tile
        return jnp.where(i < nct, 0, 1 + (i - nct) // per)

    def seq_id(self, i):
        return jnp.where(i < self.ctx_blocks, i, self.ctx_blocks + (i - self.ctx_blocks) // self.lat_blocks)

    def seq_start(self, i):
        return jnp.logical_or(i < self.ctx_blocks, (i - self.ctx_blocks) % self.lat_blocks == 0)

    def seq_end(self, i):
        return jnp.logical_or(i < self.ctx_blocks, (i - self.ctx_blocks) % self.lat_blocks == self.lat_blocks - 1)


def _ada_kernel(c_ref, w_ref, b_ref, o_ref):
    cv = c_ref[...]
    s = cv * jax.nn.sigmoid(cv)
    o_ref[...] = jnp.dot(s, w_ref[...], preferred_element_type=F32,
                         precision=lax.Precision.HIGHEST) + b_ref[...]


def _ada_call(cvec, ada_w, ada_b):
    depth = ada_w.shape[0]
    nt = 6
    return pl.pallas_call(
        _ada_kernel,
        grid=(depth, nt),
        in_specs=[pl.BlockSpec((MOD_ROWS, D_MODEL), lambda l, j: (0, 0)),
                  pl.BlockSpec((None, D_MODEL, D_MODEL), lambda l, j: (l, 0, j)),
                  pl.BlockSpec((None, 1, D_MODEL), lambda l, j: (l, 0, j))],
        out_specs=pl.BlockSpec((None, MOD_ROWS, D_MODEL), lambda l, j: (l, 0, j)),
        out_shape=jax.ShapeDtypeStruct((depth, MOD_ROWS, 6 * D_MODEL), F32),
        compiler_params=_cparams(("arbitrary", "arbitrary")),
        name="ada_mod",
    )(cvec, ada_w, ada_b.reshape(depth, 1, 6 * D_MODEL))


def _mod_spec(geom, l, which, tile, ngrid):
    if ngrid == 1:
        return pl.BlockSpec((None, None, None, 1, D_MODEL),
                            lambda i: (l, geom.mod_row(i, tile), which, 0, 0))
    return pl.BlockSpec((None, None, None, 1, D_MODEL),
                        lambda i, j: (l, geom.mod_row(i, tile), which, 0, 0))


def _split_in_specs(geom, tile, width, ngrid):
    nct = geom.n_ctx // tile
    if ngrid == 1:
        return [pl.BlockSpec((tile, width), lambda i: (jnp.minimum(i, nct - 1), 0)),
                pl.BlockSpec((tile, width), lambda i: (jnp.maximum(i - nct, 0), 0))]
    return [pl.BlockSpec((tile, width), lambda i, j: (jnp.minimum(i, nct - 1), 0)),
            pl.BlockSpec((tile, width), lambda i, j: (jnp.maximum(i - nct, 0), 0))]


def _pick_part(n_ctx_tiles, c_ref, l_ref):
    return jnp.where(pl.program_id(0) < n_ctx_tiles, c_ref[...], l_ref[...])


def _pack_halves(y):
    w = y.shape[1] // 2
    bits = pltpu.bitcast(y, jnp.uint32)
    return (bits[:, :w] >> 16) | (bits[:, w:] & jnp.uint32(0xFFFF0000))


def _unpack_halves(p):
    return pltpu.bitcast(p << 16, F32), pltpu.bitcast(p & jnp.uint32(0xFFFF0000), F32)


INPROJ_TM = 512
INPROJ_TN = 1024


def _inproj_kernel(n_ctx_tiles, xc_ref, xl_ref, sc_ref, sh_ref, nw_ref, w_ref, wkt_ref, o_ref, kt_ref):
    x = _pick_part(n_ctx_tiles, xc_ref, xl_ref)
    ms = jnp.mean(x * x, axis=-1, keepdims=True)
    y = x * lax.rsqrt(ms + EPS) * nw_ref[...]
    hb = (y * (1.0 + sc_ref[...]) + sh_ref[...]).astype(BF16)
    kt_ref[...] = lax.dot_general(wkt_ref[...], hb, (((1,), (1,)), ((), ())),
                                  preferred_element_type=F32).astype(BF16)
    for j in range(D_IN // INPROJ_TN):
        cols = slice(j * INPROJ_TN, (j + 1) * INPROJ_TN)
        o_ref[:, cols] = jnp.dot(hb, w_ref[:, cols], preferred_element_type=F32).astype(BF16)


def _inproj_call(geom, l, x_ctx, x_lat, mod6, norm_w, w_in_bf, w_rkt_bf):
    tm = INPROJ_TM
    return pl.pallas_call(
        functools.partial(_inproj_kernel, geom.n_ctx // tm),
        grid=(geom.n_tok // tm,),
        in_specs=_split_in_specs(geom, tm, D_MODEL, 1) + [
                  _mod_spec(geom, l, 1, tm, 1),
                  _mod_spec(geom, l, 0, tm, 1),
                  pl.BlockSpec((1, D_MODEL), lambda i: (0, 0)),
                  pl.BlockSpec((None, D_MODEL, D_IN), lambda i: (l, 0, 0), pipeline_mode=pl.Buffered(1)),
                  pl.BlockSpec((RET_HEADS * RET_QK, D_MODEL), lambda i: (0, 0))],
        out_specs=[pl.BlockSpec((tm, D_IN), lambda i: (i, 0)),
                   pl.BlockSpec((RET_HEADS * RET_QK, tm), lambda i: (0, i))],
        out_shape=[jax.ShapeDtypeStruct((geom.n_tok, D_IN), BF16),
                   jax.ShapeDtypeStruct((RET_HEADS * RET_QK, geom.n_tok), BF16)],
        compiler_params=_cparams(("arbitrary",)),
        name="inproj",
    )(x_ctx, x_lat, mod6, mod6, norm_w.reshape(1, D_MODEL), w_in_bf, w_rkt_bf)


def _gelu_tanh(x):
    return 0.5 * x * (1.0 + jnp.tanh(math.sqrt(2.0 / math.pi) * (x + 0.044715 * (x * x * x))))


def _lru_kernel(geom, reverse, *refs):
    if reverse:
        (xa_ref, xp_ref, xn_ref, cw_ref, cb_ref, wg_ref, bg_ref, sp_ref, h0_ref, perm_ref, permt_ref,
         ga_ref, hf_ref, out_ref, hl_ref, c_scr) = refs
    else:
        (xa_ref, xp_ref, xn_ref, cw_ref, cb_ref, wg_ref, bg_ref, sp_ref, h0_ref, perm_ref,
         out_ref, hl_ref, c_scr) = refs
    g = pl.program_id(0)
    i = geom.n_blocks - 1 - g if reverse else g
    start = geom.seq_start(i)
    end = geom.seq_end(i)

    @pl.when(end if reverse else start)
    def _():
        c_scr[...] = h0_ref[...]

    sub_len = BLK // LRU_SUB
    perm = perm_ref[...]
    x = jnp.dot(perm, xa_ref[...], preferred_element_type=F32)
    pm = jnp.where(start, 0.0, 1.0)
    nm = jnp.where(end, 0.0, 1.0)
    hp = xp_ref.shape[0]
    p1 = xp_ref[hp - 1:hp, :].astype(F32) * pm
    p2 = xp_ref[hp - 2:hp - 1, :].astype(F32) * pm
    n0 = xn_ref[0:1, :].astype(F32) * nm
    row = lax.broadcasted_iota(jnp.int32, x.shape, 0)
    xm1 = jnp.where(row < LRU_SUB, pltpu.roll(x, LRU_SUB + 1, 0), pltpu.roll(x, LRU_SUB, 0))
    xm1 = jnp.where(row == 0, p1, xm1)
    xm2 = jnp.where(row < 2 * LRU_SUB, pltpu.roll(x, 2 * LRU_SUB + 1, 0), pltpu.roll(x, 2 * LRU_SUB, 0))
    xm2 = jnp.where(row == 0, p2, jnp.where(row == LRU_SUB, p1, xm2))
    xp1 = jnp.where(row >= BLK - LRU_SUB, pltpu.roll(x, BLK - LRU_SUB - 1, 0),
                    pltpu.roll(x, BLK - LRU_SUB, 0))
    xp1 = jnp.where(row == BLK - 1, n0, xp1)
    xc = (cw_ref[0:1, :] * xm2 + cw_ref[1:2, :] * xm1 + cw_ref[2:3, :] * x
          + cw_ref[3:4, :] * xp1 + cb_ref[...])

    gt = jnp.dot(xc.astype(BF16), wg_ref[...], preferred_element_type=F32) + bg_ref[...]
    r = jax.nn.sigmoid(gt[:, :D_RNN])
    ig = jax.nn.sigmoid(gt[:, D_RNN:])
    a = jnp.exp(-LRU_C * r * sp_ref[...])
    u = jnp.sqrt(1.0 - a * a) * ig * xc

    h = jnp.zeros((LRU_SUB, D_RNN), F32)
    p = jnp.ones((LRU_SUB, D_RNN), F32)
    h_loc = [None] * sub_len
    p_loc = [None] * sub_len
    for t in (range(sub_len - 1, -1, -1) if reverse else range(sub_len)):
        a_t = a[t * LRU_SUB:(t + 1) * LRU_SUB, :]
        h = a_t * h + u[t * LRU_SUB:(t + 1) * LRU_SUB, :]
        p = a_t * p
        h_loc[t] = h
        p_loc[t] = p
    h_in = [None] * LRU_SUB
    state = c_scr[...]
    for k in (range(LRU_SUB - 1, -1, -1) if reverse else range(LRU_SUB)):
        h_in[k] = state
        state = h[k:k + 1, :] + p[k:k + 1, :] * state
    c_scr[...] = state
    hl_ref[...] = state
    h_in = jnp.concatenate(h_in, axis=0)
    h_full = jnp.concatenate([h_loc[t] + p_loc[t] * h_in for t in range(sub_len)], axis=0)
    if reverse:
        gv = jnp.dot(perm, ga_ref[...], preferred_element_type=F32)
        y = (_gelu_tanh(gv) * (hf_ref[...] + h_full)).astype(BF16)
        out_ref[...] = jnp.dot(permt_ref[...], y, preferred_element_type=F32).astype(BF16)
    else:
        out_ref[...] = h_full


def _lru_call(geom, reverse, proj, conv_w, conv_b, wg, bg, sp, h0, hf=None):
    nb = geom.n_blocks
    halo = 16
    hpb = BLK // halo

    def blk(g):
        return nb - 1 - g if reverse else g

    d = 1 if reverse else 0
    in_specs = [
        pl.BlockSpec((BLK, D_RNN), lambda g: (blk(g), C_XA // D_RNN)),
        pl.BlockSpec((halo, D_RNN), lambda g: (jnp.maximum(blk(g) * hpb - 1, 0), C_XA // D_RNN)),
        pl.BlockSpec((halo, D_RNN), lambda g: (jnp.minimum((blk(g) + 1) * hpb, nb * hpb - 1), C_XA // D_RNN)),
        pl.BlockSpec((CONV_W, D_RNN), lambda g: (0, 0)),
        pl.BlockSpec((1, D_RNN), lambda g: (0, 0)),
        pl.BlockSpec((D_RNN, 2 * D_RNN), lambda g: (0, 0)),
        pl.BlockSpec((1, 2 * D_RNN), lambda g: (0, 0)),
        pl.BlockSpec((1, D_RNN), lambda g: (0, 0)),
        pl.BlockSpec((None, None, 1, D_RNN), lambda g: (geom.seq_id(blk(g)), d, 0, 0)),
    ]
    pos = np.arange(BLK)
    perm_np = np.zeros((BLK, BLK), np.float32)
    perm_np[pos, (pos % LRU_SUB) * (BLK // LRU_SUB) + pos // LRU_SUB] = 1.0
    in_specs.append(pl.BlockSpec((BLK, BLK), lambda g: (0, 0)))
    args = [proj, proj, proj, conv_w, conv_b, wg, bg, sp, h0, jnp.asarray(perm_np, BF16)]
    if reverse:
        in_specs += [pl.BlockSpec((BLK, BLK), lambda g: (0, 0)),
                     pl.BlockSpec((BLK, D_RNN), lambda g: (blk(g), C_GA // D_RNN)),
                     pl.BlockSpec((BLK, D_RNN), lambda g: (blk(g), 0))]
        args += [jnp.asarray(perm_np.T, BF16), proj, hf]
        out_dtype = BF16
    else:
        out_dtype = F32
    scratch = [pltpu.VMEM((1, D_RNN), F32)]
    return pl.pallas_call(
        functools.partial(_lru_kernel, geom, reverse),
        grid=(nb,),
        in_specs=in_specs,
        out_specs=[pl.BlockSpec((BLK, D_RNN), lambda g: (blk(g), 0)),
                   pl.BlockSpec((None, 1, D_RNN), lambda g: (blk(g), 0, 0))],
        out_shape=[jax.ShapeDtypeStruct((geom.n_tok, D_RNN), out_dtype),
                   jax.ShapeDtypeStruct((nb, 1, D_RNN), F32)],
        scratch_shapes=scratch,
        compiler_params=_cparams(("arbitrary",)),
        name="lru_bwd" if reverse else "lru_fwd",
    )(*args)


def _group_rms(x, w, ones):
    xx = x * x
    hi = xx.astype(BF16)
    lo = (xx - hi.astype(F32)).astype(BF16)
    ss = (jnp.dot(hi, ones, preferred_element_type=F32)
          + jnp.dot(lo, ones, preferred_element_type=F32))
    return x * lax.rsqrt(ss * (1.0 / DA_QK) + EPS) * w


def _rope(x, cos, sin):
    lane = lax.broadcasted_iota(jnp.int32, x.shape, 1)
    first = (lane % (2 * ROPE_PAIRS)) < ROPE_PAIRS
    w = x.shape[1]
    partner = jnp.where(first, pltpu.roll(x, w - ROPE_PAIRS, 1), pltpu.roll(x, ROPE_PAIRS, 1))
    return x * cos + partner * sin


def _prep_kernel(rope, *refs):
    if rope:
        dq_ref, dk_ref, qw_ref, kw_ref, ones_ref, cos_ref, sin_ref, q_out, k_out = refs
    else:
        dq_ref, dk_ref, qw_ref, kw_ref, ones_ref, dv_ref, q_out, k_out, kf_out, vf_out = refs
        vf_out[...] = dv_ref[...].astype(F32)
    ones = ones_ref[...]
    q = _group_rms(dq_ref[...].astype(F32), qw_ref[...], ones)
    k = _group_rms(dk_ref[...].astype(F32), kw_ref[...], ones)
    if rope:
        cos = jnp.concatenate([cos_ref[...]] * 4, axis=1)
        sin = jnp.concatenate([sin_ref[...]] * 4, axis=1)
        q = _rope(q, cos, sin)
        k = _rope(k, cos, sin)
    else:
        kf_out[...] = k
    q_out[...] = (q * (DA_QK ** -0.5 * math.log2(math.e))).astype(BF16)
    k_out[...] = k.astype(BF16)


def _prep_call(geom, latent, proj, qw, kw, ones, cos=None, sin=None):
    tm = 512
    w = DA_HEADS * 2 * DA_QK
    if latent:
        n, off = geom.n_lat, geom.n_ctx // tm
        per = geom.dec_seq // tm
    else:
        n, off = geom.n_ctx, 0
    in_specs = [pl.BlockSpec((tm, w), lambda i: (i + off, C_DQ // w)),
                pl.BlockSpec((tm, w), lambda i: (i + off, C_DK // w)),
                pl.BlockSpec((1, w), lambda i: (0, 0)),
                pl.BlockSpec((1, w), lambda i: (0, 0)),
                pl.BlockSpec((w, w), lambda i: (0, 0))]
    args = [proj, proj, qw, kw, ones]
    out_specs = [pl.BlockSpec((tm, w), lambda i: (i, 0)), pl.BlockSpec((tm, w), lambda i: (i, 0))]
    out_shape = [jax.ShapeDtypeStruct((n, w), BF16), jax.ShapeDtypeStruct((n, w), BF16)]
    if latent:
        in_specs += [pl.BlockSpec((tm, 2 * DA_QK), lambda i: (i % per, 0)),
                     pl.BlockSpec((tm, 2 * DA_QK), lambda i: (i % per, 0))]
        args += [cos, sin]
    else:
        in_specs.append(pl.BlockSpec((tm, w), lambda i: (i, C_DV // w)))
        args.append(proj)
        out_specs += [pl.BlockSpec((tm, w), lambda i: (i, 0)), pl.BlockSpec((tm, w), lambda i: (i, 0))]
        out_shape += [jax.ShapeDtypeStruct((n, w), F32), jax.ShapeDtypeStruct((n, w), F32)]
    return pl.pallas_call(
        functools.partial(_prep_kernel, latent),
        grid=(n // tm,),
        in_specs=in_specs, out_specs=out_specs, out_shape=out_shape,
        compiler_params=_cparams(("arbitrary",)),
        name="qk_prep_lat" if latent else "qk_prep_ctx",
    )(*args)


ATT_KC = 256
ATT_TQ = 256
LOG2E = math.log2(math.e)
ATT_SAFE_LOGIT = 60.0


def _attn_kernel(out_scale, has_cache, *refs):
    if has_cache:
        par_ref, q_ref, kc_ref, vc_ref, kl_ref, vl_ref, sw_ref, o_ref, e_scr, o_scr = refs
        srcs = [(kc_ref, vc_ref), (kl_ref, vl_ref)]
    else:
        par_ref, q_ref, kl_ref, vl_ref, sw_ref, o_ref, e_scr, o_scr = refs
        srcs = [(kl_ref, vl_ref)]
    chunks = [(kr, vr, st) for kr, vr in srcs for st in range(0, kr.shape[0], ATT_KC)]
    lam = par_ref[0]
    no_shift = par_ref[1] > 0.5
    tqs = ATT_TQ
    nsub = q_ref.shape[0] // tqs
    nt = (((1,), (1,)), ((), ()))
    half = ATT_KC // 2

    def stacked_q(sb):
        q = q_ref[sb * tqs:(sb + 1) * tqs, :]
        lane = lax.broadcasted_iota(jnp.int32, q.shape, 1)
        zero = jnp.zeros_like(q)
        return jnp.concatenate([jnp.where(lane < DA_QK, q, zero), jnp.where(lane >= DA_QK, q, zero)], axis=0)

    def logits(qq, c):
        kr, vr, st = chunks[c]
        return lax.dot_general(qq, kr[st:st + ATT_KC, :], nt, preferred_element_type=F32)

    def fold(total, e):
        part = e[:, :half] + e[:, half:]
        return part if total is None else total + part

    def row_stats(lsum):
        l = jnp.sum(lsum, axis=-1, keepdims=True)
        l1 = l[0:tqs]
        return l1, lam * l1 / l[tqs:2 * tqs]

    def pv(acc, buf, c, rho):
        kr, vr, st = chunks[c]
        w = (e_scr[buf, c, 0:tqs, :] - rho * e_scr[buf, c, tqs:2 * tqs, :]).astype(BF16)
        t = jnp.dot(w, vr[st:st + ATT_KC, :], preferred_element_type=F32)
        return t if acc is None else acc + t

    nck = len(chunks)

    @pl.when(no_shift)
    def _():
        stats = None
        for sb in range(nsub + 1):
            qq = stacked_q(sb) if sb < nsub else None
            lsum, acc = None, None
            for c in range(nck):
                if sb < nsub:
                    e = jnp.exp2(logits(qq, c))
                    e_scr[sb % 2, c] = e
                    lsum = fold(lsum, e)
                if sb > 0:
                    acc = pv(acc, (sb - 1) % 2, c, stats[1])
            if sb > 0:
                o_scr[(sb - 1) * tqs:sb * tqs, :] = acc / stats[0]
            if sb < nsub:
                stats = row_stats(lsum)

    @pl.when(jnp.logical_not(no_shift))
    def _():
        for sb in range(nsub):
            qq = stacked_q(sb)
            m = None
            for c in range(nck):
                s = logits(qq, c)
                e_scr[0, c] = s
                mc = jnp.max(s, axis=-1, keepdims=True)
                m = mc if m is None else jnp.maximum(m, mc)
            lsum = None
            for c in range(nck):
                e = jnp.exp2(e_scr[0, c] - m)
                e_scr[0, c] = e
                lsum = fold(lsum, e)
            l1, rho = row_stats(lsum)
            acc = None
            for c in range(nck):
                acc = pv(acc, 0, c, rho)
            o_scr[sb * tqs:(sb + 1) * tqs, :] = acc / l1

    o = o_scr[...]
    y = o * lax.rsqrt(jnp.mean(o * o, axis=-1, keepdims=True) + EPS) * sw_ref[...]
    o_ref[...] = (y * out_scale).astype(BF16)


def _attn_call(par, lam_init, q2d, k2d, proj, v_row_off, n_b, t_q, t_kl, tq, subln_w, cache=None):
    hw = 2 * DA_QK
    nq = t_q // tq
    vcol = C_DV // DA_V
    in_specs = [pl.BlockSpec(memory_space=pltpu.SMEM),
                pl.BlockSpec((tq, hw), lambda b, h, qi: (b * nq + qi, h))]
    args = [par, q2d]
    n_chunks = t_kl // ATT_KC
    if cache is not None:
        kc, vc = cache
        p = kc.shape[1]
        n_chunks += p // ATT_KC
        in_specs += [pl.BlockSpec((None, p, hw), lambda b, h, qi: (b, 0, h)),
                     pl.BlockSpec((None, p, DA_V), lambda b, h, qi: (b, 0, h))]
        args += [kc, vc]
    in_specs += [pl.BlockSpec((t_kl, hw), lambda b, h, qi: (b, h)),
                 pl.BlockSpec((t_kl, DA_V), lambda b, h, qi: (v_row_off + b, vcol + h)),
                 pl.BlockSpec((1, DA_V), lambda b, h, qi: (0, 0))]
    args += [k2d, proj, subln_w.reshape(1, DA_V)]
    return pl.pallas_call(
        functools.partial(_attn_kernel, 1.0 - lam_init, cache is not None),
        grid=(n_b, DA_HEADS, nq),
        in_specs=in_specs,
        out_specs=pl.BlockSpec((tq, DA_V), lambda b, h, qi: (b * nq + qi, h)),
        out_shape=jax.ShapeDtypeStruct((n_b * t_q, DA_HEADS * DA_V), BF16),
        scratch_shapes=[pltpu.VMEM((2 if tq > ATT_TQ else 1, n_chunks, 2 * ATT_TQ, ATT_KC), F32),
                        pltpu.VMEM((tq, DA_V), F32)],
        compiler_params=_cparams(("arbitrary", "arbitrary", "arbitrary")),
        name="diff_attn_lat" if cache is not None else "diff_attn_ctx",
    )(*args)


def _ret_state_update(kt, v, kd, cd, s_old):
    parts = []
    for h in range(RET_HEADS):
        rows = slice(h * RET_QK, (h + 1) * RET_QK)
        kh = (kt[rows, :].astype(F32) * kd[rows, :]).astype(BF16)
        parts.append(jnp.dot(kh, v[:, h * RET_V:(h + 1) * RET_V], preferred_element_type=F32))
    return cd * s_old + jnp.concatenate(parts, axis=0)


def _ret_bwd_kernel(geom, kt_ref, v_ref, kd_ref, cd_ref, s0_ref, sstart_ref, send_ref, s_scr):
    i = geom.n_blocks - 1 - pl.program_id(0)

    @pl.when(geom.seq_end(i))
    def _():
        s_scr[...] = s0_ref[...]

    s_old = s_scr[...]
    sstart_ref[...] = s_old
    kt = kt_ref[...] * jnp.asarray(RET_QK ** -0.5, BF16)
    s_new = _ret_state_update(kt, v_ref[...], kd_ref[...], cd_ref[...], s_old)
    s_scr[...] = s_new
    send_ref[...] = s_new


def _ret_bwd_call(geom, proj, rkt, kd_b, cd_b, s0):
    nb = geom.n_blocks
    hs = RET_HEADS * RET_QK

    def blk(g):
        return nb - 1 - g

    return pl.pallas_call(
        functools.partial(_ret_bwd_kernel, geom),
        grid=(nb,),
        in_specs=[pl.BlockSpec((hs, BLK), lambda g: (0, blk(g))),
                  pl.BlockSpec((BLK, RET_HEADS * RET_V), lambda g: (blk(g), C_RV // (RET_HEADS * RET_V))),
                  pl.BlockSpec((hs, BLK), lambda g: (0, 0)),
                  pl.BlockSpec((hs, RET_V), lambda g: (0, 0)),
                  pl.BlockSpec((None, None, hs, RET_V), lambda g: (geom.seq_id(blk(g)), 1, 0, 0))],
        out_specs=[pl.BlockSpec((None, hs, RET_V), lambda g: (blk(g), 0, 0)),
                   pl.BlockSpec((None, hs, RET_V), lambda g: (blk(g), 0, 0))],
        out_shape=[jax.ShapeDtypeStruct((nb, hs, RET_V), F32),
                   jax.ShapeDtypeStruct((nb, hs, RET_V), F32)],
        scratch_shapes=[pltpu.VMEM((hs, RET_V), F32)],
        compiler_params=_cparams(("arbitrary",)),
        name="ret_bwd_state",
    )(rkt, proj, kd_b, cd_b, s0)


def _ret_main_kernel(geom, q_ref, kt_ref, v_ref, g_ref, dsum_ref, qdf_ref, qdb_ref, kd_ref, cd_ref,
                     s0_ref, sb_ref, o_ref, send_ref, s_scr):
    i = pl.program_id(0)

    @pl.when(geom.seq_start(i))
    def _():
        s_scr[...] = s0_ref[...]

    s_f = s_scr[...]
    s_fb = s_f.astype(BF16)
    s_bb = sb_ref[...].astype(BF16)
    q = q_ref[...].astype(F32)
    kt = kt_ref[...] * jnp.asarray(RET_QK ** -0.5, BF16)
    v = v_ref[...]
    lane = lax.broadcasted_iota(jnp.int32, q.shape, 1)
    for h in range(RET_HEADS):
        in_head = (lane >= h * RET_QK) & (lane < (h + 1) * RET_QK)
        qh = jnp.where(in_head, q, 0.0)
        vh = v[:, h * RET_V:(h + 1) * RET_V]
        sc = jnp.dot(qh.astype(BF16), kt, preferred_element_type=F32) * dsum_ref[h]
        o = jnp.dot(sc.astype(BF16), vh, preferred_element_type=F32)
        o += jnp.dot((qh * qdf_ref[...]).astype(BF16), s_fb, preferred_element_type=F32)
        o += jnp.dot((qh * qdb_ref[...]).astype(BF16), s_bb, preferred_element_type=F32)
        y = o * lax.rsqrt(jnp.mean(o * o, axis=-1, keepdims=True) + EPS)
        gv = g_ref[:, h * RET_V:(h + 1) * RET_V].astype(F32)
        o_ref[:, h * RET_V:(h + 1) * RET_V] = (y * (gv * jax.nn.sigmoid(gv))).astype(BF16)
    s_new = _ret_state_update(kt, v, kd_ref[...], cd_ref[...], s_f)
    s_scr[...] = s_new
    send_ref[...] = s_new


def _ret_main_call(geom, proj, rkt, dsum, qdf, qdb, kd_f, cd_f, s0, sb_start):
    nb = geom.n_blocks
    hs = RET_HEADS * RET_QK
    hv = RET_HEADS * RET_V
    return pl.pallas_call(
        functools.partial(_ret_main_kernel, geom),
        grid=(nb,),
        in_specs=[pl.BlockSpec((BLK, hs), lambda g: (g, C_RQ // hs)),
                  pl.BlockSpec((hs, BLK), lambda g: (0, g)),
                  pl.BlockSpec((BLK, hv), lambda g: (g, C_RV // hv)),
                  pl.BlockSpec((BLK, hv), lambda g: (g, C_RG // hv)),
                  pl.BlockSpec((RET_HEADS, BLK, BLK), lambda g: (0, 0, 0)),
                  pl.BlockSpec((BLK, hs), lambda g: (0, 0)),
                  pl.BlockSpec((BLK, hs), lambda g: (0, 0)),
                  pl.BlockSpec((hs, BLK), lambda g: (0, 0)),
                  pl.BlockSpec((hs, RET_V), lambda g: (0, 0)),
                  pl.BlockSpec((None, None, hs, RET_V), lambda g: (geom.seq_id(g), 0, 0, 0)),
                  pl.BlockSpec((None, hs, RET_V), lambda g: (g, 0, 0))],
        out_specs=[pl.BlockSpec((BLK, hv), lambda g: (g, 0)),
                   pl.BlockSpec((None, hs, RET_V), lambda g: (g, 0, 0))],
        out_shape=[jax.ShapeDtypeStruct((geom.n_tok, hv), BF16),
                   jax.ShapeDtypeStruct((nb, hs, RET_V), F32)],
        scratch_shapes=[pltpu.VMEM((hs, RET_V), F32)],
        compiler_params=_cparams(("arbitrary",)),
        name="ret_main",
    )(proj, rkt, proj, proj, dsum, qdf, qdb, kd_f, cd_f, s0, sb_start)


def _ret_tables(ret_decay_l):
    log_g = jax.nn.log_sigmoid(ret_decay_l.astype(F32))
    pos = jnp.arange(BLK, dtype=F32)
    diff = pos[:, None] - pos[None, :]
    lf = log_g[0][:, None, None]
    lb = log_g[1][:, None, None]
    dsum = (jnp.where(diff >= 0, jnp.exp(jnp.maximum(diff, 0.0)[None] * lf), 0.0)
            + jnp.where(diff <= 0, jnp.exp(jnp.maximum(-diff, 0.0)[None] * lb), 0.0))

    def per_lane(e, lg):
        return jnp.repeat(jnp.exp(e[:, None] * lg[None, :]), RET_QK, axis=1)

    qdf = per_lane(pos + 1.0, log_g[0])
    qdb = per_lane(BLK - pos, log_g[1])
    kd_f = per_lane(BLK - 1.0 - pos, log_g[0]).T
    kd_b = per_lane(pos, log_g[1]).T
    cd_f = jnp.broadcast_to(jnp.repeat(jnp.exp(BLK * log_g[0]), RET_QK)[:, None], (RET_HEADS * RET_QK, RET_V))
    cd_b = jnp.broadcast_to(jnp.repeat(jnp.exp(BLK * log_g[1]), RET_QK)[:, None], (RET_HEADS * RET_QK, RET_V))
    return dsum, qdf, qdb, kd_f, kd_b, cd_f, cd_b


def _merge_kernel(n_ctx_tiles, ba_ref, bbc_ref, bbl_ref, bc_ref, g0_ref, g1_ref, g2_ref, xc_ref, xl_ref,
                  gate_ref, sc_ref, sh_ref, nw_ref, wb_ref, wo_ref, rhi_ref, rlo_ref, x1_ref, h2_ref, h2p_ref,
                  lt_ref):
    branches = (ba_ref[...], _pick_part(n_ctx_tiles, bbc_ref, bbl_ref), bc_ref[...])
    acc = None
    for br, (b, g_ref) in enumerate(zip(branches, (g0_ref, g1_ref, g2_ref))):
        p = jnp.dot(b, wb_ref[br], preferred_element_type=F32)
        t = (0.5 * jnp.tanh(0.5 * g_ref[...].astype(F32)) + 0.5) * p
        acc = t if acc is None else acc + t
    m = jnp.dot(acc.astype(BF16), wo_ref[...], preferred_element_type=F32)
    x1 = _pick_part(n_ctx_tiles, xc_ref, xl_ref) + gate_ref[...] * m
    x1_ref[...] = x1
    ms = jnp.mean(x1 * x1, axis=-1, keepdims=True)
    h2 = x1 * lax.rsqrt(ms + EPS) * nw_ref[...] * (1.0 + sc_ref[...]) + sh_ref[...]
    h2b = h2.astype(BF16)
    h2_ref[...] = h2b
    h2p_ref[...] = _pack_halves(h2b.astype(F32))
    h2lo = (h2 - h2b.astype(F32)).astype(BF16)
    nt = (((1,), (1,)), ((), ()))
    lt_ref[...] = (lax.dot_general(rhi_ref[...], h2b, nt, preferred_element_type=F32)
                   + lax.dot_general(rhi_ref[...], h2lo, nt, preferred_element_type=F32)
                   + lax.dot_general(rlo_ref[...], h2b, nt, preferred_element_type=F32))


def _merge_call(geom, l, ba, bb_ctx, bb_lat, bc, proj, x_ctx, x_lat, mod6, norm2_w, wb_bf, wo_bf, r_hi, r_lo):
    tm = 512
    gcol = C_GL // D_MODEL
    full = lambda shape: pl.BlockSpec(shape, lambda i: tuple(0 for _ in shape))
    tok = lambda w: pl.BlockSpec((tm, w), lambda i: (i, 0))
    return pl.pallas_call(
        functools.partial(_merge_kernel, geom.n_ctx // tm),
        grid=(geom.n_tok // tm,),
        in_specs=[tok(BRANCH_W)] + _split_in_specs(geom, tm, BRANCH_W, 1) + [tok(BRANCH_W),
                  pl.BlockSpec((tm, D_MODEL), lambda i: (i, gcol)),
                  pl.BlockSpec((tm, D_MODEL), lambda i: (i, gcol + 1)),
                  pl.BlockSpec((tm, D_MODEL), lambda i: (i, gcol + 2))]
                 + _split_in_specs(geom, tm, D_MODEL, 1) + [
                  _mod_spec(geom, l, 2, tm, 1), _mod_spec(geom, l, 4, tm, 1), _mod_spec(geom, l, 3, tm, 1),
                  full((1, D_MODEL)),
                  full((N_BRANCH, BRANCH_W, D_MODEL)), full((D_MODEL, D_MODEL)),
                  full((N_EXPERTS, D_MODEL)), full((N_EXPERTS, D_MODEL))],
        out_specs=[tok(D_MODEL), tok(D_MODEL), tok(D_MODEL // 2), pl.BlockSpec((N_EXPERTS, tm), lambda i: (0, i))],
        out_shape=[jax.ShapeDtypeStruct((geom.n_tok, D_MODEL), F32),
                   jax.ShapeDtypeStruct((geom.n_tok, D_MODEL), BF16),
                   jax.ShapeDtypeStruct((geom.n_tok, D_MODEL // 2), jnp.uint32),
                   jax.ShapeDtypeStruct((N_EXPERTS, geom.n_tok), F32)],
        compiler_params=_cparams(("arbitrary",)),
        name="merge_out",
    )(ba, bb_ctx, bb_lat, bc, proj, proj, proj, x_ctx, x_lat, mod6, mod6, mod6,
      norm2_w.reshape(1, D_MODEL), wb_bf, wo_bf, r_hi, r_lo)


def _router_kernel(lt_ref, bias_ref, ltri_ref, utri_ref, g_ref, slot_ref, cnt_ref, cnt_scr):
    per = N_EXPERTS // N_GROUPS
    tm = lt_ref.shape[1]
    scores = jax.nn.sigmoid(lt_ref[...])
    biased = scores + bias_ref[...]
    b3 = biased.reshape(N_GROUPS, per, tm)
    neg = jnp.float32(-jnp.inf)
    m1 = jnp.max(b3, axis=1, keepdims=True)
    is_m1 = b3 == m1
    cnt = jnp.sum(is_m1.astype(F32), axis=1, keepdims=True)
    m2 = jnp.max(jnp.where(is_m1, neg, b3), axis=1, keepdims=True)
    grp = (m1 + jnp.where(cnt >= 2.0, m1, m2)).reshape(N_GROUPS, tm)
    gidx = lax.broadcasted_iota(jnp.int32, (N_GROUPS, tm), 0)
    grank = jnp.zeros((N_GROUPS, tm), F32)
    for g2 in range(N_GROUPS):
        other = grp[g2:g2 + 1, :]
        ahead = (other > grp) | ((other == grp) & (gidx > g2))
        grank += ahead.astype(F32)
    gsel = (grank < float(TOPK_GROUPS)).astype(F32)
    emask = jnp.broadcast_to(gsel.reshape(N_GROUPS, 1, tm), (N_GROUPS, per, tm)).reshape(N_EXPERTS, tm)
    masked = jnp.where(emask > 0.0, biased, neg)
    eidx = lax.broadcasted_iota(jnp.int32, (N_EXPERTS, tm), 0)
    erank = jnp.zeros((N_EXPERTS, tm), F32)
    for e2 in range(N_EXPERTS):
        other = masked[e2:e2 + 1, :]
        ahead = (other > masked) | ((other == masked) & (eidx > e2))
        erank += ahead.astype(F32)
    sel = erank < float(TOP_K)
    w = jnp.where(sel, scores, 0.0)
    gates_t = w / jnp.sum(w, axis=0, keepdims=True) * ROUTED_SCALE

    @pl.when(pl.program_id(0) == 0)
    def _():
        cnt_scr[...] = jnp.zeros_like(cnt_scr)

    selb = sel.astype(BF16)
    slot = jnp.dot(ltri_ref[...], selb, preferred_element_type=F32)
    carry = cnt_scr[:, 0:1]
    rank = jnp.dot(selb, utri_ref[...], preferred_element_type=F32) + carry
    cnt_new = cnt_scr[...] + jnp.sum(sel.astype(F32), axis=1, keepdims=True)
    cnt_scr[...] = cnt_new
    cnt_ref[...] = cnt_new
    eid_f = eidx.astype(F32)
    g_rows, e_rows, r_rows = [], [], []
    for k in range(TOP_K):
        mk = jnp.where(sel & (slot == float(k)), 1.0, 0.0)
        g_rows.append(jnp.sum(mk * gates_t, axis=0, keepdims=True))
        e_rows.append(jnp.sum(mk * eid_f, axis=0, keepdims=True))
        r_rows.append(jnp.sum(mk * rank, axis=0, keepdims=True))
    slot_ref[...] = jnp.concatenate(e_rows + r_rows, axis=0).astype(jnp.int32)
    pad = jnp.zeros((GATE_W - TOP_K, tm), F32)
    g_ref[...] = jnp.concatenate(g_rows + [pad], axis=0).T


ROUTER_TM = 512


def _router_call(geom, logits_t, bias):
    tm = ROUTER_TM
    ltri = jnp.asarray(np.tril(np.ones((N_EXPERTS, N_EXPERTS), np.float32), -1), BF16)
    utri = jnp.asarray(np.triu(np.ones((tm, tm), np.float32), 1), BF16)
    return pl.pallas_call(
        _router_kernel,
        grid=(geom.n_tok // tm,),
        in_specs=[pl.BlockSpec((N_EXPERTS, tm), lambda i: (0, i)),
                  pl.BlockSpec((N_EXPERTS, 1), lambda i: (0, 0)),
                  pl.BlockSpec((N_EXPERTS, N_EXPERTS), lambda i: (0, 0)),
                  pl.BlockSpec((tm, tm), lambda i: (0, 0))],
        out_specs=[pl.BlockSpec((tm, GATE_W), lambda i: (i, 0)),
                   pl.BlockSpec((2 * TOP_K, tm), lambda i: (0, i)),
                   pl.BlockSpec((N_EXPERTS, GATE_W), lambda i: (0, 0))],
        out_shape=[jax.ShapeDtypeStruct((geom.n_tok, GATE_W), F32),
                   jax.ShapeDtypeStruct((2 * TOP_K, geom.n_tok), jnp.int32),
                   jax.ShapeDtypeStruct((N_EXPERTS, GATE_W), F32)],
        scratch_shapes=[pltpu.VMEM((N_EXPERTS, GATE_W), F32)],
        compiler_params=_cparams(("arbitrary",)),
        name="router",
    )(logits_t, bias.reshape(N_EXPERTS, 1), ltri, utri)


MOE_TR = 512
SC_CORES = 2
SC_SUBCORES = 16
SC_CHUNK = 64


def _sc_worker_base(rows_per_worker):
    wid = lax.axis_index("s") * SC_CORES + lax.axis_index("c")
    return wid * rows_per_worker


def _sc_scatter_rows(table, pos_flat, n_slots, n_rows_out):
    n, d = table.shape
    nw = SC_CORES * SC_SUBCORES
    assert n % (nw * SC_CHUNK) == 0
    per_w = n // nw
    mesh = plsc.VectorSubcoreMesh(core_axis_name="c", subcore_axis_name="s")

    @functools.partial(
        pl.kernel, mesh=mesh,
        out_type=jax.ShapeDtypeStruct((n_rows_out, d), table.dtype),
        scratch_types=[[pltpu.VMEM((SC_CHUNK,), jnp.int32) for _ in range(n_slots)],
                       pltpu.VMEM((SC_CHUNK, d), table.dtype),
                       pltpu.SemaphoreType.DMA],
    )
    def scatter(table_hbm, pos_hbm, out_hbm, idx_v, rows_v, sem):
        base = _sc_worker_base(per_w)

        @pl.loop(0, per_w // SC_CHUNK)
        def _(ci):
            off = pl.multiple_of(base + ci * SC_CHUNK, 8)
            for k in range(n_slots):
                pltpu.sync_copy(pos_hbm.at[pl.ds(pl.multiple_of(k * n + off, 8), SC_CHUNK)], idx_v[k])
            pltpu.sync_copy(table_hbm.at[pl.ds(off, SC_CHUNK)], rows_v)
            copies = [pltpu.make_async_copy(rows_v, out_hbm.at[idx_v[k]], sem) for k in range(n_slots)]
            for cp in copies:
                cp.start()
            for cp in copies:
                cp.wait()

    return scatter(table, pos_flat)


def _sc_gather_rows(table, idx):
    b = idx.shape[0]
    d = table.shape[1]
    nw = SC_CORES * SC_SUBCORES
    nbuf = 2
    assert b % (nw * SC_CHUNK * nbuf) == 0
    per_w = b // nw
    n_chunks = per_w // SC_CHUNK
    mesh = plsc.VectorSubcoreMesh(core_axis_name="c", subcore_axis_name="s")

    @functools.partial(
        pl.kernel, mesh=mesh,
        out_type=jax.ShapeDtypeStruct((b, d), table.dtype),
        scratch_types=[pltpu.VMEM((per_w,), jnp.int32),
                       [pltpu.VMEM((SC_CHUNK, d), table.dtype) for _ in range(nbuf)],
                       [pltpu.SemaphoreType.DMA for _ in range(nbuf)],
                       [pltpu.SemaphoreType.DMA for _ in range(nbuf)]],
    )
    def gather(table_hbm, idx_hbm, out_hbm, idx_v, rows, gsem, wsem):
        base = _sc_worker_base(per_w)
        pltpu.sync_copy(idx_hbm.at[pl.ds(pl.multiple_of(base, 8), per_w)], idx_v)

        def fetch(ci, slot):
            src = table_hbm.at[idx_v.at[pl.ds(pl.multiple_of(ci * SC_CHUNK, 8), SC_CHUNK)]]
            return pltpu.make_async_copy(src, rows[slot], gsem[slot])

        def put(ci, slot):
            dst = out_hbm.at[pl.ds(pl.multiple_of(base + ci * SC_CHUNK, 8), SC_CHUNK)]
            return pltpu.make_async_copy(rows[slot], dst, wsem[slot])

        for slot in range(nbuf):
            fetch(slot, slot).start()

        @pl.loop(0, n_chunks, step=nbuf)
        def _(c0):
            for slot in range(nbuf):
                ci = c0 + slot
                fetch(ci, slot).wait()
                put(ci, slot).start()
                put(ci, slot).wait()

                @pl.when(ci + nbuf < n_chunks)
                def _():
                    fetch(ci + nbuf, slot).start()

    return gather(table, idx)


def _route_positions(n_tok, slots, counts):
    cnt = counts[:, 0].astype(jnp.int32)
    cnt_pad = ((cnt + MOE_TR - 1) // MOE_TR) * MOE_TR
    off_end = jnp.cumsum(cnt_pad)
    off = off_end - cnt_pad
    eid, rank = slots[:TOP_K], slots[TOP_K:]
    eids = jnp.arange(N_EXPERTS, dtype=jnp.int32)
    pos = jnp.sum(jnp.where(eid[..., None] == eids, off, 0), axis=-1) + rank
    n_tiles = (TOP_K * n_tok) // MOE_TR + N_EXPERTS
    tile_start = jnp.arange(n_tiles, dtype=jnp.int32) * MOE_TR
    tile_expert = jnp.sum((tile_start[:, None] >= off_end[None, :]).astype(jnp.int32), axis=1)
    tile_expert = jnp.minimum(tile_expert, N_EXPERTS - 1)
    n_used = (off_end[-1] // MOE_TR).reshape(1)
    tile_idx = jnp.arange(n_tiles, dtype=jnp.int32)
    used = tile_idx < n_used[0]
    prev = jnp.concatenate([jnp.full((1,), -1, jnp.int32), tile_expert[:-1]])
    first = jnp.logical_and(used, tile_expert != prev)
    parity = (jnp.cumsum(first.astype(jnp.int32)) - 1) % 2
    later = jnp.logical_and(eids[None, :] > eids[:, None], (cnt_pad > 0)[None, :])
    next_e = jnp.min(jnp.where(later, eids[None, :], N_EXPERTS), axis=1)
    nxt = jnp.sum(jnp.where(tile_expert[:, None] == eids, next_e, 0), axis=1)
    has_next = jnp.logical_and(first, nxt < N_EXPERTS)
    sched = (tile_expert, n_used, first.astype(jnp.int32), jnp.minimum(nxt, N_EXPERTS - 1).astype(jnp.int32),
             jnp.maximum(parity, 0).astype(jnp.int32), has_next.astype(jnp.int32))
    return pos, sched, n_tiles


def _expert_ffn(x_lo, x_hi, gu, dn):
    half = D_MODEL // 2
    a = (jnp.dot(x_lo, gu[0:half, :], preferred_element_type=F32)
         + jnp.dot(x_hi, gu[half:, :], preferred_element_type=F32))
    hg = a[:, :D_EXPERT]
    act = (hg * jax.nn.sigmoid(hg)) * a[:, D_EXPERT:]
    return jnp.dot(act.astype(BF16), dn, preferred_element_type=F32)


def _experts_kernel(l, te_ref, nu_ref, first_ref, nxt_ref, par_ref, hasn_ref, x_ref, gu_hbm, dn_hbm, y_ref,
                    gu_f, dn_f, gu_b, dn_b, sem):
    i = pl.program_id(0)

    def fetch(e, slot):
        return (pltpu.make_async_copy(gu_hbm.at[l, e], gu_f.at[slot], sem.at[0, slot]),
                pltpu.make_async_copy(dn_hbm.at[l, e], dn_f.at[slot], sem.at[1, slot]))

    @pl.when(jnp.logical_and(i == 0, nu_ref[0] > 0))
    def _():
        for cp in fetch(te_ref[0], par_ref[0]):
            cp.start()

    @pl.when(first_ref[i] == 1)
    def _():
        slot = par_ref[i]
        for cp in fetch(te_ref[i], slot):
            cp.wait()

        @pl.when(hasn_ref[i] == 1)
        def _():
            for cp in fetch(nxt_ref[i], 1 - slot):
                cp.start()

        gu_b[...] = gu_f[slot].astype(BF16)
        dn_b[...] = dn_f[slot].astype(BF16)

    @pl.when(i < nu_ref[0])
    def _():
        lo, hi = _unpack_halves(x_ref[...])
        y = _expert_ffn(lo.astype(BF16), hi.astype(BF16), gu_b[...], dn_b[...])
        y_ref[...] = _pack_halves(y.astype(BF16).astype(F32))

    @pl.when(i >= nu_ref[0])
    def _():
        y_ref[...] = jnp.zeros_like(y_ref)


def _experts_call(l, xs, sched, n_tiles, w_gu, w_dn):
    half = D_MODEL // 2
    grid_spec = pltpu.PrefetchScalarGridSpec(
        num_scalar_prefetch=len(sched),
        grid=(n_tiles,),
        in_specs=[pl.BlockSpec((MOE_TR, half), lambda i, te, nu, *_: (jnp.minimum(i, jnp.maximum(nu[0], 1) - 1), 0)),
                  pl.BlockSpec(memory_space=pl.ANY),
                  pl.BlockSpec(memory_space=pl.ANY)],
        out_specs=pl.BlockSpec((MOE_TR, half), lambda i, *_: (i, 0)),
        scratch_shapes=[pltpu.VMEM((2, D_MODEL, 2 * D_EXPERT), w_gu.dtype),
                        pltpu.VMEM((2, D_EXPERT, D_MODEL), w_dn.dtype),
                        pltpu.VMEM((D_MODEL, 2 * D_EXPERT), BF16),
                        pltpu.VMEM((D_EXPERT, D_MODEL), BF16),
                        pltpu.SemaphoreType.DMA((2, 2))],
    )
    return pl.pallas_call(
        functools.partial(_experts_kernel, l),
        grid_spec=grid_spec,
        out_shape=jax.ShapeDtypeStruct((n_tiles * MOE_TR, half), jnp.uint32),
        compiler_params=_cparams(("arbitrary",)),
        name="moe_experts",
    )(*sched, xs, w_gu, w_dn)


MOE_OUT_PARTS = 2


def _moe_out_kernel(n_ctx_tiles, tile0, first, *refs):
    if first:
        yt_ref, g_ref, h_ref, sgu_ref, sdn_ref, x1_ref, gate_ref, oc_ref, ol_ref = refs
    else:
        yt_ref, g_ref, h_ref, sgu_ref, sdn_ref, x1_ref, gate_ref, _, ol_ref = refs
    i = pl.program_id(0) + tile0
    gts = g_ref[...]
    lane = lax.broadcasted_iota(jnp.int32, gts.shape, 1)
    acc_lo, acc_hi = None, None
    for k in range(TOP_K):
        ge = jnp.sum(jnp.where(lane == k, gts, 0.0), axis=1, keepdims=True)
        lo, hi = _unpack_halves(yt_ref[k])
        acc_lo = ge * lo if acc_lo is None else acc_lo + ge * lo
        acc_hi = ge * hi if acc_hi is None else acc_hi + ge * hi
    routed = jnp.concatenate([acc_lo, acc_hi], axis=1)
    h = h_ref[...]
    half = D_MODEL // 2
    shared = _expert_ffn(h[:, :half], h[:, half:], sgu_ref[...], sdn_ref[...])
    y = x1_ref[...] + gate_ref[...] * (routed + shared)
    if first:
        @pl.when(i < n_ctx_tiles)
        def _():
            oc_ref[...] = y

        @pl.when(i >= n_ctx_tiles)
        def _():
            ol_ref[...] = y
    else:
        ol_ref[...] = y


def _moe_out_call(geom, l, part, yt, gates, h2, sgu_bf, sdn_bf, x1, mod6, prev_lat=None):
    tm = 512
    nct = geom.n_ctx // tm
    n_part = geom.n_tok // tm // MOE_OUT_PARTS
    t0 = part * n_part
    first = part == 0
    assert nct <= n_part
    half = D_MODEL // 2
    tok = lambda w: pl.BlockSpec((tm, w), lambda i: (i + t0, 0))
    in_specs = [pl.BlockSpec((TOP_K, tm, half), lambda i: (0, i, 0)),
                tok(GATE_W), tok(D_MODEL),
                pl.BlockSpec((None, D_MODEL, 2 * D_EXPERT), lambda i: (l, 0, 0)),
                pl.BlockSpec((None, D_EXPERT, D_MODEL), lambda i: (l, 0, 0)),
                tok(D_MODEL),
                pl.BlockSpec((None, None, None, 1, D_MODEL), lambda i: (l, geom.mod_row(i + t0, tm), 5, 0, 0))]
    args = [yt, gates, h2, sgu_bf, sdn_bf, x1, mod6]
    lat_shape = jax.ShapeDtypeStruct((geom.n_lat, D_MODEL), F32)
    if first:
        out_specs = [pl.BlockSpec((tm, D_MODEL), lambda i: (jnp.minimum(i, nct - 1), 0)),
                     pl.BlockSpec((tm, D_MODEL), lambda i: (jnp.maximum(i - nct, 0), 0))]
        out_shape = [jax.ShapeDtypeStruct((geom.n_ctx, D_MODEL), F32), lat_shape]
        aliases = {}
    else:
        in_specs.append(pl.BlockSpec(memory_space=pl.ANY))
        args.append(prev_lat)
        out_specs = [pl.BlockSpec((tm, D_MODEL), lambda i: (i + t0 - nct, 0))]
        out_shape = [lat_shape]
        aliases = {len(args) - 1: 0}
    return pl.pallas_call(
        functools.partial(_moe_out_kernel, nct, t0, first),
        grid=(n_part,),
        in_specs=in_specs, out_specs=out_specs, out_shape=out_shape,
        input_output_aliases=aliases,
        compiler_params=_cparams(("arbitrary",)),
        name="moe_out",
    )(*args)


def _moe(geom, l, h2, h2p, gates, slots, counts, w_gu, w_dn, sgu_bf, sdn_bf, x1, mod6):
    pos, sched, n_tiles = _route_positions(geom.n_tok, slots, counts)
    xs = _sc_scatter_rows(h2p, pos.reshape(-1), TOP_K, n_tiles * MOE_TR)
    ys = _experts_call(l, xs, sched, n_tiles, w_gu, w_dn)
    n_part = geom.n_tok // MOE_OUT_PARTS
    y_ctx, y_lat = None, None
    for part in range(MOE_OUT_PARTS):
        pos_p = pos[:, part * n_part:(part + 1) * n_part].reshape(-1)
        yt = _sc_gather_rows(ys, pos_p).reshape(TOP_K, n_part, D_MODEL // 2)
        outs = _moe_out_call(geom, l, part, yt, gates, h2, sgu_bf, sdn_bf, x1, mod6, y_lat)
        if part == 0:
            y_ctx, y_lat = outs
        else:
            (y_lat,) = outs
    return y_ctx, y_lat


def _rope_tables(dec_seq):
    rows = dec_seq // GRID_W
    row = jnp.repeat(jnp.arange(rows, dtype=F32), GRID_W)
    col = jnp.tile(jnp.arange(GRID_W, dtype=F32), rows)
    inv = ROPE_BASE ** (-jnp.arange(ROPE_PAIRS, dtype=F32) / ROPE_PAIRS)
    ar = row[:, None] * inv[None, :]
    ac = col[:, None] * inv[None, :]
    cos64 = jnp.concatenate([jnp.cos(ar), jnp.cos(ar), jnp.cos(ac), jnp.cos(ac)], axis=1)
    sin64 = jnp.concatenate([-jnp.sin(ar), jnp.sin(ar), -jnp.sin(ac), jnp.sin(ac)], axis=1)
    return jnp.tile(cos64, (1, 2)), jnp.tile(sin64, (1, 2))


def _block_diag_gate(wg_dir):
    eye = jnp.eye(LRU_BLOCKS, dtype=F32)
    dense = jnp.einsum('gnij,nm->gnimj', wg_dir.astype(F32), eye).reshape(2, D_RNN, D_RNN)
    return jnp.concatenate([dense[0], dense[1]], axis=1)


def kernel(x_prompt, x_sample, cache_k, cache_v, state_lru, state_ret, c, c_ctx, ada_w, ada_b, norm1_w, norm2_w, w_in, conv_w, conv_b, lru_gate_w, lru_gate_b, lru_lambda, q_norm_w, k_norm_w, diff_lambda, subln_w, ret_decay, w_branch, w_out, router_w, router_bias, w_exp_gu, w_exp_down, w_sh_gu, w_sh_down):
    batch, seq, _ = x_prompt.shape
    dec_batch, dec_seq, _ = x_sample.shape
    assert 1 + dec_batch <= MOD_ROWS
    geom = _Geom(batch, seq, dec_batch, dec_seq)
    hs = RET_HEADS * RET_QK
    aw = DA_HEADS * 2 * DA_QK

    x_ctx = x_prompt.reshape(geom.n_ctx, D_MODEL)
    x_lat = x_sample.reshape(geom.n_lat, D_MODEL)
    cvec = jnp.zeros((MOD_ROWS, D_MODEL), F32).at[0].set(c_ctx).at[1:1 + dec_batch].set(c)
    mod6 = _ada_call(cvec, ada_w, ada_b).reshape(DEPTH, MOD_ROWS, 6, 1, D_MODEL)

    ones_bd = jnp.kron(jnp.eye(aw // DA_QK, dtype=F32), jnp.ones((DA_QK, DA_QK), F32)).astype(BF16)
    cos_t, sin_t = _rope_tables(dec_seq)

    w_in_bf = w_in.astype(BF16)
    sgu_bf, sdn_bf = w_sh_gu.astype(BF16), w_sh_down.astype(BF16)

    ks, vs, lrus, rets = [], [], [], []
    for l in range(DEPTH):
        lam_init = 0.8 - 0.6 * math.exp(-0.3 * l)
        w_rkt_bf = w_in[l][:, C_RK:C_RK + hs].T.astype(BF16)
        proj, rkt = _inproj_call(geom, l, x_ctx, x_lat, mod6, norm1_w[l], w_in_bf, w_rkt_bf)

        sp = jax.nn.softplus(-lru_lambda[l].astype(F32))
        h0 = jnp.concatenate([jnp.zeros((batch, 2, D_RNN), F32), state_lru[:, l].astype(F32)], axis=0)
        h0 = h0.reshape(geom.n_seq, 2, 1, D_RNN)
        cb = conv_b[l].reshape(1, D_RNN)
        lru_args = []
        for d in range(2):
            lru_args.append((_block_diag_gate(lru_gate_w[l, d]).astype(BF16),
                             lru_gate_b[l, d].reshape(1, 2 * D_RNN), sp[d].reshape(1, D_RNN)))
        hf, hf_last = _lru_call(geom, False, proj, conv_w[l], cb, *lru_args[0], h0)
        branch_a, hb_last = _lru_call(geom, True, proj, conv_w[l], cb, *lru_args[1], h0, hf)

        qw = jnp.tile(q_norm_w[l], aw // DA_QK).reshape(1, aw)
        kw = jnp.tile(k_norm_w[l], aw // DA_QK).reshape(1, aw)
        q_c, k_c, k_c32, v_c32 = _prep_call(geom, False, proj, qw, kw, ones_bd)
        q_l, k_l = _prep_call(geom, True, proj, qw, kw, ones_bd, cos_t, sin_t)
        lam_p = diff_lambda[l].astype(F32)
        lam = jnp.exp(jnp.sum(lam_p[0] * lam_p[1])) - jnp.exp(jnp.sum(lam_p[2] * lam_p[3])) + lam_init
        q_bound = DA_QK * jnp.max(jnp.square(q_norm_w[l].astype(F32))) * (DA_QK ** -0.5 * LOG2E) ** 2
        k_bound = DA_QK * jnp.max(jnp.square(k_norm_w[l].astype(F32)))
        kc32 = cache_k[:, l].astype(F32)
        kc_bound = jnp.maximum(k_bound, jnp.max(jnp.sum(jnp.square(kc32), axis=-1)))

        def attn_par(kb):
            ok = (q_bound * kb * 1.05 < ATT_SAFE_LOGIT ** 2).astype(F32)
            return jnp.stack([lam, ok])

        assert geom.n_ctx % dec_seq == 0
        cache = (kc32.reshape(dec_batch, -1, aw).astype(BF16),
                 cache_v[:, l].reshape(dec_batch, -1, DA_HEADS * DA_V).astype(BF16))
        att_c = _attn_call(attn_par(k_bound), lam_init, q_c, k_c, proj, 0, batch, seq, seq, 256, subln_w[l])
        att_l = _attn_call(attn_par(kc_bound), lam_init, q_l, k_l, proj, geom.n_ctx // dec_seq, dec_batch,
                           dec_seq, dec_seq, min(8 * ATT_TQ, dec_seq), subln_w[l], cache)

        dsum, qdf, qdb, kd_f, kd_b, cd_f, cd_b = _ret_tables(ret_decay[l])
        s0 = jnp.concatenate([jnp.zeros((batch, 2, hs, RET_V), F32),
                              state_ret[:, l].astype(F32).reshape(dec_batch, 2, hs, RET_V)], axis=0)
        sb_start, sb_end = _ret_bwd_call(geom, proj, rkt, kd_b, cd_b, s0)
        branch_c, sf_end = _ret_main_call(geom, proj, rkt, dsum, qdf, qdb, kd_f, cd_f, s0, sb_start)

        r_t = router_w[l].T.astype(F32)
        r_hi = r_t.astype(BF16)
        r_lo = (r_t - r_hi.astype(F32)).astype(BF16)
        x1, h2, h2p, logits_t = _merge_call(geom, l, branch_a, att_c, att_l, branch_c, proj, x_ctx, x_lat, mod6,
                                            norm2_w[l], w_branch[l].astype(BF16), w_out[l].astype(BF16), r_hi, r_lo)
        gates, slots, counts = _router_call(geom, logits_t, router_bias[l].astype(F32))
        x_ctx, x_lat = _moe(geom, l, h2, h2p, gates, slots, counts, w_exp_gu, w_exp_down, sgu_bf, sdn_bf, x1, mod6)

        ks.append(k_c32.reshape(batch, seq, DA_HEADS, 2, DA_QK))
        vs.append(v_c32.reshape(batch, seq, DA_HEADS, DA_V))
        lrus.append(jnp.stack([hf_last[:batch, 0], hb_last[:batch, 0]], axis=1))
        rets.append(jnp.stack([sf_end[:batch].reshape(batch, RET_HEADS, RET_QK, RET_V),
                               sb_end[:batch].reshape(batch, RET_HEADS, RET_QK, RET_V)], axis=1))

    y_prompt = x_ctx.reshape(batch, seq, D_MODEL)
    y_sample = x_lat.reshape(dec_batch, dec_seq, D_MODEL)
    return (y_prompt, y_sample, jnp.stack(ks, axis=1), jnp.stack(vs, axis=1),
            jnp.stack(lrus, axis=1), jnp.stack(rets, axis=1))
```

```python
import functools
import math

import numpy as np
import jax
import jax.numpy as jnp
from jax import lax
from jax.experimental import pallas as pl
from jax.experimental.pallas import tpu as pltpu
from jax.experimental.pallas import tpu_sc as plsc

F32 = jnp.float32
BF16 = jnp.bfloat16

D_MODEL = 1024
DEPTH = 2
GRID_W = 64
D_RNN = 512
LRU_BLOCKS = 8
LRU_BLOCK = D_RNN // LRU_BLOCKS
CONV_W = 4
LRU_C = 8.0
DA_HEADS = 4
DA_QK = 64
DA_V = 128
ROPE_PAIRS = DA_QK // 4
ROPE_BASE = 10000.0
RET_HEADS = 4
RET_QK = 64
RET_V = 128
BRANCH_W = 512
N_BRANCH = 3
D_IN = 7168
N_EXPERTS = 64
TOP_K = 8
N_GROUPS = 8
TOPK_GROUPS = 4
D_EXPERT = 256
ROUTED_SCALE = 2.5
EPS = 1e-6

C_XA, C_GA, C_DQ, C_DK, C_DV = 0, 512, 1024, 1536, 2048
C_RQ, C_RK, C_RV, C_RG, C_GL = 2560, 2816, 3072, 3584, 4096

BLK = 256
LRU_SUB = 8
GATE_W = 128
MOD_ROWS = 8
VMEM_LIMIT = 56 * 1024 * 1024


def _cparams(sem, vmem_limit=VMEM_LIMIT):
    return pltpu.CompilerParams(dimension_semantics=sem, vmem_limit_bytes=vmem_limit)


class _Geom:
    def __init__(self, batch, seq, dec_batch, dec_seq):
        assert seq == BLK and dec_seq % BLK == 0
        self.batch, self.seq, self.dec_batch, self.dec_seq = batch, seq, dec_batch, dec_seq
        self.n_ctx = batch * seq
        self.n_lat = dec_batch * dec_seq
        self.n_tok = self.n_ctx + self.n_lat
        self.ctx_blocks = self.n_ctx // BLK
        self.lat_blocks = dec_seq // BLK
        self.n_blocks = self.n_tok // BLK
        self.n_seq = batch + dec_batch

    def mod_row(self, i, tile):
        nct = self.n_ctx // tile
        per = self.dec_seq // tile
        return jnp.where(i < nct, 0, 1 + (i - nct) // per)

    def seq_id(self, i):
        return jnp.where(i < self.ctx_blocks, i, self.ctx_blocks + (i - self.ctx_blocks) // self.lat_blocks)

    def seq_start(self, i):
        return jnp.logical_or(i < self.ctx_blocks, (i - self.ctx_blocks) % self.lat_blocks == 0)

    def seq_end(self, i):
        return jnp.logical_or(i < self.ctx_blocks, (i - self.ctx_blocks) % self.lat_blocks == self.lat_blocks - 1)


def _ada_kernel(c_ref, w_ref, b_ref, o_ref):
    cv = c_ref[...]
    s = cv * jax.nn.sigmoid(cv)
    o_ref[...] = jnp.dot(s, w_ref[...], preferred_element_type=F32,
                         precision=lax.Precision.HIGHEST) + b_ref[...]


def _ada_call(cvec, ada_w, ada_b):
    depth = ada_w.shape[0]
    nt = 6
    return pl.pallas_call(
        _ada_kernel,
        grid=(depth, nt),
        in_specs=[pl.BlockSpec((MOD_ROWS, D_MODEL), lambda l, j: (0, 0)),
                  pl.BlockSpec((None, D_MODEL, D_MODEL), lambda l, j: (l, 0, j)),
                  pl.BlockSpec((None, 1, D_MODEL), lambda l, j: (l, 0, j))],
        out_specs=pl.BlockSpec((None, MOD_ROWS, D_MODEL), lambda l, j: (l, 0, j)),
        out_shape=jax.ShapeDtypeStruct((depth, MOD_ROWS, 6 * D_MODEL), F32),
        compiler_params=_cparams(("arbitrary", "arbitrary")),
        name="ada_mod",
    )(cvec, ada_w, ada_b.reshape(depth, 1, 6 * D_MODEL))


def _mod_spec(geom, l, which, tile, ngrid):
    if ngrid == 1:
        return pl.BlockSpec((None, None, None, 1, D_MODEL),
                            lambda i: (l, geom.mod_row(i, tile), which, 0, 0))
    return pl.BlockSpec((None, None, None, 1, D_MODEL),
                        lambda i, j: (l, geom.mod_row(i, tile), which, 0, 0))


def _split_in_specs(geom, tile, width, ngrid):
    nct = geom.n_ctx // tile
    if ngrid == 1:
        return [pl.BlockSpec((tile, width), lambda i: (jnp.minimum(i, nct - 1), 0)),
                pl.BlockSpec((tile, width), lambda i: (jnp.maximum(i - nct, 0), 0))]
    return [pl.BlockSpec((tile, width), lambda i, j: (jnp.minimum(i, nct - 1), 0)),
            pl.BlockSpec((tile, width), lambda i, j: (jnp.maximum(i - nct, 0), 0))]


def _pick_part(n_ctx_tiles, c_ref, l_ref):
    return jnp.where(pl.program_id(0) < n_ctx_tiles, c_ref[...], l_ref[...])


def _pack_halves(y):
    w = y.shape[1] // 2
    bits = pltpu.bitcast(y, jnp.uint32)
    return (bits[:, :w] >> 16) | (bits[:, w:] & jnp.uint32(0xFFFF0000))


def _unpack_halves(p):
    return pltpu.bitcast(p << 16, F32), pltpu.bitcast(p & jnp.uint32(0xFFFF0000), F32)


INPROJ_TM = 512
INPROJ_TN = 1024


def _inproj_kernel(n_ctx_tiles, xc_ref, xl_ref, sc_ref, sh_ref, nw_ref, w_ref, wkt_ref, o_ref, kt_ref):
    x = _pick_part(n_ctx_tiles, xc_ref, xl_ref)
    ms = jnp.mean(x * x, axis=-1, keepdims=True)
    y = x * lax.rsqrt(ms + EPS) * nw_ref[...]
    hb = (y * (1.0 + sc_ref[...]) + sh_ref[...]).astype(BF16)
    kt_ref[...] = lax.dot_general(wkt_ref[...], hb, (((1,), (1,)), ((), ())),
                                  preferred_element_type=F32).astype(BF16)
    for j in range(D_IN // INPROJ_TN):
        cols = slice(j * INPROJ_TN, (j + 1) * INPROJ_TN)
        o_ref[:, cols] = jnp.dot(hb, w_ref[:, cols], preferred_element_type=F32).astype(BF16)


def _inproj_call(geom, l, x_ctx, x_lat, mod6, norm_w, w_in_bf, w_rkt_bf):
    tm = INPROJ_TM
    return pl.pallas_call(
        functools.partial(_inproj_kernel, geom.n_ctx // tm),
        grid=(geom.n_tok // tm,),
        in_specs=_split_in_specs(geom, tm, D_MODEL, 1) + [
                  _mod_spec(geom, l, 1, tm, 1),
                  _mod_spec(geom, l, 0, tm, 1),
                  pl.BlockSpec((1, D_MODEL), lambda i: (0, 0)),
                  pl.BlockSpec((None, D_MODEL, D_IN), lambda i: (l, 0, 0), pipeline_mode=pl.Buffered(1)),
                  pl.BlockSpec((RET_HEADS * RET_QK, D_MODEL), lambda i: (0, 0))],
        out_specs=[pl.BlockSpec((tm, D_IN), lambda i: (i, 0)),
                   pl.BlockSpec((RET_HEADS * RET_QK, tm), lambda i: (0, i))],
        out_shape=[jax.ShapeDtypeStruct((geom.n_tok, D_IN), BF16),
                   jax.ShapeDtypeStruct((RET_HEADS * RET_QK, geom.n_tok), BF16)],
        compiler_params=_cparams(("arbitrary",)),
        name="inproj",
    )(x_ctx, x_lat, mod6, mod6, norm_w.reshape(1, D_MODEL), w_in_bf, w_rkt_bf)


def _gelu_tanh(x):
    return 0.5 * x * (1.0 + jnp.tanh(math.sqrt(2.0 / math.pi) * (x + 0.044715 * (x * x * x))))


def _lru_kernel(geom, reverse, *refs):
    if reverse:
        (xa_ref, xp_ref, xn_ref, cw_ref, cb_ref, wg_ref, bg_ref, sp_ref, h0_ref, perm_ref, permt_ref,
         ga_ref, hf_ref, out_ref, hl_ref, c_scr) = refs
    else:
        (xa_ref, xp_ref, xn_ref, cw_ref, cb_ref, wg_ref, bg_ref, sp_ref, h0_ref, perm_ref,
         out_ref, hl_ref, c_scr) = refs
    g = pl.program_id(0)
    i = geom.n_blocks - 1 - g if reverse else g
    start = geom.seq_start(i)
    end = geom.seq_end(i)

    @pl.when(end if reverse else start)
    def _():
        c_scr[...] = h0_ref[...]

    sub_len = BLK // LRU_SUB
    perm = perm_ref[...]
    x = jnp.dot(perm, xa_ref[...], preferred_element_type=F32)
    pm = jnp.where(start, 0.0, 1.0)
    nm = jnp.where(end, 0.0, 1.0)
    hp = xp_ref.shape[0]
    p1 = xp_ref[hp - 1:hp, :].astype(F32) * pm
    p2 = xp_ref[hp - 2:hp - 1, :].astype(F32) * pm
    n0 = xn_ref[0:1, :].astype(F32) * nm
    row = lax.broadcasted_iota(jnp.int32, x.shape, 0)
    xm1 = jnp.where(row < LRU_SUB, pltpu.roll(x, LRU_SUB + 1, 0), pltpu.roll(x, LRU_SUB, 0))
    xm1 = jnp.where(row == 0, p1, xm1)
    xm2 = jnp.where(row < 2 * LRU_SUB, pltpu.roll(x, 2 * LRU_SUB + 1, 0), pltpu.roll(x, 2 * LRU_SUB, 0))
    xm2 = jnp.where(row == 0, p2, jnp.where(row == LRU_SUB, p1, xm2))
    xp1 = jnp.where(row >= BLK - LRU_SUB, pltpu.roll(x, BLK - LRU_SUB - 1, 0),
                    pltpu.roll(x, BLK - LRU_SUB, 0))
    xp1 = jnp.where(row == BLK - 1, n0, xp1)
    xc = (cw_ref[0:1, :] * xm2 + cw_ref[1:2, :] * xm1 + cw_ref[2:3, :] * x
          + cw_ref[3:4, :] * xp1 + cb_ref[...])

    gt = jnp.dot(xc.astype(BF16), wg_ref[...], preferred_element_type=F32) + bg_ref[...]
    r = jax.nn.sigmoid(gt[:, :D_RNN])
    ig = jax.nn.sigmoid(gt[:, D_RNN:])
    a = jnp.exp(-LRU_C * r * sp_ref[...])
    u = jnp.sqrt(1.0 - a * a) * ig * xc

    h = jnp.zeros((LRU_SUB, D_RNN), F32)
    p = jnp.ones((LRU_SUB, D_RNN), F32)
    h_loc = [None] * sub_len
    p_loc = [None] * sub_len
    for t in (range(sub_len - 1, -1, -1) if reverse else range(sub_len)):
        a_t = a[t * LRU_SUB:(t + 1) * LRU_SUB, :]
        h = a_t * h + u[t * LRU_SUB:(t + 1) * LRU_SUB, :]
        p = a_t * p
        h_loc[t] = h
        p_loc[t] = p
    h_in = [None] * LRU_SUB
    state = c_scr[...]
    for k in (range(LRU_SUB - 1, -1, -1) if reverse else range(LRU_SUB)):
        h_in[k] = state
        state = h[k:k + 1, :] + p[k:k + 1, :] * state
    c_scr[...] = state
    hl_ref[...] = state
    h_in = jnp.concatenate(h_in, axis=0)
    h_full = jnp.concatenate([h_loc[t] + p_loc[t] * h_in for t in range(sub_len)], axis=0)
    if reverse:
        gv = jnp.dot(perm, ga_ref[...], preferred_element_type=F32)
        y = (_gelu_tanh(gv) * (hf_ref[...] + h_full)).astype(BF16)
        out_ref[...] = jnp.dot(permt_ref[...], y, preferred_element_type=F32).astype(BF16)
    else:
        out_ref[...] = h_full


def _lru_call(geom, reverse, proj, conv_w, conv_b, wg, bg, sp, h0, hf=None):
    nb = geom.n_blocks
    halo = 16
    hpb = BLK // halo

    def blk(g):
        return nb - 1 - g if reverse else g

    d = 1 if reverse else 0
    in_specs = [
        pl.BlockSpec((BLK, D_RNN), lambda g: (blk(g), C_XA // D_RNN)),
        pl.BlockSpec((halo, D_RNN), lambda g: (jnp.maximum(blk(g) * hpb - 1, 0), C_XA // D_RNN)),
        pl.BlockSpec((halo, D_RNN), lambda g: (jnp.minimum((blk(g) + 1) * hpb, nb * hpb - 1), C_XA // D_RNN)),
        pl.BlockSpec((CONV_W, D_RNN), lambda g: (0, 0)),
        pl.BlockSpec((1, D_RNN), lambda g: (0, 0)),
        pl.BlockSpec((D_RNN, 2 * D_RNN), lambda g: (0, 0)),
        pl.BlockSpec((1, 2 * D_RNN), lambda g: (0, 0)),
        pl.BlockSpec((1, D_RNN), lambda g: (0, 0)),
        pl.BlockSpec((None, None, 1, D_RNN), lambda g: (geom.seq_id(blk(g)), d, 0, 0)),
    ]
    pos = np.arange(BLK)
    perm_np = np.zeros((BLK, BLK), np.float32)
    perm_np[pos, (pos % LRU_SUB) * (BLK // LRU_SUB) + pos // LRU_SUB] = 1.0
    in_specs.append(pl.BlockSpec((BLK, BLK), lambda g: (0, 0)))
    args = [proj, proj, proj, conv_w, conv_b, wg, bg, sp, h0, jnp.asarray(perm_np, BF16)]
    if reverse:
        in_specs += [pl.BlockSpec((BLK, BLK), lambda g: (0, 0)),
                     pl.BlockSpec((BLK, D_RNN), lambda g: (blk(g), C_GA // D_RNN)),
                     pl.BlockSpec((BLK, D_RNN), lambda g: (blk(g), 0))]
        args += [jnp.asarray(perm_np.T, BF16), proj, hf]
        out_dtype = BF16
    else:
        out_dtype = F32
    scratch = [pltpu.VMEM((1, D_RNN), F32)]
    return pl.pallas_call(
        functools.partial(_lru_kernel, geom, reverse),
        grid=(nb,),
        in_specs=in_specs,
        out_specs=[pl.BlockSpec((BLK, D_RNN), lambda g: (blk(g), 0)),
                   pl.BlockSpec((None, 1, D_RNN), lambda g: (blk(g), 0, 0))],
        out_shape=[jax.ShapeDtypeStruct((geom.n_tok, D_RNN), out_dtype),
                   jax.ShapeDtypeStruct((nb, 1, D_RNN), F32)],
        scratch_shapes=scratch,
        compiler_params=_cparams(("arbitrary",)),
        name="lru_bwd" if reverse else "lru_fwd",
    )(*args)


def _group_rms(x, w, ones):
    xx = x * x
    hi = xx.astype(BF16)
    lo = (xx - hi.astype(F32)).astype(BF16)
    ss = (jnp.dot(hi, ones, preferred_element_type=F32)
          + jnp.dot(lo, ones, preferred_element_type=F32))
    return x * lax.rsqrt(ss * (1.0 / DA_QK) + EPS) * w


def _rope(x, cos, sin):
    lane = lax.broadcasted_iota(jnp.int32, x.shape, 1)
    first = (lane % (2 * ROPE_PAIRS)) < ROPE_PAIRS
    w = x.shape[1]
    partner = jnp.where(first, pltpu.roll(x, w - ROPE_PAIRS, 1), pltpu.roll(x, ROPE_PAIRS, 1))
    return x * cos + partner * sin


def _prep_kernel(rope, *refs):
    if rope:
        dq_ref, dk_ref, qw_ref, kw_ref, ones_ref, cos_ref, sin_ref, q_out, k_out = refs
    else:
        dq_ref, dk_ref, qw_ref, kw_ref, ones_ref, dv_ref, q_out, k_out, kf_out, vf_out = refs
        vf_out[...] = dv_ref[...].astype(F32)
    ones = ones_ref[...]
    q = _group_rms(dq_ref[...].astype(F32), qw_ref[...], ones)
    k = _group_rms(dk_ref[...].astype(F32), kw_ref[...], ones)
    if rope:
        cos = jnp.concatenate([cos_ref[...]] * 4, axis=1)
        sin = jnp.concatenate([sin_ref[...]] * 4, axis=1)
        q = _rope(q, cos, sin)
        k = _rope(k, cos, sin)
    else:
        kf_out[...] = k
    q_out[...] = (q * (DA_QK ** -0.5 * math.log2(math.e))).astype(BF16)
    k_out[...] = k.astype(BF16)


def _prep_call(geom, latent, proj, qw, kw, ones, cos=None, sin=None):
    tm = 512
    w = DA_HEADS * 2 * DA_QK
    if latent:
        n, off = geom.n_lat, geom.n_ctx // tm
        per = geom.dec_seq // tm
    else:
        n, off = geom.n_ctx, 0
    in_specs = [pl.BlockSpec((tm, w), lambda i: (i + off, C_DQ // w)),
                pl.BlockSpec((tm, w), lambda i: (i + off, C_DK // w)),
                pl.BlockSpec((1, w), lambda i: (0, 0)),
                pl.BlockSpec((1, w), lambda i: (0, 0)),
                pl.BlockSpec((w, w), lambda i: (0, 0))]
    args = [proj, proj, qw, kw, ones]
    out_specs = [pl.BlockSpec((tm, w), lambda i: (i, 0)), pl.BlockSpec((tm, w), lambda i: (i, 0))]
    out_shape = [jax.ShapeDtypeStruct((n, w), BF16), jax.ShapeDtypeStruct((n, w), BF16)]
    if latent:
        in_specs += [pl.BlockSpec((tm, 2 * DA_QK), lambda i: (i % per, 0)),
                     pl.BlockSpec((tm, 2 * DA_QK), lambda i: (i % per, 0))]
        args += [cos, sin]
    else:
        in_specs.append(pl.BlockSpec((tm, w), lambda i: (i, C_DV // w)))
        args.append(proj)
        out_specs += [pl.BlockSpec((tm, w), lambda i: (i, 0)), pl.BlockSpec((tm, w), lambda i: (i, 0))]
        out_shape += [jax.ShapeDtypeStruct((n, w), F32), jax.ShapeDtypeStruct((n, w), F32)]
    return pl.pallas_call(
        functools.partial(_prep_kernel, latent),
        grid=(n // tm,),
        in_specs=in_specs, out_specs=out_specs, out_shape=out_shape,
        compiler_params=_cparams(("arbitrary",)),
        name="qk_prep_lat" if latent else "qk_prep_ctx",
    )(*args)


ATT_KC = 256
ATT_TQ = 256
LOG2E = math.log2(math.e)
ATT_SAFE_LOGIT = 60.0


def _attn_kernel(out_scale, has_cache, *refs):
    if has_cache:
        par_ref, q_ref, kc_ref, vc_ref, kl_ref, vl_ref, sw_ref, o_ref, e_scr, o_scr = refs
        srcs = [(kc_ref, vc_ref), (kl_ref, vl_ref)]
    else:
        par_ref, q_ref, kl_ref, vl_ref, sw_ref, o_ref, e_scr, o_scr = refs
        srcs = [(kl_ref, vl_ref)]
    chunks = [(kr, vr, st) for kr, vr in srcs for st in range(0, kr.shape[0], ATT_KC)]
    lam = par_ref[0]
    no_shift = par_ref[1] > 0.5
    tqs = ATT_TQ
    nsub = q_ref.shape[0] // tqs
    nt = (((1,), (1,)), ((), ()))
    half = ATT_KC // 2

    def stacked_q(sb):
        q = q_ref[sb * tqs:(sb + 1) * tqs, :]
        lane = lax.broadcasted_iota(jnp.int32, q.shape, 1)
        zero = jnp.zeros_like(q)
        return jnp.concatenate([jnp.where(lane < DA_QK, q, zero), jnp.where(lane >= DA_QK, q, zero)], axis=0)

    def logits(qq, c):
        kr, vr, st = chunks[c]
        return lax.dot_general(qq, kr[st:st + ATT_KC, :], nt, preferred_element_type=F32)

    def fold(total, e):
        part = e[:, :half] + e[:, half:]
        return part if total is None else total + part

    def row_stats(lsum):
        l = jnp.sum(lsum, axis=-1, keepdims=True)
        l1 = l[0:tqs]
        return l1, lam * l1 / l[tqs:2 * tqs]

    def pv(acc, buf, c, rho):
        kr, vr, st = chunks[c]
        w = (e_scr[buf, c, 0:tqs, :] - rho * e_scr[buf, c, tqs:2 * tqs, :]).astype(BF16)
        t = jnp.dot(w, vr[st:st + ATT_KC, :], preferred_element_type=F32)
        return t if acc is None else acc + t

    nck = len(chunks)

    @pl.when(no_shift)
    def _():
        stats = None
        for sb in range(nsub + 1):
            qq = stacked_q(sb) if sb < nsub else None
            lsum, acc = None, None
            for c in range(nck):
                if sb < nsub:
                    e = jnp.exp2(logits(qq, c))
                    e_scr[sb % 2, c] = e
                    lsum = fold(lsum, e)
                if sb > 0:
                    acc = pv(acc, (sb - 1) % 2, c, stats[1])
            if sb > 0:
                o_scr[(sb - 1) * tqs:sb * tqs, :] = acc / stats[0]
            if sb < nsub:
                stats = row_stats(lsum)

    @pl.when(jnp.logical_not(no_shift))
    def _():
        for sb in range(nsub):
            qq = stacked_q(sb)
            m = None
            for c in range(nck):
                s = logits(qq, c)
                e_scr[0, c] = s
                mc = jnp.max(s, axis=-1, keepdims=True)
                m = mc if m is None else jnp.maximum(m, mc)
            lsum = None
            for c in range(nck):
                e = jnp.exp2(e_scr[0, c] - m)
                e_scr[0, c] = e
                lsum = fold(lsum, e)
            l1, rho = row_stats(lsum)
            acc = None
            for c in range(nck):
                acc = pv(acc, 0, c, rho)
            o_scr[sb * tqs:(sb + 1) * tqs, :] = acc / l1

    o = o_scr[...]
    y = o * lax.rsqrt(jnp.mean(o * o, axis=-1, keepdims=True) + EPS) * sw_ref[...]
    o_ref[...] = (y * out_scale).astype(BF16)


def _attn_call(par, lam_init, q2d, k2d, proj, v_row_off, n_b, t_q, t_kl, tq, subln_w, cache=None):
    hw = 2 * DA_QK
    nq = t_q // tq
    vcol = C_DV // DA_V
    in_specs = [pl.BlockSpec(memory_space=pltpu.SMEM),
                pl.BlockSpec((tq, hw), lambda b, h, qi: (b * nq + qi, h))]
    args = [par, q2d]
    n_chunks = t_kl // ATT_KC
    if cache is not None:
        kc, vc = cache
        p = kc.shape[1]
        n_chunks += p // ATT_KC
        in_specs += [pl.BlockSpec((None, p, hw), lambda b, h, qi: (b, 0, h)),
                     pl.BlockSpec((None, p, DA_V), lambda b, h, qi: (b, 0, h))]
        args += [kc, vc]
    in_specs += [pl.BlockSpec((t_kl, hw), lambda b, h, qi: (b, h)),
                 pl.BlockSpec((t_kl, DA_V), lambda b, h, qi: (v_row_off + b, vcol + h)),
                 pl.BlockSpec((1, DA_V), lambda b, h, qi: (0, 0))]
    args += [k2d, proj, subln_w.reshape(1, DA_V)]
    return pl.pallas_call(
        functools.partial(_attn_kernel, 1.0 - lam_init, cache is not None),
        grid=(n_b, DA_HEADS, nq),
        in_specs=in_specs,
        out_specs=pl.BlockSpec((tq, DA_V), lambda b, h, qi: (b * nq + qi, h)),
        out_shape=jax.ShapeDtypeStruct((n_b * t_q, DA_HEADS * DA_V), BF16),
        scratch_shapes=[pltpu.VMEM((2 if tq > ATT_TQ else 1, n_chunks, 2 * ATT_TQ, ATT_KC), F32),
                        pltpu.VMEM((tq, DA_V), F32)],
        compiler_params=_cparams(("arbitrary", "arbitrary", "arbitrary")),
        name="diff_attn_lat" if cache is not None else "diff_attn_ctx",
    )(*args)


def _ret_state_update(kt, v, kd, cd, s_old):
    parts = []
    for h in range(RET_HEADS):
        rows = slice(h * RET_QK, (h + 1) * RET_QK)
        kh = (kt[rows, :].astype(F32) * kd[rows, :]).astype(BF16)
        parts.append(jnp.dot(kh, v[:, h * RET_V:(h + 1) * RET_V], preferred_element_type=F32))
    return cd * s_old + jnp.concatenate(parts, axis=0)


def _ret_bwd_kernel(geom, kt_ref, v_ref, kd_ref, cd_ref, s0_ref, sstart_ref, send_ref, s_scr):
    i = geom.n_blocks - 1 - pl.program_id(0)

    @pl.when(geom.seq_end(i))
    def _():
        s_scr[...] = s0_ref[...]

    s_old = s_scr[...]
    sstart_ref[...] = s_old
    kt = kt_ref[...] * jnp.asarray(RET_QK ** -0.5, BF16)
    s_new = _ret_state_update(kt, v_ref[...], kd_ref[...], cd_ref[...], s_old)
    s_scr[...] = s_new
    send_ref[...] = s_new


def _ret_bwd_call(geom, proj, rkt, kd_b, cd_b, s0):
    nb = geom.n_blocks
    hs = RET_HEADS * RET_QK

    def blk(g):
        return nb - 1 - g

    return pl.pallas_call(
        functools.partial(_ret_bwd_kernel, geom),
        grid=(nb,),
        in_specs=[pl.BlockSpec((hs, BLK), lambda g: (0, blk(g))),
                  pl.BlockSpec((BLK, RET_HEADS * RET_V), lambda g: (blk(g), C_RV // (RET_HEADS * RET_V))),
                  pl.BlockSpec((hs, BLK), lambda g: (0, 0)),
                  pl.BlockSpec((hs, RET_V), lambda g: (0, 0)),
                  pl.BlockSpec((None, None, hs, RET_V), lambda g: (geom.seq_id(blk(g)), 1, 0, 0))],
        out_specs=[pl.BlockSpec((None, hs, RET_V), lambda g: (blk(g), 0, 0)),
                   pl.BlockSpec((None, hs, RET_V), lambda g: (blk(g), 0, 0))],
        out_shape=[jax.ShapeDtypeStruct((nb, hs, RET_V), F32),
                   jax.ShapeDtypeStruct((nb, hs, RET_V), F32)],
        scratch_shapes=[pltpu.VMEM((hs, RET_V), F32)],
        compiler_params=_cparams(("arbitrary",)),
        name="ret_bwd_state",
    )(rkt, proj, kd_b, cd_b, s0)


def _ret_main_kernel(geom, q_ref, kt_ref, v_ref, g_ref, dsum_ref, qdf_ref, qdb_ref, kd_ref, cd_ref,
                     s0_ref, sb_ref, o_ref, send_ref, s_scr):
    i = pl.program_id(0)

    @pl.when(geom.seq_start(i))
    def _():
        s_scr[...] = s0_ref[...]

    s_f = s_scr[...]
    s_fb = s_f.astype(BF16)
    s_bb = sb_ref[...].astype(BF16)
    q = q_ref[...].astype(F32)
    kt = kt_ref[...] * jnp.asarray(RET_QK ** -0.5, BF16)
    v = v_ref[...]
    lane = lax.broadcasted_iota(jnp.int32, q.shape, 1)
    for h in range(RET_HEADS):
        in_head = (lane >= h * RET_QK) & (lane < (h + 1) * RET_QK)
        qh = jnp.where(in_head, q, 0.0)
        vh = v[:, h * RET_V:(h + 1) * RET_V]
        sc = jnp.dot(qh.astype(BF16), kt, preferred_element_type=F32) * dsum_ref[h]
        o = jnp.dot(sc.astype(BF16), vh, preferred_element_type=F32)
        o += jnp.dot((qh * qdf_ref[...]).astype(BF16), s_fb, preferred_element_type=F32)
        o += jnp.dot((qh * qdb_ref[...]).astype(BF16), s_bb, preferred_element_type=F32)
        y = o * lax.rsqrt(jnp.mean(o * o, axis=-1, keepdims=True) + EPS)
        gv = g_ref[:, h * RET_V:(h + 1) * RET_V].astype(F32)
        o_ref[:, h * RET_V:(h + 1) * RET_V] = (y * (gv * jax.nn.sigmoid(gv))).astype(BF16)
    s_new = _ret_state_update(kt, v, kd_ref[...], cd_ref[...], s_f)
    s_scr[...] = s_new
    send_ref[...] = s_new


def _ret_main_call(geom, proj, rkt, dsum, qdf, qdb, kd_f, cd_f, s0, sb_start):
    nb = geom.n_blocks
    hs = RET_HEADS * RET_QK
    hv = RET_HEADS * RET_V
    return pl.pallas_call(
        functools.partial(_ret_main_kernel, geom),
        grid=(nb,),
        in_specs=[pl.BlockSpec((BLK, hs), lambda g: (g, C_RQ // hs)),
                  pl.BlockSpec((hs, BLK), lambda g: (0, g)),
                  pl.BlockSpec((BLK, hv), lambda g: (g, C_RV // hv)),
                  pl.BlockSpec((BLK, hv), lambda g: (g, C_RG // hv)),
                  pl.BlockSpec((RET_HEADS, BLK, BLK), lambda g: (0, 0, 0)),
                  pl.BlockSpec((BLK, hs), lambda g: (0, 0)),
                  pl.BlockSpec((BLK, hs), lambda g: (0, 0)),
                  pl.BlockSpec((hs, BLK), lambda g: (0, 0)),
                  pl.BlockSpec((hs, RET_V), lambda g: (0, 0)),
                  pl.BlockSpec((None, None, hs, RET_V), lambda g: (geom.seq_id(g), 0, 0, 0)),
                  pl.BlockSpec((None, hs, RET_V), lambda g: (g, 0, 0))],
        out_specs=[pl.BlockSpec((BLK, hv), lambda g: (g, 0)),
                   pl.BlockSpec((None, hs, RET_V), lambda g: (g, 0, 0))],
        out_shape=[jax.ShapeDtypeStruct((geom.n_tok, hv), BF16),
                   jax.ShapeDtypeStruct((nb, hs, RET_V), F32)],
        scratch_shapes=[pltpu.VMEM((hs, RET_V), F32)],
        compiler_params=_cparams(("arbitrary",)),
        name="ret_main",
    )(proj, rkt, proj, proj, dsum, qdf, qdb, kd_f, cd_f, s0, sb_start)


def _ret_tables(ret_decay_l):
    log_g = jax.nn.log_sigmoid(ret_decay_l.astype(F32))
    pos = jnp.arange(BLK, dtype=F32)
    diff = pos[:, None] - pos[None, :]
    lf = log_g[0][:, None, None]
    lb = log_g[1][:, None, None]
    dsum = (jnp.where(diff >= 0, jnp.exp(jnp.maximum(diff, 0.0)[None] * lf), 0.0)
            + jnp.where(diff <= 0, jnp.exp(jnp.maximum(-diff, 0.0)[None] * lb), 0.0))

    def per_lane(e, lg):
        return jnp.repeat(jnp.exp(e[:, None] * lg[None, :]), RET_QK, axis=1)

    qdf = per_lane(pos + 1.0, log_g[0])
    qdb = per_lane(BLK - pos, log_g[1])
    kd_f = per_lane(BLK - 1.0 - pos, log_g[0]).T
    kd_b = per_lane(pos, log_g[1]).T
    cd_f = jnp.broadcast_to(jnp.repeat(jnp.exp(BLK * log_g[0]), RET_QK)[:, None], (RET_HEADS * RET_QK, RET_V))
    cd_b = jnp.broadcast_to(jnp.repeat(jnp.exp(BLK * log_g[1]), RET_QK)[:, None], (RET_HEADS * RET_QK, RET_V))
    return dsum, qdf, qdb, kd_f, kd_b, cd_f, cd_b


def _merge_kernel(n_ctx_tiles, ba_ref, bbc_ref, bbl_ref, bc_ref, g0_ref, g1_ref, g2_ref, xc_ref, xl_ref,
                  gate_ref, sc_ref, sh_ref, nw_ref, wb_ref, wo_ref, rhi_ref, rlo_ref, x1_ref, h2_ref, h2p_ref,
                  lt_ref):
    branches = (ba_ref[...], _pick_part(n_ctx_tiles, bbc_ref, bbl_ref), bc_ref[...])
    acc = None
    for br, (b, g_ref) in enumerate(zip(branches, (g0_ref, g1_ref, g2_ref))):
        p = jnp.dot(b, wb_ref[br], preferred_element_type=F32)
        t = (0.5 * jnp.tanh(0.5 * g_ref[...].astype(F32)) + 0.5) * p
        acc = t if acc is None else acc + t
    m = jnp.dot(acc.astype(BF16), wo_ref[...], preferred_element_type=F32)
    x1 = _pick_part(n_ctx_tiles, xc_ref, xl_ref) + gate_ref[...] * m
    x1_ref[...] = x1
    ms = jnp.mean(x1 * x1, axis=-1, keepdims=True)
    h2 = x1 * lax.rsqrt(ms + EPS) * nw_ref[...] * (1.0 + sc_ref[...]) + sh_ref[...]
    h2b = h2.astype(BF16)
    h2_ref[...] = h2b
    h2p_ref[...] = _pack_halves(h2b.astype(F32))
    h2lo = (h2 - h2b.astype(F32)).astype(BF16)
    nt = (((1,), (1,)), ((), ()))
    lt_ref[...] = (lax.dot_general(rhi_ref[...], h2b, nt, preferred_element_type=F32)
                   + lax.dot_general(rhi_ref[...], h2lo, nt, preferred_element_type=F32)
                   + lax.dot_general(rlo_ref[...], h2b, nt, preferred_element_type=F32))


def _merge_call(geom, l, ba, bb_ctx, bb_lat, bc, proj, x_ctx, x_lat, mod6, norm2_w, wb_bf, wo_bf, r_hi, r_lo):
    tm = 512
    gcol = C_GL // D_MODEL
    full = lambda shape: pl.BlockSpec(shape, lambda i: tuple(0 for _ in shape))
    tok = lambda w: pl.BlockSpec((tm, w), lambda i: (i, 0))
    return pl.pallas_call(
        functools.partial(_merge_kernel, geom.n_ctx // tm),
        grid=(geom.n_tok // tm,),
        in_specs=[tok(BRANCH_W)] + _split_in_specs(geom, tm, BRANCH_W, 1) + [tok(BRANCH_W),
                  pl.BlockSpec((tm, D_MODEL), lambda i: (i, gcol)),
                  pl.BlockSpec((tm, D_MODEL), lambda i: (i, gcol + 1)),
                  pl.BlockSpec((tm, D_MODEL), lambda i: (i, gcol + 2))]
                 + _split_in_specs(geom, tm, D_MODEL, 1) + [
                  _mod_spec(geom, l, 2, tm, 1), _mod_spec(geom, l, 4, tm, 1), _mod_spec(geom, l, 3, tm, 1),
                  full((1, D_MODEL)),
                  full((N_BRANCH, BRANCH_W, D_MODEL)), full((D_MODEL, D_MODEL)),
                  full((N_EXPERTS, D_MODEL)), full((N_EXPERTS, D_MODEL))],
        out_specs=[tok(D_MODEL), tok(D_MODEL), tok(D_MODEL // 2), pl.BlockSpec((N_EXPERTS, tm), lambda i: (0, i))],
        out_shape=[jax.ShapeDtypeStruct((geom.n_tok, D_MODEL), F32),
                   jax.ShapeDtypeStruct((geom.n_tok, D_MODEL), BF16),
                   jax.ShapeDtypeStruct((geom.n_tok, D_MODEL // 2), jnp.uint32),
                   jax.ShapeDtypeStruct((N_EXPERTS, geom.n_tok), F32)],
        compiler_params=_cparams(("arbitrary",)),
        name="merge_out",
    )(ba, bb_ctx, bb_lat, bc, proj, proj, proj, x_ctx, x_lat, mod6, mod6, mod6,
      norm2_w.reshape(1, D_MODEL), wb_bf, wo_bf, r_hi, r_lo)


def _router_kernel(lt_ref, bias_ref, ltri_ref, utri_ref, g_ref, slot_ref, cnt_ref, cnt_scr):
    per = N_EXPERTS // N_GROUPS
    tm = lt_ref.shape[1]
    scores = jax.nn.sigmoid(lt_ref[...])
    biased = scores + bias_ref[...]
    b3 = biased.reshape(N_GROUPS, per, tm)
    neg = jnp.float32(-jnp.inf)
    m1 = jnp.max(b3, axis=1, keepdims=True)
    is_m1 = b3 == m1
    cnt = jnp.sum(is_m1.astype(F32), axis=1, keepdims=True)
    m2 = jnp.max(jnp.where(is_m1, neg, b3), axis=1, keepdims=True)
    grp = (m1 + jnp.where(cnt >= 2.0, m1, m2)).reshape(N_GROUPS, tm)
    gidx = lax.broadcasted_iota(jnp.int32, (N_GROUPS, tm), 0)
    grank = jnp.zeros((N_GROUPS, tm), F32)
    for g2 in range(N_GROUPS):
        other = grp[g2:g2 + 1, :]
        ahead = (other > grp) | ((other == grp) & (gidx > g2))
        grank += ahead.astype(F32)
    gsel = (grank < float(TOPK_GROUPS)).astype(F32)
    emask = jnp.broadcast_to(gsel.reshape(N_GROUPS, 1, tm), (N_GROUPS, per, tm)).reshape(N_EXPERTS, tm)
    masked = jnp.where(emask > 0.0, biased, neg)
    eidx = lax.broadcasted_iota(jnp.int32, (N_EXPERTS, tm), 0)
    erank = jnp.zeros((N_EXPERTS, tm), F32)
    for e2 in range(N_EXPERTS):
        other = masked[e2:e2 + 1, :]
        ahead = (other > masked) | ((other == masked) & (eidx > e2))
        erank += ahead.astype(F32)
    sel = erank < float(TOP_K)
    w = jnp.where(sel, scores, 0.0)
    gates_t = w / jnp.sum(w, axis=0, keepdims=True) * ROUTED_SCALE

    @pl.when(pl.program_id(0) == 0)
    def _():
        cnt_scr[...] = jnp.zeros_like(cnt_scr)

    selb = sel.astype(BF16)
    slot = jnp.dot(ltri_ref[...], selb, preferred_element_type=F32)
    carry = cnt_scr[:, 0:1]
    rank = jnp.dot(selb, utri_ref[...], preferred_element_type=F32) + carry
    cnt_new = cnt_scr[...] + jnp.sum(sel.astype(F32), axis=1, keepdims=True)
    cnt_scr[...] = cnt_new
    cnt_ref[...] = cnt_new
    eid_f = eidx.astype(F32)
    g_rows, e_rows, r_rows = [], [], []
    for k in range(TOP_K):
        mk = jnp.where(sel & (slot == float(k)), 1.0, 0.0)
        g_rows.append(jnp.sum(mk * gates_t, axis=0, keepdims=True))
        e_rows.append(jnp.sum(mk * eid_f, axis=0, keepdims=True))
        r_rows.append(jnp.sum(mk * rank, axis=0, keepdims=True))
    slot_ref[...] = jnp.concatenate(e_rows + r_rows, axis=0).astype(jnp.int32)
    pad = jnp.zeros((GATE_W - TOP_K, tm), F32)
    g_ref[...] = jnp.concatenate(g_rows + [pad], axis=0).T


ROUTER_TM = 512


def _router_call(geom, logits_t, bias):
    tm = ROUTER_TM
    ltri = jnp.asarray(np.tril(np.ones((N_EXPERTS, N_EXPERTS), np.float32), -1), BF16)
    utri = jnp.asarray(np.triu(np.ones((tm, tm), np.float32), 1), BF16)
    return pl.pallas_call(
        _router_kernel,
        grid=(geom.n_tok // tm,),
        in_specs=[pl.BlockSpec((N_EXPERTS, tm), lambda i: (0, i)),
                  pl.BlockSpec((N_EXPERTS, 1), lambda i: (0, 0)),
                  pl.BlockSpec((N_EXPERTS, N_EXPERTS), lambda i: (0, 0)),
                  pl.BlockSpec((tm, tm), lambda i: (0, 0))],
        out_specs=[pl.BlockSpec((tm, GATE_W), lambda i: (i, 0)),
                   pl.BlockSpec((2 * TOP_K, tm), lambda i: (0, i)),
                   pl.BlockSpec((N_EXPERTS, GATE_W), lambda i: (0, 0))],
        out_shape=[jax.ShapeDtypeStruct((geom.n_tok, GATE_W), F32),
                   jax.ShapeDtypeStruct((2 * TOP_K, geom.n_tok), jnp.int32),
                   jax.ShapeDtypeStruct((N_EXPERTS, GATE_W), F32)],
        scratch_shapes=[pltpu.VMEM((N_EXPERTS, GATE_W), F32)],
        compiler_params=_cparams(("arbitrary",)),
        name="router",
    )(logits_t, bias.reshape(N_EXPERTS, 1), ltri, utri)


MOE_TR = 512
SC_CORES = 2
SC_SUBCORES = 16
SC_CHUNK = 64


def _sc_worker_base(rows_per_worker):
    wid = lax.axis_index("s") * SC_CORES + lax.axis_index("c")
    return wid * rows_per_worker


def _sc_scatter_rows(table, pos_flat, n_slots, n_rows_out):
    n, d = table.shape
    nw = SC_CORES * SC_SUBCORES
    assert n % (nw * SC_CHUNK) == 0
    per_w = n // nw
    mesh = plsc.VectorSubcoreMesh(core_axis_name="c", subcore_axis_name="s")

    @functools.partial(
        pl.kernel, mesh=mesh,
        out_type=jax.ShapeDtypeStruct((n_rows_out, d), table.dtype),
        scratch_types=[[pltpu.VMEM((SC_CHUNK,), jnp.int32) for _ in range(n_slots)],
                       pltpu.VMEM((SC_CHUNK, d), table.dtype),
                       pltpu.SemaphoreType.DMA],
    )
    def scatter(table_hbm, pos_hbm, out_hbm, idx_v, rows_v, sem):
        base = _sc_worker_base(per_w)

        @pl.loop(0, per_w // SC_CHUNK)
        def _(ci):
            off = pl.multiple_of(base + ci * SC_CHUNK, 8)
            for k in range(n_slots):
                pltpu.sync_copy(pos_hbm.at[pl.ds(pl.multiple_of(k * n + off, 8), SC_CHUNK)], idx_v[k])
            pltpu.sync_copy(table_hbm.at[pl.ds(off, SC_CHUNK)], rows_v)
            copies = [pltpu.make_async_copy(rows_v, out_hbm.at[idx_v[k]], sem) for k in range(n_slots)]
            for cp in copies:
                cp.start()
            for cp in copies:
                cp.wait()

    return scatter(table, pos_flat)


def _sc_gather_rows(table, idx):
    b = idx.shape[0]
    d = table.shape[1]
    nw = SC_CORES * SC_SUBCORES
    nbuf = 2
    assert b % (nw * SC_CHUNK * nbuf) == 0
    per_w = b // nw
    n_chunks = per_w // SC_CHUNK
    mesh = plsc.VectorSubcoreMesh(core_axis_name="c", subcore_axis_name="s")

    @functools.partial(
        pl.kernel, mesh=mesh,
        out_type=jax.ShapeDtypeStruct((b, d), table.dtype),
        scratch_types=[pltpu.VMEM((per_w,), jnp.int32),
                       [pltpu.VMEM((SC_CHUNK, d), table.dtype) for _ in range(nbuf)],
                       [pltpu.SemaphoreType.DMA for _ in range(nbuf)],
                       [pltpu.SemaphoreType.DMA for _ in range(nbuf)]],
    )
    def gather(table_hbm, idx_hbm, out_hbm, idx_v, rows, gsem, wsem):
        base = _sc_worker_base(per_w)
        pltpu.sync_copy(idx_hbm.at[pl.ds(pl.multiple_of(base, 8), per_w)], idx_v)

        def fetch(ci, slot):
            src = table_hbm.at[idx_v.at[pl.ds(pl.multiple_of(ci * SC_CHUNK, 8), SC_CHUNK)]]
            return pltpu.make_async_copy(src, rows[slot], gsem[slot])

        def put(ci, slot):
            dst = out_hbm.at[pl.ds(pl.multiple_of(base + ci * SC_CHUNK, 8), SC_CHUNK)]
            return pltpu.make_async_copy(rows[slot], dst, wsem[slot])

        for slot in range(nbuf):
            fetch(slot, slot).start()

        @pl.loop(0, n_chunks, step=nbuf)
        def _(c0):
            for slot in range(nbuf):
                ci = c0 + slot
                fetch(ci, slot).wait()
                put(ci, slot).start()
                put(ci, slot).wait()

                @pl.when(ci + nbuf < n_chunks)
                def _():
                    fetch(ci + nbuf, slot).start()

    return gather(table, idx)


def _route_positions(n_tok, slots, counts):
    cnt = counts[:, 0].astype(jnp.int32)
    cnt_pad = ((cnt + MOE_TR - 1) // MOE_TR) * MOE_TR
    off_end = jnp.cumsum(cnt_pad)
    off = off_end - cnt_pad
    eid, rank = slots[:TOP_K], slots[TOP_K:]
    eids = jnp.arange(N_EXPERTS, dtype=jnp.int32)
    pos = jnp.sum(jnp.where(eid[..., None] == eids, off, 0), axis=-1) + rank
    n_tiles = (TOP_K * n_tok) // MOE_TR + N_EXPERTS
    tile_start = jnp.arange(n_tiles, dtype=jnp.int32) * MOE_TR
    tile_expert = jnp.sum((tile_start[:, None] >= off_end[None, :]).astype(jnp.int32), axis=1)
    tile_expert = jnp.minimum(tile_expert, N_EXPERTS - 1)
    n_used = (off_end[-1] // MOE_TR).reshape(1)
    tile_idx = jnp.arange(n_tiles, dtype=jnp.int32)
    used = tile_idx < n_used[0]
    prev = jnp.concatenate([jnp.full((1,), -1, jnp.int32), tile_expert[:-1]])
    first = jnp.logical_and(used, tile_expert != prev)
    parity = (jnp.cumsum(first.astype(jnp.int32)) - 1) % 2
    later = jnp.logical_and(eids[None, :] > eids[:, None], (cnt_pad > 0)[None, :])
    next_e = jnp.min(jnp.where(later, eids[None, :], N_EXPERTS), axis=1)
    nxt = jnp.sum(jnp.where(tile_expert[:, None] == eids, next_e, 0), axis=1)
    has_next = jnp.logical_and(first, nxt < N_EXPERTS)
    sched = (tile_expert, n_used, first.astype(jnp.int32), jnp.minimum(nxt, N_EXPERTS - 1).astype(jnp.int32),
             jnp.maximum(parity, 0).astype(jnp.int32), has_next.astype(jnp.int32))
    return pos, sched, n_tiles


def _expert_ffn(x_lo, x_hi, gu, dn):
    half = D_MODEL // 2
    a = (jnp.dot(x_lo, gu[0:half, :], preferred_element_type=F32)
         + jnp.dot(x_hi, gu[half:, :], preferred_element_type=F32))
    hg = a[:, :D_EXPERT]
    act = (hg * jax.nn.sigmoid(hg)) * a[:, D_EXPERT:]
    return jnp.dot(act.astype(BF16), dn, preferred_element_type=F32)


def _experts_kernel(l, te_ref, nu_ref, first_ref, nxt_ref, par_ref, hasn_ref, x_ref, gu_hbm, dn_hbm, y_ref,
                    gu_f, dn_f, gu_b, dn_b, sem):
    i = pl.program_id(0)

    def fetch(e, slot):
        return (pltpu.make_async_copy(gu_hbm.at[l, e], gu_f.at[slot], sem.at[0, slot]),
                pltpu.make_async_copy(dn_hbm.at[l, e], dn_f.at[slot], sem.at[1, slot]))

    @pl.when(jnp.logical_and(i == 0, nu_ref[0] > 0))
    def _():
        for cp in fetch(te_ref[0], par_ref[0]):
            cp.start()

    @pl.when(first_ref[i] == 1)
    def _():
        slot = par_ref[i]
        for cp in fetch(te_ref[i], slot):
            cp.wait()

        @pl.when(hasn_ref[i] == 1)
        def _():
            for cp in fetch(nxt_ref[i], 1 - slot):
                cp.start()

        gu_b[...] = gu_f[slot].astype(BF16)
        dn_b[...] = dn_f[slot].astype(BF16)

    @pl.when(i < nu_ref[0])
    def _():
        lo, hi = _unpack_halves(x_ref[...])
        y = _expert_ffn(lo.astype(BF16), hi.astype(BF16), gu_b[...], dn_b[...])
        y_ref[...] = _pack_halves(y.astype(BF16).astype(F32))

    @pl.when(i >= nu_ref[0])
    def _():
        y_ref[...] = jnp.zeros_like(y_ref)


def _experts_call(l, xs, sched, n_tiles, w_gu, w_dn):
    half = D_MODEL // 2
    grid_spec = pltpu.PrefetchScalarGridSpec(
        num_scalar_prefetch=len(sched),
        grid=(n_tiles,),
        in_specs=[pl.BlockSpec((MOE_TR, half), lambda i, te, nu, *_: (jnp.minimum(i, jnp.maximum(nu[0], 1) - 1), 0)),
                  pl.BlockSpec(memory_space=pl.ANY),
                  pl.BlockSpec(memory_space=pl.ANY)],
        out_specs=pl.BlockSpec((MOE_TR, half), lambda i, *_: (i, 0)),
        scratch_shapes=[pltpu.VMEM((2, D_MODEL, 2 * D_EXPERT), w_gu.dtype),
                        pltpu.VMEM((2, D_EXPERT, D_MODEL), w_dn.dtype),
                        pltpu.VMEM((D_MODEL, 2 * D_EXPERT), BF16),
                        pltpu.VMEM((D_EXPERT, D_MODEL), BF16),
                        pltpu.SemaphoreType.DMA((2, 2))],
    )
    return pl.pallas_call(
        functools.partial(_experts_kernel, l),
        grid_spec=grid_spec,
        out_shape=jax.ShapeDtypeStruct((n_tiles * MOE_TR, half), jnp.uint32),
        compiler_params=_cparams(("arbitrary",)),
        name="moe_experts",
    )(*sched, xs, w_gu, w_dn)


MOE_OUT_PARTS = 2


def _moe_out_kernel(n_ctx_tiles, tile0, first, *refs):
    if first:
        yt_ref, g_ref, h_ref, sgu_ref, sdn_ref, x1_ref, gate_ref, oc_ref, ol_ref = refs
    else:
        yt_ref, g_ref, h_ref, sgu_ref, sdn_ref, x1_ref, gate_ref, _, ol_ref = refs
    i = pl.program_id(0) + tile0
    gts = g_ref[...]
    lane = lax.broadcasted_iota(jnp.int32, gts.shape, 1)
    acc_lo, acc_hi = None, None
    for k in range(TOP_K):
        ge = jnp.sum(jnp.where(lane == k, gts, 0.0), axis=1, keepdims=True)
        lo, hi = _unpack_halves(yt_ref[k])
        acc_lo = ge * lo if acc_lo is None else acc_lo + ge * lo
        acc_hi = ge * hi if acc_hi is None else acc_hi + ge * hi
    routed = jnp.concatenate([acc_lo, acc_hi], axis=1)
    h = h_ref[...]
    half = D_MODEL // 2
    shared = _expert_ffn(h[:, :half], h[:, half:], sgu_ref[...], sdn_ref[...])
    y = x1_ref[...] + gate_ref[...] * (routed + shared)
    if first:
        @pl.when(i < n_ctx_tiles)
        def _():
            oc_ref[...] = y

        @pl.when(i >= n_ctx_tiles)
        def _():
            ol_ref[...] = y
    else:
        ol_ref[...] = y


def _moe_out_call(geom, l, part, yt, gates, h2, sgu_bf, sdn_bf, x1, mod6, prev_lat=None):
    tm = 512
    nct = geom.n_ctx // tm
    n_part = geom.n_tok // tm // MOE_OUT_PARTS
    t0 = part * n_part
    first = part == 0
    assert nct <= n_part
    half = D_MODEL // 2
    tok = lambda w: pl.BlockSpec((tm, w), lambda i: (i + t0, 0))
    in_specs = [pl.BlockSpec((TOP_K, tm, half), lambda i: (0, i, 0)),
                tok(GATE_W), tok(D_MODEL),
                pl.BlockSpec((None, D_MODEL, 2 * D_EXPERT), lambda i: (l, 0, 0)),
                pl.BlockSpec((None, D_EXPERT, D_MODEL), lambda i: (l, 0, 0)),
                tok(D_MODEL),
                pl.BlockSpec((None, None, None, 1, D_MODEL), lambda i: (l, geom.mod_row(i + t0, tm), 5, 0, 0))]
    args = [yt, gates, h2, sgu_bf, sdn_bf, x1, mod6]
    lat_shape = jax.ShapeDtypeStruct((geom.n_lat, D_MODEL), F32)
    if first:
        out_specs = [pl.BlockSpec((tm, D_MODEL), lambda i: (jnp.minimum(i, nct - 1), 0)),
                     pl.BlockSpec((tm, D_MODEL), lambda i: (jnp.maximum(i - nct, 0), 0))]
        out_shape = [jax.ShapeDtypeStruct((geom.n_ctx, D_MODEL), F32), lat_shape]
        aliases = {}
    else:
        in_specs.append(pl.BlockSpec(memory_space=pl.ANY))
        args.append(prev_lat)
        out_specs = [pl.BlockSpec((tm, D_MODEL), lambda i: (i + t0 - nct, 0))]
        out_shape = [lat_shape]
        aliases = {len(args) - 1: 0}
    return pl.pallas_call(
        functools.partial(_moe_out_kernel, nct, t0, first),
        grid=(n_part,),
        in_specs=in_specs, out_specs=out_specs, out_shape=out_shape,
        input_output_aliases=aliases,
        compiler_params=_cparams(("arbitrary",)),
        name="moe_out",
    )(*args)


def _moe(geom, l, h2, h2p, gates, slots, counts, w_gu, w_dn, sgu_bf, sdn_bf, x1, mod6):
    pos, sched, n_tiles = _route_positions(geom.n_tok, slots, counts)
    xs = _sc_scatter_rows(h2p, pos.reshape(-1), TOP_K, n_tiles * MOE_TR)
    ys = _experts_call(l, xs, sched, n_tiles, w_gu, w_dn)
    n_part = geom.n_tok // MOE_OUT_PARTS
    y_ctx, y_lat = None, None
    for part in range(MOE_OUT_PARTS):
        pos_p = pos[:, part * n_part:(part + 1) * n_part].reshape(-1)
        yt = _sc_gather_rows(ys, pos_p).reshape(TOP_K, n_part, D_MODEL // 2)
        outs = _moe_out_call(geom, l, part, yt, gates, h2, sgu_bf, sdn_bf, x1, mod6, y_lat)
        if part == 0:
            y_ctx, y_lat = outs
        else:
            (y_lat,) = outs
    return y_ctx, y_lat


def _rope_tables(dec_seq):
    rows = dec_seq // GRID_W
    row = jnp.repeat(jnp.arange(rows, dtype=F32), GRID_W)
    col = jnp.tile(jnp.arange(GRID_W, dtype=F32), rows)
    inv = ROPE_BASE ** (-jnp.arange(ROPE_PAIRS, dtype=F32) / ROPE_PAIRS)
    ar = row[:, None] * inv[None, :]
    ac = col[:, None] * inv[None, :]
    cos64 = jnp.concatenate([jnp.cos(ar), jnp.cos(ar), jnp.cos(ac), jnp.cos(ac)], axis=1)
    sin64 = jnp.concatenate([-jnp.sin(ar), jnp.sin(ar), -jnp.sin(ac), jnp.sin(ac)], axis=1)
    return jnp.tile(cos64, (1, 2)), jnp.tile(sin64, (1, 2))


def _block_diag_gate(wg_dir):
    eye = jnp.eye(LRU_BLOCKS, dtype=F32)
    dense = jnp.einsum('gnij,nm->gnimj', wg_dir.astype(F32), eye).reshape(2, D_RNN, D_RNN)
    return jnp.concatenate([dense[0], dense[1]], axis=1)


def kernel(x_prompt, x_sample, cache_k, cache_v, state_lru, state_ret, c, c_ctx, ada_w, ada_b, norm1_w, norm2_w, w_in, conv_w, conv_b, lru_gate_w, lru_gate_b, lru_lambda, q_norm_w, k_norm_w, diff_lambda, subln_w, ret_decay, w_branch, w_out, router_w, router_bias, w_exp_gu, w_exp_down, w_sh_gu, w_sh_down):
    batch, seq, _ = x_prompt.shape
    dec_batch, dec_seq, _ = x_sample.shape
    assert 1 + dec_batch <= MOD_ROWS
    geom = _Geom(batch, seq, dec_batch, dec_seq)
    hs = RET_HEADS * RET_QK
    aw = DA_HEADS * 2 * DA_QK

    x_ctx = x_prompt.reshape(geom.n_ctx, D_MODEL)
    x_lat = x_sample.reshape(geom.n_lat, D_MODEL)
    cvec = jnp.zeros((MOD_ROWS, D_MODEL), F32).at[0].set(c_ctx).at[1:1 + dec_batch].set(c)
    mod6 = _ada_call(cvec, ada_w, ada_b).reshape(DEPTH, MOD_ROWS, 6, 1, D_MODEL)

    ones_bd = jnp.kron(jnp.eye(aw // DA_QK, dtype=F32), jnp.ones((DA_QK, DA_QK), F32)).astype(BF16)
    cos_t, sin_t = _rope_tables(dec_seq)

    w_in_bf = w_in.astype(BF16)
    sgu_bf, sdn_bf = w_sh_gu.astype(BF16), w_sh_down.astype(BF16)

    ks, vs, lrus, rets = [], [], [], []
    for l in range(DEPTH):
        lam_init = 0.8 - 0.6 * math.exp(-0.3 * l)
        w_rkt_bf = w_in[l][:, C_RK:C_RK + hs].T.astype(BF16)
        proj, rkt = _inproj_call(geom, l, x_ctx, x_lat, mod6, norm1_w[l], w_in_bf, w_rkt_bf)

        sp = jax.nn.softplus(-lru_lambda[l].astype(F32))
        h0 = jnp.concatenate([jnp.zeros((batch, 2, D_RNN), F32), state_lru[:, l].astype(F32)], axis=0)
        h0 = h0.reshape(geom.n_seq, 2, 1, D_RNN)
        cb = conv_b[l].reshape(1, D_RNN)
        lru_args = []
        for d in range(2):
            lru_args.append((_block_diag_gate(lru_gate_w[l, d]).astype(BF16),
                             lru_gate_b[l, d].reshape(1, 2 * D_RNN), sp[d].reshape(1, D_RNN)))
        hf, hf_last = _lru_call(geom, False, proj, conv_w[l], cb, *lru_args[0], h0)
        branch_a, hb_last = _lru_call(geom, True, proj, conv_w[l], cb, *lru_args[1], h0, hf)

        qw = jnp.tile(q_norm_w[l], aw // DA_QK).reshape(1, aw)
        kw = jnp.tile(k_norm_w[l], aw // DA_QK).reshape(1, aw)
        q_c, k_c, k_c32, v_c32 = _prep_call(geom, False, proj, qw, kw, ones_bd)
        q_l, k_l = _prep_call(geom, True, proj, qw, kw, ones_bd, cos_t, sin_t)
        lam_p = diff_lambda[l].astype(F32)
        lam = jnp.exp(jnp.sum(lam_p[0] * lam_p[1])) - jnp.exp(jnp.sum(lam_p[2] * lam_p[3])) + lam_init
        q_bound = DA_QK * jnp.max(jnp.square(q_norm_w[l].astype(F32))) * (DA_QK ** -0.5 * LOG2E) ** 2
        k_bound = DA_QK * jnp.max(jnp.square(k_norm_w[l].astype(F32)))
        kc32 = cache_k[:, l].astype(F32)
        kc_bound = jnp.maximum(k_bound, jnp.max(jnp.sum(jnp.square(kc32), axis=-1)))

        def attn_par(kb):
            ok = (q_bound * kb * 1.05 < ATT_SAFE_LOGIT ** 2).astype(F32)
            return jnp.stack([lam, ok])

        assert geom.n_ctx % dec_seq == 0
        cache = (kc32.reshape(dec_batch, -1, aw).astype(BF16),
                 cache_v[:, l].reshape(dec_batch, -1, DA_HEADS * DA_V).astype(BF16))
        att_c = _attn_call(attn_par(k_bound), lam_init, q_c, k_c, proj, 0, batch, seq, seq, 256, subln_w[l])
        att_l = _attn_call(attn_par(kc_bound), lam_init, q_l, k_l, proj, geom.n_ctx // dec_seq, dec_batch,
                           dec_seq, dec_seq, min(4 * ATT_TQ, dec_seq), subln_w[l], cache)

        dsum, qdf, qdb, kd_f, kd_b, cd_f, cd_b = _ret_tables(ret_decay[l])
        s0 = jnp.concatenate([jnp.zeros((batch, 2, hs, RET_V), F32),
                              state_ret[:, l].astype(F32).reshape(dec_batch, 2, hs, RET_V)], axis=0)
        sb_start, sb_end = _ret_bwd_call(geom, proj, rkt, kd_b, cd_b, s0)
        branch_c, sf_end = _ret_main_call(geom, proj, rkt, dsum, qdf, qdb, kd_f, cd_f, s0, sb_start)

        r_t = router_w[l].T.astype(F32)
        r_hi = r_t.astype(BF16)
        r_lo = (r_t - r_hi.astype(F32)).astype(BF16)
        x1, h2, h2p, logits_t = _merge_call(geom, l, branch_a, att_c, att_l, branch_c, proj, x_ctx, x_lat, mod6,
                                            norm2_w[l], w_branch[l].astype(BF16), w_out[l].astype(BF16), r_hi, r_lo)
        gates, slots, counts = _router_call(geom, logits_t, router_bias[l].astype(F32))
        x_ctx, x_lat = _moe(geom, l, h2, h2p, gates, slots, counts, w_exp_gu, w_exp_down, sgu_bf, sdn_bf, x1, mod6)

        ks.append(k_c32.reshape(batch, seq, DA_HEADS, 2, DA_QK))
        vs.append(v_c32.reshape(batch, seq, DA_HEADS, DA_V))
        lrus.append(jnp.stack([hf_last[:batch, 0], hb_last[:batch, 0]], axis=1))
        rets.append(jnp.stack([sf_end[:batch].reshape(batch, RET_HEADS, RET_QK, RET_V),
                               sb_end[:batch].reshape(batch, RET_HEADS, RET_QK, RET_V)], axis=1))

    y_prompt = x_ctx.reshape(batch, seq, D_MODEL)
    y_sample = x_lat.reshape(dec_batch, dec_seq, D_MODEL)
    return (y_prompt, y_sample, jnp.stack(ks, axis=1), jnp.stack(vs, axis=1),
            jnp.stack(lrus, axis=1), jnp.stack(rets, axis=1))
```

```python
import functools
import math

import numpy as np
import jax
import jax.numpy as jnp
from jax import lax
from jax.experimental import pallas as pl
from jax.experimental.pallas import tpu as pltpu
from jax.experimental.pallas import tpu_sc as plsc

F32 = jnp.float32
BF16 = jnp.bfloat16

D_MODEL = 1024
DEPTH = 2
GRID_W = 64
D_RNN = 512
LRU_BLOCKS = 8
LRU_BLOCK = D_RNN // LRU_BLOCKS
CONV_W = 4
LRU_C = 8.0
DA_HEADS = 4
DA_QK = 64
DA_V = 128
ROPE_PAIRS = DA_QK // 4
ROPE_BASE = 10000.0
RET_HEADS = 4
RET_QK = 64
RET_V = 128
BRANCH_W = 512
N_BRANCH = 3
D_IN = 7168
N_EXPERTS = 64
TOP_K = 8
N_GROUPS = 8
TOPK_GROUPS = 4
D_EXPERT = 256
ROUTED_SCALE = 2.5
EPS = 1e-6

C_XA, C_GA, C_DQ, C_DK, C_DV = 0, 512, 1024, 1536, 2048
C_RQ, C_RK, C_RV, C_RG, C_GL = 2560, 2816, 3072, 3584, 4096

BLK = 256
LRU_SUB = 8
GATE_W = 128
MOD_ROWS = 8
VMEM_LIMIT = 56 * 1024 * 1024


def _cparams(sem, vmem_limit=VMEM_LIMIT):
    return pltpu.CompilerParams(dimension_semantics=sem, vmem_limit_bytes=vmem_limit)


class _Geom:
    def __init__(self, batch, seq, dec_batch, dec_seq):
        assert seq == BLK and dec_seq % BLK == 0
        self.batch, self.seq, self.dec_batch, self.dec_seq = batch, seq, dec_batch, dec_seq
        self.n_ctx = batch * seq
        self.n_lat = dec_batch * dec_seq
        self.n_tok = self.n_ctx + self.n_lat
        self.ctx_blocks = self.n_ctx // BLK
        self.lat_blocks = dec_seq // BLK
        self.n_blocks = self.n_tok // BLK
        self.n_seq = batch + dec_batch

    def mod_row(self, i, tile):
        nct = self.n_ctx // tile
        per = self.dec_seq // tile
        return jnp.where(i < nct, 0, 1 + (i - nct) // per)

    def seq_id(self, i):
        return jnp.where(i < self.ctx_blocks, i, self.ctx_blocks + (i - self.ctx_blocks) // self.lat_blocks)

    def seq_start(self, i):
        return jnp.logical_or(i < self.ctx_blocks, (i - self.ctx_blocks) % self.lat_blocks == 0)

    def seq_end(self, i):
        return jnp.logical_or(i < self.ctx_blocks, (i - self.ctx_blocks) % self.lat_blocks == self.lat_blocks - 1)


def _ada_kernel(c_ref, w_ref, b_ref, o_ref):
    cv = c_ref[...]
    s = cv * jax.nn.sigmoid(cv)
    o_ref[...] = jnp.dot(s, w_ref[...], preferred_element_type=F32,
                         precision=lax.Precision.HIGHEST) + b_ref[...]


def _ada_call(cvec, ada_w, ada_b):
    depth = ada_w.shape[0]
    nt = 6
    return pl.pallas_call(
        _ada_kernel,
        grid=(depth, nt),
        in_specs=[pl.BlockSpec((MOD_ROWS, D_MODEL), lambda l, j: (0, 0)),
                  pl.BlockSpec((None, D_MODEL, D_MODEL), lambda l, j: (l, 0, j)),
                  pl.BlockSpec((None, 1, D_MODEL), lambda l, j: (l, 0, j))],
        out_specs=pl.BlockSpec((None, MOD_ROWS, D_MODEL), lambda l, j: (l, 0, j)),
        out_shape=jax.ShapeDtypeStruct((depth, MOD_ROWS, 6 * D_MODEL), F32),
        compiler_params=_cparams(("arbitrary", "arbitrary")),
        name="ada_mod",
    )(cvec, ada_w, ada_b.reshape(depth, 1, 6 * D_MODEL))


def _mod_spec(geom, l, which, tile, ngrid):
    if ngrid == 1:
        return pl.BlockSpec((None, None, None, 1, D_MODEL),
                            lambda i: (l, geom.mod_row(i, tile), which, 0, 0))
    return pl.BlockSpec((None, None, None, 1, D_MODEL),
                        lambda i, j: (l, geom.mod_row(i, tile), which, 0, 0))


def _split_in_specs(geom, tile, width, ngrid):
    nct = geom.n_ctx // tile
    if ngrid == 1:
        return [pl.BlockSpec((tile, width), lambda i: (jnp.minimum(i, nct - 1), 0)),
                pl.BlockSpec((tile, width), lambda i: (jnp.maximum(i - nct, 0), 0))]
    return [pl.BlockSpec((tile, width), lambda i, j: (jnp.minimum(i, nct - 1), 0)),
            pl.BlockSpec((tile, width), lambda i, j: (jnp.maximum(i - nct, 0), 0))]


def _pick_part(n_ctx_tiles, c_ref, l_ref):
    return jnp.where(pl.program_id(0) < n_ctx_tiles, c_ref[...], l_ref[...])


def _pack_halves(y):
    w = y.shape[1] // 2
    bits = pltpu.bitcast(y, jnp.uint32)
    return (bits[:, :w] >> 16) | (bits[:, w:] & jnp.uint32(0xFFFF0000))


def _unpack_halves(p):
    return pltpu.bitcast(p << 16, F32), pltpu.bitcast(p & jnp.uint32(0xFFFF0000), F32)


INPROJ_TM = 512
INPROJ_TN = 1024


def _inproj_kernel(n_ctx_tiles, xc_ref, xl_ref, sc_ref, sh_ref, nw_ref, w_ref, wkt_ref, o_ref, kt_ref):
    x = _pick_part(n_ctx_tiles, xc_ref, xl_ref)
    ms = jnp.mean(x * x, axis=-1, keepdims=True)
    y = x * lax.rsqrt(ms + EPS) * nw_ref[...]
    hb = (y * (1.0 + sc_ref[...]) + sh_ref[...]).astype(BF16)
    kt_ref[...] = lax.dot_general(wkt_ref[...], hb, (((1,), (1,)), ((), ())),
                                  preferred_element_type=F32).astype(BF16)
    for j in range(D_IN // INPROJ_TN):
        cols = slice(j * INPROJ_TN, (j + 1) * INPROJ_TN)
        o_ref[:, cols] = jnp.dot(hb, w_ref[:, cols], preferred_element_type=F32).astype(BF16)


def _inproj_call(geom, l, x_ctx, x_lat, mod6, norm_w, w_in_bf, w_rkt_bf):
    tm = INPROJ_TM
    return pl.pallas_call(
        functools.partial(_inproj_kernel, geom.n_ctx // tm),
        grid=(geom.n_tok // tm,),
        in_specs=_split_in_specs(geom, tm, D_MODEL, 1) + [
                  _mod_spec(geom, l, 1, tm, 1),
                  _mod_spec(geom, l, 0, tm, 1),
                  pl.BlockSpec((1, D_MODEL), lambda i: (0, 0)),
                  pl.BlockSpec((None, D_MODEL, D_IN), lambda i: (l, 0, 0), pipeline_mode=pl.Buffered(1)),
                  pl.BlockSpec((RET_HEADS * RET_QK, D_MODEL), lambda i: (0, 0))],
        out_specs=[pl.BlockSpec((tm, D_IN), lambda i: (i, 0)),
                   pl.BlockSpec((RET_HEADS * RET_QK, tm), lambda i: (0, i))],
        out_shape=[jax.ShapeDtypeStruct((geom.n_tok, D_IN), BF16),
                   jax.ShapeDtypeStruct((RET_HEADS * RET_QK, geom.n_tok), BF16)],
        compiler_params=_cparams(("arbitrary",)),
        name="inproj",
    )(x_ctx, x_lat, mod6, mod6, norm_w.reshape(1, D_MODEL), w_in_bf, w_rkt_bf)


def _gelu_tanh(x):
    return 0.5 * x * (1.0 + jnp.tanh(math.sqrt(2.0 / math.pi) * (x + 0.044715 * (x * x * x))))


def _lru_kernel(geom, reverse, *refs):
    if reverse:
        (xa_ref, xp_ref, xn_ref, cw_ref, cb_ref, wg_ref, bg_ref, sp_ref, h0_ref, perm_ref, permt_ref,
         ga_ref, hf_ref, out_ref, hl_ref, c_scr) = refs
    else:
        (xa_ref, xp_ref, xn_ref, cw_ref, cb_ref, wg_ref, bg_ref, sp_ref, h0_ref, perm_ref,
         out_ref, hl_ref, c_scr) = refs
    g = pl.program_id(0)
    i = geom.n_blocks - 1 - g if reverse else g
    start = geom.seq_start(i)
    end = geom.seq_end(i)

    @pl.when(end if reverse else start)
    def _():
        c_scr[...] = h0_ref[...]

    sub_len = BLK // LRU_SUB
    perm = perm_ref[...]
    x = jnp.dot(perm, xa_ref[...], preferred_element_type=F32)
    pm = jnp.where(start, 0.0, 1.0)
    nm = jnp.where(end, 0.0, 1.0)
    hp = xp_ref.shape[0]
    p1 = xp_ref[hp - 1:hp, :].astype(F32) * pm
    p2 = xp_ref[hp - 2:hp - 1, :].astype(F32) * pm
    n0 = xn_ref[0:1, :].astype(F32) * nm
    row = lax.broadcasted_iota(jnp.int32, x.shape, 0)
    xm1 = jnp.where(row < LRU_SUB, pltpu.roll(x, LRU_SUB + 1, 0), pltpu.roll(x, LRU_SUB, 0))
    xm1 = jnp.where(row == 0, p1, xm1)
    xm2 = jnp.where(row < 2 * LRU_SUB, pltpu.roll(x, 2 * LRU_SUB + 1, 0), pltpu.roll(x, 2 * LRU_SUB, 0))
    xm2 = jnp.where(row == 0, p2, jnp.where(row == LRU_SUB, p1, xm2))
    xp1 = jnp.where(row >= BLK - LRU_SUB, pltpu.roll(x, BLK - LRU_SUB - 1, 0),
                    pltpu.roll(x, BLK - LRU_SUB, 0))
    xp1 = jnp.where(row == BLK - 1, n0, xp1)
    xc = (cw_ref[0:1, :] * xm2 + cw_ref[1:2, :] * xm1 + cw_ref[2:3, :] * x
          + cw_ref[3:4, :] * xp1 + cb_ref[...])

    gt = jnp.dot(xc.astype(BF16), wg_ref[...], preferred_element_type=F32) + bg_ref[...]
    r = jax.nn.sigmoid(gt[:, :D_RNN])
    ig = jax.nn.sigmoid(gt[:, D_RNN:])
    a = jnp.exp(-LRU_C * r * sp_ref[...])
    u = jnp.sqrt(1.0 - a * a) * ig * xc

    h = jnp.zeros((LRU_SUB, D_RNN), F32)
    p = jnp.ones((LRU_SUB, D_RNN), F32)
    h_loc = [None] * sub_len
    p_loc = [None] * sub_len
    for t in (range(sub_len - 1, -1, -1) if reverse else range(sub_len)):
        a_t = a[t * LRU_SUB:(t + 1) * LRU_SUB, :]
        h = a_t * h + u[t * LRU_SUB:(t + 1) * LRU_SUB, :]
        p = a_t * p
        h_loc[t] = h
        p_loc[t] = p
    h_in = [None] * LRU_SUB
    state = c_scr[...]
    for k in (range(LRU_SUB - 1, -1, -1) if reverse else range(LRU_SUB)):
        h_in[k] = state
        state = h[k:k + 1, :] + p[k:k + 1, :] * state
    c_scr[...] = state
    hl_ref[...] = state
    h_in = jnp.concatenate(h_in, axis=0)
    h_full = jnp.concatenate([h_loc[t] + p_loc[t] * h_in for t in range(sub_len)], axis=0)
    if reverse:
        gv = jnp.dot(perm, ga_ref[...], preferred_element_type=F32)
        y = (_gelu_tanh(gv) * (hf_ref[...] + h_full)).astype(BF16)
        out_ref[...] = jnp.dot(permt_ref[...], y, preferred_element_type=F32).astype(BF16)
    else:
        out_ref[...] = h_full


def _lru_call(geom, reverse, proj, conv_w, conv_b, wg, bg, sp, h0, hf=None):
    nb = geom.n_blocks
    halo = 16
    hpb = BLK // halo

    def blk(g):
        return nb - 1 - g if reverse else g

    d = 1 if reverse else 0
    in_specs = [
        pl.BlockSpec((BLK, D_RNN), lambda g: (blk(g), C_XA // D_RNN)),
        pl.BlockSpec((halo, D_RNN), lambda g: (jnp.maximum(blk(g) * hpb - 1, 0), C_XA // D_RNN)),
        pl.BlockSpec((halo, D_RNN), lambda g: (jnp.minimum((blk(g) + 1) * hpb, nb * hpb - 1), C_XA // D_RNN)),
        pl.BlockSpec((CONV_W, D_RNN), lambda g: (0, 0)),
        pl.BlockSpec((1, D_RNN), lambda g: (0, 0)),
        pl.BlockSpec((D_RNN, 2 * D_RNN), lambda g: (0, 0)),
        pl.BlockSpec((1, 2 * D_RNN), lambda g: (0, 0)),
        pl.BlockSpec((1, D_RNN), lambda g: (0, 0)),
        pl.BlockSpec((None, None, 1, D_RNN), lambda g: (geom.seq_id(blk(g)), d, 0, 0)),
    ]
    pos = np.arange(BLK)
    perm_np = np.zeros((BLK, BLK), np.float32)
    perm_np[pos, (pos % LRU_SUB) * (BLK // LRU_SUB) + pos // LRU_SUB] = 1.0
    in_specs.append(pl.BlockSpec((BLK, BLK), lambda g: (0, 0)))
    args = [proj, proj, proj, conv_w, conv_b, wg, bg, sp, h0, jnp.asarray(perm_np, BF16)]
    if reverse:
        in_specs += [pl.BlockSpec((BLK, BLK), lambda g: (0, 0)),
                     pl.BlockSpec((BLK, D_RNN), lambda g: (blk(g), C_GA // D_RNN)),
                     pl.BlockSpec((BLK, D_RNN), lambda g: (blk(g), 0))]
        args += [jnp.asarray(perm_np.T, BF16), proj, hf]
        out_dtype = BF16
    else:
        out_dtype = F32
    scratch = [pltpu.VMEM((1, D_RNN), F32)]
    return pl.pallas_call(
        functools.partial(_lru_kernel, geom, reverse),
        grid=(nb,),
        in_specs=in_specs,
        out_specs=[pl.BlockSpec((BLK, D_RNN), lambda g: (blk(g), 0)),
                   pl.BlockSpec((None, 1, D_RNN), lambda g: (blk(g), 0, 0))],
        out_shape=[jax.ShapeDtypeStruct((geom.n_tok, D_RNN), out_dtype),
                   jax.ShapeDtypeStruct((nb, 1, D_RNN), F32)],
        scratch_shapes=scratch,
        compiler_params=_cparams(("arbitrary",)),
        name="lru_bwd" if reverse else "lru_fwd",
    )(*args)


def _group_rms(x, w, ones):
    xx = x * x
    hi = xx.astype(BF16)
    lo = (xx - hi.astype(F32)).astype(BF16)
    ss = (jnp.dot(hi, ones, preferred_element_type=F32)
          + jnp.dot(lo, ones, preferred_element_type=F32))
    return x * lax.rsqrt(ss * (1.0 / DA_QK) + EPS) * w


def _rope(x, cos, sin):
    lane = lax.broadcasted_iota(jnp.int32, x.shape, 1)
    first = (lane % (2 * ROPE_PAIRS)) < ROPE_PAIRS
    w = x.shape[1]
    partner = jnp.where(first, pltpu.roll(x, w - ROPE_PAIRS, 1), pltpu.roll(x, ROPE_PAIRS, 1))
    return x * cos + partner * sin


def _prep_kernel(rope, *refs):
    if rope:
        dq_ref, dk_ref, qw_ref, kw_ref, ones_ref, cos_ref, sin_ref, q_out, k_out = refs
    else:
        dq_ref, dk_ref, qw_ref, kw_ref, ones_ref, dv_ref, q_out, k_out, kf_out, vf_out = refs
        vf_out[...] = dv_ref[...].astype(F32)
    ones = ones_ref[...]
    q = _group_rms(dq_ref[...].astype(F32), qw_ref[...], ones)
    k = _group_rms(dk_ref[...].astype(F32), kw_ref[...], ones)
    if rope:
        cos = jnp.concatenate([cos_ref[...]] * 4, axis=1)
        sin = jnp.concatenate([sin_ref[...]] * 4, axis=1)
        q = _rope(q, cos, sin)
        k = _rope(k, cos, sin)
    else:
        kf_out[...] = k
    q_out[...] = (q * (DA_QK ** -0.5 * math.log2(math.e))).astype(BF16)
    k_out[...] = k.astype(BF16)


def _prep_call(geom, latent, proj, qw, kw, ones, cos=None, sin=None):
    tm = 512
    w = DA_HEADS * 2 * DA_QK
    if latent:
        n, off = geom.n_lat, geom.n_ctx // tm
        per = geom.dec_seq // tm
    else:
        n, off = geom.n_ctx, 0
    in_specs = [pl.BlockSpec((tm, w), lambda i: (i + off, C_DQ // w)),
                pl.BlockSpec((tm, w), lambda i: (i + off, C_DK // w)),
                pl.BlockSpec((1, w), lambda i: (0, 0)),
                pl.BlockSpec((1, w), lambda i: (0, 0)),
                pl.BlockSpec((w, w), lambda i: (0, 0))]
    args = [proj, proj, qw, kw, ones]
    out_specs = [pl.BlockSpec((tm, w), lambda i: (i, 0)), pl.BlockSpec((tm, w), lambda i: (i, 0))]
    out_shape = [jax.ShapeDtypeStruct((n, w), BF16), jax.ShapeDtypeStruct((n, w), BF16)]
    if latent:
        in_specs += [pl.BlockSpec((tm, 2 * DA_QK), lambda i: (i % per, 0)),
                     pl.BlockSpec((tm, 2 * DA_QK), lambda i: (i % per, 0))]
        args += [cos, sin]
    else:
        in_specs.append(pl.BlockSpec((tm, w), lambda i: (i, C_DV // w)))
        args.append(proj)
        out_specs += [pl.BlockSpec((tm, w), lambda i: (i, 0)), pl.BlockSpec((tm, w), lambda i: (i, 0))]
        out_shape += [jax.ShapeDtypeStruct((n, w), F32), jax.ShapeDtypeStruct((n, w), F32)]
    return pl.pallas_call(
        functools.partial(_prep_kernel, latent),
        grid=(n // tm,),
        in_specs=in_specs, out_specs=out_specs, out_shape=out_shape,
        compiler_params=_cparams(("arbitrary",)),
        name="qk_prep_lat" if latent else "qk_prep_ctx",
    )(*args)


ATT_KC = 256
ATT_TQ = 256
LOG2E = math.log2(math.e)
ATT_SAFE_LOGIT = 60.0


def _attn_kernel(out_scale, has_cache, *refs):
    if has_cache:
        par_ref, q_ref, kc_ref, vc_ref, kl_ref, vl_ref, sw_ref, o_ref, e_scr, o_scr = refs
        srcs = [(kc_ref, vc_ref), (kl_ref, vl_ref)]
    else:
        par_ref, q_ref, kl_ref, vl_ref, sw_ref, o_ref, e_scr, o_scr = refs
        srcs = [(kl_ref, vl_ref)]
    chunks = [(kr, vr, st) for kr, vr in srcs for st in range(0, kr.shape[0], ATT_KC)]
    lam = par_ref[0]
    no_shift = par_ref[1] > 0.5
    tqs = ATT_TQ
    nsub = q_ref.shape[0] // tqs
    nt = (((1,), (1,)), ((), ()))
    half = ATT_KC // 2

    def stacked_q(sb):
        q = q_ref[sb * tqs:(sb + 1) * tqs, :]
        lane = lax.broadcasted_iota(jnp.int32, q.shape, 1)
        zero = jnp.zeros_like(q)
        return jnp.concatenate([jnp.where(lane < DA_QK, q, zero), jnp.where(lane >= DA_QK, q, zero)], axis=0)

    def logits(qq, c):
        kr, vr, st = chunks[c]
        return lax.dot_general(qq, kr[st:st + ATT_KC, :], nt, preferred_element_type=F32)

    def fold(total, e):
        part = e[:, :half] + e[:, half:]
        return part if total is None else total + part

    def row_stats(lsum):
        l = jnp.sum(lsum, axis=-1, keepdims=True)
        l1 = l[0:tqs]
        return l1, lam * l1 / l[tqs:2 * tqs]

    def pv(acc, buf, c, rho):
        kr, vr, st = chunks[c]
        w = (e_scr[buf, c, 0:tqs, :] - rho * e_scr[buf, c, tqs:2 * tqs, :]).astype(BF16)
        t = jnp.dot(w, vr[st:st + ATT_KC, :], preferred_element_type=F32)
        return t if acc is None else acc + t

    nck = len(chunks)

    @pl.when(no_shift)
    def _():
        stats = None
        for sb in range(nsub + 1):
            qq = stacked_q(sb) if sb < nsub else None
            lsum, acc = None, None
            for c in range(nck):
                if sb < nsub:
                    e = jnp.exp2(logits(qq, c))
                    e_scr[sb % 2, c] = e
                    lsum = fold(lsum, e)
                if sb > 0:
                    acc = pv(acc, (sb - 1) % 2, c, stats[1])
            if sb > 0:
                o_scr[(sb - 1) * tqs:sb * tqs, :] = acc / stats[0]
            if sb < nsub:
                stats = row_stats(lsum)

    @pl.when(jnp.logical_not(no_shift))
    def _():
        for sb in range(nsub):
            qq = stacked_q(sb)
            m = None
            for c in range(nck):
                s = logits(qq, c)
                e_scr[0, c] = s
                mc = jnp.max(s, axis=-1, keepdims=True)
                m = mc if m is None else jnp.maximum(m, mc)
            lsum = None
            for c in range(nck):
                e = jnp.exp2(e_scr[0, c] - m)
                e_scr[0, c] = e
                lsum = fold(lsum, e)
            l1, rho = row_stats(lsum)
            acc = None
            for c in range(nck):
                acc = pv(acc, 0, c, rho)
            o_scr[sb * tqs:(sb + 1) * tqs, :] = acc / l1

    o = o_scr[...]
    y = o * lax.rsqrt(jnp.mean(o * o, axis=-1, keepdims=True) + EPS) * sw_ref[...]
    o_ref[...] = (y * out_scale).astype(BF16)


def _attn_call(par, lam_init, q2d, k2d, proj, v_row_off, n_b, t_q, t_kl, tq, subln_w, cache=None):
    hw = 2 * DA_QK
    nq = t_q // tq
    vcol = C_DV // DA_V
    in_specs = [pl.BlockSpec(memory_space=pltpu.SMEM),
                pl.BlockSpec((tq, hw), lambda b, h, qi: (b * nq + qi, h))]
    args = [par, q2d]
    n_chunks = t_kl // ATT_KC
    if cache is not None:
        kc, vc = cache
        p = kc.shape[1]
        n_chunks += p // ATT_KC
        in_specs += [pl.BlockSpec((None, p, hw), lambda b, h, qi: (b, 0, h)),
                     pl.BlockSpec((None, p, DA_V), lambda b, h, qi: (b, 0, h))]
        args += [kc, vc]
    in_specs += [pl.BlockSpec((t_kl, hw), lambda b, h, qi: (b, h)),
                 pl.BlockSpec((t_kl, DA_V), lambda b, h, qi: (v_row_off + b, vcol + h)),
                 pl.BlockSpec((1, DA_V), lambda b, h, qi: (0, 0))]
    args += [k2d, proj, subln_w.reshape(1, DA_V)]
    return pl.pallas_call(
        functools.partial(_attn_kernel, 1.0 - lam_init, cache is not None),
        grid=(n_b, DA_HEADS, nq),
        in_specs=in_specs,
        out_specs=pl.BlockSpec((tq, DA_V), lambda b, h, qi: (b * nq + qi, h)),
        out_shape=jax.ShapeDtypeStruct((n_b * t_q, DA_HEADS * DA_V), BF16),
        scratch_shapes=[pltpu.VMEM((2 if tq > ATT_TQ else 1, n_chunks, 2 * ATT_TQ, ATT_KC), F32),
                        pltpu.VMEM((tq, DA_V), F32)],
        compiler_params=_cparams(("arbitrary", "arbitrary", "arbitrary")),
        name="diff_attn_lat" if cache is not None else "diff_attn_ctx",
    )(*args)


def _ret_state_update(kt, v, kd, cd, s_old):
    parts = []
    for h in range(RET_HEADS):
        rows = slice(h * RET_QK, (h + 1) * RET_QK)
        kh = (kt[rows, :].astype(F32) * kd[rows, :]).astype(BF16)
        parts.append(jnp.dot(kh, v[:, h * RET_V:(h + 1) * RET_V], preferred_element_type=F32))
    return cd * s_old + jnp.concatenate(parts, axis=0)


def _ret_bwd_kernel(geom, kt_ref, v_ref, kd_ref, cd_ref, s0_ref, sstart_ref, send_ref, s_scr):
    i = geom.n_blocks - 1 - pl.program_id(0)

    @pl.when(geom.seq_end(i))
    def _():
        s_scr[...] = s0_ref[...]

    s_old = s_scr[...]
    sstart_ref[...] = s_old
    kt = kt_ref[...] * jnp.asarray(RET_QK ** -0.5, BF16)
    s_new = _ret_state_update(kt, v_ref[...], kd_ref[...], cd_ref[...], s_old)
    s_scr[...] = s_new
    send_ref[...] = s_new


def _ret_bwd_call(geom, proj, rkt, kd_b, cd_b, s0):
    nb = geom.n_blocks
    hs = RET_HEADS * RET_QK

    def blk(g):
        return nb - 1 - g

    return pl.pallas_call(
        functools.partial(_ret_bwd_kernel, geom),
        grid=(nb,),
        in_specs=[pl.BlockSpec((hs, BLK), lambda g: (0, blk(g))),
                  pl.BlockSpec((BLK, RET_HEADS * RET_V), lambda g: (blk(g), C_RV // (RET_HEADS * RET_V))),
                  pl.BlockSpec((hs, BLK), lambda g: (0, 0)),
                  pl.BlockSpec((hs, RET_V), lambda g: (0, 0)),
                  pl.BlockSpec((None, None, hs, RET_V), lambda g: (geom.seq_id(blk(g)), 1, 0, 0))],
        out_specs=[pl.BlockSpec((None, hs, RET_V), lambda g: (blk(g), 0, 0)),
                   pl.BlockSpec((None, hs, RET_V), lambda g: (blk(g), 0, 0))],
        out_shape=[jax.ShapeDtypeStruct((nb, hs, RET_V), F32),
                   jax.ShapeDtypeStruct((nb, hs, RET_V), F32)],
        scratch_shapes=[pltpu.VMEM((hs, RET_V), F32)],
        compiler_params=_cparams(("arbitrary",)),
        name="ret_bwd_state",
    )(rkt, proj, kd_b, cd_b, s0)


def _ret_main_kernel(geom, q_ref, kt_ref, v_ref, g_ref, dsum_ref, qdf_ref, qdb_ref, kd_ref, cd_ref,
                     s0_ref, sb_ref, o_ref, send_ref, s_scr):
    i = pl.program_id(0)

    @pl.when(geom.seq_start(i))
    def _():
        s_scr[...] = s0_ref[...]

    s_f = s_scr[...]
    s_fb = s_f.astype(BF16)
    s_bb = sb_ref[...].astype(BF16)
    q = q_ref[...].astype(F32)
    kt = kt_ref[...] * jnp.asarray(RET_QK ** -0.5, BF16)
    v = v_ref[...]
    lane = lax.broadcasted_iota(jnp.int32, q.shape, 1)
    for h in range(RET_HEADS):
        in_head = (lane >= h * RET_QK) & (lane < (h + 1) * RET_QK)
        qh = jnp.where(in_head, q, 0.0)
        vh = v[:, h * RET_V:(h + 1) * RET_V]
        sc = jnp.dot(qh.astype(BF16), kt, preferred_element_type=F32) * dsum_ref[h]
        o = jnp.dot(sc.astype(BF16), vh, preferred_element_type=F32)
        o += jnp.dot((qh * qdf_ref[...]).astype(BF16), s_fb, preferred_element_type=F32)
        o += jnp.dot((qh * qdb_ref[...]).astype(BF16), s_bb, preferred_element_type=F32)
        y = o * lax.rsqrt(jnp.mean(o * o, axis=-1, keepdims=True) + EPS)
        gv = g_ref[:, h * RET_V:(h + 1) * RET_V].astype(F32)
        o_ref[:, h * RET_V:(h + 1) * RET_V] = (y * (gv * jax.nn.sigmoid(gv))).astype(BF16)
    s_new = _ret_state_update(kt, v, kd_ref[...], cd_ref[...], s_f)
    s_scr[...] = s_new
    send_ref[...] = s_new


def _ret_main_call(geom, proj, rkt, dsum, qdf, qdb, kd_f, cd_f, s0, sb_start):
    nb = geom.n_blocks
    hs = RET_HEADS * RET_QK
    hv = RET_HEADS * RET_V
    return pl.pallas_call(
        functools.partial(_ret_main_kernel, geom),
        grid=(nb,),
        in_specs=[pl.BlockSpec((BLK, hs), lambda g: (g, C_RQ // hs)),
                  pl.BlockSpec((hs, BLK), lambda g: (0, g)),
                  pl.BlockSpec((BLK, hv), lambda g: (g, C_RV // hv)),
                  pl.BlockSpec((BLK, hv), lambda g: (g, C_RG // hv)),
                  pl.BlockSpec((RET_HEADS, BLK, BLK), lambda g: (0, 0, 0)),
                  pl.BlockSpec((BLK, hs), lambda g: (0, 0)),
                  pl.BlockSpec((BLK, hs), lambda g: (0, 0)),
                  pl.BlockSpec((hs, BLK), lambda g: (0, 0)),
                  pl.BlockSpec((hs, RET_V), lambda g: (0, 0)),
                  pl.BlockSpec((None, None, hs, RET_V), lambda g: (geom.seq_id(g), 0, 0, 0)),
                  pl.BlockSpec((None, hs, RET_V), lambda g: (g, 0, 0))],
        out_specs=[pl.BlockSpec((BLK, hv), lambda g: (g, 0)),
                   pl.BlockSpec((None, hs, RET_V), lambda g: (g, 0, 0))],
        out_shape=[jax.ShapeDtypeStruct((geom.n_tok, hv), BF16),
                   jax.ShapeDtypeStruct((nb, hs, RET_V), F32)],
        scratch_shapes=[pltpu.VMEM((hs, RET_V), F32)],
        compiler_params=_cparams(("arbitrary",)),
        name="ret_main",
    )(proj, rkt, proj, proj, dsum, qdf, qdb, kd_f, cd_f, s0, sb_start)


def _ret_tables(ret_decay_l):
    log_g = jax.nn.log_sigmoid(ret_decay_l.astype(F32))
    pos = jnp.arange(BLK, dtype=F32)
    diff = pos[:, None] - pos[None, :]
    lf = log_g[0][:, None, None]
    lb = log_g[1][:, None, None]
    dsum = (jnp.where(diff >= 0, jnp.exp(jnp.maximum(diff, 0.0)[None] * lf), 0.0)
            + jnp.where(diff <= 0, jnp.exp(jnp.maximum(-diff, 0.0)[None] * lb), 0.0))

    def per_lane(e, lg):
        return jnp.repeat(jnp.exp(e[:, None] * lg[None, :]), RET_QK, axis=1)

    qdf = per_lane(pos + 1.0, log_g[0])
    qdb = per_lane(BLK - pos, log_g[1])
    kd_f = per_lane(BLK - 1.0 - pos, log_g[0]).T
    kd_b = per_lane(pos, log_g[1]).T
    cd_f = jnp.broadcast_to(jnp.repeat(jnp.exp(BLK * log_g[0]), RET_QK)[:, None], (RET_HEADS * RET_QK, RET_V))
    cd_b = jnp.broadcast_to(jnp.repeat(jnp.exp(BLK * log_g[1]), RET_QK)[:, None], (RET_HEADS * RET_QK, RET_V))
    return dsum, qdf, qdb, kd_f, kd_b, cd_f, cd_b


def _merge_kernel(n_ctx_tiles, ba_ref, bbc_ref, bbl_ref, bc_ref, g0_ref, g1_ref, g2_ref, xc_ref, xl_ref,
                  gate_ref, sc_ref, sh_ref, nw_ref, wb_ref, wo_ref, rhi_ref, rlo_ref, x1_ref, h2_ref, h2p_ref,
                  lt_ref):
    branches = (ba_ref[...], _pick_part(n_ctx_tiles, bbc_ref, bbl_ref), bc_ref[...])
    acc = None
    for br, (b, g_ref) in enumerate(zip(branches, (g0_ref, g1_ref, g2_ref))):
        p = jnp.dot(b, wb_ref[br], preferred_element_type=F32)
        t = (0.5 * jnp.tanh(0.5 * g_ref[...].astype(F32)) + 0.5) * p
        acc = t if acc is None else acc + t
    m = jnp.dot(acc.astype(BF16), wo_ref[...], preferred_element_type=F32)
    x1 = _pick_part(n_ctx_tiles, xc_ref, xl_ref) + gate_ref[...] * m
    x1_ref[...] = x1
    ms = jnp.mean(x1 * x1, axis=-1, keepdims=True)
    h2 = x1 * lax.rsqrt(ms + EPS) * nw_ref[...] * (1.0 + sc_ref[...]) + sh_ref[...]
    h2b = h2.astype(BF16)
    h2_ref[...] = h2b
    h2p_ref[...] = _pack_halves(h2b.astype(F32))
    h2lo = (h2 - h2b.astype(F32)).astype(BF16)
    nt = (((1,), (1,)), ((), ()))
    lt_ref[...] = (lax.dot_general(rhi_ref[...], h2b, nt, preferred_element_type=F32)
                   + lax.dot_general(rhi_ref[...], h2lo, nt, preferred_element_type=F32)
                   + lax.dot_general(rlo_ref[...], h2b, nt, preferred_element_type=F32))


def _merge_call(geom, l, ba, bb_ctx, bb_lat, bc, proj, x_ctx, x_lat, mod6, norm2_w, wb_bf, wo_bf, r_hi, r_lo):
    tm = 512
    gcol = C_GL // D_MODEL
    full = lambda shape: pl.BlockSpec(shape, lambda i: tuple(0 for _ in shape))
    tok = lambda w: pl.BlockSpec((tm, w), lambda i: (i, 0))
    return pl.pallas_call(
        functools.partial(_merge_kernel, geom.n_ctx // tm),
        grid=(geom.n_tok // tm,),
        in_specs=[tok(BRANCH_W)] + _split_in_specs(geom, tm, BRANCH_W, 1) + [tok(BRANCH_W),
                  pl.BlockSpec((tm, D_MODEL), lambda i: (i, gcol)),
                  pl.BlockSpec((tm, D_MODEL), lambda i: (i, gcol + 1)),
                  pl.BlockSpec((tm, D_MODEL), lambda i: (i, gcol + 2))]
                 + _split_in_specs(geom, tm, D_MODEL, 1) + [
                  _mod_spec(geom, l, 2, tm, 1), _mod_spec(geom, l, 4, tm, 1), _mod_spec(geom, l, 3, tm, 1),
                  full((1, D_MODEL)),
                  full((N_BRANCH, BRANCH_W, D_MODEL)), full((D_MODEL, D_MODEL)),
                  full((N_EXPERTS, D_MODEL)), full((N_EXPERTS, D_MODEL))],
        out_specs=[tok(D_MODEL), tok(D_MODEL), tok(D_MODEL // 2), pl.BlockSpec((N_EXPERTS, tm), lambda i: (0, i))],
        out_shape=[jax.ShapeDtypeStruct((geom.n_tok, D_MODEL), F32),
                   jax.ShapeDtypeStruct((geom.n_tok, D_MODEL), BF16),
                   jax.ShapeDtypeStruct((geom.n_tok, D_MODEL // 2), jnp.uint32),
                   jax.ShapeDtypeStruct((N_EXPERTS, geom.n_tok), F32)],
        compiler_params=_cparams(("arbitrary",)),
        name="merge_out",
    )(ba, bb_ctx, bb_lat, bc, proj, proj, proj, x_ctx, x_lat, mod6, mod6, mod6,
      norm2_w.reshape(1, D_MODEL), wb_bf, wo_bf, r_hi, r_lo)


def _router_kernel(lt_ref, bias_ref, ltri_ref, utri_ref, g_ref, slot_ref, cnt_ref, cnt_scr):
    per = N_EXPERTS // N_GROUPS
    tm = lt_ref.shape[1]
    scores = jax.nn.sigmoid(lt_ref[...])
    biased = scores + bias_ref[...]
    b3 = biased.reshape(N_GROUPS, per, tm)
    neg = jnp.float32(-jnp.inf)
    m1 = jnp.max(b3, axis=1, keepdims=True)
    is_m1 = b3 == m1
    cnt = jnp.sum(is_m1.astype(F32), axis=1, keepdims=True)
    m2 = jnp.max(jnp.where(is_m1, neg, b3), axis=1, keepdims=True)
    grp = (m1 + jnp.where(cnt >= 2.0, m1, m2)).reshape(N_GROUPS, tm)
    gidx = lax.broadcasted_iota(jnp.int32, (N_GROUPS, tm), 0)
    grank = jnp.zeros((N_GROUPS, tm), F32)
    for g2 in range(N_GROUPS):
        other = grp[g2:g2 + 1, :]
        ahead = (other > grp) | ((other == grp) & (gidx > g2))
        grank += ahead.astype(F32)
    gsel = (grank < float(TOPK_GROUPS)).astype(F32)
    emask = jnp.broadcast_to(gsel.reshape(N_GROUPS, 1, tm), (N_GROUPS, per, tm)).reshape(N_EXPERTS, tm)
    masked = jnp.where(emask > 0.0, biased, neg)
    eidx = lax.broadcasted_iota(jnp.int32, (N_EXPERTS, tm), 0)
    erank = jnp.zeros((N_EXPERTS, tm), F32)
    for e2 in range(N_EXPERTS):
        other = masked[e2:e2 + 1, :]
        ahead = (other > masked) | ((other == masked) & (eidx > e2))
        erank += ahead.astype(F32)
    sel = erank < float(TOP_K)
    w = jnp.where(sel, scores, 0.0)
    gates_t = w / jnp.sum(w, axis=0, keepdims=True) * ROUTED_SCALE

    @pl.when(pl.program_id(0) == 0)
    def _():
        cnt_scr[...] = jnp.zeros_like(cnt_scr)

    selb = sel.astype(BF16)
    slot = jnp.dot(ltri_ref[...], selb, preferred_element_type=F32)
    carry = cnt_scr[:, 0:1]
    rank = jnp.dot(selb, utri_ref[...], preferred_element_type=F32) + carry
    cnt_new = cnt_scr[...] + jnp.sum(sel.astype(F32), axis=1, keepdims=True)
    cnt_scr[...] = cnt_new
    cnt_ref[...] = cnt_new
    eid_f = eidx.astype(F32)
    g_rows, e_rows, r_rows = [], [], []
    for k in range(TOP_K):
        mk = jnp.where(sel & (slot == float(k)), 1.0, 0.0)
        g_rows.append(jnp.sum(mk * gates_t, axis=0, keepdims=True))
        e_rows.append(jnp.sum(mk * eid_f, axis=0, keepdims=True))
        r_rows.append(jnp.sum(mk * rank, axis=0, keepdims=True))
    slot_ref[...] = jnp.concatenate(e_rows + r_rows, axis=0).astype(jnp.int32)
    pad = jnp.zeros((GATE_W - TOP_K, tm), F32)
    g_ref[...] = jnp.concatenate(g_rows + [pad], axis=0).T


ROUTER_TM = 512


def _router_call(geom, logits_t, bias):
    tm = ROUTER_TM
    ltri = jnp.asarray(np.tril(np.ones((N_EXPERTS, N_EXPERTS), np.float32), -1), BF16)
    utri = jnp.asarray(np.triu(np.ones((tm, tm), np.float32), 1), BF16)
    return pl.pallas_call(
        _router_kernel,
        grid=(geom.n_tok // tm,),
        in_specs=[pl.BlockSpec((N_EXPERTS, tm), lambda i: (0, i)),
                  pl.BlockSpec((N_EXPERTS, 1), lambda i: (0, 0)),
                  pl.BlockSpec((N_EXPERTS, N_EXPERTS), lambda i: (0, 0)),
                  pl.BlockSpec((tm, tm), lambda i: (0, 0))],
        out_specs=[pl.BlockSpec((tm, GATE_W), lambda i: (i, 0)),
                   pl.BlockSpec((2 * TOP_K, tm), lambda i: (0, i)),
                   pl.BlockSpec((N_EXPERTS, GATE_W), lambda i: (0, 0))],
        out_shape=[jax.ShapeDtypeStruct((geom.n_tok, GATE_W), F32),
                   jax.ShapeDtypeStruct((2 * TOP_K, geom.n_tok), jnp.int32),
                   jax.ShapeDtypeStruct((N_EXPERTS, GATE_W), F32)],
        scratch_shapes=[pltpu.VMEM((N_EXPERTS, GATE_W), F32)],
        compiler_params=_cparams(("arbitrary",)),
        name="router",
    )(logits_t, bias.reshape(N_EXPERTS, 1), ltri, utri)


MOE_TR = 512
SC_CORES = 2
SC_SUBCORES = 16
SC_CHUNK = 64


def _sc_worker_base(rows_per_worker):
    wid = lax.axis_index("s") * SC_CORES + lax.axis_index("c")
    return wid * rows_per_worker


def _sc_scatter_rows(table, pos_flat, n_slots, n_rows_out):
    n, d = table.shape
    nw = SC_CORES * SC_SUBCORES
    assert n % (nw * SC_CHUNK) == 0
    per_w = n // nw
    mesh = plsc.VectorSubcoreMesh(core_axis_name="c", subcore_axis_name="s")

    @functools.partial(
        pl.kernel, mesh=mesh,
        out_type=jax.ShapeDtypeStruct((n_rows_out, d), table.dtype),
        scratch_types=[[pltpu.VMEM((SC_CHUNK,), jnp.int32) for _ in range(n_slots)],
                       pltpu.VMEM((SC_CHUNK, d), table.dtype),
                       pltpu.SemaphoreType.DMA],
    )
    def scatter(table_hbm, pos_hbm, out_hbm, idx_v, rows_v, sem):
        base = _sc_worker_base(per_w)

        @pl.loop(0, per_w // SC_CHUNK)
        def _(ci):
            off = pl.multiple_of(base + ci * SC_CHUNK, 8)
            for k in range(n_slots):
                pltpu.sync_copy(pos_hbm.at[pl.ds(pl.multiple_of(k * n + off, 8), SC_CHUNK)], idx_v[k])
            pltpu.sync_copy(table_hbm.at[pl.ds(off, SC_CHUNK)], rows_v)
            copies = [pltpu.make_async_copy(rows_v, out_hbm.at[idx_v[k]], sem) for k in range(n_slots)]
            for cp in copies:
                cp.start()
            for cp in copies:
                cp.wait()

    return scatter(table, pos_flat)


def _sc_gather_rows(table, idx):
    b = idx.shape[0]
    d = table.shape[1]
    nw = SC_CORES * SC_SUBCORES
    nbuf = 2
    assert b % (nw * SC_CHUNK * nbuf) == 0
    per_w = b // nw
    n_chunks = per_w // SC_CHUNK
    mesh = plsc.VectorSubcoreMesh(core_axis_name="c", subcore_axis_name="s")

    @functools.partial(
        pl.kernel, mesh=mesh,
        out_type=jax.ShapeDtypeStruct((b, d), table.dtype),
        scratch_types=[pltpu.VMEM((per_w,), jnp.int32),
                       [pltpu.VMEM((SC_CHUNK, d), table.dtype) for _ in range(nbuf)],
                       [pltpu.SemaphoreType.DMA for _ in range(nbuf)],
                       [pltpu.SemaphoreType.DMA for _ in range(nbuf)]],
    )
    def gather(table_hbm, idx_hbm, out_hbm, idx_v, rows, gsem, wsem):
        base = _sc_worker_base(per_w)
        pltpu.sync_copy(idx_hbm.at[pl.ds(pl.multiple_of(base, 8), per_w)], idx_v)

        def fetch(ci, slot):
            src = table_hbm.at[idx_v.at[pl.ds(pl.multiple_of(ci * SC_CHUNK, 8), SC_CHUNK)]]
            return pltpu.make_async_copy(src, rows[slot], gsem[slot])

        def put(ci, slot):
            dst = out_hbm.at[pl.ds(pl.multiple_of(base + ci * SC_CHUNK, 8), SC_CHUNK)]
            return pltpu.make_async_copy(rows[slot], dst, wsem[slot])

        for slot in range(nbuf):
            fetch(slot, slot).start()

        @pl.loop(0, n_chunks, step=nbuf)
        def _(c0):
            for slot in range(nbuf):
                ci = c0 + slot
                fetch(ci, slot).wait()
                put(ci, slot).start()
                put(ci, slot).wait()

                @pl.when(ci + nbuf < n_chunks)
                def _():
                    fetch(ci + nbuf, slot).start()

    return gather(table, idx)


def _route_positions(n_tok, slots, counts):
    cnt = counts[:, 0].astype(jnp.int32)
    cnt_pad = ((cnt + MOE_TR - 1) // MOE_TR) * MOE_TR
    off_end = jnp.cumsum(cnt_pad)
    off = off_end - cnt_pad
    eid, rank = slots[:TOP_K], slots[TOP_K:]
    eids = jnp.arange(N_EXPERTS, dtype=jnp.int32)
    pos = jnp.sum(jnp.where(eid[..., None] == eids, off, 0), axis=-1) + rank
    n_tiles = (TOP_K * n_tok) // MOE_TR + N_EXPERTS
    tile_start = jnp.arange(n_tiles, dtype=jnp.int32) * MOE_TR
    tile_expert = jnp.sum((tile_start[:, None] >= off_end[None, :]).astype(jnp.int32), axis=1)
    tile_expert = jnp.minimum(tile_expert, N_EXPERTS - 1)
    n_used = (off_end[-1] // MOE_TR).reshape(1)
    tile_idx = jnp.arange(n_tiles, dtype=jnp.int32)
    used = tile_idx < n_used[0]
    prev = jnp.concatenate([jnp.full((1,), -1, jnp.int32), tile_expert[:-1]])
    first = jnp.logical_and(used, tile_expert != prev)
    parity = (jnp.cumsum(first.astype(jnp.int32)) - 1) % 2
    later = jnp.logical_and(eids[None, :] > eids[:, None], (cnt_pad > 0)[None, :])
    next_e = jnp.min(jnp.where(later, eids[None, :], N_EXPERTS), axis=1)
    nxt = jnp.sum(jnp.where(tile_expert[:, None] == eids, next_e, 0), axis=1)
    has_next = jnp.logical_and(first, nxt < N_EXPERTS)
    sched = (tile_expert, n_used, first.astype(jnp.int32), jnp.minimum(nxt, N_EXPERTS - 1).astype(jnp.int32),
             jnp.maximum(parity, 0).astype(jnp.int32), has_next.astype(jnp.int32))
    return pos, sched, n_tiles


def _expert_ffn(x_lo, x_hi, gu, dn):
    half = D_MODEL // 2
    a = (jnp.dot(x_lo, gu[0:half, :], preferred_element_type=F32)
         + jnp.dot(x_hi, gu[half:, :], preferred_element_type=F32))
    hg = a[:, :D_EXPERT]
    act = (hg * jax.nn.sigmoid(hg)) * a[:, D_EXPERT:]
    return jnp.dot(act.astype(BF16), dn, preferred_element_type=F32)


def _experts_kernel(l, te_ref, nu_ref, first_ref, nxt_ref, par_ref, hasn_ref, x_ref, gu_hbm, dn_hbm, y_ref,
                    gu_f, dn_f, gu_b, dn_b, sem):
    i = pl.program_id(0)

    def fetch(e, slot):
        return (pltpu.make_async_copy(gu_hbm.at[l, e], gu_f.at[slot], sem.at[0, slot]),
                pltpu.make_async_copy(dn_hbm.at[l, e], dn_f.at[slot], sem.at[1, slot]))

    @pl.when(jnp.logical_and(i == 0, nu_ref[0] > 0))
    def _():
        for cp in fetch(te_ref[0], par_ref[0]):
            cp.start()

    @pl.when(first_ref[i] == 1)
    def _():
        slot = par_ref[i]
        for cp in fetch(te_ref[i], slot):
            cp.wait()

        @pl.when(hasn_ref[i] == 1)
        def _():
            for cp in fetch(nxt_ref[i], 1 - slot):
                cp.start()

        gu_b[...] = gu_f[slot].astype(BF16)
        dn_b[...] = dn_f[slot].astype(BF16)

    @pl.when(i < nu_ref[0])
    def _():
        lo, hi = _unpack_halves(x_ref[...])
        y = _expert_ffn(lo.astype(BF16), hi.astype(BF16), gu_b[...], dn_b[...])
        y_ref[...] = _pack_halves(y.astype(BF16).astype(F32))

    @pl.when(i >= nu_ref[0])
    def _():
        y_ref[...] = jnp.zeros_like(y_ref)


def _experts_call(l, xs, sched, n_tiles, w_gu, w_dn):
    half = D_MODEL // 2
    grid_spec = pltpu.PrefetchScalarGridSpec(
        num_scalar_prefetch=len(sched),
        grid=(n_tiles,),
        in_specs=[pl.BlockSpec((MOE_TR, half), lambda i, te, nu, *_: (jnp.minimum(i, jnp.maximum(nu[0], 1) - 1), 0)),
                  pl.BlockSpec(memory_space=pl.ANY),
                  pl.BlockSpec(memory_space=pl.ANY)],
        out_specs=pl.BlockSpec((MOE_TR, half), lambda i, *_: (i, 0)),
        scratch_shapes=[pltpu.VMEM((2, D_MODEL, 2 * D_EXPERT), w_gu.dtype),
                        pltpu.VMEM((2, D_EXPERT, D_MODEL), w_dn.dtype),
                        pltpu.VMEM((D_MODEL, 2 * D_EXPERT), BF16),
                        pltpu.VMEM((D_EXPERT, D_MODEL), BF16),
                        pltpu.SemaphoreType.DMA((2, 2))],
    )
    return pl.pallas_call(
        functools.partial(_experts_kernel, l),
        grid_spec=grid_spec,
        out_shape=jax.ShapeDtypeStruct((n_tiles * MOE_TR, half), jnp.uint32),
        compiler_params=_cparams(("arbitrary",)),
        name="moe_experts",
    )(*sched, xs, w_gu, w_dn)


MOE_OUT_PARTS = 4


def _moe_out_kernel(n_ctx_tiles, tile0, first, *refs):
    if first:
        yt_ref, g_ref, h_ref, sgu_ref, sdn_ref, x1_ref, gate_ref, oc_ref, ol_ref = refs
    else:
        yt_ref, g_ref, h_ref, sgu_ref, sdn_ref, x1_ref, gate_ref, _, ol_ref = refs
    i = pl.program_id(0) + tile0
    gts = g_ref[...]
    lane = lax.broadcasted_iota(jnp.int32, gts.shape, 1)
    acc_lo, acc_hi = None, None
    for k in range(TOP_K):
        ge = jnp.sum(jnp.where(lane == k, gts, 0.0), axis=1, keepdims=True)
        lo, hi = _unpack_halves(yt_ref[k])
        acc_lo = ge * lo if acc_lo is None else acc_lo + ge * lo
        acc_hi = ge * hi if acc_hi is None else acc_hi + ge * hi
    routed = jnp.concatenate([acc_lo, acc_hi], axis=1)
    h = h_ref[...]
    half = D_MODEL // 2
    shared = _expert_ffn(h[:, :half], h[:, half:], sgu_ref[...], sdn_ref[...])
    y = x1_ref[...] + gate_ref[...] * (routed + shared)
    if first:
        @pl.when(i < n_ctx_tiles)
        def _():
            oc_ref[...] = y

        @pl.when(i >= n_ctx_tiles)
        def _():
            ol_ref[...] = y
    else:
        ol_ref[...] = y


def _moe_out_call(geom, l, part, yt, gates, h2, sgu_bf, sdn_bf, x1, mod6, prev_lat=None):
    tm = 512
    nct = geom.n_ctx // tm
    n_part = geom.n_tok // tm // MOE_OUT_PARTS
    t0 = part * n_part
    first = part == 0
    assert nct <= n_part
    half = D_MODEL // 2
    tok = lambda w: pl.BlockSpec((tm, w), lambda i: (i + t0, 0))
    in_specs = [pl.BlockSpec((TOP_K, tm, half), lambda i: (0, i, 0)),
                tok(GATE_W), tok(D_MODEL),
                pl.BlockSpec((None, D_MODEL, 2 * D_EXPERT), lambda i: (l, 0, 0)),
                pl.BlockSpec((None, D_EXPERT, D_MODEL), lambda i: (l, 0, 0)),
                tok(D_MODEL),
                pl.BlockSpec((None, None, None, 1, D_MODEL), lambda i: (l, geom.mod_row(i + t0, tm), 5, 0, 0))]
    args = [yt, gates, h2, sgu_bf, sdn_bf, x1, mod6]
    lat_shape = jax.ShapeDtypeStruct((geom.n_lat, D_MODEL), F32)
    if first:
        out_specs = [pl.BlockSpec((tm, D_MODEL), lambda i: (jnp.minimum(i, nct - 1), 0)),
                     pl.BlockSpec((tm, D_MODEL), lambda i: (jnp.maximum(i - nct, 0), 0))]
        out_shape = [jax.ShapeDtypeStruct((geom.n_ctx, D_MODEL), F32), lat_shape]
        aliases = {}
    else:
        in_specs.append(pl.BlockSpec(memory_space=pl.ANY))
        args.append(prev_lat)
        out_specs = [pl.BlockSpec((tm, D_MODEL), lambda i: (i + t0 - nct, 0))]
        out_shape = [lat_shape]
        aliases = {len(args) - 1: 0}
    return pl.pallas_call(
        functools.partial(_moe_out_kernel, nct, t0, first),
        grid=(n_part,),
        in_specs=in_specs, out_specs=out_specs, out_shape=out_shape,
        input_output_aliases=aliases,
        compiler_params=_cparams(("arbitrary",)),
        name="moe_out",
    )(*args)


def _moe(geom, l, h2, h2p, gates, slots, counts, w_gu, w_dn, sgu_bf, sdn_bf, x1, mod6):
    pos, sched, n_tiles = _route_positions(geom.n_tok, slots, counts)
    xs = _sc_scatter_rows(h2p, pos.reshape(-1), TOP_K, n_tiles * MOE_TR)
    ys = _experts_call(l, xs, sched, n_tiles, w_gu, w_dn)
    n_part = geom.n_tok // MOE_OUT_PARTS
    y_ctx, y_lat = None, None
    for part in range(MOE_OUT_PARTS):
        pos_p = pos[:, part * n_part:(part + 1) * n_part].reshape(-1)
        yt = _sc_gather_rows(ys, pos_p).reshape(TOP_K, n_part, D_MODEL // 2)
        outs = _moe_out_call(geom, l, part, yt, gates, h2, sgu_bf, sdn_bf, x1, mod6, y_lat)
        if part == 0:
            y_ctx, y_lat = outs
        else:
            (y_lat,) = outs
    return y_ctx, y_lat


def _rope_tables(dec_seq):
    rows = dec_seq // GRID_W
    row = jnp.repeat(jnp.arange(rows, dtype=F32), GRID_W)
    col = jnp.tile(jnp.arange(GRID_W, dtype=F32), rows)
    inv = ROPE_BASE ** (-jnp.arange(ROPE_PAIRS, dtype=F32) / ROPE_PAIRS)
    ar = row[:, None] * inv[None, :]
    ac = col[:, None] * inv[None, :]
    cos64 = jnp.concatenate([jnp.cos(ar), jnp.cos(ar), jnp.cos(ac), jnp.cos(ac)], axis=1)
    sin64 = jnp.concatenate([-jnp.sin(ar), jnp.sin(ar), -jnp.sin(ac), jnp.sin(ac)], axis=1)
    return jnp.tile(cos64, (1, 2)), jnp.tile(sin64, (1, 2))


def _block_diag_gate(wg_dir):
    eye = jnp.eye(LRU_BLOCKS, dtype=F32)
    dense = jnp.einsum('gnij,nm->gnimj', wg_dir.astype(F32), eye).reshape(2, D_RNN, D_RNN)
    return jnp.concatenate([dense[0], dense[1]], axis=1)


def kernel(x_prompt, x_sample, cache_k, cache_v, state_lru, state_ret, c, c_ctx, ada_w, ada_b, norm1_w, norm2_w, w_in, conv_w, conv_b, lru_gate_w, lru_gate_b, lru_lambda, q_norm_w, k_norm_w, diff_lambda, subln_w, ret_decay, w_branch, w_out, router_w, router_bias, w_exp_gu, w_exp_down, w_sh_gu, w_sh_down):
    batch, seq, _ = x_prompt.shape
    dec_batch, dec_seq, _ = x_sample.shape
    assert 1 + dec_batch <= MOD_ROWS
    geom = _Geom(batch, seq, dec_batch, dec_seq)
    hs = RET_HEADS * RET_QK
    aw = DA_HEADS * 2 * DA_QK

    x_ctx = x_prompt.reshape(geom.n_ctx, D_MODEL)
    x_lat = x_sample.reshape(geom.n_lat, D_MODEL)
    cvec = jnp.zeros((MOD_ROWS, D_MODEL), F32).at[0].set(c_ctx).at[1:1 + dec_batch].set(c)
    mod6 = _ada_call(cvec, ada_w, ada_b).reshape(DEPTH, MOD_ROWS, 6, 1, D_MODEL)

    ones_bd = jnp.kron(jnp.eye(aw // DA_QK, dtype=F32), jnp.ones((DA_QK, DA_QK), F32)).astype(BF16)
    cos_t, sin_t = _rope_tables(dec_seq)

    w_in_bf = w_in.astype(BF16)
    sgu_bf, sdn_bf = w_sh_gu.astype(BF16), w_sh_down.astype(BF16)

    ks, vs, lrus, rets = [], [], [], []
    for l in range(DEPTH):
        lam_init = 0.8 - 0.6 * math.exp(-0.3 * l)
        w_rkt_bf = w_in[l][:, C_RK:C_RK + hs].T.astype(BF16)
        proj, rkt = _inproj_call(geom, l, x_ctx, x_lat, mod6, norm1_w[l], w_in_bf, w_rkt_bf)

        sp = jax.nn.softplus(-lru_lambda[l].astype(F32))
        h0 = jnp.concatenate([jnp.zeros((batch, 2, D_RNN), F32), state_lru[:, l].astype(F32)], axis=0)
        h0 = h0.reshape(geom.n_seq, 2, 1, D_RNN)
        cb = conv_b[l].reshape(1, D_RNN)
        lru_args = []
        for d in range(2):
            lru_args.append((_block_diag_gate(lru_gate_w[l, d]).astype(BF16),
                             lru_gate_b[l, d].reshape(1, 2 * D_RNN), sp[d].reshape(1, D_RNN)))
        hf, hf_last = _lru_call(geom, False, proj, conv_w[l], cb, *lru_args[0], h0)
        branch_a, hb_last = _lru_call(geom, True, proj, conv_w[l], cb, *lru_args[1], h0, hf)

        qw = jnp.tile(q_norm_w[l], aw // DA_QK).reshape(1, aw)
        kw = jnp.tile(k_norm_w[l], aw // DA_QK).reshape(1, aw)
        q_c, k_c, k_c32, v_c32 = _prep_call(geom, False, proj, qw, kw, ones_bd)
        q_l, k_l = _prep_call(geom, True, proj, qw, kw, ones_bd, cos_t, sin_t)
        lam_p = diff_lambda[l].astype(F32)
        lam = jnp.exp(jnp.sum(lam_p[0] * lam_p[1])) - jnp.exp(jnp.sum(lam_p[2] * lam_p[3])) + lam_init
        q_bound = DA_QK * jnp.max(jnp.square(q_norm_w[l].astype(F32))) * (DA_QK ** -0.5 * LOG2E) ** 2
        k_bound = DA_QK * jnp.max(jnp.square(k_norm_w[l].astype(F32)))
        kc32 = cache_k[:, l].astype(F32)
        kc_bound = jnp.maximum(k_bound, jnp.max(jnp.sum(jnp.square(kc32), axis=-1)))

        def attn_par(kb):
            ok = (q_bound * kb * 1.05 < ATT_SAFE_LOGIT ** 2).astype(F32)
            return jnp.stack([lam, ok])

        assert geom.n_ctx % dec_seq == 0
        cache = (kc32.reshape(dec_batch, -1, aw).astype(BF16),
                 cache_v[:, l].reshape(dec_batch, -1, DA_HEADS * DA_V).astype(BF16))
        att_c = _attn_call(attn_par(k_bound), lam_init, q_c, k_c, proj, 0, batch, seq, seq, 256, subln_w[l])
        att_l = _attn_call(attn_par(kc_bound), lam_init, q_l, k_l, proj, geom.n_ctx // dec_seq, dec_batch,
                           dec_seq, dec_seq, min(4 * ATT_TQ, dec_seq), subln_w[l], cache)

        dsum, qdf, qdb, kd_f, kd_b, cd_f, cd_b = _ret_tables(ret_decay[l])
        s0 = jnp.concatenate([jnp.zeros((batch, 2, hs, RET_V), F32),
                              state_ret[:, l].astype(F32).reshape(dec_batch, 2, hs, RET_V)], axis=0)
        sb_start, sb_end = _ret_bwd_call(geom, proj, rkt, kd_b, cd_b, s0)
        branch_c, sf_end = _ret_main_call(geom, proj, rkt, dsum, qdf, qdb, kd_f, cd_f, s0, sb_start)

        r_t = router_w[l].T.astype(F32)
        r_hi = r_t.astype(BF16)
        r_lo = (r_t - r_hi.astype(F32)).astype(BF16)
        x1, h2, h2p, logits_t = _merge_call(geom, l, branch_a, att_c, att_l, branch_c, proj, x_ctx, x_lat, mod6,
                                            norm2_w[l], w_branch[l].astype(BF16), w_out[l].astype(BF16), r_hi, r_lo)
        gates, slots, counts = _router_call(geom, logits_t, router_bias[l].astype(F32))
        x_ctx, x_lat = _moe(geom, l, h2, h2p, gates, slots, counts, w_exp_gu, w_exp_down, sgu_bf, sdn_bf, x1, mod6)

        ks.append(k_c32.reshape(batch, seq, DA_HEADS, 2, DA_QK))
        vs.append(v_c32.reshape(batch, seq, DA_HEADS, DA_V))
        lrus.append(jnp.stack([hf_last[:batch, 0], hb_last[:batch, 0]], axis=1))
        rets.append(jnp.stack([sf_end[:batch].reshape(batch, RET_HEADS, RET_QK, RET_V),
                               sb_end[:batch].reshape(batch, RET_HEADS, RET_QK, RET_V)], axis=1))

    y_prompt = x_ctx.reshape(batch, seq, D_MODEL)
    y_sample = x_lat.reshape(dec_batch, dec_seq, D_MODEL)
    return (y_prompt, y_sample, jnp.stack(ks, axis=1), jnp.stack(vs, axis=1),
            jnp.stack(lrus, axis=1), jnp.stack(rets, axis=1))
```

```python
import functools
import math

import numpy as np
import jax
import jax.numpy as jnp
from jax import lax
from jax.experimental import pallas as pl
from jax.experimental.pallas import tpu as pltpu
from jax.experimental.pallas import tpu_sc as plsc

F32 = jnp.float32
BF16 = jnp.bfloat16

D_MODEL = 1024
DEPTH = 2
GRID_W = 64
D_RNN = 512
LRU_BLOCKS = 8
LRU_BLOCK = D_RNN // LRU_BLOCKS
CONV_W = 4
LRU_C = 8.0
DA_HEADS = 4
DA_QK = 64
DA_V = 128
ROPE_PAIRS = DA_QK // 4
ROPE_BASE = 10000.0
RET_HEADS = 4
RET_QK = 64
RET_V = 128
BRANCH_W = 512
N_BRANCH = 3
D_IN = 7168
N_EXPERTS = 64
TOP_K = 8
N_GROUPS = 8
TOPK_GROUPS = 4
D_EXPERT = 256
ROUTED_SCALE = 2.5
EPS = 1e-6

C_XA, C_GA, C_DQ, C_DK, C_DV = 0, 512, 1024, 1536, 2048
C_RQ, C_RK, C_RV, C_RG, C_GL = 2560, 2816, 3072, 3584, 4096

BLK = 256
LRU_SUB = 8
GATE_W = 128
MOD_ROWS = 8
VMEM_LIMIT = 56 * 1024 * 1024


def _cparams(sem, vmem_limit=VMEM_LIMIT):
    return pltpu.CompilerParams(dimension_semantics=sem, vmem_limit_bytes=vmem_limit)


class _Geom:
    def __init__(self, batch, seq, dec_batch, dec_seq):
        assert seq == BLK and dec_seq % BLK == 0
        self.batch, self.seq, self.dec_batch, self.dec_seq = batch, seq, dec_batch, dec_seq
        self.n_ctx = batch * seq
        self.n_lat = dec_batch * dec_seq
        self.n_tok = self.n_ctx + self.n_lat
        self.ctx_blocks = self.n_ctx // BLK
        self.lat_blocks = dec_seq // BLK
        self.n_blocks = self.n_tok // BLK
        self.n_seq = batch + dec_batch

    def mod_row(self, i, tile):
        nct = self.n_ctx // tile
        per = self.dec_seq // tile
        return jnp.where(i < nct, 0, 1 + (i - nct) // per)

    def seq_id(self, i):
        return jnp.where(i < self.ctx_blocks, i, self.ctx_blocks + (i - self.ctx_blocks) // self.lat_blocks)

    def seq_start(self, i):
        return jnp.logical_or(i < self.ctx_blocks, (i - self.ctx_blocks) % self.lat_blocks == 0)

    def seq_end(self, i):
        return jnp.logical_or(i < self.ctx_blocks, (i - self.ctx_blocks) % self.lat_blocks == self.lat_blocks - 1)


def _ada_kernel(c_ref, w_ref, b_ref, o_ref):
    cv = c_ref[...]
    s = cv * jax.nn.sigmoid(cv)
    o_ref[...] = jnp.dot(s, w_ref[...], preferred_element_type=F32,
                         precision=lax.Precision.HIGHEST) + b_ref[...]


def _ada_call(cvec, ada_w, ada_b):
    depth = ada_w.shape[0]
    nt = 6
    return pl.pallas_call(
        _ada_kernel,
        grid=(depth, nt),
        in_specs=[pl.BlockSpec((MOD_ROWS, D_MODEL), lambda l, j: (0, 0)),
                  pl.BlockSpec((None, D_MODEL, D_MODEL), lambda l, j: (l, 0, j)),
                  pl.BlockSpec((None, 1, D_MODEL), lambda l, j: (l, 0, j))],
        out_specs=pl.BlockSpec((None, MOD_ROWS, D_MODEL), lambda l, j: (l, 0, j)),
        out_shape=jax.ShapeDtypeStruct((depth, MOD_ROWS, 6 * D_MODEL), F32),
        compiler_params=_cparams(("arbitrary", "arbitrary")),
        name="ada_mod",
    )(cvec, ada_w, ada_b.reshape(depth, 1, 6 * D_MODEL))


def _mod_spec(geom, l, which, tile, ngrid):
    if ngrid == 1:
        return pl.BlockSpec((None, None, None, 1, D_MODEL),
                            lambda i: (l, geom.mod_row(i, tile), which, 0, 0))
    return pl.BlockSpec((None, None, None, 1, D_MODEL),
                        lambda i, j: (l, geom.mod_row(i, tile), which, 0, 0))


def _split_in_specs(geom, tile, width, ngrid):
    nct = geom.n_ctx // tile
    if ngrid == 1:
        return [pl.BlockSpec((tile, width), lambda i: (jnp.minimum(i, nct - 1), 0)),
                pl.BlockSpec((tile, width), lambda i: (jnp.maximum(i - nct, 0), 0))]
    return [pl.BlockSpec((tile, width), lambda i, j: (jnp.minimum(i, nct - 1), 0)),
            pl.BlockSpec((tile, width), lambda i, j: (jnp.maximum(i - nct, 0), 0))]


def _pick_part(n_ctx_tiles, c_ref, l_ref):
    return jnp.where(pl.program_id(0) < n_ctx_tiles, c_ref[...], l_ref[...])


def _pack_halves(y):
    w = y.shape[1] // 2
    bits = pltpu.bitcast(y, jnp.uint32)
    return (bits[:, :w] >> 16) | (bits[:, w:] & jnp.uint32(0xFFFF0000))


def _unpack_halves(p):
    return pltpu.bitcast(p << 16, F32), pltpu.bitcast(p & jnp.uint32(0xFFFF0000), F32)


INPROJ_TM = 512
INPROJ_TN = 1024


def _inproj_kernel(n_ctx_tiles, xc_ref, xl_ref, sc_ref, sh_ref, nw_ref, w_ref, wkt_ref, o_ref, kt_ref):
    x = _pick_part(n_ctx_tiles, xc_ref, xl_ref)
    ms = jnp.mean(x * x, axis=-1, keepdims=True)
    y = x * lax.rsqrt(ms + EPS) * nw_ref[...]
    hb = (y * (1.0 + sc_ref[...]) + sh_ref[...]).astype(BF16)
    kt_ref[...] = lax.dot_general(wkt_ref[...], hb, (((1,), (1,)), ((), ())),
                                  preferred_element_type=F32).astype(BF16)
    for j in range(D_IN // INPROJ_TN):
        cols = slice(j * INPROJ_TN, (j + 1) * INPROJ_TN)
        o_ref[:, cols] = jnp.dot(hb, w_ref[:, cols], preferred_element_type=F32).astype(BF16)


def _inproj_call(geom, l, x_ctx, x_lat, mod6, norm_w, w_in_bf, w_rkt_bf):
    tm = INPROJ_TM
    return pl.pallas_call(
        functools.partial(_inproj_kernel, geom.n_ctx // tm),
        grid=(geom.n_tok // tm,),
        in_specs=_split_in_specs(geom, tm, D_MODEL, 1) + [
                  _mod_spec(geom, l, 1, tm, 1),
                  _mod_spec(geom, l, 0, tm, 1),
                  pl.BlockSpec((1, D_MODEL), lambda i: (0, 0)),
                  pl.BlockSpec((None, D_MODEL, D_IN), lambda i: (l, 0, 0), pipeline_mode=pl.Buffered(1)),
                  pl.BlockSpec((RET_HEADS * RET_QK, D_MODEL), lambda i: (0, 0))],
        out_specs=[pl.BlockSpec((tm, D_IN), lambda i: (i, 0)),
                   pl.BlockSpec((RET_HEADS * RET_QK, tm), lambda i: (0, i))],
        out_shape=[jax.ShapeDtypeStruct((geom.n_tok, D_IN), BF16),
                   jax.ShapeDtypeStruct((RET_HEADS * RET_QK, geom.n_tok), BF16)],
        compiler_params=_cparams(("arbitrary",)),
        name="inproj",
    )(x_ctx, x_lat, mod6, mod6, norm_w.reshape(1, D_MODEL), w_in_bf, w_rkt_bf)


def _gelu_tanh(x):
    return 0.5 * x * (1.0 + jnp.tanh(math.sqrt(2.0 / math.pi) * (x + 0.044715 * (x * x * x))))


def _lru_kernel(geom, reverse, *refs):
    if reverse:
        (xa_ref, xp_ref, xn_ref, cw_ref, cb_ref, wg_ref, bg_ref, sp_ref, h0_ref, perm_ref, permt_ref,
         ga_ref, hf_ref, out_ref, hl_ref, c_scr) = refs
    else:
        (xa_ref, xp_ref, xn_ref, cw_ref, cb_ref, wg_ref, bg_ref, sp_ref, h0_ref, perm_ref,
         out_ref, hl_ref, c_scr) = refs
    g = pl.program_id(0)
    i = geom.n_blocks - 1 - g if reverse else g
    start = geom.seq_start(i)
    end = geom.seq_end(i)

    @pl.when(end if reverse else start)
    def _():
        c_scr[...] = h0_ref[...]

    sub_len = BLK // LRU_SUB
    perm = perm_ref[...]
    x = jnp.dot(perm, xa_ref[...], preferred_element_type=F32)
    pm = jnp.where(start, 0.0, 1.0)
    nm = jnp.where(end, 0.0, 1.0)
    hp = xp_ref.shape[0]
    p1 = xp_ref[hp - 1:hp, :].astype(F32) * pm
    p2 = xp_ref[hp - 2:hp - 1, :].astype(F32) * pm
    n0 = xn_ref[0:1, :].astype(F32) * nm
    row = lax.broadcasted_iota(jnp.int32, x.shape, 0)
    xm1 = jnp.where(row < LRU_SUB, pltpu.roll(x, LRU_SUB + 1, 0), pltpu.roll(x, LRU_SUB, 0))
    xm1 = jnp.where(row == 0, p1, xm1)
    xm2 = jnp.where(row < 2 * LRU_SUB, pltpu.roll(x, 2 * LRU_SUB + 1, 0), pltpu.roll(x, 2 * LRU_SUB, 0))
    xm2 = jnp.where(row == 0, p2, jnp.where(row == LRU_SUB, p1, xm2))
    xp1 = jnp.where(row >= BLK - LRU_SUB, pltpu.roll(x, BLK - LRU_SUB - 1, 0),
                    pltpu.roll(x, BLK - LRU_SUB, 0))
    xp1 = jnp.where(row == BLK - 1, n0, xp1)
    xc = (cw_ref[0:1, :] * xm2 + cw_ref[1:2, :] * xm1 + cw_ref[2:3, :] * x
          + cw_ref[3:4, :] * xp1 + cb_ref[...])

    gt = jnp.dot(xc.astype(BF16), wg_ref[...], preferred_element_type=F32) + bg_ref[...]
    r = jax.nn.sigmoid(gt[:, :D_RNN])
    ig = jax.nn.sigmoid(gt[:, D_RNN:])
    a = jnp.exp(-LRU_C * r * sp_ref[...])
    u = jnp.sqrt(1.0 - a * a) * ig * xc

    h = jnp.zeros((LRU_SUB, D_RNN), F32)
    p = jnp.ones((LRU_SUB, D_RNN), F32)
    h_loc = [None] * sub_len
    p_loc = [None] * sub_len
    for t in (range(sub_len - 1, -1, -1) if reverse else range(sub_len)):
        a_t = a[t * LRU_SUB:(t + 1) * LRU_SUB, :]
        h = a_t * h + u[t * LRU_SUB:(t + 1) * LRU_SUB, :]
        p = a_t * p
        h_loc[t] = h
        p_loc[t] = p
    h_in = [None] * LRU_SUB
    state = c_scr[...]
    for k in (range(LRU_SUB - 1, -1, -1) if reverse else range(LRU_SUB)):
        h_in[k] = state
        state = h[k:k + 1, :] + p[k:k + 1, :] * state
    c_scr[...] = state
    hl_ref[...] = state
    h_in = jnp.concatenate(h_in, axis=0)
    h_full = jnp.concatenate([h_loc[t] + p_loc[t] * h_in for t in range(sub_len)], axis=0)
    if reverse:
        gv = jnp.dot(perm, ga_ref[...], preferred_element_type=F32)
        y = (_gelu_tanh(gv) * (hf_ref[...] + h_full)).astype(BF16)
        out_ref[...] = jnp.dot(permt_ref[...], y, preferred_element_type=F32).astype(BF16)
    else:
        out_ref[...] = h_full


def _lru_call(geom, reverse, proj, conv_w, conv_b, wg, bg, sp, h0, hf=None):
    nb = geom.n_blocks
    halo = 16
    hpb = BLK // halo

    def blk(g):
        return nb - 1 - g if reverse else g

    d = 1 if reverse else 0
    in_specs = [
        pl.BlockSpec((BLK, D_RNN), lambda g: (blk(g), C_XA // D_RNN)),
        pl.BlockSpec((halo, D_RNN), lambda g: (jnp.maximum(blk(g) * hpb - 1, 0), C_XA // D_RNN)),
        pl.BlockSpec((halo, D_RNN), lambda g: (jnp.minimum((blk(g) + 1) * hpb, nb * hpb - 1), C_XA // D_RNN)),
        pl.BlockSpec((CONV_W, D_RNN), lambda g: (0, 0)),
        pl.BlockSpec((1, D_RNN), lambda g: (0, 0)),
        pl.BlockSpec((D_RNN, 2 * D_RNN), lambda g: (0, 0)),
        pl.BlockSpec((1, 2 * D_RNN), lambda g: (0, 0)),
        pl.BlockSpec((1, D_RNN), lambda g: (0, 0)),
        pl.BlockSpec((None, None, 1, D_RNN), lambda g: (geom.seq_id(blk(g)), d, 0, 0)),
    ]
    pos = np.arange(BLK)
    perm_np = np.zeros((BLK, BLK), np.float32)
    perm_np[pos, (pos % LRU_SUB) * (BLK // LRU_SUB) + pos // LRU_SUB] = 1.0
    in_specs.append(pl.BlockSpec((BLK, BLK), lambda g: (0, 0)))
    args = [proj, proj, proj, conv_w, conv_b, wg, bg, sp, h0, jnp.asarray(perm_np, BF16)]
    if reverse:
        in_specs += [pl.BlockSpec((BLK, BLK), lambda g: (0, 0)),
                     pl.BlockSpec((BLK, D_RNN), lambda g: (blk(g), C_GA // D_RNN)),
                     pl.BlockSpec((BLK, D_RNN), lambda g: (blk(g), 0))]
        args += [jnp.asarray(perm_np.T, BF16), proj, hf]
        out_dtype = BF16
    else:
        out_dtype = F32
    scratch = [pltpu.VMEM((1, D_RNN), F32)]
    return pl.pallas_call(
        functools.partial(_lru_kernel, geom, reverse),
        grid=(nb,),
        in_specs=in_specs,
        out_specs=[pl.BlockSpec((BLK, D_RNN), lambda g: (blk(g), 0)),
                   pl.BlockSpec((None, 1, D_RNN), lambda g: (blk(g), 0, 0))],
        out_shape=[jax.ShapeDtypeStruct((geom.n_tok, D_RNN), out_dtype),
                   jax.ShapeDtypeStruct((nb, 1, D_RNN), F32)],
        scratch_shapes=scratch,
        compiler_params=_cparams(("arbitrary",)),
        name="lru_bwd" if reverse else "lru_fwd",
    )(*args)


def _group_rms(x, w, ones):
    xx = x * x
    hi = xx.astype(BF16)
    lo = (xx - hi.astype(F32)).astype(BF16)
    ss = (jnp.dot(hi, ones, preferred_element_type=F32)
          + jnp.dot(lo, ones, preferred_element_type=F32))
    return x * lax.rsqrt(ss * (1.0 / DA_QK) + EPS) * w


def _rope(x, cos, sin):
    lane = lax.broadcasted_iota(jnp.int32, x.shape, 1)
    first = (lane % (2 * ROPE_PAIRS)) < ROPE_PAIRS
    w = x.shape[1]
    partner = jnp.where(first, pltpu.roll(x, w - ROPE_PAIRS, 1), pltpu.roll(x, ROPE_PAIRS, 1))
    return x * cos + partner * sin


def _prep_kernel(rope, *refs):
    if rope:
        dq_ref, dk_ref, qw_ref, kw_ref, ones_ref, cos_ref, sin_ref, q_out, k_out = refs
    else:
        dq_ref, dk_ref, qw_ref, kw_ref, ones_ref, dv_ref = refs[:6]
        q_out, k_out, kf_out, vf_out = refs[-4:]
        vf_out[...] = dv_ref[...].astype(F32).reshape(vf_out.shape)
    ones = ones_ref[...]
    q = _group_rms(dq_ref[...].astype(F32), qw_ref[...], ones)
    k = _group_rms(dk_ref[...].astype(F32), kw_ref[...], ones)
    if rope:
        cos = jnp.concatenate([cos_ref[...]] * 4, axis=1)
        sin = jnp.concatenate([sin_ref[...]] * 4, axis=1)
        q = _rope(q, cos, sin)
        k = _rope(k, cos, sin)
    else:
        kf_out[...] = k.reshape(kf_out.shape)
    q_out[...] = (q * (DA_QK ** -0.5 * math.log2(math.e))).astype(BF16)
    k_out[...] = k.astype(BF16)


def _prep_call(geom, latent, proj, qw, kw, ones, cos=None, sin=None, layer=0, prev_cache=None):
    tm = 512
    w = DA_HEADS * 2 * DA_QK
    if latent:
        n, off = geom.n_lat, geom.n_ctx // tm
        per = geom.dec_seq // tm
    else:
        n, off = geom.n_ctx, 0
    in_specs = [pl.BlockSpec((tm, w), lambda i: (i + off, C_DQ // w)),
                pl.BlockSpec((tm, w), lambda i: (i + off, C_DK // w)),
                pl.BlockSpec((1, w), lambda i: (0, 0)),
                pl.BlockSpec((1, w), lambda i: (0, 0)),
                pl.BlockSpec((w, w), lambda i: (0, 0))]
    args = [proj, proj, qw, kw, ones]
    aliases = {}
    out_specs = [pl.BlockSpec((tm, w), lambda i: (i, 0)), pl.BlockSpec((tm, w), lambda i: (i, 0))]
    out_shape = [jax.ShapeDtypeStruct((n, w), BF16), jax.ShapeDtypeStruct((n, w), BF16)]
    if latent:
        in_specs += [pl.BlockSpec((tm, 2 * DA_QK), lambda i: (i % per, 0)),
                     pl.BlockSpec((tm, 2 * DA_QK), lambda i: (i % per, 0))]
        args += [cos, sin]
    else:
        in_specs.append(pl.BlockSpec((tm, w), lambda i: (i, C_DV // w)))
        args.append(proj)
        spt = tm // geom.seq
        cache_spec = pl.BlockSpec((spt, None, geom.seq, w), lambda i: (i, layer, 0, 0))
        cache_shape = jax.ShapeDtypeStruct((geom.batch, DEPTH, geom.seq, w), F32)
        out_specs += [cache_spec, cache_spec]
        out_shape += [cache_shape, cache_shape]
        if prev_cache is not None:
            aliases = {len(args): 2, len(args) + 1: 3}
            in_specs += [pl.BlockSpec(memory_space=pl.ANY), pl.BlockSpec(memory_space=pl.ANY)]
            args += list(prev_cache)
    return pl.pallas_call(
        functools.partial(_prep_kernel, latent),
        grid=(n // tm,),
        in_specs=in_specs, out_specs=out_specs, out_shape=out_shape,
        input_output_aliases=aliases,
        compiler_params=_cparams(("arbitrary",)),
        name="qk_prep_lat" if latent else "qk_prep_ctx",
    )(*args)


ATT_KC = 256
ATT_TQ = 256
LOG2E = math.log2(math.e)
ATT_SAFE_LOGIT = 60.0


def _attn_kernel(out_scale, has_cache, *refs):
    if has_cache:
        par_ref, q_ref, kc_ref, vc_ref, kl_ref, vl_ref, sw_ref, o_ref, e_scr, o_scr = refs
        srcs = [(kc_ref, vc_ref), (kl_ref, vl_ref)]
    else:
        par_ref, q_ref, kl_ref, vl_ref, sw_ref, o_ref, e_scr, o_scr = refs
        srcs = [(kl_ref, vl_ref)]
    chunks = [(kr, vr, st) for kr, vr in srcs for st in range(0, kr.shape[0], ATT_KC)]
    lam = par_ref[0]
    no_shift = par_ref[1] > 0.5
    tqs = ATT_TQ
    nsub = q_ref.shape[0] // tqs
    nt = (((1,), (1,)), ((), ()))
    half = ATT_KC // 2

    def stacked_q(sb):
        q = q_ref[sb * tqs:(sb + 1) * tqs, :]
        lane = lax.broadcasted_iota(jnp.int32, q.shape, 1)
        zero = jnp.zeros_like(q)
        return jnp.concatenate([jnp.where(lane < DA_QK, q, zero), jnp.where(lane >= DA_QK, q, zero)], axis=0)

    def logits(qq, c):
        kr, vr, st = chunks[c]
        return lax.dot_general(qq, kr[st:st + ATT_KC, :], nt, preferred_element_type=F32)

    def fold(total, e):
        part = e[:, :half] + e[:, half:]
        return part if total is None else total + part

    def row_stats(lsum):
        l = jnp.sum(lsum, axis=-1, keepdims=True)
        l1 = l[0:tqs]
        return l1, lam * l1 / l[tqs:2 * tqs]

    def pv(acc, buf, c, rho):
        kr, vr, st = chunks[c]
        w = (e_scr[buf, c, 0:tqs, :] - rho * e_scr[buf, c, tqs:2 * tqs, :]).astype(BF16)
        t = jnp.dot(w, vr[st:st + ATT_KC, :], preferred_element_type=F32)
        return t if acc is None else acc + t

    nck = len(chunks)

    @pl.when(no_shift)
    def _():
        stats = None
        for sb in range(nsub + 1):
            qq = stacked_q(sb) if sb < nsub else None
            lsum, acc = None, None
            for c in range(nck):
                if sb < nsub:
                    e = jnp.exp2(logits(qq, c))
                    e_scr[sb % 2, c] = e
                    lsum = fold(lsum, e)
                if sb > 0:
                    acc = pv(acc, (sb - 1) % 2, c, stats[1])
            if sb > 0:
                o_scr[(sb - 1) * tqs:sb * tqs, :] = acc / stats[0]
            if sb < nsub:
                stats = row_stats(lsum)

    @pl.when(jnp.logical_not(no_shift))
    def _():
        for sb in range(nsub):
            qq = stacked_q(sb)
            m = None
            for c in range(nck):
                s = logits(qq, c)
                e_scr[0, c] = s
                mc = jnp.max(s, axis=-1, keepdims=True)
                m = mc if m is None else jnp.maximum(m, mc)
            lsum = None
            for c in range(nck):
                e = jnp.exp2(e_scr[0, c] - m)
                e_scr[0, c] = e
                lsum = fold(lsum, e)
            l1, rho = row_stats(lsum)
            acc = None
            for c in range(nck):
                acc = pv(acc, 0, c, rho)
            o_scr[sb * tqs:(sb + 1) * tqs, :] = acc / l1

    o = o_scr[...]
    y = o * lax.rsqrt(jnp.mean(o * o, axis=-1, keepdims=True) + EPS) * sw_ref[...]
    o_ref[...] = (y * out_scale).astype(BF16)


def _attn_call(par, lam_init, q2d, k2d, proj, v_row_off, n_b, t_q, t_kl, tq, subln_w, cache=None):
    hw = 2 * DA_QK
    nq = t_q // tq
    vcol = C_DV // DA_V
    in_specs = [pl.BlockSpec(memory_space=pltpu.SMEM),
                pl.BlockSpec((tq, hw), lambda b, h, qi: (b * nq + qi, h))]
    args = [par, q2d]
    n_chunks = t_kl // ATT_KC
    if cache is not None:
        kc, vc = cache
        p = kc.shape[1]
        n_chunks += p // ATT_KC
        in_specs += [pl.BlockSpec((None, p, hw), lambda b, h, qi: (b, 0, h)),
                     pl.BlockSpec((None, p, DA_V), lambda b, h, qi: (b, 0, h))]
        args += [kc, vc]
    in_specs += [pl.BlockSpec((t_kl, hw), lambda b, h, qi: (b, h)),
                 pl.BlockSpec((t_kl, DA_V), lambda b, h, qi: (v_row_off + b, vcol + h)),
                 pl.BlockSpec((1, DA_V), lambda b, h, qi: (0, 0))]
    args += [k2d, proj, subln_w.reshape(1, DA_V)]
    return pl.pallas_call(
        functools.partial(_attn_kernel, 1.0 - lam_init, cache is not None),
        grid=(n_b, DA_HEADS, nq),
        in_specs=in_specs,
        out_specs=pl.BlockSpec((tq, DA_V), lambda b, h, qi: (b * nq + qi, h)),
        out_shape=jax.ShapeDtypeStruct((n_b * t_q, DA_HEADS * DA_V), BF16),
        scratch_shapes=[pltpu.VMEM((2 if tq > ATT_TQ else 1, n_chunks, 2 * ATT_TQ, ATT_KC), F32),
                        pltpu.VMEM((tq, DA_V), F32)],
        compiler_params=_cparams(("arbitrary", "arbitrary", "arbitrary")),
        name="diff_attn_lat" if cache is not None else "diff_attn_ctx",
    )(*args)


def _ret_state_update(kt, v, kd, cd, s_old):
    parts = []
    for h in range(RET_HEADS):
        rows = slice(h * RET_QK, (h + 1) * RET_QK)
        kh = (kt[rows, :].astype(F32) * kd[rows, :]).astype(BF16)
        parts.append(jnp.dot(kh, v[:, h * RET_V:(h + 1) * RET_V], preferred_element_type=F32))
    return cd * s_old + jnp.concatenate(parts, axis=0)


def _ret_bwd_kernel(geom, kt_ref, v_ref, kd_ref, cd_ref, s0_ref, sstart_ref, send_ref, s_scr):
    i = geom.n_blocks - 1 - pl.program_id(0)

    @pl.when(geom.seq_end(i))
    def _():
        s_scr[...] = s0_ref[...]

    s_old = s_scr[...]
    sstart_ref[...] = s_old
    kt = kt_ref[...] * jnp.asarray(RET_QK ** -0.5, BF16)
    s_new = _ret_state_update(kt, v_ref[...], kd_ref[...], cd_ref[...], s_old)
    s_scr[...] = s_new
    send_ref[...] = s_new


def _ret_bwd_call(geom, proj, rkt, kd_b, cd_b, s0):
    nb = geom.n_blocks
    hs = RET_HEADS * RET_QK

    def blk(g):
        return nb - 1 - g

    return pl.pallas_call(
        functools.partial(_ret_bwd_kernel, geom),
        grid=(nb,),
        in_specs=[pl.BlockSpec((hs, BLK), lambda g: (0, blk(g))),
                  pl.BlockSpec((BLK, RET_HEADS * RET_V), lambda g: (blk(g), C_RV // (RET_HEADS * RET_V))),
                  pl.BlockSpec((hs, BLK), lambda g: (0, 0)),
                  pl.BlockSpec((hs, RET_V), lambda g: (0, 0)),
                  pl.BlockSpec((None, None, hs, RET_V), lambda g: (geom.seq_id(blk(g)), 1, 0, 0))],
        out_specs=[pl.BlockSpec((None, hs, RET_V), lambda g: (blk(g), 0, 0)),
                   pl.BlockSpec((None, hs, RET_V), lambda g: (blk(g), 0, 0))],
        out_shape=[jax.ShapeDtypeStruct((nb, hs, RET_V), F32),
                   jax.ShapeDtypeStruct((nb, hs, RET_V), F32)],
        scratch_shapes=[pltpu.VMEM((hs, RET_V), F32)],
        compiler_params=_cparams(("arbitrary",)),
        name="ret_bwd_state",
    )(rkt, proj, kd_b, cd_b, s0)


def _ret_main_kernel(geom, q_ref, kt_ref, v_ref, g_ref, dsum_ref, qdf_ref, qdb_ref, kd_ref, cd_ref,
                     s0_ref, sb_ref, o_ref, send_ref, s_scr):
    i = pl.program_id(0)

    @pl.when(geom.seq_start(i))
    def _():
        s_scr[...] = s0_ref[...]

    s_f = s_scr[...]
    s_fb = s_f.astype(BF16)
    s_bb = sb_ref[...].astype(BF16)
    q = q_ref[...].astype(F32)
    kt = kt_ref[...] * jnp.asarray(RET_QK ** -0.5, BF16)
    v = v_ref[...]
    lane = lax.broadcasted_iota(jnp.int32, q.shape, 1)
    for h in range(RET_HEADS):
        in_head = (lane >= h * RET_QK) & (lane < (h + 1) * RET_QK)
        qh = jnp.where(in_head, q, 0.0)
        vh = v[:, h * RET_V:(h + 1) * RET_V]
        sc = jnp.dot(qh.astype(BF16), kt, preferred_element_type=F32) * dsum_ref[h]
        o = jnp.dot(sc.astype(BF16), vh, preferred_element_type=F32)
        o += jnp.dot((qh * qdf_ref[...]).astype(BF16), s_fb, preferred_element_type=F32)
        o += jnp.dot((qh * qdb_ref[...]).astype(BF16), s_bb, preferred_element_type=F32)
        y = o * lax.rsqrt(jnp.mean(o * o, axis=-1, keepdims=True) + EPS)
        gv = g_ref[:, h * RET_V:(h + 1) * RET_V].astype(F32)
        o_ref[:, h * RET_V:(h + 1) * RET_V] = (y * (gv * jax.nn.sigmoid(gv))).astype(BF16)
    s_new = _ret_state_update(kt, v, kd_ref[...], cd_ref[...], s_f)
    s_scr[...] = s_new
    send_ref[...] = s_new


def _ret_main_call(geom, proj, rkt, dsum, qdf, qdb, kd_f, cd_f, s0, sb_start):
    nb = geom.n_blocks
    hs = RET_HEADS * RET_QK
    hv = RET_HEADS * RET_V
    return pl.pallas_call(
        functools.partial(_ret_main_kernel, geom),
        grid=(nb,),
        in_specs=[pl.BlockSpec((BLK, hs), lambda g: (g, C_RQ // hs)),
                  pl.BlockSpec((hs, BLK), lambda g: (0, g)),
                  pl.BlockSpec((BLK, hv), lambda g: (g, C_RV // hv)),
                  pl.BlockSpec((BLK, hv), lambda g: (g, C_RG // hv)),
                  pl.BlockSpec((RET_HEADS, BLK, BLK), lambda g: (0, 0, 0)),
                  pl.BlockSpec((BLK, hs), lambda g: (0, 0)),
                  pl.BlockSpec((BLK, hs), lambda g: (0, 0)),
                  pl.BlockSpec((hs, BLK), lambda g: (0, 0)),
                  pl.BlockSpec((hs, RET_V), lambda g: (0, 0)),
                  pl.BlockSpec((None, None, hs, RET_V), lambda g: (geom.seq_id(g), 0, 0, 0)),
                  pl.BlockSpec((None, hs, RET_V), lambda g: (g, 0, 0))],
        out_specs=[pl.BlockSpec((BLK, hv), lambda g: (g, 0)),
                   pl.BlockSpec((None, hs, RET_V), lambda g: (g, 0, 0))],
        out_shape=[jax.ShapeDtypeStruct((geom.n_tok, hv), BF16),
                   jax.ShapeDtypeStruct((nb, hs, RET_V), F32)],
        scratch_shapes=[pltpu.VMEM((hs, RET_V), F32)],
        compiler_params=_cparams(("arbitrary",)),
        name="ret_main",
    )(proj, rkt, proj, proj, dsum, qdf, qdb, kd_f, cd_f, s0, sb_start)


def _ret_tables(ret_decay_l):
    log_g = jax.nn.log_sigmoid(ret_decay_l.astype(F32))
    pos = jnp.arange(BLK, dtype=F32)
    diff = pos[:, None] - pos[None, :]
    lf = log_g[0][:, None, None]
    lb = log_g[1][:, None, None]
    dsum = (jnp.where(diff >= 0, jnp.exp(jnp.maximum(diff, 0.0)[None] * lf), 0.0)
            + jnp.where(diff <= 0, jnp.exp(jnp.maximum(-diff, 0.0)[None] * lb), 0.0))

    def per_lane(e, lg):
        return jnp.repeat(jnp.exp(e[:, None] * lg[None, :]), RET_QK, axis=1)

    qdf = per_lane(pos + 1.0, log_g[0])
    qdb = per_lane(BLK - pos, log_g[1])
    kd_f = per_lane(BLK - 1.0 - pos, log_g[0]).T
    kd_b = per_lane(pos, log_g[1]).T
    cd_f = jnp.broadcast_to(jnp.repeat(jnp.exp(BLK * log_g[0]), RET_QK)[:, None], (RET_HEADS * RET_QK, RET_V))
    cd_b = jnp.broadcast_to(jnp.repeat(jnp.exp(BLK * log_g[1]), RET_QK)[:, None], (RET_HEADS * RET_QK, RET_V))
    return dsum, qdf, qdb, kd_f, kd_b, cd_f, cd_b


def _merge_kernel(n_ctx_tiles, ba_ref, bbc_ref, bbl_ref, bc_ref, g0_ref, g1_ref, g2_ref, xc_ref, xl_ref,
                  gate_ref, sc_ref, sh_ref, nw_ref, wb_ref, wo_ref, rhi_ref, rlo_ref, x1_ref, h2_ref, h2p_ref,
                  lt_ref):
    branches = (ba_ref[...], _pick_part(n_ctx_tiles, bbc_ref, bbl_ref), bc_ref[...])
    acc = None
    for br, (b, g_ref) in enumerate(zip(branches, (g0_ref, g1_ref, g2_ref))):
        p = jnp.dot(b, wb_ref[br], preferred_element_type=F32)
        t = (0.5 * jnp.tanh(0.5 * g_ref[...].astype(F32)) + 0.5) * p
        acc = t if acc is None else acc + t
    m = jnp.dot(acc.astype(BF16), wo_ref[...], preferred_element_type=F32)
    x1 = _pick_part(n_ctx_tiles, xc_ref, xl_ref) + gate_ref[...] * m
    x1_ref[...] = x1
    ms = jnp.mean(x1 * x1, axis=-1, keepdims=True)
    h2 = x1 * lax.rsqrt(ms + EPS) * nw_ref[...] * (1.0 + sc_ref[...]) + sh_ref[...]
    h2b = h2.astype(BF16)
    h2_ref[...] = h2b
    h2p_ref[...] = _pack_halves(h2b.astype(F32))
    h2lo = (h2 - h2b.astype(F32)).astype(BF16)
    nt = (((1,), (1,)), ((), ()))
    lt_ref[...] = (lax.dot_general(rhi_ref[...], h2b, nt, preferred_element_type=F32)
                   + lax.dot_general(rhi_ref[...], h2lo, nt, preferred_element_type=F32)
                   + lax.dot_general(rlo_ref[...], h2b, nt, preferred_element_type=F32))


def _merge_call(geom, l, ba, bb_ctx, bb_lat, bc, proj, x_ctx, x_lat, mod6, norm2_w, wb_bf, wo_bf, r_hi, r_lo):
    tm = 512
    gcol = C_GL // D_MODEL
    full = lambda shape: pl.BlockSpec(shape, lambda i: tuple(0 for _ in shape))
    tok = lambda w: pl.BlockSpec((tm, w), lambda i: (i, 0))
    return pl.pallas_call(
        functools.partial(_merge_kernel, geom.n_ctx // tm),
        grid=(geom.n_tok // tm,),
        in_specs=[tok(BRANCH_W)] + _split_in_specs(geom, tm, BRANCH_W, 1) + [tok(BRANCH_W),
                  pl.BlockSpec((tm, D_MODEL), lambda i: (i, gcol)),
                  pl.BlockSpec((tm, D_MODEL), lambda i: (i, gcol + 1)),
                  pl.BlockSpec((tm, D_MODEL), lambda i: (i, gcol + 2))]
                 + _split_in_specs(geom, tm, D_MODEL, 1) + [
                  _mod_spec(geom, l, 2, tm, 1), _mod_spec(geom, l, 4, tm, 1), _mod_spec(geom, l, 3, tm, 1),
                  full((1, D_MODEL)),
                  full((N_BRANCH, BRANCH_W, D_MODEL)), full((D_MODEL, D_MODEL)),
                  full((N_EXPERTS, D_MODEL)), full((N_EXPERTS, D_MODEL))],
        out_specs=[tok(D_MODEL), tok(D_MODEL), tok(D_MODEL // 2), pl.BlockSpec((N_EXPERTS, tm), lambda i: (0, i))],
        out_shape=[jax.ShapeDtypeStruct((geom.n_tok, D_MODEL), F32),
                   jax.ShapeDtypeStruct((geom.n_tok, D_MODEL), BF16),
                   jax.ShapeDtypeStruct((geom.n_tok, D_MODEL // 2), jnp.uint32),
                   jax.ShapeDtypeStruct((N_EXPERTS, geom.n_tok), F32)],
        compiler_params=_cparams(("arbitrary",)),
        name="merge_out",
    )(ba, bb_ctx, bb_lat, bc, proj, proj, proj, x_ctx, x_lat, mod6, mod6, mod6,
      norm2_w.reshape(1, D_MODEL), wb_bf, wo_bf, r_hi, r_lo)


def _router_kernel(lt_ref, bias_ref, ltri_ref, utri_ref, g_ref, slot_ref, cnt_ref, cnt_scr):
    per = N_EXPERTS // N_GROUPS
    tm = lt_ref.shape[1]
    scores = jax.nn.sigmoid(lt_ref[...])
    biased = scores + bias_ref[...]
    b3 = biased.reshape(N_GROUPS, per, tm)
    neg = jnp.float32(-jnp.inf)
    m1 = jnp.max(b3, axis=1, keepdims=True)
    is_m1 = b3 == m1
    cnt = jnp.sum(is_m1.astype(F32), axis=1, keepdims=True)
    m2 = jnp.max(jnp.where(is_m1, neg, b3), axis=1, keepdims=True)
    grp = (m1 + jnp.where(cnt >= 2.0, m1, m2)).reshape(N_GROUPS, tm)
    gidx = lax.broadcasted_iota(jnp.int32, (N_GROUPS, tm), 0)
    grank = jnp.zeros((N_GROUPS, tm), F32)
    for g2 in range(N_GROUPS):
        other = grp[g2:g2 + 1, :]
        ahead = (other > grp) | ((other == grp) & (gidx > g2))
        grank += ahead.astype(F32)
    gsel = (grank < float(TOPK_GROUPS)).astype(F32)
    emask = jnp.broadcast_to(gsel.reshape(N_GROUPS, 1, tm), (N_GROUPS, per, tm)).reshape(N_EXPERTS, tm)
    masked = jnp.where(emask > 0.0, biased, neg)
    eidx = lax.broadcasted_iota(jnp.int32, (N_EXPERTS, tm), 0)
    erank = jnp.zeros((N_EXPERTS, tm), F32)
    for e2 in range(N_EXPERTS):
        other = masked[e2:e2 + 1, :]
        ahead = (other > masked) | ((other == masked) & (eidx > e2))
        erank += ahead.astype(F32)
    sel = erank < float(TOP_K)
    w = jnp.where(sel, scores, 0.0)
    gates_t = w / jnp.sum(w, axis=0, keepdims=True) * ROUTED_SCALE

    @pl.when(pl.program_id(0) == 0)
    def _():
        cnt_scr[...] = jnp.zeros_like(cnt_scr)

    selb = sel.astype(BF16)
    slot = jnp.dot(ltri_ref[...], selb, preferred_element_type=F32)
    carry = cnt_scr[:, 0:1]
    rank = jnp.dot(selb, utri_ref[...], preferred_element_type=F32) + carry
    cnt_new = cnt_scr[...] + jnp.sum(sel.astype(F32), axis=1, keepdims=True)
    cnt_scr[...] = cnt_new
    cnt_ref[...] = cnt_new
    eid_f = eidx.astype(F32)
    g_rows, e_rows, r_rows = [], [], []
    for k in range(TOP_K):
        mk = jnp.where(sel & (slot == float(k)), 1.0, 0.0)
        g_rows.append(jnp.sum(mk * gates_t, axis=0, keepdims=True))
        e_rows.append(jnp.sum(mk * eid_f, axis=0, keepdims=True))
        r_rows.append(jnp.sum(mk * rank, axis=0, keepdims=True))
    slot_ref[...] = jnp.concatenate(e_rows + r_rows, axis=0).astype(jnp.int32)
    pad = jnp.zeros((GATE_W - TOP_K, tm), F32)
    g_ref[...] = jnp.concatenate(g_rows + [pad], axis=0).T


ROUTER_TM = 512


def _router_call(geom, logits_t, bias):
    tm = ROUTER_TM
    ltri = jnp.asarray(np.tril(np.ones((N_EXPERTS, N_EXPERTS), np.float32), -1), BF16)
    utri = jnp.asarray(np.triu(np.ones((tm, tm), np.float32), 1), BF16)
    return pl.pallas_call(
        _router_kernel,
        grid=(geom.n_tok // tm,),
        in_specs=[pl.BlockSpec((N_EXPERTS, tm), lambda i: (0, i)),
                  pl.BlockSpec((N_EXPERTS, 1), lambda i: (0, 0)),
                  pl.BlockSpec((N_EXPERTS, N_EXPERTS), lambda i: (0, 0)),
                  pl.BlockSpec((tm, tm), lambda i: (0, 0))],
        out_specs=[pl.BlockSpec((tm, GATE_W), lambda i: (i, 0)),
                   pl.BlockSpec((2 * TOP_K, tm), lambda i: (0, i)),
                   pl.BlockSpec((N_EXPERTS, GATE_W), lambda i: (0, 0))],
        out_shape=[jax.ShapeDtypeStruct((geom.n_tok, GATE_W), F32),
                   jax.ShapeDtypeStruct((2 * TOP_K, geom.n_tok), jnp.int32),
                   jax.ShapeDtypeStruct((N_EXPERTS, GATE_W), F32)],
        scratch_shapes=[pltpu.VMEM((N_EXPERTS, GATE_W), F32)],
        compiler_params=_cparams(("arbitrary",)),
        name="router",
    )(logits_t, bias.reshape(N_EXPERTS, 1), ltri, utri)


MOE_TR = 512
SC_CORES = 2
SC_SUBCORES = 16
SC_CHUNK = 64


def _sc_worker_base(rows_per_worker):
    wid = lax.axis_index("s") * SC_CORES + lax.axis_index("c")
    return wid * rows_per_worker


def _sc_scatter_rows(table, pos_flat, n_slots, n_rows_out):
    n, d = table.shape
    nw = SC_CORES * SC_SUBCORES
    assert n % (nw * SC_CHUNK) == 0
    per_w = n // nw
    mesh = plsc.VectorSubcoreMesh(core_axis_name="c", subcore_axis_name="s")

    @functools.partial(
        pl.kernel, mesh=mesh,
        out_type=jax.ShapeDtypeStruct((n_rows_out, d), table.dtype),
        scratch_types=[[pltpu.VMEM((SC_CHUNK,), jnp.int32) for _ in range(n_slots)],
                       pltpu.VMEM((SC_CHUNK, d), table.dtype),
                       pltpu.SemaphoreType.DMA],
    )
    def scatter(table_hbm, pos_hbm, out_hbm, idx_v, rows_v, sem):
        base = _sc_worker_base(per_w)

        @pl.loop(0, per_w // SC_CHUNK)
        def _(ci):
            off = pl.multiple_of(base + ci * SC_CHUNK, 8)
            for k in range(n_slots):
                pltpu.sync_copy(pos_hbm.at[pl.ds(pl.multiple_of(k * n + off, 8), SC_CHUNK)], idx_v[k])
            pltpu.sync_copy(table_hbm.at[pl.ds(off, SC_CHUNK)], rows_v)
            copies = [pltpu.make_async_copy(rows_v, out_hbm.at[idx_v[k]], sem) for k in range(n_slots)]
            for cp in copies:
                cp.start()
            for cp in copies:
                cp.wait()

    return scatter(table, pos_flat)


def _sc_gather_rows(table, idx):
    b = idx.shape[0]
    d = table.shape[1]
    nw = SC_CORES * SC_SUBCORES
    nbuf = 2
    assert b % (nw * SC_CHUNK * nbuf) == 0
    per_w = b // nw
    n_chunks = per_w // SC_CHUNK
    mesh = plsc.VectorSubcoreMesh(core_axis_name="c", subcore_axis_name="s")

    @functools.partial(
        pl.kernel, mesh=mesh,
        out_type=jax.ShapeDtypeStruct((b, d), table.dtype),
        scratch_types=[pltpu.VMEM((per_w,), jnp.int32),
                       [pltpu.VMEM((SC_CHUNK, d), table.dtype) for _ in range(nbuf)],
                       [pltpu.SemaphoreType.DMA for _ in range(nbuf)],
                       [pltpu.SemaphoreType.DMA for _ in range(nbuf)]],
    )
    def gather(table_hbm, idx_hbm, out_hbm, idx_v, rows, gsem, wsem):
        base = _sc_worker_base(per_w)
        pltpu.sync_copy(idx_hbm.at[pl.ds(pl.multiple_of(base, 8), per_w)], idx_v)

        def fetch(ci, slot):
            src = table_hbm.at[idx_v.at[pl.ds(pl.multiple_of(ci * SC_CHUNK, 8), SC_CHUNK)]]
            return pltpu.make_async_copy(src, rows[slot], gsem[slot])

        def put(ci, slot):
            dst = out_hbm.at[pl.ds(pl.multiple_of(base + ci * SC_CHUNK, 8), SC_CHUNK)]
            return pltpu.make_async_copy(rows[slot], dst, wsem[slot])

        for slot in range(nbuf):
            fetch(slot, slot).start()

        @pl.loop(0, n_chunks, step=nbuf)
        def _(c0):
            for slot in range(nbuf):
                ci = c0 + slot
                fetch(ci, slot).wait()
                put(ci, slot).start()
                put(ci, slot).wait()

                @pl.when(ci + nbuf < n_chunks)
                def _():
                    fetch(ci + nbuf, slot).start()

    return gather(table, idx)


def _route_positions(n_tok, slots, counts):
    cnt = counts[:, 0].astype(jnp.int32)
    cnt_pad = ((cnt + MOE_TR - 1) // MOE_TR) * MOE_TR
    off_end = jnp.cumsum(cnt_pad)
    off = off_end - cnt_pad
    eid, rank = slots[:TOP_K], slots[TOP_K:]
    eids = jnp.arange(N_EXPERTS, dtype=jnp.int32)
    pos = jnp.sum(jnp.where(eid[..., None] == eids, off, 0), axis=-1) + rank
    n_tiles = (TOP_K * n_tok) // MOE_TR + N_EXPERTS
    tile_start = jnp.arange(n_tiles, dtype=jnp.int32) * MOE_TR
    tile_expert = jnp.sum((tile_start[:, None] >= off_end[None, :]).astype(jnp.int32), axis=1)
    tile_expert = jnp.minimum(tile_expert, N_EXPERTS - 1)
    n_used = (off_end[-1] // MOE_TR).reshape(1)
    tile_idx = jnp.arange(n_tiles, dtype=jnp.int32)
    used = tile_idx < n_used[0]
    prev = jnp.concatenate([jnp.full((1,), -1, jnp.int32), tile_expert[:-1]])
    first = jnp.logical_and(used, tile_expert != prev)
    parity = (jnp.cumsum(first.astype(jnp.int32)) - 1) % 2
    later = jnp.logical_and(eids[None, :] > eids[:, None], (cnt_pad > 0)[None, :])
    next_e = jnp.min(jnp.where(later, eids[None, :], N_EXPERTS), axis=1)
    nxt = jnp.sum(jnp.where(tile_expert[:, None] == eids, next_e, 0), axis=1)
    has_next = jnp.logical_and(first, nxt < N_EXPERTS)
    sched = (tile_expert, n_used, first.astype(jnp.int32), jnp.minimum(nxt, N_EXPERTS - 1).astype(jnp.int32),
             jnp.maximum(parity, 0).astype(jnp.int32), has_next.astype(jnp.int32))
    return pos, sched, n_tiles


def _expert_ffn(x_lo, x_hi, gu, dn):
    half = D_MODEL // 2
    a = (jnp.dot(x_lo, gu[0:half, :], preferred_element_type=F32)
         + jnp.dot(x_hi, gu[half:, :], preferred_element_type=F32))
    hg = a[:, :D_EXPERT]
    act = (hg * jax.nn.sigmoid(hg)) * a[:, D_EXPERT:]
    return jnp.dot(act.astype(BF16), dn, preferred_element_type=F32)


def _experts_kernel(l, te_ref, nu_ref, first_ref, nxt_ref, par_ref, hasn_ref, x_ref, gu_hbm, dn_hbm, y_ref,
                    gu_f, dn_f, gu_b, dn_b, sem):
    i = pl.program_id(0)

    def fetch(e, slot):
        return (pltpu.make_async_copy(gu_hbm.at[l, e], gu_f.at[slot], sem.at[0, slot]),
                pltpu.make_async_copy(dn_hbm.at[l, e], dn_f.at[slot], sem.at[1, slot]))

    @pl.when(jnp.logical_and(i == 0, nu_ref[0] > 0))
    def _():
        for cp in fetch(te_ref[0], par_ref[0]):
            cp.start()

    @pl.when(first_ref[i] == 1)
    def _():
        slot = par_ref[i]
        for cp in fetch(te_ref[i], slot):
            cp.wait()

        @pl.when(hasn_ref[i] == 1)
        def _():
            for cp in fetch(nxt_ref[i], 1 - slot):
                cp.start()

        gu_b[...] = gu_f[slot].astype(BF16)
        dn_b[...] = dn_f[slot].astype(BF16)

    @pl.when(i < nu_ref[0])
    def _():
        lo, hi = _unpack_halves(x_ref[...])
        y = _expert_ffn(lo.astype(BF16), hi.astype(BF16), gu_b[...], dn_b[...])
        y_ref[...] = _pack_halves(y.astype(BF16).astype(F32))

    @pl.when(i >= nu_ref[0])
    def _():
        y_ref[...] = jnp.zeros_like(y_ref)


def _experts_call(l, xs, sched, n_tiles, w_gu, w_dn):
    half = D_MODEL // 2
    grid_spec = pltpu.PrefetchScalarGridSpec(
        num_scalar_prefetch=len(sched),
        grid=(n_tiles,),
        in_specs=[pl.BlockSpec((MOE_TR, half), lambda i, te, nu, *_: (jnp.minimum(i, jnp.maximum(nu[0], 1) - 1), 0)),
                  pl.BlockSpec(memory_space=pl.ANY),
                  pl.BlockSpec(memory_space=pl.ANY)],
        out_specs=pl.BlockSpec((MOE_TR, half), lambda i, *_: (i, 0)),
        scratch_shapes=[pltpu.VMEM((2, D_MODEL, 2 * D_EXPERT), w_gu.dtype),
                        pltpu.VMEM((2, D_EXPERT, D_MODEL), w_dn.dtype),
                        pltpu.VMEM((D_MODEL, 2 * D_EXPERT), BF16),
                        pltpu.VMEM((D_EXPERT, D_MODEL), BF16),
                        pltpu.SemaphoreType.DMA((2, 2))],
    )
    return pl.pallas_call(
        functools.partial(_experts_kernel, l),
        grid_spec=grid_spec,
        out_shape=jax.ShapeDtypeStruct((n_tiles * MOE_TR, half), jnp.uint32),
        compiler_params=_cparams(("arbitrary",)),
        name="moe_experts",
    )(*sched, xs, w_gu, w_dn)


MOE_OUT_PARTS = 2


def _moe_out_kernel(n_ctx_tiles, tile0, first, *refs):
    if first:
        yt_ref, g_ref, h_ref, sgu_ref, sdn_ref, x1_ref, gate_ref, oc_ref, ol_ref = refs
    else:
        yt_ref, g_ref, h_ref, sgu_ref, sdn_ref, x1_ref, gate_ref, _, ol_ref = refs
    i = pl.program_id(0) + tile0
    gts = g_ref[...]
    lane = lax.broadcasted_iota(jnp.int32, gts.shape, 1)
    acc_lo, acc_hi = None, None
    for k in range(TOP_K):
        ge = jnp.sum(jnp.where(lane == k, gts, 0.0), axis=1, keepdims=True)
        lo, hi = _unpack_halves(yt_ref[k])
        acc_lo = ge * lo if acc_lo is None else acc_lo + ge * lo
        acc_hi = ge * hi if acc_hi is None else acc_hi + ge * hi
    routed = jnp.concatenate([acc_lo, acc_hi], axis=1)
    h = h_ref[...]
    half = D_MODEL // 2
    shared = _expert_ffn(h[:, :half], h[:, half:], sgu_ref[...], sdn_ref[...])
    y = x1_ref[...] + gate_ref[...] * (routed + shared)
    if first:
        @pl.when(i < n_ctx_tiles)
        def _():
            oc_ref[...] = y

        @pl.when(i >= n_ctx_tiles)
        def _():
            ol_ref[...] = y
    else:
        ol_ref[...] = y


def _moe_out_call(geom, l, part, yt, gates, h2, sgu_bf, sdn_bf, x1, mod6, prev_lat=None):
    tm = 512
    nct = geom.n_ctx // tm
    n_part = geom.n_tok // tm // MOE_OUT_PARTS
    t0 = part * n_part
    first = part == 0
    assert nct <= n_part
    half = D_MODEL // 2
    tok = lambda w: pl.BlockSpec((tm, w), lambda i: (i + t0, 0))
    in_specs = [pl.BlockSpec((TOP_K, tm, half), lambda i: (0, i, 0)),
                tok(GATE_W), tok(D_MODEL),
                pl.BlockSpec((None, D_MODEL, 2 * D_EXPERT), lambda i: (l, 0, 0)),
                pl.BlockSpec((None, D_EXPERT, D_MODEL), lambda i: (l, 0, 0)),
                tok(D_MODEL),
                pl.BlockSpec((None, None, None, 1, D_MODEL), lambda i: (l, geom.mod_row(i + t0, tm), 5, 0, 0))]
    args = [yt, gates, h2, sgu_bf, sdn_bf, x1, mod6]
    lat_shape = jax.ShapeDtypeStruct((geom.n_lat, D_MODEL), F32)
    if first:
        out_specs = [pl.BlockSpec((tm, D_MODEL), lambda i: (jnp.minimum(i, nct - 1), 0)),
                     pl.BlockSpec((tm, D_MODEL), lambda i: (jnp.maximum(i - nct, 0), 0))]
        out_shape = [jax.ShapeDtypeStruct((geom.n_ctx, D_MODEL), F32), lat_shape]
        aliases = {}
    else:
        in_specs.append(pl.BlockSpec(memory_space=pl.ANY))
        args.append(prev_lat)
        out_specs = [pl.BlockSpec((tm, D_MODEL), lambda i: (i + t0 - nct, 0))]
        out_shape = [lat_shape]
        aliases = {len(args) - 1: 0}
    return pl.pallas_call(
        functools.partial(_moe_out_kernel, nct, t0, first),
        grid=(n_part,),
        in_specs=in_specs, out_specs=out_specs, out_shape=out_shape,
        input_output_aliases=aliases,
        compiler_params=_cparams(("arbitrary",)),
        name="moe_out",
    )(*args)


def _moe(geom, l, h2, h2p, gates, slots, counts, w_gu, w_dn, sgu_bf, sdn_bf, x1, mod6):
    pos, sched, n_tiles = _route_positions(geom.n_tok, slots, counts)
    xs = _sc_scatter_rows(h2p, pos.reshape(-1), TOP_K, n_tiles * MOE_TR)
    ys = _experts_call(l, xs, sched, n_tiles, w_gu, w_dn)
    n_part = geom.n_tok // MOE_OUT_PARTS
    y_ctx, y_lat = None, None
    for part in range(MOE_OUT_PARTS):
        pos_p = pos[:, part * n_part:(part + 1) * n_part].reshape(-1)
        yt = _sc_gather_rows(ys, pos_p).reshape(TOP_K, n_part, D_MODEL // 2)
        outs = _moe_out_call(geom, l, part, yt, gates, h2, sgu_bf, sdn_bf, x1, mod6, y_lat)
        if part == 0:
            y_ctx, y_lat = outs
        else:
            (y_lat,) = outs
    return y_ctx, y_lat


def _rope_tables(dec_seq):
    rows = dec_seq // GRID_W
    row = jnp.repeat(jnp.arange(rows, dtype=F32), GRID_W)
    col = jnp.tile(jnp.arange(GRID_W, dtype=F32), rows)
    inv = ROPE_BASE ** (-jnp.arange(ROPE_PAIRS, dtype=F32) / ROPE_PAIRS)
    ar = row[:, None] * inv[None, :]
    ac = col[:, None] * inv[None, :]
    cos64 = jnp.concatenate([jnp.cos(ar), jnp.cos(ar), jnp.cos(ac), jnp.cos(ac)], axis=1)
    sin64 = jnp.concatenate([-jnp.sin(ar), jnp.sin(ar), -jnp.sin(ac), jnp.sin(ac)], axis=1)
    return jnp.tile(cos64, (1, 2)), jnp.tile(sin64, (1, 2))


def _block_diag_gate(wg_dir):
    eye = jnp.eye(LRU_BLOCKS, dtype=F32)
    dense = jnp.einsum('gnij,nm->gnimj', wg_dir.astype(F32), eye).reshape(2, D_RNN, D_RNN)
    return jnp.concatenate([dense[0], dense[1]], axis=1)


def kernel(x_prompt, x_sample, cache_k, cache_v, state_lru, state_ret, c, c_ctx, ada_w, ada_b, norm1_w, norm2_w, w_in, conv_w, conv_b, lru_gate_w, lru_gate_b, lru_lambda, q_norm_w, k_norm_w, diff_lambda, subln_w, ret_decay, w_branch, w_out, router_w, router_bias, w_exp_gu, w_exp_down, w_sh_gu, w_sh_down):
    batch, seq, _ = x_prompt.shape
    dec_batch, dec_seq, _ = x_sample.shape
    assert 1 + dec_batch <= MOD_ROWS
    geom = _Geom(batch, seq, dec_batch, dec_seq)
    hs = RET_HEADS * RET_QK
    aw = DA_HEADS * 2 * DA_QK

    x_ctx = x_prompt.reshape(geom.n_ctx, D_MODEL)
    x_lat = x_sample.reshape(geom.n_lat, D_MODEL)
    cvec = jnp.zeros((MOD_ROWS, D_MODEL), F32).at[0].set(c_ctx).at[1:1 + dec_batch].set(c)
    mod6 = _ada_call(cvec, ada_w, ada_b).reshape(DEPTH, MOD_ROWS, 6, 1, D_MODEL)

    ones_bd = jnp.kron(jnp.eye(aw // DA_QK, dtype=F32), jnp.ones((DA_QK, DA_QK), F32)).astype(BF16)
    cos_t, sin_t = _rope_tables(dec_seq)

    w_in_bf = w_in.astype(BF16)
    sgu_bf, sdn_bf = w_sh_gu.astype(BF16), w_sh_down.astype(BF16)

    new_cache, lrus, rets = None, [], []
    for l in range(DEPTH):
        lam_init = 0.8 - 0.6 * math.exp(-0.3 * l)
        w_rkt_bf = w_in[l][:, C_RK:C_RK + hs].T.astype(BF16)
        proj, rkt = _inproj_call(geom, l, x_ctx, x_lat, mod6, norm1_w[l], w_in_bf, w_rkt_bf)

        sp = jax.nn.softplus(-lru_lambda[l].astype(F32))
        h0 = jnp.concatenate([jnp.zeros((batch, 2, D_RNN), F32), state_lru[:, l].astype(F32)], axis=0)
        h0 = h0.reshape(geom.n_seq, 2, 1, D_RNN)
        cb = conv_b[l].reshape(1, D_RNN)
        lru_args = []
        for d in range(2):
            lru_args.append((_block_diag_gate(lru_gate_w[l, d]).astype(BF16),
                             lru_gate_b[l, d].reshape(1, 2 * D_RNN), sp[d].reshape(1, D_RNN)))
        hf, hf_last = _lru_call(geom, False, proj, conv_w[l], cb, *lru_args[0], h0)
        branch_a, hb_last = _lru_call(geom, True, proj, conv_w[l], cb, *lru_args[1], h0, hf)

        qw = jnp.tile(q_norm_w[l], aw // DA_QK).reshape(1, aw)
        kw = jnp.tile(k_norm_w[l], aw // DA_QK).reshape(1, aw)
        q_c, k_c, *new_cache = _prep_call(geom, False, proj, qw, kw, ones_bd, layer=l, prev_cache=new_cache)
        q_l, k_l = _prep_call(geom, True, proj, qw, kw, ones_bd, cos_t, sin_t)
        lam_p = diff_lambda[l].astype(F32)
        lam = jnp.exp(jnp.sum(lam_p[0] * lam_p[1])) - jnp.exp(jnp.sum(lam_p[2] * lam_p[3])) + lam_init
        q_bound = DA_QK * jnp.max(jnp.square(q_norm_w[l].astype(F32))) * (DA_QK ** -0.5 * LOG2E) ** 2
        k_bound = DA_QK * jnp.max(jnp.square(k_norm_w[l].astype(F32)))
        kc32 = cache_k[:, l].astype(F32)
        kc_bound = jnp.maximum(k_bound, jnp.max(jnp.sum(jnp.square(kc32), axis=-1)))

        def attn_par(kb):
            ok = (q_bound * kb * 1.05 < ATT_SAFE_LOGIT ** 2).astype(F32)
            return jnp.stack([lam, ok])

        assert geom.n_ctx % dec_seq == 0
        cache = (kc32.reshape(dec_batch, -1, aw).astype(BF16),
                 cache_v[:, l].reshape(dec_batch, -1, DA_HEADS * DA_V).astype(BF16))
        att_c = _attn_call(attn_par(k_bound), lam_init, q_c, k_c, proj, 0, batch, seq, seq, 256, subln_w[l])
        att_l = _attn_call(attn_par(kc_bound), lam_init, q_l, k_l, proj, geom.n_ctx // dec_seq, dec_batch,
                           dec_seq, dec_seq, min(4 * ATT_TQ, dec_seq), subln_w[l], cache)

        dsum, qdf, qdb, kd_f, kd_b, cd_f, cd_b = _ret_tables(ret_decay[l])
        s0 = jnp.concatenate([jnp.zeros((batch, 2, hs, RET_V), F32),
                              state_ret[:, l].astype(F32).reshape(dec_batch, 2, hs, RET_V)], axis=0)
        sb_start, sb_end = _ret_bwd_call(geom, proj, rkt, kd_b, cd_b, s0)
        branch_c, sf_end = _ret_main_call(geom, proj, rkt, dsum, qdf, qdb, kd_f, cd_f, s0, sb_start)

        r_t = router_w[l].T.astype(F32)
        r_hi = r_t.astype(BF16)
        r_lo = (r_t - r_hi.astype(F32)).astype(BF16)
        x1, h2, h2p, logits_t = _merge_call(geom, l, branch_a, att_c, att_l, branch_c, proj, x_ctx, x_lat, mod6,
                                            norm2_w[l], w_branch[l].astype(BF16), w_out[l].astype(BF16), r_hi, r_lo)
        gates, slots, counts = _router_call(geom, logits_t, router_bias[l].astype(F32))
        x_ctx, x_lat = _moe(geom, l, h2, h2p, gates, slots, counts, w_exp_gu, w_exp_down, sgu_bf, sdn_bf, x1, mod6)

        lrus.append(jnp.stack([hf_last[:batch, 0], hb_last[:batch, 0]], axis=1))
        rets.append(jnp.stack([sf_end[:batch].reshape(batch, RET_HEADS, RET_QK, RET_V),
                               sb_end[:batch].reshape(batch, RET_HEADS, RET_QK, RET_V)], axis=1))

    y_prompt = x_ctx.reshape(batch, seq, D_MODEL)
    y_sample = x_lat.reshape(dec_batch, dec_seq, D_MODEL)
    new_k = new_cache[0].reshape(batch, DEPTH, seq, DA_HEADS, 2, DA_QK)
    new_v = new_cache[1].reshape(batch, DEPTH, seq, DA_HEADS, DA_V)
    return (y_prompt, y_sample, new_k, new_v, jnp.stack(lrus, axis=1), jnp.stack(rets, axis=1))
```

```python
import functools
import math

import numpy as np
import jax
import jax.numpy as jnp
from jax import lax
from jax.experimental import pallas as pl
from jax.experimental.pallas import tpu as pltpu
from jax.experimental.pallas import tpu_sc as plsc

F32 = jnp.float32
BF16 = jnp.bfloat16

D_MODEL = 1024
DEPTH = 2
GRID_W = 64
D_RNN = 512
LRU_BLOCKS = 8
LRU_BLOCK = D_RNN // LRU_BLOCKS
CONV_W = 4
LRU_C = 8.0
DA_HEADS = 4
DA_QK = 64
DA_V = 128
ROPE_PAIRS = DA_QK // 4
ROPE_BASE = 10000.0
RET_HEADS = 4
RET_QK = 64
RET_V = 128
BRANCH_W = 512
N_BRANCH = 3
D_IN = 7168
N_EXPERTS = 64
TOP_K = 8
N_GROUPS = 8
TOPK_GROUPS = 4
D_EXPERT = 256
ROUTED_SCALE = 2.5
EPS = 1e-6

C_XA, C_GA, C_DQ, C_DK, C_DV = 0, 512, 1024, 1536, 2048
C_RQ, C_RK, C_RV, C_RG, C_GL = 2560, 2816, 3072, 3584, 4096

BLK = 256
LRU_SUB = 8
GATE_W = 128
MOD_ROWS = 8
VMEM_LIMIT = 56 * 1024 * 1024


def _cparams(sem, vmem_limit=VMEM_LIMIT):
    return pltpu.CompilerParams(dimension_semantics=sem, vmem_limit_bytes=vmem_limit)


class _Geom:
    def __init__(self, batch, seq, dec_batch, dec_seq):
        assert seq == BLK and dec_seq % BLK == 0
        self.batch, self.seq, self.dec_batch, self.dec_seq = batch, seq, dec_batch, dec_seq
        self.n_ctx = batch * seq
        self.n_lat = dec_batch * dec_seq
        self.n_tok = self.n_ctx + self.n_lat
        self.ctx_blocks = self.n_ctx // BLK
        self.lat_blocks = dec_seq // BLK
        self.n_blocks = self.n_tok // BLK
        self.n_seq = batch + dec_batch

    def mod_row(self, i, tile):
        nct = self.n_ctx // tile
        per = self.dec_seq // tile
        return jnp.where(i < nct, 0, 1 + (i - nct) // per)

    def seq_id(self, i):
        return jnp.where(i < self.ctx_blocks, i, self.ctx_blocks + (i - self.ctx_blocks) // self.lat_blocks)

    def seq_start(self, i):
        return jnp.logical_or(i < self.ctx_blocks, (i - self.ctx_blocks) % self.lat_blocks == 0)

    def seq_end(self, i):
        return jnp.logical_or(i < self.ctx_blocks, (i - self.ctx_blocks) % self.lat_blocks == self.lat_blocks - 1)


def _ada_kernel(c_ref, w_ref, b_ref, o_ref):
    cv = c_ref[...]
    s = cv * jax.nn.sigmoid(cv)
    o_ref[...] = jnp.dot(s, w_ref[...], preferred_element_type=F32,
                         precision=lax.Precision.HIGHEST) + b_ref[...]


def _ada_call(cvec, ada_w, ada_b):
    depth = ada_w.shape[0]
    nt = 6
    return pl.pallas_call(
        _ada_kernel,
        grid=(depth, nt),
        in_specs=[pl.BlockSpec((MOD_ROWS, D_MODEL), lambda l, j: (0, 0)),
                  pl.BlockSpec((None, D_MODEL, D_MODEL), lambda l, j: (l, 0, j)),
                  pl.BlockSpec((None, 1, D_MODEL), lambda l, j: (l, 0, j))],
        out_specs=pl.BlockSpec((None, MOD_ROWS, D_MODEL), lambda l, j: (l, 0, j)),
        out_shape=jax.ShapeDtypeStruct((depth, MOD_ROWS, 6 * D_MODEL), F32),
        compiler_params=_cparams(("arbitrary", "arbitrary")),
        name="ada_mod",
    )(cvec, ada_w, ada_b.reshape(depth, 1, 6 * D_MODEL))


def _mod_spec(geom, l, which, tile, ngrid):
    if ngrid == 1:
        return pl.BlockSpec((None, None, None, 1, D_MODEL),
                            lambda i: (l, geom.mod_row(i, tile), which, 0, 0))
    return pl.BlockSpec((None, None, None, 1, D_MODEL),
                        lambda i, j: (l, geom.mod_row(i, tile), which, 0, 0))


def _split_in_specs(geom, tile, width, ngrid):
    nct = geom.n_ctx // tile
    if ngrid == 1:
        return [pl.BlockSpec((tile, width), lambda i: (jnp.minimum(i, nct - 1), 0)),
                pl.BlockSpec((tile, width), lambda i: (jnp.maximum(i - nct, 0), 0))]
    return [pl.BlockSpec((tile, width), lambda i, j: (jnp.minimum(i, nct - 1), 0)),
            pl.BlockSpec((tile, width), lambda i, j: (jnp.maximum(i - nct, 0), 0))]


def _pick_part(n_ctx_tiles, c_ref, l_ref):
    return jnp.where(pl.program_id(0) < n_ctx_tiles, c_ref[...], l_ref[...])


def _pack_halves(y):
    w = y.shape[1] // 2
    bits = pltpu.bitcast(y, jnp.uint32)
    return (bits[:, :w] >> 16) | (bits[:, w:] & jnp.uint32(0xFFFF0000))


def _unpack_halves(p):
    return pltpu.bitcast(p << 16, F32), pltpu.bitcast(p & jnp.uint32(0xFFFF0000), F32)


INPROJ_TM = 512
INPROJ_TN = 1024


def _inproj_kernel(n_ctx_tiles, xc_ref, xl_ref, sc_ref, sh_ref, nw_ref, w_ref, wkt_ref, o_ref, kt_ref):
    x = _pick_part(n_ctx_tiles, xc_ref, xl_ref)
    ms = jnp.mean(x * x, axis=-1, keepdims=True)
    y = x * lax.rsqrt(ms + EPS) * nw_ref[...]
    hb = (y * (1.0 + sc_ref[...]) + sh_ref[...]).astype(BF16)
    kt_ref[...] = lax.dot_general(wkt_ref[...], hb, (((1,), (1,)), ((), ())),
                                  preferred_element_type=F32).astype(BF16)
    for j in range(D_IN // INPROJ_TN):
        cols = slice(j * INPROJ_TN, (j + 1) * INPROJ_TN)
        o_ref[:, cols] = jnp.dot(hb, w_ref[:, cols], preferred_element_type=F32).astype(BF16)


def _inproj_call(geom, l, x_ctx, x_lat, mod6, norm_w, w_in_bf, w_rkt_bf):
    tm = INPROJ_TM
    return pl.pallas_call(
        functools.partial(_inproj_kernel, geom.n_ctx // tm),
        grid=(geom.n_tok // tm,),
        in_specs=_split_in_specs(geom, tm, D_MODEL, 1) + [
                  _mod_spec(geom, l, 1, tm, 1),
                  _mod_spec(geom, l, 0, tm, 1),
                  pl.BlockSpec((1, D_MODEL), lambda i: (0, 0)),
                  pl.BlockSpec((None, D_MODEL, D_IN), lambda i: (l, 0, 0), pipeline_mode=pl.Buffered(1)),
                  pl.BlockSpec((RET_HEADS * RET_QK, D_MODEL), lambda i: (0, 0))],
        out_specs=[pl.BlockSpec((tm, D_IN), lambda i: (i, 0)),
                   pl.BlockSpec((RET_HEADS * RET_QK, tm), lambda i: (0, i))],
        out_shape=[jax.ShapeDtypeStruct((geom.n_tok, D_IN), BF16),
                   jax.ShapeDtypeStruct((RET_HEADS * RET_QK, geom.n_tok), BF16)],
        compiler_params=_cparams(("arbitrary",)),
        name="inproj",
    )(x_ctx, x_lat, mod6, mod6, norm_w.reshape(1, D_MODEL), w_in_bf, w_rkt_bf)


def _gelu_tanh(x):
    return 0.5 * x * (1.0 + jnp.tanh(math.sqrt(2.0 / math.pi) * (x + 0.044715 * (x * x * x))))


def _lru_kernel(geom, reverse, *refs):
    if reverse:
        (xa_ref, xp_ref, xn_ref, cw_ref, cb_ref, wg_ref, bg_ref, sp_ref, h0_ref, perm_ref, permt_ref,
         ga_ref, hf_ref, out_ref, hl_ref, c_scr) = refs
    else:
        (xa_ref, xp_ref, xn_ref, cw_ref, cb_ref, wg_ref, bg_ref, sp_ref, h0_ref, perm_ref,
         out_ref, hl_ref, c_scr) = refs
    g = pl.program_id(0)
    i = geom.n_blocks - 1 - g if reverse else g
    start = geom.seq_start(i)
    end = geom.seq_end(i)

    @pl.when(end if reverse else start)
    def _():
        c_scr[...] = h0_ref[...]

    sub_len = BLK // LRU_SUB
    perm = perm_ref[...]
    x = jnp.dot(perm, xa_ref[...], preferred_element_type=F32)
    pm = jnp.where(start, 0.0, 1.0)
    nm = jnp.where(end, 0.0, 1.0)
    hp = xp_ref.shape[0]
    p1 = xp_ref[hp - 1:hp, :].astype(F32) * pm
    p2 = xp_ref[hp - 2:hp - 1, :].astype(F32) * pm
    n0 = xn_ref[0:1, :].astype(F32) * nm
    row = lax.broadcasted_iota(jnp.int32, x.shape, 0)
    xm1 = jnp.where(row < LRU_SUB, pltpu.roll(x, LRU_SUB + 1, 0), pltpu.roll(x, LRU_SUB, 0))
    xm1 = jnp.where(row == 0, p1, xm1)
    xm2 = jnp.where(row < 2 * LRU_SUB, pltpu.roll(x, 2 * LRU_SUB + 1, 0), pltpu.roll(x, 2 * LRU_SUB, 0))
    xm2 = jnp.where(row == 0, p2, jnp.where(row == LRU_SUB, p1, xm2))
    xp1 = jnp.where(row >= BLK - LRU_SUB, pltpu.roll(x, BLK - LRU_SUB - 1, 0),
                    pltpu.roll(x, BLK - LRU_SUB, 0))
    xp1 = jnp.where(row == BLK - 1, n0, xp1)
    xc = (cw_ref[0:1, :] * xm2 + cw_ref[1:2, :] * xm1 + cw_ref[2:3, :] * x
          + cw_ref[3:4, :] * xp1 + cb_ref[...])

    gt = jnp.dot(xc.astype(BF16), wg_ref[...], preferred_element_type=F32) + bg_ref[...]
    r = jax.nn.sigmoid(gt[:, :D_RNN])
    ig = jax.nn.sigmoid(gt[:, D_RNN:])
    a = jnp.exp(-LRU_C * r * sp_ref[...])
    u = jnp.sqrt(1.0 - a * a) * ig * xc

    h = jnp.zeros((LRU_SUB, D_RNN), F32)
    p = jnp.ones((LRU_SUB, D_RNN), F32)
    h_loc = [None] * sub_len
    p_loc = [None] * sub_len
    for t in (range(sub_len - 1, -1, -1) if reverse else range(sub_len)):
        a_t = a[t * LRU_SUB:(t + 1) * LRU_SUB, :]
        h = a_t * h + u[t * LRU_SUB:(t + 1) * LRU_SUB, :]
        p = a_t * p
        h_loc[t] = h
        p_loc[t] = p
    h_in = [None] * LRU_SUB
    state = c_scr[...]
    for k in (range(LRU_SUB - 1, -1, -1) if reverse else range(LRU_SUB)):
        h_in[k] = state
        state = h[k:k + 1, :] + p[k:k + 1, :] * state
    c_scr[...] = state
    hl_ref[...] = state
    h_in = jnp.concatenate(h_in, axis=0)
    h_full = jnp.concatenate([h_loc[t] + p_loc[t] * h_in for t in range(sub_len)], axis=0)
    if reverse:
        gv = jnp.dot(perm, ga_ref[...], preferred_element_type=F32)
        y = (_gelu_tanh(gv) * (hf_ref[...] + h_full)).astype(BF16)
        out_ref[...] = jnp.dot(permt_ref[...], y, preferred_element_type=F32).astype(BF16)
    else:
        out_ref[...] = h_full


def _lru_call(geom, reverse, proj, conv_w, conv_b, wg, bg, sp, h0, hf=None):
    nb = geom.n_blocks
    halo = 16
    hpb = BLK // halo

    def blk(g):
        return nb - 1 - g if reverse else g

    d = 1 if reverse else 0
    in_specs = [
        pl.BlockSpec((BLK, D_RNN), lambda g: (blk(g), C_XA // D_RNN)),
        pl.BlockSpec((halo, D_RNN), lambda g: (jnp.maximum(blk(g) * hpb - 1, 0), C_XA // D_RNN)),
        pl.BlockSpec((halo, D_RNN), lambda g: (jnp.minimum((blk(g) + 1) * hpb, nb * hpb - 1), C_XA // D_RNN)),
        pl.BlockSpec((CONV_W, D_RNN), lambda g: (0, 0)),
        pl.BlockSpec((1, D_RNN), lambda g: (0, 0)),
        pl.BlockSpec((D_RNN, 2 * D_RNN), lambda g: (0, 0)),
        pl.BlockSpec((1, 2 * D_RNN), lambda g: (0, 0)),
        pl.BlockSpec((1, D_RNN), lambda g: (0, 0)),
        pl.BlockSpec((None, None, 1, D_RNN), lambda g: (geom.seq_id(blk(g)), d, 0, 0)),
    ]
    pos = np.arange(BLK)
    perm_np = np.zeros((BLK, BLK), np.float32)
    perm_np[pos, (pos % LRU_SUB) * (BLK // LRU_SUB) + pos // LRU_SUB] = 1.0
    in_specs.append(pl.BlockSpec((BLK, BLK), lambda g: (0, 0)))
    args = [proj, proj, proj, conv_w, conv_b, wg, bg, sp, h0, jnp.asarray(perm_np, BF16)]
    if reverse:
        in_specs += [pl.BlockSpec((BLK, BLK), lambda g: (0, 0)),
                     pl.BlockSpec((BLK, D_RNN), lambda g: (blk(g), C_GA // D_RNN)),
                     pl.BlockSpec((BLK, D_RNN), lambda g: (blk(g), 0))]
        args += [jnp.asarray(perm_np.T, BF16), proj, hf]
        out_dtype = BF16
    else:
        out_dtype = F32
    scratch = [pltpu.VMEM((1, D_RNN), F32)]
    return pl.pallas_call(
        functools.partial(_lru_kernel, geom, reverse),
        grid=(nb,),
        in_specs=in_specs,
        out_specs=[pl.BlockSpec((BLK, D_RNN), lambda g: (blk(g), 0)),
                   pl.BlockSpec((None, 1, D_RNN), lambda g: (blk(g), 0, 0))],
        out_shape=[jax.ShapeDtypeStruct((geom.n_tok, D_RNN), out_dtype),
                   jax.ShapeDtypeStruct((nb, 1, D_RNN), F32)],
        scratch_shapes=scratch,
        compiler_params=_cparams(("arbitrary",)),
        name="lru_bwd" if reverse else "lru_fwd",
    )(*args)


def _group_rms(x, w, ones):
    xx = x * x
    hi = xx.astype(BF16)
    lo = (xx - hi.astype(F32)).astype(BF16)
    ss = (jnp.dot(hi, ones, preferred_element_type=F32)
          + jnp.dot(lo, ones, preferred_element_type=F32))
    return x * lax.rsqrt(ss * (1.0 / DA_QK) + EPS) * w


def _rope(x, cos, sin):
    lane = lax.broadcasted_iota(jnp.int32, x.shape, 1)
    first = (lane % (2 * ROPE_PAIRS)) < ROPE_PAIRS
    w = x.shape[1]
    partner = jnp.where(first, pltpu.roll(x, w - ROPE_PAIRS, 1), pltpu.roll(x, ROPE_PAIRS, 1))
    return x * cos + partner * sin


def _prep_kernel(rope, *refs):
    if rope:
        dq_ref, dk_ref, qw_ref, kw_ref, ones_ref, cos_ref, sin_ref, q_out, k_out = refs
    else:
        dq_ref, dk_ref, qw_ref, kw_ref, ones_ref, dv_ref = refs[:6]
        q_out, k_out, kf_out, vf_out = refs[-4:]
        vf_out[...] = dv_ref[...].astype(F32).reshape(vf_out.shape)
    ones = ones_ref[...]
    q = _group_rms(dq_ref[...].astype(F32), qw_ref[...], ones)
    k = _group_rms(dk_ref[...].astype(F32), kw_ref[...], ones)
    if rope:
        cos = jnp.concatenate([cos_ref[...]] * 4, axis=1)
        sin = jnp.concatenate([sin_ref[...]] * 4, axis=1)
        q = _rope(q, cos, sin)
        k = _rope(k, cos, sin)
    else:
        kf_out[...] = k.reshape(kf_out.shape)
    q_out[...] = (q * (DA_QK ** -0.5 * math.log2(math.e))).astype(BF16)
    k_out[...] = k.astype(BF16)


def _prep_call(geom, latent, proj, qw, kw, ones, cos=None, sin=None, layer=0, prev_cache=None):
    tm = 512
    w = DA_HEADS * 2 * DA_QK
    if latent:
        n, off = geom.n_lat, geom.n_ctx // tm
        per = geom.dec_seq // tm
    else:
        n, off = geom.n_ctx, 0
    in_specs = [pl.BlockSpec((tm, w), lambda i: (i + off, C_DQ // w)),
                pl.BlockSpec((tm, w), lambda i: (i + off, C_DK // w)),
                pl.BlockSpec((1, w), lambda i: (0, 0)),
                pl.BlockSpec((1, w), lambda i: (0, 0)),
                pl.BlockSpec((w, w), lambda i: (0, 0))]
    args = [proj, proj, qw, kw, ones]
    aliases = {}
    out_specs = [pl.BlockSpec((tm, w), lambda i: (i, 0)), pl.BlockSpec((tm, w), lambda i: (i, 0))]
    out_shape = [jax.ShapeDtypeStruct((n, w), BF16), jax.ShapeDtypeStruct((n, w), BF16)]
    if latent:
        in_specs += [pl.BlockSpec((tm, 2 * DA_QK), lambda i: (i % per, 0)),
                     pl.BlockSpec((tm, 2 * DA_QK), lambda i: (i % per, 0))]
        args += [cos, sin]
    else:
        in_specs.append(pl.BlockSpec((tm, w), lambda i: (i, C_DV // w)))
        args.append(proj)
        spt = tm // geom.seq
        cache_spec = pl.BlockSpec((spt, None, geom.seq, w), lambda i: (i, layer, 0, 0))
        cache_shape = jax.ShapeDtypeStruct((geom.batch, DEPTH, geom.seq, w), F32)
        out_specs += [cache_spec, cache_spec]
        out_shape += [cache_shape, cache_shape]
        if prev_cache is not None:
            aliases = {len(args): 2, len(args) + 1: 3}
            in_specs += [pl.BlockSpec(memory_space=pl.ANY), pl.BlockSpec(memory_space=pl.ANY)]
            args += list(prev_cache)
    return pl.pallas_call(
        functools.partial(_prep_kernel, latent),
        grid=(n // tm,),
        in_specs=in_specs, out_specs=out_specs, out_shape=out_shape,
        input_output_aliases=aliases,
        compiler_params=_cparams(("arbitrary",)),
        name="qk_prep_lat" if latent else "qk_prep_ctx",
    )(*args)


ATT_KC = 256
ATT_TQ = 256
LOG2E = math.log2(math.e)
ATT_SAFE_LOGIT = 60.0


def _attn_kernel(out_scale, has_cache, *refs):
    if has_cache:
        par_ref, q_ref, kc_ref, vc_ref, kl_ref, vl_ref, sw_ref, o_ref, e_scr, o_scr = refs
        srcs = [(kc_ref, vc_ref), (kl_ref, vl_ref)]
    else:
        par_ref, q_ref, kl_ref, vl_ref, sw_ref, o_ref, e_scr, o_scr = refs
        srcs = [(kl_ref, vl_ref)]
    chunks = [(kr, vr, st) for kr, vr in srcs for st in range(0, kr.shape[0], ATT_KC)]
    lam = par_ref[0]
    no_shift = par_ref[1] > 0.5
    tqs = ATT_TQ
    nsub = q_ref.shape[0] // tqs
    nt = (((1,), (1,)), ((), ()))
    half = ATT_KC // 2

    def stacked_q(sb):
        q = q_ref[sb * tqs:(sb + 1) * tqs, :]
        lane = lax.broadcasted_iota(jnp.int32, q.shape, 1)
        zero = jnp.zeros_like(q)
        return jnp.concatenate([jnp.where(lane < DA_QK, q, zero), jnp.where(lane >= DA_QK, q, zero)], axis=0)

    def logits(qq, c):
        kr, vr, st = chunks[c]
        return lax.dot_general(qq, kr[st:st + ATT_KC, :], nt, preferred_element_type=F32)

    def fold(total, e):
        part = e[:, :half] + e[:, half:]
        return part if total is None else total + part

    def row_stats(lsum):
        l = jnp.sum(lsum, axis=-1, keepdims=True)
        l1 = l[0:tqs]
        return l1, lam * l1 / l[tqs:2 * tqs]

    def pv(acc, buf, c, rho):
        kr, vr, st = chunks[c]
        w = (e_scr[buf, c, 0:tqs, :] - rho * e_scr[buf, c, tqs:2 * tqs, :]).astype(BF16)
        t = jnp.dot(w, vr[st:st + ATT_KC, :], preferred_element_type=F32)
        return t if acc is None else acc + t

    nck = len(chunks)

    @pl.when(no_shift)
    def _():
        stats = None
        for sb in range(nsub + 1):
            qq = stacked_q(sb) if sb < nsub else None
            lsum, acc = None, None
            for c in range(nck):
                if sb < nsub:
                    e = jnp.exp2(logits(qq, c))
                    e_scr[sb % 2, c] = e
                    lsum = fold(lsum, e)
                if sb > 0:
                    acc = pv(acc, (sb - 1) % 2, c, stats[1])
            if sb > 0:
                o_scr[(sb - 1) * tqs:sb * tqs, :] = acc / stats[0]
            if sb < nsub:
                stats = row_stats(lsum)

    @pl.when(jnp.logical_not(no_shift))
    def _():
        for sb in range(nsub):
            qq = stacked_q(sb)
            m = None
            for c in range(nck):
                s = logits(qq, c)
                e_scr[0, c] = s
                mc = jnp.max(s, axis=-1, keepdims=True)
                m = mc if m is None else jnp.maximum(m, mc)
            lsum = None
            for c in range(nck):
                e = jnp.exp2(e_scr[0, c] - m)
                e_scr[0, c] = e
                lsum = fold(lsum, e)
            l1, rho = row_stats(lsum)
            acc = None
            for c in range(nck):
                acc = pv(acc, 0, c, rho)
            o_scr[sb * tqs:(sb + 1) * tqs, :] = acc / l1

    o = o_scr[...]
    y = o * lax.rsqrt(jnp.mean(o * o, axis=-1, keepdims=True) + EPS) * sw_ref[...]
    o_ref[...] = (y * out_scale).astype(BF16)


def _attn_call(par, lam_init, q2d, k2d, proj, v_row_off, n_b, t_q, t_kl, tq, subln_w, cache=None):
    hw = 2 * DA_QK
    nq = t_q // tq
    vcol = C_DV // DA_V
    in_specs = [pl.BlockSpec(memory_space=pltpu.SMEM),
                pl.BlockSpec((tq, hw), lambda b, h, qi: (b * nq + qi, h))]
    args = [par, q2d]
    n_chunks = t_kl // ATT_KC
    if cache is not None:
        kc, vc = cache
        p = kc.shape[1]
        n_chunks += p // ATT_KC
        in_specs += [pl.BlockSpec((None, p, hw), lambda b, h, qi: (b, 0, h)),
                     pl.BlockSpec((None, p, DA_V), lambda b, h, qi: (b, 0, h))]
        args += [kc, vc]
    in_specs += [pl.BlockSpec((t_kl, hw), lambda b, h, qi: (b, h)),
                 pl.BlockSpec((t_kl, DA_V), lambda b, h, qi: (v_row_off + b, vcol + h)),
                 pl.BlockSpec((1, DA_V), lambda b, h, qi: (0, 0))]
    args += [k2d, proj, subln_w.reshape(1, DA_V)]
    return pl.pallas_call(
        functools.partial(_attn_kernel, 1.0 - lam_init, cache is not None),
        grid=(n_b, DA_HEADS, nq),
        in_specs=in_specs,
        out_specs=pl.BlockSpec((tq, DA_V), lambda b, h, qi: (b * nq + qi, h)),
        out_shape=jax.ShapeDtypeStruct((n_b * t_q, DA_HEADS * DA_V), BF16),
        scratch_shapes=[pltpu.VMEM((2 if tq > ATT_TQ else 1, n_chunks, 2 * ATT_TQ, ATT_KC), F32),
                        pltpu.VMEM((tq, DA_V), F32)],
        compiler_params=_cparams(("arbitrary", "arbitrary", "arbitrary")),
        name="diff_attn_lat" if cache is not None else "diff_attn_ctx",
    )(*args)


def _ret_state_update(kt, v, kd, cd, s_old):
    parts = []
    for h in range(RET_HEADS):
        rows = slice(h * RET_QK, (h + 1) * RET_QK)
        kh = (kt[rows, :].astype(F32) * kd[rows, :]).astype(BF16)
        parts.append(jnp.dot(kh, v[:, h * RET_V:(h + 1) * RET_V], preferred_element_type=F32))
    return cd * s_old + jnp.concatenate(parts, axis=0)


def _ret_bwd_kernel(geom, kt_ref, v_ref, kd_ref, cd_ref, s0_ref, sstart_ref, send_ref, s_scr):
    i = geom.n_blocks - 1 - pl.program_id(0)

    @pl.when(geom.seq_end(i))
    def _():
        s_scr[...] = jnp.where(i >= geom.ctx_blocks, s0_ref[...].astype(F32), 0.0)

    s_old = s_scr[...]
    sstart_ref[...] = s_old
    kt = kt_ref[...] * jnp.asarray(RET_QK ** -0.5, BF16)
    s_new = _ret_state_update(kt, v_ref[...], kd_ref[...], cd_ref[...], s_old)
    s_scr[...] = s_new
    send_ref[...] = s_new


def _ret_bwd_call(geom, l, proj, rkt, kd_b, cd_b, s0):
    nb = geom.n_blocks
    hs = RET_HEADS * RET_QK

    def blk(g):
        return nb - 1 - g

    return pl.pallas_call(
        functools.partial(_ret_bwd_kernel, geom),
        grid=(nb,),
        in_specs=[pl.BlockSpec((hs, BLK), lambda g: (0, blk(g))),
                  pl.BlockSpec((BLK, RET_HEADS * RET_V), lambda g: (blk(g), C_RV // (RET_HEADS * RET_V))),
                  pl.BlockSpec((hs, BLK), lambda g: (0, 0)),
                  pl.BlockSpec((hs, RET_V), lambda g: (0, 0)),
                  pl.BlockSpec((None, None, None, hs, RET_V),
                               lambda g: (jnp.maximum(geom.seq_id(blk(g)) - geom.batch, 0), l, 1, 0, 0))],
        out_specs=[pl.BlockSpec((None, hs, RET_V), lambda g: (blk(g), 0, 0)),
                   pl.BlockSpec((None, hs, RET_V), lambda g: (blk(g), 0, 0))],
        out_shape=[jax.ShapeDtypeStruct((nb, hs, RET_V), F32),
                   jax.ShapeDtypeStruct((nb, hs, RET_V), F32)],
        scratch_shapes=[pltpu.VMEM((hs, RET_V), F32)],
        compiler_params=_cparams(("arbitrary",)),
        name="ret_bwd_state",
    )(rkt, proj, kd_b, cd_b, s0)


def _ret_main_kernel(geom, q_ref, kt_ref, v_ref, g_ref, dsum_ref, qdf_ref, qdb_ref, kd_ref, cd_ref,
                     s0_ref, sb_ref, o_ref, send_ref, s_scr):
    i = pl.program_id(0)

    @pl.when(geom.seq_start(i))
    def _():
        s_scr[...] = jnp.where(i >= geom.ctx_blocks, s0_ref[...].astype(F32), 0.0)

    s_f = s_scr[...]
    s_fb = s_f.astype(BF16)
    s_bb = sb_ref[...].astype(BF16)
    q = q_ref[...].astype(F32)
    kt = kt_ref[...] * jnp.asarray(RET_QK ** -0.5, BF16)
    v = v_ref[...]
    lane = lax.broadcasted_iota(jnp.int32, q.shape, 1)
    for h in range(RET_HEADS):
        in_head = (lane >= h * RET_QK) & (lane < (h + 1) * RET_QK)
        qh = jnp.where(in_head, q, 0.0)
        vh = v[:, h * RET_V:(h + 1) * RET_V]
        sc = jnp.dot(qh.astype(BF16), kt, preferred_element_type=F32) * dsum_ref[h]
        o = jnp.dot(sc.astype(BF16), vh, preferred_element_type=F32)
        o += jnp.dot((qh * qdf_ref[...]).astype(BF16), s_fb, preferred_element_type=F32)
        o += jnp.dot((qh * qdb_ref[...]).astype(BF16), s_bb, preferred_element_type=F32)
        y = o * lax.rsqrt(jnp.mean(o * o, axis=-1, keepdims=True) + EPS)
        gv = g_ref[:, h * RET_V:(h + 1) * RET_V].astype(F32)
        o_ref[:, h * RET_V:(h + 1) * RET_V] = (y * (gv * jax.nn.sigmoid(gv))).astype(BF16)
    s_new = _ret_state_update(kt, v, kd_ref[...], cd_ref[...], s_f)
    s_scr[...] = s_new
    send_ref[...] = s_new


def _ret_main_call(geom, l, proj, rkt, dsum, qdf, qdb, kd_f, cd_f, s0, sb_start):
    nb = geom.n_blocks
    hs = RET_HEADS * RET_QK
    hv = RET_HEADS * RET_V
    return pl.pallas_call(
        functools.partial(_ret_main_kernel, geom),
        grid=(nb,),
        in_specs=[pl.BlockSpec((BLK, hs), lambda g: (g, C_RQ // hs)),
                  pl.BlockSpec((hs, BLK), lambda g: (0, g)),
                  pl.BlockSpec((BLK, hv), lambda g: (g, C_RV // hv)),
                  pl.BlockSpec((BLK, hv), lambda g: (g, C_RG // hv)),
                  pl.BlockSpec((RET_HEADS, BLK, BLK), lambda g: (0, 0, 0)),
                  pl.BlockSpec((BLK, hs), lambda g: (0, 0)),
                  pl.BlockSpec((BLK, hs), lambda g: (0, 0)),
                  pl.BlockSpec((hs, BLK), lambda g: (0, 0)),
                  pl.BlockSpec((hs, RET_V), lambda g: (0, 0)),
                  pl.BlockSpec((None, None, None, hs, RET_V),
                               lambda g: (jnp.maximum(geom.seq_id(g) - geom.batch, 0), l, 0, 0, 0)),
                  pl.BlockSpec((None, hs, RET_V), lambda g: (g, 0, 0))],
        out_specs=[pl.BlockSpec((BLK, hv), lambda g: (g, 0)),
                   pl.BlockSpec((None, hs, RET_V), lambda g: (g, 0, 0))],
        out_shape=[jax.ShapeDtypeStruct((geom.n_tok, hv), BF16),
                   jax.ShapeDtypeStruct((nb, hs, RET_V), F32)],
        scratch_shapes=[pltpu.VMEM((hs, RET_V), F32)],
        compiler_params=_cparams(("arbitrary",)),
        name="ret_main",
    )(proj, rkt, proj, proj, dsum, qdf, qdb, kd_f, cd_f, s0, sb_start)


def _ret_tables(ret_decay_l):
    log_g = jax.nn.log_sigmoid(ret_decay_l.astype(F32))
    pos = jnp.arange(BLK, dtype=F32)
    diff = pos[:, None] - pos[None, :]
    lf = log_g[0][:, None, None]
    lb = log_g[1][:, None, None]
    dsum = (jnp.where(diff >= 0, jnp.exp(jnp.maximum(diff, 0.0)[None] * lf), 0.0)
            + jnp.where(diff <= 0, jnp.exp(jnp.maximum(-diff, 0.0)[None] * lb), 0.0))

    def per_lane(e, lg):
        return jnp.repeat(jnp.exp(e[:, None] * lg[None, :]), RET_QK, axis=1)

    qdf = per_lane(pos + 1.0, log_g[0])
    qdb = per_lane(BLK - pos, log_g[1])
    kd_f = per_lane(BLK - 1.0 - pos, log_g[0]).T
    kd_b = per_lane(pos, log_g[1]).T
    cd_f = jnp.broadcast_to(jnp.repeat(jnp.exp(BLK * log_g[0]), RET_QK)[:, None], (RET_HEADS * RET_QK, RET_V))
    cd_b = jnp.broadcast_to(jnp.repeat(jnp.exp(BLK * log_g[1]), RET_QK)[:, None], (RET_HEADS * RET_QK, RET_V))
    return dsum, qdf, qdb, kd_f, kd_b, cd_f, cd_b


def _merge_kernel(n_ctx_tiles, ba_ref, bbc_ref, bbl_ref, bc_ref, g0_ref, g1_ref, g2_ref, xc_ref, xl_ref,
                  gate_ref, sc_ref, sh_ref, nw_ref, wb_ref, wo_ref, rhi_ref, rlo_ref, x1_ref, h2_ref, h2p_ref,
                  lt_ref):
    branches = (ba_ref[...], _pick_part(n_ctx_tiles, bbc_ref, bbl_ref), bc_ref[...])
    acc = None
    for br, (b, g_ref) in enumerate(zip(branches, (g0_ref, g1_ref, g2_ref))):
        p = jnp.dot(b, wb_ref[br], preferred_element_type=F32)
        t = (0.5 * jnp.tanh(0.5 * g_ref[...].astype(F32)) + 0.5) * p
        acc = t if acc is None else acc + t
    m = jnp.dot(acc.astype(BF16), wo_ref[...], preferred_element_type=F32)
    x1 = _pick_part(n_ctx_tiles, xc_ref, xl_ref) + gate_ref[...] * m
    x1_ref[...] = x1
    ms = jnp.mean(x1 * x1, axis=-1, keepdims=True)
    h2 = x1 * lax.rsqrt(ms + EPS) * nw_ref[...] * (1.0 + sc_ref[...]) + sh_ref[...]
    h2b = h2.astype(BF16)
    h2_ref[...] = h2b
    h2p_ref[...] = _pack_halves(h2b.astype(F32))
    h2lo = (h2 - h2b.astype(F32)).astype(BF16)
    nt = (((1,), (1,)), ((), ()))
    lt_ref[...] = (lax.dot_general(rhi_ref[...], h2b, nt, preferred_element_type=F32)
                   + lax.dot_general(rhi_ref[...], h2lo, nt, preferred_element_type=F32)
                   + lax.dot_general(rlo_ref[...], h2b, nt, preferred_element_type=F32))


def _merge_call(geom, l, ba, bb_ctx, bb_lat, bc, proj, x_ctx, x_lat, mod6, norm2_w, wb_bf, wo_bf, r_hi, r_lo):
    tm = 512
    gcol = C_GL // D_MODEL
    full = lambda shape: pl.BlockSpec(shape, lambda i: tuple(0 for _ in shape))
    tok = lambda w: pl.BlockSpec((tm, w), lambda i: (i, 0))
    return pl.pallas_call(
        functools.partial(_merge_kernel, geom.n_ctx // tm),
        grid=(geom.n_tok // tm,),
        in_specs=[tok(BRANCH_W)] + _split_in_specs(geom, tm, BRANCH_W, 1) + [tok(BRANCH_W),
                  pl.BlockSpec((tm, D_MODEL), lambda i: (i, gcol)),
                  pl.BlockSpec((tm, D_MODEL), lambda i: (i, gcol + 1)),
                  pl.BlockSpec((tm, D_MODEL), lambda i: (i, gcol + 2))]
                 + _split_in_specs(geom, tm, D_MODEL, 1) + [
                  _mod_spec(geom, l, 2, tm, 1), _mod_spec(geom, l, 4, tm, 1), _mod_spec(geom, l, 3, tm, 1),
                  full((1, D_MODEL)),
                  full((N_BRANCH, BRANCH_W, D_MODEL)), full((D_MODEL, D_MODEL)),
                  full((N_EXPERTS, D_MODEL)), full((N_EXPERTS, D_MODEL))],
        out_specs=[tok(D_MODEL), tok(D_MODEL), tok(D_MODEL // 2), pl.BlockSpec((N_EXPERTS, tm), lambda i: (0, i))],
        out_shape=[jax.ShapeDtypeStruct((geom.n_tok, D_MODEL), F32),
                   jax.ShapeDtypeStruct((geom.n_tok, D_MODEL), BF16),
                   jax.ShapeDtypeStruct((geom.n_tok, D_MODEL // 2), jnp.uint32),
                   jax.ShapeDtypeStruct((N_EXPERTS, geom.n_tok), F32)],
        compiler_params=_cparams(("arbitrary",)),
        name="merge_out",
    )(ba, bb_ctx, bb_lat, bc, proj, proj, proj, x_ctx, x_lat, mod6, mod6, mod6,
      norm2_w.reshape(1, D_MODEL), wb_bf, wo_bf, r_hi, r_lo)


def _router_kernel(lt_ref, bias_ref, ltri_ref, utri_ref, g_ref, slot_ref, cnt_ref, cnt_scr):
    per = N_EXPERTS // N_GROUPS
    tm = lt_ref.shape[1]
    scores = jax.nn.sigmoid(lt_ref[...])
    biased = scores + bias_ref[...]
    b3 = biased.reshape(N_GROUPS, per, tm)
    neg = jnp.float32(-jnp.inf)
    m1 = jnp.max(b3, axis=1, keepdims=True)
    is_m1 = b3 == m1
    cnt = jnp.sum(is_m1.astype(F32), axis=1, keepdims=True)
    m2 = jnp.max(jnp.where(is_m1, neg, b3), axis=1, keepdims=True)
    grp = (m1 + jnp.where(cnt >= 2.0, m1, m2)).reshape(N_GROUPS, tm)
    gidx = lax.broadcasted_iota(jnp.int32, (N_GROUPS, tm), 0)
    grank = jnp.zeros((N_GROUPS, tm), F32)
    for g2 in range(N_GROUPS):
        other = grp[g2:g2 + 1, :]
        ahead = (other > grp) | ((other == grp) & (gidx > g2))
        grank += ahead.astype(F32)
    gsel = (grank < float(TOPK_GROUPS)).astype(F32)
    emask = jnp.broadcast_to(gsel.reshape(N_GROUPS, 1, tm), (N_GROUPS, per, tm)).reshape(N_EXPERTS, tm)
    masked = jnp.where(emask > 0.0, biased, neg)
    eidx = lax.broadcasted_iota(jnp.int32, (N_EXPERTS, tm), 0)
    erank = jnp.zeros((N_EXPERTS, tm), F32)
    for e2 in range(N_EXPERTS):
        other = masked[e2:e2 + 1, :]
        ahead = (other > masked) | ((other == masked) & (eidx > e2))
        erank += ahead.astype(F32)
    sel = erank < float(TOP_K)
    w = jnp.where(sel, scores, 0.0)
    gates_t = w / jnp.sum(w, axis=0, keepdims=True) * ROUTED_SCALE

    @pl.when(pl.program_id(0) == 0)
    def _():
        cnt_scr[...] = jnp.zeros_like(cnt_scr)

    selb = sel.astype(BF16)
    slot = jnp.dot(ltri_ref[...], selb, preferred_element_type=F32)
    carry = cnt_scr[:, 0:1]
    rank = jnp.dot(selb, utri_ref[...], preferred_element_type=F32) + carry
    cnt_new = cnt_scr[...] + jnp.sum(sel.astype(F32), axis=1, keepdims=True)
    cnt_scr[...] = cnt_new
    cnt_ref[...] = cnt_new
    eid_f = eidx.astype(F32)
    g_rows, e_rows, r_rows = [], [], []
    for k in range(TOP_K):
        mk = jnp.where(sel & (slot == float(k)), 1.0, 0.0)
        g_rows.append(jnp.sum(mk * gates_t, axis=0, keepdims=True))
        e_rows.append(jnp.sum(mk * eid_f, axis=0, keepdims=True))
        r_rows.append(jnp.sum(mk * rank, axis=0, keepdims=True))
    slot_ref[...] = jnp.concatenate(e_rows + r_rows, axis=0).astype(jnp.int32)
    pad = jnp.zeros((GATE_W - TOP_K, tm), F32)
    g_ref[...] = jnp.concatenate(g_rows + [pad], axis=0).T


ROUTER_TM = 512


def _router_call(geom, logits_t, bias):
    tm = ROUTER_TM
    ltri = jnp.asarray(np.tril(np.ones((N_EXPERTS, N_EXPERTS), np.float32), -1), BF16)
    utri = jnp.asarray(np.triu(np.ones((tm, tm), np.float32), 1), BF16)
    return pl.pallas_call(
        _router_kernel,
        grid=(geom.n_tok // tm,),
        in_specs=[pl.BlockSpec((N_EXPERTS, tm), lambda i: (0, i)),
                  pl.BlockSpec((N_EXPERTS, 1), lambda i: (0, 0)),
                  pl.BlockSpec((N_EXPERTS, N_EXPERTS), lambda i: (0, 0)),
                  pl.BlockSpec((tm, tm), lambda i: (0, 0))],
        out_specs=[pl.BlockSpec((tm, GATE_W), lambda i: (i, 0)),
                   pl.BlockSpec((2 * TOP_K, tm), lambda i: (0, i)),
                   pl.BlockSpec((N_EXPERTS, GATE_W), lambda i: (0, 0))],
        out_shape=[jax.ShapeDtypeStruct((geom.n_tok, GATE_W), F32),
                   jax.ShapeDtypeStruct((2 * TOP_K, geom.n_tok), jnp.int32),
                   jax.ShapeDtypeStruct((N_EXPERTS, GATE_W), F32)],
        scratch_shapes=[pltpu.VMEM((N_EXPERTS, GATE_W), F32)],
        compiler_params=_cparams(("arbitrary",)),
        name="router",
    )(logits_t, bias.reshape(N_EXPERTS, 1), ltri, utri)


MOE_TR = 512
SC_CORES = 2
SC_SUBCORES = 16
SC_CHUNK = 64


def _sc_worker_base(rows_per_worker):
    wid = lax.axis_index("s") * SC_CORES + lax.axis_index("c")
    return wid * rows_per_worker


def _sc_scatter_rows(table, pos_flat, n_slots, n_rows_out):
    n, d = table.shape
    nw = SC_CORES * SC_SUBCORES
    assert n % (nw * SC_CHUNK) == 0
    per_w = n // nw
    mesh = plsc.VectorSubcoreMesh(core_axis_name="c", subcore_axis_name="s")

    @functools.partial(
        pl.kernel, mesh=mesh,
        out_type=jax.ShapeDtypeStruct((n_rows_out, d), table.dtype),
        scratch_types=[[pltpu.VMEM((SC_CHUNK,), jnp.int32) for _ in range(n_slots)],
                       pltpu.VMEM((SC_CHUNK, d), table.dtype),
                       pltpu.SemaphoreType.DMA],
    )
    def scatter(table_hbm, pos_hbm, out_hbm, idx_v, rows_v, sem):
        base = _sc_worker_base(per_w)

        @pl.loop(0, per_w // SC_CHUNK)
        def _(ci):
            off = pl.multiple_of(base + ci * SC_CHUNK, 8)
            for k in range(n_slots):
                pltpu.sync_copy(pos_hbm.at[pl.ds(pl.multiple_of(k * n + off, 8), SC_CHUNK)], idx_v[k])
            pltpu.sync_copy(table_hbm.at[pl.ds(off, SC_CHUNK)], rows_v)
            copies = [pltpu.make_async_copy(rows_v, out_hbm.at[idx_v[k]], sem) for k in range(n_slots)]
            for cp in copies:
                cp.start()
            for cp in copies:
                cp.wait()

    return scatter(table, pos_flat)


def _sc_gather_rows(table, idx):
    b = idx.shape[0]
    d = table.shape[1]
    nw = SC_CORES * SC_SUBCORES
    nbuf = 2
    assert b % (nw * SC_CHUNK * nbuf) == 0
    per_w = b // nw
    n_chunks = per_w // SC_CHUNK
    mesh = plsc.VectorSubcoreMesh(core_axis_name="c", subcore_axis_name="s")

    @functools.partial(
        pl.kernel, mesh=mesh,
        out_type=jax.ShapeDtypeStruct((b, d), table.dtype),
        scratch_types=[pltpu.VMEM((per_w,), jnp.int32),
                       [pltpu.VMEM((SC_CHUNK, d), table.dtype) for _ in range(nbuf)],
                       [pltpu.SemaphoreType.DMA for _ in range(nbuf)],
                       [pltpu.SemaphoreType.DMA for _ in range(nbuf)]],
    )
    def gather(table_hbm, idx_hbm, out_hbm, idx_v, rows, gsem, wsem):
        base = _sc_worker_base(per_w)
        pltpu.sync_copy(idx_hbm.at[pl.ds(pl.multiple_of(base, 8), per_w)], idx_v)

        def fetch(ci, slot):
            src = table_hbm.at[idx_v.at[pl.ds(pl.multiple_of(ci * SC_CHUNK, 8), SC_CHUNK)]]
            return pltpu.make_async_copy(src, rows[slot], gsem[slot])

        def put(ci, slot):
            dst = out_hbm.at[pl.ds(pl.multiple_of(base + ci * SC_CHUNK, 8), SC_CHUNK)]
            return pltpu.make_async_copy(rows[slot], dst, wsem[slot])

        for slot in range(nbuf):
            fetch(slot, slot).start()

        @pl.loop(0, n_chunks, step=nbuf)
        def _(c0):
            for slot in range(nbuf):
                ci = c0 + slot
                fetch(ci, slot).wait()
                put(ci, slot).start()
                put(ci, slot).wait()

                @pl.when(ci + nbuf < n_chunks)
                def _():
                    fetch(ci + nbuf, slot).start()

    return gather(table, idx)


def _route_positions(n_tok, slots, counts):
    cnt = counts[:, 0].astype(jnp.int32)
    cnt_pad = ((cnt + MOE_TR - 1) // MOE_TR) * MOE_TR
    off_end = jnp.cumsum(cnt_pad)
    off = off_end - cnt_pad
    eid, rank = slots[:TOP_K], slots[TOP_K:]
    eids = jnp.arange(N_EXPERTS, dtype=jnp.int32)
    pos = jnp.sum(jnp.where(eid[..., None] == eids, off, 0), axis=-1) + rank
    n_tiles = (TOP_K * n_tok) // MOE_TR + N_EXPERTS
    tile_start = jnp.arange(n_tiles, dtype=jnp.int32) * MOE_TR
    tile_expert = jnp.sum((tile_start[:, None] >= off_end[None, :]).astype(jnp.int32), axis=1)
    tile_expert = jnp.minimum(tile_expert, N_EXPERTS - 1)
    n_used = (off_end[-1] // MOE_TR).reshape(1)
    tile_idx = jnp.arange(n_tiles, dtype=jnp.int32)
    used = tile_idx < n_used[0]
    prev = jnp.concatenate([jnp.full((1,), -1, jnp.int32), tile_expert[:-1]])
    first = jnp.logical_and(used, tile_expert != prev)
    parity = (jnp.cumsum(first.astype(jnp.int32)) - 1) % 2
    later = jnp.logical_and(eids[None, :] > eids[:, None], (cnt_pad > 0)[None, :])
    next_e = jnp.min(jnp.where(later, eids[None, :], N_EXPERTS), axis=1)
    nxt = jnp.sum(jnp.where(tile_expert[:, None] == eids, next_e, 0), axis=1)
    has_next = jnp.logical_and(first, nxt < N_EXPERTS)
    sched = (tile_expert, n_used, first.astype(jnp.int32), jnp.minimum(nxt, N_EXPERTS - 1).astype(jnp.int32),
             jnp.maximum(parity, 0).astype(jnp.int32), has_next.astype(jnp.int32))
    return pos, sched, n_tiles


def _expert_ffn(x_lo, x_hi, gu, dn):
    half = D_MODEL // 2
    a = (jnp.dot(x_lo, gu[0:half, :], preferred_element_type=F32)
         + jnp.dot(x_hi, gu[half:, :], preferred_element_type=F32))
    hg = a[:, :D_EXPERT]
    act = (hg * jax.nn.sigmoid(hg)) * a[:, D_EXPERT:]
    return jnp.dot(act.astype(BF16), dn, preferred_element_type=F32)


def _experts_kernel(l, te_ref, nu_ref, first_ref, nxt_ref, par_ref, hasn_ref, x_ref, gu_hbm, dn_hbm, y_ref,
                    gu_f, dn_f, gu_b, dn_b, sem):
    i = pl.program_id(0)

    def fetch(e, slot):
        return (pltpu.make_async_copy(gu_hbm.at[l, e], gu_f.at[slot], sem.at[0, slot]),
                pltpu.make_async_copy(dn_hbm.at[l, e], dn_f.at[slot], sem.at[1, slot]))

    @pl.when(jnp.logical_and(i == 0, nu_ref[0] > 0))
    def _():
        for cp in fetch(te_ref[0], par_ref[0]):
            cp.start()

    @pl.when(first_ref[i] == 1)
    def _():
        slot = par_ref[i]
        for cp in fetch(te_ref[i], slot):
            cp.wait()

        @pl.when(hasn_ref[i] == 1)
        def _():
            for cp in fetch(nxt_ref[i], 1 - slot):
                cp.start()

        gu_b[...] = gu_f[slot].astype(BF16)
        dn_b[...] = dn_f[slot].astype(BF16)

    @pl.when(i < nu_ref[0])
    def _():
        lo, hi = _unpack_halves(x_ref[...])
        y = _expert_ffn(lo.astype(BF16), hi.astype(BF16), gu_b[...], dn_b[...])
        y_ref[...] = _pack_halves(y.astype(BF16).astype(F32))

    @pl.when(i >= nu_ref[0])
    def _():
        y_ref[...] = jnp.zeros_like(y_ref)


def _experts_call(l, xs, sched, n_tiles, w_gu, w_dn):
    half = D_MODEL // 2
    grid_spec = pltpu.PrefetchScalarGridSpec(
        num_scalar_prefetch=len(sched),
        grid=(n_tiles,),
        in_specs=[pl.BlockSpec((MOE_TR, half), lambda i, te, nu, *_: (jnp.minimum(i, jnp.maximum(nu[0], 1) - 1), 0)),
                  pl.BlockSpec(memory_space=pl.ANY),
                  pl.BlockSpec(memory_space=pl.ANY)],
        out_specs=pl.BlockSpec((MOE_TR, half), lambda i, *_: (i, 0)),
        scratch_shapes=[pltpu.VMEM((2, D_MODEL, 2 * D_EXPERT), w_gu.dtype),
                        pltpu.VMEM((2, D_EXPERT, D_MODEL), w_dn.dtype),
                        pltpu.VMEM((D_MODEL, 2 * D_EXPERT), BF16),
                        pltpu.VMEM((D_EXPERT, D_MODEL), BF16),
                        pltpu.SemaphoreType.DMA((2, 2))],
    )
    return pl.pallas_call(
        functools.partial(_experts_kernel, l),
        grid_spec=grid_spec,
        out_shape=jax.ShapeDtypeStruct((n_tiles * MOE_TR, half), jnp.uint32),
        compiler_params=_cparams(("arbitrary",)),
        name="moe_experts",
    )(*sched, xs, w_gu, w_dn)


MOE_OUT_PARTS = 2


def _moe_out_kernel(n_ctx_tiles, tile0, first, *refs):
    if first:
        yt_ref, g_ref, h_ref, sgu_ref, sdn_ref, x1_ref, gate_ref, oc_ref, ol_ref = refs
    else:
        yt_ref, g_ref, h_ref, sgu_ref, sdn_ref, x1_ref, gate_ref, _, ol_ref = refs
    i = pl.program_id(0) + tile0
    gts = g_ref[...]
    lane = lax.broadcasted_iota(jnp.int32, gts.shape, 1)
    acc_lo, acc_hi = None, None
    for k in range(TOP_K):
        ge = jnp.sum(jnp.where(lane == k, gts, 0.0), axis=1, keepdims=True)
        lo, hi = _unpack_halves(yt_ref[k])
        acc_lo = ge * lo if acc_lo is None else acc_lo + ge * lo
        acc_hi = ge * hi if acc_hi is None else acc_hi + ge * hi
    routed = jnp.concatenate([acc_lo, acc_hi], axis=1)
    h = h_ref[...]
    half = D_MODEL // 2
    shared = _expert_ffn(h[:, :half], h[:, half:], sgu_ref[...], sdn_ref[...])
    y = x1_ref[...] + gate_ref[...] * (routed + shared)
    if first:
        @pl.when(i < n_ctx_tiles)
        def _():
            oc_ref[...] = y

        @pl.when(i >= n_ctx_tiles)
        def _():
            ol_ref[...] = y
    else:
        ol_ref[...] = y


def _moe_out_call(geom, l, part, yt, gates, h2, sgu_bf, sdn_bf, x1, mod6, prev_lat=None):
    tm = 512
    nct = geom.n_ctx // tm
    n_part = geom.n_tok // tm // MOE_OUT_PARTS
    t0 = part * n_part
    first = part == 0
    assert nct <= n_part
    half = D_MODEL // 2
    tok = lambda w: pl.BlockSpec((tm, w), lambda i: (i + t0, 0))
    in_specs = [pl.BlockSpec((TOP_K, tm, half), lambda i: (0, i, 0)),
                tok(GATE_W), tok(D_MODEL),
                pl.BlockSpec((None, D_MODEL, 2 * D_EXPERT), lambda i: (l, 0, 0)),
                pl.BlockSpec((None, D_EXPERT, D_MODEL), lambda i: (l, 0, 0)),
                tok(D_MODEL),
                pl.BlockSpec((None, None, None, 1, D_MODEL), lambda i: (l, geom.mod_row(i + t0, tm), 5, 0, 0))]
    args = [yt, gates, h2, sgu_bf, sdn_bf, x1, mod6]
    lat_shape = jax.ShapeDtypeStruct((geom.n_lat, D_MODEL), F32)
    if first:
        out_specs = [pl.BlockSpec((tm, D_MODEL), lambda i: (jnp.minimum(i, nct - 1), 0)),
                     pl.BlockSpec((tm, D_MODEL), lambda i: (jnp.maximum(i - nct, 0), 0))]
        out_shape = [jax.ShapeDtypeStruct((geom.n_ctx, D_MODEL), F32), lat_shape]
        aliases = {}
    else:
        in_specs.append(pl.BlockSpec(memory_space=pl.ANY))
        args.append(prev_lat)
        out_specs = [pl.BlockSpec((tm, D_MODEL), lambda i: (i + t0 - nct, 0))]
        out_shape = [lat_shape]
        aliases = {len(args) - 1: 0}
    return pl.pallas_call(
        functools.partial(_moe_out_kernel, nct, t0, first),
        grid=(n_part,),
        in_specs=in_specs, out_specs=out_specs, out_shape=out_shape,
        input_output_aliases=aliases,
        compiler_params=_cparams(("arbitrary",)),
        name="moe_out",
    )(*args)


def _moe(geom, l, h2, h2p, gates, slots, counts, w_gu, w_dn, sgu_bf, sdn_bf, x1, mod6):
    pos, sched, n_tiles = _route_positions(geom.n_tok, slots, counts)
    xs = _sc_scatter_rows(h2p, pos.reshape(-1), TOP_K, n_tiles * MOE_TR)
    ys = _experts_call(l, xs, sched, n_tiles, w_gu, w_dn)
    n_part = geom.n_tok // MOE_OUT_PARTS
    y_ctx, y_lat = None, None
    for part in range(MOE_OUT_PARTS):
        pos_p = pos[:, part * n_part:(part + 1) * n_part].reshape(-1)
        yt = _sc_gather_rows(ys, pos_p).reshape(TOP_K, n_part, D_MODEL // 2)
        outs = _moe_out_call(geom, l, part, yt, gates, h2, sgu_bf, sdn_bf, x1, mod6, y_lat)
        if part == 0:
            y_ctx, y_lat = outs
        else:
            (y_lat,) = outs
    return y_ctx, y_lat


def _rope_tables(dec_seq):
    rows = dec_seq // GRID_W
    row = jnp.repeat(jnp.arange(rows, dtype=F32), GRID_W)
    col = jnp.tile(jnp.arange(GRID_W, dtype=F32), rows)
    inv = ROPE_BASE ** (-jnp.arange(ROPE_PAIRS, dtype=F32) / ROPE_PAIRS)
    ar = row[:, None] * inv[None, :]
    ac = col[:, None] * inv[None, :]
    cos64 = jnp.concatenate([jnp.cos(ar), jnp.cos(ar), jnp.cos(ac), jnp.cos(ac)], axis=1)
    sin64 = jnp.concatenate([-jnp.sin(ar), jnp.sin(ar), -jnp.sin(ac), jnp.sin(ac)], axis=1)
    return jnp.tile(cos64, (1, 2)), jnp.tile(sin64, (1, 2))


def _block_diag_gate(wg_dir):
    eye = jnp.eye(LRU_BLOCKS, dtype=F32)
    dense = jnp.einsum('gnij,nm->gnimj', wg_dir.astype(F32), eye).reshape(2, D_RNN, D_RNN)
    return jnp.concatenate([dense[0], dense[1]], axis=1)


def kernel(x_prompt, x_sample, cache_k, cache_v, state_lru, state_ret, c, c_ctx, ada_w, ada_b, norm1_w, norm2_w, w_in, conv_w, conv_b, lru_gate_w, lru_gate_b, lru_lambda, q_norm_w, k_norm_w, diff_lambda, subln_w, ret_decay, w_branch, w_out, router_w, router_bias, w_exp_gu, w_exp_down, w_sh_gu, w_sh_down):
    batch, seq, _ = x_prompt.shape
    dec_batch, dec_seq, _ = x_sample.shape
    assert 1 + dec_batch <= MOD_ROWS
    geom = _Geom(batch, seq, dec_batch, dec_seq)
    hs = RET_HEADS * RET_QK
    aw = DA_HEADS * 2 * DA_QK

    x_ctx = x_prompt.reshape(geom.n_ctx, D_MODEL)
    x_lat = x_sample.reshape(geom.n_lat, D_MODEL)
    cvec = jnp.zeros((MOD_ROWS, D_MODEL), F32).at[0].set(c_ctx).at[1:1 + dec_batch].set(c)
    mod6 = _ada_call(cvec, ada_w, ada_b).reshape(DEPTH, MOD_ROWS, 6, 1, D_MODEL)

    ones_bd = jnp.kron(jnp.eye(aw // DA_QK, dtype=F32), jnp.ones((DA_QK, DA_QK), F32)).astype(BF16)
    cos_t, sin_t = _rope_tables(dec_seq)

    w_in_bf = w_in.astype(BF16)
    sgu_bf, sdn_bf = w_sh_gu.astype(BF16), w_sh_down.astype(BF16)

    new_cache, lrus, rets = None, [], []
    for l in range(DEPTH):
        lam_init = 0.8 - 0.6 * math.exp(-0.3 * l)
        w_rkt_bf = w_in[l][:, C_RK:C_RK + hs].T.astype(BF16)
        proj, rkt = _inproj_call(geom, l, x_ctx, x_lat, mod6, norm1_w[l], w_in_bf, w_rkt_bf)

        sp = jax.nn.softplus(-lru_lambda[l].astype(F32))
        h0 = jnp.concatenate([jnp.zeros((batch, 2, D_RNN), F32), state_lru[:, l].astype(F32)], axis=0)
        h0 = h0.reshape(geom.n_seq, 2, 1, D_RNN)
        cb = conv_b[l].reshape(1, D_RNN)
        lru_args = []
        for d in range(2):
            lru_args.append((_block_diag_gate(lru_gate_w[l, d]).astype(BF16),
                             lru_gate_b[l, d].reshape(1, 2 * D_RNN), sp[d].reshape(1, D_RNN)))
        hf, hf_last = _lru_call(geom, False, proj, conv_w[l], cb, *lru_args[0], h0)
        branch_a, hb_last = _lru_call(geom, True, proj, conv_w[l], cb, *lru_args[1], h0, hf)

        qw = jnp.tile(q_norm_w[l], aw // DA_QK).reshape(1, aw)
        kw = jnp.tile(k_norm_w[l], aw // DA_QK).reshape(1, aw)
        q_c, k_c, *new_cache = _prep_call(geom, False, proj, qw, kw, ones_bd, layer=l, prev_cache=new_cache)
        q_l, k_l = _prep_call(geom, True, proj, qw, kw, ones_bd, cos_t, sin_t)
        lam_p = diff_lambda[l].astype(F32)
        lam = jnp.exp(jnp.sum(lam_p[0] * lam_p[1])) - jnp.exp(jnp.sum(lam_p[2] * lam_p[3])) + lam_init
        q_bound = DA_QK * jnp.max(jnp.square(q_norm_w[l].astype(F32))) * (DA_QK ** -0.5 * LOG2E) ** 2
        k_bound = DA_QK * jnp.max(jnp.square(k_norm_w[l].astype(F32)))
        kc32 = cache_k[:, l].astype(F32)
        kc_bound = jnp.maximum(k_bound, jnp.max(jnp.sum(jnp.square(kc32), axis=-1)))

        def attn_par(kb):
            ok = (q_bound * kb * 1.05 < ATT_SAFE_LOGIT ** 2).astype(F32)
            return jnp.stack([lam, ok])

        assert geom.n_ctx % dec_seq == 0
        cache = (kc32.reshape(dec_batch, -1, aw).astype(BF16),
                 cache_v[:, l].reshape(dec_batch, -1, DA_HEADS * DA_V).astype(BF16))
        att_c = _attn_call(attn_par(k_bound), lam_init, q_c, k_c, proj, 0, batch, seq, seq, 256, subln_w[l])
        att_l = _attn_call(attn_par(kc_bound), lam_init, q_l, k_l, proj, geom.n_ctx // dec_seq, dec_batch,
                           dec_seq, dec_seq, min(4 * ATT_TQ, dec_seq), subln_w[l], cache)

        dsum, qdf, qdb, kd_f, kd_b, cd_f, cd_b = _ret_tables(ret_decay[l])
        s0 = state_ret.reshape(dec_batch, DEPTH, 2, hs, RET_V)
        sb_start, sb_end = _ret_bwd_call(geom, l, proj, rkt, kd_b, cd_b, s0)
        branch_c, sf_end = _ret_main_call(geom, l, proj, rkt, dsum, qdf, qdb, kd_f, cd_f, s0, sb_start)

        r_t = router_w[l].T.astype(F32)
        r_hi = r_t.astype(BF16)
        r_lo = (r_t - r_hi.astype(F32)).astype(BF16)
        x1, h2, h2p, logits_t = _merge_call(geom, l, branch_a, att_c, att_l, branch_c, proj, x_ctx, x_lat, mod6,
                                            norm2_w[l], w_branch[l].astype(BF16), w_out[l].astype(BF16), r_hi, r_lo)
        gates, slots, counts = _router_call(geom, logits_t, router_bias[l].astype(F32))
        x_ctx, x_lat = _moe(geom, l, h2, h2p, gates, slots, counts, w_exp_gu, w_exp_down, sgu_bf, sdn_bf, x1, mod6)

        lrus.append(jnp.stack([hf_last[:batch, 0], hb_last[:batch, 0]], axis=1))
        rets.append(jnp.stack([sf_end[:batch].reshape(batch, RET_HEADS, RET_QK, RET_V),
                               sb_end[:batch].reshape(batch, RET_HEADS, RET_QK, RET_V)], axis=1))

    y_prompt = x_ctx.reshape(batch, seq, D_MODEL)
    y_sample = x_lat.reshape(dec_batch, dec_seq, D_MODEL)
    new_k = new_cache[0].reshape(batch, DEPTH, seq, DA_HEADS, 2, DA_QK)
    new_v = new_cache[1].reshape(batch, DEPTH, seq, DA_HEADS, DA_V)
    return (y_prompt, y_sample, new_k, new_v, jnp.stack(lrus, axis=1), jnp.stack(rets, axis=1))
```

```python
import functools
import math

import numpy as np
import jax
import jax.numpy as jnp
from jax import lax
from jax.experimental import pallas as pl
from jax.experimental.pallas import tpu as pltpu
from jax.experimental.pallas import tpu_sc as plsc

F32 = jnp.float32
BF16 = jnp.bfloat16

D_MODEL = 1024
DEPTH = 2
GRID_W = 64
D_RNN = 512
LRU_BLOCKS = 8
LRU_BLOCK = D_RNN // LRU_BLOCKS
CONV_W = 4
LRU_C = 8.0
DA_HEADS = 4
DA_QK = 64
DA_V = 128
ROPE_PAIRS = DA_QK // 4
ROPE_BASE = 10000.0
RET_HEADS = 4
RET_QK = 64
RET_V = 128
BRANCH_W = 512
N_BRANCH = 3
D_IN = 7168
N_EXPERTS = 64
TOP_K = 8
N_GROUPS = 8
TOPK_GROUPS = 4
D_EXPERT = 256
ROUTED_SCALE = 2.5
EPS = 1e-6

C_XA, C_GA, C_DQ, C_DK, C_DV = 0, 512, 1024, 1536, 2048
C_RQ, C_RK, C_RV, C_RG, C_GL = 2560, 2816, 3072, 3584, 4096

BLK = 256
LRU_SUB = 8
GATE_W = 128
MOD_ROWS = 8
VMEM_LIMIT = 56 * 1024 * 1024


def _cparams(sem, vmem_limit=VMEM_LIMIT):
    return pltpu.CompilerParams(dimension_semantics=sem, vmem_limit_bytes=vmem_limit)


class _Geom:
    def __init__(self, batch, seq, dec_batch, dec_seq):
        assert seq == BLK and dec_seq % BLK == 0
        self.batch, self.seq, self.dec_batch, self.dec_seq = batch, seq, dec_batch, dec_seq
        self.n_ctx = batch * seq
        self.n_lat = dec_batch * dec_seq
        self.n_tok = self.n_ctx + self.n_lat
        self.ctx_blocks = self.n_ctx // BLK
        self.lat_blocks = dec_seq // BLK
        self.n_blocks = self.n_tok // BLK
        self.n_seq = batch + dec_batch

    def mod_row(self, i, tile):
        nct = self.n_ctx // tile
        per = self.dec_seq // tile
        return jnp.where(i < nct, 0, 1 + (i - nct) // per)

    def seq_id(self, i):
        return jnp.where(i < self.ctx_blocks, i, self.ctx_blocks + (i - self.ctx_blocks) // self.lat_blocks)

    def seq_start(self, i):
        return jnp.logical_or(i < self.ctx_blocks, (i - self.ctx_blocks) % self.lat_blocks == 0)

    def seq_end(self, i):
        return jnp.logical_or(i < self.ctx_blocks, (i - self.ctx_blocks) % self.lat_blocks == self.lat_blocks - 1)


def _ada_kernel(c_ref, w_ref, b_ref, o_ref):
    cv = c_ref[...]
    s = cv * jax.nn.sigmoid(cv)
    o_ref[...] = jnp.dot(s, w_ref[...], preferred_element_type=F32,
                         precision=lax.Precision.HIGHEST) + b_ref[...]


def _ada_call(cvec, ada_w, ada_b):
    depth = ada_w.shape[0]
    nt = 6
    return pl.pallas_call(
        _ada_kernel,
        grid=(depth, nt),
        in_specs=[pl.BlockSpec((MOD_ROWS, D_MODEL), lambda l, j: (0, 0)),
                  pl.BlockSpec((None, D_MODEL, D_MODEL), lambda l, j: (l, 0, j)),
                  pl.BlockSpec((None, 1, D_MODEL), lambda l, j: (l, 0, j))],
        out_specs=pl.BlockSpec((None, MOD_ROWS, D_MODEL), lambda l, j: (l, 0, j)),
        out_shape=jax.ShapeDtypeStruct((depth, MOD_ROWS, 6 * D_MODEL), F32),
        compiler_params=_cparams(("arbitrary", "arbitrary")),
        name="ada_mod",
    )(cvec, ada_w, ada_b.reshape(depth, 1, 6 * D_MODEL))


def _mod_spec(geom, l, which, tile, ngrid):
    if ngrid == 1:
        return pl.BlockSpec((None, None, None, 1, D_MODEL),
                            lambda i: (l, geom.mod_row(i, tile), which, 0, 0))
    return pl.BlockSpec((None, None, None, 1, D_MODEL),
                        lambda i, j: (l, geom.mod_row(i, tile), which, 0, 0))


def _split_in_specs(geom, tile, width, ngrid):
    nct = geom.n_ctx // tile
    if ngrid == 1:
        return [pl.BlockSpec((tile, width), lambda i: (jnp.minimum(i, nct - 1), 0)),
                pl.BlockSpec((tile, width), lambda i: (jnp.maximum(i - nct, 0), 0))]
    return [pl.BlockSpec((tile, width), lambda i, j: (jnp.minimum(i, nct - 1), 0)),
            pl.BlockSpec((tile, width), lambda i, j: (jnp.maximum(i - nct, 0), 0))]


def _pick_part(n_ctx_tiles, c_ref, l_ref):
    return jnp.where(pl.program_id(0) < n_ctx_tiles, c_ref[...], l_ref[...])


def _pack_halves(y):
    w = y.shape[1] // 2
    bits = pltpu.bitcast(y, jnp.uint32)
    return (bits[:, :w] >> 16) | (bits[:, w:] & jnp.uint32(0xFFFF0000))


def _unpack_halves(p):
    return pltpu.bitcast(p << 16, F32), pltpu.bitcast(p & jnp.uint32(0xFFFF0000), F32)


INPROJ_TM = 512
INPROJ_TN = 1024


def _inproj_kernel(n_ctx_tiles, xc_ref, xl_ref, sc_ref, sh_ref, nw_ref, w_ref, wkt_ref, o_ref, kt_ref):
    x = _pick_part(n_ctx_tiles, xc_ref, xl_ref)
    ms = jnp.mean(x * x, axis=-1, keepdims=True)
    y = x * lax.rsqrt(ms + EPS) * nw_ref[...]
    hb = (y * (1.0 + sc_ref[...]) + sh_ref[...]).astype(BF16)
    kt_ref[...] = lax.dot_general(wkt_ref[...], hb, (((1,), (1,)), ((), ())),
                                  preferred_element_type=F32).astype(BF16)
    for j in range(D_IN // INPROJ_TN):
        cols = slice(j * INPROJ_TN, (j + 1) * INPROJ_TN)
        o_ref[:, cols] = jnp.dot(hb, w_ref[:, cols], preferred_element_type=F32).astype(BF16)


def _inproj_call(geom, l, x_ctx, x_lat, mod6, norm_w, w_in_bf, w_rkt_bf):
    tm = INPROJ_TM
    return pl.pallas_call(
        functools.partial(_inproj_kernel, geom.n_ctx // tm),
        grid=(geom.n_tok // tm,),
        in_specs=_split_in_specs(geom, tm, D_MODEL, 1) + [
                  _mod_spec(geom, l, 1, tm, 1),
                  _mod_spec(geom, l, 0, tm, 1),
                  pl.BlockSpec((1, D_MODEL), lambda i: (0, 0)),
                  pl.BlockSpec((None, D_MODEL, D_IN), lambda i: (l, 0, 0), pipeline_mode=pl.Buffered(1)),
                  pl.BlockSpec((RET_HEADS * RET_QK, D_MODEL), lambda i: (0, 0))],
        out_specs=[pl.BlockSpec((tm, D_IN), lambda i: (i, 0)),
                   pl.BlockSpec((RET_HEADS * RET_QK, tm), lambda i: (0, i))],
        out_shape=[jax.ShapeDtypeStruct((geom.n_tok, D_IN), BF16),
                   jax.ShapeDtypeStruct((RET_HEADS * RET_QK, geom.n_tok), BF16)],
        compiler_params=_cparams(("arbitrary",)),
        name="inproj",
    )(x_ctx, x_lat, mod6, mod6, norm_w.reshape(1, D_MODEL), w_in_bf, w_rkt_bf)


def _gelu_tanh(x):
    return 0.5 * x * (1.0 + jnp.tanh(math.sqrt(2.0 / math.pi) * (x + 0.044715 * (x * x * x))))


def _lru_kernel(geom, reverse, *refs):
    if reverse:
        (xa_ref, xp_ref, xn_ref, cw_ref, cb_ref, wg_ref, bg_ref, sp_ref, h0_ref, perm_ref, permt_ref,
         ga_ref, hf_ref, out_ref, hl_ref, c_scr) = refs
    else:
        (xa_ref, xp_ref, xn_ref, cw_ref, cb_ref, wg_ref, bg_ref, sp_ref, h0_ref, perm_ref,
         out_ref, hl_ref, c_scr) = refs
    g = pl.program_id(0)
    i = geom.n_blocks - 1 - g if reverse else g
    start = geom.seq_start(i)
    end = geom.seq_end(i)

    @pl.when(end if reverse else start)
    def _():
        c_scr[...] = h0_ref[...]

    sub_len = BLK // LRU_SUB
    perm = perm_ref[...]
    x = jnp.dot(perm, xa_ref[...], preferred_element_type=F32)
    pm = jnp.where(start, 0.0, 1.0)
    nm = jnp.where(end, 0.0, 1.0)
    hp = xp_ref.shape[0]
    p1 = xp_ref[hp - 1:hp, :].astype(F32) * pm
    p2 = xp_ref[hp - 2:hp - 1, :].astype(F32) * pm
    n0 = xn_ref[0:1, :].astype(F32) * nm
    row = lax.broadcasted_iota(jnp.int32, x.shape, 0)
    xm1 = jnp.where(row < LRU_SUB, pltpu.roll(x, LRU_SUB + 1, 0), pltpu.roll(x, LRU_SUB, 0))
    xm1 = jnp.where(row == 0, p1, xm1)
    xm2 = jnp.where(row < 2 * LRU_SUB, pltpu.roll(x, 2 * LRU_SUB + 1, 0), pltpu.roll(x, 2 * LRU_SUB, 0))
    xm2 = jnp.where(row == 0, p2, jnp.where(row == LRU_SUB, p1, xm2))
    xp1 = jnp.where(row >= BLK - LRU_SUB, pltpu.roll(x, BLK - LRU_SUB - 1, 0),
                    pltpu.roll(x, BLK - LRU_SUB, 0))
    xp1 = jnp.where(row == BLK - 1, n0, xp1)
    xc = (cw_ref[0:1, :] * xm2 + cw_ref[1:2, :] * xm1 + cw_ref[2:3, :] * x
          + cw_ref[3:4, :] * xp1 + cb_ref[...])

    gt = jnp.dot(xc.astype(BF16), wg_ref[...], preferred_element_type=F32) + bg_ref[...]
    r = jax.nn.sigmoid(gt[:, :D_RNN])
    ig = jax.nn.sigmoid(gt[:, D_RNN:])
    a = jnp.exp(-LRU_C * r * sp_ref[...])
    u = jnp.sqrt(1.0 - a * a) * ig * xc

    h = jnp.zeros((LRU_SUB, D_RNN), F32)
    p = jnp.ones((LRU_SUB, D_RNN), F32)
    h_loc = [None] * sub_len
    p_loc = [None] * sub_len
    for t in (range(sub_len - 1, -1, -1) if reverse else range(sub_len)):
        a_t = a[t * LRU_SUB:(t + 1) * LRU_SUB, :]
        h = a_t * h + u[t * LRU_SUB:(t + 1) * LRU_SUB, :]
        p = a_t * p
        h_loc[t] = h
        p_loc[t] = p
    h_in = [None] * LRU_SUB
    state = c_scr[...]
    for k in (range(LRU_SUB - 1, -1, -1) if reverse else range(LRU_SUB)):
        h_in[k] = state
        state = h[k:k + 1, :] + p[k:k + 1, :] * state
    c_scr[...] = state
    hl_ref[...] = state
    h_in = jnp.concatenate(h_in, axis=0)
    h_full = jnp.concatenate([h_loc[t] + p_loc[t] * h_in for t in range(sub_len)], axis=0)
    if reverse:
        gv = jnp.dot(perm, ga_ref[...], preferred_element_type=F32)
        y = (_gelu_tanh(gv) * (hf_ref[...] + h_full)).astype(BF16)
        out_ref[...] = jnp.dot(permt_ref[...], y, preferred_element_type=F32).astype(BF16)
    else:
        out_ref[...] = h_full


def _lru_ret_bwd_kernel(geom, n_lru_in, n_ret_in, *refs):
    a, b = n_lru_in, n_lru_in + n_ret_in
    lru_in, ret_in = refs[:a], refs[a:b]
    lru_out, ret_out = refs[b:b + 2], refs[b + 2:b + 4]
    lru_scr, ret_scr = refs[b + 4:b + 5], refs[b + 5:b + 6]
    _lru_kernel(geom, True, *lru_in, *lru_out, *lru_scr)
    _ret_bwd_kernel(geom, *ret_in, *ret_out, *ret_scr)


def _lru_call(geom, reverse, proj, conv_w, conv_b, wg, bg, sp, h0, hf=None, ret=None):
    nb = geom.n_blocks
    halo = 16
    hpb = BLK // halo

    def blk(g):
        return nb - 1 - g if reverse else g

    d = 1 if reverse else 0
    in_specs = [
        pl.BlockSpec((BLK, D_RNN), lambda g: (blk(g), C_XA // D_RNN)),
        pl.BlockSpec((halo, D_RNN), lambda g: (jnp.maximum(blk(g) * hpb - 1, 0), C_XA // D_RNN)),
        pl.BlockSpec((halo, D_RNN), lambda g: (jnp.minimum((blk(g) + 1) * hpb, nb * hpb - 1), C_XA // D_RNN)),
        pl.BlockSpec((CONV_W, D_RNN), lambda g: (0, 0)),
        pl.BlockSpec((1, D_RNN), lambda g: (0, 0)),
        pl.BlockSpec((D_RNN, 2 * D_RNN), lambda g: (0, 0)),
        pl.BlockSpec((1, 2 * D_RNN), lambda g: (0, 0)),
        pl.BlockSpec((1, D_RNN), lambda g: (0, 0)),
        pl.BlockSpec((None, None, 1, D_RNN), lambda g: (geom.seq_id(blk(g)), d, 0, 0)),
    ]
    pos = np.arange(BLK)
    perm_np = np.zeros((BLK, BLK), np.float32)
    perm_np[pos, (pos % LRU_SUB) * (BLK // LRU_SUB) + pos // LRU_SUB] = 1.0
    in_specs.append(pl.BlockSpec((BLK, BLK), lambda g: (0, 0)))
    args = [proj, proj, proj, conv_w, conv_b, wg, bg, sp, h0, jnp.asarray(perm_np, BF16)]
    if reverse:
        in_specs += [pl.BlockSpec((BLK, BLK), lambda g: (0, 0)),
                     pl.BlockSpec((BLK, D_RNN), lambda g: (blk(g), C_GA // D_RNN)),
                     pl.BlockSpec((BLK, D_RNN), lambda g: (blk(g), 0))]
        args += [jnp.asarray(perm_np.T, BF16), proj, hf]
        out_dtype = BF16
    else:
        out_dtype = F32
    scratch = [pltpu.VMEM((1, D_RNN), F32)]
    out_specs = [pl.BlockSpec((BLK, D_RNN), lambda g: (blk(g), 0)),
                 pl.BlockSpec((None, 1, D_RNN), lambda g: (blk(g), 0, 0))]
    out_shape = [jax.ShapeDtypeStruct((geom.n_tok, D_RNN), out_dtype),
                 jax.ShapeDtypeStruct((nb, 1, D_RNN), F32)]
    body = functools.partial(_lru_kernel, geom, reverse)
    if ret is not None:
        assert reverse
        l, rkt, kd_b, cd_b, s0 = ret
        hs = RET_HEADS * RET_QK
        ret_specs = [pl.BlockSpec((hs, BLK), lambda g: (0, blk(g))),
                     pl.BlockSpec((BLK, RET_HEADS * RET_V), lambda g: (blk(g), C_RV // (RET_HEADS * RET_V))),
                     pl.BlockSpec((hs, BLK), lambda g: (0, 0)),
                     pl.BlockSpec((hs, RET_V), lambda g: (0, 0)),
                     pl.BlockSpec((None, None, None, hs, RET_V),
                                  lambda g: (jnp.maximum(geom.seq_id(blk(g)) - geom.batch, 0), l, 1, 0, 0))]
        body = functools.partial(_lru_ret_bwd_kernel, geom, len(in_specs), len(ret_specs))
        in_specs += ret_specs
        args += [rkt, proj, kd_b, cd_b, s0]
        out_specs += [pl.BlockSpec((None, hs, RET_V), lambda g: (blk(g), 0, 0)),
                      pl.BlockSpec((None, hs, RET_V), lambda g: (blk(g), 0, 0))]
        out_shape += [jax.ShapeDtypeStruct((nb, hs, RET_V), F32), jax.ShapeDtypeStruct((nb, hs, RET_V), F32)]
        scratch.append(pltpu.VMEM((hs, RET_V), F32))
    return pl.pallas_call(
        body,
        grid=(nb,),
        in_specs=in_specs,
        out_specs=out_specs,
        out_shape=out_shape,
        scratch_shapes=scratch,
        compiler_params=_cparams(("arbitrary",)),
        name=("lru_ret_bwd" if ret is not None else "lru_bwd") if reverse else "lru_fwd",
    )(*args)


def _group_rms(x, w, ones):
    xx = x * x
    hi = xx.astype(BF16)
    lo = (xx - hi.astype(F32)).astype(BF16)
    ss = (jnp.dot(hi, ones, preferred_element_type=F32)
          + jnp.dot(lo, ones, preferred_element_type=F32))
    return x * lax.rsqrt(ss * (1.0 / DA_QK) + EPS) * w


def _rope(x, cos, sin):
    lane = lax.broadcasted_iota(jnp.int32, x.shape, 1)
    first = (lane % (2 * ROPE_PAIRS)) < ROPE_PAIRS
    w = x.shape[1]
    partner = jnp.where(first, pltpu.roll(x, w - ROPE_PAIRS, 1), pltpu.roll(x, ROPE_PAIRS, 1))
    return x * cos + partner * sin


def _prep_kernel(rope, *refs):
    if rope:
        dq_ref, dk_ref, qw_ref, kw_ref, ones_ref, cos_ref, sin_ref, q_out, k_out = refs
    else:
        dq_ref, dk_ref, qw_ref, kw_ref, ones_ref, dv_ref = refs[:6]
        q_out, k_out, kf_out, vf_out = refs[-4:]
        vf_out[...] = dv_ref[...].astype(F32).reshape(vf_out.shape)
    ones = ones_ref[...]
    q = _group_rms(dq_ref[...].astype(F32), qw_ref[...], ones)
    k = _group_rms(dk_ref[...].astype(F32), kw_ref[...], ones)
    if rope:
        cos = jnp.concatenate([cos_ref[...]] * 4, axis=1)
        sin = jnp.concatenate([sin_ref[...]] * 4, axis=1)
        q = _rope(q, cos, sin)
        k = _rope(k, cos, sin)
    else:
        kf_out[...] = k.reshape(kf_out.shape)
    q_out[...] = (q * (DA_QK ** -0.5 * math.log2(math.e))).astype(BF16)
    k_out[...] = k.astype(BF16)


def _prep_call(geom, latent, proj, qw, kw, ones, cos=None, sin=None, layer=0, prev_cache=None):
    tm = 512
    w = DA_HEADS * 2 * DA_QK
    if latent:
        n, off = geom.n_lat, geom.n_ctx // tm
        per = geom.dec_seq // tm
    else:
        n, off = geom.n_ctx, 0
    in_specs = [pl.BlockSpec((tm, w), lambda i: (i + off, C_DQ // w)),
                pl.BlockSpec((tm, w), lambda i: (i + off, C_DK // w)),
                pl.BlockSpec((1, w), lambda i: (0, 0)),
                pl.BlockSpec((1, w), lambda i: (0, 0)),
                pl.BlockSpec((w, w), lambda i: (0, 0))]
    args = [proj, proj, qw, kw, ones]
    aliases = {}
    out_specs = [pl.BlockSpec((tm, w), lambda i: (i, 0)), pl.BlockSpec((tm, w), lambda i: (i, 0))]
    out_shape = [jax.ShapeDtypeStruct((n, w), BF16), jax.ShapeDtypeStruct((n, w), BF16)]
    if latent:
        in_specs += [pl.BlockSpec((tm, 2 * DA_QK), lambda i: (i % per, 0)),
                     pl.BlockSpec((tm, 2 * DA_QK), lambda i: (i % per, 0))]
        args += [cos, sin]
    else:
        in_specs.append(pl.BlockSpec((tm, w), lambda i: (i, C_DV // w)))
        args.append(proj)
        spt = tm // geom.seq
        cache_spec = pl.BlockSpec((spt, None, geom.seq, w), lambda i: (i, layer, 0, 0))
        cache_shape = jax.ShapeDtypeStruct((geom.batch, DEPTH, geom.seq, w), F32)
        out_specs += [cache_spec, cache_spec]
        out_shape += [cache_shape, cache_shape]
        if prev_cache is not None:
            aliases = {len(args): 2, len(args) + 1: 3}
            in_specs += [pl.BlockSpec(memory_space=pl.ANY), pl.BlockSpec(memory_space=pl.ANY)]
            args += list(prev_cache)
    return pl.pallas_call(
        functools.partial(_prep_kernel, latent),
        grid=(n // tm,),
        in_specs=in_specs, out_specs=out_specs, out_shape=out_shape,
        input_output_aliases=aliases,
        compiler_params=_cparams(("arbitrary",)),
        name="qk_prep_lat" if latent else "qk_prep_ctx",
    )(*args)


ATT_KC = 256
ATT_TQ = 256
LOG2E = math.log2(math.e)
ATT_SAFE_LOGIT = 60.0


def _attn_kernel(out_scale, has_cache, *refs):
    if has_cache:
        par_ref, q_ref, kc_ref, vc_ref, kl_ref, vl_ref, sw_ref, o_ref, e_scr, o_scr = refs
        srcs = [(kc_ref, vc_ref), (kl_ref, vl_ref)]
    else:
        par_ref, q_ref, kl_ref, vl_ref, sw_ref, o_ref, e_scr, o_scr = refs
        srcs = [(kl_ref, vl_ref)]
    chunks = [(kr, vr, st) for kr, vr in srcs for st in range(0, kr.shape[0], ATT_KC)]
    lam = par_ref[0]
    no_shift = par_ref[1] > 0.5
    tqs = ATT_TQ
    nsub = q_ref.shape[0] // tqs
    nt = (((1,), (1,)), ((), ()))
    half = ATT_KC // 2

    def stacked_q(sb):
        q = q_ref[sb * tqs:(sb + 1) * tqs, :]
        lane = lax.broadcasted_iota(jnp.int32, q.shape, 1)
        zero = jnp.zeros_like(q)
        return jnp.concatenate([jnp.where(lane < DA_QK, q, zero), jnp.where(lane >= DA_QK, q, zero)], axis=0)

    def logits(qq, c):
        kr, vr, st = chunks[c]
        return lax.dot_general(qq, kr[st:st + ATT_KC, :], nt, preferred_element_type=F32)

    def fold(total, e):
        part = e[:, :half] + e[:, half:]
        return part if total is None else total + part

    def row_stats(lsum):
        l = jnp.sum(lsum, axis=-1, keepdims=True)
        l1 = l[0:tqs]
        return l1, lam * l1 / l[tqs:2 * tqs]

    def pv(acc, buf, c, rho):
        kr, vr, st = chunks[c]
        w = (e_scr[buf, c, 0:tqs, :] - rho * e_scr[buf, c, tqs:2 * tqs, :]).astype(BF16)
        t = jnp.dot(w, vr[st:st + ATT_KC, :], preferred_element_type=F32)
        return t if acc is None else acc + t

    nck = len(chunks)

    @pl.when(no_shift)
    def _():
        stats = None
        for sb in range(nsub + 1):
            qq = stacked_q(sb) if sb < nsub else None
            lsum, acc = None, None
            for c in range(nck):
                if sb < nsub:
                    e = jnp.exp2(logits(qq, c))
                    e_scr[sb % 2, c] = e
                    lsum = fold(lsum, e)
                if sb > 0:
                    acc = pv(acc, (sb - 1) % 2, c, stats[1])
            if sb > 0:
                o_scr[(sb - 1) * tqs:sb * tqs, :] = acc / stats[0]
            if sb < nsub:
                stats = row_stats(lsum)

    @pl.when(jnp.logical_not(no_shift))
    def _():
        for sb in range(nsub):
            qq = stacked_q(sb)
            m = None
            for c in range(nck):
                s = logits(qq, c)
                e_scr[0, c] = s
                mc = jnp.max(s, axis=-1, keepdims=True)
                m = mc if m is None else jnp.maximum(m, mc)
            lsum = None
            for c in range(nck):
                e = jnp.exp2(e_scr[0, c] - m)
                e_scr[0, c] = e
                lsum = fold(lsum, e)
            l1, rho = row_stats(lsum)
            acc = None
            for c in range(nck):
                acc = pv(acc, 0, c, rho)
            o_scr[sb * tqs:(sb + 1) * tqs, :] = acc / l1

    o = o_scr[...]
    y = o * lax.rsqrt(jnp.mean(o * o, axis=-1, keepdims=True) + EPS) * sw_ref[...]
    o_ref[...] = (y * out_scale).astype(BF16)


def _attn_call(par, lam_init, q2d, k2d, proj, v_row_off, n_b, t_q, t_kl, tq, subln_w, cache=None):
    hw = 2 * DA_QK
    nq = t_q // tq
    vcol = C_DV // DA_V
    in_specs = [pl.BlockSpec(memory_space=pltpu.SMEM),
                pl.BlockSpec((tq, hw), lambda b, h, qi: (b * nq + qi, h))]
    args = [par, q2d]
    n_chunks = t_kl // ATT_KC
    if cache is not None:
        kc, vc = cache
        p = kc.shape[1]
        n_chunks += p // ATT_KC
        in_specs += [pl.BlockSpec((None, p, hw), lambda b, h, qi: (b, 0, h)),
                     pl.BlockSpec((None, p, DA_V), lambda b, h, qi: (b, 0, h))]
        args += [kc, vc]
    in_specs += [pl.BlockSpec((t_kl, hw), lambda b, h, qi: (b, h)),
                 pl.BlockSpec((t_kl, DA_V), lambda b, h, qi: (v_row_off + b, vcol + h)),
                 pl.BlockSpec((1, DA_V), lambda b, h, qi: (0, 0))]
    args += [k2d, proj, subln_w.reshape(1, DA_V)]
    return pl.pallas_call(
        functools.partial(_attn_kernel, 1.0 - lam_init, cache is not None),
        grid=(n_b, DA_HEADS, nq),
        in_specs=in_specs,
        out_specs=pl.BlockSpec((tq, DA_V), lambda b, h, qi: (b * nq + qi, h)),
        out_shape=jax.ShapeDtypeStruct((n_b * t_q, DA_HEADS * DA_V), BF16),
        scratch_shapes=[pltpu.VMEM((2 if tq > ATT_TQ else 1, n_chunks, 2 * ATT_TQ, ATT_KC), F32),
                        pltpu.VMEM((tq, DA_V), F32)],
        compiler_params=_cparams(("arbitrary", "arbitrary", "arbitrary")),
        name="diff_attn_lat" if cache is not None else "diff_attn_ctx",
    )(*args)


def _ret_state_update(kt, v, kd, cd, s_old):
    parts = []
    for h in range(RET_HEADS):
        rows = slice(h * RET_QK, (h + 1) * RET_QK)
        kh = (kt[rows, :].astype(F32) * kd[rows, :]).astype(BF16)
        parts.append(jnp.dot(kh, v[:, h * RET_V:(h + 1) * RET_V], preferred_element_type=F32))
    return cd * s_old + jnp.concatenate(parts, axis=0)


def _ret_bwd_kernel(geom, kt_ref, v_ref, kd_ref, cd_ref, s0_ref, sstart_ref, send_ref, s_scr):
    i = geom.n_blocks - 1 - pl.program_id(0)

    @pl.when(geom.seq_end(i))
    def _():
        s_scr[...] = jnp.where(i >= geom.ctx_blocks, s0_ref[...].astype(F32), 0.0)

    s_old = s_scr[...]
    sstart_ref[...] = s_old
    kt = kt_ref[...] * jnp.asarray(RET_QK ** -0.5, BF16)
    s_new = _ret_state_update(kt, v_ref[...], kd_ref[...], cd_ref[...], s_old)
    s_scr[...] = s_new
    send_ref[...] = s_new


def _ret_main_kernel(geom, q_ref, kt_ref, v_ref, g_ref, dsum_ref, qdf_ref, qdb_ref, kd_ref, cd_ref,
                     s0_ref, sb_ref, o_ref, send_ref, s_scr):
    i = pl.program_id(0)

    @pl.when(geom.seq_start(i))
    def _():
        s_scr[...] = jnp.where(i >= geom.ctx_blocks, s0_ref[...].astype(F32), 0.0)

    s_f = s_scr[...]
    s_fb = s_f.astype(BF16)
    s_bb = sb_ref[...].astype(BF16)
    q = q_ref[...].astype(F32)
    kt = kt_ref[...] * jnp.asarray(RET_QK ** -0.5, BF16)
    v = v_ref[...]
    lane = lax.broadcasted_iota(jnp.int32, q.shape, 1)
    for h in range(RET_HEADS):
        in_head = (lane >= h * RET_QK) & (lane < (h + 1) * RET_QK)
        qh = jnp.where(in_head, q, 0.0)
        vh = v[:, h * RET_V:(h + 1) * RET_V]
        sc = jnp.dot(qh.astype(BF16), kt, preferred_element_type=F32) * dsum_ref[h]
        o = jnp.dot(sc.astype(BF16), vh, preferred_element_type=F32)
        o += jnp.dot((qh * qdf_ref[...]).astype(BF16), s_fb, preferred_element_type=F32)
        o += jnp.dot((qh * qdb_ref[...]).astype(BF16), s_bb, preferred_element_type=F32)
        y = o * lax.rsqrt(jnp.mean(o * o, axis=-1, keepdims=True) + EPS)
        gv = g_ref[:, h * RET_V:(h + 1) * RET_V].astype(F32)
        o_ref[:, h * RET_V:(h + 1) * RET_V] = (y * (gv * jax.nn.sigmoid(gv))).astype(BF16)
    s_new = _ret_state_update(kt, v, kd_ref[...], cd_ref[...], s_f)
    s_scr[...] = s_new
    send_ref[...] = s_new


def _ret_main_call(geom, l, proj, rkt, dsum, qdf, qdb, kd_f, cd_f, s0, sb_start):
    nb = geom.n_blocks
    hs = RET_HEADS * RET_QK
    hv = RET_HEADS * RET_V
    return pl.pallas_call(
        functools.partial(_ret_main_kernel, geom),
        grid=(nb,),
        in_specs=[pl.BlockSpec((BLK, hs), lambda g: (g, C_RQ // hs)),
                  pl.BlockSpec((hs, BLK), lambda g: (0, g)),
                  pl.BlockSpec((BLK, hv), lambda g: (g, C_RV // hv)),
                  pl.BlockSpec((BLK, hv), lambda g: (g, C_RG // hv)),
                  pl.BlockSpec((RET_HEADS, BLK, BLK), lambda g: (0, 0, 0)),
                  pl.BlockSpec((BLK, hs), lambda g: (0, 0)),
                  pl.BlockSpec((BLK, hs), lambda g: (0, 0)),
                  pl.BlockSpec((hs, BLK), lambda g: (0, 0)),
                  pl.BlockSpec((hs, RET_V), lambda g: (0, 0)),
                  pl.BlockSpec((None, None, None, hs, RET_V),
                               lambda g: (jnp.maximum(geom.seq_id(g) - geom.batch, 0), l, 0, 0, 0)),
                  pl.BlockSpec((None, hs, RET_V), lambda g: (g, 0, 0))],
        out_specs=[pl.BlockSpec((BLK, hv), lambda g: (g, 0)),
                   pl.BlockSpec((None, hs, RET_V), lambda g: (g, 0, 0))],
        out_shape=[jax.ShapeDtypeStruct((geom.n_tok, hv), BF16),
                   jax.ShapeDtypeStruct((nb, hs, RET_V), F32)],
        scratch_shapes=[pltpu.VMEM((hs, RET_V), F32)],
        compiler_params=_cparams(("arbitrary",)),
        name="ret_main",
    )(proj, rkt, proj, proj, dsum, qdf, qdb, kd_f, cd_f, s0, sb_start)


def _ret_tables(ret_decay_l):
    log_g = jax.nn.log_sigmoid(ret_decay_l.astype(F32))
    pos = jnp.arange(BLK, dtype=F32)
    diff = pos[:, None] - pos[None, :]
    lf = log_g[0][:, None, None]
    lb = log_g[1][:, None, None]
    dsum = (jnp.where(diff >= 0, jnp.exp(jnp.maximum(diff, 0.0)[None] * lf), 0.0)
            + jnp.where(diff <= 0, jnp.exp(jnp.maximum(-diff, 0.0)[None] * lb), 0.0))

    def per_lane(e, lg):
        return jnp.repeat(jnp.exp(e[:, None] * lg[None, :]), RET_QK, axis=1)

    qdf = per_lane(pos + 1.0, log_g[0])
    qdb = per_lane(BLK - pos, log_g[1])
    kd_f = per_lane(BLK - 1.0 - pos, log_g[0]).T
    kd_b = per_lane(pos, log_g[1]).T
    cd_f = jnp.broadcast_to(jnp.repeat(jnp.exp(BLK * log_g[0]), RET_QK)[:, None], (RET_HEADS * RET_QK, RET_V))
    cd_b = jnp.broadcast_to(jnp.repeat(jnp.exp(BLK * log_g[1]), RET_QK)[:, None], (RET_HEADS * RET_QK, RET_V))
    return dsum, qdf, qdb, kd_f, kd_b, cd_f, cd_b


def _merge_kernel(n_ctx_tiles, ba_ref, bbc_ref, bbl_ref, bc_ref, g0_ref, g1_ref, g2_ref, xc_ref, xl_ref,
                  gate_ref, sc_ref, sh_ref, nw_ref, wb_ref, wo_ref, rhi_ref, rlo_ref, x1_ref, h2_ref, h2p_ref,
                  lt_ref):
    branches = (ba_ref[...], _pick_part(n_ctx_tiles, bbc_ref, bbl_ref), bc_ref[...])
    acc = None
    for br, (b, g_ref) in enumerate(zip(branches, (g0_ref, g1_ref, g2_ref))):
        p = jnp.dot(b, wb_ref[br], preferred_element_type=F32)
        t = (0.5 * jnp.tanh(0.5 * g_ref[...].astype(F32)) + 0.5) * p
        acc = t if acc is None else acc + t
    m = jnp.dot(acc.astype(BF16), wo_ref[...], preferred_element_type=F32)
    x1 = _pick_part(n_ctx_tiles, xc_ref, xl_ref) + gate_ref[...] * m
    x1_ref[...] = x1
    ms = jnp.mean(x1 * x1, axis=-1, keepdims=True)
    h2 = x1 * lax.rsqrt(ms + EPS) * nw_ref[...] * (1.0 + sc_ref[...]) + sh_ref[...]
    h2b = h2.astype(BF16)
    h2_ref[...] = h2b
    h2p_ref[...] = _pack_halves(h2b.astype(F32))
    h2lo = (h2 - h2b.astype(F32)).astype(BF16)
    nt = (((1,), (1,)), ((), ()))
    lt_ref[...] = (lax.dot_general(rhi_ref[...], h2b, nt, preferred_element_type=F32)
                   + lax.dot_general(rhi_ref[...], h2lo, nt, preferred_element_type=F32)
                   + lax.dot_general(rlo_ref[...], h2b, nt, preferred_element_type=F32))


def _merge_call(geom, l, ba, bb_ctx, bb_lat, bc, proj, x_ctx, x_lat, mod6, norm2_w, wb_bf, wo_bf, r_hi, r_lo):
    tm = 512
    gcol = C_GL // D_MODEL
    full = lambda shape: pl.BlockSpec(shape, lambda i: tuple(0 for _ in shape))
    tok = lambda w: pl.BlockSpec((tm, w), lambda i: (i, 0))
    return pl.pallas_call(
        functools.partial(_merge_kernel, geom.n_ctx // tm),
        grid=(geom.n_tok // tm,),
        in_specs=[tok(BRANCH_W)] + _split_in_specs(geom, tm, BRANCH_W, 1) + [tok(BRANCH_W),
                  pl.BlockSpec((tm, D_MODEL), lambda i: (i, gcol)),
                  pl.BlockSpec((tm, D_MODEL), lambda i: (i, gcol + 1)),
                  pl.BlockSpec((tm, D_MODEL), lambda i: (i, gcol + 2))]
                 + _split_in_specs(geom, tm, D_MODEL, 1) + [
                  _mod_spec(geom, l, 2, tm, 1), _mod_spec(geom, l, 4, tm, 1), _mod_spec(geom, l, 3, tm, 1),
                  full((1, D_MODEL)),
                  full((N_BRANCH, BRANCH_W, D_MODEL)), full((D_MODEL, D_MODEL)),
                  full((N_EXPERTS, D_MODEL)), full((N_EXPERTS, D_MODEL))],
        out_specs=[tok(D_MODEL), tok(D_MODEL), tok(D_MODEL // 2), pl.BlockSpec((N_EXPERTS, tm), lambda i: (0, i))],
        out_shape=[jax.ShapeDtypeStruct((geom.n_tok, D_MODEL), F32),
                   jax.ShapeDtypeStruct((geom.n_tok, D_MODEL), BF16),
                   jax.ShapeDtypeStruct((geom.n_tok, D_MODEL // 2), jnp.uint32),
                   jax.ShapeDtypeStruct((N_EXPERTS, geom.n_tok), F32)],
        compiler_params=_cparams(("arbitrary",)),
        name="merge_out",
    )(ba, bb_ctx, bb_lat, bc, proj, proj, proj, x_ctx, x_lat, mod6, mod6, mod6,
      norm2_w.reshape(1, D_MODEL), wb_bf, wo_bf, r_hi, r_lo)


def _router_kernel(lt_ref, bias_ref, ltri_ref, utri_ref, g_ref, slot_ref, cnt_ref, cnt_scr):
    per = N_EXPERTS // N_GROUPS
    tm = lt_ref.shape[1]
    scores = jax.nn.sigmoid(lt_ref[...])
    biased = scores + bias_ref[...]
    b3 = biased.reshape(N_GROUPS, per, tm)
    neg = jnp.float32(-jnp.inf)
    m1 = jnp.max(b3, axis=1, keepdims=True)
    is_m1 = b3 == m1
    cnt = jnp.sum(is_m1.astype(F32), axis=1, keepdims=True)
    m2 = jnp.max(jnp.where(is_m1, neg, b3), axis=1, keepdims=True)
    grp = (m1 + jnp.where(cnt >= 2.0, m1, m2)).reshape(N_GROUPS, tm)
    gidx = lax.broadcasted_iota(jnp.int32, (N_GROUPS, tm), 0)
    grank = jnp.zeros((N_GROUPS, tm), F32)
    for g2 in range(N_GROUPS):
        other = grp[g2:g2 + 1, :]
        ahead = (other > grp) | ((other == grp) & (gidx > g2))
        grank += ahead.astype(F32)
    gsel = (grank < float(TOPK_GROUPS)).astype(F32)
    emask = jnp.broadcast_to(gsel.reshape(N_GROUPS, 1, tm), (N_GROUPS, per, tm)).reshape(N_EXPERTS, tm)
    masked = jnp.where(emask > 0.0, biased, neg)
    eidx = lax.broadcasted_iota(jnp.int32, (N_EXPERTS, tm), 0)
    erank = jnp.zeros((N_EXPERTS, tm), F32)
    for e2 in range(N_EXPERTS):
        other = masked[e2:e2 + 1, :]
        ahead = (other > masked) | ((other == masked) & (eidx > e2))
        erank += ahead.astype(F32)
    sel = erank < float(TOP_K)
    w = jnp.where(sel, scores, 0.0)
    gates_t = w / jnp.sum(w, axis=0, keepdims=True) * ROUTED_SCALE

    @pl.when(pl.program_id(0) == 0)
    def _():
        cnt_scr[...] = jnp.zeros_like(cnt_scr)

    selb = sel.astype(BF16)
    slot = jnp.dot(ltri_ref[...], selb, preferred_element_type=F32)
    carry = cnt_scr[:, 0:1]
    rank = jnp.dot(selb, utri_ref[...], preferred_element_type=F32) + carry
    cnt_new = cnt_scr[...] + jnp.sum(sel.astype(F32), axis=1, keepdims=True)
    cnt_scr[...] = cnt_new
    cnt_ref[...] = cnt_new
    eid_f = eidx.astype(F32)
    g_rows, e_rows, r_rows = [], [], []
    for k in range(TOP_K):
        mk = jnp.where(sel & (slot == float(k)), 1.0, 0.0)
        g_rows.append(jnp.sum(mk * gates_t, axis=0, keepdims=True))
        e_rows.append(jnp.sum(mk * eid_f, axis=0, keepdims=True))
        r_rows.append(jnp.sum(mk * rank, axis=0, keepdims=True))
    slot_ref[...] = jnp.concatenate(e_rows + r_rows, axis=0).astype(jnp.int32)
    pad = jnp.zeros((GATE_W - TOP_K, tm), F32)
    g_ref[...] = jnp.concatenate(g_rows + [pad], axis=0).T


ROUTER_TM = 512


def _router_call(geom, logits_t, bias):
    tm = ROUTER_TM
    ltri = jnp.asarray(np.tril(np.ones((N_EXPERTS, N_EXPERTS), np.float32), -1), BF16)
    utri = jnp.asarray(np.triu(np.ones((tm, tm), np.float32), 1), BF16)
    return pl.pallas_call(
        _router_kernel,
        grid=(geom.n_tok // tm,),
        in_specs=[pl.BlockSpec((N_EXPERTS, tm), lambda i: (0, i)),
                  pl.BlockSpec((N_EXPERTS, 1), lambda i: (0, 0)),
                  pl.BlockSpec((N_EXPERTS, N_EXPERTS), lambda i: (0, 0)),
                  pl.BlockSpec((tm, tm), lambda i: (0, 0))],
        out_specs=[pl.BlockSpec((tm, GATE_W), lambda i: (i, 0)),
                   pl.BlockSpec((2 * TOP_K, tm), lambda i: (0, i)),
                   pl.BlockSpec((N_EXPERTS, GATE_W), lambda i: (0, 0))],
        out_shape=[jax.ShapeDtypeStruct((geom.n_tok, GATE_W), F32),
                   jax.ShapeDtypeStruct((2 * TOP_K, geom.n_tok), jnp.int32),
                   jax.ShapeDtypeStruct((N_EXPERTS, GATE_W), F32)],
        scratch_shapes=[pltpu.VMEM((N_EXPERTS, GATE_W), F32)],
        compiler_params=_cparams(("arbitrary",)),
        name="router",
    )(logits_t, bias.reshape(N_EXPERTS, 1), ltri, utri)


MOE_TR = 512
SC_CORES = 2
SC_SUBCORES = 16
SC_CHUNK = 64


def _sc_worker_base(rows_per_worker):
    wid = lax.axis_index("s") * SC_CORES + lax.axis_index("c")
    return wid * rows_per_worker


def _sc_scatter_rows(table, pos_flat, n_slots, n_rows_out):
    n, d = table.shape
    nw = SC_CORES * SC_SUBCORES
    assert n % (nw * SC_CHUNK) == 0
    per_w = n // nw
    mesh = plsc.VectorSubcoreMesh(core_axis_name="c", subcore_axis_name="s")

    @functools.partial(
        pl.kernel, mesh=mesh,
        out_type=jax.ShapeDtypeStruct((n_rows_out, d), table.dtype),
        scratch_types=[[pltpu.VMEM((SC_CHUNK,), jnp.int32) for _ in range(n_slots)],
                       pltpu.VMEM((SC_CHUNK, d), table.dtype),
                       pltpu.SemaphoreType.DMA],
    )
    def scatter(table_hbm, pos_hbm, out_hbm, idx_v, rows_v, sem):
        base = _sc_worker_base(per_w)

        @pl.loop(0, per_w // SC_CHUNK)
        def _(ci):
            off = pl.multiple_of(base + ci * SC_CHUNK, 8)
            for k in range(n_slots):
                pltpu.sync_copy(pos_hbm.at[pl.ds(pl.multiple_of(k * n + off, 8), SC_CHUNK)], idx_v[k])
            pltpu.sync_copy(table_hbm.at[pl.ds(off, SC_CHUNK)], rows_v)
            copies = [pltpu.make_async_copy(rows_v, out_hbm.at[idx_v[k]], sem) for k in range(n_slots)]
            for cp in copies:
                cp.start()
            for cp in copies:
                cp.wait()

    return scatter(table, pos_flat)


def _sc_gather_rows(table, idx):
    b = idx.shape[0]
    d = table.shape[1]
    nw = SC_CORES * SC_SUBCORES
    nbuf = 2
    assert b % (nw * SC_CHUNK * nbuf) == 0
    per_w = b // nw
    n_chunks = per_w // SC_CHUNK
    mesh = plsc.VectorSubcoreMesh(core_axis_name="c", subcore_axis_name="s")

    @functools.partial(
        pl.kernel, mesh=mesh,
        out_type=jax.ShapeDtypeStruct((b, d), table.dtype),
        scratch_types=[pltpu.VMEM((per_w,), jnp.int32),
                       [pltpu.VMEM((SC_CHUNK, d), table.dtype) for _ in range(nbuf)],
                       [pltpu.SemaphoreType.DMA for _ in range(nbuf)],
                       [pltpu.SemaphoreType.DMA for _ in range(nbuf)]],
    )
    def gather(table_hbm, idx_hbm, out_hbm, idx_v, rows, gsem, wsem):
        base = _sc_worker_base(per_w)
        pltpu.sync_copy(idx_hbm.at[pl.ds(pl.multiple_of(base, 8), per_w)], idx_v)

        def fetch(ci, slot):
            src = table_hbm.at[idx_v.at[pl.ds(pl.multiple_of(ci * SC_CHUNK, 8), SC_CHUNK)]]
            return pltpu.make_async_copy(src, rows[slot], gsem[slot])

        def put(ci, slot):
            dst = out_hbm.at[pl.ds(pl.multiple_of(base + ci * SC_CHUNK, 8), SC_CHUNK)]
            return pltpu.make_async_copy(rows[slot], dst, wsem[slot])

        for slot in range(nbuf):
            fetch(slot, slot).start()

        @pl.loop(0, n_chunks, step=nbuf)
        def _(c0):
            for slot in range(nbuf):
                ci = c0 + slot
                fetch(ci, slot).wait()
                put(ci, slot).start()
                put(ci, slot).wait()

                @pl.when(ci + nbuf < n_chunks)
                def _():
                    fetch(ci + nbuf, slot).start()

    return gather(table, idx)


def _route_positions(n_tok, slots, counts):
    cnt = counts[:, 0].astype(jnp.int32)
    cnt_pad = ((cnt + MOE_TR - 1) // MOE_TR) * MOE_TR
    off_end = jnp.cumsum(cnt_pad)
    off = off_end - cnt_pad
    eid, rank = slots[:TOP_K], slots[TOP_K:]
    eids = jnp.arange(N_EXPERTS, dtype=jnp.int32)
    pos = jnp.sum(jnp.where(eid[..., None] == eids, off, 0), axis=-1) + rank
    n_tiles = (TOP_K * n_tok) // MOE_TR + N_EXPERTS
    tile_start = jnp.arange(n_tiles, dtype=jnp.int32) * MOE_TR
    tile_expert = jnp.sum((tile_start[:, None] >= off_end[None, :]).astype(jnp.int32), axis=1)
    tile_expert = jnp.minimum(tile_expert, N_EXPERTS - 1)
    n_used = (off_end[-1] // MOE_TR).reshape(1)
    tile_idx = jnp.arange(n_tiles, dtype=jnp.int32)
    used = tile_idx < n_used[0]
    prev = jnp.concatenate([jnp.full((1,), -1, jnp.int32), tile_expert[:-1]])
    first = jnp.logical_and(used, tile_expert != prev)
    parity = (jnp.cumsum(first.astype(jnp.int32)) - 1) % 2
    later = jnp.logical_and(eids[None, :] > eids[:, None], (cnt_pad > 0)[None, :])
    next_e = jnp.min(jnp.where(later, eids[None, :], N_EXPERTS), axis=1)
    nxt = jnp.sum(jnp.where(tile_expert[:, None] == eids, next_e, 0), axis=1)
    has_next = jnp.logical_and(first, nxt < N_EXPERTS)
    sched = (tile_expert, n_used, first.astype(jnp.int32), jnp.minimum(nxt, N_EXPERTS - 1).astype(jnp.int32),
             jnp.maximum(parity, 0).astype(jnp.int32), has_next.astype(jnp.int32))
    return pos, sched, n_tiles


def _expert_ffn(x_lo, x_hi, gu, dn):
    half = D_MODEL // 2
    a = (jnp.dot(x_lo, gu[0:half, :], preferred_element_type=F32)
         + jnp.dot(x_hi, gu[half:, :], preferred_element_type=F32))
    hg = a[:, :D_EXPERT]
    act = (hg * jax.nn.sigmoid(hg)) * a[:, D_EXPERT:]
    return jnp.dot(act.astype(BF16), dn, preferred_element_type=F32)


def _experts_kernel(l, te_ref, nu_ref, first_ref, nxt_ref, par_ref, hasn_ref, x_ref, gu_hbm, dn_hbm, y_ref,
                    gu_f, dn_f, gu_b, dn_b, sem):
    i = pl.program_id(0)

    def fetch(e, slot):
        return (pltpu.make_async_copy(gu_hbm.at[l, e], gu_f.at[slot], sem.at[0, slot]),
                pltpu.make_async_copy(dn_hbm.at[l, e], dn_f.at[slot], sem.at[1, slot]))

    @pl.when(jnp.logical_and(i == 0, nu_ref[0] > 0))
    def _():
        for cp in fetch(te_ref[0], par_ref[0]):
            cp.start()

    @pl.when(first_ref[i] == 1)
    def _():
        slot = par_ref[i]
        for cp in fetch(te_ref[i], slot):
            cp.wait()

        @pl.when(hasn_ref[i] == 1)
        def _():
            for cp in fetch(nxt_ref[i], 1 - slot):
                cp.start()

        gu_b[...] = gu_f[slot].astype(BF16)
        dn_b[...] = dn_f[slot].astype(BF16)

    @pl.when(i < nu_ref[0])
    def _():
        lo, hi = _unpack_halves(x_ref[...])
        y = _expert_ffn(lo.astype(BF16), hi.astype(BF16), gu_b[...], dn_b[...])
        y_ref[...] = _pack_halves(y.astype(BF16).astype(F32))

    @pl.when(i >= nu_ref[0])
    def _():
        y_ref[...] = jnp.zeros_like(y_ref)


def _experts_call(l, xs, sched, n_tiles, w_gu, w_dn):
    half = D_MODEL // 2
    grid_spec = pltpu.PrefetchScalarGridSpec(
        num_scalar_prefetch=len(sched),
        grid=(n_tiles,),
        in_specs=[pl.BlockSpec((MOE_TR, half), lambda i, te, nu, *_: (jnp.minimum(i, jnp.maximum(nu[0], 1) - 1), 0)),
                  pl.BlockSpec(memory_space=pl.ANY),
                  pl.BlockSpec(memory_space=pl.ANY)],
        out_specs=pl.BlockSpec((MOE_TR, half), lambda i, *_: (i, 0)),
        scratch_shapes=[pltpu.VMEM((2, D_MODEL, 2 * D_EXPERT), w_gu.dtype),
                        pltpu.VMEM((2, D_EXPERT, D_MODEL), w_dn.dtype),
                        pltpu.VMEM((D_MODEL, 2 * D_EXPERT), BF16),
                        pltpu.VMEM((D_EXPERT, D_MODEL), BF16),
                        pltpu.SemaphoreType.DMA((2, 2))],
    )
    return pl.pallas_call(
        functools.partial(_experts_kernel, l),
        grid_spec=grid_spec,
        out_shape=jax.ShapeDtypeStruct((n_tiles * MOE_TR, half), jnp.uint32),
        compiler_params=_cparams(("arbitrary",)),
        name="moe_experts",
    )(*sched, xs, w_gu, w_dn)


MOE_OUT_PARTS = 2


def _moe_out_kernel(n_ctx_tiles, tile0, first, *refs):
    if first:
        yt_ref, g_ref, h_ref, sgu_ref, sdn_ref, x1_ref, gate_ref, oc_ref, ol_ref = refs
    else:
        yt_ref, g_ref, h_ref, sgu_ref, sdn_ref, x1_ref, gate_ref, _, ol_ref = refs
    i = pl.program_id(0) + tile0
    gts = g_ref[...]
    lane = lax.broadcasted_iota(jnp.int32, gts.shape, 1)
    acc_lo, acc_hi = None, None
    for k in range(TOP_K):
        ge = jnp.sum(jnp.where(lane == k, gts, 0.0), axis=1, keepdims=True)
        lo, hi = _unpack_halves(yt_ref[k])
        acc_lo = ge * lo if acc_lo is None else acc_lo + ge * lo
        acc_hi = ge * hi if acc_hi is None else acc_hi + ge * hi
    routed = jnp.concatenate([acc_lo, acc_hi], axis=1)
    h = h_ref[...]
    half = D_MODEL // 2
    shared = _expert_ffn(h[:, :half], h[:, half:], sgu_ref[...], sdn_ref[...])
    y = x1_ref[...] + gate_ref[...] * (routed + shared)
    if first:
        @pl.when(i < n_ctx_tiles)
        def _():
            oc_ref[...] = y

        @pl.when(i >= n_ctx_tiles)
        def _():
            ol_ref[...] = y
    else:
        ol_ref[...] = y


def _moe_out_call(geom, l, part, yt, gates, h2, sgu_bf, sdn_bf, x1, mod6, prev_lat=None):
    tm = 512
    nct = geom.n_ctx // tm
    n_part = geom.n_tok // tm // MOE_OUT_PARTS
    t0 = part * n_part
    first = part == 0
    assert nct <= n_part
    half = D_MODEL // 2
    tok = lambda w: pl.BlockSpec((tm, w), lambda i: (i + t0, 0))
    in_specs = [pl.BlockSpec((TOP_K, tm, half), lambda i: (0, i, 0)),
                tok(GATE_W), tok(D_MODEL),
                pl.BlockSpec((None, D_MODEL, 2 * D_EXPERT), lambda i: (l, 0, 0)),
                pl.BlockSpec((None, D_EXPERT, D_MODEL), lambda i: (l, 0, 0)),
                tok(D_MODEL),
                pl.BlockSpec((None, None, None, 1, D_MODEL), lambda i: (l, geom.mod_row(i + t0, tm), 5, 0, 0))]
    args = [yt, gates, h2, sgu_bf, sdn_bf, x1, mod6]
    lat_shape = jax.ShapeDtypeStruct((geom.n_lat, D_MODEL), F32)
    if first:
        out_specs = [pl.BlockSpec((tm, D_MODEL), lambda i: (jnp.minimum(i, nct - 1), 0)),
                     pl.BlockSpec((tm, D_MODEL), lambda i: (jnp.maximum(i - nct, 0), 0))]
        out_shape = [jax.ShapeDtypeStruct((geom.n_ctx, D_MODEL), F32), lat_shape]
        aliases = {}
    else:
        in_specs.append(pl.BlockSpec(memory_space=pl.ANY))
        args.append(prev_lat)
        out_specs = [pl.BlockSpec((tm, D_MODEL), lambda i: (i + t0 - nct, 0))]
        out_shape = [lat_shape]
        aliases = {len(args) - 1: 0}
    return pl.pallas_call(
        functools.partial(_moe_out_kernel, nct, t0, first),
        grid=(n_part,),
        in_specs=in_specs, out_specs=out_specs, out_shape=out_shape,
        input_output_aliases=aliases,
        compiler_params=_cparams(("arbitrary",)),
        name="moe_out",
    )(*args)


def _moe(geom, l, h2, h2p, gates, slots, counts, w_gu, w_dn, sgu_bf, sdn_bf, x1, mod6):
    pos, sched, n_tiles = _route_positions(geom.n_tok, slots, counts)
    xs = _sc_scatter_rows(h2p, pos.reshape(-1), TOP_K, n_tiles * MOE_TR)
    ys = _experts_call(l, xs, sched, n_tiles, w_gu, w_dn)
    n_part = geom.n_tok // MOE_OUT_PARTS
    y_ctx, y_lat = None, None
    for part in range(MOE_OUT_PARTS):
        pos_p = pos[:, part * n_part:(part + 1) * n_part].reshape(-1)
        yt = _sc_gather_rows(ys, pos_p).reshape(TOP_K, n_part, D_MODEL // 2)
        outs = _moe_out_call(geom, l, part, yt, gates, h2, sgu_bf, sdn_bf, x1, mod6, y_lat)
        if part == 0:
            y_ctx, y_lat = outs
        else:
            (y_lat,) = outs
    return y_ctx, y_lat


def _rope_tables(dec_seq):
    rows = dec_seq // GRID_W
    row = jnp.repeat(jnp.arange(rows, dtype=F32), GRID_W)
    col = jnp.tile(jnp.arange(GRID_W, dtype=F32), rows)
    inv = ROPE_BASE ** (-jnp.arange(ROPE_PAIRS, dtype=F32) / ROPE_PAIRS)
    ar = row[:, None] * inv[None, :]
    ac = col[:, None] * inv[None, :]
    cos64 = jnp.concatenate([jnp.cos(ar), jnp.cos(ar), jnp.cos(ac), jnp.cos(ac)], axis=1)
    sin64 = jnp.concatenate([-jnp.sin(ar), jnp.sin(ar), -jnp.sin(ac), jnp.sin(ac)], axis=1)
    return jnp.tile(cos64, (1, 2)), jnp.tile(sin64, (1, 2))


def _block_diag_gate(wg_dir):
    eye = jnp.eye(LRU_BLOCKS, dtype=F32)
    dense = jnp.einsum('gnij,nm->gnimj', wg_dir.astype(F32), eye).reshape(2, D_RNN, D_RNN)
    return jnp.concatenate([dense[0], dense[1]], axis=1)


def kernel(x_prompt, x_sample, cache_k, cache_v, state_lru, state_ret, c, c_ctx, ada_w, ada_b, norm1_w, norm2_w, w_in, conv_w, conv_b, lru_gate_w, lru_gate_b, lru_lambda, q_norm_w, k_norm_w, diff_lambda, subln_w, ret_decay, w_branch, w_out, router_w, router_bias, w_exp_gu, w_exp_down, w_sh_gu, w_sh_down):
    batch, seq, _ = x_prompt.shape
    dec_batch, dec_seq, _ = x_sample.shape
    assert 1 + dec_batch <= MOD_ROWS
    geom = _Geom(batch, seq, dec_batch, dec_seq)
    hs = RET_HEADS * RET_QK
    aw = DA_HEADS * 2 * DA_QK

    x_ctx = x_prompt.reshape(geom.n_ctx, D_MODEL)
    x_lat = x_sample.reshape(geom.n_lat, D_MODEL)
    cvec = jnp.zeros((MOD_ROWS, D_MODEL), F32).at[0].set(c_ctx).at[1:1 + dec_batch].set(c)
    mod6 = _ada_call(cvec, ada_w, ada_b).reshape(DEPTH, MOD_ROWS, 6, 1, D_MODEL)

    ones_bd = jnp.kron(jnp.eye(aw // DA_QK, dtype=F32), jnp.ones((DA_QK, DA_QK), F32)).astype(BF16)
    cos_t, sin_t = _rope_tables(dec_seq)

    w_in_bf = w_in.astype(BF16)
    sgu_bf, sdn_bf = w_sh_gu.astype(BF16), w_sh_down.astype(BF16)

    new_cache, lrus, rets = None, [], []
    for l in range(DEPTH):
        lam_init = 0.8 - 0.6 * math.exp(-0.3 * l)
        w_rkt_bf = w_in[l][:, C_RK:C_RK + hs].T.astype(BF16)
        proj, rkt = _inproj_call(geom, l, x_ctx, x_lat, mod6, norm1_w[l], w_in_bf, w_rkt_bf)

        sp = jax.nn.softplus(-lru_lambda[l].astype(F32))
        h0 = jnp.concatenate([jnp.zeros((batch, 2, D_RNN), F32), state_lru[:, l].astype(F32)], axis=0)
        h0 = h0.reshape(geom.n_seq, 2, 1, D_RNN)
        cb = conv_b[l].reshape(1, D_RNN)
        lru_args = []
        for d in range(2):
            lru_args.append((_block_diag_gate(lru_gate_w[l, d]).astype(BF16),
                             lru_gate_b[l, d].reshape(1, 2 * D_RNN), sp[d].reshape(1, D_RNN)))
        hf, hf_last = _lru_call(geom, False, proj, conv_w[l], cb, *lru_args[0], h0)
        dsum, qdf, qdb, kd_f, kd_b, cd_f, cd_b = _ret_tables(ret_decay[l])
        s0 = state_ret.reshape(dec_batch, DEPTH, 2, hs, RET_V)
        branch_a, hb_last, sb_start, sb_end = _lru_call(geom, True, proj, conv_w[l], cb, *lru_args[1], h0, hf,
                                                        ret=(l, rkt, kd_b, cd_b, s0))

        qw = jnp.tile(q_norm_w[l], aw // DA_QK).reshape(1, aw)
        kw = jnp.tile(k_norm_w[l], aw // DA_QK).reshape(1, aw)
        q_c, k_c, *new_cache = _prep_call(geom, False, proj, qw, kw, ones_bd, layer=l, prev_cache=new_cache)
        q_l, k_l = _prep_call(geom, True, proj, qw, kw, ones_bd, cos_t, sin_t)
        lam_p = diff_lambda[l].astype(F32)
        lam = jnp.exp(jnp.sum(lam_p[0] * lam_p[1])) - jnp.exp(jnp.sum(lam_p[2] * lam_p[3])) + lam_init
        q_bound = DA_QK * jnp.max(jnp.square(q_norm_w[l].astype(F32))) * (DA_QK ** -0.5 * LOG2E) ** 2
        k_bound = DA_QK * jnp.max(jnp.square(k_norm_w[l].astype(F32)))
        kc32 = cache_k[:, l].astype(F32)
        kc_bound = jnp.maximum(k_bound, jnp.max(jnp.sum(jnp.square(kc32), axis=-1)))

        def attn_par(kb):
            ok = (q_bound * kb * 1.05 < ATT_SAFE_LOGIT ** 2).astype(F32)
            return jnp.stack([lam, ok])

        assert geom.n_ctx % dec_seq == 0
        cache = (kc32.reshape(dec_batch, -1, aw).astype(BF16),
                 cache_v[:, l].reshape(dec_batch, -1, DA_HEADS * DA_V).astype(BF16))
        att_c = _attn_call(attn_par(k_bound), lam_init, q_c, k_c, proj, 0, batch, seq, seq, 256, subln_w[l])
        att_l = _attn_call(attn_par(kc_bound), lam_init, q_l, k_l, proj, geom.n_ctx // dec_seq, dec_batch,
                           dec_seq, dec_seq, min(4 * ATT_TQ, dec_seq), subln_w[l], cache)

        branch_c, sf_end = _ret_main_call(geom, l, proj, rkt, dsum, qdf, qdb, kd_f, cd_f, s0, sb_start)

        r_t = router_w[l].T.astype(F32)
        r_hi = r_t.astype(BF16)
        r_lo = (r_t - r_hi.astype(F32)).astype(BF16)
        x1, h2, h2p, logits_t = _merge_call(geom, l, branch_a, att_c, att_l, branch_c, proj, x_ctx, x_lat, mod6,
                                            norm2_w[l], w_branch[l].astype(BF16), w_out[l].astype(BF16), r_hi, r_lo)
        gates, slots, counts = _router_call(geom, logits_t, router_bias[l].astype(F32))
        x_ctx, x_lat = _moe(geom, l, h2, h2p, gates, slots, counts, w_exp_gu, w_exp_down, sgu_bf, sdn_bf, x1, mod6)

        lrus.append(jnp.stack([hf_last[:batch, 0], hb_last[:batch, 0]], axis=1))
        rets.append(jnp.stack([sf_end[:batch].reshape(batch, RET_HEADS, RET_QK, RET_V),
                               sb_end[:batch].reshape(batch, RET_HEADS, RET_QK, RET_V)], axis=1))

    y_prompt = x_ctx.reshape(batch, seq, D_MODEL)
    y_sample = x_lat.reshape(dec_batch, dec_seq, D_MODEL)
    new_k = new_cache[0].reshape(batch, DEPTH, seq, DA_HEADS, 2, DA_QK)
    new_v = new_cache[1].reshape(batch, DEPTH, seq, DA_HEADS, DA_V)
    return (y_prompt, y_sample, new_k, new_v, jnp.stack(lrus, axis=1), jnp.stack(rets, axis=1))
```

```python
import functools
import math

import numpy as np
import jax
import jax.numpy as jnp
from jax import lax
from jax.experimental import pallas as pl
from jax.experimental.pallas import tpu as pltpu
from jax.experimental.pallas import tpu_sc as plsc

F32 = jnp.float32
BF16 = jnp.bfloat16

D_MODEL = 1024
DEPTH = 2
GRID_W = 64
D_RNN = 512
LRU_BLOCKS = 8
LRU_BLOCK = D_RNN // LRU_BLOCKS
CONV_W = 4
LRU_C = 8.0
DA_HEADS = 4
DA_QK = 64
DA_V = 128
ROPE_PAIRS = DA_QK // 4
ROPE_BASE = 10000.0
RET_HEADS = 4
RET_QK = 64
RET_V = 128
BRANCH_W = 512
N_BRANCH = 3
D_IN = 7168
N_EXPERTS = 64
TOP_K = 8
N_GROUPS = 8
TOPK_GROUPS = 4
D_EXPERT = 256
ROUTED_SCALE = 2.5
EPS = 1e-6

C_XA, C_GA, C_DQ, C_DK, C_DV = 0, 512, 1024, 1536, 2048
C_RQ, C_RK, C_RV, C_RG, C_GL = 2560, 2816, 3072, 3584, 4096

BLK = 256
LRU_SUB = 8
GATE_W = 128
MOD_ROWS = 8
VMEM_LIMIT = 56 * 1024 * 1024


def _cparams(sem, vmem_limit=VMEM_LIMIT):
    return pltpu.CompilerParams(dimension_semantics=sem, vmem_limit_bytes=vmem_limit)


class _Geom:
    def __init__(self, batch, seq, dec_batch, dec_seq):
        assert seq == BLK and dec_seq % BLK == 0
        self.batch, self.seq, self.dec_batch, self.dec_seq = batch, seq, dec_batch, dec_seq
        self.n_ctx = batch * seq
        self.n_lat = dec_batch * dec_seq
        self.n_tok = self.n_ctx + self.n_lat
        self.ctx_blocks = self.n_ctx // BLK
        self.lat_blocks = dec_seq // BLK
        self.n_blocks = self.n_tok // BLK
        self.n_seq = batch + dec_batch

    def mod_row(self, i, tile):
        nct = self.n_ctx // tile
        per = self.dec_seq // tile
        return jnp.where(i < nct, 0, 1 + (i - nct) // per)

    def seq_id(self, i):
        return jnp.where(i < self.ctx_blocks, i, self.ctx_blocks + (i - self.ctx_blocks) // self.lat_blocks)

    def seq_start(self, i):
        return jnp.logical_or(i < self.ctx_blocks, (i - self.ctx_blocks) % self.lat_blocks == 0)

    def seq_end(self, i):
        return jnp.logical_or(i < self.ctx_blocks, (i - self.ctx_blocks) % self.lat_blocks == self.lat_blocks - 1)


def _ada_kernel(c_ref, w_ref, b_ref, o_ref):
    cv = c_ref[...]
    s = cv * jax.nn.sigmoid(cv)
    o_ref[...] = jnp.dot(s, w_ref[...], preferred_element_type=F32,
                         precision=lax.Precision.HIGHEST) + b_ref[...]


def _ada_call(cvec, ada_w, ada_b):
    depth = ada_w.shape[0]
    nt = 6
    return pl.pallas_call(
        _ada_kernel,
        grid=(depth, nt),
        in_specs=[pl.BlockSpec((MOD_ROWS, D_MODEL), lambda l, j: (0, 0)),
                  pl.BlockSpec((None, D_MODEL, D_MODEL), lambda l, j: (l, 0, j)),
                  pl.BlockSpec((None, 1, D_MODEL), lambda l, j: (l, 0, j))],
        out_specs=pl.BlockSpec((None, MOD_ROWS, D_MODEL), lambda l, j: (l, 0, j)),
        out_shape=jax.ShapeDtypeStruct((depth, MOD_ROWS, 6 * D_MODEL), F32),
        compiler_params=_cparams(("arbitrary", "arbitrary")),
        name="ada_mod",
    )(cvec, ada_w, ada_b.reshape(depth, 1, 6 * D_MODEL))


def _mod_spec(geom, l, which, tile, ngrid):
    if ngrid == 1:
        return pl.BlockSpec((None, None, None, 1, D_MODEL),
                            lambda i: (l, geom.mod_row(i, tile), which, 0, 0))
    return pl.BlockSpec((None, None, None, 1, D_MODEL),
                        lambda i, j: (l, geom.mod_row(i, tile), which, 0, 0))


def _split_in_specs(geom, tile, width, ngrid):
    nct = geom.n_ctx // tile
    if ngrid == 1:
        return [pl.BlockSpec((tile, width), lambda i: (jnp.minimum(i, nct - 1), 0)),
                pl.BlockSpec((tile, width), lambda i: (jnp.maximum(i - nct, 0), 0))]
    return [pl.BlockSpec((tile, width), lambda i, j: (jnp.minimum(i, nct - 1), 0)),
            pl.BlockSpec((tile, width), lambda i, j: (jnp.maximum(i - nct, 0), 0))]


def _pick_part(n_ctx_tiles, c_ref, l_ref):
    return jnp.where(pl.program_id(0) < n_ctx_tiles, c_ref[...], l_ref[...])


def _pack_halves(y):
    w = y.shape[1] // 2
    bits = pltpu.bitcast(y, jnp.uint32)
    return (bits[:, :w] >> 16) | (bits[:, w:] & jnp.uint32(0xFFFF0000))


def _unpack_halves(p):
    return pltpu.bitcast(p << 16, F32), pltpu.bitcast(p & jnp.uint32(0xFFFF0000), F32)


INPROJ_TM = 512
INPROJ_TN = 1024


def _inproj_kernel(n_ctx_tiles, xc_ref, xl_ref, sc_ref, sh_ref, nw_ref, w_ref, wkt_ref, o_ref, kt_ref):
    x = _pick_part(n_ctx_tiles, xc_ref, xl_ref)
    ms = jnp.mean(x * x, axis=-1, keepdims=True)
    y = x * lax.rsqrt(ms + EPS) * nw_ref[...]
    hb = (y * (1.0 + sc_ref[...]) + sh_ref[...]).astype(BF16)
    kt_ref[...] = lax.dot_general(wkt_ref[...], hb, (((1,), (1,)), ((), ())),
                                  preferred_element_type=F32).astype(BF16)
    for j in range(D_IN // INPROJ_TN):
        cols = slice(j * INPROJ_TN, (j + 1) * INPROJ_TN)
        o_ref[:, cols] = jnp.dot(hb, w_ref[:, cols], preferred_element_type=F32).astype(BF16)


def _inproj_call(geom, l, x_ctx, x_lat, mod6, norm_w, w_in_bf, w_rkt_bf):
    tm = INPROJ_TM
    return pl.pallas_call(
        functools.partial(_inproj_kernel, geom.n_ctx // tm),
        grid=(geom.n_tok // tm,),
        in_specs=_split_in_specs(geom, tm, D_MODEL, 1) + [
                  _mod_spec(geom, l, 1, tm, 1),
                  _mod_spec(geom, l, 0, tm, 1),
                  pl.BlockSpec((1, D_MODEL), lambda i: (0, 0)),
                  pl.BlockSpec((None, D_MODEL, D_IN), lambda i: (l, 0, 0), pipeline_mode=pl.Buffered(1)),
                  pl.BlockSpec((RET_HEADS * RET_QK, D_MODEL), lambda i: (0, 0))],
        out_specs=[pl.BlockSpec((tm, D_IN), lambda i: (i, 0)),
                   pl.BlockSpec((RET_HEADS * RET_QK, tm), lambda i: (0, i))],
        out_shape=[jax.ShapeDtypeStruct((geom.n_tok, D_IN), BF16),
                   jax.ShapeDtypeStruct((RET_HEADS * RET_QK, geom.n_tok), BF16)],
        compiler_params=_cparams(("arbitrary",)),
        name="inproj",
    )(x_ctx, x_lat, mod6, mod6, norm_w.reshape(1, D_MODEL), w_in_bf, w_rkt_bf)


def _gelu_tanh(x):
    return 0.5 * x * (1.0 + jnp.tanh(math.sqrt(2.0 / math.pi) * (x + 0.044715 * (x * x * x))))


def _lru_kernel(geom, reverse, *refs):
    if reverse:
        (xa_ref, xp_ref, xn_ref, cw_ref, cb_ref, wg_ref, bg_ref, sp_ref, h0_ref, perm_ref, permt_ref,
         ga_ref, hf_ref, out_ref, hl_ref, c_scr) = refs
    else:
        (xa_ref, xp_ref, xn_ref, cw_ref, cb_ref, wg_ref, bg_ref, sp_ref, h0_ref, perm_ref,
         out_ref, hl_ref, c_scr) = refs
    g = pl.program_id(0)
    i = geom.n_blocks - 1 - g if reverse else g
    start = geom.seq_start(i)
    end = geom.seq_end(i)

    @pl.when(end if reverse else start)
    def _():
        c_scr[...] = h0_ref[...]

    sub_len = BLK // LRU_SUB
    perm = perm_ref[...]
    x = jnp.dot(perm, xa_ref[...], preferred_element_type=F32)
    pm = jnp.where(start, 0.0, 1.0)
    nm = jnp.where(end, 0.0, 1.0)
    hp = xp_ref.shape[0]
    p1 = xp_ref[hp - 1:hp, :].astype(F32) * pm
    p2 = xp_ref[hp - 2:hp - 1, :].astype(F32) * pm
    n0 = xn_ref[0:1, :].astype(F32) * nm
    row = lax.broadcasted_iota(jnp.int32, x.shape, 0)
    xm1 = jnp.where(row < LRU_SUB, pltpu.roll(x, LRU_SUB + 1, 0), pltpu.roll(x, LRU_SUB, 0))
    xm1 = jnp.where(row == 0, p1, xm1)
    xm2 = jnp.where(row < 2 * LRU_SUB, pltpu.roll(x, 2 * LRU_SUB + 1, 0), pltpu.roll(x, 2 * LRU_SUB, 0))
    xm2 = jnp.where(row == 0, p2, jnp.where(row == LRU_SUB, p1, xm2))
    xp1 = jnp.where(row >= BLK - LRU_SUB, pltpu.roll(x, BLK - LRU_SUB - 1, 0),
                    pltpu.roll(x, BLK - LRU_SUB, 0))
    xp1 = jnp.where(row == BLK - 1, n0, xp1)
    xc = (cw_ref[0:1, :] * xm2 + cw_ref[1:2, :] * xm1 + cw_ref[2:3, :] * x
          + cw_ref[3:4, :] * xp1 + cb_ref[...])

    gt = jnp.dot(xc.astype(BF16), wg_ref[...], preferred_element_type=F32) + bg_ref[...]
    r = jax.nn.sigmoid(gt[:, :D_RNN])
    ig = jax.nn.sigmoid(gt[:, D_RNN:])
    a = jnp.exp(-LRU_C * r * sp_ref[...])
    u = jnp.sqrt(1.0 - a * a) * ig * xc

    h = jnp.zeros((LRU_SUB, D_RNN), F32)
    p = jnp.ones((LRU_SUB, D_RNN), F32)
    h_loc = [None] * sub_len
    p_loc = [None] * sub_len
    for t in (range(sub_len - 1, -1, -1) if reverse else range(sub_len)):
        a_t = a[t * LRU_SUB:(t + 1) * LRU_SUB, :]
        h = a_t * h + u[t * LRU_SUB:(t + 1) * LRU_SUB, :]
        p = a_t * p
        h_loc[t] = h
        p_loc[t] = p
    h_in = [None] * LRU_SUB
    state = c_scr[...]
    for k in (range(LRU_SUB - 1, -1, -1) if reverse else range(LRU_SUB)):
        h_in[k] = state
        state = h[k:k + 1, :] + p[k:k + 1, :] * state
    c_scr[...] = state
    hl_ref[...] = state
    h_in = jnp.concatenate(h_in, axis=0)
    h_full = jnp.concatenate([h_loc[t] + p_loc[t] * h_in for t in range(sub_len)], axis=0)
    if reverse:
        gv = jnp.dot(perm, ga_ref[...], preferred_element_type=F32)
        y = (_gelu_tanh(gv) * (hf_ref[...] + h_full)).astype(BF16)
        out_ref[...] = jnp.dot(permt_ref[...], y, preferred_element_type=F32).astype(BF16)
    else:
        out_ref[...] = h_full


def _lru_ret_bwd_kernel(geom, n_lru_in, n_ret_in, *refs):
    a, b = n_lru_in, n_lru_in + n_ret_in
    lru_in, ret_in = refs[:a], refs[a:b]
    lru_out, ret_out = refs[b:b + 2], refs[b + 2:b + 4]
    lru_scr, ret_scr = refs[b + 4:b + 5], refs[b + 5:b + 6]
    _lru_kernel(geom, True, *lru_in, *lru_out, *lru_scr)
    _ret_bwd_kernel(geom, *ret_in, *ret_out, *ret_scr)


def _lru_call(geom, reverse, proj, conv_w, conv_b, wg, bg, sp, h0, hf=None, ret=None):
    nb = geom.n_blocks
    halo = 16
    hpb = BLK // halo

    def blk(g):
        return nb - 1 - g if reverse else g

    d = 1 if reverse else 0
    in_specs = [
        pl.BlockSpec((BLK, D_RNN), lambda g: (blk(g), C_XA // D_RNN)),
        pl.BlockSpec((halo, D_RNN), lambda g: (jnp.maximum(blk(g) * hpb - 1, 0), C_XA // D_RNN)),
        pl.BlockSpec((halo, D_RNN), lambda g: (jnp.minimum((blk(g) + 1) * hpb, nb * hpb - 1), C_XA // D_RNN)),
        pl.BlockSpec((CONV_W, D_RNN), lambda g: (0, 0)),
        pl.BlockSpec((1, D_RNN), lambda g: (0, 0)),
        pl.BlockSpec((D_RNN, 2 * D_RNN), lambda g: (0, 0)),
        pl.BlockSpec((1, 2 * D_RNN), lambda g: (0, 0)),
        pl.BlockSpec((1, D_RNN), lambda g: (0, 0)),
        pl.BlockSpec((None, None, 1, D_RNN), lambda g: (geom.seq_id(blk(g)), d, 0, 0)),
    ]
    pos = np.arange(BLK)
    perm_np = np.zeros((BLK, BLK), np.float32)
    perm_np[pos, (pos % LRU_SUB) * (BLK // LRU_SUB) + pos // LRU_SUB] = 1.0
    in_specs.append(pl.BlockSpec((BLK, BLK), lambda g: (0, 0)))
    args = [proj, proj, proj, conv_w, conv_b, wg, bg, sp, h0, jnp.asarray(perm_np, BF16)]
    if reverse:
        in_specs += [pl.BlockSpec((BLK, BLK), lambda g: (0, 0)),
                     pl.BlockSpec((BLK, D_RNN), lambda g: (blk(g), C_GA // D_RNN)),
                     pl.BlockSpec((BLK, D_RNN), lambda g: (blk(g), 0))]
        args += [jnp.asarray(perm_np.T, BF16), proj, hf]
        out_dtype = BF16
    else:
        out_dtype = F32
    scratch = [pltpu.VMEM((1, D_RNN), F32)]
    out_specs = [pl.BlockSpec((BLK, D_RNN), lambda g: (blk(g), 0)),
                 pl.BlockSpec((None, 1, D_RNN), lambda g: (blk(g), 0, 0))]
    out_shape = [jax.ShapeDtypeStruct((geom.n_tok, D_RNN), out_dtype),
                 jax.ShapeDtypeStruct((nb, 1, D_RNN), F32)]
    body = functools.partial(_lru_kernel, geom, reverse)
    if ret is not None:
        assert reverse
        l, rkt, kd_b, cd_b, s0 = ret
        hs = RET_HEADS * RET_QK
        ret_specs = [pl.BlockSpec((hs, BLK), lambda g: (0, blk(g))),
                     pl.BlockSpec((BLK, RET_HEADS * RET_V), lambda g: (blk(g), C_RV // (RET_HEADS * RET_V))),
                     pl.BlockSpec((hs, BLK), lambda g: (0, 0)),
                     pl.BlockSpec((hs, RET_V), lambda g: (0, 0)),
                     pl.BlockSpec((None, None, None, hs, RET_V),
                                  lambda g: (jnp.maximum(geom.seq_id(blk(g)) - geom.batch, 0), l, 1, 0, 0))]
        body = functools.partial(_lru_ret_bwd_kernel, geom, len(in_specs), len(ret_specs))
        in_specs += ret_specs
        args += [rkt, proj, kd_b, cd_b, s0]
        out_specs += [pl.BlockSpec((None, hs, RET_V), lambda g: (blk(g), 0, 0)),
                      pl.BlockSpec((None, hs, RET_V), lambda g: (blk(g), 0, 0))]
        out_shape += [jax.ShapeDtypeStruct((nb, hs, RET_V), F32), jax.ShapeDtypeStruct((nb, hs, RET_V), F32)]
        scratch.append(pltpu.VMEM((hs, RET_V), F32))
    return pl.pallas_call(
        body,
        grid=(nb,),
        in_specs=in_specs,
        out_specs=out_specs,
        out_shape=out_shape,
        scratch_shapes=scratch,
        compiler_params=_cparams(("arbitrary",)),
        name=("lru_ret_bwd" if ret is not None else "lru_bwd") if reverse else "lru_fwd",
    )(*args)


def _group_rms(x, w, ones):
    xx = x * x
    hi = xx.astype(BF16)
    lo = (xx - hi.astype(F32)).astype(BF16)
    ss = (jnp.dot(hi, ones, preferred_element_type=F32)
          + jnp.dot(lo, ones, preferred_element_type=F32))
    return x * lax.rsqrt(ss * (1.0 / DA_QK) + EPS) * w


def _rope(x, cos, sin):
    lane = lax.broadcasted_iota(jnp.int32, x.shape, 1)
    first = (lane % (2 * ROPE_PAIRS)) < ROPE_PAIRS
    w = x.shape[1]
    partner = jnp.where(first, pltpu.roll(x, w - ROPE_PAIRS, 1), pltpu.roll(x, ROPE_PAIRS, 1))
    return x * cos + partner * sin


def _store_cache(out_ref, rows):
    if len(out_ref.shape) == 3:
        out_ref[...] = rows.reshape(out_ref.shape)
    else:
        s, d, t, w = out_ref.shape
        out_ref[:, 0] = rows.reshape(s, t, w)
        out_ref[:, 1:] = jnp.zeros((s, d - 1, t, w), out_ref.dtype)


def _prep_kernel(rope, *refs):
    if rope:
        dq_ref, dk_ref, qw_ref, kw_ref, ones_ref, cos_ref, sin_ref, q_out, k_out = refs
    else:
        dq_ref, dk_ref, qw_ref, kw_ref, ones_ref, dv_ref = refs[:6]
        q_out, k_out, kf_out, vf_out = refs[-4:]
        _store_cache(vf_out, dv_ref[...].astype(F32))
    ones = ones_ref[...]
    q = _group_rms(dq_ref[...].astype(F32), qw_ref[...], ones)
    k = _group_rms(dk_ref[...].astype(F32), kw_ref[...], ones)
    if rope:
        cos = jnp.concatenate([cos_ref[...]] * 4, axis=1)
        sin = jnp.concatenate([sin_ref[...]] * 4, axis=1)
        q = _rope(q, cos, sin)
        k = _rope(k, cos, sin)
    else:
        _store_cache(kf_out, k)
    q_out[...] = (q * (DA_QK ** -0.5 * math.log2(math.e))).astype(BF16)
    k_out[...] = k.astype(BF16)


def _prep_call(geom, latent, proj, qw, kw, ones, cos=None, sin=None, layer=0, prev_cache=None):
    tm = 512
    w = DA_HEADS * 2 * DA_QK
    if latent:
        n, off = geom.n_lat, geom.n_ctx // tm
        per = geom.dec_seq // tm
    else:
        n, off = geom.n_ctx, 0
    in_specs = [pl.BlockSpec((tm, w), lambda i: (i + off, C_DQ // w)),
                pl.BlockSpec((tm, w), lambda i: (i + off, C_DK // w)),
                pl.BlockSpec((1, w), lambda i: (0, 0)),
                pl.BlockSpec((1, w), lambda i: (0, 0)),
                pl.BlockSpec((w, w), lambda i: (0, 0))]
    args = [proj, proj, qw, kw, ones]
    aliases = {}
    out_specs = [pl.BlockSpec((tm, w), lambda i: (i, 0)), pl.BlockSpec((tm, w), lambda i: (i, 0))]
    out_shape = [jax.ShapeDtypeStruct((n, w), BF16), jax.ShapeDtypeStruct((n, w), BF16)]
    if latent:
        in_specs += [pl.BlockSpec((tm, 2 * DA_QK), lambda i: (i % per, 0)),
                     pl.BlockSpec((tm, 2 * DA_QK), lambda i: (i % per, 0))]
        args += [cos, sin]
    else:
        in_specs.append(pl.BlockSpec((tm, w), lambda i: (i, C_DV // w)))
        args.append(proj)
        spt = tm // geom.seq
        if prev_cache is None:
            assert layer == 0
            cache_spec = pl.BlockSpec((spt, DEPTH, geom.seq, w), lambda i: (i, 0, 0, 0))
        else:
            cache_spec = pl.BlockSpec((spt, None, geom.seq, w), lambda i: (i, layer, 0, 0))
        cache_shape = jax.ShapeDtypeStruct((geom.batch, DEPTH, geom.seq, w), F32)
        out_specs += [cache_spec, cache_spec]
        out_shape += [cache_shape, cache_shape]
        if prev_cache is not None:
            aliases = {len(args): 2, len(args) + 1: 3}
            in_specs += [pl.BlockSpec(memory_space=pl.ANY), pl.BlockSpec(memory_space=pl.ANY)]
            args += list(prev_cache)
    return pl.pallas_call(
        functools.partial(_prep_kernel, latent),
        grid=(n // tm,),
        in_specs=in_specs, out_specs=out_specs, out_shape=out_shape,
        input_output_aliases=aliases,
        compiler_params=_cparams(("arbitrary",)),
        name="qk_prep_lat" if latent else "qk_prep_ctx",
    )(*args)


ATT_KC = 256
ATT_TQ = 256
LOG2E = math.log2(math.e)
ATT_SAFE_LOGIT = 60.0


def _attn_kernel(out_scale, has_cache, *refs):
    if has_cache:
        par_ref, q_ref, kc_ref, vc_ref, kl_ref, vl_ref, sw_ref, o_ref, e_scr, o_scr = refs
        srcs = [(kc_ref, vc_ref), (kl_ref, vl_ref)]
    else:
        par_ref, q_ref, kl_ref, vl_ref, sw_ref, o_ref, e_scr, o_scr = refs
        srcs = [(kl_ref, vl_ref)]
    chunks = [(kr, vr, st) for kr, vr in srcs for st in range(0, kr.shape[0], ATT_KC)]
    lam = par_ref[0]
    no_shift = par_ref[1] > 0.5
    tqs = ATT_TQ
    nsub = q_ref.shape[0] // tqs
    nt = (((1,), (1,)), ((), ()))
    half = ATT_KC // 2

    def stacked_q(sb):
        q = q_ref[sb * tqs:(sb + 1) * tqs, :]
        lane = lax.broadcasted_iota(jnp.int32, q.shape, 1)
        zero = jnp.zeros_like(q)
        return jnp.concatenate([jnp.where(lane < DA_QK, q, zero), jnp.where(lane >= DA_QK, q, zero)], axis=0)

    def logits(qq, c):
        kr, vr, st = chunks[c]
        return lax.dot_general(qq, kr[st:st + ATT_KC, :], nt, preferred_element_type=F32)

    def fold(total, e):
        part = e[:, :half] + e[:, half:]
        return part if total is None else total + part

    def row_stats(lsum):
        l = jnp.sum(lsum, axis=-1, keepdims=True)
        l1 = l[0:tqs]
        return l1, lam * l1 / l[tqs:2 * tqs]

    def pv(acc, buf, c, rho):
        kr, vr, st = chunks[c]
        w = (e_scr[buf, c, 0:tqs, :] - rho * e_scr[buf, c, tqs:2 * tqs, :]).astype(BF16)
        t = jnp.dot(w, vr[st:st + ATT_KC, :], preferred_element_type=F32)
        return t if acc is None else acc + t

    nck = len(chunks)

    @pl.when(no_shift)
    def _():
        stats = None
        for sb in range(nsub + 1):
            qq = stacked_q(sb) if sb < nsub else None
            lsum, acc = None, None
            for c in range(nck):
                if sb < nsub:
                    e = jnp.exp2(logits(qq, c))
                    e_scr[sb % 2, c] = e
                    lsum = fold(lsum, e)
                if sb > 0:
                    acc = pv(acc, (sb - 1) % 2, c, stats[1])
            if sb > 0:
                o_scr[(sb - 1) * tqs:sb * tqs, :] = acc / stats[0]
            if sb < nsub:
                stats = row_stats(lsum)

    @pl.when(jnp.logical_not(no_shift))
    def _():
        for sb in range(nsub):
            qq = stacked_q(sb)
            m = None
            for c in range(nck):
                s = logits(qq, c)
                e_scr[0, c] = s
                mc = jnp.max(s, axis=-1, keepdims=True)
                m = mc if m is None else jnp.maximum(m, mc)
            lsum = None
            for c in range(nck):
                e = jnp.exp2(e_scr[0, c] - m)
                e_scr[0, c] = e
                lsum = fold(lsum, e)
            l1, rho = row_stats(lsum)
            acc = None
            for c in range(nck):
                acc = pv(acc, 0, c, rho)
            o_scr[sb * tqs:(sb + 1) * tqs, :] = acc / l1

    o = o_scr[...]
    y = o * lax.rsqrt(jnp.mean(o * o, axis=-1, keepdims=True) + EPS) * sw_ref[...]
    o_ref[...] = (y * out_scale).astype(BF16)


def _attn_call(par, lam_init, q2d, k2d, proj, v_row_off, n_b, t_q, t_kl, tq, subln_w, cache=None):
    hw = 2 * DA_QK
    nq = t_q // tq
    vcol = C_DV // DA_V
    in_specs = [pl.BlockSpec(memory_space=pltpu.SMEM),
                pl.BlockSpec((tq, hw), lambda b, h, qi: (b * nq + qi, h))]
    args = [par, q2d]
    n_chunks = t_kl // ATT_KC
    if cache is not None:
        kc, vc = cache
        p = kc.shape[1]
        n_chunks += p // ATT_KC
        in_specs += [pl.BlockSpec((None, p, hw), lambda b, h, qi: (b, 0, h)),
                     pl.BlockSpec((None, p, DA_V), lambda b, h, qi: (b, 0, h))]
        args += [kc, vc]
    in_specs += [pl.BlockSpec((t_kl, hw), lambda b, h, qi: (b, h)),
                 pl.BlockSpec((t_kl, DA_V), lambda b, h, qi: (v_row_off + b, vcol + h)),
                 pl.BlockSpec((1, DA_V), lambda b, h, qi: (0, 0))]
    args += [k2d, proj, subln_w.reshape(1, DA_V)]
    return pl.pallas_call(
        functools.partial(_attn_kernel, 1.0 - lam_init, cache is not None),
        grid=(n_b, DA_HEADS, nq),
        in_specs=in_specs,
        out_specs=pl.BlockSpec((tq, DA_V), lambda b, h, qi: (b * nq + qi, h)),
        out_shape=jax.ShapeDtypeStruct((n_b * t_q, DA_HEADS * DA_V), BF16),
        scratch_shapes=[pltpu.VMEM((2 if tq > ATT_TQ else 1, n_chunks, 2 * ATT_TQ, ATT_KC), F32),
                        pltpu.VMEM((tq, DA_V), F32)],
        compiler_params=_cparams(("arbitrary", "arbitrary", "arbitrary")),
        name="diff_attn_lat" if cache is not None else "diff_attn_ctx",
    )(*args)


def _ret_state_update(kt, v, kd, cd, s_old):
    parts = []
    for h in range(RET_HEADS):
        rows = slice(h * RET_QK, (h + 1) * RET_QK)
        kh = (kt[rows, :].astype(F32) * kd[rows, :]).astype(BF16)
        parts.append(jnp.dot(kh, v[:, h * RET_V:(h + 1) * RET_V], preferred_element_type=F32))
    return cd * s_old + jnp.concatenate(parts, axis=0)


def _ret_bwd_kernel(geom, kt_ref, v_ref, kd_ref, cd_ref, s0_ref, sstart_ref, send_ref, s_scr):
    i = geom.n_blocks - 1 - pl.program_id(0)

    @pl.when(geom.seq_end(i))
    def _():
        s_scr[...] = jnp.where(i >= geom.ctx_blocks, s0_ref[...].astype(F32), 0.0)

    s_old = s_scr[...]
    sstart_ref[...] = s_old
    kt = kt_ref[...] * jnp.asarray(RET_QK ** -0.5, BF16)
    s_new = _ret_state_update(kt, v_ref[...], kd_ref[...], cd_ref[...], s_old)
    s_scr[...] = s_new
    send_ref[...] = s_new


def _ret_main_kernel(geom, q_ref, kt_ref, v_ref, g_ref, dsum_ref, qdf_ref, qdb_ref, kd_ref, cd_ref,
                     s0_ref, sb_ref, o_ref, send_ref, s_scr):
    i = pl.program_id(0)

    @pl.when(geom.seq_start(i))
    def _():
        s_scr[...] = jnp.where(i >= geom.ctx_blocks, s0_ref[...].astype(F32), 0.0)

    s_f = s_scr[...]
    s_fb = s_f.astype(BF16)
    s_bb = sb_ref[...].astype(BF16)
    q = q_ref[...].astype(F32)
    kt = kt_ref[...] * jnp.asarray(RET_QK ** -0.5, BF16)
    v = v_ref[...]
    lane = lax.broadcasted_iota(jnp.int32, q.shape, 1)
    for h in range(RET_HEADS):
        in_head = (lane >= h * RET_QK) & (lane < (h + 1) * RET_QK)
        qh = jnp.where(in_head, q, 0.0)
        vh = v[:, h * RET_V:(h + 1) * RET_V]
        sc = jnp.dot(qh.astype(BF16), kt, preferred_element_type=F32) * dsum_ref[h]
        o = jnp.dot(sc.astype(BF16), vh, preferred_element_type=F32)
        o += jnp.dot((qh * qdf_ref[...]).astype(BF16), s_fb, preferred_element_type=F32)
        o += jnp.dot((qh * qdb_ref[...]).astype(BF16), s_bb, preferred_element_type=F32)
        y = o * lax.rsqrt(jnp.mean(o * o, axis=-1, keepdims=True) + EPS)
        gv = g_ref[:, h * RET_V:(h + 1) * RET_V].astype(F32)
        o_ref[:, h * RET_V:(h + 1) * RET_V] = (y * (gv * jax.nn.sigmoid(gv))).astype(BF16)
    s_new = _ret_state_update(kt, v, kd_ref[...], cd_ref[...], s_f)
    s_scr[...] = s_new
    send_ref[...] = s_new


def _ret_main_call(geom, l, proj, rkt, dsum, qdf, qdb, kd_f, cd_f, s0, sb_start):
    nb = geom.n_blocks
    hs = RET_HEADS * RET_QK
    hv = RET_HEADS * RET_V
    return pl.pallas_call(
        functools.partial(_ret_main_kernel, geom),
        grid=(nb,),
        in_specs=[pl.BlockSpec((BLK, hs), lambda g: (g, C_RQ // hs)),
                  pl.BlockSpec((hs, BLK), lambda g: (0, g)),
                  pl.BlockSpec((BLK, hv), lambda g: (g, C_RV // hv)),
                  pl.BlockSpec((BLK, hv), lambda g: (g, C_RG // hv)),
                  pl.BlockSpec((RET_HEADS, BLK, BLK), lambda g: (0, 0, 0)),
                  pl.BlockSpec((BLK, hs), lambda g: (0, 0)),
                  pl.BlockSpec((BLK, hs), lambda g: (0, 0)),
                  pl.BlockSpec((hs, BLK), lambda g: (0, 0)),
                  pl.BlockSpec((hs, RET_V), lambda g: (0, 0)),
                  pl.BlockSpec((None, None, None, hs, RET_V),
                               lambda g: (jnp.maximum(geom.seq_id(g) - geom.batch, 0), l, 0, 0, 0)),
                  pl.BlockSpec((None, hs, RET_V), lambda g: (g, 0, 0))],
        out_specs=[pl.BlockSpec((BLK, hv), lambda g: (g, 0)),
                   pl.BlockSpec((None, hs, RET_V), lambda g: (g, 0, 0))],
        out_shape=[jax.ShapeDtypeStruct((geom.n_tok, hv), BF16),
                   jax.ShapeDtypeStruct((nb, hs, RET_V), F32)],
        scratch_shapes=[pltpu.VMEM((hs, RET_V), F32)],
        compiler_params=_cparams(("arbitrary",)),
        name="ret_main",
    )(proj, rkt, proj, proj, dsum, qdf, qdb, kd_f, cd_f, s0, sb_start)


def _ret_tables(ret_decay_l):
    log_g = jax.nn.log_sigmoid(ret_decay_l.astype(F32))
    pos = jnp.arange(BLK, dtype=F32)
    diff = pos[:, None] - pos[None, :]
    lf = log_g[0][:, None, None]
    lb = log_g[1][:, None, None]
    dsum = (jnp.where(diff >= 0, jnp.exp(jnp.maximum(diff, 0.0)[None] * lf), 0.0)
            + jnp.where(diff <= 0, jnp.exp(jnp.maximum(-diff, 0.0)[None] * lb), 0.0))

    def per_lane(e, lg):
        return jnp.repeat(jnp.exp(e[:, None] * lg[None, :]), RET_QK, axis=1)

    qdf = per_lane(pos + 1.0, log_g[0])
    qdb = per_lane(BLK - pos, log_g[1])
    kd_f = per_lane(BLK - 1.0 - pos, log_g[0]).T
    kd_b = per_lane(pos, log_g[1]).T
    cd_f = jnp.broadcast_to(jnp.repeat(jnp.exp(BLK * log_g[0]), RET_QK)[:, None], (RET_HEADS * RET_QK, RET_V))
    cd_b = jnp.broadcast_to(jnp.repeat(jnp.exp(BLK * log_g[1]), RET_QK)[:, None], (RET_HEADS * RET_QK, RET_V))
    return dsum, qdf, qdb, kd_f, kd_b, cd_f, cd_b


def _merge_kernel(n_ctx_tiles, ba_ref, bbc_ref, bbl_ref, bc_ref, g0_ref, g1_ref, g2_ref, xc_ref, xl_ref,
                  gate_ref, sc_ref, sh_ref, nw_ref, wb_ref, wo_ref, rhi_ref, rlo_ref, x1_ref, h2_ref, h2p_ref,
                  lt_ref):
    branches = (ba_ref[...], _pick_part(n_ctx_tiles, bbc_ref, bbl_ref), bc_ref[...])
    acc = None
    for br, (b, g_ref) in enumerate(zip(branches, (g0_ref, g1_ref, g2_ref))):
        p = jnp.dot(b, wb_ref[br], preferred_element_type=F32)
        t = (0.5 * jnp.tanh(0.5 * g_ref[...].astype(F32)) + 0.5) * p
        acc = t if acc is None else acc + t
    m = jnp.dot(acc.astype(BF16), wo_ref[...], preferred_element_type=F32)
    x1 = _pick_part(n_ctx_tiles, xc_ref, xl_ref) + gate_ref[...] * m
    x1_ref[...] = x1
    ms = jnp.mean(x1 * x1, axis=-1, keepdims=True)
    h2 = x1 * lax.rsqrt(ms + EPS) * nw_ref[...] * (1.0 + sc_ref[...]) + sh_ref[...]
    h2b = h2.astype(BF16)
    h2_ref[...] = h2b
    h2p_ref[...] = _pack_halves(h2b.astype(F32))
    h2lo = (h2 - h2b.astype(F32)).astype(BF16)
    nt = (((1,), (1,)), ((), ()))
    lt_ref[...] = (lax.dot_general(rhi_ref[...], h2b, nt, preferred_element_type=F32)
                   + lax.dot_general(rhi_ref[...], h2lo, nt, preferred_element_type=F32)
                   + lax.dot_general(rlo_ref[...], h2b, nt, preferred_element_type=F32))


def _merge_call(geom, l, ba, bb_ctx, bb_lat, bc, proj, x_ctx, x_lat, mod6, norm2_w, wb_bf, wo_bf, r_hi, r_lo):
    tm = 512
    gcol = C_GL // D_MODEL
    full = lambda shape: pl.BlockSpec(shape, lambda i: tuple(0 for _ in shape))
    tok = lambda w: pl.BlockSpec((tm, w), lambda i: (i, 0))
    return pl.pallas_call(
        functools.partial(_merge_kernel, geom.n_ctx // tm),
        grid=(geom.n_tok // tm,),
        in_specs=[tok(BRANCH_W)] + _split_in_specs(geom, tm, BRANCH_W, 1) + [tok(BRANCH_W),
                  pl.BlockSpec((tm, D_MODEL), lambda i: (i, gcol)),
                  pl.BlockSpec((tm, D_MODEL), lambda i: (i, gcol + 1)),
                  pl.BlockSpec((tm, D_MODEL), lambda i: (i, gcol + 2))]
                 + _split_in_specs(geom, tm, D_MODEL, 1) + [
                  _mod_spec(geom, l, 2, tm, 1), _mod_spec(geom, l, 4, tm, 1), _mod_spec(geom, l, 3, tm, 1),
                  full((1, D_MODEL)),
                  full((N_BRANCH, BRANCH_W, D_MODEL)), full((D_MODEL, D_MODEL)),
                  full((N_EXPERTS, D_MODEL)), full((N_EXPERTS, D_MODEL))],
        out_specs=[tok(D_MODEL), tok(D_MODEL), tok(D_MODEL // 2), pl.BlockSpec((N_EXPERTS, tm), lambda i: (0, i))],
        out_shape=[jax.ShapeDtypeStruct((geom.n_tok, D_MODEL), F32),
                   jax.ShapeDtypeStruct((geom.n_tok, D_MODEL), BF16),
                   jax.ShapeDtypeStruct((geom.n_tok, D_MODEL // 2), jnp.uint32),
                   jax.ShapeDtypeStruct((N_EXPERTS, geom.n_tok), F32)],
        compiler_params=_cparams(("arbitrary",)),
        name="merge_out",
    )(ba, bb_ctx, bb_lat, bc, proj, proj, proj, x_ctx, x_lat, mod6, mod6, mod6,
      norm2_w.reshape(1, D_MODEL), wb_bf, wo_bf, r_hi, r_lo)


def _router_kernel(lt_ref, bias_ref, ltri_ref, utri_ref, g_ref, slot_ref, cnt_ref, cnt_scr):
    per = N_EXPERTS // N_GROUPS
    tm = lt_ref.shape[1]
    scores = jax.nn.sigmoid(lt_ref[...])
    biased = scores + bias_ref[...]
    b3 = biased.reshape(N_GROUPS, per, tm)
    neg = jnp.float32(-jnp.inf)
    m1 = jnp.max(b3, axis=1, keepdims=True)
    is_m1 = b3 == m1
    cnt = jnp.sum(is_m1.astype(F32), axis=1, keepdims=True)
    m2 = jnp.max(jnp.where(is_m1, neg, b3), axis=1, keepdims=True)
    grp = (m1 + jnp.where(cnt >= 2.0, m1, m2)).reshape(N_GROUPS, tm)
    gidx = lax.broadcasted_iota(jnp.int32, (N_GROUPS, tm), 0)
    grank = jnp.zeros((N_GROUPS, tm), F32)
    for g2 in range(N_GROUPS):
        other = grp[g2:g2 + 1, :]
        ahead = (other > grp) | ((other == grp) & (gidx > g2))
        grank += ahead.astype(F32)
    gsel = (grank < float(TOPK_GROUPS)).astype(F32)
    emask = jnp.broadcast_to(gsel.reshape(N_GROUPS, 1, tm), (N_GROUPS, per, tm)).reshape(N_EXPERTS, tm)
    masked = jnp.where(emask > 0.0, biased, neg)
    eidx = lax.broadcasted_iota(jnp.int32, (N_EXPERTS, tm), 0)
    erank = jnp.zeros((N_EXPERTS, tm), F32)
    for e2 in range(N_EXPERTS):
        other = masked[e2:e2 + 1, :]
        ahead = (other > masked) | ((other == masked) & (eidx > e2))
        erank += ahead.astype(F32)
    sel = erank < float(TOP_K)
    w = jnp.where(sel, scores, 0.0)
    gates_t = w / jnp.sum(w, axis=0, keepdims=True) * ROUTED_SCALE

    @pl.when(pl.program_id(0) == 0)
    def _():
        cnt_scr[...] = jnp.zeros_like(cnt_scr)

    selb = sel.astype(BF16)
    slot = jnp.dot(ltri_ref[...], selb, preferred_element_type=F32)
    carry = cnt_scr[:, 0:1]
    rank = jnp.dot(selb, utri_ref[...], preferred_element_type=F32) + carry
    cnt_new = cnt_scr[...] + jnp.sum(sel.astype(F32), axis=1, keepdims=True)
    cnt_scr[...] = cnt_new
    cnt_ref[...] = cnt_new
    eid_f = eidx.astype(F32)
    g_rows, e_rows, r_rows = [], [], []
    for k in range(TOP_K):
        mk = jnp.where(sel & (slot == float(k)), 1.0, 0.0)
        g_rows.append(jnp.sum(mk * gates_t, axis=0, keepdims=True))
        e_rows.append(jnp.sum(mk * eid_f, axis=0, keepdims=True))
        r_rows.append(jnp.sum(mk * rank, axis=0, keepdims=True))
    slot_ref[...] = jnp.concatenate(e_rows + r_rows, axis=0).astype(jnp.int32)
    pad = jnp.zeros((GATE_W - TOP_K, tm), F32)
    g_ref[...] = jnp.concatenate(g_rows + [pad], axis=0).T


ROUTER_TM = 512


def _router_call(geom, logits_t, bias):
    tm = ROUTER_TM
    ltri = jnp.asarray(np.tril(np.ones((N_EXPERTS, N_EXPERTS), np.float32), -1), BF16)
    utri = jnp.asarray(np.triu(np.ones((tm, tm), np.float32), 1), BF16)
    return pl.pallas_call(
        _router_kernel,
        grid=(geom.n_tok // tm,),
        in_specs=[pl.BlockSpec((N_EXPERTS, tm), lambda i: (0, i)),
                  pl.BlockSpec((N_EXPERTS, 1), lambda i: (0, 0)),
                  pl.BlockSpec((N_EXPERTS, N_EXPERTS), lambda i: (0, 0)),
                  pl.BlockSpec((tm, tm), lambda i: (0, 0))],
        out_specs=[pl.BlockSpec((tm, GATE_W), lambda i: (i, 0)),
                   pl.BlockSpec((2 * TOP_K, tm), lambda i: (0, i)),
                   pl.BlockSpec((N_EXPERTS, GATE_W), lambda i: (0, 0))],
        out_shape=[jax.ShapeDtypeStruct((geom.n_tok, GATE_W), F32),
                   jax.ShapeDtypeStruct((2 * TOP_K, geom.n_tok), jnp.int32),
                   jax.ShapeDtypeStruct((N_EXPERTS, GATE_W), F32)],
        scratch_shapes=[pltpu.VMEM((N_EXPERTS, GATE_W), F32)],
        compiler_params=_cparams(("arbitrary",)),
        name="router",
    )(logits_t, bias.reshape(N_EXPERTS, 1), ltri, utri)


MOE_TR = 512
SC_CORES = 2
SC_SUBCORES = 16
SC_CHUNK = 64


def _sc_worker_base(rows_per_worker):
    wid = lax.axis_index("s") * SC_CORES + lax.axis_index("c")
    return wid * rows_per_worker


def _sc_scatter_rows(table, pos_flat, n_slots, n_rows_out):
    n, d = table.shape
    nw = SC_CORES * SC_SUBCORES
    assert n % (nw * SC_CHUNK) == 0
    per_w = n // nw
    mesh = plsc.VectorSubcoreMesh(core_axis_name="c", subcore_axis_name="s")

    @functools.partial(
        pl.kernel, mesh=mesh,
        out_type=jax.ShapeDtypeStruct((n_rows_out, d), table.dtype),
        scratch_types=[[pltpu.VMEM((SC_CHUNK,), jnp.int32) for _ in range(n_slots)],
                       pltpu.VMEM((SC_CHUNK, d), table.dtype),
                       pltpu.SemaphoreType.DMA],
    )
    def scatter(table_hbm, pos_hbm, out_hbm, idx_v, rows_v, sem):
        base = _sc_worker_base(per_w)

        @pl.loop(0, per_w // SC_CHUNK)
        def _(ci):
            off = pl.multiple_of(base + ci * SC_CHUNK, 8)
            for k in range(n_slots):
                pltpu.sync_copy(pos_hbm.at[pl.ds(pl.multiple_of(k * n + off, 8), SC_CHUNK)], idx_v[k])
            pltpu.sync_copy(table_hbm.at[pl.ds(off, SC_CHUNK)], rows_v)
            copies = [pltpu.make_async_copy(rows_v, out_hbm.at[idx_v[k]], sem) for k in range(n_slots)]
            for cp in copies:
                cp.start()
            for cp in copies:
                cp.wait()

    return scatter(table, pos_flat)


def _sc_gather_rows(table, idx):
    b = idx.shape[0]
    d = table.shape[1]
    nw = SC_CORES * SC_SUBCORES
    nbuf = 2
    assert b % (nw * SC_CHUNK * nbuf) == 0
    per_w = b // nw
    n_chunks = per_w // SC_CHUNK
    mesh = plsc.VectorSubcoreMesh(core_axis_name="c", subcore_axis_name="s")

    @functools.partial(
        pl.kernel, mesh=mesh,
        out_type=jax.ShapeDtypeStruct((b, d), table.dtype),
        scratch_types=[pltpu.VMEM((per_w,), jnp.int32),
                       [pltpu.VMEM((SC_CHUNK, d), table.dtype) for _ in range(nbuf)],
                       [pltpu.SemaphoreType.DMA for _ in range(nbuf)],
                       [pltpu.SemaphoreType.DMA for _ in range(nbuf)]],
    )
    def gather(table_hbm, idx_hbm, out_hbm, idx_v, rows, gsem, wsem):
        base = _sc_worker_base(per_w)
        pltpu.sync_copy(idx_hbm.at[pl.ds(pl.multiple_of(base, 8), per_w)], idx_v)

        def fetch(ci, slot):
            src = table_hbm.at[idx_v.at[pl.ds(pl.multiple_of(ci * SC_CHUNK, 8), SC_CHUNK)]]
            return pltpu.make_async_copy(src, rows[slot], gsem[slot])

        def put(ci, slot):
            dst = out_hbm.at[pl.ds(pl.multiple_of(base + ci * SC_CHUNK, 8), SC_CHUNK)]
            return pltpu.make_async_copy(rows[slot], dst, wsem[slot])

        for slot in range(nbuf):
            fetch(slot, slot).start()

        @pl.loop(0, n_chunks, step=nbuf)
        def _(c0):
            for slot in range(nbuf):
                ci = c0 + slot
                fetch(ci, slot).wait()
                put(ci, slot).start()
                put(ci, slot).wait()

                @pl.when(ci + nbuf < n_chunks)
                def _():
                    fetch(ci + nbuf, slot).start()

    return gather(table, idx)


def _route_positions(n_tok, slots, counts):
    cnt = counts[:, 0].astype(jnp.int32)
    cnt_pad = ((cnt + MOE_TR - 1) // MOE_TR) * MOE_TR
    off_end = jnp.cumsum(cnt_pad)
    off = off_end - cnt_pad
    eid, rank = slots[:TOP_K], slots[TOP_K:]
    eids = jnp.arange(N_EXPERTS, dtype=jnp.int32)
    pos = jnp.sum(jnp.where(eid[..., None] == eids, off, 0), axis=-1) + rank
    n_tiles = (TOP_K * n_tok) // MOE_TR + N_EXPERTS
    tile_start = jnp.arange(n_tiles, dtype=jnp.int32) * MOE_TR
    tile_expert = jnp.sum((tile_start[:, None] >= off_end[None, :]).astype(jnp.int32), axis=1)
    tile_expert = jnp.minimum(tile_expert, N_EXPERTS - 1)
    n_used = (off_end[-1] // MOE_TR).reshape(1)
    tile_idx = jnp.arange(n_tiles, dtype=jnp.int32)
    used = tile_idx < n_used[0]
    prev = jnp.concatenate([jnp.full((1,), -1, jnp.int32), tile_expert[:-1]])
    first = jnp.logical_and(used, tile_expert != prev)
    parity = (jnp.cumsum(first.astype(jnp.int32)) - 1) % 2
    later = jnp.logical_and(eids[None, :] > eids[:, None], (cnt_pad > 0)[None, :])
    next_e = jnp.min(jnp.where(later, eids[None, :], N_EXPERTS), axis=1)
    nxt = jnp.sum(jnp.where(tile_expert[:, None] == eids, next_e, 0), axis=1)
    has_next = jnp.logical_and(first, nxt < N_EXPERTS)
    sched = (tile_expert, n_used, first.astype(jnp.int32), jnp.minimum(nxt, N_EXPERTS - 1).astype(jnp.int32),
             jnp.maximum(parity, 0).astype(jnp.int32), has_next.astype(jnp.int32))
    return pos, sched, n_tiles


def _expert_ffn(x_lo, x_hi, gu, dn):
    half = D_MODEL // 2
    a = (jnp.dot(x_lo, gu[0:half, :], preferred_element_type=F32)
         + jnp.dot(x_hi, gu[half:, :], preferred_element_type=F32))
    hg = a[:, :D_EXPERT]
    act = (hg * jax.nn.sigmoid(hg)) * a[:, D_EXPERT:]
    return jnp.dot(act.astype(BF16), dn, preferred_element_type=F32)


def _experts_kernel(l, te_ref, nu_ref, first_ref, nxt_ref, par_ref, hasn_ref, x_ref, gu_hbm, dn_hbm, y_ref,
                    gu_f, dn_f, gu_b, dn_b, sem):
    i = pl.program_id(0)

    def fetch(e, slot):
        return (pltpu.make_async_copy(gu_hbm.at[l, e], gu_f.at[slot], sem.at[0, slot]),
                pltpu.make_async_copy(dn_hbm.at[l, e], dn_f.at[slot], sem.at[1, slot]))

    @pl.when(jnp.logical_and(i == 0, nu_ref[0] > 0))
    def _():
        for cp in fetch(te_ref[0], par_ref[0]):
            cp.start()

    @pl.when(first_ref[i] == 1)
    def _():
        slot = par_ref[i]
        for cp in fetch(te_ref[i], slot):
            cp.wait()

        @pl.when(hasn_ref[i] == 1)
        def _():
            for cp in fetch(nxt_ref[i], 1 - slot):
                cp.start()

        gu_b[...] = gu_f[slot].astype(BF16)
        dn_b[...] = dn_f[slot].astype(BF16)

    @pl.when(i < nu_ref[0])
    def _():
        lo, hi = _unpack_halves(x_ref[...])
        y = _expert_ffn(lo.astype(BF16), hi.astype(BF16), gu_b[...], dn_b[...])
        y_ref[...] = _pack_halves(y.astype(BF16).astype(F32))

    @pl.when(i >= nu_ref[0])
    def _():
        y_ref[...] = jnp.zeros_like(y_ref)


def _experts_call(l, xs, sched, n_tiles, w_gu, w_dn):
    half = D_MODEL // 2
    grid_spec = pltpu.PrefetchScalarGridSpec(
        num_scalar_prefetch=len(sched),
        grid=(n_tiles,),
        in_specs=[pl.BlockSpec((MOE_TR, half), lambda i, te, nu, *_: (jnp.minimum(i, jnp.maximum(nu[0], 1) - 1), 0)),
                  pl.BlockSpec(memory_space=pl.ANY),
                  pl.BlockSpec(memory_space=pl.ANY)],
        out_specs=pl.BlockSpec((MOE_TR, half), lambda i, *_: (i, 0)),
        scratch_shapes=[pltpu.VMEM((2, D_MODEL, 2 * D_EXPERT), w_gu.dtype),
                        pltpu.VMEM((2, D_EXPERT, D_MODEL), w_dn.dtype),
                        pltpu.VMEM((D_MODEL, 2 * D_EXPERT), BF16),
                        pltpu.VMEM((D_EXPERT, D_MODEL), BF16),
                        pltpu.SemaphoreType.DMA((2, 2))],
    )
    return pl.pallas_call(
        functools.partial(_experts_kernel, l),
        grid_spec=grid_spec,
        out_shape=jax.ShapeDtypeStruct((n_tiles * MOE_TR, half), jnp.uint32),
        compiler_params=_cparams(("arbitrary",)),
        name="moe_experts",
    )(*sched, xs, w_gu, w_dn)


MOE_OUT_PARTS = 2


def _moe_out_kernel(n_ctx_tiles, tile0, first, *refs):
    if first:
        yt_ref, g_ref, h_ref, sgu_ref, sdn_ref, x1_ref, gate_ref, oc_ref, ol_ref = refs
    else:
        yt_ref, g_ref, h_ref, sgu_ref, sdn_ref, x1_ref, gate_ref, _, ol_ref = refs
    i = pl.program_id(0) + tile0
    gts = g_ref[...]
    lane = lax.broadcasted_iota(jnp.int32, gts.shape, 1)
    acc_lo, acc_hi = None, None
    for k in range(TOP_K):
        ge = jnp.sum(jnp.where(lane == k, gts, 0.0), axis=1, keepdims=True)
        lo, hi = _unpack_halves(yt_ref[k])
        acc_lo = ge * lo if acc_lo is None else acc_lo + ge * lo
        acc_hi = ge * hi if acc_hi is None else acc_hi + ge * hi
    routed = jnp.concatenate([acc_lo, acc_hi], axis=1)
    h = h_ref[...]
    half = D_MODEL // 2
    shared = _expert_ffn(h[:, :half], h[:, half:], sgu_ref[...], sdn_ref[...])
    y = x1_ref[...] + gate_ref[...] * (routed + shared)
    if first:
        @pl.when(i < n_ctx_tiles)
        def _():
            oc_ref[...] = y

        @pl.when(i >= n_ctx_tiles)
        def _():
            ol_ref[...] = y
    else:
        ol_ref[...] = y


def _moe_out_call(geom, l, part, yt, gates, h2, sgu_bf, sdn_bf, x1, mod6, prev_lat=None):
    tm = 512
    nct = geom.n_ctx // tm
    n_part = geom.n_tok // tm // MOE_OUT_PARTS
    t0 = part * n_part
    first = part == 0
    assert nct <= n_part
    half = D_MODEL // 2
    tok = lambda w: pl.BlockSpec((tm, w), lambda i: (i + t0, 0))
    in_specs = [pl.BlockSpec((TOP_K, tm, half), lambda i: (0, i, 0)),
                tok(GATE_W), tok(D_MODEL),
                pl.BlockSpec((None, D_MODEL, 2 * D_EXPERT), lambda i: (l, 0, 0)),
                pl.BlockSpec((None, D_EXPERT, D_MODEL), lambda i: (l, 0, 0)),
                tok(D_MODEL),
                pl.BlockSpec((None, None, None, 1, D_MODEL), lambda i: (l, geom.mod_row(i + t0, tm), 5, 0, 0))]
    args = [yt, gates, h2, sgu_bf, sdn_bf, x1, mod6]
    lat_shape = jax.ShapeDtypeStruct((geom.n_lat, D_MODEL), F32)
    if first:
        out_specs = [pl.BlockSpec((tm, D_MODEL), lambda i: (jnp.minimum(i, nct - 1), 0)),
                     pl.BlockSpec((tm, D_MODEL), lambda i: (jnp.maximum(i - nct, 0), 0))]
        out_shape = [jax.ShapeDtypeStruct((geom.n_ctx, D_MODEL), F32), lat_shape]
        aliases = {}
    else:
        in_specs.append(pl.BlockSpec(memory_space=pl.ANY))
        args.append(prev_lat)
        out_specs = [pl.BlockSpec((tm, D_MODEL), lambda i: (i + t0 - nct, 0))]
        out_shape = [lat_shape]
        aliases = {len(args) - 1: 0}
    return pl.pallas_call(
        functools.partial(_moe_out_kernel, nct, t0, first),
        grid=(n_part,),
        in_specs=in_specs, out_specs=out_specs, out_shape=out_shape,
        input_output_aliases=aliases,
        compiler_params=_cparams(("arbitrary",)),
        name="moe_out",
    )(*args)


def _moe(geom, l, h2, h2p, gates, slots, counts, w_gu, w_dn, sgu_bf, sdn_bf, x1, mod6):
    pos, sched, n_tiles = _route_positions(geom.n_tok, slots, counts)
    xs = _sc_scatter_rows(h2p, pos.reshape(-1), TOP_K, n_tiles * MOE_TR)
    ys = _experts_call(l, xs, sched, n_tiles, w_gu, w_dn)
    n_part = geom.n_tok // MOE_OUT_PARTS
    y_ctx, y_lat = None, None
    for part in range(MOE_OUT_PARTS):
        pos_p = pos[:, part * n_part:(part + 1) * n_part].reshape(-1)
        yt = _sc_gather_rows(ys, pos_p).reshape(TOP_K, n_part, D_MODEL // 2)
        outs = _moe_out_call(geom, l, part, yt, gates, h2, sgu_bf, sdn_bf, x1, mod6, y_lat)
        if part == 0:
            y_ctx, y_lat = outs
        else:
            (y_lat,) = outs
    return y_ctx, y_lat


def _rope_tables(dec_seq):
    rows = dec_seq // GRID_W
    row = jnp.repeat(jnp.arange(rows, dtype=F32), GRID_W)
    col = jnp.tile(jnp.arange(GRID_W, dtype=F32), rows)
    inv = ROPE_BASE ** (-jnp.arange(ROPE_PAIRS, dtype=F32) / ROPE_PAIRS)
    ar = row[:, None] * inv[None, :]
    ac = col[:, None] * inv[None, :]
    cos64 = jnp.concatenate([jnp.cos(ar), jnp.cos(ar), jnp.cos(ac), jnp.cos(ac)], axis=1)
    sin64 = jnp.concatenate([-jnp.sin(ar), jnp.sin(ar), -jnp.sin(ac), jnp.sin(ac)], axis=1)
    return jnp.tile(cos64, (1, 2)), jnp.tile(sin64, (1, 2))


def _block_diag_gate(wg_dir):
    eye = jnp.eye(LRU_BLOCKS, dtype=F32)
    dense = jnp.einsum('gnij,nm->gnimj', wg_dir.astype(F32), eye).reshape(2, D_RNN, D_RNN)
    return jnp.concatenate([dense[0], dense[1]], axis=1)


def kernel(x_prompt, x_sample, cache_k, cache_v, state_lru, state_ret, c, c_ctx, ada_w, ada_b, norm1_w, norm2_w, w_in, conv_w, conv_b, lru_gate_w, lru_gate_b, lru_lambda, q_norm_w, k_norm_w, diff_lambda, subln_w, ret_decay, w_branch, w_out, router_w, router_bias, w_exp_gu, w_exp_down, w_sh_gu, w_sh_down):
    batch, seq, _ = x_prompt.shape
    dec_batch, dec_seq, _ = x_sample.shape
    assert 1 + dec_batch <= MOD_ROWS
    geom = _Geom(batch, seq, dec_batch, dec_seq)
    hs = RET_HEADS * RET_QK
    aw = DA_HEADS * 2 * DA_QK

    x_ctx = x_prompt.reshape(geom.n_ctx, D_MODEL)
    x_lat = x_sample.reshape(geom.n_lat, D_MODEL)
    cvec = jnp.zeros((MOD_ROWS, D_MODEL), F32).at[0].set(c_ctx).at[1:1 + dec_batch].set(c)
    mod6 = _ada_call(cvec, ada_w, ada_b).reshape(DEPTH, MOD_ROWS, 6, 1, D_MODEL)

    ones_bd = jnp.kron(jnp.eye(aw // DA_QK, dtype=F32), jnp.ones((DA_QK, DA_QK), F32)).astype(BF16)
    cos_t, sin_t = _rope_tables(dec_seq)

    w_in_bf = w_in.astype(BF16)
    sgu_bf, sdn_bf = w_sh_gu.astype(BF16), w_sh_down.astype(BF16)

    new_cache, lrus, rets = None, [], []
    for l in range(DEPTH):
        lam_init = 0.8 - 0.6 * math.exp(-0.3 * l)
        w_rkt_bf = w_in[l][:, C_RK:C_RK + hs].T.astype(BF16)
        proj, rkt = _inproj_call(geom, l, x_ctx, x_lat, mod6, norm1_w[l], w_in_bf, w_rkt_bf)

        sp = jax.nn.softplus(-lru_lambda[l].astype(F32))
        h0 = jnp.concatenate([jnp.zeros((batch, 2, D_RNN), F32), state_lru[:, l].astype(F32)], axis=0)
        h0 = h0.reshape(geom.n_seq, 2, 1, D_RNN)
        cb = conv_b[l].reshape(1, D_RNN)
        lru_args = []
        for d in range(2):
            lru_args.append((_block_diag_gate(lru_gate_w[l, d]).astype(BF16),
                             lru_gate_b[l, d].reshape(1, 2 * D_RNN), sp[d].reshape(1, D_RNN)))
        hf, hf_last = _lru_call(geom, False, proj, conv_w[l], cb, *lru_args[0], h0)
        dsum, qdf, qdb, kd_f, kd_b, cd_f, cd_b = _ret_tables(ret_decay[l])
        s0 = state_ret.reshape(dec_batch, DEPTH, 2, hs, RET_V)
        branch_a, hb_last, sb_start, sb_end = _lru_call(geom, True, proj, conv_w[l], cb, *lru_args[1], h0, hf,
                                                        ret=(l, rkt, kd_b, cd_b, s0))

        qw = jnp.tile(q_norm_w[l], aw // DA_QK).reshape(1, aw)
        kw = jnp.tile(k_norm_w[l], aw // DA_QK).reshape(1, aw)
        q_c, k_c, *new_cache = _prep_call(geom, False, proj, qw, kw, ones_bd, layer=l, prev_cache=new_cache)
        q_l, k_l = _prep_call(geom, True, proj, qw, kw, ones_bd, cos_t, sin_t)
        lam_p = diff_lambda[l].astype(F32)
        lam = jnp.exp(jnp.sum(lam_p[0] * lam_p[1])) - jnp.exp(jnp.sum(lam_p[2] * lam_p[3])) + lam_init
        q_bound = DA_QK * jnp.max(jnp.square(q_norm_w[l].astype(F32))) * (DA_QK ** -0.5 * LOG2E) ** 2
        k_bound = DA_QK * jnp.max(jnp.square(k_norm_w[l].astype(F32)))
        kc32 = cache_k[:, l].astype(F32)
        kc_bound = jnp.maximum(k_bound, jnp.max(jnp.sum(jnp.square(kc32), axis=-1)))

        def attn_par(kb):
            ok = (q_bound * kb * 1.05 < ATT_SAFE_LOGIT ** 2).astype(F32)
            return jnp.stack([lam, ok])

        assert geom.n_ctx % dec_seq == 0
        cache = (kc32.reshape(dec_batch, -1, aw).astype(BF16),
                 cache_v[:, l].reshape(dec_batch, -1, DA_HEADS * DA_V).astype(BF16))
        att_c = _attn_call(attn_par(k_bound), lam_init, q_c, k_c, proj, 0, batch, seq, seq, 256, subln_w[l])
        att_l = _attn_call(attn_par(kc_bound), lam_init, q_l, k_l, proj, geom.n_ctx // dec_seq, dec_batch,
                           dec_seq, dec_seq, min(4 * ATT_TQ, dec_seq), subln_w[l], cache)

        branch_c, sf_end = _ret_main_call(geom, l, proj, rkt, dsum, qdf, qdb, kd_f, cd_f, s0, sb_start)

        r_t = router_w[l].T.astype(F32)
        r_hi = r_t.astype(BF16)
        r_lo = (r_t - r_hi.astype(F32)).astype(BF16)
        x1, h2, h2p, logits_t = _merge_call(geom, l, branch_a, att_c, att_l, branch_c, proj, x_ctx, x_lat, mod6,
                                            norm2_w[l], w_branch[l].astype(BF16), w_out[l].astype(BF16), r_hi, r_lo)
        gates, slots, counts = _router_call(geom, logits_t, router_bias[l].astype(F32))
        x_ctx, x_lat = _moe(geom, l, h2, h2p, gates, slots, counts, w_exp_gu, w_exp_down, sgu_bf, sdn_bf, x1, mod6)

        lrus.append(jnp.stack([hf_last[:batch, 0], hb_last[:batch, 0]], axis=1))
        rets.append(jnp.stack([sf_end[:batch].reshape(batch, RET_HEADS, RET_QK, RET_V),
                               sb_end[:batch].reshape(batch, RET_HEADS, RET_QK, RET_V)], axis=1))

    y_prompt = x_ctx.reshape(batch, seq, D_MODEL)
    y_sample = x_lat.reshape(dec_batch, dec_seq, D_MODEL)
    new_k = new_cache[0].reshape(batch, DEPTH, seq, DA_HEADS, 2, DA_QK)
    new_v = new_cache[1].reshape(batch, DEPTH, seq, DA_HEADS, DA_V)
    return (y_prompt, y_sample, new_k, new_v, jnp.stack(lrus, axis=1), jnp.stack(rets, axis=1))
```
